```python
import jax, jax.numpy as jnp
from jax import lax
import numpy as np

D_MODEL = 1024
BATCH = 4
SEQ = 4096
DEPTH = 2

CHUNK = 64
HEAD_DIM = 64
N_MIXERS = 4
GROUP_W = D_MODEL // N_MIXERS
FOX_HEADS = GROUP_W // HEAD_DIM
GMLP_HEADS = GROUP_W // HEAD_DIM
GDN_HEADS = GROUP_W // HEAD_DIM
POOL_GROUPS = 4
POOL_GW = GROUP_W // POOL_GROUPS
POOL_WINDOWS = (2, 4, 8, 16)
FOX_QBLOCK = 128
GMLP_LEN = 128
CONV_K = 4
N_GROUPS = 4
EXPERTS_PER_GROUP = 8
N_EXPERTS = N_GROUPS * EXPERTS_PER_GROUP
TOPK_IN = 2
D_EXPERT = 512
MOE_BLOCK = 128
EPS = 1e-6
PROJ_SIZES = (GROUP_W, GROUP_W, GROUP_W, FOX_HEADS,
              GROUP_W, GROUP_W,
              GROUP_W, GROUP_W, GROUP_W, GDN_HEADS, GDN_HEADS, GROUP_W,
              GROUP_W)
D_PROJ = sum(PROJ_SIZES)

kernel_name = 'hybrid_chunk_causal_encoder'


def rms_norm(x, g):
    xf = x.astype(jnp.float32)
    y = xf * lax.rsqrt(jnp.mean(xf * xf, axis=-1, keepdims=True) + EPS)
    return (y * g).astype(x.dtype)


def layer_norm(x, g, b):
    xf = x.astype(jnp.float32)
    mu = jnp.mean(xf, axis=-1, keepdims=True)
    var = jnp.mean(jnp.square(xf - mu), axis=-1, keepdims=True)
    return ((xf - mu) * lax.rsqrt(var + EPS) * g + b).astype(x.dtype)


def fox_attention(q, k, v, f_logit, f_bias, qn_g, kn_g):
    B, S, _ = q.shape
    H, dh = FOX_HEADS, HEAD_DIM
    heads = lambda t: t.reshape(B, S, H, dh).transpose(0, 2, 1, 3)
    q = rms_norm(heads(q), qn_g)
    k = rms_norm(heads(k), kn_g)
    v = heads(v)
    log_f = jax.nn.log_sigmoid(f_logit.astype(jnp.float32) + f_bias.astype(jnp.float32))
    c = jnp.cumsum(log_f, axis=1).transpose(0, 2, 1)
    nb = S // FOX_QBLOCK
    q_blocks = q.reshape(B, H, nb, FOX_QBLOCK, dh).transpose(2, 0, 1, 3, 4)
    c_blocks = c.reshape(B, H, nb, FOX_QBLOCK).transpose(2, 0, 1, 3)
    starts = jnp.arange(nb) * FOX_QBLOCK
    k_pos = jnp.arange(S)
    scale = HEAD_DIM ** -0.5

    def one_block(args):
        q_i, c_i, s0 = args
        logits = jnp.einsum('bhqd,bhkd->bhqk', q_i, k, preferred_element_type=jnp.float32) * scale
        logits = logits + c_i[..., :, None] - c[..., None, :]
        q_pos = s0 + jnp.arange(FOX_QBLOCK)
        logits = jnp.where(k_pos[None, :] <= q_pos[:, None], logits, -jnp.inf)
        p = jax.nn.softmax(logits, axis=-1)
        return jnp.einsum('bhqk,bhkd->bhqd', p.astype(v.dtype), v)

    o = lax.map(one_block, (q_blocks, c_blocks, starts))
    return o.transpose(1, 0, 3, 2, 4).reshape(B, S, H * dh)


def gmlp_spatial_gate(u, v, ln_g, ln_b, ws, bs):
    B, S, _ = u.shape
    H, dh, L = GMLP_HEADS, HEAD_DIM, GMLP_LEN
    u = jax.nn.gelu(u, approximate=False)
    v = layer_norm(jax.nn.gelu(v, approximate=False), ln_g, ln_b)
    pos_chunk = jnp.arange(L) // CHUNK
    mask = pos_chunk[:, None] >= pos_chunk[None, :]
    w = jnp.where(mask[None], ws, 0)
    vr = v.reshape(B, S // L, L, H, dh)
    mixed = jnp.einsum('hij,bnjhd->bnihd', w, vr) + bs.T[None, None, :, :, None]
    return u * mixed.reshape(B, S, H * dh)


def short_conv(x, w):
    C = x.shape[-1]
    y = lax.conv_general_dilated(x, w[:, None, :], window_strides=(1,), padding=[(CONV_K - 1, 0)],
                                 dimension_numbers=('NWC', 'WIO', 'NWC'), feature_group_count=C)
    return jax.nn.silu(y)


def gated_delta_rule(q, k, v, g, beta):
    B, H, S, dk = q.shape
    dv = v.shape[-1]
    N, C = S // CHUNK, CHUNK
    to_chunks = lambda t: t.reshape(B, H, N, C, *t.shape[3:])
    q = to_chunks(q * dk ** -0.5)
    k = to_chunks(k)
    v = to_chunks(v)
    g = jnp.cumsum(to_chunks(g), axis=-1)
    beta = to_chunks(beta)
    tri = jnp.tril(jnp.ones((C, C), bool))
    tri_strict = jnp.tril(jnp.ones((C, C), bool), -1)
    decay = jnp.exp(jnp.where(tri, g[..., :, None] - g[..., None, :], -jnp.inf))
    k_beta = k * beta[..., None]
    lower = jnp.where(tri_strict, jnp.einsum('bhncd,bhnsd->bhncs', k_beta, k) * decay, 0.0)
    eye = jnp.eye(C, dtype=q.dtype)
    t_inv = lax.linalg.triangular_solve(eye + lower, jnp.broadcast_to(eye, lower.shape),
                                        left_side=True, lower=True)
    u = t_inv @ (v * beta[..., None])
    w = t_inv @ (k_beta * jnp.exp(g)[..., None])
    intra = jnp.where(tri, jnp.einsum('bhncd,bhnsd->bhncs', q, k) * decay, 0.0)

    def step(state, xs):
        q_c, k_c, u_c, w_c, g_c, a_c = xs
        v_new = u_c - w_c @ state
        o_c = (q_c * jnp.exp(g_c)[..., None]) @ state + a_c @ v_new
        g_last = g_c[..., -1]
        state = state * jnp.exp(g_last)[..., None, None] + jnp.einsum(
            'bhck,bhcv->bhkv', k_c * jnp.exp(g_last[..., None] - g_c)[..., None], v_new)
        return state, o_c

    xs = tuple(jnp.moveaxis(t, 2, 0) for t in (q, k, u, w, g, intra))
    state0 = jnp.zeros((B, H, dk, dv), q.dtype)
    _, o = lax.scan(step, state0, xs)
    return jnp.moveaxis(o, 0, 2).reshape(B, H, S, dv)


def gated_deltanet(q, k, v, a, b, gate, conv_w, a_log, dt_bias, norm_g):
    B, S, _ = q.shape
    H, dh = GDN_HEADS, HEAD_DIM
    qkv = short_conv(jnp.concatenate([q, k, v], axis=-1), conv_w).astype(jnp.float32)
    q, k, v = jnp.split(qkv, 3, axis=-1)
    heads = lambda t: t.reshape(B, S, H, dh).transpose(0, 2, 1, 3)
    l2 = lambda t: t * lax.rsqrt(jnp.sum(t * t, axis=-1, keepdims=True) + EPS)
    q, k, v = l2(heads(q)), l2(heads(k)), heads(v)
    g = -jnp.exp(a_log.astype(jnp.float32)) * jax.nn.softplus(a.astype(jnp.float32) + dt_bias.astype(jnp.float32))
    beta = jax.nn.sigmoid(b.astype(jnp.float32))
    o = gated_delta_rule(q, k, v, g.transpose(0, 2, 1), beta.transpose(0, 2, 1))
    o = o.transpose(0, 2, 1, 3)
    o = rms_norm(o, norm_g) * jax.nn.silu(gate.astype(jnp.float32).reshape(B, S, H, dh))
    return o.reshape(B, S, H * dh).astype(gate.dtype)


def multiscale_pool(z, w, scale):
    B, S, _ = z.shape
    zf = z.astype(jnp.float32).reshape(B, S, POOL_GROUPS, POOL_GW)
    cs = jnp.concatenate([jnp.zeros_like(zf[:, :1]), jnp.cumsum(zf, axis=1)], axis=1)
    t = jnp.arange(S)
    pooled = []
    for gi, win in enumerate(POOL_WINDOWS):
        lo = jnp.maximum(t + 1 - win, 0)
        total = cs[:, 1:, gi] - cs[:, lo, gi]
        pooled.append(total / jnp.minimum(t + 1, win).astype(jnp.float32)[None, :, None])
    pooled = jnp.stack(pooled, axis=2)
    y = jnp.einsum('bsgc,gcd->bsgd', pooled - zf, w.astype(jnp.float32))
    return (y.reshape(B, S, GROUP_W) * scale).astype(z.dtype)


def hybrid_mixer(xn, w_in, w_out, fox_f_bias, fox_qn_g, fox_kn_g, fox_out_g,
                 gmlp_ln_g, gmlp_ln_b, gmlp_ws, gmlp_bs, gmlp_out_g,
                 gdn_conv_w, gdn_a_log, gdn_dt_bias, gdn_norm_g,
                 pool_w, pool_scale, pool_out_g):
    proj = xn @ w_in
    points = np.cumsum(PROJ_SIZES)[:-1].tolist()
    (fq, fk, fv, ff, gu, gv, dq, dk, dv, da, db, dg, pz) = jnp.split(proj, points, axis=-1)
    y_a = rms_norm(fox_attention(fq, fk, fv, ff, fox_f_bias, fox_qn_g, fox_kn_g), fox_out_g)
    y_b = rms_norm(gmlp_spatial_gate(gu, gv, gmlp_ln_g, gmlp_ln_b, gmlp_ws, gmlp_bs), gmlp_out_g)
    y_c = gated_deltanet(dq, dk, dv, da, db, dg, gdn_conv_w, gdn_a_log, gdn_dt_bias, gdn_norm_g)
    y_d = rms_norm(multiscale_pool(pz, pool_w, pool_scale), pool_out_g)
    return jnp.concatenate([y_a, y_b, y_c, y_d], axis=-1) @ w_out


def hier_moe(xf, rg_w, rg_b, re_w, re_b, w1, w3, w2):
    T, D = xf.shape
    g_prob = jax.nn.softmax((xf @ rg_w).astype(jnp.float32) + rg_b.astype(jnp.float32), axis=-1)
    g_top, g_idx = lax.top_k(g_prob, 1)
    e_logits = ((xf @ re_w).astype(jnp.float32) + re_b.astype(jnp.float32)).reshape(T, N_GROUPS, EXPERTS_PER_GROUP)
    e_in = jnp.take_along_axis(e_logits, g_idx[:, :, None], axis=1)[:, 0]
    e_top, e_idx = lax.top_k(jax.nn.softmax(e_in, axis=-1), TOPK_IN)
    gate = g_top * e_top / jnp.sum(e_top, axis=-1, keepdims=True)
    expert = g_idx * EXPERTS_PER_GROUP + e_idx
    A = T * TOPK_IN
    flat_e = expert.reshape(A)
    order = jnp.argsort(flat_e)
    sorted_e = flat_e[order]
    tok = order // TOPK_IN
    counts = jnp.bincount(flat_e, length=N_EXPERTS)
    padded = (counts + MOE_BLOCK - 1) // MOE_BLOCK * MOE_BLOCK
    pad_end = jnp.cumsum(padded)
    pad_start = pad_end - padded
    start = jnp.cumsum(counts) - counts
    dest = pad_start[sorted_e] + jnp.arange(A) - start[sorted_e]
    n_blk = -(-A // MOE_BLOCK) + N_EXPERTS
    x_pad = jnp.zeros((n_blk * MOE_BLOCK, D), xf.dtype).at[dest].set(xf[tok])
    blk_e = jnp.minimum(jnp.searchsorted(pad_end, jnp.arange(n_blk) * MOE_BLOCK, side='right'), N_EXPERTS - 1)

    def run_block(args):
        xb, e = args
        h = jax.nn.silu(xb @ w1[e]) * (xb @ w3[e])
        return h @ w2[e]

    y_pad = lax.map(run_block, (x_pad.reshape(n_blk, MOE_BLOCK, D), blk_e)).reshape(-1, D)
    contrib = y_pad[dest] * gate.reshape(A)[order][:, None].astype(y_pad.dtype)
    return jnp.zeros_like(xf).at[tok].add(contrib)


def setup_inputs(seed: int = 0) -> dict:
    key = jax.random.key(seed)
    ks = iter(jax.random.split(key, 40))
    L = DEPTH
    nrm = lambda shape, scale: scale * jax.random.normal(next(ks), shape, jnp.float32)
    gain = lambda *shape: 1.0 + nrm((L,) + shape, 0.05)
    x = nrm((BATCH, SEQ, D_MODEL), 1.0)
    attn_norm_g = gain(D_MODEL)
    w_in = nrm((L, D_MODEL, D_PROJ), D_MODEL ** -0.5)
    w_out = nrm((L, D_MODEL, D_MODEL), D_MODEL ** -0.5)
    fox_f_bias = 2.0 + nrm((L, FOX_HEADS), 0.5)
    fox_qn_g = gain(HEAD_DIM)
    fox_kn_g = gain(HEAD_DIM)
    fox_out_g = gain(GROUP_W)
    gmlp_ln_g = gain(GROUP_W)
    gmlp_ln_b = nrm((L, GROUP_W), 0.02)
    gmlp_ws = nrm((L, GMLP_HEADS, GMLP_LEN, GMLP_LEN), GMLP_LEN ** -0.5)
    gmlp_bs = gain(GMLP_HEADS, GMLP_LEN)
    gmlp_out_g = gain(GROUP_W)
    gdn_conv_w = nrm((L, CONV_K, 3 * GROUP_W), CONV_K ** -0.5)
    gdn_a_log = jnp.log(jax.random.uniform(next(ks), (L, GDN_HEADS), jnp.float32, 1.0, 16.0))
    dt = jnp.exp(jax.random.uniform(next(ks), (L, GDN_HEADS), jnp.float32, np.log(1e-3), np.log(1e-1)))
    gdn_dt_bias = dt + jnp.log(-jnp.expm1(-dt))
    gdn_norm_g = gain(HEAD_DIM)
    pool_w = nrm((L, POOL_GROUPS, POOL_GW, POOL_GW), POOL_GW ** -0.5)
    pool_scale = gain(GROUP_W)
    pool_out_g = gain(GROUP_W)
    ffn_norm_g = gain(D_MODEL)
    router_g_w = nrm((L, D_MODEL, N_GROUPS), D_MODEL ** -0.5)
    router_g_b = nrm((L, N_GROUPS), 0.01)
    router_e_w = nrm((L, D_MODEL, N_EXPERTS), D_MODEL ** -0.5)
    router_e_b = nrm((L, N_EXPERTS), 0.01)
    moe_w1 = nrm((L, N_EXPERTS, D_MODEL, D_EXPERT), D_MODEL ** -0.5)
    moe_w3 = nrm((L, N_EXPERTS, D_MODEL, D_EXPERT), D_MODEL ** -0.5)
    moe_w2 = nrm((L, N_EXPERTS, D_EXPERT, D_MODEL), D_EXPERT ** -0.5)
    return {'x': x, 'attn_norm_g': attn_norm_g, 'w_in': w_in, 'w_out': w_out,
            'fox_f_bias': fox_f_bias, 'fox_qn_g': fox_qn_g, 'fox_kn_g': fox_kn_g, 'fox_out_g': fox_out_g,
            'gmlp_ln_g': gmlp_ln_g, 'gmlp_ln_b': gmlp_ln_b, 'gmlp_ws': gmlp_ws, 'gmlp_bs': gmlp_bs,
            'gmlp_out_g': gmlp_out_g, 'gdn_conv_w': gdn_conv_w, 'gdn_a_log': gdn_a_log,
            'gdn_dt_bias': gdn_dt_bias, 'gdn_norm_g': gdn_norm_g, 'pool_w': pool_w,
            'pool_scale': pool_scale, 'pool_out_g': pool_out_g, 'ffn_norm_g': ffn_norm_g,
            'router_g_w': router_g_w, 'router_g_b': router_g_b, 'router_e_w': router_e_w,
            'router_e_b': router_e_b, 'moe_w1': moe_w1, 'moe_w3': moe_w3, 'moe_w2': moe_w2}


def reference(x, attn_norm_g, w_in, w_out, fox_f_bias, fox_qn_g, fox_kn_g, fox_out_g,
              gmlp_ln_g, gmlp_ln_b, gmlp_ws, gmlp_bs, gmlp_out_g, gdn_conv_w, gdn_a_log,
              gdn_dt_bias, gdn_norm_g, pool_w, pool_scale, pool_out_g, ffn_norm_g,
              router_g_w, router_g_b, router_e_w, router_e_b, moe_w1, moe_w3, moe_w2):
    h = x
    B, S, D = x.shape
    for l in range(DEPTH):
        xn = rms_norm(h, attn_norm_g[l])
        h = h + hybrid_mixer(xn, w_in[l], w_out[l], fox_f_bias[l], fox_qn_g[l], fox_kn_g[l], fox_out_g[l],
                             gmlp_ln_g[l], gmlp_ln_b[l], gmlp_ws[l], gmlp_bs[l], gmlp_out_g[l],
                             gdn_conv_w[l], gdn_a_log[l], gdn_dt_bias[l], gdn_norm_g[l],
                             pool_w[l], pool_scale[l], pool_out_g[l])
        hn = rms_norm(h, ffn_norm_g[l]).reshape(B * S, D)
        h = h + hier_moe(hn, router_g_w[l], router_g_b[l], router_e_w[l], router_e_b[l],
                         moe_w1[l], moe_w3[l], moe_w2[l]).reshape(B, S, D)
    return h
```

```python
import functools

import jax
import jax.numpy as jnp
import numpy as np
from jax import lax
from jax.experimental import pallas as pl
from jax.experimental.pallas import tpu as pltpu

F32 = jnp.float32
BF16 = jnp.bfloat16
I32 = jnp.int32

EPS = 1e-6
HEAD_DIM = 64
GROUP_W = 256
N_HEADS = GROUP_W // HEAD_DIM
CHUNK = 64
GMLP_LEN = 128
CONV_K = 4
POOL_WINDOWS = (2, 4, 8, 16)
N_GROUPS = 4
EXPERTS_PER_GROUP = 8
N_EXPERTS = N_GROUPS * EXPERTS_PER_GROUP
TOPK_IN = 2
MOE_BLOCK = 128
LANES = 128
SUBLANES = 8
VMEM_LIMIT = 56 * 1024 * 1024

COL_FQ, COL_FK, COL_FV, COL_GU, COL_GV, COL_DQ, COL_DK, COL_DV, COL_DG, COL_PZ = range(10)
N_BIG_COLS = 10
LANE_FOX, LANE_DECAY, LANE_BETA = 0, 4, 8


def _params(*sem):
    return pltpu.CompilerParams(dimension_semantics=sem, vmem_limit_bytes=VMEM_LIMIT)


def _head_ones():
    r = lax.broadcasted_iota(I32, (GROUP_W, GROUP_W), 0) // HEAD_DIM
    c = lax.broadcasted_iota(I32, (GROUP_W, GROUP_W), 1) // HEAD_DIM
    return (r == c).astype(BF16)


def _head_sums(x, ones_bd):
    hi = x.astype(BF16)
    lo = (x - hi.astype(F32)).astype(BF16)
    return (jnp.dot(hi, ones_bd, preferred_element_type=F32)
            + jnp.dot(lo, ones_bd, preferred_element_type=F32))


def _rms(x, g):
    return x * lax.rsqrt(jnp.mean(x * x, axis=-1, keepdims=True) + EPS) * g


def _mm(a, b):
    return jnp.dot(a.astype(BF16), b.astype(BF16), preferred_element_type=F32)


def _mm_nt(a, b):
    return lax.dot_general(a.astype(BF16), b.astype(BF16), (((1,), (1,)), ((), ())),
                           preferred_element_type=F32)


def _mm_tn(a, b):
    return lax.dot_general(a.astype(BF16), b.astype(BF16), (((0,), (0,)), ((), ())),
                           preferred_element_type=F32)


def _split(a):
    hi = a.astype(BF16)
    return hi, (a - hi.astype(F32)).astype(BF16)


def _mm3(a, b):
    ah, al = _split(a)
    bh, bl = _split(b)
    d = functools.partial(jnp.dot, preferred_element_type=F32)
    return d(ah, bh) + (d(ah, bl) + d(al, bh))


def _inproj_body(x_ref, g_ref, wb_ref, ws_ref, big_ref, small_ref):
    xn = _rms(x_ref[...], g_ref[...]).astype(BF16)
    big_ref[...] = jnp.dot(xn, wb_ref[...], preferred_element_type=F32)
    small_ref[...] = jnp.dot(xn, ws_ref[...], preferred_element_type=F32)


def _inproj(x2d, g, wb, ws, tm):
    T, D = x2d.shape
    nb = wb.shape[1]
    return pl.pallas_call(
        _inproj_body, grid=(T // tm,),
        in_specs=[pl.BlockSpec((tm, D), lambda i: (i, 0)),
                  pl.BlockSpec((1, D), lambda i: (0, 0)),
                  pl.BlockSpec((D, nb), lambda i: (0, 0)),
                  pl.BlockSpec((D, LANES), lambda i: (0, 0))],
        out_specs=[pl.BlockSpec((tm, nb), lambda i: (i, 0)),
                   pl.BlockSpec((tm, LANES), lambda i: (i, 0))],
        out_shape=[jax.ShapeDtypeStruct((T, nb), F32), jax.ShapeDtypeStruct((T, LANES), F32)],
        compiler_params=_params("parallel"), name="inproj")(x2d, g, wb, ws)


def _gates_body(sm_ref, p_ref, col_ref, row_ref, carry_ref, *, ts):
    @pl.when(pl.program_id(1) == 0)
    def _():
        carry_ref[...] = jnp.zeros_like(carry_ref)

    x = sm_ref[...]
    lane = lax.broadcasted_iota(I32, (ts, LANES), 1)
    is_fox = lane < LANE_DECAY
    is_dec = (lane >= LANE_DECAY) & (lane < LANE_BETA)
    is_beta = (lane >= LANE_BETA) & (lane < LANE_BETA + N_HEADS)
    logf = jax.nn.log_sigmoid(x + p_ref[0:1, :])
    g = -jnp.exp(p_ref[2:3, :]) * jax.nn.softplus(x + p_ref[1:2, :])
    beta = jax.nn.sigmoid(x)
    r = lax.broadcasted_iota(I32, (ts, ts), 0)
    c = lax.broadcasted_iota(I32, (ts, ts), 1)
    tri_full = (r >= c).astype(F32)
    tri_chunk = ((r >= c) & (r // CHUNK == c // CHUNK)).astype(F32)
    hp = functools.partial(jnp.dot, preferred_element_type=F32, precision=lax.Precision.HIGHEST)
    cf = hp(tri_full, jnp.where(is_fox, logf, 0.0)) + carry_ref[...]
    cg = hp(tri_chunk, jnp.where(is_dec, g, 0.0))
    carry_ref[...] = cf[ts - 1:ts, :]
    out = jnp.where(is_fox, cf, jnp.where(is_dec, cg, jnp.where(is_beta, beta, 0.0)))
    col_ref[...] = out
    row_ref[...] = out.T[:2 * SUBLANES, :]


def _gates(small3, prm, ts):
    B, S, _ = small3.shape
    return pl.pallas_call(
        functools.partial(_gates_body, ts=ts), grid=(B, S // ts),
        in_specs=[pl.BlockSpec((None, ts, LANES), lambda b, j: (b, j, 0)),
                  pl.BlockSpec((SUBLANES, LANES), lambda b, j: (0, 0))],
        out_specs=[pl.BlockSpec((None, ts, LANES), lambda b, j: (b, j, 0)),
                   pl.BlockSpec((None, 2 * SUBLANES, ts), lambda b, j: (b, 0, j))],
        out_shape=[jax.ShapeDtypeStruct((B, S, LANES), F32),
                   jax.ShapeDtypeStruct((B, 2 * SUBLANES, S), F32)],
        scratch_shapes=[pltpu.VMEM((1, LANES), F32)],
        compiler_params=_params("parallel", "arbitrary"), name="gates")(small3, prm)


def _fox_body(q_ref, k_ref, v_ref, ccol_ref, crow_ref, qg_ref, kg_ref, og_ref, o_ref,
              kn_scr, vb_scr, *, tq, nk):
    i = pl.program_id(1)
    ones_bd = _head_ones()

    def head_norm(x, g):
        ss = _head_sums(x * x, ones_bd)
        return x * lax.rsqrt(ss * (1.0 / HEAD_DIM) + EPS) * g

    @pl.when(i == 0)
    def _():
        for c in range(nk):
            kc = head_norm(k_ref[c * tq:(c + 1) * tq, :], kg_ref[...]).astype(BF16)
            vc = v_ref[c * tq:(c + 1) * tq, :].astype(BF16)
            for h in range(N_HEADS):
                kn_scr[h, c] = kc[:, h * HEAD_DIM:(h + 1) * HEAD_DIM]
                vb_scr[h, c] = vc[:, h * HEAD_DIM:(h + 1) * HEAD_DIM]

    q = (head_norm(q_ref[...], qg_ref[...]) * (HEAD_DIM ** -0.5)).astype(BF16)
    ccol = ccol_ref[...]
    rows = lax.broadcasted_iota(I32, (tq, tq), 0)
    cols = lax.broadcasted_iota(I32, (tq, tq), 1)
    causal = cols <= rows
    outs = []
    for h in range(N_HEADS):
        qh = q[:, h * HEAD_DIM:(h + 1) * HEAD_DIM]
        cq = ccol[:, LANE_FOX + h:LANE_FOX + h + 1]

        def step(j, carry, masked, h=h, qh=qh, cq=cq):
            m, l, acc = carry
            s = lax.dot_general(qh, kn_scr[h, j], (((1,), (1,)), ((), ())),
                                preferred_element_type=F32)
            s = s + (cq - crow_ref[j, LANE_FOX + h:LANE_FOX + h + 1, :])
            if masked:
                s = jnp.where(causal, s, -jnp.inf)
            m_new = jnp.maximum(m, jnp.max(s, axis=-1, keepdims=True))
            alpha = jnp.exp(m - m_new)
            p = jnp.exp(s - m_new)
            l = alpha * l + jnp.sum(p, axis=-1, keepdims=True)
            acc = alpha * acc + jnp.dot(p.astype(BF16), vb_scr[h, j], preferred_element_type=F32)
            return m_new, l, acc

        init = (jnp.full((tq, 1), -jnp.inf, F32), jnp.zeros((tq, 1), F32),
                jnp.zeros((tq, HEAD_DIM), F32))
        carry = lax.fori_loop(0, i, functools.partial(step, masked=False), init)
        _, l, acc = step(i, carry, True)
        outs.append(acc / l)
    o_ref[...] = _rms(jnp.concatenate(outs, axis=-1), og_ref[...])


def _fox(big3, gcol, crow4, qg, kg, og, tq):
    B, S, _ = big3.shape
    nk = S // tq
    row = lambda a: pl.BlockSpec((1, GROUP_W), lambda b, i: (0, 0))
    return pl.pallas_call(
        functools.partial(_fox_body, tq=tq, nk=nk), grid=(B, nk),
        in_specs=[pl.BlockSpec((None, tq, GROUP_W), lambda b, i: (b, i, COL_FQ)),
                  pl.BlockSpec((None, S, GROUP_W), lambda b, i: (b, 0, COL_FK)),
                  pl.BlockSpec((None, S, GROUP_W), lambda b, i: (b, 0, COL_FV)),
                  pl.BlockSpec((None, tq, LANES), lambda b, i: (b, i, 0)),
                  pl.BlockSpec((None, nk, 2 * SUBLANES, tq), lambda b, i: (b, 0, 0, 0)),
                  row(qg), row(kg), row(og)],
        out_specs=pl.BlockSpec((None, tq, GROUP_W), lambda b, i: (b, i, 0)),
        out_shape=jax.ShapeDtypeStruct((B, S, GROUP_W), F32),
        scratch_shapes=[pltpu.VMEM((N_HEADS, nk, tq, HEAD_DIM), BF16),
                        pltpu.VMEM((N_HEADS, nk, tq, HEAD_DIM), BF16)],
        compiler_params=_params("parallel", "arbitrary"), name="fox")(
            big3, big3, big3, gcol, crow4, qg, kg, og)


def _gelu(x):
    return 0.5 * x * (1.0 + lax.erf(x * (2.0 ** -0.5)))


def _gmlp_body(u_ref, v_ref, lg_ref, lb_ref, ws_ref, bst_ref, og_ref, o_ref, *, nwin):
    L = GMLP_LEN
    r = lax.broadcasted_iota(I32, (L, L), 0) // CHUNK
    c = lax.broadcasted_iota(I32, (L, L), 1) // CHUNK
    mask = r >= c
    ws = [jnp.where(mask, ws_ref[h], 0.0).astype(BF16) for h in range(N_HEADS)]
    for n in range(nwin):
        u = _gelu(u_ref[n * L:(n + 1) * L, :])
        v = _gelu(v_ref[n * L:(n + 1) * L, :])
        mu = jnp.mean(v, axis=-1, keepdims=True)
        vc = v - mu
        var = jnp.mean(vc * vc, axis=-1, keepdims=True)
        vn = (vc * lax.rsqrt(var + EPS) * lg_ref[...] + lb_ref[...]).astype(BF16)
        mixed = jnp.concatenate(
            [jnp.dot(ws[h], vn[:, h * HEAD_DIM:(h + 1) * HEAD_DIM], preferred_element_type=F32)
             for h in range(N_HEADS)], axis=-1) + bst_ref[...]
        o_ref[n * L:(n + 1) * L, :] = _rms(u * mixed, og_ref[...])


def _gmlp(big, lg, lb, ws, bst, og, tm):
    T = big.shape[0]
    row = pl.BlockSpec((1, GROUP_W), lambda i: (0, 0))
    return pl.pallas_call(
        functools.partial(_gmlp_body, nwin=tm // GMLP_LEN), grid=(T // tm,),
        in_specs=[pl.BlockSpec((tm, GROUP_W), lambda i: (i, COL_GU)),
                  pl.BlockSpec((tm, GROUP_W), lambda i: (i, COL_GV)),
                  row, row,
                  pl.BlockSpec((N_HEADS, GMLP_LEN, GMLP_LEN), lambda i: (0, 0, 0)),
                  pl.BlockSpec((GMLP_LEN, GROUP_W), lambda i: (0, 0)),
                  row],
        out_specs=pl.BlockSpec((tm, GROUP_W), lambda i: (i, 0)),
        out_shape=jax.ShapeDtypeStruct((T, GROUP_W), F32),
        compiler_params=_params("parallel"), name="gmlp")(big, big, lg, lb, ws, bst, og)


def _gdn_prep_body(q_ref, k_ref, v_ref, hq_ref, hk_ref, hv_ref, w_ref, qo_ref, ko_ref, vo_ref, *, ts):
    first = pl.program_id(1) == 0
    ones_bd = _head_ones()

    def conv(x_ref, halo_ref, w):
        halo = jnp.where(first, 0.0, halo_ref[...])
        xx = jnp.concatenate([halo, x_ref[...]], axis=0)
        y = w[CONV_K - 1:CONV_K, :] * xx[SUBLANES:, :]
        for j in range(CONV_K - 1):
            y = y + w[j:j + 1, :] * pltpu.roll(xx, CONV_K - 1 - j, 0)[SUBLANES:, :]
        return y * jax.nn.sigmoid(y)

    def l2(t):
        return t * lax.rsqrt(_head_sums(t * t, ones_bd) + EPS)

    qo_ref[...] = l2(conv(q_ref, hq_ref, w_ref[0])) * (HEAD_DIM ** -0.5)
    ko_ref[...] = l2(conv(k_ref, hk_ref, w_ref[1]))
    vo_ref[...] = conv(v_ref, hv_ref, w_ref[2])


def _gdn_prep(big3, conv_w3, ts):
    B, S, _ = big3.shape
    hb = ts // SUBLANES
    blk = lambda col: pl.BlockSpec((None, ts, GROUP_W), lambda b, i: (b, i, col))
    halo = lambda col: pl.BlockSpec((None, SUBLANES, GROUP_W),
                                    lambda b, i: (b, jnp.maximum(i * hb - 1, 0), col))
    out = pl.BlockSpec((None, ts, GROUP_W), lambda b, i: (b, i, 0))
    shp = jax.ShapeDtypeStruct((B, S, GROUP_W), F32)
    return pl.pallas_call(
        functools.partial(_gdn_prep_body, ts=ts), grid=(B, S // ts),
        in_specs=[blk(COL_DQ), blk(COL_DK), blk(COL_DV), halo(COL_DQ), halo(COL_DK), halo(COL_DV),
                  pl.BlockSpec((3, CONV_K, GROUP_W), lambda b, i: (0, 0, 0))],
        out_specs=[out, out, out], out_shape=[shp, shp, shp],
        compiler_params=_params("parallel", "parallel"), name="gdn_prep")(
            big3, big3, big3, big3, big3, big3, conv_w3)


def _gdn_body(q_ref, k_ref, v_ref, gate_ref, gcol_ref, grow_ref, ng_ref, o_ref, s_scr, *, nchunk):
    C = CHUNK

    @pl.when(pl.program_id(1) == 0)
    def _():
        s_scr[...] = jnp.zeros_like(s_scr)

    r = lax.broadcasted_iota(I32, (C, C), 0)
    c = lax.broadcasted_iota(I32, (C, C), 1)
    tri = r >= c
    strict = r > c
    eye = (r == c).astype(F32)

    def chunk(n, carry):
        r0 = pl.multiple_of(n * C, C)
        qc = q_ref[pl.ds(r0, C), :]
        kc = k_ref[pl.ds(r0, C), :]
        vc = v_ref[pl.ds(r0, C), :]
        gcol = gcol_ref[pl.ds(r0, C), :]
        grow = grow_ref[n]
        outs = []
        for h in range(N_HEADS):
            sl = slice(h * HEAD_DIM, (h + 1) * HEAD_DIM)
            q, k, v = qc[:, sl], kc[:, sl], vc[:, sl]
            gc = gcol[:, LANE_DECAY + h:LANE_DECAY + h + 1]
            gr = grow[LANE_DECAY + h:LANE_DECAY + h + 1, :]
            beta = gcol[:, LANE_BETA + h:LANE_BETA + h + 1]
            decay = jnp.exp(jnp.where(tri, gc - gr, -jnp.inf))
            kb = k * beta
            x = jnp.where(strict, -(_mm_nt(kb, k) * decay), 0.0)
            t_inv = eye + x
            p = x
            for _ in range(5):
                p = _mm3(p, p)
                t_inv = t_inv + _mm3(t_inv, p)
            eg = jnp.exp(gc)
            u = _mm(t_inv, v * beta)
            w = _mm(t_inv, kb * eg)
            intra = jnp.where(tri, _mm_nt(q, k) * decay, 0.0)
            state = s_scr[h]
            v_new = u - _mm(w, state)
            o = _mm(q * eg, state) + _mm(intra, v_new)
            g_last = gc[C - 1:C, :]
            s_scr[h] = state * jnp.exp(g_last) + _mm_tn(k * jnp.exp(g_last - gc), v_new)
            outs.append(_rms(o, ng_ref[...]))
        gate = gate_ref[pl.ds(r0, C), :]
        o_ref[pl.ds(r0, C), :] = jnp.concatenate(outs, axis=-1) * (gate * jax.nn.sigmoid(gate))
        return carry

    lax.fori_loop(0, nchunk, chunk, 0)


def _gdn(qn, kn, vv, big3, gcol, grow4, ng, ts):
    B, S, _ = qn.shape
    nchunk = ts // CHUNK
    blk = pl.BlockSpec((None, ts, GROUP_W), lambda b, i: (b, i, 0))
    return pl.pallas_call(
        functools.partial(_gdn_body, nchunk=nchunk), grid=(B, S // ts),
        in_specs=[blk, blk, blk,
                  pl.BlockSpec((None, ts, GROUP_W), lambda b, i: (b, i, COL_DG)),
                  pl.BlockSpec((None, ts, LANES), lambda b, i: (b, i, 0)),
                  pl.BlockSpec((None, nchunk, 2 * SUBLANES, CHUNK), lambda b, i: (b, i, 0, 0)),
                  pl.BlockSpec((1, HEAD_DIM), lambda b, i: (0, 0))],
        out_specs=blk, out_shape=jax.ShapeDtypeStruct((B, S, GROUP_W), F32),
        scratch_shapes=[pltpu.VMEM((N_HEADS, HEAD_DIM, HEAD_DIM), F32)],
        compiler_params=_params("parallel", "arbitrary"), name="gdn")(
            qn, kn, vv, big3, gcol, grow4, ng)


def _pool_body(z_ref, halo_ref, w_ref, sc_ref, og_ref, o_ref, *, ts):
    i = pl.program_id(1)
    hr = 2 * SUBLANES
    z = z_ref[...]
    halo = jnp.where(i == 0, 0.0, halo_ref[...])
    s1 = jnp.concatenate([halo, z], axis=0)
    s2 = s1 + pltpu.roll(s1, 1, 0)
    s4 = s2 + pltpu.roll(s2, 2, 0)
    s8 = s4 + pltpu.roll(s4, 4, 0)
    s16 = s8 + pltpu.roll(s8, 8, 0)
    grp = lax.broadcasted_iota(I32, (ts, GROUP_W), 1) // (GROUP_W // len(POOL_WINDOWS))
    t = lax.broadcasted_iota(I32, (ts, GROUP_W), 0) + i * ts
    total = jnp.where(grp == 0, s2[hr:], jnp.where(grp == 1, s4[hr:], jnp.where(grp == 2, s8[hr:], s16[hr:])))
    win = jnp.where(grp == 0, POOL_WINDOWS[0], jnp.where(grp == 1, POOL_WINDOWS[1],
                    jnp.where(grp == 2, POOL_WINDOWS[2], POOL_WINDOWS[3])))
    pooled = total / jnp.minimum(t + 1, win).astype(F32)
    y = _mm(pooled - z, w_ref[...]) * sc_ref[...]
    o_ref[...] = _rms(y, og_ref[...])


def _pool(big3, wbd, sc, og, ts):
    B, S, _ = big3.shape
    hr = 2 * SUBLANES
    hb = ts // hr
    row = pl.BlockSpec((1, GROUP_W), lambda b, i: (0, 0))
    return pl.pallas_call(
        functools.partial(_pool_body, ts=ts), grid=(B, S // ts),
        in_specs=[pl.BlockSpec((None, ts, GROUP_W), lambda b, i: (b, i, COL_PZ)),
                  pl.BlockSpec((None, hr, GROUP_W), lambda b, i: (b, jnp.maximum(i * hb - 1, 0), COL_PZ)),
                  pl.BlockSpec((GROUP_W, GROUP_W), lambda b, i: (0, 0)), row, row],
        out_specs=pl.BlockSpec((None, ts, GROUP_W), lambda b, i: (b, i, 0)),
        out_shape=jax.ShapeDtypeStruct((B, S, GROUP_W), F32),
        compiler_params=_params("parallel", "parallel"), name="pool")(big3, big3, wbd, sc, og)


def _outproj_body(ya_ref, yb_ref, yc_ref, yd_ref, h_ref, wo_ref, g_ref, rw_ref, rb_ref,
                  hnew_ref, hn_ref, ri_ref, rf_ref, *, tm, d):
    y = jnp.concatenate([ya_ref[...], yb_ref[...], yc_ref[...], yd_ref[...]], axis=-1).astype(BF16)
    h_new = h_ref[...] + jnp.dot(y, wo_ref[...], preferred_element_type=F32)
    hnew_ref[...] = h_new
    hn = _rms(h_new, g_ref[...])
    nseg = d // LANES
    for s in range(nseg):
        hn_ref[pl.ds(s, tm, stride=nseg), :] = hn[:, s * LANES:(s + 1) * LANES]

    logits = jnp.dot(hn, rw_ref[...], preferred_element_type=F32,
                     precision=lax.Precision.HIGHEST) + rb_ref[...]
    lane = lax.broadcasted_iota(I32, (tm, LANES), 1)
    neg = -jnp.inf
    big_lane = LANES

    def masked_top(vals, mask):
        v = jnp.where(mask, vals, neg)
        mx = jnp.max(v, axis=-1, keepdims=True)
        idx = jnp.min(jnp.where(mask & (v == mx), lane, big_lane), axis=-1, keepdims=True)
        return v, mx, idx

    gmask = lane < N_GROUPS
    gv, gmx, gidx = masked_top(logits, gmask)
    g_top = 1.0 / jnp.sum(jnp.where(gmask, jnp.exp(gv - gmx), 0.0), axis=-1, keepdims=True)
    lo = N_GROUPS + gidx * EXPERTS_PER_GROUP
    emask = (lane >= lo) & (lane < lo + EXPERTS_PER_GROUP)
    ev, emx, eidx1 = masked_top(logits, emask)
    esum = jnp.sum(jnp.where(emask, jnp.exp(ev - emx), 0.0), axis=-1, keepdims=True)
    p1 = 1.0 / esum
    _, emx2, eidx2 = masked_top(logits, emask & (lane != eidx1))
    p2 = jnp.exp(emx2 - emx) / esum
    denom = p1 + p2
    ri_ref[...] = jnp.where(lane == 0, eidx1 - N_GROUPS, jnp.where(lane == 1, eidx2 - N_GROUPS, 0))
    rf_ref[...] = jnp.where(lane == 0, g_top * p1 / denom, jnp.where(lane == 1, g_top * p2 / denom, 0.0))


def _outproj(ya, yb, yc, yd, h2d, wo, g, rw, rb, tm):
    T, D = h2d.shape
    nseg = D // LANES
    yblk = pl.BlockSpec((tm, GROUP_W), lambda i: (i, 0))
    return pl.pallas_call(
        functools.partial(_outproj_body, tm=tm, d=D), grid=(T // tm,),
        in_specs=[yblk, yblk, yblk, yblk,
                  pl.BlockSpec((tm, D), lambda i: (i, 0)),
                  pl.BlockSpec((D, D), lambda i: (0, 0)),
                  pl.BlockSpec((1, D), lambda i: (0, 0)),
                  pl.BlockSpec((D, LANES), lambda i: (0, 0)),
                  pl.BlockSpec((1, LANES), lambda i: (0, 0))],
        out_specs=[pl.BlockSpec((tm, D), lambda i: (i, 0)),
                   pl.BlockSpec((tm * nseg, LANES), lambda i: (i, 0)),
                   pl.BlockSpec((tm, LANES), lambda i: (i, 0)),
                   pl.BlockSpec((tm, LANES), lambda i: (i, 0))],
        out_shape=[jax.ShapeDtypeStruct((T, D), F32),
                   jax.ShapeDtypeStruct((T * nseg, LANES), F32),
                   jax.ShapeDtypeStruct((T, LANES), I32),
                   jax.ShapeDtypeStruct((T, LANES), F32)],
        compiler_params=_params("parallel"), name="outproj")(ya, yb, yc, yd, h2d, wo, g, rw, rb)


def _moe_body(blk_e_ref, src_ref, dst_ref, hn_hbm, w1_ref, w3_ref, w2_ref, y_hbm,
              xbuf, ybuf, gsem, ssem, *, nseg):
    del blk_e_ref
    R = MOE_BLOCK

    def gather_copy(r):
        return pltpu.make_async_copy(hn_hbm.at[pl.ds(src_ref[0, 0, r] * nseg, nseg), :],
                                     xbuf.at[pl.ds(r * nseg, nseg), :], gsem)

    def scatter_copy(r):
        return pltpu.make_async_copy(ybuf.at[pl.ds(r * nseg, nseg), :],
                                     y_hbm.at[pl.ds(dst_ref[0, 0, r] * nseg, nseg), :], ssem)

    def start_gather(r, c):
        gather_copy(r).start()
        return c

    def wait_gather(r, c):
        gather_copy(r).wait()
        return c

    lax.fori_loop(0, R, start_gather, 0)
    lax.fori_loop(0, R, wait_gather, 0)

    x = jnp.concatenate([xbuf[pl.ds(s, R, stride=nseg), :] for s in range(nseg)], axis=-1).astype(BF16)
    a = jnp.dot(x, w1_ref[...].astype(BF16), preferred_element_type=F32)
    b = jnp.dot(x, w3_ref[...].astype(BF16), preferred_element_type=F32)
    hid = (a * jax.nn.sigmoid(a) * b).astype(BF16)
    y = jnp.dot(hid, w2_ref[...].astype(BF16), preferred_element_type=F32)
    for s in range(nseg):
        ybuf[pl.ds(s, R, stride=nseg), :] = y[:, s * LANES:(s + 1) * LANES]

    def start_scatter(r, c):
        @pl.when(dst_ref[0, 0, r] >= 0)
        def _():
            scatter_copy(r).start()
        return c

    def wait_scatter(r, c):
        @pl.when(dst_ref[0, 0, r] >= 0)
        def _():
            scatter_copy(r).wait()
        return c

    lax.fori_loop(0, R, start_scatter, 0)
    lax.fori_loop(0, R, wait_scatter, 0)


def _moe(blk_e, src3, dst3, hn_rows, w1, w3, w2, n_assign):
    n_blk = blk_e.shape[0]
    _, D, DE = w1.shape
    nseg = D // LANES
    R = MOE_BLOCK
    smem_blk = pl.BlockSpec((1, 1, R), lambda b, be: (b, 0, 0), memory_space=pltpu.SMEM)
    grid_spec = pltpu.PrefetchScalarGridSpec(
        num_scalar_prefetch=1, grid=(n_blk,),
        in_specs=[smem_blk, smem_blk,
                  pl.BlockSpec(memory_space=pl.ANY),
                  pl.BlockSpec((None, D, DE), lambda b, be: (be[b], 0, 0)),
                  pl.BlockSpec((None, D, DE), lambda b, be: (be[b], 0, 0)),
                  pl.BlockSpec((None, DE, D), lambda b, be: (be[b], 0, 0))],
        out_specs=pl.BlockSpec(memory_space=pl.ANY),
        scratch_shapes=[pltpu.VMEM((R * nseg, LANES), F32), pltpu.VMEM((R * nseg, LANES), F32),
                        pltpu.SemaphoreType.DMA, pltpu.SemaphoreType.DMA])
    return pl.pallas_call(
        functools.partial(_moe_body, nseg=nseg), grid_spec=grid_spec,
        out_shape=jax.ShapeDtypeStruct((n_assign * nseg, LANES), F32),
        compiler_params=pltpu.CompilerParams(dimension_semantics=("arbitrary",),
                                             vmem_limit_bytes=VMEM_LIMIT, has_side_effects=True),
        name="moe")(blk_e, src3, dst3, hn_rows, w1, w3, w2)


def _combine_body(h_ref, y_ref, rf_ref, o_ref, *, tm, nseg):
    g0 = rf_ref[:, 0:1]
    g1 = rf_ref[:, 1:2]
    stride = TOPK_IN * nseg
    y0 = jnp.concatenate([y_ref[pl.ds(s, tm, stride=stride), :] for s in range(nseg)], axis=-1)
    y1 = jnp.concatenate([y_ref[pl.ds(nseg + s, tm, stride=stride), :] for s in range(nseg)], axis=-1)
    o_ref[...] = h_ref[...] + (y0 * g0 + y1 * g1)


def _combine(h2d, y_rows, rf, tm):
    T, D = h2d.shape
    nseg = D // LANES
    return pl.pallas_call(
        functools.partial(_combine_body, tm=tm, nseg=nseg), grid=(T // tm,),
        in_specs=[pl.BlockSpec((tm, D), lambda i: (i, 0)),
                  pl.BlockSpec((tm * TOPK_IN * nseg, LANES), lambda i: (i, 0)),
                  pl.BlockSpec((tm, LANES), lambda i: (i, 0))],
        out_specs=pl.BlockSpec((tm, D), lambda i: (i, 0)),
        out_shape=jax.ShapeDtypeStruct((T, D), F32),
        compiler_params=_params("parallel"), name="combine")(h2d, y_rows, rf)


def _dispatch_plan(expert, n_tok):
    A = n_tok * TOPK_IN
    flat_e = expert.reshape(A)
    order = jnp.argsort(flat_e)
    sorted_e = flat_e[order]
    counts = jnp.bincount(flat_e, length=N_EXPERTS)
    padded = (counts + MOE_BLOCK - 1) // MOE_BLOCK * MOE_BLOCK
    pad_end = jnp.cumsum(padded)
    pad_start = pad_end - padded
    start = jnp.cumsum(counts) - counts
    dest = pad_start[sorted_e] + jnp.arange(A) - start[sorted_e]
    n_blk = -(-A // MOE_BLOCK) + N_EXPERTS
    n_pad = n_blk * MOE_BLOCK
    src_tok = jnp.zeros((n_pad,), I32).at[dest].set((order // TOPK_IN).astype(I32))
    dst_slot = jnp.full((n_pad,), -1, I32).at[dest].set(order.astype(I32))
    blk_e = jnp.minimum(jnp.searchsorted(pad_end, jnp.arange(n_blk) * MOE_BLOCK, side='right'),
                        N_EXPERTS - 1).astype(I32)
    return blk_e, src_tok.reshape(n_blk, 1, MOE_BLOCK), dst_slot.reshape(n_blk, 1, MOE_BLOCK)


def _pad_lanes(a, lane0, rows=1):
    out = jnp.zeros((rows, LANES), a.dtype)
    return out.at[:, lane0:lane0 + a.shape[-1]].set(a.reshape(rows, -1))


def _layer(h2d, B, S, p):
    T, D = h2d.shape
    tile = lambda a, n: jnp.tile(a.reshape(1, -1), (1, n))

    offs = np.cumsum([0, GROUP_W, GROUP_W, GROUP_W, N_HEADS, GROUP_W, GROUP_W,
                      GROUP_W, GROUP_W, GROUP_W, N_HEADS, N_HEADS, GROUP_W, GROUP_W])
    seg = lambda k: p['w_in'][:, offs[k]:offs[k + 1]]
    wb = jnp.concatenate([seg(0), seg(1), seg(2), seg(4), seg(5), seg(6), seg(7), seg(8), seg(11), seg(12)],
                         axis=1).astype(BF16)
    ws = jnp.zeros((D, LANES), F32)
    ws = ws.at[:, LANE_FOX:LANE_FOX + N_HEADS].set(seg(3))
    ws = ws.at[:, LANE_DECAY:LANE_DECAY + N_HEADS].set(seg(9))
    ws = ws.at[:, LANE_BETA:LANE_BETA + N_HEADS].set(seg(10)).astype(BF16)
    gate_prm = jnp.concatenate([_pad_lanes(p['fox_f_bias'], LANE_FOX), _pad_lanes(p['gdn_dt_bias'], LANE_DECAY),
                                _pad_lanes(p['gdn_a_log'], LANE_DECAY), jnp.zeros((SUBLANES - 3, LANES), F32)], axis=0)

    big, small = _inproj(h2d, p['attn_norm_g'].reshape(1, D), wb, ws, tm=min(512, T))
    big3 = big.reshape(B, S, N_BIG_COLS * GROUP_W)
    ts = min(512, S)
    gcol, grow = _gates(small.reshape(B, S, LANES), gate_prm, ts)

    tq = min(256, S)
    crow4 = grow.reshape(B, 2 * SUBLANES, S // tq, tq).transpose(0, 2, 1, 3)
    ya = _fox(big3, gcol, crow4, tile(p['fox_qn_g'], N_HEADS), tile(p['fox_kn_g'], N_HEADS),
              p['fox_out_g'].reshape(1, GROUP_W), tq)

    bst = jnp.repeat(p['gmlp_bs'].T, HEAD_DIM, axis=1)
    yb = _gmlp(big, p['gmlp_ln_g'].reshape(1, -1), p['gmlp_ln_b'].reshape(1, -1), p['gmlp_ws'], bst,
               p['gmlp_out_g'].reshape(1, -1), tm=min(512, T))

    conv_w3 = p['gdn_conv_w'].reshape(CONV_K, 3, GROUP_W).transpose(1, 0, 2)
    qn, kn, vv = _gdn_prep(big3, conv_w3, ts)
    grow4 = grow.reshape(B, 2 * SUBLANES, S // CHUNK, CHUNK).transpose(0, 2, 1, 3)
    yc = _gdn(qn, kn, vv, big3, gcol, grow4, p['gdn_norm_g'].reshape(1, HEAD_DIM), ts)

    wbd = jax.scipy.linalg.block_diag(*[p['pool_w'][g] for g in range(len(POOL_WINDOWS))]).astype(BF16)
    yd = _pool(big3, wbd, p['pool_scale'].reshape(1, -1), p['pool_out_g'].reshape(1, -1), ts)

    rw = jnp.zeros((D, LANES), F32).at[:, :N_GROUPS].set(p['router_g_w'])
    rw = rw.at[:, N_GROUPS:N_GROUPS + N_EXPERTS].set(p['router_e_w'])
    rb = jnp.zeros((1, LANES), F32).at[0, :N_GROUPS].set(p['router_g_b'])
    rb = rb.at[0, N_GROUPS:N_GROUPS + N_EXPERTS].set(p['router_e_b'])
    flat = lambda a: a.reshape(T, GROUP_W)
    h_new, hn_rows, ri, rf = _outproj(flat(ya), yb, flat(yc), flat(yd), h2d, p['w_out'].astype(BF16),
                                      p['ffn_norm_g'].reshape(1, D), rw, rb, tm=min(512, T))

    blk_e, src3, dst3 = _dispatch_plan(ri[:, :TOPK_IN], T)
    y_rows = _moe(blk_e, src3, dst3, hn_rows, p['moe_w1'], p['moe_w3'], p['moe_w2'], T * TOPK_IN)
    return _combine(h_new, y_rows, rf, tm=min(512, T))


def kernel(x, attn_norm_g, w_in, w_out, fox_f_bias, fox_qn_g, fox_kn_g, fox_out_g, gmlp_ln_g, gmlp_ln_b, gmlp_ws, gmlp_bs, gmlp_out_g, gdn_conv_w, gdn_a_log, gdn_dt_bias, gdn_norm_g, pool_w, pool_scale, pool_out_g, ffn_norm_g, router_g_w, router_g_b, router_e_w, router_e_b, moe_w1, moe_w3, moe_w2):
    B, S, D = x.shape
    names = ('attn_norm_g', 'w_in', 'w_out', 'fox_f_bias', 'fox_qn_g', 'fox_kn_g', 'fox_out_g', 'gmlp_ln_g',
             'gmlp_ln_b', 'gmlp_ws', 'gmlp_bs', 'gmlp_out_g', 'gdn_conv_w', 'gdn_a_log', 'gdn_dt_bias',
             'gdn_norm_g', 'pool_w', 'pool_scale', 'pool_out_g', 'ffn_norm_g', 'router_g_w', 'router_g_b',
             'router_e_w', 'router_e_b', 'moe_w1', 'moe_w3', 'moe_w2')
    vals = (attn_norm_g, w_in, w_out, fox_f_bias, fox_qn_g, fox_kn_g, fox_out_g, gmlp_ln_g, gmlp_ln_b, gmlp_ws,
            gmlp_bs, gmlp_out_g, gdn_conv_w, gdn_a_log, gdn_dt_bias, gdn_norm_g, pool_w, pool_scale, pool_out_g,
            ffn_norm_g, router_g_w, router_g_b, router_e_w, router_e_b, moe_w1, moe_w3, moe_w2)
    h = x.reshape(B * S, D)
    for l in range(w_in.shape[0]):
        h = _layer(h, B, S, {n: v[l] for n, v in zip(names, vals)})
    return h.reshape(B, S, D)
```

```python
import functools

import jax
import jax.numpy as jnp
import numpy as np
from jax import lax
from jax.experimental import pallas as pl
from jax.experimental.pallas import tpu as pltpu

F32 = jnp.float32
BF16 = jnp.bfloat16
I32 = jnp.int32

EPS = 1e-6
HEAD_DIM = 64
GROUP_W = 256
N_HEADS = GROUP_W // HEAD_DIM
CHUNK = 64
GMLP_LEN = 128
CONV_K = 4
POOL_WINDOWS = (2, 4, 8, 16)
N_GROUPS = 4
EXPERTS_PER_GROUP = 8
N_EXPERTS = N_GROUPS * EXPERTS_PER_GROUP
TOPK_IN = 2
MOE_BLOCK = 128
LANES = 128
SUBLANES = 8
VMEM_LIMIT = 56 * 1024 * 1024

COL_FQ, COL_FK, COL_FV, COL_GU, COL_GV, COL_DQ, COL_DK, COL_DV, COL_DG, COL_PZ = range(10)
N_BIG_COLS = 10
LANE_FOX, LANE_DECAY, LANE_BETA = 0, 4, 8


def _params(*sem):
    return pltpu.CompilerParams(dimension_semantics=sem, vmem_limit_bytes=VMEM_LIMIT)


def _head_ones():
    r = lax.broadcasted_iota(I32, (GROUP_W, GROUP_W), 0) // HEAD_DIM
    c = lax.broadcasted_iota(I32, (GROUP_W, GROUP_W), 1) // HEAD_DIM
    return (r == c).astype(BF16)


def _head_sums(x, ones_bd):
    hi = x.astype(BF16)
    lo = (x - hi.astype(F32)).astype(BF16)
    return (jnp.dot(hi, ones_bd, preferred_element_type=F32)
            + jnp.dot(lo, ones_bd, preferred_element_type=F32))


def _rms(x, g):
    return x * lax.rsqrt(jnp.mean(x * x, axis=-1, keepdims=True) + EPS) * g


def _mm(a, b):
    return jnp.dot(a.astype(BF16), b.astype(BF16), preferred_element_type=F32)


def _mm_nt(a, b):
    return lax.dot_general(a.astype(BF16), b.astype(BF16), (((1,), (1,)), ((), ())),
                           preferred_element_type=F32)


def _mm_tn(a, b):
    return lax.dot_general(a.astype(BF16), b.astype(BF16), (((0,), (0,)), ((), ())),
                           preferred_element_type=F32)


def _split(a):
    hi = a.astype(BF16)
    return hi, (a - hi.astype(F32)).astype(BF16)


def _mm3(a, b):
    ah, al = _split(a)
    bh, bl = _split(b)
    d = functools.partial(jnp.dot, preferred_element_type=F32)
    return d(ah, bh) + (d(ah, bl) + d(al, bh))


def _inproj_body(x_ref, g_ref, wb_ref, ws_ref, big_ref, small_ref):
    xn = _rms(x_ref[...], g_ref[...]).astype(BF16)
    big_ref[...] = jnp.dot(xn, wb_ref[...], preferred_element_type=F32)
    small_ref[...] = jnp.dot(xn, ws_ref[...], preferred_element_type=F32)


def _inproj(x2d, g, wb, ws, tm):
    T, D = x2d.shape
    nb = wb.shape[1]
    return pl.pallas_call(
        _inproj_body, grid=(T // tm,),
        in_specs=[pl.BlockSpec((tm, D), lambda i: (i, 0)),
                  pl.BlockSpec((1, D), lambda i: (0, 0)),
                  pl.BlockSpec((D, nb), lambda i: (0, 0)),
                  pl.BlockSpec((D, LANES), lambda i: (0, 0))],
        out_specs=[pl.BlockSpec((tm, nb), lambda i: (i, 0)),
                   pl.BlockSpec((tm, LANES), lambda i: (i, 0))],
        out_shape=[jax.ShapeDtypeStruct((T, nb), F32), jax.ShapeDtypeStruct((T, LANES), F32)],
        compiler_params=_params("parallel"), name="inproj")(x2d, g, wb, ws)


def _gates_body(sm_ref, p_ref, col_ref, row_ref, carry_ref, *, ts):
    @pl.when(pl.program_id(1) == 0)
    def _():
        carry_ref[...] = jnp.zeros_like(carry_ref)

    x = sm_ref[...]
    lane = lax.broadcasted_iota(I32, (ts, LANES), 1)
    is_fox = lane < LANE_DECAY
    is_dec = (lane >= LANE_DECAY) & (lane < LANE_BETA)
    is_beta = (lane >= LANE_BETA) & (lane < LANE_BETA + N_HEADS)
    logf = jax.nn.log_sigmoid(x + p_ref[0:1, :])
    g = -jnp.exp(p_ref[2:3, :]) * jax.nn.softplus(x + p_ref[1:2, :])
    beta = jax.nn.sigmoid(x)
    r = lax.broadcasted_iota(I32, (ts, ts), 0)
    c = lax.broadcasted_iota(I32, (ts, ts), 1)
    tri_full = (r >= c).astype(F32)
    tri_chunk = ((r >= c) & (r // CHUNK == c // CHUNK)).astype(F32)
    hp = functools.partial(jnp.dot, preferred_element_type=F32, precision=lax.Precision.HIGHEST)
    cf = hp(tri_full, jnp.where(is_fox, logf, 0.0)) + carry_ref[...]
    cg = hp(tri_chunk, jnp.where(is_dec, g, 0.0))
    carry_ref[...] = cf[ts - 1:ts, :]
    out = jnp.where(is_fox, cf, jnp.where(is_dec, cg, jnp.where(is_beta, beta, 0.0)))
    col_ref[...] = out
    row_ref[...] = out.T[:2 * SUBLANES, :]


def _gates(small3, prm, ts):
    B, S, _ = small3.shape
    return pl.pallas_call(
        functools.partial(_gates_body, ts=ts), grid=(B, S // ts),
        in_specs=[pl.BlockSpec((None, ts, LANES), lambda b, j: (b, j, 0)),
                  pl.BlockSpec((SUBLANES, LANES), lambda b, j: (0, 0))],
        out_specs=[pl.BlockSpec((None, ts, LANES), lambda b, j: (b, j, 0)),
                   pl.BlockSpec((None, 2 * SUBLANES, ts), lambda b, j: (b, 0, j))],
        out_shape=[jax.ShapeDtypeStruct((B, S, LANES), F32),
                   jax.ShapeDtypeStruct((B, 2 * SUBLANES, S), F32)],
        scratch_shapes=[pltpu.VMEM((1, LANES), F32)],
        compiler_params=_params("parallel", "arbitrary"), name="gates")(small3, prm)


def _fox_body(q_ref, k_ref, v_ref, ccol_ref, qg_ref, kg_ref, og_ref, o_ref,
              kn_scr, vt_scr, q_scr, m_scr, l_scr, acc_scr, *, tq, nk):
    i = pl.program_id(1)
    ones_bd = _head_ones()
    lane = lax.broadcasted_iota(I32, (tq, LANES), 1)
    log2e = 1.0 / np.log(2.0)

    def head_norm(x, g):
        ss = _head_sums(x * x, ones_bd)
        return x * lax.rsqrt(ss * (1.0 / HEAD_DIM) + EPS) * g

    def head_tile(x, h, extra):
        pair = x[:, (h // 2) * LANES:(h // 2 + 1) * LANES]
        if h % 2:
            pair = pltpu.roll(pair, HEAD_DIM, 1)
        return jnp.where(lane < HEAD_DIM, pair, extra).astype(BF16)

    @pl.when(i == 0)
    def _():
        for c in range(nk):
            rows = slice(c * tq, (c + 1) * tq)
            kc = head_norm(k_ref[rows, :], kg_ref[...])
            vt = v_ref[rows, :].T.astype(BF16)
            cc = ccol_ref[rows, :] * (-log2e)
            for h in range(N_HEADS):
                cj = cc[:, LANE_FOX + h:LANE_FOX + h + 1]
                hi = cj.astype(BF16).astype(F32)
                mid = (cj - hi).astype(BF16).astype(F32)
                lo = (cj - hi - mid).astype(BF16).astype(F32)
                extra = jnp.where(lane == HEAD_DIM, hi, jnp.where(lane == HEAD_DIM + 1, mid,
                                  jnp.where(lane == HEAD_DIM + 2, lo, 0.0)))
                kn_scr[h, c] = head_tile(kc, h, extra)
                vt_scr[h, c] = vt[h * HEAD_DIM:(h + 1) * HEAD_DIM, :]

    qn = head_norm(q_ref[...], qg_ref[...]) * (HEAD_DIM ** -0.5 * log2e)
    q_ones = jnp.where(lane < HEAD_DIM + 3, 1.0, 0.0)
    for h in range(N_HEADS):
        q_scr[h] = head_tile(qn, h, q_ones)
    m_scr[...] = jnp.full_like(m_scr, -jnp.inf)
    l_scr[...] = jnp.zeros_like(l_scr)
    acc_scr[...] = jnp.zeros_like(acc_scr)
    causal = (lax.broadcasted_iota(I32, (tq, tq), 0) <= lax.broadcasted_iota(I32, (tq, tq), 1))

    def step(j, masked):
        heads = range(N_HEADS)
        s = [lax.dot_general(kn_scr[h, j], q_scr[h], (((1,), (1,)), ((), ())),
                             preferred_element_type=F32) for h in heads]
        if masked:
            s = [jnp.where(causal, s[h], -jnp.inf) for h in heads]
        p, alpha = [], []
        for h in heads:
            m_old = m_scr[h]
            m_new = jnp.maximum(m_old, jnp.max(s[h], axis=0, keepdims=True))
            alpha.append(jnp.exp2(m_old - m_new))
            ph = jnp.exp2(s[h] - m_new)
            l_scr[h] = alpha[h] * l_scr[h] + jnp.sum(ph, axis=0, keepdims=True)
            m_scr[h] = m_new
            p.append(ph.astype(BF16))
        pv = [jnp.dot(vt_scr[h, j], p[h], preferred_element_type=F32) for h in heads]
        for h in heads:
            acc_scr[h] = alpha[h] * acc_scr[h] + pv[h]

    def unmasked(j, c):
        step(j, False)
        return c

    lax.fori_loop(0, i, unmasked, 0)
    step(i, True)
    o_t = jnp.concatenate([acc_scr[h] / l_scr[h] for h in range(N_HEADS)], axis=0)
    o_ref[...] = _rms(o_t.T, og_ref[...])


def _fox(big3, gcol, qg, kg, og, tq):
    B, S, _ = big3.shape
    nk = S // tq
    row = pl.BlockSpec((1, GROUP_W), lambda b, i: (0, 0))
    return pl.pallas_call(
        functools.partial(_fox_body, tq=tq, nk=nk), grid=(B, nk),
        in_specs=[pl.BlockSpec((None, tq, GROUP_W), lambda b, i: (b, i, COL_FQ)),
                  pl.BlockSpec((None, S, GROUP_W), lambda b, i: (b, 0, COL_FK)),
                  pl.BlockSpec((None, S, GROUP_W), lambda b, i: (b, 0, COL_FV)),
                  pl.BlockSpec((None, S, LANES), lambda b, i: (b, 0, 0)),
                  row, row, row],
        out_specs=pl.BlockSpec((None, tq, GROUP_W), lambda b, i: (b, i, 0)),
        out_shape=jax.ShapeDtypeStruct((B, S, GROUP_W), F32),
        scratch_shapes=[pltpu.VMEM((N_HEADS, nk, tq, LANES), BF16),
                        pltpu.VMEM((N_HEADS, nk, HEAD_DIM, tq), BF16),
                        pltpu.VMEM((N_HEADS, tq, LANES), BF16),
                        pltpu.VMEM((N_HEADS, 1, tq), F32),
                        pltpu.VMEM((N_HEADS, 1, tq), F32),
                        pltpu.VMEM((N_HEADS, HEAD_DIM, tq), F32)],
        compiler_params=_params("parallel", "arbitrary"), name="fox")(
            big3, big3, big3, gcol, qg, kg, og)


def _gelu(x):
    return 0.5 * x * (1.0 + lax.erf(x * (2.0 ** -0.5)))


def _gmlp_body(u_ref, v_ref, lg_ref, lb_ref, ws_ref, bst_ref, og_ref, o_ref, *, nwin):
    L = GMLP_LEN
    r = lax.broadcasted_iota(I32, (L, L), 0) // CHUNK
    c = lax.broadcasted_iota(I32, (L, L), 1) // CHUNK
    mask = r >= c
    ws = [jnp.where(mask, ws_ref[h], 0.0).astype(BF16) for h in range(N_HEADS)]
    for n in range(nwin):
        u = _gelu(u_ref[n * L:(n + 1) * L, :])
        v = _gelu(v_ref[n * L:(n + 1) * L, :])
        mu = jnp.mean(v, axis=-1, keepdims=True)
        vc = v - mu
        var = jnp.mean(vc * vc, axis=-1, keepdims=True)
        vn = (vc * lax.rsqrt(var + EPS) * lg_ref[...] + lb_ref[...]).astype(BF16)
        mixed = jnp.concatenate(
            [jnp.dot(ws[h], vn[:, h * HEAD_DIM:(h + 1) * HEAD_DIM], preferred_element_type=F32)
             for h in range(N_HEADS)], axis=-1) + bst_ref[...]
        o_ref[n * L:(n + 1) * L, :] = _rms(u * mixed, og_ref[...])


def _gmlp(big, lg, lb, ws, bst, og, tm):
    T = big.shape[0]
    row = pl.BlockSpec((1, GROUP_W), lambda i: (0, 0))
    return pl.pallas_call(
        functools.partial(_gmlp_body, nwin=tm // GMLP_LEN), grid=(T // tm,),
        in_specs=[pl.BlockSpec((tm, GROUP_W), lambda i: (i, COL_GU)),
                  pl.BlockSpec((tm, GROUP_W), lambda i: (i, COL_GV)),
                  row, row,
                  pl.BlockSpec((N_HEADS, GMLP_LEN, GMLP_LEN), lambda i: (0, 0, 0)),
                  pl.BlockSpec((GMLP_LEN, GROUP_W), lambda i: (0, 0)),
                  row],
        out_specs=pl.BlockSpec((tm, GROUP_W), lambda i: (i, 0)),
        out_shape=jax.ShapeDtypeStruct((T, GROUP_W), F32),
        compiler_params=_params("parallel"), name="gmlp")(big, big, lg, lb, ws, bst, og)


def _gdn_prep_body(q_ref, k_ref, v_ref, hq_ref, hk_ref, hv_ref, w_ref, qo_ref, ko_ref, vo_ref, *, ts):
    first = pl.program_id(1) == 0
    ones_bd = _head_ones()

    def conv(x_ref, halo_ref, w):
        halo = jnp.where(first, 0.0, halo_ref[...])
        xx = jnp.concatenate([halo, x_ref[...]], axis=0)
        y = w[CONV_K - 1:CONV_K, :] * xx[SUBLANES:, :]
        for j in range(CONV_K - 1):
            y = y + w[j:j + 1, :] * pltpu.roll(xx, CONV_K - 1 - j, 0)[SUBLANES:, :]
        return y * jax.nn.sigmoid(y)

    def l2(t):
        return t * lax.rsqrt(_head_sums(t * t, ones_bd) + EPS)

    qo_ref[...] = l2(conv(q_ref, hq_ref, w_ref[0])) * (HEAD_DIM ** -0.5)
    ko_ref[...] = l2(conv(k_ref, hk_ref, w_ref[1]))
    vo_ref[...] = conv(v_ref, hv_ref, w_ref[2])


def _gdn_prep(big3, conv_w3, ts):
    B, S, _ = big3.shape
    hb = ts // SUBLANES
    blk = lambda col: pl.BlockSpec((None, ts, GROUP_W), lambda b, i: (b, i, col))
    halo = lambda col: pl.BlockSpec((None, SUBLANES, GROUP_W),
                                    lambda b, i: (b, jnp.maximum(i * hb - 1, 0), col))
    out = pl.BlockSpec((None, ts, GROUP_W), lambda b, i: (b, i, 0))
    shp = jax.ShapeDtypeStruct((B, S, GROUP_W), F32)
    return pl.pallas_call(
        functools.partial(_gdn_prep_body, ts=ts), grid=(B, S // ts),
        in_specs=[blk(COL_DQ), blk(COL_DK), blk(COL_DV), halo(COL_DQ), halo(COL_DK), halo(COL_DV),
                  pl.BlockSpec((3, CONV_K, GROUP_W), lambda b, i: (0, 0, 0))],
        out_specs=[out, out, out], out_shape=[shp, shp, shp],
        compiler_params=_params("parallel", "parallel"), name="gdn_prep")(
            big3, big3, big3, big3, big3, big3, conv_w3)


def _gdn_body(q_ref, k_ref, v_ref, gate_ref, gcol_ref, grow_ref, ng_ref, o_ref,
              s_scr, u_scr, wq_scr, a_scr, kd_scr, dl_scr, t_scr, p_scr, rhs_scr, *, nchunk):
    C = CHUNK

    @pl.when(pl.program_id(1) == 0)
    def _():
        s_scr[...] = jnp.zeros_like(s_scr)

    r = lax.broadcasted_iota(I32, (C, C), 0)
    c = lax.broadcasted_iota(I32, (C, C), 1)
    tri = r >= c
    strict = r > c
    eye = (r == c).astype(F32)

    items = [(n, h) for n in range(nchunk) for h in range(N_HEADS)]
    mmb = functools.partial(jnp.dot, preferred_element_type=F32)
    for it, (n, h) in enumerate(items):
        rows = slice(n * C, (n + 1) * C)
        sl = slice(h * HEAD_DIM, (h + 1) * HEAD_DIM)
        q, k, v = q_ref[rows, sl], k_ref[rows, sl], v_ref[rows, sl]
        gc = gcol_ref[rows, LANE_DECAY + h:LANE_DECAY + h + 1]
        gr = grow_ref[n, LANE_DECAY + h:LANE_DECAY + h + 1, :]
        beta = gcol_ref[rows, LANE_BETA + h:LANE_BETA + h + 1]
        decay = jnp.exp(jnp.where(tri, gc - gr, -jnp.inf))
        kb = k * beta
        kk = _mm_nt(jnp.concatenate([kb, q], axis=0), k)
        x = jnp.where(strict, -(kk[:C] * decay), 0.0)
        t_scr[it] = eye + x
        p_scr[it] = x.astype(BF16)
        eg = jnp.exp(gc)
        g_last = gc[C - 1:C, :]
        rhs_scr[it] = jnp.concatenate([v * beta, kb * eg], axis=1).astype(BF16)
        wq_scr[n, h, C:, :] = (q * eg).astype(BF16)
        a_scr[n, h] = jnp.where(tri, kk[C:] * decay, 0.0).astype(BF16)
        kd_scr[n, h] = (k * jnp.exp(g_last - gc)).astype(BF16)
        dl_scr[n, h] = jnp.broadcast_to(jnp.exp(g_last), (1, HEAD_DIM))
    for _ in range(5):
        for it in range(len(items)):
            p = p_scr[it]
            p_scr[it] = mmb(p, p).astype(BF16)
        for it in range(len(items)):
            t = t_scr[it]
            t_scr[it] = t + mmb(t.astype(BF16), p_scr[it])
    for it, (n, h) in enumerate(items):
        uw = mmb(t_scr[it].astype(BF16), rhs_scr[it])
        u_scr[n, h] = uw[:, :HEAD_DIM]
        wq_scr[n, h, :C, :] = uw[:, HEAD_DIM:].astype(BF16)

    heads = range(N_HEADS)
    state = [s_scr[h] for h in heads]
    for n in range(nchunk):
        rows = slice(n * C, (n + 1) * C)
        ws = [mmb(wq_scr[n, h], state[h].astype(BF16)) for h in heads]
        vb = [(u_scr[n, h] - ws[h][:C]).astype(BF16) for h in heads]
        o = [ws[h][C:] + mmb(a_scr[n, h], vb[h]) for h in heads]
        state = [state[h] * dl_scr[n, h] + lax.dot_general(
            kd_scr[n, h], vb[h], (((0,), (0,)), ((), ())), preferred_element_type=F32) for h in heads]
        gate = gate_ref[rows, :]
        y = jnp.concatenate([_rms(o[h], ng_ref[...]) for h in heads], axis=-1)
        o_ref[rows, :] = y * (gate * jax.nn.sigmoid(gate))
    for h in heads:
        s_scr[h] = state[h]


def _gdn(qn, kn, vv, big3, gcol, grow4, ng, ts):
    B, S, _ = qn.shape
    nchunk = ts // CHUNK
    blk = pl.BlockSpec((None, ts, GROUP_W), lambda b, i: (b, i, 0))
    return pl.pallas_call(
        functools.partial(_gdn_body, nchunk=nchunk), grid=(B, S // ts),
        in_specs=[blk, blk, blk,
                  pl.BlockSpec((None, ts, GROUP_W), lambda b, i: (b, i, COL_DG)),
                  pl.BlockSpec((None, ts, LANES), lambda b, i: (b, i, 0)),
                  pl.BlockSpec((None, nchunk, 2 * SUBLANES, CHUNK), lambda b, i: (b, i, 0, 0)),
                  pl.BlockSpec((1, HEAD_DIM), lambda b, i: (0, 0))],
        out_specs=blk, out_shape=jax.ShapeDtypeStruct((B, S, GROUP_W), F32),
        scratch_shapes=[pltpu.VMEM((N_HEADS, HEAD_DIM, HEAD_DIM), F32),
                        pltpu.VMEM((nchunk, N_HEADS, CHUNK, HEAD_DIM), F32),
                        pltpu.VMEM((nchunk, N_HEADS, 2 * CHUNK, HEAD_DIM), BF16),
                        pltpu.VMEM((nchunk, N_HEADS, CHUNK, CHUNK), BF16),
                        pltpu.VMEM((nchunk, N_HEADS, CHUNK, HEAD_DIM), BF16),
                        pltpu.VMEM((nchunk, N_HEADS, 1, HEAD_DIM), F32),
                        pltpu.VMEM((nchunk * N_HEADS, CHUNK, CHUNK), F32),
                        pltpu.VMEM((nchunk * N_HEADS, CHUNK, CHUNK), BF16),
                        pltpu.VMEM((nchunk * N_HEADS, CHUNK, 2 * HEAD_DIM), BF16)],
        compiler_params=_params("parallel", "arbitrary"), name="gdn")(
            qn, kn, vv, big3, gcol, grow4, ng)


def _pool_body(z_ref, halo_ref, w_ref, sc_ref, og_ref, o_ref, *, ts):
    i = pl.program_id(1)
    hr = 2 * SUBLANES
    z = z_ref[...]
    halo = jnp.where(i == 0, 0.0, halo_ref[...])
    s1 = jnp.concatenate([halo, z], axis=0)
    s2 = s1 + pltpu.roll(s1, 1, 0)
    s4 = s2 + pltpu.roll(s2, 2, 0)
    s8 = s4 + pltpu.roll(s4, 4, 0)
    s16 = s8 + pltpu.roll(s8, 8, 0)
    grp = lax.broadcasted_iota(I32, (ts, GROUP_W), 1) // (GROUP_W // len(POOL_WINDOWS))
    t = lax.broadcasted_iota(I32, (ts, GROUP_W), 0) + i * ts
    total = jnp.where(grp == 0, s2[hr:], jnp.where(grp == 1, s4[hr:], jnp.where(grp == 2, s8[hr:], s16[hr:])))
    win = jnp.where(grp == 0, POOL_WINDOWS[0], jnp.where(grp == 1, POOL_WINDOWS[1],
                    jnp.where(grp == 2, POOL_WINDOWS[2], POOL_WINDOWS[3])))
    pooled = total / jnp.minimum(t + 1, win).astype(F32)
    y = _mm(pooled - z, w_ref[...]) * sc_ref[...]
    o_ref[...] = _rms(y, og_ref[...])


def _pool(big3, wbd, sc, og, ts):
    B, S, _ = big3.shape
    hr = 2 * SUBLANES
    hb = ts // hr
    row = pl.BlockSpec((1, GROUP_W), lambda b, i: (0, 0))
    return pl.pallas_call(
        functools.partial(_pool_body, ts=ts), grid=(B, S // ts),
        in_specs=[pl.BlockSpec((None, ts, GROUP_W), lambda b, i: (b, i, COL_PZ)),
                  pl.BlockSpec((None, hr, GROUP_W), lambda b, i: (b, jnp.maximum(i * hb - 1, 0), COL_PZ)),
                  pl.BlockSpec((GROUP_W, GROUP_W), lambda b, i: (0, 0)), row, row],
        out_specs=pl.BlockSpec((None, ts, GROUP_W), lambda b, i: (b, i, 0)),
        out_shape=jax.ShapeDtypeStruct((B, S, GROUP_W), F32),
        compiler_params=_params("parallel", "parallel"), name="pool")(big3, big3, wbd, sc, og)


def _outproj_body(ya_ref, yb_ref, yc_ref, yd_ref, h_ref, wo_ref, g_ref, rw_ref, rb_ref,
                  hnew_ref, hn_ref, ri_ref, rf_ref, *, tm, d):
    y = jnp.concatenate([ya_ref[...], yb_ref[...], yc_ref[...], yd_ref[...]], axis=-1).astype(BF16)
    h_new = h_ref[...] + jnp.dot(y, wo_ref[...], preferred_element_type=F32)
    hnew_ref[...] = h_new
    hn = _rms(h_new, g_ref[...])
    nseg = d // LANES
    for s in range(nseg):
        hn_ref[pl.ds(s, tm, stride=nseg), :] = hn[:, s * LANES:(s + 1) * LANES]

    logits = jnp.dot(hn, rw_ref[...], preferred_element_type=F32,
                     precision=lax.Precision.HIGHEST) + rb_ref[...]
    lane = lax.broadcasted_iota(I32, (tm, LANES), 1)
    neg = -jnp.inf
    big_lane = LANES

    def masked_top(vals, mask):
        v = jnp.where(mask, vals, neg)
        mx = jnp.max(v, axis=-1, keepdims=True)
        idx = jnp.min(jnp.where(mask & (v == mx), lane, big_lane), axis=-1, keepdims=True)
        return v, mx, idx

    gmask = lane < N_GROUPS
    gv, gmx, gidx = masked_top(logits, gmask)
    g_top = 1.0 / jnp.sum(jnp.where(gmask, jnp.exp(gv - gmx), 0.0), axis=-1, keepdims=True)
    lo = N_GROUPS + gidx * EXPERTS_PER_GROUP
    emask = (lane >= lo) & (lane < lo + EXPERTS_PER_GROUP)
    ev, emx, eidx1 = masked_top(logits, emask)
    esum = jnp.sum(jnp.where(emask, jnp.exp(ev - emx), 0.0), axis=-1, keepdims=True)
    p1 = 1.0 / esum
    _, emx2, eidx2 = masked_top(logits, emask & (lane != eidx1))
    p2 = jnp.exp(emx2 - emx) / esum
    denom = p1 + p2
    ri_ref[...] = jnp.where(lane == 0, eidx1 - N_GROUPS, jnp.where(lane == 1, eidx2 - N_GROUPS, 0))
    rf_ref[...] = jnp.where(lane == 0, g_top * p1 / denom, jnp.where(lane == 1, g_top * p2 / denom, 0.0))


def _outproj(ya, yb, yc, yd, h2d, wo, g, rw, rb, tm):
    T, D = h2d.shape
    nseg = D // LANES
    yblk = pl.BlockSpec((tm, GROUP_W), lambda i: (i, 0))
    return pl.pallas_call(
        functools.partial(_outproj_body, tm=tm, d=D), grid=(T // tm,),
        in_specs=[yblk, yblk, yblk, yblk,
                  pl.BlockSpec((tm, D), lambda i: (i, 0)),
                  pl.BlockSpec((D, D), lambda i: (0, 0)),
                  pl.BlockSpec((1, D), lambda i: (0, 0)),
                  pl.BlockSpec((D, LANES), lambda i: (0, 0)),
                  pl.BlockSpec((1, LANES), lambda i: (0, 0))],
        out_specs=[pl.BlockSpec((tm, D), lambda i: (i, 0)),
                   pl.BlockSpec((tm * nseg, LANES), lambda i: (i, 0)),
                   pl.BlockSpec((tm, LANES), lambda i: (i, 0)),
                   pl.BlockSpec((tm, LANES), lambda i: (i, 0))],
        out_shape=[jax.ShapeDtypeStruct((T, D), F32),
                   jax.ShapeDtypeStruct((T * nseg, LANES), F32),
                   jax.ShapeDtypeStruct((T, LANES), I32),
                   jax.ShapeDtypeStruct((T, LANES), F32)],
        compiler_params=_params("parallel"), name="outproj")(ya, yb, yc, yd, h2d, wo, g, rw, rb)


def _moe_body(blk_e_ref, src_ref, dst_ref, hn_hbm, w1_ref, w3_ref, w2_ref, y_hbm,
              xbuf, ybuf, gsem, ssem, *, nseg):
    del blk_e_ref
    R = MOE_BLOCK

    def gather_copy(r):
        return pltpu.make_async_copy(hn_hbm.at[pl.ds(src_ref[0, 0, r] * nseg, nseg), :],
                                     xbuf.at[pl.ds(r * nseg, nseg), :], gsem)

    def scatter_copy(r):
        return pltpu.make_async_copy(ybuf.at[pl.ds(r * nseg, nseg), :],
                                     y_hbm.at[pl.ds(dst_ref[0, 0, r] * nseg, nseg), :], ssem)

    def start_gather(r, c):
        gather_copy(r).start()
        return c

    def wait_gather(r, c):
        gather_copy(r).wait()
        return c

    lax.fori_loop(0, R, start_gather, 0)
    lax.fori_loop(0, R, wait_gather, 0)

    x = jnp.concatenate([xbuf[pl.ds(s, R, stride=nseg), :] for s in range(nseg)], axis=-1).astype(BF16)
    a = jnp.dot(x, w1_ref[...].astype(BF16), preferred_element_type=F32)
    b = jnp.dot(x, w3_ref[...].astype(BF16), preferred_element_type=F32)
    hid = (a * jax.nn.sigmoid(a) * b).astype(BF16)
    y = jnp.dot(hid, w2_ref[...].astype(BF16), preferred_element_type=F32)
    for s in range(nseg):
        ybuf[pl.ds(s, R, stride=nseg), :] = y[:, s * LANES:(s + 1) * LANES]

    def start_scatter(r, c):
        @pl.when(dst_ref[0, 0, r] >= 0)
        def _():
            scatter_copy(r).start()
        return c

    def wait_scatter(r, c):
        @pl.when(dst_ref[0, 0, r] >= 0)
        def _():
            scatter_copy(r).wait()
        return c

    lax.fori_loop(0, R, start_scatter, 0)
    lax.fori_loop(0, R, wait_scatter, 0)


def _moe(blk_e, src3, dst3, hn_rows, w1, w3, w2, n_assign):
    n_blk = blk_e.shape[0]
    _, D, DE = w1.shape
    nseg = D // LANES
    R = MOE_BLOCK
    smem_blk = pl.BlockSpec((1, 1, R), lambda b, be: (b, 0, 0), memory_space=pltpu.SMEM)
    grid_spec = pltpu.PrefetchScalarGridSpec(
        num_scalar_prefetch=1, grid=(n_blk,),
        in_specs=[smem_blk, smem_blk,
                  pl.BlockSpec(memory_space=pl.ANY),
                  pl.BlockSpec((None, D, DE), lambda b, be: (be[b], 0, 0)),
                  pl.BlockSpec((None, D, DE), lambda b, be: (be[b], 0, 0)),
                  pl.BlockSpec((None, DE, D), lambda b, be: (be[b], 0, 0))],
        out_specs=pl.BlockSpec(memory_space=pl.ANY),
        scratch_shapes=[pltpu.VMEM((R * nseg, LANES), F32), pltpu.VMEM((R * nseg, LANES), F32),
                        pltpu.SemaphoreType.DMA, pltpu.SemaphoreType.DMA])
    return pl.pallas_call(
        functools.partial(_moe_body, nseg=nseg), grid_spec=grid_spec,
        out_shape=jax.ShapeDtypeStruct((n_assign * nseg, LANES), F32),
        compiler_params=pltpu.CompilerParams(dimension_semantics=("arbitrary",),
                                             vmem_limit_bytes=VMEM_LIMIT, has_side_effects=True),
        name="moe")(blk_e, src3, dst3, hn_rows, w1, w3, w2)


def _combine_body(h_ref, y_ref, rf_ref, o_ref, *, tm, nseg):
    g0 = rf_ref[:, 0:1]
    g1 = rf_ref[:, 1:2]
    stride = TOPK_IN * nseg
    y0 = jnp.concatenate([y_ref[pl.ds(s, tm, stride=stride), :] for s in range(nseg)], axis=-1)
    y1 = jnp.concatenate([y_ref[pl.ds(nseg + s, tm, stride=stride), :] for s in range(nseg)], axis=-1)
    o_ref[...] = h_ref[...] + (y0 * g0 + y1 * g1)


def _combine(h2d, y_rows, rf, tm):
    T, D = h2d.shape
    nseg = D // LANES
    return pl.pallas_call(
        functools.partial(_combine_body, tm=tm, nseg=nseg), grid=(T // tm,),
        in_specs=[pl.BlockSpec((tm, D), lambda i: (i, 0)),
                  pl.BlockSpec((tm * TOPK_IN * nseg, LANES), lambda i: (i, 0)),
                  pl.BlockSpec((tm, LANES), lambda i: (i, 0))],
        out_specs=pl.BlockSpec((tm, D), lambda i: (i, 0)),
        out_shape=jax.ShapeDtypeStruct((T, D), F32),
        compiler_params=_params("parallel"), name="combine")(h2d, y_rows, rf)


def _dispatch_plan(expert, n_tok):
    A = n_tok * TOPK_IN
    flat_e = expert.reshape(A)
    order = jnp.argsort(flat_e)
    sorted_e = flat_e[order]
    counts = jnp.bincount(flat_e, length=N_EXPERTS)
    padded = (counts + MOE_BLOCK - 1) // MOE_BLOCK * MOE_BLOCK
    pad_end = jnp.cumsum(padded)
    pad_start = pad_end - padded
    start = jnp.cumsum(counts) - counts
    dest = pad_start[sorted_e] + jnp.arange(A) - start[sorted_e]
    n_blk = -(-A // MOE_BLOCK) + N_EXPERTS
    n_pad = n_blk * MOE_BLOCK
    src_tok = jnp.zeros((n_pad,), I32).at[dest].set((order // TOPK_IN).astype(I32))
    dst_slot = jnp.full((n_pad,), -1, I32).at[dest].set(order.astype(I32))
    blk_e = jnp.minimum(jnp.searchsorted(pad_end, jnp.arange(n_blk) * MOE_BLOCK, side='right'),
                        N_EXPERTS - 1).astype(I32)
    return blk_e, src_tok.reshape(n_blk, 1, MOE_BLOCK), dst_slot.reshape(n_blk, 1, MOE_BLOCK)


def _pad_lanes(a, lane0, rows=1):
    out = jnp.zeros((rows, LANES), a.dtype)
    return out.at[:, lane0:lane0 + a.shape[-1]].set(a.reshape(rows, -1))


def _layer(h2d, B, S, p):
    T, D = h2d.shape
    tile = lambda a, n: jnp.tile(a.reshape(1, -1), (1, n))

    offs = np.cumsum([0, GROUP_W, GROUP_W, GROUP_W, N_HEADS, GROUP_W, GROUP_W,
                      GROUP_W, GROUP_W, GROUP_W, N_HEADS, N_HEADS, GROUP_W, GROUP_W])
    seg = lambda k: p['w_in'][:, offs[k]:offs[k + 1]]
    wb = jnp.concatenate([seg(0), seg(1), seg(2), seg(4), seg(5), seg(6), seg(7), seg(8), seg(11), seg(12)],
                         axis=1).astype(BF16)
    ws = jnp.zeros((D, LANES), F32)
    ws = ws.at[:, LANE_FOX:LANE_FOX + N_HEADS].set(seg(3))
    ws = ws.at[:, LANE_DECAY:LANE_DECAY + N_HEADS].set(seg(9))
    ws = ws.at[:, LANE_BETA:LANE_BETA + N_HEADS].set(seg(10)).astype(BF16)
    gate_prm = jnp.concatenate([_pad_lanes(p['fox_f_bias'], LANE_FOX), _pad_lanes(p['gdn_dt_bias'], LANE_DECAY),
                                _pad_lanes(p['gdn_a_log'], LANE_DECAY), jnp.zeros((SUBLANES - 3, LANES), F32)], axis=0)

    big, small = _inproj(h2d, p['attn_norm_g'].reshape(1, D), wb, ws, tm=min(512, T))
    big3 = big.reshape(B, S, N_BIG_COLS * GROUP_W)
    ts = min(512, S)
    gcol, grow = _gates(small.reshape(B, S, LANES), gate_prm, ts)

    tq = min(256, S)
    ya = _fox(big3, gcol, tile(p['fox_qn_g'], N_HEADS), tile(p['fox_kn_g'], N_HEADS),
              p['fox_out_g'].reshape(1, GROUP_W), tq)

    bst = jnp.repeat(p['gmlp_bs'].T, HEAD_DIM, axis=1)
    yb = _gmlp(big, p['gmlp_ln_g'].reshape(1, -1), p['gmlp_ln_b'].reshape(1, -1), p['gmlp_ws'], bst,
               p['gmlp_out_g'].reshape(1, -1), tm=min(512, T))

    conv_w3 = p['gdn_conv_w'].reshape(CONV_K, 3, GROUP_W).transpose(1, 0, 2)
    qn, kn, vv = _gdn_prep(big3, conv_w3, ts)
    grow4 = grow.reshape(B, 2 * SUBLANES, S // CHUNK, CHUNK).transpose(0, 2, 1, 3)
    yc = _gdn(qn, kn, vv, big3, gcol, grow4, p['gdn_norm_g'].reshape(1, HEAD_DIM), ts)

    wbd = jax.scipy.linalg.block_diag(*[p['pool_w'][g] for g in range(len(POOL_WINDOWS))]).astype(BF16)
    yd = _pool(big3, wbd, p['pool_scale'].reshape(1, -1), p['pool_out_g'].reshape(1, -1), ts)

    rw = jnp.zeros((D, LANES), F32).at[:, :N_GROUPS].set(p['router_g_w'])
    rw = rw.at[:, N_GROUPS:N_GROUPS + N_EXPERTS].set(p['router_e_w'])
    rb = jnp.zeros((1, LANES), F32).at[0, :N_GROUPS].set(p['router_g_b'])
    rb = rb.at[0, N_GROUPS:N_GROUPS + N_EXPERTS].set(p['router_e_b'])
    flat = lambda a: a.reshape(T, GROUP_W)
    h_new, hn_rows, ri, rf = _outproj(flat(ya), yb, flat(yc), flat(yd), h2d, p['w_out'].astype(BF16),
                                      p['ffn_norm_g'].reshape(1, D), rw, rb, tm=min(512, T))

    blk_e, src3, dst3 = _dispatch_plan(ri[:, :TOPK_IN], T)
    y_rows = _moe(blk_e, src3, dst3, hn_rows, p['moe_w1'], p['moe_w3'], p['moe_w2'], T * TOPK_IN)
    return _combine(h_new, y_rows, rf, tm=min(512, T))


def kernel(x, attn_norm_g, w_in, w_out, fox_f_bias, fox_qn_g, fox_kn_g, fox_out_g, gmlp_ln_g, gmlp_ln_b, gmlp_ws, gmlp_bs, gmlp_out_g, gdn_conv_w, gdn_a_log, gdn_dt_bias, gdn_norm_g, pool_w, pool_scale, pool_out_g, ffn_norm_g, router_g_w, router_g_b, router_e_w, router_e_b, moe_w1, moe_w3, moe_w2):
    B, S, D = x.shape
    names = ('attn_norm_g', 'w_in', 'w_out', 'fox_f_bias', 'fox_qn_g', 'fox_kn_g', 'fox_out_g', 'gmlp_ln_g',
             'gmlp_ln_b', 'gmlp_ws', 'gmlp_bs', 'gmlp_out_g', 'gdn_conv_w', 'gdn_a_log', 'gdn_dt_bias',
             'gdn_norm_g', 'pool_w', 'pool_scale', 'pool_out_g', 'ffn_norm_g', 'router_g_w', 'router_g_b',
             'router_e_w', 'router_e_b', 'moe_w1', 'moe_w3', 'moe_w2')
    vals = (attn_norm_g, w_in, w_out, fox_f_bias, fox_qn_g, fox_kn_g, fox_out_g, gmlp_ln_g, gmlp_ln_b, gmlp_ws,
            gmlp_bs, gmlp_out_g, gdn_conv_w, gdn_a_log, gdn_dt_bias, gdn_norm_g, pool_w, pool_scale, pool_out_g,
            ffn_norm_g, router_g_w, router_g_b, router_e_w, router_e_b, moe_w1, moe_w3, moe_w2)
    h = x.reshape(B * S, D)
    for l in range(w_in.shape[0]):
        h = _layer(h, B, S, {n: v[l] for n, v in zip(names, vals)})
    return h.reshape(B, S, D)
```

```python
import functools

import jax
import jax.numpy as jnp
import numpy as np
from jax import lax
from jax.experimental import pallas as pl
from jax.experimental.pallas import tpu as pltpu

F32 = jnp.float32
BF16 = jnp.bfloat16
I32 = jnp.int32

EPS = 1e-6
HEAD_DIM = 64
GROUP_W = 256
N_HEADS = GROUP_W // HEAD_DIM
CHUNK = 64
GMLP_LEN = 128
CONV_K = 4
POOL_WINDOWS = (2, 4, 8, 16)
N_GROUPS = 4
EXPERTS_PER_GROUP = 8
N_EXPERTS = N_GROUPS * EXPERTS_PER_GROUP
TOPK_IN = 2
MOE_BLOCK = 128
LANES = 128
SUBLANES = 8
VMEM_LIMIT = 56 * 1024 * 1024

COL_FQ, COL_FK, COL_FV, COL_GU, COL_GV, COL_DQ, COL_DK, COL_DV, COL_DG, COL_PZ = range(10)
N_BIG_COLS = 10
LANE_FOX, LANE_DECAY, LANE_BETA = 0, 4, 8


def _params(*sem):
    return pltpu.CompilerParams(dimension_semantics=sem, vmem_limit_bytes=VMEM_LIMIT)


def _head_ones():
    r = lax.broadcasted_iota(I32, (GROUP_W, GROUP_W), 0) // HEAD_DIM
    c = lax.broadcasted_iota(I32, (GROUP_W, GROUP_W), 1) // HEAD_DIM
    return (r == c).astype(BF16)


def _head_sums(x, ones_bd):
    hi = x.astype(BF16)
    lo = (x - hi.astype(F32)).astype(BF16)
    return (jnp.dot(hi, ones_bd, preferred_element_type=F32)
            + jnp.dot(lo, ones_bd, preferred_element_type=F32))


def _rms(x, g):
    return x * lax.rsqrt(jnp.mean(x * x, axis=-1, keepdims=True) + EPS) * g


def _mm(a, b):
    return jnp.dot(a.astype(BF16), b.astype(BF16), preferred_element_type=F32)


def _mm_nt(a, b):
    return lax.dot_general(a.astype(BF16), b.astype(BF16), (((1,), (1,)), ((), ())),
                           preferred_element_type=F32)


def _mm_tn(a, b):
    return lax.dot_general(a.astype(BF16), b.astype(BF16), (((0,), (0,)), ((), ())),
                           preferred_element_type=F32)


def _split(a):
    hi = a.astype(BF16)
    return hi, (a - hi.astype(F32)).astype(BF16)


def _mm3(a, b):
    ah, al = _split(a)
    bh, bl = _split(b)
    d = functools.partial(jnp.dot, preferred_element_type=F32)
    return d(ah, bh) + (d(ah, bl) + d(al, bh))


def _inproj_body(x_ref, g_ref, wb_ref, ws_ref, big_ref, small_ref):
    xn = _rms(x_ref[...], g_ref[...]).astype(BF16)
    big_ref[...] = jnp.dot(xn, wb_ref[...], preferred_element_type=F32)
    small_ref[...] = jnp.dot(xn, ws_ref[...], preferred_element_type=F32)


def _inproj(x2d, g, wb, ws, tm):
    T, D = x2d.shape
    nb = wb.shape[1]
    return pl.pallas_call(
        _inproj_body, grid=(T // tm,),
        in_specs=[pl.BlockSpec((tm, D), lambda i: (i, 0)),
                  pl.BlockSpec((1, D), lambda i: (0, 0)),
                  pl.BlockSpec((D, nb), lambda i: (0, 0)),
                  pl.BlockSpec((D, LANES), lambda i: (0, 0))],
        out_specs=[pl.BlockSpec((tm, nb), lambda i: (i, 0)),
                   pl.BlockSpec((tm, LANES), lambda i: (i, 0))],
        out_shape=[jax.ShapeDtypeStruct((T, nb), F32), jax.ShapeDtypeStruct((T, LANES), F32)],
        compiler_params=_params("parallel"), name="inproj")(x2d, g, wb, ws)


def _gates_body(sm_ref, p_ref, col_ref, row_ref, carry_ref, *, ts):
    @pl.when(pl.program_id(1) == 0)
    def _():
        carry_ref[...] = jnp.zeros_like(carry_ref)

    x = sm_ref[...]
    lane = lax.broadcasted_iota(I32, (ts, LANES), 1)
    is_fox = lane < LANE_DECAY
    is_dec = (lane >= LANE_DECAY) & (lane < LANE_BETA)
    is_beta = (lane >= LANE_BETA) & (lane < LANE_BETA + N_HEADS)
    logf = jax.nn.log_sigmoid(x + p_ref[0:1, :])
    g = -jnp.exp(p_ref[2:3, :]) * jax.nn.softplus(x + p_ref[1:2, :])
    beta = jax.nn.sigmoid(x)
    r = lax.broadcasted_iota(I32, (ts, ts), 0)
    c = lax.broadcasted_iota(I32, (ts, ts), 1)
    tri_full = (r >= c).astype(F32)
    tri_chunk = ((r >= c) & (r // CHUNK == c // CHUNK)).astype(F32)
    hp = functools.partial(jnp.dot, preferred_element_type=F32, precision=lax.Precision.HIGHEST)
    cf = hp(tri_full, jnp.where(is_fox, logf, 0.0)) + carry_ref[...]
    cg = hp(tri_chunk, jnp.where(is_dec, g, 0.0))
    carry_ref[...] = cf[ts - 1:ts, :]
    out = jnp.where(is_fox, cf, jnp.where(is_dec, cg, jnp.where(is_beta, beta, 0.0)))
    col_ref[...] = out
    row_ref[...] = out.T[:2 * SUBLANES, :]


def _gates(small3, prm, ts):
    B, S, _ = small3.shape
    return pl.pallas_call(
        functools.partial(_gates_body, ts=ts), grid=(B, S // ts),
        in_specs=[pl.BlockSpec((None, ts, LANES), lambda b, j: (b, j, 0)),
                  pl.BlockSpec((SUBLANES, LANES), lambda b, j: (0, 0))],
        out_specs=[pl.BlockSpec((None, ts, LANES), lambda b, j: (b, j, 0)),
                   pl.BlockSpec((None, 2 * SUBLANES, ts), lambda b, j: (b, 0, j))],
        out_shape=[jax.ShapeDtypeStruct((B, S, LANES), F32),
                   jax.ShapeDtypeStruct((B, 2 * SUBLANES, S), F32)],
        scratch_shapes=[pltpu.VMEM((1, LANES), F32)],
        compiler_params=_params("parallel", "arbitrary"), name="gates")(small3, prm)


def _fox_body(q_ref, k_ref, v_ref, ccol_ref, qg_ref, kg_ref, og_ref, o_ref,
              kn_scr, vt_scr, q_scr, m_scr, l_scr, acc_scr, *, tq, nk):
    i = pl.program_id(1)
    ones_bd = _head_ones()
    lane = lax.broadcasted_iota(I32, (tq, LANES), 1)
    log2e = 1.0 / np.log(2.0)

    def head_norm(x, g):
        ss = _head_sums(x * x, ones_bd)
        return x * lax.rsqrt(ss * (1.0 / HEAD_DIM) + EPS) * g

    def head_tile(x, h, extra):
        pair = x[:, (h // 2) * LANES:(h // 2 + 1) * LANES]
        if h % 2:
            pair = pltpu.roll(pair, HEAD_DIM, 1)
        return jnp.where(lane < HEAD_DIM, pair, extra).astype(BF16)

    @pl.when(i == 0)
    def _():
        for c in range(nk):
            rows = slice(c * tq, (c + 1) * tq)
            kc = head_norm(k_ref[rows, :], kg_ref[...])
            vt = v_ref[rows, :].T.astype(BF16)
            cc = ccol_ref[rows, :] * (-log2e)
            for h in range(N_HEADS):
                cj = cc[:, LANE_FOX + h:LANE_FOX + h + 1]
                hi = cj.astype(BF16).astype(F32)
                mid = (cj - hi).astype(BF16).astype(F32)
                lo = (cj - hi - mid).astype(BF16).astype(F32)
                extra = jnp.where(lane == HEAD_DIM, hi, jnp.where(lane == HEAD_DIM + 1, mid,
                                  jnp.where(lane == HEAD_DIM + 2, lo, 0.0)))
                kn_scr[h, c] = head_tile(kc, h, extra)
                vt_scr[h, c] = vt[h * HEAD_DIM:(h + 1) * HEAD_DIM, :]

    qn = head_norm(q_ref[...], qg_ref[...]) * (HEAD_DIM ** -0.5 * log2e)
    q_ones = jnp.where(lane < HEAD_DIM + 3, 1.0, 0.0)
    for h in range(N_HEADS):
        q_scr[h] = head_tile(qn, h, q_ones)
    m_scr[...] = jnp.full_like(m_scr, -jnp.inf)
    l_scr[...] = jnp.zeros_like(l_scr)
    acc_scr[...] = jnp.zeros_like(acc_scr)
    causal = (lax.broadcasted_iota(I32, (tq, tq), 0) <= lax.broadcasted_iota(I32, (tq, tq), 1))

    def step(j, masked):
        heads = range(N_HEADS)
        s = [lax.dot_general(kn_scr[h, j], q_scr[h], (((1,), (1,)), ((), ())),
                             preferred_element_type=F32) for h in heads]
        if masked:
            s = [jnp.where(causal, s[h], -jnp.inf) for h in heads]
        p, alpha = [], []
        for h in heads:
            m_old = m_scr[h]
            m_new = jnp.maximum(m_old, jnp.max(s[h], axis=0, keepdims=True))
            alpha.append(jnp.exp2(m_old - m_new))
            ph = jnp.exp2(s[h] - m_new)
            l_scr[h] = alpha[h] * l_scr[h] + jnp.sum(ph, axis=0, keepdims=True)
            m_scr[h] = m_new
            p.append(ph.astype(BF16))
        pv = [jnp.dot(vt_scr[h, j], p[h], preferred_element_type=F32) for h in heads]
        for h in heads:
            acc_scr[h] = alpha[h] * acc_scr[h] + pv[h]

    def unmasked(j, c):
        step(j, False)
        return c

    lax.fori_loop(0, i, unmasked, 0)
    step(i, True)
    o_t = jnp.concatenate([acc_scr[h] / l_scr[h] for h in range(N_HEADS)], axis=0)
    o_ref[...] = _rms(o_t.T, og_ref[...])


def _fox(big3, gcol, qg, kg, og, tq):
    B, S, _ = big3.shape
    nk = S // tq
    row = pl.BlockSpec((1, GROUP_W), lambda b, i: (0, 0))
    return pl.pallas_call(
        functools.partial(_fox_body, tq=tq, nk=nk), grid=(B, nk),
        in_specs=[pl.BlockSpec((None, tq, GROUP_W), lambda b, i: (b, i, COL_FQ)),
                  pl.BlockSpec((None, S, GROUP_W), lambda b, i: (b, 0, COL_FK)),
                  pl.BlockSpec((None, S, GROUP_W), lambda b, i: (b, 0, COL_FV)),
                  pl.BlockSpec((None, S, LANES), lambda b, i: (b, 0, 0)),
                  row, row, row],
        out_specs=pl.BlockSpec((None, tq, GROUP_W), lambda b, i: (b, i, 0)),
        out_shape=jax.ShapeDtypeStruct((B, S, GROUP_W), F32),
        scratch_shapes=[pltpu.VMEM((N_HEADS, nk, tq, LANES), BF16),
                        pltpu.VMEM((N_HEADS, nk, HEAD_DIM, tq), BF16),
                        pltpu.VMEM((N_HEADS, tq, LANES), BF16),
                        pltpu.VMEM((N_HEADS, 1, tq), F32),
                        pltpu.VMEM((N_HEADS, 1, tq), F32),
                        pltpu.VMEM((N_HEADS, HEAD_DIM, tq), F32)],
        compiler_params=_params("parallel", "arbitrary"), name="fox")(
            big3, big3, big3, gcol, qg, kg, og)


def _gelu(x):
    return 0.5 * x * (1.0 + lax.erf(x * (2.0 ** -0.5)))


def _gmlp_body(u_ref, v_ref, lg_ref, lb_ref, ws_ref, bst_ref, og_ref, o_ref, *, nwin):
    L = GMLP_LEN
    r = lax.broadcasted_iota(I32, (L, L), 0) // CHUNK
    c = lax.broadcasted_iota(I32, (L, L), 1) // CHUNK
    mask = r >= c
    ws = [jnp.where(mask, ws_ref[h], 0.0).astype(BF16) for h in range(N_HEADS)]
    for n in range(nwin):
        u = _gelu(u_ref[n * L:(n + 1) * L, :])
        v = _gelu(v_ref[n * L:(n + 1) * L, :])
        mu = jnp.mean(v, axis=-1, keepdims=True)
        vc = v - mu
        var = jnp.mean(vc * vc, axis=-1, keepdims=True)
        vn = (vc * lax.rsqrt(var + EPS) * lg_ref[...] + lb_ref[...]).astype(BF16)
        mixed = jnp.concatenate(
            [jnp.dot(ws[h], vn[:, h * HEAD_DIM:(h + 1) * HEAD_DIM], preferred_element_type=F32)
             for h in range(N_HEADS)], axis=-1) + bst_ref[...]
        o_ref[n * L:(n + 1) * L, :] = _rms(u * mixed, og_ref[...])


def _gmlp(big, lg, lb, ws, bst, og, tm):
    T = big.shape[0]
    row = pl.BlockSpec((1, GROUP_W), lambda i: (0, 0))
    return pl.pallas_call(
        functools.partial(_gmlp_body, nwin=tm // GMLP_LEN), grid=(T // tm,),
        in_specs=[pl.BlockSpec((tm, GROUP_W), lambda i: (i, COL_GU)),
                  pl.BlockSpec((tm, GROUP_W), lambda i: (i, COL_GV)),
                  row, row,
                  pl.BlockSpec((N_HEADS, GMLP_LEN, GMLP_LEN), lambda i: (0, 0, 0)),
                  pl.BlockSpec((GMLP_LEN, GROUP_W), lambda i: (0, 0)),
                  row],
        out_specs=pl.BlockSpec((tm, GROUP_W), lambda i: (i, 0)),
        out_shape=jax.ShapeDtypeStruct((T, GROUP_W), F32),
        compiler_params=_params("parallel"), name="gmlp")(big, big, lg, lb, ws, bst, og)


def _gdn_prep_body(q_ref, k_ref, v_ref, hq_ref, hk_ref, hv_ref, w_ref, qo_ref, ko_ref, vo_ref, *, ts):
    first = pl.program_id(1) == 0
    ones_bd = _head_ones()

    def conv(x_ref, halo_ref, w):
        halo = jnp.where(first, 0.0, halo_ref[...])
        xx = jnp.concatenate([halo, x_ref[...]], axis=0)
        y = w[CONV_K - 1:CONV_K, :] * xx[SUBLANES:, :]
        for j in range(CONV_K - 1):
            y = y + w[j:j + 1, :] * pltpu.roll(xx, CONV_K - 1 - j, 0)[SUBLANES:, :]
        return y * jax.nn.sigmoid(y)

    def l2(t):
        return t * lax.rsqrt(_head_sums(t * t, ones_bd) + EPS)

    qo_ref[...] = l2(conv(q_ref, hq_ref, w_ref[0])) * (HEAD_DIM ** -0.5)
    ko_ref[...] = l2(conv(k_ref, hk_ref, w_ref[1]))
    vo_ref[...] = conv(v_ref, hv_ref, w_ref[2])


def _gdn_prep(big3, conv_w3, ts):
    B, S, _ = big3.shape
    hb = ts // SUBLANES
    blk = lambda col: pl.BlockSpec((None, ts, GROUP_W), lambda b, i: (b, i, col))
    halo = lambda col: pl.BlockSpec((None, SUBLANES, GROUP_W),
                                    lambda b, i: (b, jnp.maximum(i * hb - 1, 0), col))
    out = pl.BlockSpec((None, ts, GROUP_W), lambda b, i: (b, i, 0))
    shp = jax.ShapeDtypeStruct((B, S, GROUP_W), F32)
    return pl.pallas_call(
        functools.partial(_gdn_prep_body, ts=ts), grid=(B, S // ts),
        in_specs=[blk(COL_DQ), blk(COL_DK), blk(COL_DV), halo(COL_DQ), halo(COL_DK), halo(COL_DV),
                  pl.BlockSpec((3, CONV_K, GROUP_W), lambda b, i: (0, 0, 0))],
        out_specs=[out, out, out], out_shape=[shp, shp, shp],
        compiler_params=_params("parallel", "parallel"), name="gdn_prep")(
            big3, big3, big3, big3, big3, big3, conv_w3)


def _gdn_body(q_ref, k_ref, v_ref, gate_ref, gcol_ref, grow_ref, ng_ref, o_ref,
              s_scr, u_scr, wq_scr, a_scr, kd_scr, dl_scr, t_scr, p_scr, rhs_scr, *, nchunk):
    C = CHUNK

    @pl.when(pl.program_id(1) == 0)
    def _():
        s_scr[...] = jnp.zeros_like(s_scr)

    r = lax.broadcasted_iota(I32, (C, C), 0)
    c = lax.broadcasted_iota(I32, (C, C), 1)
    tri = r >= c
    strict = r > c
    eye = (r == c).astype(F32)

    items = [(n, h) for n in range(nchunk) for h in range(N_HEADS)]
    mmb = functools.partial(jnp.dot, preferred_element_type=F32)
    for it, (n, h) in enumerate(items):
        rows = slice(n * C, (n + 1) * C)
        sl = slice(h * HEAD_DIM, (h + 1) * HEAD_DIM)
        q, k, v = q_ref[rows, sl], k_ref[rows, sl], v_ref[rows, sl]
        gc = gcol_ref[rows, LANE_DECAY + h:LANE_DECAY + h + 1]
        gr = grow_ref[n, LANE_DECAY + h:LANE_DECAY + h + 1, :]
        beta = gcol_ref[rows, LANE_BETA + h:LANE_BETA + h + 1]
        decay = jnp.exp(jnp.where(tri, gc - gr, -jnp.inf))
        kb = k * beta
        kk = _mm_nt(jnp.concatenate([kb, q], axis=0), k)
        x = jnp.where(strict, -(kk[:C] * decay), 0.0)
        t_scr[it] = eye + x
        p_scr[it] = x.astype(BF16)
        eg = jnp.exp(gc)
        g_last = gc[C - 1:C, :]
        rhs_scr[it] = jnp.concatenate([v * beta, kb * eg], axis=1).astype(BF16)
        wq_scr[n, h, C:, :] = (q * eg).astype(BF16)
        a_scr[n, h] = jnp.where(tri, kk[C:] * decay, 0.0).astype(BF16)
        kd_scr[n, h] = (k * jnp.exp(g_last - gc)).astype(BF16)
        dl_scr[n, h] = jnp.broadcast_to(jnp.exp(g_last), (1, HEAD_DIM))
    for _ in range(5):
        for it in range(len(items)):
            p = p_scr[it]
            p_scr[it] = mmb(p, p).astype(BF16)
        for it in range(len(items)):
            t = t_scr[it]
            t_scr[it] = t + mmb(t.astype(BF16), p_scr[it])
    for it, (n, h) in enumerate(items):
        uw = mmb(t_scr[it].astype(BF16), rhs_scr[it])
        u_scr[n, h] = uw[:, :HEAD_DIM]
        wq_scr[n, h, :C, :] = uw[:, HEAD_DIM:].astype(BF16)

    heads = range(N_HEADS)
    state = [s_scr[h] for h in heads]
    for n in range(nchunk):
        rows = slice(n * C, (n + 1) * C)
        ws = [mmb(wq_scr[n, h], state[h].astype(BF16)) for h in heads]
        vb = [(u_scr[n, h] - ws[h][:C]).astype(BF16) for h in heads]
        o = [ws[h][C:] + mmb(a_scr[n, h], vb[h]) for h in heads]
        state = [state[h] * dl_scr[n, h] + lax.dot_general(
            kd_scr[n, h], vb[h], (((0,), (0,)), ((), ())), preferred_element_type=F32) for h in heads]
        gate = gate_ref[rows, :]
        y = jnp.concatenate([_rms(o[h], ng_ref[...]) for h in heads], axis=-1)
        o_ref[rows, :] = y * (gate * jax.nn.sigmoid(gate))
    for h in heads:
        s_scr[h] = state[h]


def _gdn(qn, kn, vv, big3, gcol, grow4, ng, ts):
    B, S, _ = qn.shape
    nchunk = ts // CHUNK
    blk = pl.BlockSpec((None, ts, GROUP_W), lambda b, i: (b, i, 0))
    return pl.pallas_call(
        functools.partial(_gdn_body, nchunk=nchunk), grid=(B, S // ts),
        in_specs=[blk, blk, blk,
                  pl.BlockSpec((None, ts, GROUP_W), lambda b, i: (b, i, COL_DG)),
                  pl.BlockSpec((None, ts, LANES), lambda b, i: (b, i, 0)),
                  pl.BlockSpec((None, nchunk, 2 * SUBLANES, CHUNK), lambda b, i: (b, i, 0, 0)),
                  pl.BlockSpec((1, HEAD_DIM), lambda b, i: (0, 0))],
        out_specs=blk, out_shape=jax.ShapeDtypeStruct((B, S, GROUP_W), F32),
        scratch_shapes=[pltpu.VMEM((N_HEADS, HEAD_DIM, HEAD_DIM), F32),
                        pltpu.VMEM((nchunk, N_HEADS, CHUNK, HEAD_DIM), F32),
                        pltpu.VMEM((nchunk, N_HEADS, 2 * CHUNK, HEAD_DIM), BF16),
                        pltpu.VMEM((nchunk, N_HEADS, CHUNK, CHUNK), BF16),
                        pltpu.VMEM((nchunk, N_HEADS, CHUNK, HEAD_DIM), BF16),
                        pltpu.VMEM((nchunk, N_HEADS, 1, HEAD_DIM), F32),
                        pltpu.VMEM((nchunk * N_HEADS, CHUNK, CHUNK), F32),
                        pltpu.VMEM((nchunk * N_HEADS, CHUNK, CHUNK), BF16),
                        pltpu.VMEM((nchunk * N_HEADS, CHUNK, 2 * HEAD_DIM), BF16)],
        compiler_params=_params("parallel", "arbitrary"), name="gdn")(
            qn, kn, vv, big3, gcol, grow4, ng)


def _pool_body(z_ref, halo_ref, w_ref, sc_ref, og_ref, o_ref, *, ts):
    i = pl.program_id(1)
    hr = 2 * SUBLANES
    z = z_ref[...]
    halo = jnp.where(i == 0, 0.0, halo_ref[...])
    s1 = jnp.concatenate([halo, z], axis=0)
    s2 = s1 + pltpu.roll(s1, 1, 0)
    s4 = s2 + pltpu.roll(s2, 2, 0)
    s8 = s4 + pltpu.roll(s4, 4, 0)
    s16 = s8 + pltpu.roll(s8, 8, 0)
    grp = lax.broadcasted_iota(I32, (ts, GROUP_W), 1) // (GROUP_W // len(POOL_WINDOWS))
    t = lax.broadcasted_iota(I32, (ts, GROUP_W), 0) + i * ts
    total = jnp.where(grp == 0, s2[hr:], jnp.where(grp == 1, s4[hr:], jnp.where(grp == 2, s8[hr:], s16[hr:])))
    win = jnp.where(grp == 0, POOL_WINDOWS[0], jnp.where(grp == 1, POOL_WINDOWS[1],
                    jnp.where(grp == 2, POOL_WINDOWS[2], POOL_WINDOWS[3])))
    pooled = total / jnp.minimum(t + 1, win).astype(F32)
    y = _mm(pooled - z, w_ref[...]) * sc_ref[...]
    o_ref[...] = _rms(y, og_ref[...])


def _pool(big3, wbd, sc, og, ts):
    B, S, _ = big3.shape
    hr = 2 * SUBLANES
    hb = ts // hr
    row = pl.BlockSpec((1, GROUP_W), lambda b, i: (0, 0))
    return pl.pallas_call(
        functools.partial(_pool_body, ts=ts), grid=(B, S // ts),
        in_specs=[pl.BlockSpec((None, ts, GROUP_W), lambda b, i: (b, i, COL_PZ)),
                  pl.BlockSpec((None, hr, GROUP_W), lambda b, i: (b, jnp.maximum(i * hb - 1, 0), COL_PZ)),
                  pl.BlockSpec((GROUP_W, GROUP_W), lambda b, i: (0, 0)), row, row],
        out_specs=pl.BlockSpec((None, ts, GROUP_W), lambda b, i: (b, i, 0)),
        out_shape=jax.ShapeDtypeStruct((B, S, GROUP_W), F32),
        compiler_params=_params("parallel", "parallel"), name="pool")(big3, big3, wbd, sc, og)


def _outproj_body(ya_ref, yb_ref, yc_ref, yd_ref, h_ref, wo_ref, g_ref, rw_ref, rb_ref,
                  hnew_ref, hn_ref, ri_ref, rf_ref, *, tm, d):
    y = jnp.concatenate([ya_ref[...], yb_ref[...], yc_ref[...], yd_ref[...]], axis=-1).astype(BF16)
    h_new = h_ref[...] + jnp.dot(y, wo_ref[...], preferred_element_type=F32)
    hnew_ref[...] = h_new
    hn = _rms(h_new, g_ref[...])
    nseg = d // LANES
    for s in range(nseg):
        hn_ref[pl.ds(s, tm, stride=nseg), :] = hn[:, s * LANES:(s + 1) * LANES]

    logits = jnp.dot(hn, rw_ref[...], preferred_element_type=F32,
                     precision=lax.Precision.HIGHEST) + rb_ref[...]
    lane = lax.broadcasted_iota(I32, (tm, LANES), 1)
    neg = -jnp.inf
    big_lane = LANES

    def masked_top(vals, mask):
        v = jnp.where(mask, vals, neg)
        mx = jnp.max(v, axis=-1, keepdims=True)
        idx = jnp.min(jnp.where(mask & (v == mx), lane, big_lane), axis=-1, keepdims=True)
        return v, mx, idx

    gmask = lane < N_GROUPS
    gv, gmx, gidx = masked_top(logits, gmask)
    g_top = 1.0 / jnp.sum(jnp.where(gmask, jnp.exp(gv - gmx), 0.0), axis=-1, keepdims=True)
    lo = N_GROUPS + gidx * EXPERTS_PER_GROUP
    emask = (lane >= lo) & (lane < lo + EXPERTS_PER_GROUP)
    ev, emx, eidx1 = masked_top(logits, emask)
    esum = jnp.sum(jnp.where(emask, jnp.exp(ev - emx), 0.0), axis=-1, keepdims=True)
    p1 = 1.0 / esum
    _, emx2, eidx2 = masked_top(logits, emask & (lane != eidx1))
    p2 = jnp.exp(emx2 - emx) / esum
    denom = p1 + p2
    ri_ref[...] = jnp.where(lane == 0, eidx1 - N_GROUPS, jnp.where(lane == 1, eidx2 - N_GROUPS, 0))
    rf_ref[...] = jnp.where(lane == 0, g_top * p1 / denom, jnp.where(lane == 1, g_top * p2 / denom, 0.0))


def _outproj(ya, yb, yc, yd, h2d, wo, g, rw, rb, tm):
    T, D = h2d.shape
    nseg = D // LANES
    yblk = pl.BlockSpec((tm, GROUP_W), lambda i: (i, 0))
    return pl.pallas_call(
        functools.partial(_outproj_body, tm=tm, d=D), grid=(T // tm,),
        in_specs=[yblk, yblk, yblk, yblk,
                  pl.BlockSpec((tm, D), lambda i: (i, 0)),
                  pl.BlockSpec((D, D), lambda i: (0, 0)),
                  pl.BlockSpec((1, D), lambda i: (0, 0)),
                  pl.BlockSpec((D, LANES), lambda i: (0, 0)),
                  pl.BlockSpec((1, LANES), lambda i: (0, 0))],
        out_specs=[pl.BlockSpec((tm, D), lambda i: (i, 0)),
                   pl.BlockSpec((tm * nseg, LANES), lambda i: (i, 0)),
                   pl.BlockSpec((tm, LANES), lambda i: (i, 0)),
                   pl.BlockSpec((tm, LANES), lambda i: (i, 0))],
        out_shape=[jax.ShapeDtypeStruct((T, D), F32),
                   jax.ShapeDtypeStruct((T * nseg, LANES), F32),
                   jax.ShapeDtypeStruct((T, LANES), I32),
                   jax.ShapeDtypeStruct((T, LANES), F32)],
        compiler_params=_params("parallel"), name="outproj")(ya, yb, yc, yd, h2d, wo, g, rw, rb)


def _rank_body(ri_ref, rank_ref, cnt_ref, carry_ref, *, tm):
    @pl.when(pl.program_id(0) == 0)
    def _():
        carry_ref[...] = jnp.zeros_like(carry_ref)

    lane = lax.broadcasted_iota(I32, (tm, LANES), 1)
    e = ri_ref[...]
    oh = [lane == e[:, s:s + 1] for s in range(TOPK_IN)]
    m = (oh[0].astype(F32) + oh[1].astype(F32)).astype(BF16)
    below = (lax.broadcasted_iota(I32, (tm, tm), 0) > lax.broadcasted_iota(I32, (tm, tm), 1)).astype(BF16)
    before = jnp.dot(below, m, preferred_element_type=F32) + carry_ref[...]
    rank = [jnp.sum(jnp.where(oh[s], before, 0.0), axis=-1, keepdims=True) for s in range(TOPK_IN)]
    rank_ref[...] = jnp.where(lane == 0, rank[0], jnp.where(lane == 1, rank[1], 0.0)).astype(I32)
    total = before[tm - 1:tm, :] + m[tm - 1:tm, :].astype(F32)
    carry_ref[...] = total
    cnt_ref[...] = jnp.broadcast_to(total, cnt_ref.shape)


def _rank(ri, tm):
    T = ri.shape[0]
    return pl.pallas_call(
        functools.partial(_rank_body, tm=tm), grid=(T // tm,),
        in_specs=[pl.BlockSpec((tm, LANES), lambda i: (i, 0))],
        out_specs=[pl.BlockSpec((tm, LANES), lambda i: (i, 0)),
                   pl.BlockSpec((SUBLANES, LANES), lambda i: (0, 0))],
        out_shape=[jax.ShapeDtypeStruct((T, LANES), I32), jax.ShapeDtypeStruct((SUBLANES, LANES), F32)],
        scratch_shapes=[pltpu.VMEM((1, LANES), F32)],
        compiler_params=_params("arbitrary"), name="rank")(ri)


def _dispatch_plan(expert, rank, counts):
    T = expert.shape[0]
    counts = counts.astype(I32)
    padded = (counts + MOE_BLOCK - 1) // MOE_BLOCK * MOE_BLOCK
    pad_end = jnp.cumsum(padded)
    pad_start = pad_end - padded
    onehot = expert[:, :, None] == jnp.arange(N_EXPERTS, dtype=I32)[None, None, :]
    dest = rank + jnp.sum(jnp.where(onehot, pad_start[None, None, :], 0), axis=-1)
    n_blk = -(-T * TOPK_IN // MOE_BLOCK) + N_EXPERTS
    blk_start = jnp.arange(n_blk, dtype=I32) * MOE_BLOCK
    blk_e = jnp.minimum(jnp.sum(pad_end[None, :] <= blk_start[:, None], axis=-1), N_EXPERTS - 1).astype(I32)
    n_used = (pad_end[-1] // MOE_BLOCK).astype(I32).reshape(1)
    return dest.astype(I32), blk_e, n_used, n_blk


def _row_tile(ref, r, nseg):
    return ref.at[pl.ds(pl.multiple_of(r * nseg, nseg), nseg), :]


def _dispatch_body(dst_ref, hn_ref, xz_hbm, x_hbm, sem, *, tm, nseg, unroll):
    del xz_hbm

    def issue(g, c):
        for u in range(unroll):
            r = g * unroll + u
            for s in range(TOPK_IN):
                pltpu.make_async_copy(_row_tile(hn_ref, r, nseg),
                                      _row_tile(x_hbm, dst_ref[0, 0, r * TOPK_IN + s], nseg), sem).start()
        return c

    lax.fori_loop(0, tm // unroll, issue, 0)
    for s in range(TOPK_IN):
        pltpu.make_async_copy(hn_ref, x_hbm.at[pl.ds(0, tm * nseg), :], sem).wait()


def _dispatch(dest3, hn_rows, n_pad, tm):
    nt = dest3.shape[0]
    nseg = hn_rows.shape[0] // (nt * tm)
    x_zero = jnp.zeros((n_pad * nseg, LANES), F32)
    return pl.pallas_call(
        functools.partial(_dispatch_body, tm=tm, nseg=nseg, unroll=8), grid=(nt,),
        in_specs=[pl.BlockSpec((1, 1, tm * TOPK_IN), lambda i: (i, 0, 0), memory_space=pltpu.SMEM),
                  pl.BlockSpec((tm * nseg, LANES), lambda i: (i, 0)),
                  pl.BlockSpec(memory_space=pl.ANY)],
        out_specs=pl.BlockSpec(memory_space=pl.ANY),
        out_shape=jax.ShapeDtypeStruct((n_pad * nseg, LANES), F32),
        scratch_shapes=[pltpu.SemaphoreType.DMA],
        input_output_aliases={2: 0},
        compiler_params=_params("arbitrary"), name="dispatch")(dest3, hn_rows, x_zero)


def _moe_body(blk_e_ref, n_used_ref, x_ref, w1_ref, w3_ref, w2_ref, y_ref, w1b, w3b, w2b, *, nseg):
    b = pl.program_id(0)
    R = MOE_BLOCK

    @pl.when(b < n_used_ref[0])
    def _():
        @pl.when((b == 0) | (blk_e_ref[b] != blk_e_ref[jnp.maximum(b - 1, 0)]))
        def _():
            w1b[...] = w1_ref[...].astype(BF16)
            w3b[...] = w3_ref[...].astype(BF16)
            w2b[...] = w2_ref[...].astype(BF16)

        x = jnp.concatenate([x_ref[pl.ds(s, R, stride=nseg), :] for s in range(nseg)], axis=-1).astype(BF16)
        a = jnp.dot(x, w1b[...], preferred_element_type=F32)
        g = jnp.dot(x, w3b[...], preferred_element_type=F32)
        hid = (a * jax.nn.sigmoid(a) * g).astype(BF16)
        y = jnp.dot(hid, w2b[...], preferred_element_type=F32)
        for s in range(nseg):
            y_ref[pl.ds(s, R, stride=nseg), :] = y[:, s * LANES:(s + 1) * LANES]

    @pl.when(b >= n_used_ref[0])
    def _():
        y_ref[...] = jnp.zeros_like(y_ref)


def _moe(blk_e, n_used, x_rows, w1, w3, w2):
    n_blk = blk_e.shape[0]
    _, D, DE = w1.shape
    nseg = D // LANES
    R = MOE_BLOCK
    rows = lambda b, be, nu: (jnp.minimum(b, nu[0] - 1), 0)
    wmap = lambda b, be, nu: (be[b], 0, 0)
    grid_spec = pltpu.PrefetchScalarGridSpec(
        num_scalar_prefetch=2, grid=(n_blk,),
        in_specs=[pl.BlockSpec((R * nseg, LANES), rows),
                  pl.BlockSpec((None, D, DE), wmap),
                  pl.BlockSpec((None, D, DE), wmap),
                  pl.BlockSpec((None, DE, D), wmap)],
        out_specs=pl.BlockSpec((R * nseg, LANES), lambda b, be, nu: (b, 0)),
        scratch_shapes=[pltpu.VMEM((D, DE), BF16), pltpu.VMEM((D, DE), BF16), pltpu.VMEM((DE, D), BF16)])
    return pl.pallas_call(
        functools.partial(_moe_body, nseg=nseg), grid_spec=grid_spec,
        out_shape=jax.ShapeDtypeStruct(x_rows.shape, F32),
        compiler_params=_params("arbitrary"), name="moe")(blk_e, n_used, x_rows, w1, w3, w2)


def _combine_body(dst_ref, h_ref, rf_ref, y_hbm, o_ref, ybuf, sem, *, tm, nseg, unroll):
    def issue(g, c):
        for u in range(unroll):
            a = g * unroll + u
            pltpu.make_async_copy(_row_tile(y_hbm, dst_ref[0, 0, a], nseg), _row_tile(ybuf, a, nseg), sem).start()
        return c

    lax.fori_loop(0, tm * TOPK_IN // unroll, issue, 0)
    pltpu.make_async_copy(y_hbm.at[pl.ds(0, tm * TOPK_IN * nseg), :], ybuf, sem).wait()
    stride = TOPK_IN * nseg
    acc = h_ref[...]
    for s in range(TOPK_IN):
        ys = jnp.concatenate([ybuf[pl.ds(s * nseg + k, tm, stride=stride), :] for k in range(nseg)], axis=-1)
        acc = acc + ys * rf_ref[:, s:s + 1]
    o_ref[...] = acc


def _combine(dest3, h2d, rf, y_rows, tm):
    T, D = h2d.shape
    nseg = D // LANES
    return pl.pallas_call(
        functools.partial(_combine_body, tm=tm, nseg=nseg, unroll=8), grid=(T // tm,),
        in_specs=[pl.BlockSpec((1, 1, tm * TOPK_IN), lambda i: (i, 0, 0), memory_space=pltpu.SMEM),
                  pl.BlockSpec((tm, D), lambda i: (i, 0)),
                  pl.BlockSpec((tm, LANES), lambda i: (i, 0)),
                  pl.BlockSpec(memory_space=pl.ANY)],
        out_specs=pl.BlockSpec((tm, D), lambda i: (i, 0)),
        out_shape=jax.ShapeDtypeStruct((T, D), F32),
        scratch_shapes=[pltpu.VMEM((tm * TOPK_IN * nseg, LANES), F32), pltpu.SemaphoreType.DMA],
        compiler_params=_params("arbitrary"), name="combine")(dest3, h2d, rf, y_rows)


def _pad_lanes(a, lane0, rows=1):
    out = jnp.zeros((rows, LANES), a.dtype)
    return out.at[:, lane0:lane0 + a.shape[-1]].set(a.reshape(rows, -1))


def _layer(h2d, B, S, p):
    T, D = h2d.shape
    tile = lambda a, n: jnp.tile(a.reshape(1, -1), (1, n))

    offs = np.cumsum([0, GROUP_W, GROUP_W, GROUP_W, N_HEADS, GROUP_W, GROUP_W,
                      GROUP_W, GROUP_W, GROUP_W, N_HEADS, N_HEADS, GROUP_W, GROUP_W])
    seg = lambda k: p['w_in'][:, offs[k]:offs[k + 1]]
    wb = jnp.concatenate([seg(0), seg(1), seg(2), seg(4), seg(5), seg(6), seg(7), seg(8), seg(11), seg(12)],
                         axis=1).astype(BF16)
    ws = jnp.zeros((D, LANES), F32)
    ws = ws.at[:, LANE_FOX:LANE_FOX + N_HEADS].set(seg(3))
    ws = ws.at[:, LANE_DECAY:LANE_DECAY + N_HEADS].set(seg(9))
    ws = ws.at[:, LANE_BETA:LANE_BETA + N_HEADS].set(seg(10)).astype(BF16)
    gate_prm = jnp.concatenate([_pad_lanes(p['fox_f_bias'], LANE_FOX), _pad_lanes(p['gdn_dt_bias'], LANE_DECAY),
                                _pad_lanes(p['gdn_a_log'], LANE_DECAY), jnp.zeros((SUBLANES - 3, LANES), F32)], axis=0)

    big, small = _inproj(h2d, p['attn_norm_g'].reshape(1, D), wb, ws, tm=min(512, T))
    big3 = big.reshape(B, S, N_BIG_COLS * GROUP_W)
    ts = min(512, S)
    gcol, grow = _gates(small.reshape(B, S, LANES), gate_prm, ts)

    tq = min(256, S)
    ya = _fox(big3, gcol, tile(p['fox_qn_g'], N_HEADS), tile(p['fox_kn_g'], N_HEADS),
              p['fox_out_g'].reshape(1, GROUP_W), tq)

    bst = jnp.repeat(p['gmlp_bs'].T, HEAD_DIM, axis=1)
    yb = _gmlp(big, p['gmlp_ln_g'].reshape(1, -1), p['gmlp_ln_b'].reshape(1, -1), p['gmlp_ws'], bst,
               p['gmlp_out_g'].reshape(1, -1), tm=min(512, T))

    conv_w3 = p['gdn_conv_w'].reshape(CONV_K, 3, GROUP_W).transpose(1, 0, 2)
    qn, kn, vv = _gdn_prep(big3, conv_w3, ts)
    grow4 = grow.reshape(B, 2 * SUBLANES, S // CHUNK, CHUNK).transpose(0, 2, 1, 3)
    yc = _gdn(qn, kn, vv, big3, gcol, grow4, p['gdn_norm_g'].reshape(1, HEAD_DIM), ts)

    wbd = jax.scipy.linalg.block_diag(*[p['pool_w'][g] for g in range(len(POOL_WINDOWS))]).astype(BF16)
    yd = _pool(big3, wbd, p['pool_scale'].reshape(1, -1), p['pool_out_g'].reshape(1, -1), ts)

    rw = jnp.zeros((D, LANES), F32).at[:, :N_GROUPS].set(p['router_g_w'])
    rw = rw.at[:, N_GROUPS:N_GROUPS + N_EXPERTS].set(p['router_e_w'])
    rb = jnp.zeros((1, LANES), F32).at[0, :N_GROUPS].set(p['router_g_b'])
    rb = rb.at[0, N_GROUPS:N_GROUPS + N_EXPERTS].set(p['router_e_b'])
    flat = lambda a: a.reshape(T, GROUP_W)
    h_new, hn_rows, ri, rf = _outproj(flat(ya), yb, flat(yc), flat(yd), h2d, p['w_out'].astype(BF16),
                                      p['ffn_norm_g'].reshape(1, D), rw, rb, tm=min(512, T))

    rank, cnt = _rank(ri, tm=min(512, T))
    dest, blk_e, n_used, n_blk = _dispatch_plan(ri[:, :TOPK_IN], rank[:, :TOPK_IN], cnt[0, :N_EXPERTS])
    tmd = min(256, T)
    dest3 = dest.reshape(T // tmd, 1, tmd * TOPK_IN)
    x_rows = _dispatch(dest3, hn_rows, n_blk * MOE_BLOCK, tmd)
    y_rows = _moe(blk_e, n_used, x_rows, p['moe_w1'], p['moe_w3'], p['moe_w2'])
    return _combine(dest3, h_new, rf, y_rows, tmd)


def kernel(x, attn_norm_g, w_in, w_out, fox_f_bias, fox_qn_g, fox_kn_g, fox_out_g, gmlp_ln_g, gmlp_ln_b, gmlp_ws, gmlp_bs, gmlp_out_g, gdn_conv_w, gdn_a_log, gdn_dt_bias, gdn_norm_g, pool_w, pool_scale, pool_out_g, ffn_norm_g, router_g_w, router_g_b, router_e_w, router_e_b, moe_w1, moe_w3, moe_w2):
    B, S, D = x.shape
    names = ('attn_norm_g', 'w_in', 'w_out', 'fox_f_bias', 'fox_qn_g', 'fox_kn_g', 'fox_out_g', 'gmlp_ln_g',
             'gmlp_ln_b', 'gmlp_ws', 'gmlp_bs', 'gmlp_out_g', 'gdn_conv_w', 'gdn_a_log', 'gdn_dt_bias',
             'gdn_norm_g', 'pool_w', 'pool_scale', 'pool_out_g', 'ffn_norm_g', 'router_g_w', 'router_g_b',
             'router_e_w', 'router_e_b', 'moe_w1', 'moe_w3', 'moe_w2')
    vals = (attn_norm_g, w_in, w_out, fox_f_bias, fox_qn_g, fox_kn_g, fox_out_g, gmlp_ln_g, gmlp_ln_b, gmlp_ws,
            gmlp_bs, gmlp_out_g, gdn_conv_w, gdn_a_log, gdn_dt_bias, gdn_norm_g, pool_w, pool_scale, pool_out_g,
            ffn_norm_g, router_g_w, router_g_b, router_e_w, router_e_b, moe_w1, moe_w3, moe_w2)
    h = x.reshape(B * S, D)
    for l in range(w_in.shape[0]):
        h = _layer(h, B, S, {n: v[l] for n, v in zip(names, vals)})
    return h.reshape(B, S, D)
```

```python
import functools

import jax
import jax.numpy as jnp
import numpy as np
from jax import lax
from jax.experimental import pallas as pl
from jax.experimental.pallas import tpu as pltpu

F32 = jnp.float32
BF16 = jnp.bfloat16
I32 = jnp.int32

EPS = 1e-6
HEAD_DIM = 64
GROUP_W = 256
N_HEADS = GROUP_W // HEAD_DIM
CHUNK = 64
GMLP_LEN = 128
CONV_K = 4
POOL_WINDOWS = (2, 4, 8, 16)
N_GROUPS = 4
EXPERTS_PER_GROUP = 8
N_EXPERTS = N_GROUPS * EXPERTS_PER_GROUP
TOPK_IN = 2
MOE_BLOCK = 256
LANES = 128
SUBLANES = 8
VMEM_LIMIT = 56 * 1024 * 1024

COL_FQ, COL_FK, COL_FV, COL_GU, COL_GV, COL_DQ, COL_DK, COL_DV, COL_DG, COL_PZ = range(10)
N_BIG_COLS = 10
LANE_FOX, LANE_DECAY, LANE_BETA = 0, 4, 8


def _params(*sem):
    return pltpu.CompilerParams(dimension_semantics=sem, vmem_limit_bytes=VMEM_LIMIT)


def _head_ones():
    r = lax.broadcasted_iota(I32, (GROUP_W, GROUP_W), 0) // HEAD_DIM
    c = lax.broadcasted_iota(I32, (GROUP_W, GROUP_W), 1) // HEAD_DIM
    return (r == c).astype(BF16)


def _head_sums(x, ones_bd):
    hi = x.astype(BF16)
    lo = (x - hi.astype(F32)).astype(BF16)
    return (jnp.dot(hi, ones_bd, preferred_element_type=F32)
            + jnp.dot(lo, ones_bd, preferred_element_type=F32))


def _rms(x, g):
    return x * lax.rsqrt(jnp.mean(x * x, axis=-1, keepdims=True) + EPS) * g


def _mm(a, b):
    return jnp.dot(a.astype(BF16), b.astype(BF16), preferred_element_type=F32)


def _mm_nt(a, b):
    return lax.dot_general(a.astype(BF16), b.astype(BF16), (((1,), (1,)), ((), ())),
                           preferred_element_type=F32)


def _mm_tn(a, b):
    return lax.dot_general(a.astype(BF16), b.astype(BF16), (((0,), (0,)), ((), ())),
                           preferred_element_type=F32)


def _split(a):
    hi = a.astype(BF16)
    return hi, (a - hi.astype(F32)).astype(BF16)


def _mm3(a, b):
    ah, al = _split(a)
    bh, bl = _split(b)
    d = functools.partial(jnp.dot, preferred_element_type=F32)
    return d(ah, bh) + (d(ah, bl) + d(al, bh))


def _inproj_body(x_ref, g_ref, wb_ref, ws_ref, big_ref, small_ref):
    xn = _rms(x_ref[...], g_ref[...]).astype(BF16)
    big_ref[...] = jnp.dot(xn, wb_ref[...], preferred_element_type=F32)
    small_ref[...] = jnp.dot(xn, ws_ref[...], preferred_element_type=F32)


def _inproj(x2d, g, wb, ws, tm):
    T, D = x2d.shape
    nb = wb.shape[1]
    return pl.pallas_call(
        _inproj_body, grid=(T // tm,),
        in_specs=[pl.BlockSpec((tm, D), lambda i: (i, 0)),
                  pl.BlockSpec((1, D), lambda i: (0, 0)),
                  pl.BlockSpec((D, nb), lambda i: (0, 0)),
                  pl.BlockSpec((D, LANES), lambda i: (0, 0))],
        out_specs=[pl.BlockSpec((tm, nb), lambda i: (i, 0)),
                   pl.BlockSpec((tm, LANES), lambda i: (i, 0))],
        out_shape=[jax.ShapeDtypeStruct((T, nb), F32), jax.ShapeDtypeStruct((T, LANES), F32)],
        compiler_params=_params("parallel"), name="inproj")(x2d, g, wb, ws)


def _gates_body(sm_ref, p_ref, col_ref, row_ref, carry_ref, *, ts):
    @pl.when(pl.program_id(1) == 0)
    def _():
        carry_ref[...] = jnp.zeros_like(carry_ref)

    x = sm_ref[...]
    lane = lax.broadcasted_iota(I32, (ts, LANES), 1)
    is_fox = lane < LANE_DECAY
    is_dec = (lane >= LANE_DECAY) & (lane < LANE_BETA)
    is_beta = (lane >= LANE_BETA) & (lane < LANE_BETA + N_HEADS)
    logf = jax.nn.log_sigmoid(x + p_ref[0:1, :])
    g = -jnp.exp(p_ref[2:3, :]) * jax.nn.softplus(x + p_ref[1:2, :])
    beta = jax.nn.sigmoid(x)
    r = lax.broadcasted_iota(I32, (ts, ts), 0)
    c = lax.broadcasted_iota(I32, (ts, ts), 1)
    tri_full = (r >= c).astype(BF16)
    tri_chunk = ((r >= c) & (r // CHUNK == c // CHUNK)).astype(BF16)
    vals = jnp.where(is_fox, logf, jnp.where(is_dec, g, 0.0))
    hi = vals.astype(BF16)
    mid = (vals - hi.astype(F32)).astype(BF16)
    lo = (vals - hi.astype(F32) - mid.astype(F32)).astype(BF16)
    parts = jnp.concatenate([hi, mid, lo], axis=1)

    def tri_sum(tri):
        t = jnp.dot(tri, parts, preferred_element_type=F32)
        return t[:, :LANES] + (t[:, LANES:2 * LANES] + t[:, 2 * LANES:])

    cf = tri_sum(tri_full) + carry_ref[...]
    cg = tri_sum(tri_chunk)
    carry_ref[...] = cf[ts - 1:ts, :]
    out = jnp.where(is_fox, cf, jnp.where(is_dec, cg, jnp.where(is_beta, beta, 0.0)))
    col_ref[...] = out
    row_ref[...] = out.T[:2 * SUBLANES, :]


def _gates(small3, prm, ts):
    B, S, _ = small3.shape
    return pl.pallas_call(
        functools.partial(_gates_body, ts=ts), grid=(B, S // ts),
        in_specs=[pl.BlockSpec((None, ts, LANES), lambda b, j: (b, j, 0)),
                  pl.BlockSpec((SUBLANES, LANES), lambda b, j: (0, 0))],
        out_specs=[pl.BlockSpec((None, ts, LANES), lambda b, j: (b, j, 0)),
                   pl.BlockSpec((None, 2 * SUBLANES, ts), lambda b, j: (b, 0, j))],
        out_shape=[jax.ShapeDtypeStruct((B, S, LANES), F32),
                   jax.ShapeDtypeStruct((B, 2 * SUBLANES, S), F32)],
        scratch_shapes=[pltpu.VMEM((1, LANES), F32)],
        compiler_params=_params("parallel", "arbitrary"), name="gates")(small3, prm)


def _fox_body(q_ref, k_ref, v_ref, ccol_ref, qg_ref, kg_ref, og_ref, o_ref,
              kn_scr, vt_scr, q_scr, m_scr, l_scr, acc_scr, *, tq, nk):
    i = pl.program_id(1)
    ones_bd = _head_ones()
    lane = lax.broadcasted_iota(I32, (tq, LANES), 1)
    log2e = 1.0 / np.log(2.0)

    def head_norm(x, g):
        ss = _head_sums(x * x, ones_bd)
        return x * lax.rsqrt(ss * (1.0 / HEAD_DIM) + EPS) * g

    def head_tile(x, h, extra):
        pair = x[:, (h // 2) * LANES:(h // 2 + 1) * LANES]
        if h % 2:
            pair = pltpu.roll(pair, HEAD_DIM, 1)
        return jnp.where(lane < HEAD_DIM, pair, extra).astype(BF16)

    @pl.when(i == 0)
    def _():
        for c in range(nk):
            rows = slice(c * tq, (c + 1) * tq)
            kc = head_norm(k_ref[rows, :], kg_ref[...])
            vt = v_ref[rows, :].T.astype(BF16)
            cc = ccol_ref[rows, :] * (-log2e)
            for h in range(N_HEADS):
                cj = cc[:, LANE_FOX + h:LANE_FOX + h + 1]
                hi = cj.astype(BF16).astype(F32)
                mid = (cj - hi).astype(BF16).astype(F32)
                lo = (cj - hi - mid).astype(BF16).astype(F32)
                extra = jnp.where(lane == HEAD_DIM, hi, jnp.where(lane == HEAD_DIM + 1, mid,
                                  jnp.where(lane == HEAD_DIM + 2, lo, 0.0)))
                kn_scr[h, c] = head_tile(kc, h, extra)
                vt_scr[h, c] = vt[h * HEAD_DIM:(h + 1) * HEAD_DIM, :]

    qn = head_norm(q_ref[...], qg_ref[...]) * (HEAD_DIM ** -0.5 * log2e)
    q_ones = jnp.where(lane < HEAD_DIM + 3, 1.0, 0.0)
    for h in range(N_HEADS):
        q_scr[h] = head_tile(qn, h, q_ones)
    m_scr[...] = jnp.full_like(m_scr, -jnp.inf)
    l_scr[...] = jnp.zeros_like(l_scr)
    acc_scr[...] = jnp.zeros_like(acc_scr)
    causal = (lax.broadcasted_iota(I32, (tq, tq), 0) <= lax.broadcasted_iota(I32, (tq, tq), 1))

    def step(j, masked):
        heads = range(N_HEADS)
        s = [lax.dot_general(kn_scr[h, j], q_scr[h], (((1,), (1,)), ((), ())),
                             preferred_element_type=F32) for h in heads]
        if masked:
            s = [jnp.where(causal, s[h], -jnp.inf) for h in heads]
        p, alpha = [], []
        for h in heads:
            m_old = m_scr[h]
            m_new = jnp.maximum(m_old, jnp.max(s[h], axis=0, keepdims=True))
            alpha.append(jnp.exp2(m_old - m_new))
            ph = jnp.exp2(s[h] - m_new)
            l_scr[h] = alpha[h] * l_scr[h] + jnp.sum(ph, axis=0, keepdims=True)
            m_scr[h] = m_new
            p.append(ph.astype(BF16))
        pv = [jnp.dot(vt_scr[h, j], p[h], preferred_element_type=F32) for h in heads]
        for h in heads:
            acc_scr[h] = alpha[h] * acc_scr[h] + pv[h]

    def unmasked(j, c):
        step(j, False)
        return c

    lax.fori_loop(0, i, unmasked, 0)
    step(i, True)
    o_t = jnp.concatenate([acc_scr[h] / l_scr[h] for h in range(N_HEADS)], axis=0)
    o_ref[...] = _rms(o_t.T, og_ref[...])


def _fox(big3, gcol, qg, kg, og, tq):
    B, S, _ = big3.shape
    nk = S // tq
    row = pl.BlockSpec((1, GROUP_W), lambda b, i: (0, 0))
    return pl.pallas_call(
        functools.partial(_fox_body, tq=tq, nk=nk), grid=(B, nk),
        in_specs=[pl.BlockSpec((None, tq, GROUP_W), lambda b, i: (b, i, COL_FQ)),
                  pl.BlockSpec((None, S, GROUP_W), lambda b, i: (b, 0, COL_FK)),
                  pl.BlockSpec((None, S, GROUP_W), lambda b, i: (b, 0, COL_FV)),
                  pl.BlockSpec((None, S, LANES), lambda b, i: (b, 0, 0)),
                  row, row, row],
        out_specs=pl.BlockSpec((None, tq, GROUP_W), lambda b, i: (b, i, 0)),
        out_shape=jax.ShapeDtypeStruct((B, S, GROUP_W), F32),
        scratch_shapes=[pltpu.VMEM((N_HEADS, nk, tq, LANES), BF16),
                        pltpu.VMEM((N_HEADS, nk, HEAD_DIM, tq), BF16),
                        pltpu.VMEM((N_HEADS, tq, LANES), BF16),
                        pltpu.VMEM((N_HEADS, 1, tq), F32),
                        pltpu.VMEM((N_HEADS, 1, tq), F32),
                        pltpu.VMEM((N_HEADS, HEAD_DIM, tq), F32)],
        compiler_params=_params("parallel", "arbitrary"), name="fox")(
            big3, big3, big3, gcol, qg, kg, og)


def _gelu(x):
    return 0.5 * x * (1.0 + lax.erf(x * (2.0 ** -0.5)))


def _gmlp_body(u_ref, v_ref, lg_ref, lb_ref, ws_ref, bst_ref, og_ref, o_ref, *, nwin):
    L = GMLP_LEN
    r = lax.broadcasted_iota(I32, (L, L), 0) // CHUNK
    c = lax.broadcasted_iota(I32, (L, L), 1) // CHUNK
    mask = r >= c
    ws = [jnp.where(mask, ws_ref[h], 0.0).astype(BF16) for h in range(N_HEADS)]
    for n in range(nwin):
        u = _gelu(u_ref[n * L:(n + 1) * L, :])
        v = _gelu(v_ref[n * L:(n + 1) * L, :])
        mu = jnp.mean(v, axis=-1, keepdims=True)
        vc = v - mu
        var = jnp.mean(vc * vc, axis=-1, keepdims=True)
        vn = (vc * lax.rsqrt(var + EPS) * lg_ref[...] + lb_ref[...]).astype(BF16)
        mixed = jnp.concatenate(
            [jnp.dot(ws[h], vn[:, h * HEAD_DIM:(h + 1) * HEAD_DIM], preferred_element_type=F32)
             for h in range(N_HEADS)], axis=-1) + bst_ref[...]
        o_ref[n * L:(n + 1) * L, :] = _rms(u * mixed, og_ref[...])


def _gmlp(big, lg, lb, ws, bst, og, tm):
    T = big.shape[0]
    row = pl.BlockSpec((1, GROUP_W), lambda i: (0, 0))
    return pl.pallas_call(
        functools.partial(_gmlp_body, nwin=tm // GMLP_LEN), grid=(T // tm,),
        in_specs=[pl.BlockSpec((tm, GROUP_W), lambda i: (i, COL_GU)),
                  pl.BlockSpec((tm, GROUP_W), lambda i: (i, COL_GV)),
                  row, row,
                  pl.BlockSpec((N_HEADS, GMLP_LEN, GMLP_LEN), lambda i: (0, 0, 0)),
                  pl.BlockSpec((GMLP_LEN, GROUP_W), lambda i: (0, 0)),
                  row],
        out_specs=pl.BlockSpec((tm, GROUP_W), lambda i: (i, 0)),
        out_shape=jax.ShapeDtypeStruct((T, GROUP_W), F32),
        compiler_params=_params("parallel"), name="gmlp")(big, big, lg, lb, ws, bst, og)


def _gdn_prep_body(q_ref, k_ref, v_ref, hq_ref, hk_ref, hv_ref, w_ref, qo_ref, ko_ref, vo_ref, *, ts):
    first = pl.program_id(1) == 0
    ones_bd = _head_ones()

    def conv(x_ref, halo_ref, w):
        halo = jnp.where(first, 0.0, halo_ref[...])
        xx = jnp.concatenate([halo, x_ref[...]], axis=0)
        y = w[CONV_K - 1:CONV_K, :] * xx[SUBLANES:, :]
        for j in range(CONV_K - 1):
            y = y + w[j:j + 1, :] * pltpu.roll(xx, CONV_K - 1 - j, 0)[SUBLANES:, :]
        return y * jax.nn.sigmoid(y)

    def l2(t):
        return t * lax.rsqrt(_head_sums(t * t, ones_bd) + EPS)

    qo_ref[...] = l2(conv(q_ref, hq_ref, w_ref[0])) * (HEAD_DIM ** -0.5)
    ko_ref[...] = l2(conv(k_ref, hk_ref, w_ref[1]))
    vo_ref[...] = conv(v_ref, hv_ref, w_ref[2])


def _gdn_prep(big3, conv_w3, ts):
    B, S, _ = big3.shape
    hb = ts // SUBLANES
    blk = lambda col: pl.BlockSpec((None, ts, GROUP_W), lambda b, i: (b, i, col))
    halo = lambda col: pl.BlockSpec((None, SUBLANES, GROUP_W),
                                    lambda b, i: (b, jnp.maximum(i * hb - 1, 0), col))
    out = pl.BlockSpec((None, ts, GROUP_W), lambda b, i: (b, i, 0))
    shp = jax.ShapeDtypeStruct((B, S, GROUP_W), F32)
    return pl.pallas_call(
        functools.partial(_gdn_prep_body, ts=ts), grid=(B, S // ts),
        in_specs=[blk(COL_DQ), blk(COL_DK), blk(COL_DV), halo(COL_DQ), halo(COL_DK), halo(COL_DV),
                  pl.BlockSpec((3, CONV_K, GROUP_W), lambda b, i: (0, 0, 0))],
        out_specs=[out, out, out], out_shape=[shp, shp, shp],
        compiler_params=_params("parallel", "parallel"), name="gdn_prep")(
            big3, big3, big3, big3, big3, big3, conv_w3)


def _gdn_body(q_ref, k_ref, v_ref, gate_ref, gcol_ref, grow_ref, ng_ref, o_ref,
              s_scr, u_scr, wq_scr, a_scr, kd_scr, dl_scr, t_scr, p_scr, rhs_scr, *, nchunk):
    C = CHUNK

    @pl.when(pl.program_id(1) == 0)
    def _():
        s_scr[...] = jnp.zeros_like(s_scr)

    r = lax.broadcasted_iota(I32, (C, C), 0)
    c = lax.broadcasted_iota(I32, (C, C), 1)
    tri = r >= c
    strict = r > c
    eye = (r == c).astype(F32)

    items = [(n, h) for n in range(nchunk) for h in range(N_HEADS)]
    mmb = functools.partial(jnp.dot, preferred_element_type=F32)
    for it, (n, h) in enumerate(items):
        rows = slice(n * C, (n + 1) * C)
        sl = slice(h * HEAD_DIM, (h + 1) * HEAD_DIM)
        q, k, v = q_ref[rows, sl], k_ref[rows, sl], v_ref[rows, sl]
        gc = gcol_ref[rows, LANE_DECAY + h:LANE_DECAY + h + 1]
        gr = grow_ref[n, LANE_DECAY + h:LANE_DECAY + h + 1, :]
        beta = gcol_ref[rows, LANE_BETA + h:LANE_BETA + h + 1]
        decay = jnp.exp(jnp.where(tri, gc - gr, -jnp.inf))
        kb = k * beta
        kk = _mm_nt(jnp.concatenate([kb, q], axis=0), k)
        x = jnp.where(strict, -(kk[:C] * decay), 0.0)
        t_scr[it] = eye + x
        p_scr[it] = x.astype(BF16)
        eg = jnp.exp(gc)
        g_last = gc[C - 1:C, :]
        rhs_scr[it] = jnp.concatenate([v * beta, kb * eg], axis=1).astype(BF16)
        wq_scr[n, h, C:, :] = (q * eg).astype(BF16)
        a_scr[n, h] = jnp.where(tri, kk[C:] * decay, 0.0).astype(BF16)
        kd_scr[n, h] = (k * jnp.exp(g_last - gc)).astype(BF16)
        dl_scr[n, h] = jnp.broadcast_to(jnp.exp(g_last), (1, HEAD_DIM))
    for _ in range(5):
        for it in range(len(items)):
            p = p_scr[it]
            p_scr[it] = mmb(p, p).astype(BF16)
        for it in range(len(items)):
            t = t_scr[it]
            t_scr[it] = t + mmb(t.astype(BF16), p_scr[it])
    for it, (n, h) in enumerate(items):
        uw = mmb(t_scr[it].astype(BF16), rhs_scr[it])
        u_scr[n, h] = uw[:, :HEAD_DIM]
        wq_scr[n, h, :C, :] = uw[:, HEAD_DIM:].astype(BF16)

    heads = range(N_HEADS)
    state = [s_scr[h] for h in heads]
    for n in range(nchunk):
        rows = slice(n * C, (n + 1) * C)
        ws = [mmb(wq_scr[n, h], state[h].astype(BF16)) for h in heads]
        vb = [(u_scr[n, h] - ws[h][:C]).astype(BF16) for h in heads]
        o = [ws[h][C:] + mmb(a_scr[n, h], vb[h]) for h in heads]
        state = [state[h] * dl_scr[n, h] + lax.dot_general(
            kd_scr[n, h], vb[h], (((0,), (0,)), ((), ())), preferred_element_type=F32) for h in heads]
        gate = gate_ref[rows, :]
        y = jnp.concatenate([_rms(o[h], ng_ref[...]) for h in heads], axis=-1)
        o_ref[rows, :] = y * (gate * jax.nn.sigmoid(gate))
    for h in heads:
        s_scr[h] = state[h]


def _gdn(qn, kn, vv, big3, gcol, grow4, ng, ts):
    B, S, _ = qn.shape
    nchunk = ts // CHUNK
    blk = pl.BlockSpec((None, ts, GROUP_W), lambda b, i: (b, i, 0))
    return pl.pallas_call(
        functools.partial(_gdn_body, nchunk=nchunk), grid=(B, S // ts),
        in_specs=[blk, blk, blk,
                  pl.BlockSpec((None, ts, GROUP_W), lambda b, i: (b, i, COL_DG)),
                  pl.BlockSpec((None, ts, LANES), lambda b, i: (b, i, 0)),
                  pl.BlockSpec((None, nchunk, 2 * SUBLANES, CHUNK), lambda b, i: (b, i, 0, 0)),
                  pl.BlockSpec((1, HEAD_DIM), lambda b, i: (0, 0))],
        out_specs=blk, out_shape=jax.ShapeDtypeStruct((B, S, GROUP_W), F32),
        scratch_shapes=[pltpu.VMEM((N_HEADS, HEAD_DIM, HEAD_DIM), F32),
                        pltpu.VMEM((nchunk, N_HEADS, CHUNK, HEAD_DIM), F32),
                        pltpu.VMEM((nchunk, N_HEADS, 2 * CHUNK, HEAD_DIM), BF16),
                        pltpu.VMEM((nchunk, N_HEADS, CHUNK, CHUNK), BF16),
                        pltpu.VMEM((nchunk, N_HEADS, CHUNK, HEAD_DIM), BF16),
                        pltpu.VMEM((nchunk, N_HEADS, 1, HEAD_DIM), F32),
                        pltpu.VMEM((nchunk * N_HEADS, CHUNK, CHUNK), F32),
                        pltpu.VMEM((nchunk * N_HEADS, CHUNK, CHUNK), BF16),
                        pltpu.VMEM((nchunk * N_HEADS, CHUNK, 2 * HEAD_DIM), BF16)],
        compiler_params=_params("parallel", "arbitrary"), name="gdn")(
            qn, kn, vv, big3, gcol, grow4, ng)


def _pool_body(z_ref, halo_ref, w_ref, sc_ref, og_ref, o_ref, *, ts):
    i = pl.program_id(1)
    hr = 2 * SUBLANES
    z = z_ref[...]
    halo = jnp.where(i == 0, 0.0, halo_ref[...])
    s1 = jnp.concatenate([halo, z], axis=0)
    s2 = s1 + pltpu.roll(s1, 1, 0)
    s4 = s2 + pltpu.roll(s2, 2, 0)
    s8 = s4 + pltpu.roll(s4, 4, 0)
    s16 = s8 + pltpu.roll(s8, 8, 0)
    grp = lax.broadcasted_iota(I32, (ts, GROUP_W), 1) // (GROUP_W // len(POOL_WINDOWS))
    t = lax.broadcasted_iota(I32, (ts, GROUP_W), 0) + i * ts
    total = jnp.where(grp == 0, s2[hr:], jnp.where(grp == 1, s4[hr:], jnp.where(grp == 2, s8[hr:], s16[hr:])))
    win = jnp.where(grp == 0, POOL_WINDOWS[0], jnp.where(grp == 1, POOL_WINDOWS[1],
                    jnp.where(grp == 2, POOL_WINDOWS[2], POOL_WINDOWS[3])))
    pooled = total / jnp.minimum(t + 1, win).astype(F32)
    y = _mm(pooled - z, w_ref[...]) * sc_ref[...]
    o_ref[...] = _rms(y, og_ref[...])


def _pool(big3, wbd, sc, og, ts):
    B, S, _ = big3.shape
    hr = 2 * SUBLANES
    hb = ts // hr
    row = pl.BlockSpec((1, GROUP_W), lambda b, i: (0, 0))
    return pl.pallas_call(
        functools.partial(_pool_body, ts=ts), grid=(B, S // ts),
        in_specs=[pl.BlockSpec((None, ts, GROUP_W), lambda b, i: (b, i, COL_PZ)),
                  pl.BlockSpec((None, hr, GROUP_W), lambda b, i: (b, jnp.maximum(i * hb - 1, 0), COL_PZ)),
                  pl.BlockSpec((GROUP_W, GROUP_W), lambda b, i: (0, 0)), row, row],
        out_specs=pl.BlockSpec((None, ts, GROUP_W), lambda b, i: (b, i, 0)),
        out_shape=jax.ShapeDtypeStruct((B, S, GROUP_W), F32),
        compiler_params=_params("parallel", "parallel"), name="pool")(big3, big3, wbd, sc, og)


def _outproj_body(ya_ref, yb_ref, yc_ref, yd_ref, h_ref, wo_ref, g_ref, rw_ref, rb_ref,
                  hnew_ref, hn_ref, ri_ref, rf_ref, *, tm, d):
    y = jnp.concatenate([ya_ref[...], yb_ref[...], yc_ref[...], yd_ref[...]], axis=-1).astype(BF16)
    h_new = h_ref[...] + jnp.dot(y, wo_ref[...], preferred_element_type=F32)
    hnew_ref[...] = h_new
    hn = _rms(h_new, g_ref[...])
    nseg = d // LANES
    for s in range(nseg):
        hn_ref[pl.ds(s, tm, stride=nseg), :] = hn[:, s * LANES:(s + 1) * LANES]

    hn_hi, hn_lo = _split(hn)
    t = jnp.dot(hn_hi, rw_ref[...], preferred_element_type=F32)
    logits = (t[:, :LANES] + t[:, LANES:]
              + jnp.dot(hn_lo, rw_ref[:, :LANES], preferred_element_type=F32)) + rb_ref[...]
    lane = lax.broadcasted_iota(I32, (tm, LANES), 1)
    neg = -jnp.inf
    big_lane = LANES

    def masked_top(vals, mask):
        v = jnp.where(mask, vals, neg)
        mx = jnp.max(v, axis=-1, keepdims=True)
        idx = jnp.min(jnp.where(mask & (v == mx), lane, big_lane), axis=-1, keepdims=True)
        return v, mx, idx

    gmask = lane < N_GROUPS
    gv, gmx, gidx = masked_top(logits, gmask)
    g_top = 1.0 / jnp.sum(jnp.where(gmask, jnp.exp(gv - gmx), 0.0), axis=-1, keepdims=True)
    lo = N_GROUPS + gidx * EXPERTS_PER_GROUP
    emask = (lane >= lo) & (lane < lo + EXPERTS_PER_GROUP)
    ev, emx, eidx1 = masked_top(logits, emask)
    esum = jnp.sum(jnp.where(emask, jnp.exp(ev - emx), 0.0), axis=-1, keepdims=True)
    p1 = 1.0 / esum
    _, emx2, eidx2 = masked_top(logits, emask & (lane != eidx1))
    p2 = jnp.exp(emx2 - emx) / esum
    denom = p1 + p2
    ri_ref[...] = jnp.where(lane == 0, eidx1 - N_GROUPS, jnp.where(lane == 1, eidx2 - N_GROUPS, 0))
    rf_ref[...] = jnp.where(lane == 0, g_top * p1 / denom, jnp.where(lane == 1, g_top * p2 / denom, 0.0))


def _outproj(ya, yb, yc, yd, h2d, wo, g, rw, rb, tm):
    T, D = h2d.shape
    nseg = D // LANES
    yblk = pl.BlockSpec((tm, GROUP_W), lambda i: (i, 0))
    return pl.pallas_call(
        functools.partial(_outproj_body, tm=tm, d=D), grid=(T // tm,),
        in_specs=[yblk, yblk, yblk, yblk,
                  pl.BlockSpec((tm, D), lambda i: (i, 0)),
                  pl.BlockSpec((D, D), lambda i: (0, 0)),
                  pl.BlockSpec((1, D), lambda i: (0, 0)),
                  pl.BlockSpec((D, 2 * LANES), lambda i: (0, 0)),
                  pl.BlockSpec((1, LANES), lambda i: (0, 0))],
        out_specs=[pl.BlockSpec((tm, D), lambda i: (i, 0)),
                   pl.BlockSpec((tm * nseg, LANES), lambda i: (i, 0)),
                   pl.BlockSpec((tm, LANES), lambda i: (i, 0)),
                   pl.BlockSpec((tm, LANES), lambda i: (i, 0))],
        out_shape=[jax.ShapeDtypeStruct((T, D), F32),
                   jax.ShapeDtypeStruct((T * nseg, LANES), F32),
                   jax.ShapeDtypeStruct((T, LANES), I32),
                   jax.ShapeDtypeStruct((T, LANES), F32)],
        compiler_params=_params("parallel"), name="outproj")(ya, yb, yc, yd, h2d, wo, g, rw, rb)


def _rank_body(ri_ref, rank_ref, cnt_ref, carry_ref, *, tm):
    @pl.when(pl.program_id(0) == 0)
    def _():
        carry_ref[...] = jnp.zeros_like(carry_ref)

    lane = lax.broadcasted_iota(I32, (tm, LANES), 1)
    e = ri_ref[...]
    oh = [lane == e[:, s:s + 1] for s in range(TOPK_IN)]
    m = (oh[0].astype(F32) + oh[1].astype(F32)).astype(BF16)
    below = (lax.broadcasted_iota(I32, (tm, tm), 0) > lax.broadcasted_iota(I32, (tm, tm), 1)).astype(BF16)
    before = jnp.dot(below, m, preferred_element_type=F32) + carry_ref[...]
    rank = [jnp.sum(jnp.where(oh[s], before, 0.0), axis=-1, keepdims=True) for s in range(TOPK_IN)]
    rank_ref[...] = jnp.where(lane == 0, rank[0], jnp.where(lane == 1, rank[1], 0.0)).astype(I32)
    total = before[tm - 1:tm, :] + m[tm - 1:tm, :].astype(F32)
    carry_ref[...] = total
    cnt_ref[...] = jnp.broadcast_to(total, cnt_ref.shape)


def _rank(ri, tm):
    T = ri.shape[0]
    return pl.pallas_call(
        functools.partial(_rank_body, tm=tm), grid=(T // tm,),
        in_specs=[pl.BlockSpec((tm, LANES), lambda i: (i, 0))],
        out_specs=[pl.BlockSpec((tm, LANES), lambda i: (i, 0)),
                   pl.BlockSpec((SUBLANES, LANES), lambda i: (0, 0))],
        out_shape=[jax.ShapeDtypeStruct((T, LANES), I32), jax.ShapeDtypeStruct((SUBLANES, LANES), F32)],
        scratch_shapes=[pltpu.VMEM((1, LANES), F32)],
        compiler_params=_params("arbitrary"), name="rank")(ri)


def _dispatch_plan(expert, rank, counts):
    T = expert.shape[0]
    counts = counts.astype(I32)
    padded = (counts + MOE_BLOCK - 1) // MOE_BLOCK * MOE_BLOCK
    pad_end = jnp.cumsum(padded)
    pad_start = pad_end - padded
    onehot = expert[:, :, None] == jnp.arange(N_EXPERTS, dtype=I32)[None, None, :]
    dest = rank + jnp.sum(jnp.where(onehot, pad_start[None, None, :], 0), axis=-1)
    n_blk = -(-T * TOPK_IN // MOE_BLOCK) + N_EXPERTS
    blk_start = jnp.arange(n_blk, dtype=I32) * MOE_BLOCK
    blk_e = jnp.minimum(jnp.sum(pad_end[None, :] <= blk_start[:, None], axis=-1), N_EXPERTS - 1).astype(I32)
    n_used = (pad_end[-1] // MOE_BLOCK).astype(I32).reshape(1)
    return dest.astype(I32), blk_e, n_used, n_blk


def _row_tile(ref, r, nseg):
    return ref.at[pl.ds(pl.multiple_of(r * nseg, nseg), nseg), :]


def _dispatch_body(dst_ref, hn_ref, xz_hbm, x_hbm, sem, *, tm, nseg, unroll):
    del xz_hbm

    def issue(g, c):
        for u in range(unroll):
            r = g * unroll + u
            for s in range(TOPK_IN):
                pltpu.make_async_copy(_row_tile(hn_ref, r, nseg),
                                      _row_tile(x_hbm, dst_ref[0, 0, r * TOPK_IN + s], nseg), sem).start()
        return c

    lax.fori_loop(0, tm // unroll, issue, 0)
    for s in range(TOPK_IN):
        pltpu.make_async_copy(hn_ref, x_hbm.at[pl.ds(0, tm * nseg), :], sem).wait()


def _dispatch(dest3, hn_rows, n_pad, tm):
    nt = dest3.shape[0]
    nseg = hn_rows.shape[0] // (nt * tm)
    x_zero = jnp.zeros((n_pad * nseg, LANES), F32)
    return pl.pallas_call(
        functools.partial(_dispatch_body, tm=tm, nseg=nseg, unroll=8), grid=(nt,),
        in_specs=[pl.BlockSpec((1, 1, tm * TOPK_IN), lambda i: (i, 0, 0), memory_space=pltpu.SMEM),
                  pl.BlockSpec((tm * nseg, LANES), lambda i: (i, 0)),
                  pl.BlockSpec(memory_space=pl.ANY)],
        out_specs=pl.BlockSpec(memory_space=pl.ANY),
        out_shape=jax.ShapeDtypeStruct((n_pad * nseg, LANES), F32),
        scratch_shapes=[pltpu.SemaphoreType.DMA],
        input_output_aliases={2: 0},
        compiler_params=_params("arbitrary"), name="dispatch")(dest3, hn_rows, x_zero)


def _moe_body(blk_e_ref, n_used_ref, x_ref, w1_ref, w3_ref, w2_ref, y_ref, w1b, w3b, w2b, *, nseg):
    b = pl.program_id(0)
    R = MOE_BLOCK

    @pl.when(b < n_used_ref[0])
    def _():
        @pl.when((b == 0) | (blk_e_ref[b] != blk_e_ref[jnp.maximum(b - 1, 0)]))
        def _():
            w1b[...] = w1_ref[...].astype(BF16)
            w3b[...] = w3_ref[...].astype(BF16)
            w2b[...] = w2_ref[...].astype(BF16)

        x = jnp.concatenate([x_ref[pl.ds(s, R, stride=nseg), :] for s in range(nseg)], axis=-1).astype(BF16)
        a = jnp.dot(x, w1b[...], preferred_element_type=F32)
        g = jnp.dot(x, w3b[...], preferred_element_type=F32)
        hid = (a * jax.nn.sigmoid(a) * g).astype(BF16)
        y = jnp.dot(hid, w2b[...], preferred_element_type=F32)
        for s in range(nseg):
            y_ref[pl.ds(s, R, stride=nseg), :] = y[:, s * LANES:(s + 1) * LANES]

    @pl.when(b >= n_used_ref[0])
    def _():
        y_ref[...] = jnp.zeros_like(y_ref)


def _moe(blk_e, n_used, x_rows, w1, w3, w2, layer):
    n_blk = blk_e.shape[0]
    _, _, D, DE = w1.shape
    nseg = D // LANES
    R = MOE_BLOCK
    rows = lambda b, be, nu: (jnp.minimum(b, nu[0] - 1), 0)
    wmap = lambda b, be, nu: (layer, be[b], 0, 0)
    grid_spec = pltpu.PrefetchScalarGridSpec(
        num_scalar_prefetch=2, grid=(n_blk,),
        in_specs=[pl.BlockSpec((R * nseg, LANES), rows),
                  pl.BlockSpec((None, None, D, DE), wmap),
                  pl.BlockSpec((None, None, D, DE), wmap),
                  pl.BlockSpec((None, None, DE, D), wmap)],
        out_specs=pl.BlockSpec((R * nseg, LANES), lambda b, be, nu: (b, 0)),
        scratch_shapes=[pltpu.VMEM((D, DE), BF16), pltpu.VMEM((D, DE), BF16), pltpu.VMEM((DE, D), BF16)])
    return pl.pallas_call(
        functools.partial(_moe_body, nseg=nseg), grid_spec=grid_spec,
        out_shape=jax.ShapeDtypeStruct(x_rows.shape, F32),
        compiler_params=_params("arbitrary"), name="moe")(blk_e, n_used, x_rows, w1, w3, w2)


def _combine_body(dst_ref, h_ref, rf_ref, y_hbm, o_ref, ybuf, sem, *, tm, nseg, unroll):
    def issue(g, c):
        for u in range(unroll):
            a = g * unroll + u
            pltpu.make_async_copy(_row_tile(y_hbm, dst_ref[0, 0, a], nseg), _row_tile(ybuf, a, nseg), sem).start()
        return c

    lax.fori_loop(0, tm * TOPK_IN // unroll, issue, 0)
    pltpu.make_async_copy(y_hbm.at[pl.ds(0, tm * TOPK_IN * nseg), :], ybuf, sem).wait()
    stride = TOPK_IN * nseg
    acc = h_ref[...]
    for s in range(TOPK_IN):
        ys = jnp.concatenate([ybuf[pl.ds(s * nseg + k, tm, stride=stride), :] for k in range(nseg)], axis=-1)
        acc = acc + ys * rf_ref[:, s:s + 1]
    o_ref[...] = acc


def _combine(dest3, h2d, rf, y_rows, tm):
    T, D = h2d.shape
    nseg = D // LANES
    return pl.pallas_call(
        functools.partial(_combine_body, tm=tm, nseg=nseg, unroll=8), grid=(T // tm,),
        in_specs=[pl.BlockSpec((1, 1, tm * TOPK_IN), lambda i: (i, 0, 0), memory_space=pltpu.SMEM),
                  pl.BlockSpec((tm, D), lambda i: (i, 0)),
                  pl.BlockSpec((tm, LANES), lambda i: (i, 0)),
                  pl.BlockSpec(memory_space=pl.ANY)],
        out_specs=pl.BlockSpec((tm, D), lambda i: (i, 0)),
        out_shape=jax.ShapeDtypeStruct((T, D), F32),
        scratch_shapes=[pltpu.VMEM((tm * TOPK_IN * nseg, LANES), F32), pltpu.SemaphoreType.DMA],
        compiler_params=_params("arbitrary"), name="combine")(dest3, h2d, rf, y_rows)


def _pad_lanes(a, lane0, rows=1):
    out = jnp.zeros((rows, LANES), a.dtype)
    return out.at[:, lane0:lane0 + a.shape[-1]].set(a.reshape(rows, -1))


def _layer(h2d, B, S, p):
    T, D = h2d.shape
    tile = lambda a, n: jnp.tile(a.reshape(1, -1), (1, n))

    offs = np.cumsum([0, GROUP_W, GROUP_W, GROUP_W, N_HEADS, GROUP_W, GROUP_W,
                      GROUP_W, GROUP_W, GROUP_W, N_HEADS, N_HEADS, GROUP_W, GROUP_W])
    seg = lambda k: p['w_in'][:, offs[k]:offs[k + 1]]
    wb = jnp.concatenate([seg(0), seg(1), seg(2), seg(4), seg(5), seg(6), seg(7), seg(8), seg(11), seg(12)],
                         axis=1).astype(BF16)
    ws = jnp.zeros((D, LANES), F32)
    ws = ws.at[:, LANE_FOX:LANE_FOX + N_HEADS].set(seg(3))
    ws = ws.at[:, LANE_DECAY:LANE_DECAY + N_HEADS].set(seg(9))
    ws = ws.at[:, LANE_BETA:LANE_BETA + N_HEADS].set(seg(10)).astype(BF16)
    gate_prm = jnp.concatenate([_pad_lanes(p['fox_f_bias'], LANE_FOX), _pad_lanes(p['gdn_dt_bias'], LANE_DECAY),
                                _pad_lanes(p['gdn_a_log'], LANE_DECAY), jnp.zeros((SUBLANES - 3, LANES), F32)], axis=0)

    big, small = _inproj(h2d, p['attn_norm_g'].reshape(1, D), wb, ws, tm=min(512, T))
    big3 = big.reshape(B, S, N_BIG_COLS * GROUP_W)
    ts = min(512, S)
    gcol, grow = _gates(small.reshape(B, S, LANES), gate_prm, ts)

    tq = min(256, S)
    ya = _fox(big3, gcol, tile(p['fox_qn_g'], N_HEADS), tile(p['fox_kn_g'], N_HEADS),
              p['fox_out_g'].reshape(1, GROUP_W), tq)

    bst = jnp.repeat(p['gmlp_bs'].T, HEAD_DIM, axis=1)
    yb = _gmlp(big, p['gmlp_ln_g'].reshape(1, -1), p['gmlp_ln_b'].reshape(1, -1), p['gmlp_ws'], bst,
               p['gmlp_out_g'].reshape(1, -1), tm=min(512, T))

    conv_w3 = p['gdn_conv_w'].reshape(CONV_K, 3, GROUP_W).transpose(1, 0, 2)
    qn, kn, vv = _gdn_prep(big3, conv_w3, ts)
    grow4 = grow.reshape(B, 2 * SUBLANES, S // CHUNK, CHUNK).transpose(0, 2, 1, 3)
    yc = _gdn(qn, kn, vv, big3, gcol, grow4, p['gdn_norm_g'].reshape(1, HEAD_DIM), ts)

    wbd = jax.scipy.linalg.block_diag(*[p['pool_w'][g] for g in range(len(POOL_WINDOWS))]).astype(BF16)
    yd = _pool(big3, wbd, p['pool_scale'].reshape(1, -1), p['pool_out_g'].reshape(1, -1), ts)

    rw = jnp.zeros((D, LANES), F32).at[:, :N_GROUPS].set(p['router_g_w'])
    rw = rw.at[:, N_GROUPS:N_GROUPS + N_EXPERTS].set(p['router_e_w'])
    rw = jnp.concatenate(_split(rw), axis=1)
    rb = jnp.zeros((1, LANES), F32).at[0, :N_GROUPS].set(p['router_g_b'])
    rb = rb.at[0, N_GROUPS:N_GROUPS + N_EXPERTS].set(p['router_e_b'])
    flat = lambda a: a.reshape(T, GROUP_W)
    h_new, hn_rows, ri, rf = _outproj(flat(ya), yb, flat(yc), flat(yd), h2d, p['w_out'].astype(BF16),
                                      p['ffn_norm_g'].reshape(1, D), rw, rb, tm=min(512, T))

    rank, cnt = _rank(ri, tm=min(512, T))
    dest, blk_e, n_used, n_blk = _dispatch_plan(ri[:, :TOPK_IN], rank[:, :TOPK_IN], cnt[0, :N_EXPERTS])
    tmd = min(256, T)
    dest3 = dest.reshape(T // tmd, 1, tmd * TOPK_IN)
    x_rows = _dispatch(dest3, hn_rows, n_blk * MOE_BLOCK, tmd)
    y_rows = _moe(blk_e, n_used, x_rows, p['moe_w1'], p['moe_w3'], p['moe_w2'], p['layer'])
    return _combine(dest3, h_new, rf, y_rows, tmd)


def kernel(x, attn_norm_g, w_in, w_out, fox_f_bias, fox_qn_g, fox_kn_g, fox_out_g, gmlp_ln_g, gmlp_ln_b, gmlp_ws, gmlp_bs, gmlp_out_g, gdn_conv_w, gdn_a_log, gdn_dt_bias, gdn_norm_g, pool_w, pool_scale, pool_out_g, ffn_norm_g, router_g_w, router_g_b, router_e_w, router_e_b, moe_w1, moe_w3, moe_w2):
    B, S, D = x.shape
    names = ('attn_norm_g', 'w_in', 'w_out', 'fox_f_bias', 'fox_qn_g', 'fox_kn_g', 'fox_out_g', 'gmlp_ln_g',
             'gmlp_ln_b', 'gmlp_ws', 'gmlp_bs', 'gmlp_out_g', 'gdn_conv_w', 'gdn_a_log', 'gdn_dt_bias',
             'gdn_norm_g', 'pool_w', 'pool_scale', 'pool_out_g', 'ffn_norm_g', 'router_g_w', 'router_g_b',
             'router_e_w', 'router_e_b', 'moe_w1', 'moe_w3', 'moe_w2')
    vals = (attn_norm_g, w_in, w_out, fox_f_bias, fox_qn_g, fox_kn_g, fox_out_g, gmlp_ln_g, gmlp_ln_b, gmlp_ws,
            gmlp_bs, gmlp_out_g, gdn_conv_w, gdn_a_log, gdn_dt_bias, gdn_norm_g, pool_w, pool_scale, pool_out_g,
            ffn_norm_g, router_g_w, router_g_b, router_e_w, router_e_b, moe_w1, moe_w3, moe_w2)
    h = x.reshape(B * S, D)
    stacked = ('moe_w1', 'moe_w3', 'moe_w2')
    for l in range(w_in.shape[0]):
        p = {n: (v if n in stacked else v[l]) for n, v in zip(names, vals)}
        p['layer'] = l
        h = _layer(h, B, S, p)
    return h.reshape(B, S, D)
```

```python
import functools

import jax
import jax.numpy as jnp
import numpy as np
from jax import lax
from jax.experimental import pallas as pl
from jax.experimental.pallas import tpu as pltpu

F32 = jnp.float32
BF16 = jnp.bfloat16
I32 = jnp.int32

EPS = 1e-6
HEAD_DIM = 64
GROUP_W = 256
N_HEADS = GROUP_W // HEAD_DIM
CHUNK = 64
GMLP_LEN = 128
CONV_K = 4
POOL_WINDOWS = (2, 4, 8, 16)
N_GROUPS = 4
EXPERTS_PER_GROUP = 8
N_EXPERTS = N_GROUPS * EXPERTS_PER_GROUP
TOPK_IN = 2
MOE_BLOCK = 256
LANES = 128
SUBLANES = 8
VMEM_LIMIT = 56 * 1024 * 1024

COL_FQ, COL_FK, COL_FV, COL_GU, COL_GV, COL_DQ, COL_DK, COL_DV, COL_DG, COL_PZ = range(10)
N_BIG_COLS = 10
LANE_FOX, LANE_DECAY, LANE_BETA = 0, 4, 8


def _params(*sem):
    return pltpu.CompilerParams(dimension_semantics=sem, vmem_limit_bytes=VMEM_LIMIT)


def _head_ones():
    r = lax.broadcasted_iota(I32, (GROUP_W, GROUP_W), 0) // HEAD_DIM
    c = lax.broadcasted_iota(I32, (GROUP_W, GROUP_W), 1) // HEAD_DIM
    return (r == c).astype(BF16)


def _head_sums(x, ones_bd):
    hi = x.astype(BF16)
    lo = (x - hi.astype(F32)).astype(BF16)
    return (jnp.dot(hi, ones_bd, preferred_element_type=F32)
            + jnp.dot(lo, ones_bd, preferred_element_type=F32))


def _rms(x, g):
    return x * lax.rsqrt(jnp.mean(x * x, axis=-1, keepdims=True) + EPS) * g


def _mm(a, b):
    return jnp.dot(a.astype(BF16), b.astype(BF16), preferred_element_type=F32)


def _mm_nt(a, b):
    return lax.dot_general(a.astype(BF16), b.astype(BF16), (((1,), (1,)), ((), ())),
                           preferred_element_type=F32)


def _mm_tn(a, b):
    return lax.dot_general(a.astype(BF16), b.astype(BF16), (((0,), (0,)), ((), ())),
                           preferred_element_type=F32)


def _split(a):
    hi = a.astype(BF16)
    return hi, (a - hi.astype(F32)).astype(BF16)


def _mm3(a, b):
    ah, al = _split(a)
    bh, bl = _split(b)
    d = functools.partial(jnp.dot, preferred_element_type=F32)
    return d(ah, bh) + (d(ah, bl) + d(al, bh))


def _inproj_body(x_ref, g_ref, wb_ref, ws_ref, big_ref, small_ref):
    xn = _rms(x_ref[...], g_ref[...]).astype(BF16)
    big_ref[...] = jnp.dot(xn, wb_ref[...], preferred_element_type=F32)
    small_ref[...] = jnp.dot(xn, ws_ref[...], preferred_element_type=F32)


def _inproj(x2d, g, wb, ws, tm):
    T, D = x2d.shape
    nb = wb.shape[1]
    return pl.pallas_call(
        _inproj_body, grid=(T // tm,),
        in_specs=[pl.BlockSpec((tm, D), lambda i: (i, 0)),
                  pl.BlockSpec((1, D), lambda i: (0, 0)),
                  pl.BlockSpec((D, nb), lambda i: (0, 0)),
                  pl.BlockSpec((D, LANES), lambda i: (0, 0))],
        out_specs=[pl.BlockSpec((tm, nb), lambda i: (i, 0)),
                   pl.BlockSpec((tm, LANES), lambda i: (i, 0))],
        out_shape=[jax.ShapeDtypeStruct((T, nb), F32), jax.ShapeDtypeStruct((T, LANES), F32)],
        compiler_params=_params("parallel"), name="inproj")(x2d, g, wb, ws)


def _gates_body(sm_ref, p_ref, col_ref, row_ref, carry_ref, *, ts):
    @pl.when(pl.program_id(1) == 0)
    def _():
        carry_ref[...] = jnp.zeros_like(carry_ref)

    x = sm_ref[...]
    lane = lax.broadcasted_iota(I32, (ts, LANES), 1)
    is_fox = lane < LANE_DECAY
    is_dec = (lane >= LANE_DECAY) & (lane < LANE_BETA)
    is_beta = (lane >= LANE_BETA) & (lane < LANE_BETA + N_HEADS)
    logf = jax.nn.log_sigmoid(x + p_ref[0:1, :])
    g = -jnp.exp(p_ref[2:3, :]) * jax.nn.softplus(x + p_ref[1:2, :])
    beta = jax.nn.sigmoid(x)
    r = lax.broadcasted_iota(I32, (ts, ts), 0)
    c = lax.broadcasted_iota(I32, (ts, ts), 1)
    tri_full = (r >= c).astype(BF16)
    tri_chunk = ((r >= c) & (r // CHUNK == c // CHUNK)).astype(BF16)
    vals = jnp.where(is_fox, logf, jnp.where(is_dec, g, 0.0))
    hi = vals.astype(BF16)
    mid = (vals - hi.astype(F32)).astype(BF16)
    lo = (vals - hi.astype(F32) - mid.astype(F32)).astype(BF16)
    parts = jnp.concatenate([hi, mid, lo], axis=1)

    def tri_sum(tri):
        t = jnp.dot(tri, parts, preferred_element_type=F32)
        return t[:, :LANES] + (t[:, LANES:2 * LANES] + t[:, 2 * LANES:])

    cf = tri_sum(tri_full) + carry_ref[...]
    cg = tri_sum(tri_chunk)
    carry_ref[...] = cf[ts - 1:ts, :]
    out = jnp.where(is_fox, cf, jnp.where(is_dec, cg, jnp.where(is_beta, beta, 0.0)))
    col_ref[...] = out
    row_ref[...] = out.T[:2 * SUBLANES, :]


def _gates(small3, prm, ts):
    B, S, _ = small3.shape
    return pl.pallas_call(
        functools.partial(_gates_body, ts=ts), grid=(B, S // ts),
        in_specs=[pl.BlockSpec((None, ts, LANES), lambda b, j: (b, j, 0)),
                  pl.BlockSpec((SUBLANES, LANES), lambda b, j: (0, 0))],
        out_specs=[pl.BlockSpec((None, ts, LANES), lambda b, j: (b, j, 0)),
                   pl.BlockSpec((None, 2 * SUBLANES, ts), lambda b, j: (b, 0, j))],
        out_shape=[jax.ShapeDtypeStruct((B, S, LANES), F32),
                   jax.ShapeDtypeStruct((B, 2 * SUBLANES, S), F32)],
        scratch_shapes=[pltpu.VMEM((1, LANES), F32)],
        compiler_params=_params("parallel", "arbitrary"), name="gates")(small3, prm)


def _fox_body(q_ref, k_ref, v_ref, ccol_ref, qg_ref, kg_ref, og_ref, o_ref,
              kn_scr, vt_scr, q_scr, m_scr, l_scr, acc_scr, s_scr, *, tq, nk):
    i = pl.program_id(1)
    ones_bd = _head_ones()
    lane = lax.broadcasted_iota(I32, (tq, LANES), 1)
    log2e = 1.0 / np.log(2.0)

    def head_norm(x, g):
        ss = _head_sums(x * x, ones_bd)
        return x * lax.rsqrt(ss * (1.0 / HEAD_DIM) + EPS) * g

    def head_tile(x, h, extra):
        pair = x[:, (h // 2) * LANES:(h // 2 + 1) * LANES]
        if h % 2:
            pair = pltpu.roll(pair, HEAD_DIM, 1)
        return jnp.where(lane < HEAD_DIM, pair, extra).astype(BF16)

    @pl.when(i == 0)
    def _():
        for c in range(nk):
            rows = slice(c * tq, (c + 1) * tq)
            kc = head_norm(k_ref[rows, :], kg_ref[...])
            vt = v_ref[rows, :].T.astype(BF16)
            cc = ccol_ref[rows, :] * (-log2e)
            for h in range(N_HEADS):
                cj = cc[:, LANE_FOX + h:LANE_FOX + h + 1]
                hi = cj.astype(BF16).astype(F32)
                mid = (cj - hi).astype(BF16).astype(F32)
                lo = (cj - hi - mid).astype(BF16).astype(F32)
                extra = jnp.where(lane == HEAD_DIM, hi, jnp.where(lane == HEAD_DIM + 1, mid,
                                  jnp.where(lane == HEAD_DIM + 2, lo, 0.0)))
                kn_scr[h, c] = head_tile(kc, h, extra)
                vt_scr[h, c] = vt[h * HEAD_DIM:(h + 1) * HEAD_DIM, :]

    qn = head_norm(q_ref[...], qg_ref[...]) * (HEAD_DIM ** -0.5 * log2e)
    q_ones = jnp.where(lane < HEAD_DIM + 3, 1.0, 0.0)
    for h in range(N_HEADS):
        q_scr[h] = head_tile(qn, h, q_ones)
    m_scr[...] = jnp.full_like(m_scr, -jnp.inf)
    l_scr[...] = jnp.zeros_like(l_scr)
    acc_scr[...] = jnp.zeros_like(acc_scr)
    causal = (lax.broadcasted_iota(I32, (tq, tq), 0) <= lax.broadcasted_iota(I32, (tq, tq), 1))

    heads = range(N_HEADS)

    def scores(j):
        return [lax.dot_general(kn_scr[h, j], q_scr[h], (((1,), (1,)), ((), ())),
                                preferred_element_type=F32) for h in heads]

    def stash(s):
        for h in heads:
            s_scr[h] = s[h]

    def absorb(j, masked):
        p, alpha = [], []
        for h in heads:
            s = s_scr[h]
            if masked:
                s = jnp.where(causal, s, -jnp.inf)
            m_old = m_scr[h]
            m_new = jnp.maximum(m_old, jnp.max(s, axis=0, keepdims=True))
            alpha.append(jnp.exp2(m_old - m_new))
            ph = jnp.exp2(s - m_new)
            l_scr[h] = alpha[h] * l_scr[h] + jnp.sum(ph, axis=0, keepdims=True)
            m_scr[h] = m_new
            p.append(ph.astype(BF16))
        pv = [jnp.dot(vt_scr[h, j], p[h], preferred_element_type=F32) for h in heads]
        for h in heads:
            acc_scr[h] = alpha[h] * acc_scr[h] + pv[h]

    stash(scores(0))

    def body(j, c):
        s_next = scores(j + 1)
        absorb(j, False)
        stash(s_next)
        return c

    lax.fori_loop(0, i, body, 0)
    absorb(i, True)
    o_t = jnp.concatenate([acc_scr[h] / l_scr[h] for h in range(N_HEADS)], axis=0)
    o_ref[...] = _rms(o_t.T, og_ref[...])


def _fox(big3, gcol, qg, kg, og, tq):
    B, S, _ = big3.shape
    nk = S // tq
    row = pl.BlockSpec((1, GROUP_W), lambda b, i: (0, 0))
    return pl.pallas_call(
        functools.partial(_fox_body, tq=tq, nk=nk), grid=(B, nk),
        in_specs=[pl.BlockSpec((None, tq, GROUP_W), lambda b, i: (b, i, COL_FQ)),
                  pl.BlockSpec((None, S, GROUP_W), lambda b, i: (b, 0, COL_FK)),
                  pl.BlockSpec((None, S, GROUP_W), lambda b, i: (b, 0, COL_FV)),
                  pl.BlockSpec((None, S, LANES), lambda b, i: (b, 0, 0)),
                  row, row, row],
        out_specs=pl.BlockSpec((None, tq, GROUP_W), lambda b, i: (b, i, 0)),
        out_shape=jax.ShapeDtypeStruct((B, S, GROUP_W), F32),
        scratch_shapes=[pltpu.VMEM((N_HEADS, nk, tq, LANES), BF16),
                        pltpu.VMEM((N_HEADS, nk, HEAD_DIM, tq), BF16),
                        pltpu.VMEM((N_HEADS, tq, LANES), BF16),
                        pltpu.VMEM((N_HEADS, 1, tq), F32),
                        pltpu.VMEM((N_HEADS, 1, tq), F32),
                        pltpu.VMEM((N_HEADS, HEAD_DIM, tq), F32),
                        pltpu.VMEM((N_HEADS, tq, tq), F32)],
        compiler_params=_params("parallel", "arbitrary"), name="fox")(
            big3, big3, big3, gcol, qg, kg, og)


def _gelu(x):
    return 0.5 * x * (1.0 + lax.erf(x * (2.0 ** -0.5)))


def _gmlp_body(u_ref, v_ref, lg_ref, lb_ref, ws_ref, bst_ref, og_ref, o_ref, *, nwin):
    L = GMLP_LEN
    r = lax.broadcasted_iota(I32, (L, L), 0) // CHUNK
    c = lax.broadcasted_iota(I32, (L, L), 1) // CHUNK
    mask = r >= c
    ws = [jnp.where(mask, ws_ref[h], 0.0).astype(BF16) for h in range(N_HEADS)]
    for n in range(nwin):
        u = _gelu(u_ref[n * L:(n + 1) * L, :])
        v = _gelu(v_ref[n * L:(n + 1) * L, :])
        mu = jnp.mean(v, axis=-1, keepdims=True)
        vc = v - mu
        var = jnp.mean(vc * vc, axis=-1, keepdims=True)
        vn = (vc * lax.rsqrt(var + EPS) * lg_ref[...] + lb_ref[...]).astype(BF16)
        mixed = jnp.concatenate(
            [jnp.dot(ws[h], vn[:, h * HEAD_DIM:(h + 1) * HEAD_DIM], preferred_element_type=F32)
             for h in range(N_HEADS)], axis=-1) + bst_ref[...]
        o_ref[n * L:(n + 1) * L, :] = _rms(u * mixed, og_ref[...])


def _gmlp(big, lg, lb, ws, bst, og, tm):
    T = big.shape[0]
    row = pl.BlockSpec((1, GROUP_W), lambda i: (0, 0))
    return pl.pallas_call(
        functools.partial(_gmlp_body, nwin=tm // GMLP_LEN), grid=(T // tm,),
        in_specs=[pl.BlockSpec((tm, GROUP_W), lambda i: (i, COL_GU)),
                  pl.BlockSpec((tm, GROUP_W), lambda i: (i, COL_GV)),
                  row, row,
                  pl.BlockSpec((N_HEADS, GMLP_LEN, GMLP_LEN), lambda i: (0, 0, 0)),
                  pl.BlockSpec((GMLP_LEN, GROUP_W), lambda i: (0, 0)),
                  row],
        out_specs=pl.BlockSpec((tm, GROUP_W), lambda i: (i, 0)),
        out_shape=jax.ShapeDtypeStruct((T, GROUP_W), F32),
        compiler_params=_params("parallel"), name="gmlp")(big, big, lg, lb, ws, bst, og)


def _gdn_prep_body(q_ref, k_ref, v_ref, hq_ref, hk_ref, hv_ref, w_ref, qo_ref, ko_ref, vo_ref, *, ts):
    first = pl.program_id(1) == 0
    ones_bd = _head_ones()

    def conv(x_ref, halo_ref, w):
        halo = jnp.where(first, 0.0, halo_ref[...])
        xx = jnp.concatenate([halo, x_ref[...]], axis=0)
        y = w[CONV_K - 1:CONV_K, :] * xx[SUBLANES:, :]
        for j in range(CONV_K - 1):
            y = y + w[j:j + 1, :] * pltpu.roll(xx, CONV_K - 1 - j, 0)[SUBLANES:, :]
        return y * jax.nn.sigmoid(y)

    def l2(t):
        return t * lax.rsqrt(_head_sums(t * t, ones_bd) + EPS)

    qo_ref[...] = l2(conv(q_ref, hq_ref, w_ref[0])) * (HEAD_DIM ** -0.5)
    ko_ref[...] = l2(conv(k_ref, hk_ref, w_ref[1]))
    vo_ref[...] = conv(v_ref, hv_ref, w_ref[2])


def _gdn_prep(big3, conv_w3, ts):
    B, S, _ = big3.shape
    hb = ts // SUBLANES
    blk = lambda col: pl.BlockSpec((None, ts, GROUP_W), lambda b, i: (b, i, col))
    halo = lambda col: pl.BlockSpec((None, SUBLANES, GROUP_W),
                                    lambda b, i: (b, jnp.maximum(i * hb - 1, 0), col))
    out = pl.BlockSpec((None, ts, GROUP_W), lambda b, i: (b, i, 0))
    shp = jax.ShapeDtypeStruct((B, S, GROUP_W), F32)
    return pl.pallas_call(
        functools.partial(_gdn_prep_body, ts=ts), grid=(B, S // ts),
        in_specs=[blk(COL_DQ), blk(COL_DK), blk(COL_DV), halo(COL_DQ), halo(COL_DK), halo(COL_DV),
                  pl.BlockSpec((3, CONV_K, GROUP_W), lambda b, i: (0, 0, 0))],
        out_specs=[out, out, out], out_shape=[shp, shp, shp],
        compiler_params=_params("parallel", "parallel"), name="gdn_prep")(
            big3, big3, big3, big3, big3, big3, conv_w3)


def _gdn_body(q_ref, k_ref, v_ref, gate_ref, gcol_ref, grow_ref, ng_ref, o_ref,
              s_scr, u_scr, wq_scr, a_scr, kd_scr, dl_scr, t_scr, p_scr, rhs_scr, *, nchunk):
    C = CHUNK

    @pl.when(pl.program_id(1) == 0)
    def _():
        s_scr[...] = jnp.zeros_like(s_scr)

    r = lax.broadcasted_iota(I32, (C, C), 0)
    c = lax.broadcasted_iota(I32, (C, C), 1)
    tri = r >= c
    strict = r > c
    eye = (r == c).astype(F32)

    items = [(n, h) for n in range(nchunk) for h in range(N_HEADS)]
    mmb = functools.partial(jnp.dot, preferred_element_type=F32)
    for it, (n, h) in enumerate(items):
        rows = slice(n * C, (n + 1) * C)
        sl = slice(h * HEAD_DIM, (h + 1) * HEAD_DIM)
        q, k, v = q_ref[rows, sl], k_ref[rows, sl], v_ref[rows, sl]
        gc = gcol_ref[rows, LANE_DECAY + h:LANE_DECAY + h + 1]
        gr = grow_ref[n, LANE_DECAY + h:LANE_DECAY + h + 1, :]
        beta = gcol_ref[rows, LANE_BETA + h:LANE_BETA + h + 1]
        decay = jnp.exp(jnp.where(tri, gc - gr, -jnp.inf))
        kb = k * beta
        kk = _mm_nt(jnp.concatenate([kb, q], axis=0), k)
        x = jnp.where(strict, -(kk[:C] * decay), 0.0)
        t_scr[it] = eye + x
        p_scr[it] = x.astype(BF16)
        eg = jnp.exp(gc)
        g_last = gc[C - 1:C, :]
        rhs_scr[it] = jnp.concatenate([v * beta, kb * eg], axis=1).astype(BF16)
        wq_scr[n, h, C:, :] = (q * eg).astype(BF16)
        a_scr[n, h] = jnp.where(tri, kk[C:] * decay, 0.0).astype(BF16)
        kd_scr[n, h] = (k * jnp.exp(g_last - gc)).astype(BF16)
        dl_scr[n, h] = jnp.broadcast_to(jnp.exp(g_last), (1, HEAD_DIM))
    for _ in range(5):
        for it in range(len(items)):
            p = p_scr[it]
            p_scr[it] = mmb(p, p).astype(BF16)
        for it in range(len(items)):
            t = t_scr[it]
            t_scr[it] = t + mmb(t.astype(BF16), p_scr[it])
    for it, (n, h) in enumerate(items):
        uw = mmb(t_scr[it].astype(BF16), rhs_scr[it])
        u_scr[n, h] = uw[:, :HEAD_DIM]
        wq_scr[n, h, :C, :] = uw[:, HEAD_DIM:].astype(BF16)

    heads = range(N_HEADS)
    state = [s_scr[h] for h in heads]
    for n in range(nchunk):
        rows = slice(n * C, (n + 1) * C)
        ws = [mmb(wq_scr[n, h], state[h].astype(BF16)) for h in heads]
        vb = [(u_scr[n, h] - ws[h][:C]).astype(BF16) for h in heads]
        o = [ws[h][C:] + mmb(a_scr[n, h], vb[h]) for h in heads]
        state = [state[h] * dl_scr[n, h] + lax.dot_general(
            kd_scr[n, h], vb[h], (((0,), (0,)), ((), ())), preferred_element_type=F32) for h in heads]
        gate = gate_ref[rows, :]
        y = jnp.concatenate([_rms(o[h], ng_ref[...]) for h in heads], axis=-1)
        o_ref[rows, :] = y * (gate * jax.nn.sigmoid(gate))
    for h in heads:
        s_scr[h] = state[h]


def _gdn(qn, kn, vv, big3, gcol, grow4, ng, ts):
    B, S, _ = qn.shape
    nchunk = ts // CHUNK
    blk = pl.BlockSpec((None, ts, GROUP_W), lambda b, i: (b, i, 0))
    return pl.pallas_call(
        functools.partial(_gdn_body, nchunk=nchunk), grid=(B, S // ts),
        in_specs=[blk, blk, blk,
                  pl.BlockSpec((None, ts, GROUP_W), lambda b, i: (b, i, COL_DG)),
                  pl.BlockSpec((None, ts, LANES), lambda b, i: (b, i, 0)),
                  pl.BlockSpec((None, nchunk, 2 * SUBLANES, CHUNK), lambda b, i: (b, i, 0, 0)),
                  pl.BlockSpec((1, HEAD_DIM), lambda b, i: (0, 0))],
        out_specs=blk, out_shape=jax.ShapeDtypeStruct((B, S, GROUP_W), F32),
        scratch_shapes=[pltpu.VMEM((N_HEADS, HEAD_DIM, HEAD_DIM), F32),
                        pltpu.VMEM((nchunk, N_HEADS, CHUNK, HEAD_DIM), F32),
                        pltpu.VMEM((nchunk, N_HEADS, 2 * CHUNK, HEAD_DIM), BF16),
                        pltpu.VMEM((nchunk, N_HEADS, CHUNK, CHUNK), BF16),
                        pltpu.VMEM((nchunk, N_HEADS, CHUNK, HEAD_DIM), BF16),
                        pltpu.VMEM((nchunk, N_HEADS, 1, HEAD_DIM), F32),
                        pltpu.VMEM((nchunk * N_HEADS, CHUNK, CHUNK), F32),
                        pltpu.VMEM((nchunk * N_HEADS, CHUNK, CHUNK), BF16),
                        pltpu.VMEM((nchunk * N_HEADS, CHUNK, 2 * HEAD_DIM), BF16)],
        compiler_params=_params("parallel", "arbitrary"), name="gdn")(
            qn, kn, vv, big3, gcol, grow4, ng)


def _pool_body(z_ref, halo_ref, w_ref, sc_ref, og_ref, o_ref, *, ts):
    i = pl.program_id(1)
    hr = 2 * SUBLANES
    z = z_ref[...]
    halo = jnp.where(i == 0, 0.0, halo_ref[...])
    s1 = jnp.concatenate([halo, z], axis=0)
    s2 = s1 + pltpu.roll(s1, 1, 0)
    s4 = s2 + pltpu.roll(s2, 2, 0)
    s8 = s4 + pltpu.roll(s4, 4, 0)
    s16 = s8 + pltpu.roll(s8, 8, 0)
    grp = lax.broadcasted_iota(I32, (ts, GROUP_W), 1) // (GROUP_W // len(POOL_WINDOWS))
    t = lax.broadcasted_iota(I32, (ts, GROUP_W), 0) + i * ts
    total = jnp.where(grp == 0, s2[hr:], jnp.where(grp == 1, s4[hr:], jnp.where(grp == 2, s8[hr:], s16[hr:])))
    win = jnp.where(grp == 0, POOL_WINDOWS[0], jnp.where(grp == 1, POOL_WINDOWS[1],
                    jnp.where(grp == 2, POOL_WINDOWS[2], POOL_WINDOWS[3])))
    pooled = total / jnp.minimum(t + 1, win).astype(F32)
    y = _mm(pooled - z, w_ref[...]) * sc_ref[...]
    o_ref[...] = _rms(y, og_ref[...])


def _pool(big3, wbd, sc, og, ts):
    B, S, _ = big3.shape
    hr = 2 * SUBLANES
    hb = ts // hr
    row = pl.BlockSpec((1, GROUP_W), lambda b, i: (0, 0))
    return pl.pallas_call(
        functools.partial(_pool_body, ts=ts), grid=(B, S // ts),
        in_specs=[pl.BlockSpec((None, ts, GROUP_W), lambda b, i: (b, i, COL_PZ)),
                  pl.BlockSpec((None, hr, GROUP_W), lambda b, i: (b, jnp.maximum(i * hb - 1, 0), COL_PZ)),
                  pl.BlockSpec((GROUP_W, GROUP_W), lambda b, i: (0, 0)), row, row],
        out_specs=pl.BlockSpec((None, ts, GROUP_W), lambda b, i: (b, i, 0)),
        out_shape=jax.ShapeDtypeStruct((B, S, GROUP_W), F32),
        compiler_params=_params("parallel", "parallel"), name="pool")(big3, big3, wbd, sc, og)


def _outproj_body(ya_ref, yb_ref, yc_ref, yd_ref, h_ref, wo_ref, g_ref, rw_ref, rb_ref,
                  hnew_ref, hn_ref, ri_ref, rf_ref, *, tm, d):
    y = jnp.concatenate([ya_ref[...], yb_ref[...], yc_ref[...], yd_ref[...]], axis=-1).astype(BF16)
    h_new = h_ref[...] + jnp.dot(y, wo_ref[...], preferred_element_type=F32)
    hnew_ref[...] = h_new
    hn = _rms(h_new, g_ref[...])
    nseg = d // LANES
    for s in range(nseg):
        hn_ref[pl.ds(s, tm, stride=nseg), :] = hn[:, s * LANES:(s + 1) * LANES]

    hn_hi, hn_lo = _split(hn)
    t = jnp.dot(hn_hi, rw_ref[...], preferred_element_type=F32)
    logits = (t[:, :LANES] + t[:, LANES:]
              + jnp.dot(hn_lo, rw_ref[:, :LANES], preferred_element_type=F32)) + rb_ref[...]
    lane = lax.broadcasted_iota(I32, (tm, LANES), 1)
    neg = -jnp.inf
    big_lane = LANES

    def masked_top(vals, mask):
        v = jnp.where(mask, vals, neg)
        mx = jnp.max(v, axis=-1, keepdims=True)
        idx = jnp.min(jnp.where(mask & (v == mx), lane, big_lane), axis=-1, keepdims=True)
        return v, mx, idx

    gmask = lane < N_GROUPS
    gv, gmx, gidx = masked_top(logits, gmask)
    g_top = 1.0 / jnp.sum(jnp.where(gmask, jnp.exp(gv - gmx), 0.0), axis=-1, keepdims=True)
    lo = N_GROUPS + gidx * EXPERTS_PER_GROUP
    emask = (lane >= lo) & (lane < lo + EXPERTS_PER_GROUP)
    ev, emx, eidx1 = masked_top(logits, emask)
    esum = jnp.sum(jnp.where(emask, jnp.exp(ev - emx), 0.0), axis=-1, keepdims=True)
    p1 = 1.0 / esum
    _, emx2, eidx2 = masked_top(logits, emask & (lane != eidx1))
    p2 = jnp.exp(emx2 - emx) / esum
    denom = p1 + p2
    ri_ref[...] = jnp.where(lane == 0, eidx1 - N_GROUPS, jnp.where(lane == 1, eidx2 - N_GROUPS, 0))
    rf_ref[...] = jnp.where(lane == 0, g_top * p1 / denom, jnp.where(lane == 1, g_top * p2 / denom, 0.0))


def _outproj(ya, yb, yc, yd, h2d, wo, g, rw, rb, tm):
    T, D = h2d.shape
    nseg = D // LANES
    yblk = pl.BlockSpec((tm, GROUP_W), lambda i: (i, 0))
    return pl.pallas_call(
        functools.partial(_outproj_body, tm=tm, d=D), grid=(T // tm,),
        in_specs=[yblk, yblk, yblk, yblk,
                  pl.BlockSpec((tm, D), lambda i: (i, 0)),
                  pl.BlockSpec((D, D), lambda i: (0, 0)),
                  pl.BlockSpec((1, D), lambda i: (0, 0)),
                  pl.BlockSpec((D, 2 * LANES), lambda i: (0, 0)),
                  pl.BlockSpec((1, LANES), lambda i: (0, 0))],
        out_specs=[pl.BlockSpec((tm, D), lambda i: (i, 0)),
                   pl.BlockSpec((tm * nseg, LANES), lambda i: (i, 0)),
                   pl.BlockSpec((tm, LANES), lambda i: (i, 0)),
                   pl.BlockSpec((tm, LANES), lambda i: (i, 0))],
        out_shape=[jax.ShapeDtypeStruct((T, D), F32),
                   jax.ShapeDtypeStruct((T * nseg, LANES), F32),
                   jax.ShapeDtypeStruct((T, LANES), I32),
                   jax.ShapeDtypeStruct((T, LANES), F32)],
        compiler_params=_params("parallel"), name="outproj")(ya, yb, yc, yd, h2d, wo, g, rw, rb)


def _rank_body(ri_ref, rank_ref, cnt_ref, carry_ref, *, tm):
    @pl.when(pl.program_id(0) == 0)
    def _():
        carry_ref[...] = jnp.zeros_like(carry_ref)

    lane = lax.broadcasted_iota(I32, (tm, LANES), 1)
    e = ri_ref[...]
    oh = [lane == e[:, s:s + 1] for s in range(TOPK_IN)]
    m = (oh[0].astype(F32) + oh[1].astype(F32)).astype(BF16)
    below = (lax.broadcasted_iota(I32, (tm, tm), 0) > lax.broadcasted_iota(I32, (tm, tm), 1)).astype(BF16)
    before = jnp.dot(below, m, preferred_element_type=F32) + carry_ref[...]
    rank = [jnp.sum(jnp.where(oh[s], before, 0.0), axis=-1, keepdims=True) for s in range(TOPK_IN)]
    rank_ref[...] = jnp.where(lane == 0, rank[0], jnp.where(lane == 1, rank[1], 0.0)).astype(I32)
    total = before[tm - 1:tm, :] + m[tm - 1:tm, :].astype(F32)
    carry_ref[...] = total
    cnt_ref[...] = jnp.broadcast_to(total, cnt_ref.shape)


def _rank(ri, tm):
    T = ri.shape[0]
    return pl.pallas_call(
        functools.partial(_rank_body, tm=tm), grid=(T // tm,),
        in_specs=[pl.BlockSpec((tm, LANES), lambda i: (i, 0))],
        out_specs=[pl.BlockSpec((tm, LANES), lambda i: (i, 0)),
                   pl.BlockSpec((SUBLANES, LANES), lambda i: (0, 0))],
        out_shape=[jax.ShapeDtypeStruct((T, LANES), I32), jax.ShapeDtypeStruct((SUBLANES, LANES), F32)],
        scratch_shapes=[pltpu.VMEM((1, LANES), F32)],
        compiler_params=_params("arbitrary"), name="rank")(ri)


def _dispatch_plan(expert, rank, counts):
    T = expert.shape[0]
    counts = counts.astype(I32)
    padded = (counts + MOE_BLOCK - 1) // MOE_BLOCK * MOE_BLOCK
    pad_end = jnp.cumsum(padded)
    pad_start = pad_end - padded
    onehot = expert[:, :, None] == jnp.arange(N_EXPERTS, dtype=I32)[None, None, :]
    dest = rank + jnp.sum(jnp.where(onehot, pad_start[None, None, :], 0), axis=-1)
    n_blk = -(-T * TOPK_IN // MOE_BLOCK) + N_EXPERTS
    blk_start = jnp.arange(n_blk, dtype=I32) * MOE_BLOCK
    blk_e = jnp.minimum(jnp.sum(pad_end[None, :] <= blk_start[:, None], axis=-1), N_EXPERTS - 1).astype(I32)
    n_used = (pad_end[-1] // MOE_BLOCK).astype(I32).reshape(1)
    return dest.astype(I32), blk_e, n_used, n_blk


def _row_tile(ref, r, nseg):
    return ref.at[pl.ds(pl.multiple_of(r * nseg, nseg), nseg), :]


def _dispatch_body(dst_ref, hn_ref, xz_hbm, x_hbm, sem, *, tm, nseg, unroll):
    del xz_hbm

    def issue(g, c):
        for u in range(unroll):
            r = g * unroll + u
            for s in range(TOPK_IN):
                pltpu.make_async_copy(_row_tile(hn_ref, r, nseg),
                                      _row_tile(x_hbm, dst_ref[0, 0, r * TOPK_IN + s], nseg), sem).start()
        return c

    lax.fori_loop(0, tm // unroll, issue, 0)
    for s in range(TOPK_IN):
        pltpu.make_async_copy(hn_ref, x_hbm.at[pl.ds(0, tm * nseg), :], sem).wait()


def _dispatch(dest3, hn_rows, n_pad, tm):
    nt = dest3.shape[0]
    nseg = hn_rows.shape[0] // (nt * tm)
    x_zero = jnp.zeros((n_pad * nseg, LANES), F32)
    return pl.pallas_call(
        functools.partial(_dispatch_body, tm=tm, nseg=nseg, unroll=8), grid=(nt,),
        in_specs=[pl.BlockSpec((1, 1, tm * TOPK_IN), lambda i: (i, 0, 0), memory_space=pltpu.SMEM),
                  pl.BlockSpec((tm * nseg, LANES), lambda i: (i, 0)),
                  pl.BlockSpec(memory_space=pl.ANY)],
        out_specs=pl.BlockSpec(memory_space=pl.ANY),
        out_shape=jax.ShapeDtypeStruct((n_pad * nseg, LANES), F32),
        scratch_shapes=[pltpu.SemaphoreType.DMA],
        input_output_aliases={2: 0},
        compiler_params=_params("arbitrary"), name="dispatch")(dest3, hn_rows, x_zero)


def _moe_body(blk_e_ref, n_used_ref, x_ref, w1_ref, w3_ref, w2_ref, y_ref, w1b, w3b, w2b, *, nseg):
    b = pl.program_id(0)
    R = MOE_BLOCK

    @pl.when(b < n_used_ref[0])
    def _():
        @pl.when((b == 0) | (blk_e_ref[b] != blk_e_ref[jnp.maximum(b - 1, 0)]))
        def _():
            w1b[...] = w1_ref[...].astype(BF16)
            w3b[...] = w3_ref[...].astype(BF16)
            w2b[...] = w2_ref[...].astype(BF16)

        x = jnp.concatenate([x_ref[pl.ds(s, R, stride=nseg), :] for s in range(nseg)], axis=-1).astype(BF16)
        a = jnp.dot(x, w1b[...], preferred_element_type=F32)
        g = jnp.dot(x, w3b[...], preferred_element_type=F32)
        hid = (a * jax.nn.sigmoid(a) * g).astype(BF16)
        y = jnp.dot(hid, w2b[...], preferred_element_type=F32)
        for s in range(nseg):
            y_ref[pl.ds(s, R, stride=nseg), :] = y[:, s * LANES:(s + 1) * LANES]

    @pl.when(b >= n_used_ref[0])
    def _():
        y_ref[...] = jnp.zeros_like(y_ref)


def _moe(blk_e, n_used, x_rows, w1, w3, w2, layer):
    n_blk = blk_e.shape[0]
    _, _, D, DE = w1.shape
    nseg = D // LANES
    R = MOE_BLOCK
    rows = lambda b, be, nu: (jnp.minimum(b, nu[0] - 1), 0)
    wmap = lambda b, be, nu: (layer, be[b], 0, 0)
    grid_spec = pltpu.PrefetchScalarGridSpec(
        num_scalar_prefetch=2, grid=(n_blk,),
        in_specs=[pl.BlockSpec((R * nseg, LANES), rows),
                  pl.BlockSpec((None, None, D, DE), wmap),
                  pl.BlockSpec((None, None, D, DE), wmap),
                  pl.BlockSpec((None, None, DE, D), wmap)],
        out_specs=pl.BlockSpec((R * nseg, LANES), lambda b, be, nu: (b, 0)),
        scratch_shapes=[pltpu.VMEM((D, DE), BF16), pltpu.VMEM((D, DE), BF16), pltpu.VMEM((DE, D), BF16)])
    return pl.pallas_call(
        functools.partial(_moe_body, nseg=nseg), grid_spec=grid_spec,
        out_shape=jax.ShapeDtypeStruct(x_rows.shape, F32),
        compiler_params=_params("arbitrary"), name="moe")(blk_e, n_used, x_rows, w1, w3, w2)


def _combine_body(dst_ref, h_ref, rf_ref, y_hbm, o_ref, ybuf, sem, *, tm, nseg, unroll):
    def issue(g, c):
        for u in range(unroll):
            a = g * unroll + u
            pltpu.make_async_copy(_row_tile(y_hbm, dst_ref[0, 0, a], nseg), _row_tile(ybuf, a, nseg), sem).start()
        return c

    lax.fori_loop(0, tm * TOPK_IN // unroll, issue, 0)
    pltpu.make_async_copy(y_hbm.at[pl.ds(0, tm * TOPK_IN * nseg), :], ybuf, sem).wait()
    stride = TOPK_IN * nseg
    acc = h_ref[...]
    for s in range(TOPK_IN):
        ys = jnp.concatenate([ybuf[pl.ds(s * nseg + k, tm, stride=stride), :] for k in range(nseg)], axis=-1)
        acc = acc + ys * rf_ref[:, s:s + 1]
    o_ref[...] = acc


def _combine(dest3, h2d, rf, y_rows, tm):
    T, D = h2d.shape
    nseg = D // LANES
    return pl.pallas_call(
        functools.partial(_combine_body, tm=tm, nseg=nseg, unroll=8), grid=(T // tm,),
        in_specs=[pl.BlockSpec((1, 1, tm * TOPK_IN), lambda i: (i, 0, 0), memory_space=pltpu.SMEM),
                  pl.BlockSpec((tm, D), lambda i: (i, 0)),
                  pl.BlockSpec((tm, LANES), lambda i: (i, 0)),
                  pl.BlockSpec(memory_space=pl.ANY)],
        out_specs=pl.BlockSpec((tm, D), lambda i: (i, 0)),
        out_shape=jax.ShapeDtypeStruct((T, D), F32),
        scratch_shapes=[pltpu.VMEM((tm * TOPK_IN * nseg, LANES), F32), pltpu.SemaphoreType.DMA],
        compiler_params=_params("arbitrary"), name="combine")(dest3, h2d, rf, y_rows)


def _pad_lanes(a, lane0, rows=1):
    out = jnp.zeros((rows, LANES), a.dtype)
    return out.at[:, lane0:lane0 + a.shape[-1]].set(a.reshape(rows, -1))


def _layer(h2d, B, S, p):
    T, D = h2d.shape
    tile = lambda a, n: jnp.tile(a.reshape(1, -1), (1, n))

    offs = np.cumsum([0, GROUP_W, GROUP_W, GROUP_W, N_HEADS, GROUP_W, GROUP_W,
                      GROUP_W, GROUP_W, GROUP_W, N_HEADS, N_HEADS, GROUP_W, GROUP_W])
    seg = lambda k: p['w_in'][:, offs[k]:offs[k + 1]]
    wb = jnp.concatenate([seg(0), seg(1), seg(2), seg(4), seg(5), seg(6), seg(7), seg(8), seg(11), seg(12)],
                         axis=1).astype(BF16)
    ws = jnp.zeros((D, LANES), F32)
    ws = ws.at[:, LANE_FOX:LANE_FOX + N_HEADS].set(seg(3))
    ws = ws.at[:, LANE_DECAY:LANE_DECAY + N_HEADS].set(seg(9))
    ws = ws.at[:, LANE_BETA:LANE_BETA + N_HEADS].set(seg(10)).astype(BF16)
    gate_prm = jnp.concatenate([_pad_lanes(p['fox_f_bias'], LANE_FOX), _pad_lanes(p['gdn_dt_bias'], LANE_DECAY),
                                _pad_lanes(p['gdn_a_log'], LANE_DECAY), jnp.zeros((SUBLANES - 3, LANES), F32)], axis=0)

    big, small = _inproj(h2d, p['attn_norm_g'].reshape(1, D), wb, ws, tm=min(512, T))
    big3 = big.reshape(B, S, N_BIG_COLS * GROUP_W)
    ts = min(512, S)
    gcol, grow = _gates(small.reshape(B, S, LANES), gate_prm, ts)

    tq = min(256, S)
    ya = _fox(big3, gcol, tile(p['fox_qn_g'], N_HEADS), tile(p['fox_kn_g'], N_HEADS),
              p['fox_out_g'].reshape(1, GROUP_W), tq)

    bst = jnp.repeat(p['gmlp_bs'].T, HEAD_DIM, axis=1)
    yb = _gmlp(big, p['gmlp_ln_g'].reshape(1, -1), p['gmlp_ln_b'].reshape(1, -1), p['gmlp_ws'], bst,
               p['gmlp_out_g'].reshape(1, -1), tm=min(512, T))

    conv_w3 = p['gdn_conv_w'].reshape(CONV_K, 3, GROUP_W).transpose(1, 0, 2)
    qn, kn, vv = _gdn_prep(big3, conv_w3, ts)
    grow4 = grow.reshape(B, 2 * SUBLANES, S // CHUNK, CHUNK).transpose(0, 2, 1, 3)
    yc = _gdn(qn, kn, vv, big3, gcol, grow4, p['gdn_norm_g'].reshape(1, HEAD_DIM), ts)

    wbd = jax.scipy.linalg.block_diag(*[p['pool_w'][g] for g in range(len(POOL_WINDOWS))]).astype(BF16)
    yd = _pool(big3, wbd, p['pool_scale'].reshape(1, -1), p['pool_out_g'].reshape(1, -1), ts)

    rw = jnp.zeros((D, LANES), F32).at[:, :N_GROUPS].set(p['router_g_w'])
    rw = rw.at[:, N_GROUPS:N_GROUPS + N_EXPERTS].set(p['router_e_w'])
    rw = jnp.concatenate(_split(rw), axis=1)
    rb = jnp.zeros((1, LANES), F32).at[0, :N_GROUPS].set(p['router_g_b'])
    rb = rb.at[0, N_GROUPS:N_GROUPS + N_EXPERTS].set(p['router_e_b'])
    flat = lambda a: a.reshape(T, GROUP_W)
    h_new, hn_rows, ri, rf = _outproj(flat(ya), yb, flat(yc), flat(yd), h2d, p['w_out'].astype(BF16),
                                      p['ffn_norm_g'].reshape(1, D), rw, rb, tm=min(512, T))

    rank, cnt = _rank(ri, tm=min(512, T))
    dest, blk_e, n_used, n_blk = _dispatch_plan(ri[:, :TOPK_IN], rank[:, :TOPK_IN], cnt[0, :N_EXPERTS])
    tmd = min(256, T)
    dest3 = dest.reshape(T // tmd, 1, tmd * TOPK_IN)
    x_rows = _dispatch(dest3, hn_rows, n_blk * MOE_BLOCK, tmd)
    y_rows = _moe(blk_e, n_used, x_rows, p['moe_w1'], p['moe_w3'], p['moe_w2'], p['layer'])
    return _combine(dest3, h_new, rf, y_rows, tmd)


def kernel(x, attn_norm_g, w_in, w_out, fox_f_bias, fox_qn_g, fox_kn_g, fox_out_g, gmlp_ln_g, gmlp_ln_b, gmlp_ws, gmlp_bs, gmlp_out_g, gdn_conv_w, gdn_a_log, gdn_dt_bias, gdn_norm_g, pool_w, pool_scale, pool_out_g, ffn_norm_g, router_g_w, router_g_b, router_e_w, router_e_b, moe_w1, moe_w3, moe_w2):
    B, S, D = x.shape
    names = ('attn_norm_g', 'w_in', 'w_out', 'fox_f_bias', 'fox_qn_g', 'fox_kn_g', 'fox_out_g', 'gmlp_ln_g',
             'gmlp_ln_b', 'gmlp_ws', 'gmlp_bs', 'gmlp_out_g', 'gdn_conv_w', 'gdn_a_log', 'gdn_dt_bias',
             'gdn_norm_g', 'pool_w', 'pool_scale', 'pool_out_g', 'ffn_norm_g', 'router_g_w', 'router_g_b',
             'router_e_w', 'router_e_b', 'moe_w1', 'moe_w3', 'moe_w2')
    vals = (attn_norm_g, w_in, w_out, fox_f_bias, fox_qn_g, fox_kn_g, fox_out_g, gmlp_ln_g, gmlp_ln_b, gmlp_ws,
            gmlp_bs, gmlp_out_g, gdn_conv_w, gdn_a_log, gdn_dt_bias, gdn_norm_g, pool_w, pool_scale, pool_out_g,
            ffn_norm_g, router_g_w, router_g_b, router_e_w, router_e_b, moe_w1, moe_w3, moe_w2)
    h = x.reshape(B * S, D)
    stacked = ('moe_w1', 'moe_w3', 'moe_w2')
    for l in range(w_in.shape[0]):
        p = {n: (v if n in stacked else v[l]) for n, v in zip(names, vals)}
        p['layer'] = l
        h = _layer(h, B, S, p)
    return h.reshape(B, S, D)
```

```python
import functools

import jax
import jax.numpy as jnp
import numpy as np
from jax import lax
from jax.experimental import pallas as pl
from jax.experimental.pallas import tpu as pltpu

F32 = jnp.float32
BF16 = jnp.bfloat16
I32 = jnp.int32

EPS = 1e-6
HEAD_DIM = 64
GROUP_W = 256
N_HEADS = GROUP_W // HEAD_DIM
CHUNK = 64
GMLP_LEN = 128
CONV_K = 4
POOL_WINDOWS = (2, 4, 8, 16)
N_GROUPS = 4
EXPERTS_PER_GROUP = 8
N_EXPERTS = N_GROUPS * EXPERTS_PER_GROUP
TOPK_IN = 2
MOE_BLOCK = 256
RUN_ROWS = 8
LANES = 128
SUBLANES = 8
VMEM_LIMIT = 56 * 1024 * 1024

COL_FQ, COL_FK, COL_FV, COL_GU, COL_GV, COL_DQ, COL_DK, COL_DV, COL_DG, COL_PZ = range(10)
N_BIG_COLS = 10
LANE_FOX, LANE_DECAY, LANE_BETA = 0, 4, 8


def _params(*sem):
    return pltpu.CompilerParams(dimension_semantics=sem, vmem_limit_bytes=VMEM_LIMIT)


def _head_ones():
    r = lax.broadcasted_iota(I32, (GROUP_W, GROUP_W), 0) // HEAD_DIM
    c = lax.broadcasted_iota(I32, (GROUP_W, GROUP_W), 1) // HEAD_DIM
    return (r == c).astype(BF16)


def _head_sums(x, ones_bd):
    hi = x.astype(BF16)
    lo = (x - hi.astype(F32)).astype(BF16)
    return (jnp.dot(hi, ones_bd, preferred_element_type=F32)
            + jnp.dot(lo, ones_bd, preferred_element_type=F32))


def _rms(x, g):
    return x * lax.rsqrt(jnp.mean(x * x, axis=-1, keepdims=True) + EPS) * g


def _mm(a, b):
    return jnp.dot(a.astype(BF16), b.astype(BF16), preferred_element_type=F32)


def _mm_nt(a, b):
    return lax.dot_general(a.astype(BF16), b.astype(BF16), (((1,), (1,)), ((), ())),
                           preferred_element_type=F32)


def _mm_tn(a, b):
    return lax.dot_general(a.astype(BF16), b.astype(BF16), (((0,), (0,)), ((), ())),
                           preferred_element_type=F32)


def _split(a):
    hi = a.astype(BF16)
    return hi, (a - hi.astype(F32)).astype(BF16)


def _mm3(a, b):
    ah, al = _split(a)
    bh, bl = _split(b)
    d = functools.partial(jnp.dot, preferred_element_type=F32)
    return d(ah, bh) + (d(ah, bl) + d(al, bh))


def _inproj_body(x_ref, g_ref, wb_ref, ws_ref, big_ref, small_ref):
    xn = _rms(x_ref[...], g_ref[...]).astype(BF16)
    big_ref[...] = jnp.dot(xn, wb_ref[...], preferred_element_type=F32)
    small_ref[...] = jnp.dot(xn, ws_ref[...], preferred_element_type=F32)


def _inproj(x2d, g, wb, ws, tm):
    T, D = x2d.shape
    nb = wb.shape[1]
    return pl.pallas_call(
        _inproj_body, grid=(T // tm,),
        in_specs=[pl.BlockSpec((tm, D), lambda i: (i, 0)),
                  pl.BlockSpec((1, D), lambda i: (0, 0)),
                  pl.BlockSpec((D, nb), lambda i: (0, 0)),
                  pl.BlockSpec((D, LANES), lambda i: (0, 0))],
        out_specs=[pl.BlockSpec((tm, nb), lambda i: (i, 0)),
                   pl.BlockSpec((tm, LANES), lambda i: (i, 0))],
        out_shape=[jax.ShapeDtypeStruct((T, nb), F32), jax.ShapeDtypeStruct((T, LANES), F32)],
        compiler_params=_params("parallel"), name="inproj")(x2d, g, wb, ws)


def _gates_body(sm_ref, p_ref, col_ref, row_ref, carry_ref, *, ts):
    @pl.when(pl.program_id(1) == 0)
    def _():
        carry_ref[...] = jnp.zeros_like(carry_ref)

    x = sm_ref[...]
    lane = lax.broadcasted_iota(I32, (ts, LANES), 1)
    is_fox = lane < LANE_DECAY
    is_dec = (lane >= LANE_DECAY) & (lane < LANE_BETA)
    is_beta = (lane >= LANE_BETA) & (lane < LANE_BETA + N_HEADS)
    logf = jax.nn.log_sigmoid(x + p_ref[0:1, :])
    g = -jnp.exp(p_ref[2:3, :]) * jax.nn.softplus(x + p_ref[1:2, :])
    beta = jax.nn.sigmoid(x)
    r = lax.broadcasted_iota(I32, (ts, ts), 0)
    c = lax.broadcasted_iota(I32, (ts, ts), 1)
    tri_full = (r >= c).astype(BF16)
    tri_chunk = ((r >= c) & (r // CHUNK == c // CHUNK)).astype(BF16)
    vals = jnp.where(is_fox, logf, jnp.where(is_dec, g, 0.0))
    hi = vals.astype(BF16)
    mid = (vals - hi.astype(F32)).astype(BF16)
    lo = (vals - hi.astype(F32) - mid.astype(F32)).astype(BF16)
    parts = jnp.concatenate([hi, mid, lo], axis=1)

    def tri_sum(tri):
        t = jnp.dot(tri, parts, preferred_element_type=F32)
        return t[:, :LANES] + (t[:, LANES:2 * LANES] + t[:, 2 * LANES:])

    cf = tri_sum(tri_full) + carry_ref[...]
    cg = tri_sum(tri_chunk)
    carry_ref[...] = cf[ts - 1:ts, :]
    out = jnp.where(is_fox, cf, jnp.where(is_dec, cg, jnp.where(is_beta, beta, 0.0)))
    col_ref[...] = out
    row_ref[...] = out.T[:2 * SUBLANES, :]


def _gates(small3, prm, ts):
    B, S, _ = small3.shape
    return pl.pallas_call(
        functools.partial(_gates_body, ts=ts), grid=(B, S // ts),
        in_specs=[pl.BlockSpec((None, ts, LANES), lambda b, j: (b, j, 0)),
                  pl.BlockSpec((SUBLANES, LANES), lambda b, j: (0, 0))],
        out_specs=[pl.BlockSpec((None, ts, LANES), lambda b, j: (b, j, 0)),
                   pl.BlockSpec((None, 2 * SUBLANES, ts), lambda b, j: (b, 0, j))],
        out_shape=[jax.ShapeDtypeStruct((B, S, LANES), F32),
                   jax.ShapeDtypeStruct((B, 2 * SUBLANES, S), F32)],
        scratch_shapes=[pltpu.VMEM((1, LANES), F32)],
        compiler_params=_params("parallel", "arbitrary"), name="gates")(small3, prm)


def _fox_body(q_ref, k_ref, v_ref, ccol_ref, qg_ref, kg_ref, og_ref, o_ref,
              kn_scr, vt_scr, q_scr, m_scr, l_scr, acc_scr, s_scr, *, tq, nk):
    i = pl.program_id(1)
    ones_bd = _head_ones()
    lane = lax.broadcasted_iota(I32, (tq, LANES), 1)
    log2e = 1.0 / np.log(2.0)

    def head_norm(x, g):
        ss = _head_sums(x * x, ones_bd)
        return x * lax.rsqrt(ss * (1.0 / HEAD_DIM) + EPS) * g

    def head_tile(x, h, extra):
        pair = x[:, (h // 2) * LANES:(h // 2 + 1) * LANES]
        if h % 2:
            pair = pltpu.roll(pair, HEAD_DIM, 1)
        return jnp.where(lane < HEAD_DIM, pair, extra).astype(BF16)

    @pl.when(i == 0)
    def _():
        for c in range(nk):
            rows = slice(c * tq, (c + 1) * tq)
            kc = head_norm(k_ref[rows, :], kg_ref[...])
            vt = v_ref[rows, :].T.astype(BF16)
            cc = ccol_ref[rows, :] * (-log2e)
            for h in range(N_HEADS):
                cj = cc[:, LANE_FOX + h:LANE_FOX + h + 1]
                hi = cj.astype(BF16).astype(F32)
                mid = (cj - hi).astype(BF16).astype(F32)
                lo = (cj - hi - mid).astype(BF16).astype(F32)
                extra = jnp.where(lane == HEAD_DIM, hi, jnp.where(lane == HEAD_DIM + 1, mid,
                                  jnp.where(lane == HEAD_DIM + 2, lo, 0.0)))
                kn_scr[h, c] = head_tile(kc, h, extra)
                vt_scr[h, c] = vt[h * HEAD_DIM:(h + 1) * HEAD_DIM, :]

    qn = head_norm(q_ref[...], qg_ref[...]) * (HEAD_DIM ** -0.5 * log2e)
    q_ones = jnp.where(lane < HEAD_DIM + 3, 1.0, 0.0)
    for h in range(N_HEADS):
        q_scr[h] = head_tile(qn, h, q_ones)
    m_scr[...] = jnp.full_like(m_scr, -jnp.inf)
    l_scr[...] = jnp.zeros_like(l_scr)
    acc_scr[...] = jnp.zeros_like(acc_scr)
    causal = (lax.broadcasted_iota(I32, (tq, tq), 0) <= lax.broadcasted_iota(I32, (tq, tq), 1))

    heads = range(N_HEADS)

    def scores(j):
        return [lax.dot_general(kn_scr[h, j], q_scr[h], (((1,), (1,)), ((), ())),
                                preferred_element_type=F32) for h in heads]

    def stash(s):
        for h in heads:
            s_scr[h] = s[h]

    def absorb(j, masked):
        p, alpha = [], []
        for h in heads:
            s = s_scr[h]
            if masked:
                s = jnp.where(causal, s, -jnp.inf)
            m_old = m_scr[h]
            m_new = jnp.maximum(m_old, jnp.max(s, axis=0, keepdims=True))
            alpha.append(jnp.exp2(m_old - m_new))
            ph = jnp.exp2(s - m_new)
            l_scr[h] = alpha[h] * l_scr[h] + jnp.sum(ph, axis=0, keepdims=True)
            m_scr[h] = m_new
            p.append(ph.astype(BF16))
        pv = [jnp.dot(vt_scr[h, j], p[h], preferred_element_type=F32) for h in heads]
        for h in heads:
            acc_scr[h] = alpha[h] * acc_scr[h] + pv[h]

    stash(scores(0))

    def body(j, c):
        s_next = scores(j + 1)
        absorb(j, False)
        stash(s_next)
        return c

    lax.fori_loop(0, i, body, 0)
    absorb(i, True)
    o_t = jnp.concatenate([acc_scr[h] / l_scr[h] for h in range(N_HEADS)], axis=0)
    o_ref[...] = _rms(o_t.T, og_ref[...])


def _fox(big3, gcol, qg, kg, og, tq):
    B, S, _ = big3.shape
    nk = S // tq
    row = pl.BlockSpec((1, GROUP_W), lambda b, i: (0, 0))
    return pl.pallas_call(
        functools.partial(_fox_body, tq=tq, nk=nk), grid=(B, nk),
        in_specs=[pl.BlockSpec((None, tq, GROUP_W), lambda b, i: (b, i, COL_FQ)),
                  pl.BlockSpec((None, S, GROUP_W), lambda b, i: (b, 0, COL_FK)),
                  pl.BlockSpec((None, S, GROUP_W), lambda b, i: (b, 0, COL_FV)),
                  pl.BlockSpec((None, S, LANES), lambda b, i: (b, 0, 0)),
                  row, row, row],
        out_specs=pl.BlockSpec((None, tq, GROUP_W), lambda b, i: (b, i, 0)),
        out_shape=jax.ShapeDtypeStruct((B, S, GROUP_W), F32),
        scratch_shapes=[pltpu.VMEM((N_HEADS, nk, tq, LANES), BF16),
                        pltpu.VMEM((N_HEADS, nk, HEAD_DIM, tq), BF16),
                        pltpu.VMEM((N_HEADS, tq, LANES), BF16),
                        pltpu.VMEM((N_HEADS, 1, tq), F32),
                        pltpu.VMEM((N_HEADS, 1, tq), F32),
                        pltpu.VMEM((N_HEADS, HEAD_DIM, tq), F32),
                        pltpu.VMEM((N_HEADS, tq, tq), F32)],
        compiler_params=_params("parallel", "arbitrary"), name="fox")(
            big3, big3, big3, gcol, qg, kg, og)


def _gelu(x):
    return 0.5 * x * (1.0 + lax.erf(x * (2.0 ** -0.5)))


def _gmlp_body(u_ref, v_ref, lg_ref, lb_ref, ws_ref, bst_ref, og_ref, o_ref, *, nwin):
    L = GMLP_LEN
    r = lax.broadcasted_iota(I32, (L, L), 0) // CHUNK
    c = lax.broadcasted_iota(I32, (L, L), 1) // CHUNK
    mask = r >= c
    ws = [jnp.where(mask, ws_ref[h], 0.0).astype(BF16) for h in range(N_HEADS)]
    for n in range(nwin):
        u = _gelu(u_ref[n * L:(n + 1) * L, :])
        v = _gelu(v_ref[n * L:(n + 1) * L, :])
        mu = jnp.mean(v, axis=-1, keepdims=True)
        vc = v - mu
        var = jnp.mean(vc * vc, axis=-1, keepdims=True)
        vn = (vc * lax.rsqrt(var + EPS) * lg_ref[...] + lb_ref[...]).astype(BF16)
        mixed = jnp.concatenate(
            [jnp.dot(ws[h], vn[:, h * HEAD_DIM:(h + 1) * HEAD_DIM], preferred_element_type=F32)
             for h in range(N_HEADS)], axis=-1) + bst_ref[...]
        o_ref[n * L:(n + 1) * L, :] = _rms(u * mixed, og_ref[...])


def _gmlp(big, lg, lb, ws, bst, og, tm):
    T = big.shape[0]
    row = pl.BlockSpec((1, GROUP_W), lambda i: (0, 0))
    return pl.pallas_call(
        functools.partial(_gmlp_body, nwin=tm // GMLP_LEN), grid=(T // tm,),
        in_specs=[pl.BlockSpec((tm, GROUP_W), lambda i: (i, COL_GU)),
                  pl.BlockSpec((tm, GROUP_W), lambda i: (i, COL_GV)),
                  row, row,
                  pl.BlockSpec((N_HEADS, GMLP_LEN, GMLP_LEN), lambda i: (0, 0, 0)),
                  pl.BlockSpec((GMLP_LEN, GROUP_W), lambda i: (0, 0)),
                  row],
        out_specs=pl.BlockSpec((tm, GROUP_W), lambda i: (i, 0)),
        out_shape=jax.ShapeDtypeStruct((T, GROUP_W), F32),
        compiler_params=_params("parallel"), name="gmlp")(big, big, lg, lb, ws, bst, og)


def _gdn_prep_body(q_ref, k_ref, v_ref, hq_ref, hk_ref, hv_ref, w_ref, qo_ref, ko_ref, vo_ref, *, ts):
    first = pl.program_id(1) == 0
    ones_bd = _head_ones()

    def conv(x_ref, halo_ref, w):
        halo = jnp.where(first, 0.0, halo_ref[...])
        xx = jnp.concatenate([halo, x_ref[...]], axis=0)
        y = w[CONV_K - 1:CONV_K, :] * xx[SUBLANES:, :]
        for j in range(CONV_K - 1):
            y = y + w[j:j + 1, :] * pltpu.roll(xx, CONV_K - 1 - j, 0)[SUBLANES:, :]
        return y * jax.nn.sigmoid(y)

    def l2(t):
        return t * lax.rsqrt(_head_sums(t * t, ones_bd) + EPS)

    qo_ref[...] = l2(conv(q_ref, hq_ref, w_ref[0])) * (HEAD_DIM ** -0.5)
    ko_ref[...] = l2(conv(k_ref, hk_ref, w_ref[1]))
    vo_ref[...] = conv(v_ref, hv_ref, w_ref[2])


def _gdn_prep(big3, conv_w3, ts):
    B, S, _ = big3.shape
    hb = ts // SUBLANES
    blk = lambda col: pl.BlockSpec((None, ts, GROUP_W), lambda b, i: (b, i, col))
    halo = lambda col: pl.BlockSpec((None, SUBLANES, GROUP_W),
                                    lambda b, i: (b, jnp.maximum(i * hb - 1, 0), col))
    out = pl.BlockSpec((None, ts, GROUP_W), lambda b, i: (b, i, 0))
    shp = jax.ShapeDtypeStruct((B, S, GROUP_W), F32)
    return pl.pallas_call(
        functools.partial(_gdn_prep_body, ts=ts), grid=(B, S // ts),
        in_specs=[blk(COL_DQ), blk(COL_DK), blk(COL_DV), halo(COL_DQ), halo(COL_DK), halo(COL_DV),
                  pl.BlockSpec((3, CONV_K, GROUP_W), lambda b, i: (0, 0, 0))],
        out_specs=[out, out, out], out_shape=[shp, shp, shp],
        compiler_params=_params("parallel", "parallel"), name="gdn_prep")(
            big3, big3, big3, big3, big3, big3, conv_w3)


def _gdn_body(q_ref, k_ref, v_ref, gate_ref, gcol_ref, grow_ref, ng_ref, o_ref,
              s_scr, u_scr, wq_scr, a_scr, kd_scr, dl_scr, t_scr, p_scr, rhs_scr, *, nchunk):
    C = CHUNK

    @pl.when(pl.program_id(1) == 0)
    def _():
        s_scr[...] = jnp.zeros_like(s_scr)

    r = lax.broadcasted_iota(I32, (C, C), 0)
    c = lax.broadcasted_iota(I32, (C, C), 1)
    tri = r >= c
    strict = r > c
    eye = (r == c).astype(F32)

    items = [(n, h) for n in range(nchunk) for h in range(N_HEADS)]
    mmb = functools.partial(jnp.dot, preferred_element_type=F32)
    for it, (n, h) in enumerate(items):
        rows = slice(n * C, (n + 1) * C)
        sl = slice(h * HEAD_DIM, (h + 1) * HEAD_DIM)
        q, k, v = q_ref[rows, sl], k_ref[rows, sl], v_ref[rows, sl]
        gc = gcol_ref[rows, LANE_DECAY + h:LANE_DECAY + h + 1]
        gr = grow_ref[n, LANE_DECAY + h:LANE_DECAY + h + 1, :]
        beta = gcol_ref[rows, LANE_BETA + h:LANE_BETA + h + 1]
        decay = jnp.exp(jnp.where(tri, gc - gr, -jnp.inf))
        kb = k * beta
        kk = _mm_nt(jnp.concatenate([kb, q], axis=0), k)
        x = jnp.where(strict, -(kk[:C] * decay), 0.0)
        t_scr[it] = eye + x
        p_scr[it] = x.astype(BF16)
        eg = jnp.exp(gc)
        g_last = gc[C - 1:C, :]
        rhs_scr[it] = jnp.concatenate([v * beta, kb * eg], axis=1).astype(BF16)
        wq_scr[n, h, C:, :] = (q * eg).astype(BF16)
        a_scr[n, h] = jnp.where(tri, kk[C:] * decay, 0.0).astype(BF16)
        kd_scr[n, h] = (k * jnp.exp(g_last - gc)).astype(BF16)
        dl_scr[n, h] = jnp.broadcast_to(jnp.exp(g_last), (1, HEAD_DIM))
    for _ in range(5):
        for it in range(len(items)):
            p = p_scr[it]
            p_scr[it] = mmb(p, p).astype(BF16)
        for it in range(len(items)):
            t = t_scr[it]
            t_scr[it] = t + mmb(t.astype(BF16), p_scr[it])
    for it, (n, h) in enumerate(items):
        uw = mmb(t_scr[it].astype(BF16), rhs_scr[it])
        u_scr[n, h] = uw[:, :HEAD_DIM]
        wq_scr[n, h, :C, :] = uw[:, HEAD_DIM:].astype(BF16)

    heads = range(N_HEADS)
    state = [s_scr[h] for h in heads]
    for n in range(nchunk):
        rows = slice(n * C, (n + 1) * C)
        ws = [mmb(wq_scr[n, h], state[h].astype(BF16)) for h in heads]
        vb = [(u_scr[n, h] - ws[h][:C]).astype(BF16) for h in heads]
        o = [ws[h][C:] + mmb(a_scr[n, h], vb[h]) for h in heads]
        state = [state[h] * dl_scr[n, h] + lax.dot_general(
            kd_scr[n, h], vb[h], (((0,), (0,)), ((), ())), preferred_element_type=F32) for h in heads]
        gate = gate_ref[rows, :]
        y = jnp.concatenate([_rms(o[h], ng_ref[...]) for h in heads], axis=-1)
        o_ref[rows, :] = y * (gate * jax.nn.sigmoid(gate))
    for h in heads:
        s_scr[h] = state[h]


def _gdn(qn, kn, vv, big3, gcol, grow4, ng, ts):
    B, S, _ = qn.shape
    nchunk = ts // CHUNK
    blk = pl.BlockSpec((None, ts, GROUP_W), lambda b, i: (b, i, 0))
    return pl.pallas_call(
        functools.partial(_gdn_body, nchunk=nchunk), grid=(B, S // ts),
        in_specs=[blk, blk, blk,
                  pl.BlockSpec((None, ts, GROUP_W), lambda b, i: (b, i, COL_DG)),
                  pl.BlockSpec((None, ts, LANES), lambda b, i: (b, i, 0)),
                  pl.BlockSpec((None, nchunk, 2 * SUBLANES, CHUNK), lambda b, i: (b, i, 0, 0)),
                  pl.BlockSpec((1, HEAD_DIM), lambda b, i: (0, 0))],
        out_specs=blk, out_shape=jax.ShapeDtypeStruct((B, S, GROUP_W), F32),
        scratch_shapes=[pltpu.VMEM((N_HEADS, HEAD_DIM, HEAD_DIM), F32),
                        pltpu.VMEM((nchunk, N_HEADS, CHUNK, HEAD_DIM), F32),
                        pltpu.VMEM((nchunk, N_HEADS, 2 * CHUNK, HEAD_DIM), BF16),
                        pltpu.VMEM((nchunk, N_HEADS, CHUNK, CHUNK), BF16),
                        pltpu.VMEM((nchunk, N_HEADS, CHUNK, HEAD_DIM), BF16),
                        pltpu.VMEM((nchunk, N_HEADS, 1, HEAD_DIM), F32),
                        pltpu.VMEM((nchunk * N_HEADS, CHUNK, CHUNK), F32),
                        pltpu.VMEM((nchunk * N_HEADS, CHUNK, CHUNK), BF16),
                        pltpu.VMEM((nchunk * N_HEADS, CHUNK, 2 * HEAD_DIM), BF16)],
        compiler_params=_params("parallel", "arbitrary"), name="gdn")(
            qn, kn, vv, big3, gcol, grow4, ng)


def _pool_body(z_ref, halo_ref, w_ref, sc_ref, og_ref, o_ref, *, ts):
    i = pl.program_id(1)
    hr = 2 * SUBLANES
    z = z_ref[...]
    halo = jnp.where(i == 0, 0.0, halo_ref[...])
    s1 = jnp.concatenate([halo, z], axis=0)
    s2 = s1 + pltpu.roll(s1, 1, 0)
    s4 = s2 + pltpu.roll(s2, 2, 0)
    s8 = s4 + pltpu.roll(s4, 4, 0)
    s16 = s8 + pltpu.roll(s8, 8, 0)
    grp = lax.broadcasted_iota(I32, (ts, GROUP_W), 1) // (GROUP_W // len(POOL_WINDOWS))
    t = lax.broadcasted_iota(I32, (ts, GROUP_W), 0) + i * ts
    total = jnp.where(grp == 0, s2[hr:], jnp.where(grp == 1, s4[hr:], jnp.where(grp == 2, s8[hr:], s16[hr:])))
    win = jnp.where(grp == 0, POOL_WINDOWS[0], jnp.where(grp == 1, POOL_WINDOWS[1],
                    jnp.where(grp == 2, POOL_WINDOWS[2], POOL_WINDOWS[3])))
    pooled = total / jnp.minimum(t + 1, win).astype(F32)
    y = _mm(pooled - z, w_ref[...]) * sc_ref[...]
    o_ref[...] = _rms(y, og_ref[...])


def _pool(big3, wbd, sc, og, ts):
    B, S, _ = big3.shape
    hr = 2 * SUBLANES
    hb = ts // hr
    row = pl.BlockSpec((1, GROUP_W), lambda b, i: (0, 0))
    return pl.pallas_call(
        functools.partial(_pool_body, ts=ts), grid=(B, S // ts),
        in_specs=[pl.BlockSpec((None, ts, GROUP_W), lambda b, i: (b, i, COL_PZ)),
                  pl.BlockSpec((None, hr, GROUP_W), lambda b, i: (b, jnp.maximum(i * hb - 1, 0), COL_PZ)),
                  pl.BlockSpec((GROUP_W, GROUP_W), lambda b, i: (0, 0)), row, row],
        out_specs=pl.BlockSpec((None, ts, GROUP_W), lambda b, i: (b, i, 0)),
        out_shape=jax.ShapeDtypeStruct((B, S, GROUP_W), F32),
        compiler_params=_params("parallel", "parallel"), name="pool")(big3, big3, wbd, sc, og)


def _outproj_body(ya_ref, yb_ref, yc_ref, yd_ref, h_ref, wo_ref, g_ref, rw_ref, rb_ref,
                  hnew_ref, hn_ref, ri_ref, rf_ref, *, tm, d):
    y = jnp.concatenate([ya_ref[...], yb_ref[...], yc_ref[...], yd_ref[...]], axis=-1).astype(BF16)
    h_new = h_ref[...] + jnp.dot(y, wo_ref[...], preferred_element_type=F32)
    hnew_ref[...] = h_new
    hn = _rms(h_new, g_ref[...])
    hn_hi, hn_lo = _split(hn)
    hn_ref[...] = hn_hi
    t = jnp.dot(hn_hi, rw_ref[...], preferred_element_type=F32)
    logits = (t[:, :LANES] + t[:, LANES:]
              + jnp.dot(hn_lo, rw_ref[:, :LANES], preferred_element_type=F32)) + rb_ref[...]
    lane = lax.broadcasted_iota(I32, (tm, LANES), 1)
    neg = -jnp.inf
    big_lane = LANES

    def masked_top(vals, mask):
        v = jnp.where(mask, vals, neg)
        mx = jnp.max(v, axis=-1, keepdims=True)
        idx = jnp.min(jnp.where(mask & (v == mx), lane, big_lane), axis=-1, keepdims=True)
        return v, mx, idx

    gmask = lane < N_GROUPS
    gv, gmx, gidx = masked_top(logits, gmask)
    g_top = 1.0 / jnp.sum(jnp.where(gmask, jnp.exp(gv - gmx), 0.0), axis=-1, keepdims=True)
    lo = N_GROUPS + gidx * EXPERTS_PER_GROUP
    emask = (lane >= lo) & (lane < lo + EXPERTS_PER_GROUP)
    ev, emx, eidx1 = masked_top(logits, emask)
    esum = jnp.sum(jnp.where(emask, jnp.exp(ev - emx), 0.0), axis=-1, keepdims=True)
    p1 = 1.0 / esum
    _, emx2, eidx2 = masked_top(logits, emask & (lane != eidx1))
    p2 = jnp.exp(emx2 - emx) / esum
    denom = p1 + p2
    ri_ref[...] = jnp.where(lane == 0, eidx1 - N_GROUPS, jnp.where(lane == 1, eidx2 - N_GROUPS, 0))
    rf_ref[...] = jnp.where(lane == 0, g_top * p1 / denom, jnp.where(lane == 1, g_top * p2 / denom, 0.0))


def _outproj(ya, yb, yc, yd, h2d, wo, g, rw, rb, tm):
    T, D = h2d.shape
    nseg = D // LANES
    yblk = pl.BlockSpec((tm, GROUP_W), lambda i: (i, 0))
    return pl.pallas_call(
        functools.partial(_outproj_body, tm=tm, d=D), grid=(T // tm,),
        in_specs=[yblk, yblk, yblk, yblk,
                  pl.BlockSpec((tm, D), lambda i: (i, 0)),
                  pl.BlockSpec((D, D), lambda i: (0, 0)),
                  pl.BlockSpec((1, D), lambda i: (0, 0)),
                  pl.BlockSpec((D, 2 * LANES), lambda i: (0, 0)),
                  pl.BlockSpec((1, LANES), lambda i: (0, 0))],
        out_specs=[pl.BlockSpec((tm, D), lambda i: (i, 0)),
                   pl.BlockSpec((tm, D), lambda i: (i, 0)),
                   pl.BlockSpec((tm, LANES), lambda i: (i, 0)),
                   pl.BlockSpec((tm, LANES), lambda i: (i, 0))],
        out_shape=[jax.ShapeDtypeStruct((T, D), F32),
                   jax.ShapeDtypeStruct((T, D), BF16),
                   jax.ShapeDtypeStruct((T, LANES), I32),
                   jax.ShapeDtypeStruct((T, LANES), F32)],
        compiler_params=_params("parallel"), name="outproj")(ya, yb, yc, yd, h2d, wo, g, rw, rb)


def _rank_body(ri_ref, rank_ref, cnt_ref, *, tm):
    lane = lax.broadcasted_iota(I32, (tm, LANES), 1)
    e = ri_ref[...]
    oh = [lane == e[:, s:s + 1] for s in range(TOPK_IN)]
    m = (oh[0].astype(F32) + oh[1].astype(F32)).astype(BF16)
    below = (lax.broadcasted_iota(I32, (tm, tm), 0) > lax.broadcasted_iota(I32, (tm, tm), 1)).astype(BF16)
    before = jnp.dot(below, m, preferred_element_type=F32)
    rank = [jnp.sum(jnp.where(oh[s], before, 0.0), axis=-1, keepdims=True) for s in range(TOPK_IN)]
    rank_ref[...] = jnp.where(lane == 0, rank[0], jnp.where(lane == 1, rank[1], 0.0)).astype(I32)
    total = before[tm - 1:tm, :] + m[tm - 1:tm, :].astype(F32)
    cnt_ref[...] = jnp.broadcast_to(total, cnt_ref.shape).astype(I32)


def _rank(ri, tm):
    T = ri.shape[0]
    return pl.pallas_call(
        functools.partial(_rank_body, tm=tm), grid=(T // tm,),
        in_specs=[pl.BlockSpec((tm, LANES), lambda i: (i, 0))],
        out_specs=[pl.BlockSpec((tm, LANES), lambda i: (i, 0)),
                   pl.BlockSpec((SUBLANES, LANES), lambda i: (i, 0))],
        out_shape=[jax.ShapeDtypeStruct((T, LANES), I32),
                   jax.ShapeDtypeStruct((T // tm * SUBLANES, LANES), I32)],
        compiler_params=_params("parallel"), name="rank")(ri)


def _dispatch_plan(expert, lrank, cnt_tile, tm):
    T = expert.shape[0]
    nt = T // tm
    counts = jnp.sum(cnt_tile, axis=0)
    padded = (counts + (RUN_ROWS - 1) + MOE_BLOCK - 1) // MOE_BLOCK * MOE_BLOCK
    pad_end = jnp.cumsum(padded)
    pad_start = pad_end - padded
    gstart = pad_start[None, :] + jnp.cumsum(cnt_tile, axis=0) - cnt_tile
    nchunk = (cnt_tile + RUN_ROWS - 1) // RUN_ROWS
    chunk_end = jnp.cumsum(nchunk, axis=1)
    lstart = (chunk_end - nchunk) * RUN_ROWS
    onehot = expert[:, :, None] == jnp.arange(N_EXPERTS, dtype=I32)[None, None, :]
    pick = lambda tab: jnp.sum(jnp.where(onehot, jnp.repeat(tab, tm, axis=0)[:, None, :], 0), axis=-1)
    lpos =(lrank + pick(lstart)).astype(I32)
    max_chunks = tm * TOPK_IN // RUN_ROWS + N_EXPERTS
    c = jnp.arange(max_chunks, dtype=I32)
    ce = jnp.minimum(jnp.sum(chunk_end[:, None, :] <= c[None, :, None], axis=-1), N_EXPERTS - 1)
    ce_hot = ce[:, :, None] == jnp.arange(N_EXPERTS, dtype=I32)[None, None, :]
    take = lambda tab: jnp.sum(jnp.where(ce_hot, tab[:, None, :], 0), axis=-1)
    chunk_row = take(gstart) + (c[None, :] - take(chunk_end - nchunk)) * RUN_ROWS
    chunk_row = jnp.where(c[None, :] < chunk_end[:, -1:], chunk_row, 0)
    table = jnp.zeros((nt, 1, LANES), I32).at[:, 0, :max_chunks].set(chunk_row.astype(I32))
    table = table.at[:, 0, LANES - 1].set(chunk_end[:, -1].astype(I32))
    n_blk = -(-(T * TOPK_IN + N_EXPERTS * (RUN_ROWS - 1)) // MOE_BLOCK) + N_EXPERTS
    blk_start = jnp.arange(n_blk, dtype=I32) * MOE_BLOCK
    blk_e = jnp.minimum(jnp.sum(pad_end[None, :] <= blk_start[:, None], axis=-1), N_EXPERTS - 1).astype(I32)
    n_used = (pad_end[-1] // MOE_BLOCK).astype(I32).reshape(1)
    return lpos, table, blk_e, n_used, n_blk


def _dispatch_body(tab_ref, lpos_ref, hn_ref, xz_hbm, x_hbm, xs, sem, *, tm, nseg, nrow):
    del xz_hbm
    row = lax.broadcasted_iota(I32, (nrow, tm), 0)
    sel = (row == lpos_ref[0:1, :]) | (row == lpos_ref[1:2, :])
    rows = jnp.dot(sel.astype(BF16), hn_ref[...], preferred_element_type=F32)
    for k in range(nseg):
        xs[pl.ds(k, nrow, stride=nseg), :] = rows[:, k * LANES:(k + 1) * LANES]

    step = RUN_ROWS * nseg

    def chunk_copy(c):
        return pltpu.make_async_copy(
            xs.at[pl.ds(pl.multiple_of(c * step, step), step), :],
            x_hbm.at[pl.ds(pl.multiple_of(tab_ref[0, 0, c] * nseg, nseg), step), :], sem)

    n_chunks = tab_ref[0, 0, LANES - 1]

    def start(c, carry):
        chunk_copy(c).start()
        return carry

    def wait(c, carry):
        chunk_copy(c).wait()
        return carry

    lax.fori_loop(0, n_chunks, start, 0)
    lax.fori_loop(0, n_chunks, wait, 0)


def _dispatch(table, lpos_t, hn, n_pad, tm):
    T, D = hn.shape
    nseg = D // LANES
    nrow = tm * TOPK_IN + N_EXPERTS * RUN_ROWS
    x_zero = jnp.zeros((n_pad * nseg, LANES), F32)
    return pl.pallas_call(
        functools.partial(_dispatch_body, tm=tm, nseg=nseg, nrow=nrow), grid=(T // tm,),
        in_specs=[pl.BlockSpec((1, 1, LANES), lambda i: (i, 0, 0), memory_space=pltpu.SMEM),
                  pl.BlockSpec((None, SUBLANES, tm), lambda i: (i, 0, 0)),
                  pl.BlockSpec((tm, D), lambda i: (i, 0)),
                  pl.BlockSpec(memory_space=pl.ANY)],
        out_specs=pl.BlockSpec(memory_space=pl.ANY),
        out_shape=jax.ShapeDtypeStruct((n_pad * nseg, LANES), F32),
        scratch_shapes=[pltpu.VMEM((nrow * nseg, LANES), F32), pltpu.SemaphoreType.DMA],
        input_output_aliases={3: 0},
        compiler_params=_params("arbitrary"), name="dispatch")(table, lpos_t, hn, x_zero)


def _moe_body(blk_e_ref, n_used_ref, x_ref, w1_ref, w3_ref, w2_ref, y_ref, w1b, w3b, w2b, *, nseg):
    b = pl.program_id(0)
    R = MOE_BLOCK

    @pl.when(b < n_used_ref[0])
    def _():
        @pl.when((b == 0) | (blk_e_ref[b] != blk_e_ref[jnp.maximum(b - 1, 0)]))
        def _():
            w1b[...] = w1_ref[...].astype(BF16)
            w3b[...] = w3_ref[...].astype(BF16)
            w2b[...] = w2_ref[...].astype(BF16)

        x = jnp.concatenate([x_ref[pl.ds(s, R, stride=nseg), :] for s in range(nseg)], axis=-1).astype(BF16)
        a = jnp.dot(x, w1b[...], preferred_element_type=F32)
        g = jnp.dot(x, w3b[...], preferred_element_type=F32)
        hid = (a * jax.nn.sigmoid(a) * g).astype(BF16)
        y = jnp.dot(hid, w2b[...], preferred_element_type=F32)
        for s in range(nseg):
            y_ref[pl.ds(s, R, stride=nseg), :] = y[:, s * LANES:(s + 1) * LANES]

    @pl.when(b >= n_used_ref[0])
    def _():
        y_ref[...] = jnp.zeros_like(y_ref)


def _moe(blk_e, n_used, x_rows, w1, w3, w2, layer):
    n_blk = blk_e.shape[0]
    _, _, D, DE = w1.shape
    nseg = D // LANES
    R = MOE_BLOCK
    rows = lambda b, be, nu: (jnp.minimum(b, nu[0] - 1), 0)
    wmap = lambda b, be, nu: (layer, be[b], 0, 0)
    grid_spec = pltpu.PrefetchScalarGridSpec(
        num_scalar_prefetch=2, grid=(n_blk,),
        in_specs=[pl.BlockSpec((R * nseg, LANES), rows),
                  pl.BlockSpec((None, None, D, DE), wmap),
                  pl.BlockSpec((None, None, D, DE), wmap),
                  pl.BlockSpec((None, None, DE, D), wmap)],
        out_specs=pl.BlockSpec((R * nseg, LANES), lambda b, be, nu: (b, 0)),
        scratch_shapes=[pltpu.VMEM((D, DE), BF16), pltpu.VMEM((D, DE), BF16), pltpu.VMEM((DE, D), BF16)])
    return pl.pallas_call(
        functools.partial(_moe_body, nseg=nseg), grid_spec=grid_spec,
        out_shape=jax.ShapeDtypeStruct(x_rows.shape, F32),
        compiler_params=_params("arbitrary"), name="moe")(blk_e, n_used, x_rows, w1, w3, w2)


def _combine_body(tab_ref, h_ref, rf_ref, lpos_ref, y_hbm, o_ref, ybuf, sem, *, tm, nseg, nrow):
    @pl.when(pl.program_id(0) == 0)
    def _():
        ybuf[...] = jnp.zeros_like(ybuf)

    step = RUN_ROWS * nseg

    def chunk_copy(c):
        return pltpu.make_async_copy(
            y_hbm.at[pl.ds(pl.multiple_of(tab_ref[0, 0, c] * nseg, nseg), step), :],
            ybuf.at[pl.ds(pl.multiple_of(c * step, step), step), :], sem)

    n_chunks = tab_ref[0, 0, LANES - 1]

    def start(c, carry):
        chunk_copy(c).start()
        return carry

    def wait(c, carry):
        chunk_copy(c).wait()
        return carry

    lax.fori_loop(0, n_chunks, start, 0)
    col = lax.broadcasted_iota(I32, (tm, nrow), 1)
    sel = jnp.zeros((tm, nrow), F32)
    for s in range(TOPK_IN):
        sel = sel + jnp.where(col == lpos_ref[:, s:s + 1], rf_ref[:, s:s + 1], 0.0)
    lax.fori_loop(0, n_chunks, wait, 0)
    y = jnp.concatenate([ybuf[pl.ds(k, nrow, stride=nseg), :] for k in range(nseg)], axis=-1)
    o_ref[...] = h_ref[...] + jnp.dot(sel.astype(BF16), y.astype(BF16), preferred_element_type=F32)


def _combine(table, h2d, rf, lpos, y_rows, tm):
    T, D = h2d.shape
    nseg = D // LANES
    nrow = tm * TOPK_IN + N_EXPERTS * RUN_ROWS
    return pl.pallas_call(
        functools.partial(_combine_body, tm=tm, nseg=nseg, nrow=nrow), grid=(T // tm,),
        in_specs=[pl.BlockSpec((1, 1, LANES), lambda i: (i, 0, 0), memory_space=pltpu.SMEM),
                  pl.BlockSpec((tm, D), lambda i: (i, 0)),
                  pl.BlockSpec((tm, LANES), lambda i: (i, 0)),
                  pl.BlockSpec((tm, LANES), lambda i: (i, 0)),
                  pl.BlockSpec(memory_space=pl.ANY)],
        out_specs=pl.BlockSpec((tm, D), lambda i: (i, 0)),
        out_shape=jax.ShapeDtypeStruct((T, D), F32),
        scratch_shapes=[pltpu.VMEM((nrow * nseg, LANES), F32), pltpu.SemaphoreType.DMA],
        compiler_params=_params("arbitrary"), name="combine")(table, h2d, rf, lpos, y_rows)


def _pad_lanes(a, lane0, rows=1):
    out = jnp.zeros((rows, LANES), a.dtype)
    return out.at[:, lane0:lane0 + a.shape[-1]].set(a.reshape(rows, -1))


def _layer(h2d, B, S, p):
    T, D = h2d.shape
    tile = lambda a, n: jnp.tile(a.reshape(1, -1), (1, n))

    offs = np.cumsum([0, GROUP_W, GROUP_W, GROUP_W, N_HEADS, GROUP_W, GROUP_W,
                      GROUP_W, GROUP_W, GROUP_W, N_HEADS, N_HEADS, GROUP_W, GROUP_W])
    seg = lambda k: p['w_in'][:, offs[k]:offs[k + 1]]
    wb = jnp.concatenate([seg(0), seg(1), seg(2), seg(4), seg(5), seg(6), seg(7), seg(8), seg(11), seg(12)],
                         axis=1).astype(BF16)
    ws = jnp.zeros((D, LANES), F32)
    ws = ws.at[:, LANE_FOX:LANE_FOX + N_HEADS].set(seg(3))
    ws = ws.at[:, LANE_DECAY:LANE_DECAY + N_HEADS].set(seg(9))
    ws = ws.at[:, LANE_BETA:LANE_BETA + N_HEADS].set(seg(10)).astype(BF16)
    gate_prm = jnp.concatenate([_pad_lanes(p['fox_f_bias'], LANE_FOX), _pad_lanes(p['gdn_dt_bias'], LANE_DECAY),
                                _pad_lanes(p['gdn_a_log'], LANE_DECAY), jnp.zeros((SUBLANES - 3, LANES), F32)], axis=0)

    big, small = _inproj(h2d, p['attn_norm_g'].reshape(1, D), wb, ws, tm=min(512, T))
    big3 = big.reshape(B, S, N_BIG_COLS * GROUP_W)
    ts = min(512, S)
    gcol, grow = _gates(small.reshape(B, S, LANES), gate_prm, ts)

    tq = min(256, S)
    ya = _fox(big3, gcol, tile(p['fox_qn_g'], N_HEADS), tile(p['fox_kn_g'], N_HEADS),
              p['fox_out_g'].reshape(1, GROUP_W), tq)

    bst = jnp.repeat(p['gmlp_bs'].T, HEAD_DIM, axis=1)
    yb = _gmlp(big, p['gmlp_ln_g'].reshape(1, -1), p['gmlp_ln_b'].reshape(1, -1), p['gmlp_ws'], bst,
               p['gmlp_out_g'].reshape(1, -1), tm=min(512, T))

    conv_w3 = p['gdn_conv_w'].reshape(CONV_K, 3, GROUP_W).transpose(1, 0, 2)
    qn, kn, vv = _gdn_prep(big3, conv_w3, ts)
    grow4 = grow.reshape(B, 2 * SUBLANES, S // CHUNK, CHUNK).transpose(0, 2, 1, 3)
    yc = _gdn(qn, kn, vv, big3, gcol, grow4, p['gdn_norm_g'].reshape(1, HEAD_DIM), ts)

    wbd = jax.scipy.linalg.block_diag(*[p['pool_w'][g] for g in range(len(POOL_WINDOWS))]).astype(BF16)
    yd = _pool(big3, wbd, p['pool_scale'].reshape(1, -1), p['pool_out_g'].reshape(1, -1), ts)

    rw = jnp.zeros((D, LANES), F32).at[:, :N_GROUPS].set(p['router_g_w'])
    rw = rw.at[:, N_GROUPS:N_GROUPS + N_EXPERTS].set(p['router_e_w'])
    rw = jnp.concatenate(_split(rw), axis=1)
    rb = jnp.zeros((1, LANES), F32).at[0, :N_GROUPS].set(p['router_g_b'])
    rb = rb.at[0, N_GROUPS:N_GROUPS + N_EXPERTS].set(p['router_e_b'])
    flat = lambda a: a.reshape(T, GROUP_W)
    h_new, hn_rows, ri, rf = _outproj(flat(ya), yb, flat(yc), flat(yd), h2d, p['w_out'].astype(BF16),
                                      p['ffn_norm_g'].reshape(1, D), rw, rb, tm=min(512, T))

    tmd = min(256, T)
    lrank, cnt = _rank(ri, tmd)
    cnt_tile = cnt.reshape(T // tmd, SUBLANES, LANES)[:, 0, :N_EXPERTS]
    lpos, table, blk_e, n_used, n_blk = _dispatch_plan(ri[:, :TOPK_IN], lrank[:, :TOPK_IN], cnt_tile, tmd)
    lpos_t = jnp.zeros((T // tmd, SUBLANES, tmd), I32).at[:, :TOPK_IN, :].set(
        lpos.reshape(T // tmd, tmd, TOPK_IN).transpose(0, 2, 1))
    x_rows = _dispatch(table, lpos_t, hn_rows, n_blk * MOE_BLOCK, tmd)
    y_rows = _moe(blk_e, n_used, x_rows, p['moe_w1'], p['moe_w3'], p['moe_w2'], p['layer'])
    lpos_pad = jnp.zeros((T, LANES), I32).at[:, :TOPK_IN].set(lpos)
    return _combine(table, h_new, rf, lpos_pad, y_rows, tmd)


def kernel(x, attn_norm_g, w_in, w_out, fox_f_bias, fox_qn_g, fox_kn_g, fox_out_g, gmlp_ln_g, gmlp_ln_b, gmlp_ws, gmlp_bs, gmlp_out_g, gdn_conv_w, gdn_a_log, gdn_dt_bias, gdn_norm_g, pool_w, pool_scale, pool_out_g, ffn_norm_g, router_g_w, router_g_b, router_e_w, router_e_b, moe_w1, moe_w3, moe_w2):
    B, S, D = x.shape
    names = ('attn_norm_g', 'w_in', 'w_out', 'fox_f_bias', 'fox_qn_g', 'fox_kn_g', 'fox_out_g', 'gmlp_ln_g',
             'gmlp_ln_b', 'gmlp_ws', 'gmlp_bs', 'gmlp_out_g', 'gdn_conv_w', 'gdn_a_log', 'gdn_dt_bias',
             'gdn_norm_g', 'pool_w', 'pool_scale', 'pool_out_g', 'ffn_norm_g', 'router_g_w', 'router_g_b',
             'router_e_w', 'router_e_b', 'moe_w1', 'moe_w3', 'moe_w2')
    vals = (attn_norm_g, w_in, w_out, fox_f_bias, fox_qn_g, fox_kn_g, fox_out_g, gmlp_ln_g, gmlp_ln_b, gmlp_ws,
            gmlp_bs, gmlp_out_g, gdn_conv_w, gdn_a_log, gdn_dt_bias, gdn_norm_g, pool_w, pool_scale, pool_out_g,
            ffn_norm_g, router_g_w, router_g_b, router_e_w, router_e_b, moe_w1, moe_w3, moe_w2)
    h = x.reshape(B * S, D)
    stacked = ('moe_w1', 'moe_w3', 'moe_w2')
    for l in range(w_in.shape[0]):
        p = {n: (v if n in stacked else v[l]) for n, v in zip(names, vals)}
        p['layer'] = l
        h = _layer(h, B, S, p)
    return h.reshape(B, S, D)
```

```python
import functools

import jax
import jax.numpy as jnp
import numpy as np
from jax import lax
from jax.experimental import pallas as pl
from jax.experimental.pallas import tpu as pltpu

F32 = jnp.float32
BF16 = jnp.bfloat16
I32 = jnp.int32

EPS = 1e-6
HEAD_DIM = 64
GROUP_W = 256
N_HEADS = GROUP_W // HEAD_DIM
CHUNK = 64
GMLP_LEN = 128
CONV_K = 4
POOL_WINDOWS = (2, 4, 8, 16)
N_GROUPS = 4
EXPERTS_PER_GROUP = 8
N_EXPERTS = N_GROUPS * EXPERTS_PER_GROUP
TOPK_IN = 2
MOE_BLOCK = 256
RUN_ROWS = 8
LANES = 128
SUBLANES = 8
VMEM_LIMIT = 56 * 1024 * 1024

COL_FQ, COL_FK, COL_FV, COL_GU, COL_GV, COL_DQ, COL_DK, COL_DV, COL_DG, COL_PZ = range(10)
N_BIG_COLS = 10
LANE_FOX, LANE_DECAY, LANE_BETA = 0, 4, 8


def _params(*sem):
    return pltpu.CompilerParams(dimension_semantics=sem, vmem_limit_bytes=VMEM_LIMIT)


def _head_ones():
    r = lax.broadcasted_iota(I32, (GROUP_W, GROUP_W), 0) // HEAD_DIM
    c = lax.broadcasted_iota(I32, (GROUP_W, GROUP_W), 1) // HEAD_DIM
    return (r == c).astype(BF16)


def _head_sums(x, ones_bd):
    hi = x.astype(BF16)
    lo = (x - hi.astype(F32)).astype(BF16)
    return (jnp.dot(hi, ones_bd, preferred_element_type=F32)
            + jnp.dot(lo, ones_bd, preferred_element_type=F32))


def _rms(x, g):
    return x * lax.rsqrt(jnp.mean(x * x, axis=-1, keepdims=True) + EPS) * g


def _mm(a, b):
    return jnp.dot(a.astype(BF16), b.astype(BF16), preferred_element_type=F32)


def _mm_nt(a, b):
    return lax.dot_general(a.astype(BF16), b.astype(BF16), (((1,), (1,)), ((), ())),
                           preferred_element_type=F32)


def _mm_tn(a, b):
    return lax.dot_general(a.astype(BF16), b.astype(BF16), (((0,), (0,)), ((), ())),
                           preferred_element_type=F32)


def _split(a):
    hi = a.astype(BF16)
    return hi, (a - hi.astype(F32)).astype(BF16)


def _mm3(a, b):
    ah, al = _split(a)
    bh, bl = _split(b)
    d = functools.partial(jnp.dot, preferred_element_type=F32)
    return d(ah, bh) + (d(ah, bl) + d(al, bh))


def _inproj_body(x_ref, g_ref, wb_ref, ws_ref, big_ref, small_ref):
    xn = _rms(x_ref[...], g_ref[...]).astype(BF16)
    big_ref[...] = jnp.dot(xn, wb_ref[...], preferred_element_type=F32)
    small_ref[...] = jnp.dot(xn, ws_ref[...], preferred_element_type=F32)


def _inproj(x2d, g, wb, ws, tm):
    T, D = x2d.shape
    nb = wb.shape[1]
    return pl.pallas_call(
        _inproj_body, grid=(T // tm,),
        in_specs=[pl.BlockSpec((tm, D), lambda i: (i, 0)),
                  pl.BlockSpec((1, D), lambda i: (0, 0)),
                  pl.BlockSpec((D, nb), lambda i: (0, 0)),
                  pl.BlockSpec((D, LANES), lambda i: (0, 0))],
        out_specs=[pl.BlockSpec((tm, nb), lambda i: (i, 0)),
                   pl.BlockSpec((tm, LANES), lambda i: (i, 0))],
        out_shape=[jax.ShapeDtypeStruct((T, nb), F32), jax.ShapeDtypeStruct((T, LANES), F32)],
        compiler_params=_params("parallel"), name="inproj")(x2d, g, wb, ws)


def _gates_body(sm_ref, p_ref, col_ref, row_ref, carry_ref, *, ts):
    @pl.when(pl.program_id(1) == 0)
    def _():
        carry_ref[...] = jnp.zeros_like(carry_ref)

    x = sm_ref[...]
    lane = lax.broadcasted_iota(I32, (ts, LANES), 1)
    is_fox = lane < LANE_DECAY
    is_dec = (lane >= LANE_DECAY) & (lane < LANE_BETA)
    is_beta = (lane >= LANE_BETA) & (lane < LANE_BETA + N_HEADS)
    logf = jax.nn.log_sigmoid(x + p_ref[0:1, :])
    g = -jnp.exp(p_ref[2:3, :]) * jax.nn.softplus(x + p_ref[1:2, :])
    beta = jax.nn.sigmoid(x)
    r = lax.broadcasted_iota(I32, (ts, ts), 0)
    c = lax.broadcasted_iota(I32, (ts, ts), 1)
    tri_full = (r >= c).astype(BF16)
    tri_chunk = ((r >= c) & (r // CHUNK == c // CHUNK)).astype(BF16)
    vals = jnp.where(is_fox, logf, jnp.where(is_dec, g, 0.0))
    hi = vals.astype(BF16)
    mid = (vals - hi.astype(F32)).astype(BF16)
    lo = (vals - hi.astype(F32) - mid.astype(F32)).astype(BF16)
    parts = jnp.concatenate([hi, mid, lo], axis=1)

    def tri_sum(tri):
        t = jnp.dot(tri, parts, preferred_element_type=F32)
        return t[:, :LANES] + (t[:, LANES:2 * LANES] + t[:, 2 * LANES:])

    cf = tri_sum(tri_full) + carry_ref[...]
    cg = tri_sum(tri_chunk)
    carry_ref[...] = cf[ts - 1:ts, :]
    out = jnp.where(is_fox, cf, jnp.where(is_dec, cg, jnp.where(is_beta, beta, 0.0)))
    col_ref[...] = out
    row_ref[...] = out.T[:2 * SUBLANES, :]


def _gates(small3, prm, ts):
    B, S, _ = small3.shape
    return pl.pallas_call(
        functools.partial(_gates_body, ts=ts), grid=(B, S // ts),
        in_specs=[pl.BlockSpec((None, ts, LANES), lambda b, j: (b, j, 0)),
                  pl.BlockSpec((SUBLANES, LANES), lambda b, j: (0, 0))],
        out_specs=[pl.BlockSpec((None, ts, LANES), lambda b, j: (b, j, 0)),
                   pl.BlockSpec((None, 2 * SUBLANES, ts), lambda b, j: (b, 0, j))],
        out_shape=[jax.ShapeDtypeStruct((B, S, LANES), F32),
                   jax.ShapeDtypeStruct((B, 2 * SUBLANES, S), F32)],
        scratch_shapes=[pltpu.VMEM((1, LANES), F32)],
        compiler_params=_params("parallel", "arbitrary"), name="gates")(small3, prm)


def _fox_body(q_ref, k_ref, v_ref, ccol_ref, qg_ref, kg_ref, og_ref, o_ref,
              kn_scr, vt_scr, q_scr, m_scr, l_scr, acc_scr, s_scr, *, tq, nk):
    i = pl.program_id(1)
    ones_bd = _head_ones()
    lane = lax.broadcasted_iota(I32, (tq, LANES), 1)
    log2e = 1.0 / np.log(2.0)

    def head_norm(x, g):
        ss = _head_sums(x * x, ones_bd)
        return x * lax.rsqrt(ss * (1.0 / HEAD_DIM) + EPS) * g

    def head_tile(x, h, extra):
        pair = x[:, (h // 2) * LANES:(h // 2 + 1) * LANES]
        if h % 2:
            pair = pltpu.roll(pair, HEAD_DIM, 1)
        return jnp.where(lane < HEAD_DIM, pair, extra).astype(BF16)

    @pl.when(i == 0)
    def _():
        for c in range(nk):
            rows = slice(c * tq, (c + 1) * tq)
            kc = head_norm(k_ref[rows, :], kg_ref[...])
            vt = v_ref[rows, :].T.astype(BF16)
            cc = ccol_ref[rows, :] * (-log2e)
            for h in range(N_HEADS):
                cj = cc[:, LANE_FOX + h:LANE_FOX + h + 1]
                hi = cj.astype(BF16).astype(F32)
                mid = (cj - hi).astype(BF16).astype(F32)
                lo = (cj - hi - mid).astype(BF16).astype(F32)
                extra = jnp.where(lane == HEAD_DIM, hi, jnp.where(lane == HEAD_DIM + 1, mid,
                                  jnp.where(lane == HEAD_DIM + 2, lo, 0.0)))
                kn_scr[h, c] = head_tile(kc, h, extra)
                vt_scr[h, c] = vt[h * HEAD_DIM:(h + 1) * HEAD_DIM, :]

    qn = head_norm(q_ref[...], qg_ref[...]) * (HEAD_DIM ** -0.5 * log2e)
    q_ones = jnp.where(lane < HEAD_DIM + 3, 1.0, 0.0)
    for h in range(N_HEADS):
        q_scr[h] = head_tile(qn, h, q_ones)
    m_scr[...] = jnp.full_like(m_scr, -jnp.inf)
    l_scr[...] = jnp.zeros_like(l_scr)
    acc_scr[...] = jnp.zeros_like(acc_scr)
    causal = (lax.broadcasted_iota(I32, (tq, tq), 0) <= lax.broadcasted_iota(I32, (tq, tq), 1))

    heads = range(N_HEADS)

    def scores(j):
        return [lax.dot_general(kn_scr[h, j], q_scr[h], (((1,), (1,)), ((), ())),
                                preferred_element_type=F32) for h in heads]

    def stash(s):
        for h in heads:
            s_scr[h] = s[h]

    def absorb(j, masked):
        p, alpha = [], []
        for h in heads:
            s = s_scr[h]
            if masked:
                s = jnp.where(causal, s, -jnp.inf)
            m_old = m_scr[h]
            m_new = jnp.maximum(m_old, jnp.max(s, axis=0, keepdims=True))
            alpha.append(jnp.exp2(m_old - m_new))
            ph = jnp.exp2(s - m_new)
            l_scr[h] = alpha[h] * l_scr[h] + jnp.sum(ph, axis=0, keepdims=True)
            m_scr[h] = m_new
            p.append(ph.astype(BF16))
        pv = [jnp.dot(vt_scr[h, j], p[h], preferred_element_type=F32) for h in heads]
        for h in heads:
            acc_scr[h] = alpha[h] * acc_scr[h] + pv[h]

    stash(scores(0))

    def body(j, c):
        s_next = scores(j + 1)
        absorb(j, False)
        stash(s_next)
        return c

    lax.fori_loop(0, i, body, 0)
    absorb(i, True)
    o_t = jnp.concatenate([acc_scr[h] / l_scr[h] for h in range(N_HEADS)], axis=0)
    o_ref[...] = _rms(o_t.T, og_ref[...])


def _fox(big3, gcol, qg, kg, og, tq):
    B, S, _ = big3.shape
    nk = S // tq
    row = pl.BlockSpec((1, GROUP_W), lambda b, i: (0, 0))
    return pl.pallas_call(
        functools.partial(_fox_body, tq=tq, nk=nk), grid=(B, nk),
        in_specs=[pl.BlockSpec((None, tq, GROUP_W), lambda b, i: (b, i, COL_FQ)),
                  pl.BlockSpec((None, S, GROUP_W), lambda b, i: (b, 0, COL_FK)),
                  pl.BlockSpec((None, S, GROUP_W), lambda b, i: (b, 0, COL_FV)),
                  pl.BlockSpec((None, S, LANES), lambda b, i: (b, 0, 0)),
                  row, row, row],
        out_specs=pl.BlockSpec((None, tq, GROUP_W), lambda b, i: (b, i, 0)),
        out_shape=jax.ShapeDtypeStruct((B, S, GROUP_W), F32),
        scratch_shapes=[pltpu.VMEM((N_HEADS, nk, tq, LANES), BF16),
                        pltpu.VMEM((N_HEADS, nk, HEAD_DIM, tq), BF16),
                        pltpu.VMEM((N_HEADS, tq, LANES), BF16),
                        pltpu.VMEM((N_HEADS, 1, tq), F32),
                        pltpu.VMEM((N_HEADS, 1, tq), F32),
                        pltpu.VMEM((N_HEADS, HEAD_DIM, tq), F32),
                        pltpu.VMEM((N_HEADS, tq, tq), F32)],
        compiler_params=_params("parallel", "arbitrary"), name="fox")(
            big3, big3, big3, gcol, qg, kg, og)


def _gelu(x):
    return 0.5 * x * (1.0 + lax.erf(x * (2.0 ** -0.5)))


def _gmlp_body(u_ref, v_ref, lg_ref, lb_ref, ws_ref, bst_ref, og_ref, o_ref, *, nwin):
    L = GMLP_LEN
    r = lax.broadcasted_iota(I32, (L, L), 0) // CHUNK
    c = lax.broadcasted_iota(I32, (L, L), 1) // CHUNK
    mask = r >= c
    ws = [jnp.where(mask, ws_ref[h], 0.0).astype(BF16) for h in range(N_HEADS)]
    for n in range(nwin):
        u = _gelu(u_ref[n * L:(n + 1) * L, :])
        v = _gelu(v_ref[n * L:(n + 1) * L, :])
        mu = jnp.mean(v, axis=-1, keepdims=True)
        vc = v - mu
        var = jnp.mean(vc * vc, axis=-1, keepdims=True)
        vn = (vc * lax.rsqrt(var + EPS) * lg_ref[...] + lb_ref[...]).astype(BF16)
        mixed = jnp.concatenate(
            [jnp.dot(ws[h], vn[:, h * HEAD_DIM:(h + 1) * HEAD_DIM], preferred_element_type=F32)
             for h in range(N_HEADS)], axis=-1) + bst_ref[...]
        o_ref[n * L:(n + 1) * L, :] = _rms(u * mixed, og_ref[...])


def _gmlp(big, lg, lb, ws, bst, og, tm):
    T = big.shape[0]
    row = pl.BlockSpec((1, GROUP_W), lambda i: (0, 0))
    return pl.pallas_call(
        functools.partial(_gmlp_body, nwin=tm // GMLP_LEN), grid=(T // tm,),
        in_specs=[pl.BlockSpec((tm, GROUP_W), lambda i: (i, COL_GU)),
                  pl.BlockSpec((tm, GROUP_W), lambda i: (i, COL_GV)),
                  row, row,
                  pl.BlockSpec((N_HEADS, GMLP_LEN, GMLP_LEN), lambda i: (0, 0, 0)),
                  pl.BlockSpec((GMLP_LEN, GROUP_W), lambda i: (0, 0)),
                  row],
        out_specs=pl.BlockSpec((tm, GROUP_W), lambda i: (i, 0)),
        out_shape=jax.ShapeDtypeStruct((T, GROUP_W), F32),
        compiler_params=_params("parallel"), name="gmlp")(big, big, lg, lb, ws, bst, og)


def _gdn_prep_body(q_ref, k_ref, v_ref, hq_ref, hk_ref, hv_ref, w_ref, qo_ref, ko_ref, vo_ref, *, ts):
    first = pl.program_id(1) == 0
    ones_bd = _head_ones()

    def conv(x_ref, halo_ref, w):
        halo = jnp.where(first, 0.0, halo_ref[...])
        xx = jnp.concatenate([halo, x_ref[...]], axis=0)
        y = w[CONV_K - 1:CONV_K, :] * xx[SUBLANES:, :]
        for j in range(CONV_K - 1):
            y = y + w[j:j + 1, :] * pltpu.roll(xx, CONV_K - 1 - j, 0)[SUBLANES:, :]
        return y * jax.nn.sigmoid(y)

    def l2(t):
        return t * lax.rsqrt(_head_sums(t * t, ones_bd) + EPS)

    qo_ref[...] = l2(conv(q_ref, hq_ref, w_ref[0])) * (HEAD_DIM ** -0.5)
    ko_ref[...] = l2(conv(k_ref, hk_ref, w_ref[1]))
    vo_ref[...] = conv(v_ref, hv_ref, w_ref[2])


def _gdn_prep(big3, conv_w3, ts):
    B, S, _ = big3.shape
    hb = ts // SUBLANES
    blk = lambda col: pl.BlockSpec((None, ts, GROUP_W), lambda b, i: (b, i, col))
    halo = lambda col: pl.BlockSpec((None, SUBLANES, GROUP_W),
                                    lambda b, i: (b, jnp.maximum(i * hb - 1, 0), col))
    out = pl.BlockSpec((None, ts, GROUP_W), lambda b, i: (b, i, 0))
    shp = jax.ShapeDtypeStruct((B, S, GROUP_W), F32)
    return pl.pallas_call(
        functools.partial(_gdn_prep_body, ts=ts), grid=(B, S // ts),
        in_specs=[blk(COL_DQ), blk(COL_DK), blk(COL_DV), halo(COL_DQ), halo(COL_DK), halo(COL_DV),
                  pl.BlockSpec((3, CONV_K, GROUP_W), lambda b, i: (0, 0, 0))],
        out_specs=[out, out, out], out_shape=[shp, shp, shp],
        compiler_params=_params("parallel", "parallel"), name="gdn_prep")(
            big3, big3, big3, big3, big3, big3, conv_w3)


def _gdn_body(q_ref, k_ref, v_ref, gate_ref, gcol_ref, grow_ref, ng_ref, o_ref,
              s_scr, u_scr, wq_scr, a_scr, kd_scr, dl_scr, t_scr, p_scr, rhs_scr, *, nchunk):
    C = CHUNK

    @pl.when(pl.program_id(1) == 0)
    def _():
        s_scr[...] = jnp.zeros_like(s_scr)

    r = lax.broadcasted_iota(I32, (C, C), 0)
    c = lax.broadcasted_iota(I32, (C, C), 1)
    tri = r >= c
    strict = r > c
    eye = (r == c).astype(F32)

    items = [(n, h) for n in range(nchunk) for h in range(N_HEADS)]
    mmb = functools.partial(jnp.dot, preferred_element_type=F32)
    for it, (n, h) in enumerate(items):
        rows = slice(n * C, (n + 1) * C)
        sl = slice(h * HEAD_DIM, (h + 1) * HEAD_DIM)
        q, k, v = q_ref[rows, sl], k_ref[rows, sl], v_ref[rows, sl]
        gc = gcol_ref[rows, LANE_DECAY + h:LANE_DECAY + h + 1]
        gr = grow_ref[n, LANE_DECAY + h:LANE_DECAY + h + 1, :]
        beta = gcol_ref[rows, LANE_BETA + h:LANE_BETA + h + 1]
        decay = jnp.exp(jnp.where(tri, gc - gr, -jnp.inf))
        kb = k * beta
        kk = _mm_nt(jnp.concatenate([kb, q], axis=0), k)
        x = jnp.where(strict, -(kk[:C] * decay), 0.0)
        t_scr[it] = eye + x
        p_scr[it] = x.astype(BF16)
        eg = jnp.exp(gc)
        g_last = gc[C - 1:C, :]
        rhs_scr[it] = jnp.concatenate([v * beta, kb * eg], axis=1).astype(BF16)
        wq_scr[n, h, C:, :] = (q * eg).astype(BF16)
        a_scr[n, h] = jnp.where(tri, kk[C:] * decay, 0.0).astype(BF16)
        kd_scr[n, h] = (k * jnp.exp(g_last - gc)).astype(BF16)
        dl_scr[n, h] = jnp.broadcast_to(jnp.exp(g_last), (1, HEAD_DIM))
    for _ in range(5):
        for it in range(len(items)):
            p = p_scr[it]
            p_scr[it] = mmb(p, p).astype(BF16)
        for it in range(len(items)):
            t = t_scr[it]
            t_scr[it] = t + mmb(t.astype(BF16), p_scr[it])
    for it, (n, h) in enumerate(items):
        uw = mmb(t_scr[it].astype(BF16), rhs_scr[it])
        u_scr[n, h] = uw[:, :HEAD_DIM]
        wq_scr[n, h, :C, :] = uw[:, HEAD_DIM:].astype(BF16)

    heads = range(N_HEADS)
    state = [s_scr[h] for h in heads]
    for n in range(nchunk):
        rows = slice(n * C, (n + 1) * C)
        ws = [mmb(wq_scr[n, h], state[h].astype(BF16)) for h in heads]
        vb = [(u_scr[n, h] - ws[h][:C]).astype(BF16) for h in heads]
        o = [ws[h][C:] + mmb(a_scr[n, h], vb[h]) for h in heads]
        state = [state[h] * dl_scr[n, h] + lax.dot_general(
            kd_scr[n, h], vb[h], (((0,), (0,)), ((), ())), preferred_element_type=F32) for h in heads]
        gate = gate_ref[rows, :]
        y = jnp.concatenate([_rms(o[h], ng_ref[...]) for h in heads], axis=-1)
        o_ref[rows, :] = y * (gate * jax.nn.sigmoid(gate))
    for h in heads:
        s_scr[h] = state[h]


def _gdn(qn, kn, vv, big3, gcol, grow4, ng, ts):
    B, S, _ = qn.shape
    nchunk = ts // CHUNK
    blk = pl.BlockSpec((None, ts, GROUP_W), lambda b, i: (b, i, 0))
    return pl.pallas_call(
        functools.partial(_gdn_body, nchunk=nchunk), grid=(B, S // ts),
        in_specs=[blk, blk, blk,
                  pl.BlockSpec((None, ts, GROUP_W), lambda b, i: (b, i, COL_DG)),
                  pl.BlockSpec((None, ts, LANES), lambda b, i: (b, i, 0)),
                  pl.BlockSpec((None, nchunk, 2 * SUBLANES, CHUNK), lambda b, i: (b, i, 0, 0)),
                  pl.BlockSpec((1, HEAD_DIM), lambda b, i: (0, 0))],
        out_specs=blk, out_shape=jax.ShapeDtypeStruct((B, S, GROUP_W), F32),
        scratch_shapes=[pltpu.VMEM((N_HEADS, HEAD_DIM, HEAD_DIM), F32),
                        pltpu.VMEM((nchunk, N_HEADS, CHUNK, HEAD_DIM), F32),
                        pltpu.VMEM((nchunk, N_HEADS, 2 * CHUNK, HEAD_DIM), BF16),
                        pltpu.VMEM((nchunk, N_HEADS, CHUNK, CHUNK), BF16),
                        pltpu.VMEM((nchunk, N_HEADS, CHUNK, HEAD_DIM), BF16),
                        pltpu.VMEM((nchunk, N_HEADS, 1, HEAD_DIM), F32),
                        pltpu.VMEM((nchunk * N_HEADS, CHUNK, CHUNK), F32),
                        pltpu.VMEM((nchunk * N_HEADS, CHUNK, CHUNK), BF16),
                        pltpu.VMEM((nchunk * N_HEADS, CHUNK, 2 * HEAD_DIM), BF16)],
        compiler_params=_params("parallel", "arbitrary"), name="gdn")(
            qn, kn, vv, big3, gcol, grow4, ng)


def _pool_body(z_ref, halo_ref, w_ref, sc_ref, og_ref, o_ref, *, ts):
    i = pl.program_id(1)
    hr = 2 * SUBLANES
    z = z_ref[...]
    halo = jnp.where(i == 0, 0.0, halo_ref[...])
    s1 = jnp.concatenate([halo, z], axis=0)
    s2 = s1 + pltpu.roll(s1, 1, 0)
    s4 = s2 + pltpu.roll(s2, 2, 0)
    s8 = s4 + pltpu.roll(s4, 4, 0)
    s16 = s8 + pltpu.roll(s8, 8, 0)
    grp = lax.broadcasted_iota(I32, (ts, GROUP_W), 1) // (GROUP_W // len(POOL_WINDOWS))
    t = lax.broadcasted_iota(I32, (ts, GROUP_W), 0) + i * ts
    total = jnp.where(grp == 0, s2[hr:], jnp.where(grp == 1, s4[hr:], jnp.where(grp == 2, s8[hr:], s16[hr:])))
    win = jnp.where(grp == 0, POOL_WINDOWS[0], jnp.where(grp == 1, POOL_WINDOWS[1],
                    jnp.where(grp == 2, POOL_WINDOWS[2], POOL_WINDOWS[3])))
    pooled = total / jnp.minimum(t + 1, win).astype(F32)
    y = _mm(pooled - z, w_ref[...]) * sc_ref[...]
    o_ref[...] = _rms(y, og_ref[...])


def _pool(big3, wbd, sc, og, ts):
    B, S, _ = big3.shape
    hr = 2 * SUBLANES
    hb = ts // hr
    row = pl.BlockSpec((1, GROUP_W), lambda b, i: (0, 0))
    return pl.pallas_call(
        functools.partial(_pool_body, ts=ts), grid=(B, S // ts),
        in_specs=[pl.BlockSpec((None, ts, GROUP_W), lambda b, i: (b, i, COL_PZ)),
                  pl.BlockSpec((None, hr, GROUP_W), lambda b, i: (b, jnp.maximum(i * hb - 1, 0), COL_PZ)),
                  pl.BlockSpec((GROUP_W, GROUP_W), lambda b, i: (0, 0)), row, row],
        out_specs=pl.BlockSpec((None, ts, GROUP_W), lambda b, i: (b, i, 0)),
        out_shape=jax.ShapeDtypeStruct((B, S, GROUP_W), F32),
        compiler_params=_params("parallel", "parallel"), name="pool")(big3, big3, wbd, sc, og)


def _outproj_body(ya_ref, yb_ref, yc_ref, yd_ref, h_ref, wo_ref, g_ref, rw_ref, rb_ref,
                  hnew_ref, hn_ref, ri_ref, rf_ref, *, tm, d):
    y = jnp.concatenate([ya_ref[...], yb_ref[...], yc_ref[...], yd_ref[...]], axis=-1).astype(BF16)
    h_new = h_ref[...] + jnp.dot(y, wo_ref[...], preferred_element_type=F32)
    hnew_ref[...] = h_new
    hn = _rms(h_new, g_ref[...])
    hn_hi, hn_lo = _split(hn)
    hn_ref[...] = hn_hi
    t = jnp.dot(hn_hi, rw_ref[...], preferred_element_type=F32)
    logits = (t[:, :LANES] + t[:, LANES:]
              + jnp.dot(hn_lo, rw_ref[:, :LANES], preferred_element_type=F32)) + rb_ref[...]
    lane = lax.broadcasted_iota(I32, (tm, LANES), 1)
    neg = -jnp.inf
    big_lane = LANES

    def masked_top(vals, mask):
        v = jnp.where(mask, vals, neg)
        mx = jnp.max(v, axis=-1, keepdims=True)
        idx = jnp.min(jnp.where(mask & (v == mx), lane, big_lane), axis=-1, keepdims=True)
        return v, mx, idx

    gmask = lane < N_GROUPS
    gv, gmx, gidx = masked_top(logits, gmask)
    g_top = 1.0 / jnp.sum(jnp.where(gmask, jnp.exp(gv - gmx), 0.0), axis=-1, keepdims=True)
    lo = N_GROUPS + gidx * EXPERTS_PER_GROUP
    emask = (lane >= lo) & (lane < lo + EXPERTS_PER_GROUP)
    ev, emx, eidx1 = masked_top(logits, emask)
    esum = jnp.sum(jnp.where(emask, jnp.exp(ev - emx), 0.0), axis=-1, keepdims=True)
    p1 = 1.0 / esum
    _, emx2, eidx2 = masked_top(logits, emask & (lane != eidx1))
    p2 = jnp.exp(emx2 - emx) / esum
    denom = p1 + p2
    ri_ref[...] = jnp.where(lane == 0, eidx1 - N_GROUPS, jnp.where(lane == 1, eidx2 - N_GROUPS, 0))
    rf_ref[...] = jnp.where(lane == 0, g_top * p1 / denom, jnp.where(lane == 1, g_top * p2 / denom, 0.0))


def _outproj(ya, yb, yc, yd, h2d, wo, g, rw, rb, tm):
    T, D = h2d.shape
    nseg = D // LANES
    yblk = pl.BlockSpec((tm, GROUP_W), lambda i: (i, 0))
    return pl.pallas_call(
        functools.partial(_outproj_body, tm=tm, d=D), grid=(T // tm,),
        in_specs=[yblk, yblk, yblk, yblk,
                  pl.BlockSpec((tm, D), lambda i: (i, 0)),
                  pl.BlockSpec((D, D), lambda i: (0, 0)),
                  pl.BlockSpec((1, D), lambda i: (0, 0)),
                  pl.BlockSpec((D, 2 * LANES), lambda i: (0, 0)),
                  pl.BlockSpec((1, LANES), lambda i: (0, 0))],
        out_specs=[pl.BlockSpec((tm, D), lambda i: (i, 0)),
                   pl.BlockSpec((tm, D), lambda i: (i, 0)),
                   pl.BlockSpec((tm, LANES), lambda i: (i, 0)),
                   pl.BlockSpec((tm, LANES), lambda i: (i, 0))],
        out_shape=[jax.ShapeDtypeStruct((T, D), F32),
                   jax.ShapeDtypeStruct((T, D), BF16),
                   jax.ShapeDtypeStruct((T, LANES), I32),
                   jax.ShapeDtypeStruct((T, LANES), F32)],
        compiler_params=_params("parallel"), name="outproj")(ya, yb, yc, yd, h2d, wo, g, rw, rb)


def _rank_body(ri_ref, rank_ref, cnt_ref, *, tm):
    lane = lax.broadcasted_iota(I32, (tm, LANES), 1)
    e = ri_ref[...]
    oh = [lane == e[:, s:s + 1] for s in range(TOPK_IN)]
    m = (oh[0].astype(F32) + oh[1].astype(F32)).astype(BF16)
    below = (lax.broadcasted_iota(I32, (tm, tm), 0) > lax.broadcasted_iota(I32, (tm, tm), 1)).astype(BF16)
    before = jnp.dot(below, m, preferred_element_type=F32)
    rank = [jnp.sum(jnp.where(oh[s], before, 0.0), axis=-1, keepdims=True) for s in range(TOPK_IN)]
    rank_ref[...] = jnp.where(lane == 0, rank[0], jnp.where(lane == 1, rank[1], 0.0)).astype(I32)
    total = before[tm - 1:tm, :] + m[tm - 1:tm, :].astype(F32)
    cnt_ref[...] = jnp.broadcast_to(total, cnt_ref.shape).astype(I32)


def _rank(ri, tm):
    T = ri.shape[0]
    return pl.pallas_call(
        functools.partial(_rank_body, tm=tm), grid=(T // tm,),
        in_specs=[pl.BlockSpec((tm, LANES), lambda i: (i, 0))],
        out_specs=[pl.BlockSpec((tm, LANES), lambda i: (i, 0)),
                   pl.BlockSpec((SUBLANES, LANES), lambda i: (i, 0))],
        out_shape=[jax.ShapeDtypeStruct((T, LANES), I32),
                   jax.ShapeDtypeStruct((T // tm * SUBLANES, LANES), I32)],
        compiler_params=_params("parallel"), name="rank")(ri)


def _dispatch_plan(expert, lrank, cnt_tile, tm):
    T = expert.shape[0]
    nt = T // tm
    counts = jnp.sum(cnt_tile, axis=0)
    padded = (counts + (RUN_ROWS - 1) + MOE_BLOCK - 1) // MOE_BLOCK * MOE_BLOCK
    pad_end = jnp.cumsum(padded)
    pad_start = pad_end - padded
    gstart = pad_start[None, :] + jnp.cumsum(cnt_tile, axis=0) - cnt_tile
    nchunk = (cnt_tile + RUN_ROWS - 1) // RUN_ROWS
    chunk_end = jnp.cumsum(nchunk, axis=1)
    lstart = (chunk_end - nchunk) * RUN_ROWS
    onehot = expert[:, :, None] == jnp.arange(N_EXPERTS, dtype=I32)[None, None, :]
    pick = lambda tab: jnp.sum(jnp.where(onehot, jnp.repeat(tab, tm, axis=0)[:, None, :], 0), axis=-1)
    lpos =(lrank + pick(lstart)).astype(I32)
    max_chunks = tm * TOPK_IN // RUN_ROWS + N_EXPERTS
    c = jnp.arange(max_chunks, dtype=I32)
    ce = jnp.minimum(jnp.sum(chunk_end[:, None, :] <= c[None, :, None], axis=-1), N_EXPERTS - 1)
    ce_hot = ce[:, :, None] == jnp.arange(N_EXPERTS, dtype=I32)[None, None, :]
    take = lambda tab: jnp.sum(jnp.where(ce_hot, tab[:, None, :], 0), axis=-1)
    chunk_row = take(gstart) + (c[None, :] - take(chunk_end - nchunk)) * RUN_ROWS
    chunk_row = jnp.where(c[None, :] < chunk_end[:, -1:], chunk_row, 0)
    table = jnp.zeros((nt, 1, LANES), I32).at[:, 0, :max_chunks].set(chunk_row.astype(I32))
    table = table.at[:, 0, LANES - 1].set(chunk_end[:, -1].astype(I32))
    n_blk = -(-(T * TOPK_IN + N_EXPERTS * (RUN_ROWS - 1)) // MOE_BLOCK) + N_EXPERTS
    blk_start = jnp.arange(n_blk, dtype=I32) * MOE_BLOCK
    blk_e = jnp.minimum(jnp.sum(pad_end[None, :] <= blk_start[:, None], axis=-1), N_EXPERTS - 1).astype(I32)
    n_used = (pad_end[-1] // MOE_BLOCK).astype(I32).reshape(1)
    return lpos, table, blk_e, n_used, n_blk


def _dispatch_body(tab_ref, lpos_ref, hn_ref, xz_hbm, x_hbm, xs, sem, n_prev, *, tm, nseg, nrow):
    del xz_hbm
    i = pl.program_id(0)
    slot = i % 2
    row = lax.broadcasted_iota(I32, (nrow, tm), 0)
    sel = (row == lpos_ref[0:1, :]) | (row == lpos_ref[1:2, :])
    rows = jnp.dot(sel.astype(BF16), hn_ref[...], preferred_element_type=F32)
    for k in range(nseg):
        xs[slot, pl.ds(k, nrow, stride=nseg), :] = rows[:, k * LANES:(k + 1) * LANES]

    step = RUN_ROWS * nseg

    def chunk_copy(sl, c):
        return pltpu.make_async_copy(
            xs.at[sl, pl.ds(pl.multiple_of(c * step, step), step), :],
            x_hbm.at[pl.ds(pl.multiple_of(tab_ref[0, 0, c] * nseg, nseg), step), :], sem.at[sl])

    def wait_chunks(sl, n):
        def wait(c, carry):
            chunk_copy(sl, 0).wait()
            return carry
        lax.fori_loop(0, n, wait, 0)

    n_chunks = tab_ref[0, 0, LANES - 1]

    @pl.when(i > 0)
    def _():
        wait_chunks(1 - slot, n_prev[0])

    def start(c, carry):
        chunk_copy(slot, c).start()
        return carry

    lax.fori_loop(0, n_chunks, start, 0)
    n_prev[0] = n_chunks

    @pl.when(i == pl.num_programs(0) - 1)
    def _():
        wait_chunks(slot, n_chunks)


def _dispatch(table, lpos_t, hn, n_pad, tm):
    T, D = hn.shape
    nseg = D // LANES
    nrow = tm * TOPK_IN + N_EXPERTS * RUN_ROWS
    x_zero = jnp.zeros((n_pad * nseg, LANES), F32)
    return pl.pallas_call(
        functools.partial(_dispatch_body, tm=tm, nseg=nseg, nrow=nrow), grid=(T // tm,),
        in_specs=[pl.BlockSpec((1, 1, LANES), lambda i: (i, 0, 0), memory_space=pltpu.SMEM),
                  pl.BlockSpec((None, SUBLANES, tm), lambda i: (i, 0, 0)),
                  pl.BlockSpec((tm, D), lambda i: (i, 0)),
                  pl.BlockSpec(memory_space=pl.ANY)],
        out_specs=pl.BlockSpec(memory_space=pl.ANY),
        out_shape=jax.ShapeDtypeStruct((n_pad * nseg, LANES), F32),
        scratch_shapes=[pltpu.VMEM((2, nrow * nseg, LANES), F32), pltpu.SemaphoreType.DMA((2,)),
                        pltpu.SMEM((1,), I32)],
        input_output_aliases={3: 0},
        compiler_params=_params("arbitrary"), name="dispatch")(table, lpos_t, hn, x_zero)


def _moe_body(blk_e_ref, n_used_ref, x_ref, w1_ref, w3_ref, w2_ref, y_ref, w1b, w3b, w2b, *, nseg):
    b = pl.program_id(0)
    R = MOE_BLOCK

    @pl.when(b < n_used_ref[0])
    def _():
        @pl.when((b == 0) | (blk_e_ref[b] != blk_e_ref[jnp.maximum(b - 1, 0)]))
        def _():
            w1b[...] = w1_ref[...].astype(BF16)
            w3b[...] = w3_ref[...].astype(BF16)
            w2b[...] = w2_ref[...].astype(BF16)

        x = jnp.concatenate([x_ref[pl.ds(s, R, stride=nseg), :] for s in range(nseg)], axis=-1).astype(BF16)
        a = jnp.dot(x, w1b[...], preferred_element_type=F32)
        g = jnp.dot(x, w3b[...], preferred_element_type=F32)
        hid = (a * jax.nn.sigmoid(a) * g).astype(BF16)
        y = jnp.dot(hid, w2b[...], preferred_element_type=F32)
        for s in range(nseg):
            y_ref[pl.ds(s, R, stride=nseg), :] = y[:, s * LANES:(s + 1) * LANES]

    @pl.when(b >= n_used_ref[0])
    def _():
        y_ref[...] = jnp.zeros_like(y_ref)


def _moe(blk_e, n_used, x_rows, w1, w3, w2, layer):
    n_blk = blk_e.shape[0]
    _, _, D, DE = w1.shape
    nseg = D // LANES
    R = MOE_BLOCK
    rows = lambda b, be, nu: (jnp.minimum(b, nu[0] - 1), 0)
    wmap = lambda b, be, nu: (layer, be[b], 0, 0)
    grid_spec = pltpu.PrefetchScalarGridSpec(
        num_scalar_prefetch=2, grid=(n_blk,),
        in_specs=[pl.BlockSpec((R * nseg, LANES), rows),
                  pl.BlockSpec((None, None, D, DE), wmap),
                  pl.BlockSpec((None, None, D, DE), wmap),
                  pl.BlockSpec((None, None, DE, D), wmap)],
        out_specs=pl.BlockSpec((R * nseg, LANES), lambda b, be, nu: (b, 0)),
        scratch_shapes=[pltpu.VMEM((D, DE), BF16), pltpu.VMEM((D, DE), BF16), pltpu.VMEM((DE, D), BF16)])
    return pl.pallas_call(
        functools.partial(_moe_body, nseg=nseg), grid_spec=grid_spec,
        out_shape=jax.ShapeDtypeStruct(x_rows.shape, F32),
        compiler_params=_params("arbitrary"), name="moe")(blk_e, n_used, x_rows, w1, w3, w2)


def _combine_body(tab_ref, tabn_ref, h_ref, rf_ref, lpos_ref, y_hbm, o_ref, ybuf, sem, *, tm, nseg, nrow):
    i = pl.program_id(0)
    slot = i % 2
    step = RUN_ROWS * nseg

    def start_chunks(tab, sl):
        def start(c, carry):
            pltpu.make_async_copy(
                y_hbm.at[pl.ds(pl.multiple_of(tab[0, 0, c] * nseg, nseg), step), :],
                ybuf.at[sl, pl.ds(pl.multiple_of(c * step, step), step), :], sem.at[sl]).start()
            return carry
        lax.fori_loop(0, tab[0, 0, LANES - 1], start, 0)

    @pl.when(i == 0)
    def _():
        ybuf[...] = jnp.zeros_like(ybuf)
        start_chunks(tab_ref, 0)

    @pl.when(i + 1 < pl.num_programs(0))
    def _():
        start_chunks(tabn_ref, 1 - slot)

    col = lax.broadcasted_iota(I32, (tm, nrow), 1)
    sel = jnp.zeros((tm, nrow), F32)
    for s in range(TOPK_IN):
        sel = sel + jnp.where(col == lpos_ref[:, s:s + 1], rf_ref[:, s:s + 1], 0.0)

    def wait(c, carry):
        pltpu.make_async_copy(y_hbm.at[pl.ds(0, step), :], ybuf.at[slot, pl.ds(0, step), :],
                              sem.at[slot]).wait()
        return carry

    lax.fori_loop(0, tab_ref[0, 0, LANES - 1], wait, 0)
    y = jnp.concatenate([ybuf[slot, pl.ds(k, nrow, stride=nseg), :] for k in range(nseg)], axis=-1)
    o_ref[...] = h_ref[...] + jnp.dot(sel.astype(BF16), y.astype(BF16), preferred_element_type=F32)


def _combine(table, h2d, rf, lpos, y_rows, tm):
    T, D = h2d.shape
    nseg = D // LANES
    nt = T // tm
    nrow = tm * TOPK_IN + N_EXPERTS * RUN_ROWS
    return pl.pallas_call(
        functools.partial(_combine_body, tm=tm, nseg=nseg, nrow=nrow), grid=(nt,),
        in_specs=[pl.BlockSpec((1, 1, LANES), lambda i: (i, 0, 0), memory_space=pltpu.SMEM),
                  pl.BlockSpec((1, 1, LANES), lambda i: (jnp.minimum(i + 1, nt - 1), 0, 0),
                               memory_space=pltpu.SMEM),
                  pl.BlockSpec((tm, D), lambda i: (i, 0)),
                  pl.BlockSpec((tm, LANES), lambda i: (i, 0)),
                  pl.BlockSpec((tm, LANES), lambda i: (i, 0)),
                  pl.BlockSpec(memory_space=pl.ANY)],
        out_specs=pl.BlockSpec((tm, D), lambda i: (i, 0)),
        out_shape=jax.ShapeDtypeStruct((T, D), F32),
        scratch_shapes=[pltpu.VMEM((2, nrow * nseg, LANES), F32), pltpu.SemaphoreType.DMA((2,))],
        compiler_params=_params("arbitrary"), name="combine")(table, table, h2d, rf, lpos, y_rows)


def _pad_lanes(a, lane0, rows=1):
    out = jnp.zeros((rows, LANES), a.dtype)
    return out.at[:, lane0:lane0 + a.shape[-1]].set(a.reshape(rows, -1))


def _layer(h2d, B, S, p):
    T, D = h2d.shape
    tile = lambda a, n: jnp.tile(a.reshape(1, -1), (1, n))

    offs = np.cumsum([0, GROUP_W, GROUP_W, GROUP_W, N_HEADS, GROUP_W, GROUP_W,
                      GROUP_W, GROUP_W, GROUP_W, N_HEADS, N_HEADS, GROUP_W, GROUP_W])
    seg = lambda k: p['w_in'][:, offs[k]:offs[k + 1]]
    wb = jnp.concatenate([seg(0), seg(1), seg(2), seg(4), seg(5), seg(6), seg(7), seg(8), seg(11), seg(12)],
                         axis=1).astype(BF16)
    ws = jnp.zeros((D, LANES), F32)
    ws = ws.at[:, LANE_FOX:LANE_FOX + N_HEADS].set(seg(3))
    ws = ws.at[:, LANE_DECAY:LANE_DECAY + N_HEADS].set(seg(9))
    ws = ws.at[:, LANE_BETA:LANE_BETA + N_HEADS].set(seg(10)).astype(BF16)
    gate_prm = jnp.concatenate([_pad_lanes(p['fox_f_bias'], LANE_FOX), _pad_lanes(p['gdn_dt_bias'], LANE_DECAY),
                                _pad_lanes(p['gdn_a_log'], LANE_DECAY), jnp.zeros((SUBLANES - 3, LANES), F32)], axis=0)

    big, small = _inproj(h2d, p['attn_norm_g'].reshape(1, D), wb, ws, tm=min(512, T))
    big3 = big.reshape(B, S, N_BIG_COLS * GROUP_W)
    ts = min(512, S)
    gcol, grow = _gates(small.reshape(B, S, LANES), gate_prm, ts)

    tq = min(256, S)
    ya = _fox(big3, gcol, tile(p['fox_qn_g'], N_HEADS), tile(p['fox_kn_g'], N_HEADS),
              p['fox_out_g'].reshape(1, GROUP_W), tq)

    bst = jnp.repeat(p['gmlp_bs'].T, HEAD_DIM, axis=1)
    yb = _gmlp(big, p['gmlp_ln_g'].reshape(1, -1), p['gmlp_ln_b'].reshape(1, -1), p['gmlp_ws'], bst,
               p['gmlp_out_g'].reshape(1, -1), tm=min(512, T))

    conv_w3 = p['gdn_conv_w'].reshape(CONV_K, 3, GROUP_W).transpose(1, 0, 2)
    qn, kn, vv = _gdn_prep(big3, conv_w3, ts)
    grow4 = grow.reshape(B, 2 * SUBLANES, S // CHUNK, CHUNK).transpose(0, 2, 1, 3)
    yc = _gdn(qn, kn, vv, big3, gcol, grow4, p['gdn_norm_g'].reshape(1, HEAD_DIM), ts)

    wbd = jax.scipy.linalg.block_diag(*[p['pool_w'][g] for g in range(len(POOL_WINDOWS))]).astype(BF16)
    yd = _pool(big3, wbd, p['pool_scale'].reshape(1, -1), p['pool_out_g'].reshape(1, -1), ts)

    rw = jnp.zeros((D, LANES), F32).at[:, :N_GROUPS].set(p['router_g_w'])
    rw = rw.at[:, N_GROUPS:N_GROUPS + N_EXPERTS].set(p['router_e_w'])
    rw = jnp.concatenate(_split(rw), axis=1)
    rb = jnp.zeros((1, LANES), F32).at[0, :N_GROUPS].set(p['router_g_b'])
    rb = rb.at[0, N_GROUPS:N_GROUPS + N_EXPERTS].set(p['router_e_b'])
    flat = lambda a: a.reshape(T, GROUP_W)
    h_new, hn_rows, ri, rf = _outproj(flat(ya), yb, flat(yc), flat(yd), h2d, p['w_out'].astype(BF16),
                                      p['ffn_norm_g'].reshape(1, D), rw, rb, tm=min(512, T))

    tmd = min(256, T)
    lrank, cnt = _rank(ri, tmd)
    cnt_tile = cnt.reshape(T // tmd, SUBLANES, LANES)[:, 0, :N_EXPERTS]
    lpos, table, blk_e, n_used, n_blk = _dispatch_plan(ri[:, :TOPK_IN], lrank[:, :TOPK_IN], cnt_tile, tmd)
    lpos_t = jnp.zeros((T // tmd, SUBLANES, tmd), I32).at[:, :TOPK_IN, :].set(
        lpos.reshape(T // tmd, tmd, TOPK_IN).transpose(0, 2, 1))
    x_rows = _dispatch(table, lpos_t, hn_rows, n_blk * MOE_BLOCK, tmd)
    y_rows = _moe(blk_e, n_used, x_rows, p['moe_w1'], p['moe_w3'], p['moe_w2'], p['layer'])
    lpos_pad = jnp.zeros((T, LANES), I32).at[:, :TOPK_IN].set(lpos)
    return _combine(table, h_new, rf, lpos_pad, y_rows, tmd)


def kernel(x, attn_norm_g, w_in, w_out, fox_f_bias, fox_qn_g, fox_kn_g, fox_out_g, gmlp_ln_g, gmlp_ln_b, gmlp_ws, gmlp_bs, gmlp_out_g, gdn_conv_w, gdn_a_log, gdn_dt_bias, gdn_norm_g, pool_w, pool_scale, pool_out_g, ffn_norm_g, router_g_w, router_g_b, router_e_w, router_e_b, moe_w1, moe_w3, moe_w2):
    B, S, D = x.shape
    names = ('attn_norm_g', 'w_in', 'w_out', 'fox_f_bias', 'fox_qn_g', 'fox_kn_g', 'fox_out_g', 'gmlp_ln_g',
             'gmlp_ln_b', 'gmlp_ws', 'gmlp_bs', 'gmlp_out_g', 'gdn_conv_w', 'gdn_a_log', 'gdn_dt_bias',
             'gdn_norm_g', 'pool_w', 'pool_scale', 'pool_out_g', 'ffn_norm_g', 'router_g_w', 'router_g_b',
             'router_e_w', 'router_e_b', 'moe_w1', 'moe_w3', 'moe_w2')
    vals = (attn_norm_g, w_in, w_out, fox_f_bias, fox_qn_g, fox_kn_g, fox_out_g, gmlp_ln_g, gmlp_ln_b, gmlp_ws,
            gmlp_bs, gmlp_out_g, gdn_conv_w, gdn_a_log, gdn_dt_bias, gdn_norm_g, pool_w, pool_scale, pool_out_g,
            ffn_norm_g, router_g_w, router_g_b, router_e_w, router_e_b, moe_w1, moe_w3, moe_w2)
    h = x.reshape(B * S, D)
    stacked = ('moe_w1', 'moe_w3', 'moe_w2')
    for l in range(w_in.shape[0]):
        p = {n: (v if n in stacked else v[l]) for n, v in zip(names, vals)}
        p['layer'] = l
        h = _layer(h, B, S, p)
    return h.reshape(B, S, D)
```

```python
import functools

import jax
import jax.numpy as jnp
import numpy as np
from jax import lax
from jax.experimental import pallas as pl
from jax.experimental.pallas import tpu as pltpu

F32 = jnp.float32
BF16 = jnp.bfloat16
I32 = jnp.int32

EPS = 1e-6
HEAD_DIM = 64
GROUP_W = 256
N_HEADS = GROUP_W // HEAD_DIM
CHUNK = 64
GMLP_LEN = 128
CONV_K = 4
POOL_WINDOWS = (2, 4, 8, 16)
N_GROUPS = 4
EXPERTS_PER_GROUP = 8
N_EXPERTS = N_GROUPS * EXPERTS_PER_GROUP
TOPK_IN = 2
MOE_BLOCK = 256
FOX_VT_ROWS = HEAD_DIM + 16
FOX_BOUND_LOG2 = 40.0
FOX_SKIP_LOG2 = 160.0
RUN_ROWS = 8
LANES = 128
SUBLANES = 8
VMEM_LIMIT = 56 * 1024 * 1024

COL_FQ, COL_FK, COL_FV, COL_GU, COL_GV, COL_DQ, COL_DK, COL_DV, COL_DG, COL_PZ = range(10)
N_BIG_COLS = 10
LANE_FOX, LANE_DECAY, LANE_BETA = 0, 4, 8


def _params(*sem):
    return pltpu.CompilerParams(dimension_semantics=sem, vmem_limit_bytes=VMEM_LIMIT)


def _head_ones():
    r = lax.broadcasted_iota(I32, (GROUP_W, GROUP_W), 0) // HEAD_DIM
    c = lax.broadcasted_iota(I32, (GROUP_W, GROUP_W), 1) // HEAD_DIM
    return (r == c).astype(BF16)


def _head_sums(x, ones_bd):
    hi = x.astype(BF16)
    lo = (x - hi.astype(F32)).astype(BF16)
    return (jnp.dot(hi, ones_bd, preferred_element_type=F32)
            + jnp.dot(lo, ones_bd, preferred_element_type=F32))


def _rms(x, g):
    return x * lax.rsqrt(jnp.mean(x * x, axis=-1, keepdims=True) + EPS) * g


def _mm(a, b):
    return jnp.dot(a.astype(BF16), b.astype(BF16), preferred_element_type=F32)


def _mm_nt(a, b):
    return lax.dot_general(a.astype(BF16), b.astype(BF16), (((1,), (1,)), ((), ())),
                           preferred_element_type=F32)


def _mm_tn(a, b):
    return lax.dot_general(a.astype(BF16), b.astype(BF16), (((0,), (0,)), ((), ())),
                           preferred_element_type=F32)


def _split(a):
    hi = a.astype(BF16)
    return hi, (a - hi.astype(F32)).astype(BF16)


def _mm3(a, b):
    ah, al = _split(a)
    bh, bl = _split(b)
    d = functools.partial(jnp.dot, preferred_element_type=F32)
    return d(ah, bh) + (d(ah, bl) + d(al, bh))


def _inproj_body(x_ref, g_ref, wb_ref, ws_ref, big_ref, small_ref):
    xn = _rms(x_ref[...], g_ref[...]).astype(BF16)
    big_ref[...] = jnp.dot(xn, wb_ref[...], preferred_element_type=F32)
    small_ref[...] = jnp.dot(xn, ws_ref[...], preferred_element_type=F32)


def _inproj(x2d, g, wb, ws, tm):
    T, D = x2d.shape
    nb = wb.shape[1]
    return pl.pallas_call(
        _inproj_body, grid=(T // tm,),
        in_specs=[pl.BlockSpec((tm, D), lambda i: (i, 0)),
                  pl.BlockSpec((1, D), lambda i: (0, 0)),
                  pl.BlockSpec((D, nb), lambda i: (0, 0)),
                  pl.BlockSpec((D, LANES), lambda i: (0, 0))],
        out_specs=[pl.BlockSpec((tm, nb), lambda i: (i, 0)),
                   pl.BlockSpec((tm, LANES), lambda i: (i, 0))],
        out_shape=[jax.ShapeDtypeStruct((T, nb), F32), jax.ShapeDtypeStruct((T, LANES), F32)],
        compiler_params=_params("parallel"), name="inproj")(x2d, g, wb, ws)


def _gates_body(sm_ref, p_ref, col_ref, row_ref, carry_ref, *, ts):
    @pl.when(pl.program_id(1) == 0)
    def _():
        carry_ref[...] = jnp.zeros_like(carry_ref)

    x = sm_ref[...]
    lane = lax.broadcasted_iota(I32, (ts, LANES), 1)
    is_fox = lane < LANE_DECAY
    is_dec = (lane >= LANE_DECAY) & (lane < LANE_BETA)
    is_beta = (lane >= LANE_BETA) & (lane < LANE_BETA + N_HEADS)
    logf = jax.nn.log_sigmoid(x + p_ref[0:1, :])
    g = -jnp.exp(p_ref[2:3, :]) * jax.nn.softplus(x + p_ref[1:2, :])
    beta = jax.nn.sigmoid(x)
    r = lax.broadcasted_iota(I32, (ts, ts), 0)
    c = lax.broadcasted_iota(I32, (ts, ts), 1)
    tri_full = (r >= c).astype(BF16)
    tri_chunk = ((r >= c) & (r // CHUNK == c // CHUNK)).astype(BF16)
    vals = jnp.where(is_fox, logf, jnp.where(is_dec, g, 0.0))
    hi = vals.astype(BF16)
    mid = (vals - hi.astype(F32)).astype(BF16)
    lo = (vals - hi.astype(F32) - mid.astype(F32)).astype(BF16)
    parts = jnp.concatenate([hi, mid, lo], axis=1)

    def tri_sum(tri):
        t = jnp.dot(tri, parts, preferred_element_type=F32)
        return t[:, :LANES] + (t[:, LANES:2 * LANES] + t[:, 2 * LANES:])

    cf = tri_sum(tri_full) + carry_ref[...]
    cg = tri_sum(tri_chunk)
    carry_ref[...] = cf[ts - 1:ts, :]
    out = jnp.where(is_fox, cf, jnp.where(is_dec, cg, jnp.where(is_beta, beta, 0.0)))
    col_ref[...] = out
    row_ref[...] = out.T[:2 * SUBLANES, :]


def _gates(small3, prm, ts):
    B, S, _ = small3.shape
    return pl.pallas_call(
        functools.partial(_gates_body, ts=ts), grid=(B, S // ts),
        in_specs=[pl.BlockSpec((None, ts, LANES), lambda b, j: (b, j, 0)),
                  pl.BlockSpec((SUBLANES, LANES), lambda b, j: (0, 0))],
        out_specs=[pl.BlockSpec((None, ts, LANES), lambda b, j: (b, j, 0)),
                   pl.BlockSpec((None, 2 * SUBLANES, ts), lambda b, j: (b, 0, j))],
        out_shape=[jax.ShapeDtypeStruct((B, S, LANES), F32),
                   jax.ShapeDtypeStruct((B, 2 * SUBLANES, S), F32)],
        scratch_shapes=[pltpu.VMEM((1, LANES), F32)],
        compiler_params=_params("parallel", "arbitrary"), name="gates")(small3, prm)


def _split3_lanes(x, lane, lane0):
    hi = x.astype(BF16).astype(F32)
    mid = (x - hi).astype(BF16).astype(F32)
    lo = (x - hi - mid).astype(BF16).astype(F32)
    return jnp.where(lane == lane0, hi, jnp.where(lane == lane0 + 1, mid, jnp.where(lane == lane0 + 2, lo, 0.0)))


def _fox_body(jstart_ref, q_ref, k_ref, v_ref, ccol_ref, qg_ref, kg_ref, og_ref, o_ref,
              kn_scr, vt_scr, q_scr, m_scr, l_scr, acc_scr, s_scr, kmax_scr, *, tq, nk):
    b = pl.program_id(0)
    i = pl.program_id(1)
    ones_bd = _head_ones()
    lane = lax.broadcasted_iota(I32, (tq, LANES), 1)
    log2e = 1.0 / np.log(2.0)
    c_lane, r_lane = HEAD_DIM, HEAD_DIM + 3

    def head_norm(x, g):
        ss = _head_sums(x * x, ones_bd)
        return x * lax.rsqrt(ss * (1.0 / HEAD_DIM) + EPS) * g

    def head_tile(x, h, extra):
        pair = x[:, (h // 2) * LANES:(h // 2 + 1) * LANES]
        if h % 2:
            pair = pltpu.roll(pair, HEAD_DIM, 1)
        return jnp.where(lane < HEAD_DIM, pair, extra).astype(BF16)

    def sq_norms(x):
        xr = x.astype(BF16).astype(F32)
        return _head_sums(xr * xr, ones_bd)

    @pl.when(i == 0)
    def _():
        k_ones = jnp.where((lane >= r_lane) & (lane < r_lane + 3), 1.0, 0.0)
        vt_tail = (lax.broadcasted_iota(I32, (FOX_VT_ROWS - HEAD_DIM, tq), 0) == 0).astype(BF16)
        kmax = jnp.zeros((1, GROUP_W), F32)
        for c in range(nk):
            rows = slice(c * tq, (c + 1) * tq)
            kc = head_norm(k_ref[rows, :], kg_ref[...])
            kmax = jnp.maximum(kmax, jnp.max(sq_norms(kc), axis=0, keepdims=True))
            vt = v_ref[rows, :].T.astype(BF16)
            cc = ccol_ref[rows, :] * (-log2e)
            for h in range(N_HEADS):
                extra = _split3_lanes(cc[:, LANE_FOX + h:LANE_FOX + h + 1], lane, c_lane) + k_ones
                kn_scr[h, c] = head_tile(kc, h, extra)
                vt_scr[h, c] = jnp.concatenate([vt[h * HEAD_DIM:(h + 1) * HEAD_DIM, :], vt_tail], axis=0)
        kmax_scr[...] = kmax

    qn = head_norm(q_ref[...], qg_ref[...]) * (HEAD_DIM ** -0.5 * log2e)
    bound = jnp.sqrt(sq_norms(qn) * kmax_scr[...]) * 1.001
    bounded = jnp.max(bound) <= FOX_BOUND_LOG2
    c_i = ccol_ref[pl.ds(pl.multiple_of(i * tq, tq), tq), :] * log2e
    q_ones = jnp.where((lane >= c_lane) & (lane < c_lane + 3), 1.0, 0.0)
    for h in range(N_HEADS):
        r_i = bound[:, h * HEAD_DIM:h * HEAD_DIM + 1] - c_i[:, LANE_FOX + h:LANE_FOX + h + 1]
        neg_r = jnp.where(bounded, -r_i, 0.0)
        q_scr[h] = head_tile(qn, h, q_ones + _split3_lanes(neg_r, lane, r_lane))
    acc_scr[...] = jnp.zeros_like(acc_scr)
    causal = (lax.broadcasted_iota(I32, (tq, tq), 0) <= lax.broadcasted_iota(I32, (tq, tq), 1))

    heads = range(N_HEADS)

    def scores(j):
        return [lax.dot_general(kn_scr[h, j], q_scr[h], (((1,), (1,)), ((), ())),
                                preferred_element_type=F32) for h in heads]

    def stash(s):
        for h in heads:
            s_scr[h] = s[h]

    def absorb_general(j, masked):
        p, alpha = [], []
        for h in heads:
            s = s_scr[h]
            if masked:
                s = jnp.where(causal, s, -jnp.inf)
            m_old = m_scr[h]
            m_new = jnp.maximum(m_old, jnp.max(s, axis=0, keepdims=True))
            alpha.append(jnp.exp2(m_old - m_new))
            ph = jnp.exp2(s - m_new)
            l_scr[h] = alpha[h] * l_scr[h] + jnp.sum(ph, axis=0, keepdims=True)
            m_scr[h] = m_new
            p.append(ph.astype(BF16))
        pv = [jnp.dot(vt_scr[h, j], p[h], preferred_element_type=F32) for h in heads]
        for h in heads:
            acc_scr[h] = alpha[h] * acc_scr[h] + pv[h]

    def absorb_bounded(j, masked):
        p = []
        for h in heads:
            s = s_scr[h]
            if masked:
                s = jnp.where(causal, s, -jnp.inf)
            p.append(jnp.exp2(s).astype(BF16))
        pv = [jnp.dot(vt_scr[h, j], p[h], preferred_element_type=F32) for h in heads]
        for h in heads:
            acc_scr[h] += pv[h]

    def run(absorb, j0):
        stash(scores(j0))

        def body(j, c):
            s_next = scores(j + 1)
            absorb(j, False)
            stash(s_next)
            return c

        lax.fori_loop(j0, i, body, 0)
        absorb(i, True)

    @pl.when(bounded)
    def _():
        run(absorb_bounded, jstart_ref[b * nk + i])

    @pl.when(jnp.logical_not(bounded))
    def _():
        m_scr[...] = jnp.full_like(m_scr, -jnp.inf)
        l_scr[...] = jnp.zeros_like(l_scr)
        run(absorb_general, 0)
        for h in heads:
            acc_scr[h, HEAD_DIM:HEAD_DIM + 1, :] = l_scr[h]

    o_t = jnp.concatenate([acc_scr[h, :HEAD_DIM, :] / acc_scr[h, HEAD_DIM:HEAD_DIM + 1, :] for h in heads],
                          axis=0)
    o_ref[...] = _rms(o_t.T, og_ref[...])


def _fox_first_block(gcol, tq):
    B, S, _ = gcol.shape
    nk = S // tq
    c2 = gcol[:, :, LANE_FOX:LANE_FOX + N_HEADS] * (1.0 / np.log(2.0))
    first = c2[:, 0::tq, :]
    last = c2[:, tq - 1::tq, :]
    dead = (first[:, :, None, :] - last[:, None, :, :]) < -FOX_SKIP_LOG2
    dead = dead & (jnp.arange(nk)[None, :, None, None] > jnp.arange(nk)[None, None, :, None])
    return jnp.min(jnp.sum(dead, axis=2), axis=-1).astype(I32).reshape(B * nk)


def _fox(big3, gcol, qg, kg, og, tq):
    B, S, _ = big3.shape
    nk = S // tq
    row = pl.BlockSpec((1, GROUP_W), lambda b, i, js: (0, 0))
    grid_spec = pltpu.PrefetchScalarGridSpec(
        num_scalar_prefetch=1, grid=(B, nk),
        in_specs=[pl.BlockSpec((None, tq, GROUP_W), lambda b, i, js: (b, i, COL_FQ)),
                  pl.BlockSpec((None, S, GROUP_W), lambda b, i, js: (b, 0, COL_FK)),
                  pl.BlockSpec((None, S, GROUP_W), lambda b, i, js: (b, 0, COL_FV)),
                  pl.BlockSpec((None, S, LANES), lambda b, i, js: (b, 0, 0)),
                  row, row, row],
        out_specs=pl.BlockSpec((None, tq, GROUP_W), lambda b, i, js: (b, i, 0)),
        scratch_shapes=[pltpu.VMEM((N_HEADS, nk, tq, LANES), BF16),
                        pltpu.VMEM((N_HEADS, nk, FOX_VT_ROWS, tq), BF16),
                        pltpu.VMEM((N_HEADS, tq, LANES), BF16),
                        pltpu.VMEM((N_HEADS, 1, tq), F32),
                        pltpu.VMEM((N_HEADS, 1, tq), F32),
                        pltpu.VMEM((N_HEADS, FOX_VT_ROWS, tq), F32),
                        pltpu.VMEM((N_HEADS, tq, tq), F32),
                        pltpu.VMEM((1, GROUP_W), F32)])
    return pl.pallas_call(
        functools.partial(_fox_body, tq=tq, nk=nk), grid_spec=grid_spec,
        out_shape=jax.ShapeDtypeStruct((B, S, GROUP_W), F32),
        compiler_params=_params("parallel", "arbitrary"), name="fox")(
            _fox_first_block(gcol, tq), big3, big3, big3, gcol, qg, kg, og)


def _gelu(x):
    return 0.5 * x * (1.0 + lax.erf(x * (2.0 ** -0.5)))


def _gmlp_body(u_ref, v_ref, lg_ref, lb_ref, ws_ref, bst_ref, og_ref, o_ref, *, nwin):
    L = GMLP_LEN
    r = lax.broadcasted_iota(I32, (L, L), 0) // CHUNK
    c = lax.broadcasted_iota(I32, (L, L), 1) // CHUNK
    mask = r >= c
    ws = [jnp.where(mask, ws_ref[h], 0.0).astype(BF16) for h in range(N_HEADS)]
    for n in range(nwin):
        u = _gelu(u_ref[n * L:(n + 1) * L, :])
        v = _gelu(v_ref[n * L:(n + 1) * L, :])
        mu = jnp.mean(v, axis=-1, keepdims=True)
        vc = v - mu
        var = jnp.mean(vc * vc, axis=-1, keepdims=True)
        vn = (vc * lax.rsqrt(var + EPS) * lg_ref[...] + lb_ref[...]).astype(BF16)
        mixed = jnp.concatenate(
            [jnp.dot(ws[h], vn[:, h * HEAD_DIM:(h + 1) * HEAD_DIM], preferred_element_type=F32)
             for h in range(N_HEADS)], axis=-1) + bst_ref[...]
        o_ref[n * L:(n + 1) * L, :] = _rms(u * mixed, og_ref[...])


def _gmlp(big, lg, lb, ws, bst, og, tm):
    T = big.shape[0]
    row = pl.BlockSpec((1, GROUP_W), lambda i: (0, 0))
    return pl.pallas_call(
        functools.partial(_gmlp_body, nwin=tm // GMLP_LEN), grid=(T // tm,),
        in_specs=[pl.BlockSpec((tm, GROUP_W), lambda i: (i, COL_GU)),
                  pl.BlockSpec((tm, GROUP_W), lambda i: (i, COL_GV)),
                  row, row,
                  pl.BlockSpec((N_HEADS, GMLP_LEN, GMLP_LEN), lambda i: (0, 0, 0)),
                  pl.BlockSpec((GMLP_LEN, GROUP_W), lambda i: (0, 0)),
                  row],
        out_specs=pl.BlockSpec((tm, GROUP_W), lambda i: (i, 0)),
        out_shape=jax.ShapeDtypeStruct((T, GROUP_W), F32),
        compiler_params=_params("parallel"), name="gmlp")(big, big, lg, lb, ws, bst, og)


def _gdn_prep_body(q_ref, k_ref, v_ref, hq_ref, hk_ref, hv_ref, w_ref, qo_ref, ko_ref, vo_ref, *, ts):
    first = pl.program_id(1) == 0
    ones_bd = _head_ones()

    def conv(x_ref, halo_ref, w):
        halo = jnp.where(first, 0.0, halo_ref[...])
        xx = jnp.concatenate([halo, x_ref[...]], axis=0)
        y = w[CONV_K - 1:CONV_K, :] * xx[SUBLANES:, :]
        for j in range(CONV_K - 1):
            y = y + w[j:j + 1, :] * pltpu.roll(xx, CONV_K - 1 - j, 0)[SUBLANES:, :]
        return y * jax.nn.sigmoid(y)

    def l2(t):
        return t * lax.rsqrt(_head_sums(t * t, ones_bd) + EPS)

    qo_ref[...] = l2(conv(q_ref, hq_ref, w_ref[0])) * (HEAD_DIM ** -0.5)
    ko_ref[...] = l2(conv(k_ref, hk_ref, w_ref[1]))
    vo_ref[...] = conv(v_ref, hv_ref, w_ref[2])


def _gdn_prep(big3, conv_w3, ts):
    B, S, _ = big3.shape
    hb = ts // SUBLANES
    blk = lambda col: pl.BlockSpec((None, ts, GROUP_W), lambda b, i: (b, i, col))
    halo = lambda col: pl.BlockSpec((None, SUBLANES, GROUP_W),
                                    lambda b, i: (b, jnp.maximum(i * hb - 1, 0), col))
    out = pl.BlockSpec((None, ts, GROUP_W), lambda b, i: (b, i, 0))
    shp = jax.ShapeDtypeStruct((B, S, GROUP_W), F32)
    return pl.pallas_call(
        functools.partial(_gdn_prep_body, ts=ts), grid=(B, S // ts),
        in_specs=[blk(COL_DQ), blk(COL_DK), blk(COL_DV), halo(COL_DQ), halo(COL_DK), halo(COL_DV),
                  pl.BlockSpec((3, CONV_K, GROUP_W), lambda b, i: (0, 0, 0))],
        out_specs=[out, out, out], out_shape=[shp, shp, shp],
        compiler_params=_params("parallel", "parallel"), name="gdn_prep")(
            big3, big3, big3, big3, big3, big3, conv_w3)


def _gdn_body(q_ref, k_ref, v_ref, gate_ref, gcol_ref, grow_ref, ng_ref, o_ref,
              s_scr, u_scr, wq_scr, a_scr, kd_scr, dl_scr, t_scr, p_scr, rhs_scr, *, nchunk):
    C = CHUNK

    @pl.when(pl.program_id(1) == 0)
    def _():
        s_scr[...] = jnp.zeros_like(s_scr)

    r = lax.broadcasted_iota(I32, (C, C), 0)
    c = lax.broadcasted_iota(I32, (C, C), 1)
    tri = r >= c
    strict = r > c
    eye = (r == c).astype(F32)

    items = [(n, h) for n in range(nchunk) for h in range(N_HEADS)]
    mmb = functools.partial(jnp.dot, preferred_element_type=F32)
    for it, (n, h) in enumerate(items):
        rows = slice(n * C, (n + 1) * C)
        sl = slice(h * HEAD_DIM, (h + 1) * HEAD_DIM)
        q, k, v = q_ref[rows, sl], k_ref[rows, sl], v_ref[rows, sl]
        gc = gcol_ref[rows, LANE_DECAY + h:LANE_DECAY + h + 1]
        gr = grow_ref[n, LANE_DECAY + h:LANE_DECAY + h + 1, :]
        beta = gcol_ref[rows, LANE_BETA + h:LANE_BETA + h + 1]
        decay = jnp.exp(jnp.where(tri, gc - gr, -jnp.inf))
        kb = k * beta
        kk = _mm_nt(jnp.concatenate([kb, q], axis=0), k)
        x = jnp.where(strict, -(kk[:C] * decay), 0.0)
        t_scr[it] = eye + x
        p_scr[it] = x.astype(BF16)
        eg = jnp.exp(gc)
        g_last = gc[C - 1:C, :]
        rhs_scr[it] = jnp.concatenate([v * beta, kb * eg], axis=1).astype(BF16)
        wq_scr[n, h, C:, :] = (q * eg).astype(BF16)
        a_scr[n, h] = jnp.where(tri, kk[C:] * decay, 0.0).astype(BF16)
        kd_scr[n, h] = (k * jnp.exp(g_last - gc)).astype(BF16)
        dl_scr[n, h] = jnp.broadcast_to(jnp.exp(g_last), (1, HEAD_DIM))
    for _ in range(5):
        for it in range(len(items)):
            p = p_scr[it]
            p_scr[it] = mmb(p, p).astype(BF16)
        for it in range(len(items)):
            t = t_scr[it]
            t_scr[it] = t + mmb(t.astype(BF16), p_scr[it])
    for it, (n, h) in enumerate(items):
        uw = mmb(t_scr[it].astype(BF16), rhs_scr[it])
        u_scr[n, h] = uw[:, :HEAD_DIM]
        wq_scr[n, h, :C, :] = uw[:, HEAD_DIM:].astype(BF16)

    heads = range(N_HEADS)
    state = [s_scr[h] for h in heads]
    for n in range(nchunk):
        rows = slice(n * C, (n + 1) * C)
        ws = [mmb(wq_scr[n, h], state[h].astype(BF16)) for h in heads]
        vb = [(u_scr[n, h] - ws[h][:C]).astype(BF16) for h in heads]
        o = [ws[h][C:] + mmb(a_scr[n, h], vb[h]) for h in heads]
        state = [state[h] * dl_scr[n, h] + lax.dot_general(
            kd_scr[n, h], vb[h], (((0,), (0,)), ((), ())), preferred_element_type=F32) for h in heads]
        gate = gate_ref[rows, :]
        y = jnp.concatenate([_rms(o[h], ng_ref[...]) for h in heads], axis=-1)
        o_ref[rows, :] = y * (gate * jax.nn.sigmoid(gate))
    for h in heads:
        s_scr[h] = state[h]


def _gdn(qn, kn, vv, big3, gcol, grow4, ng, ts):
    B, S, _ = qn.shape
    nchunk = ts // CHUNK
    blk = pl.BlockSpec((None, ts, GROUP_W), lambda b, i: (b, i, 0))
    return pl.pallas_call(
        functools.partial(_gdn_body, nchunk=nchunk), grid=(B, S // ts),
        in_specs=[blk, blk, blk,
                  pl.BlockSpec((None, ts, GROUP_W), lambda b, i: (b, i, COL_DG)),
                  pl.BlockSpec((None, ts, LANES), lambda b, i: (b, i, 0)),
                  pl.BlockSpec((None, nchunk, 2 * SUBLANES, CHUNK), lambda b, i: (b, i, 0, 0)),
                  pl.BlockSpec((1, HEAD_DIM), lambda b, i: (0, 0))],
        out_specs=blk, out_shape=jax.ShapeDtypeStruct((B, S, GROUP_W), F32),
        scratch_shapes=[pltpu.VMEM((N_HEADS, HEAD_DIM, HEAD_DIM), F32),
                        pltpu.VMEM((nchunk, N_HEADS, CHUNK, HEAD_DIM), F32),
                        pltpu.VMEM((nchunk, N_HEADS, 2 * CHUNK, HEAD_DIM), BF16),
                        pltpu.VMEM((nchunk, N_HEADS, CHUNK, CHUNK), BF16),
                        pltpu.VMEM((nchunk, N_HEADS, CHUNK, HEAD_DIM), BF16),
                        pltpu.VMEM((nchunk, N_HEADS, 1, HEAD_DIM), F32),
                        pltpu.VMEM((nchunk * N_HEADS, CHUNK, CHUNK), F32),
                        pltpu.VMEM((nchunk * N_HEADS, CHUNK, CHUNK), BF16),
                        pltpu.VMEM((nchunk * N_HEADS, CHUNK, 2 * HEAD_DIM), BF16)],
        compiler_params=_params("parallel", "arbitrary"), name="gdn")(
            qn, kn, vv, big3, gcol, grow4, ng)


def _pool_body(z_ref, halo_ref, w_ref, sc_ref, og_ref, o_ref, *, ts):
    i = pl.program_id(1)
    hr = 2 * SUBLANES
    z = z_ref[...]
    halo = jnp.where(i == 0, 0.0, halo_ref[...])
    s1 = jnp.concatenate([halo, z], axis=0)
    s2 = s1 + pltpu.roll(s1, 1, 0)
    s4 = s2 + pltpu.roll(s2, 2, 0)
    s8 = s4 + pltpu.roll(s4, 4, 0)
    s16 = s8 + pltpu.roll(s8, 8, 0)
    grp = lax.broadcasted_iota(I32, (ts, GROUP_W), 1) // (GROUP_W // len(POOL_WINDOWS))
    t = lax.broadcasted_iota(I32, (ts, GROUP_W), 0) + i * ts
    total = jnp.where(grp == 0, s2[hr:], jnp.where(grp == 1, s4[hr:], jnp.where(grp == 2, s8[hr:], s16[hr:])))
    win = jnp.where(grp == 0, POOL_WINDOWS[0], jnp.where(grp == 1, POOL_WINDOWS[1],
                    jnp.where(grp == 2, POOL_WINDOWS[2], POOL_WINDOWS[3])))
    pooled = total / jnp.minimum(t + 1, win).astype(F32)
    y = _mm(pooled - z, w_ref[...]) * sc_ref[...]
    o_ref[...] = _rms(y, og_ref[...])


def _pool(big3, wbd, sc, og, ts):
    B, S, _ = big3.shape
    hr = 2 * SUBLANES
    hb = ts // hr
    row = pl.BlockSpec((1, GROUP_W), lambda b, i: (0, 0))
    return pl.pallas_call(
        functools.partial(_pool_body, ts=ts), grid=(B, S // ts),
        in_specs=[pl.BlockSpec((None, ts, GROUP_W), lambda b, i: (b, i, COL_PZ)),
                  pl.BlockSpec((None, hr, GROUP_W), lambda b, i: (b, jnp.maximum(i * hb - 1, 0), COL_PZ)),
                  pl.BlockSpec((GROUP_W, GROUP_W), lambda b, i: (0, 0)), row, row],
        out_specs=pl.BlockSpec((None, ts, GROUP_W), lambda b, i: (b, i, 0)),
        out_shape=jax.ShapeDtypeStruct((B, S, GROUP_W), F32),
        compiler_params=_params("parallel", "parallel"), name="pool")(big3, big3, wbd, sc, og)


def _outproj_body(ya_ref, yb_ref, yc_ref, yd_ref, h_ref, wo_ref, g_ref, rw_ref, rb_ref,
                  hnew_ref, hn_ref, ri_ref, rf_ref, *, tm, d):
    y = jnp.concatenate([ya_ref[...], yb_ref[...], yc_ref[...], yd_ref[...]], axis=-1).astype(BF16)
    h_new = h_ref[...] + jnp.dot(y, wo_ref[...], preferred_element_type=F32)
    hnew_ref[...] = h_new
    hn = _rms(h_new, g_ref[...])
    hn_hi, hn_lo = _split(hn)
    hn_ref[...] = hn_hi
    t = jnp.dot(hn_hi, rw_ref[...], preferred_element_type=F32)
    logits = (t[:, :LANES] + t[:, LANES:]
              + jnp.dot(hn_lo, rw_ref[:, :LANES], preferred_element_type=F32)) + rb_ref[...]
    lane = lax.broadcasted_iota(I32, (tm, LANES), 1)
    neg = -jnp.inf
    big_lane = LANES

    def masked_top(vals, mask):
        v = jnp.where(mask, vals, neg)
        mx = jnp.max(v, axis=-1, keepdims=True)
        idx = jnp.min(jnp.where(mask & (v == mx), lane, big_lane), axis=-1, keepdims=True)
        return v, mx, idx

    gmask = lane < N_GROUPS
    gv, gmx, gidx = masked_top(logits, gmask)
    g_top = 1.0 / jnp.sum(jnp.where(gmask, jnp.exp(gv - gmx), 0.0), axis=-1, keepdims=True)
    lo = N_GROUPS + gidx * EXPERTS_PER_GROUP
    emask = (lane >= lo) & (lane < lo + EXPERTS_PER_GROUP)
    ev, emx, eidx1 = masked_top(logits, emask)
    esum = jnp.sum(jnp.where(emask, jnp.exp(ev - emx), 0.0), axis=-1, keepdims=True)
    p1 = 1.0 / esum
    _, emx2, eidx2 = masked_top(logits, emask & (lane != eidx1))
    p2 = jnp.exp(emx2 - emx) / esum
    denom = p1 + p2
    ri_ref[...] = jnp.where(lane == 0, eidx1 - N_GROUPS, jnp.where(lane == 1, eidx2 - N_GROUPS, 0))
    rf_ref[...] = jnp.where(lane == 0, g_top * p1 / denom, jnp.where(lane == 1, g_top * p2 / denom, 0.0))


def _outproj(ya, yb, yc, yd, h2d, wo, g, rw, rb, tm):
    T, D = h2d.shape
    nseg = D // LANES
    yblk = pl.BlockSpec((tm, GROUP_W), lambda i: (i, 0))
    return pl.pallas_call(
        functools.partial(_outproj_body, tm=tm, d=D), grid=(T // tm,),
        in_specs=[yblk, yblk, yblk, yblk,
                  pl.BlockSpec((tm, D), lambda i: (i, 0)),
                  pl.BlockSpec((D, D), lambda i: (0, 0)),
                  pl.BlockSpec((1, D), lambda i: (0, 0)),
                  pl.BlockSpec((D, 2 * LANES), lambda i: (0, 0)),
                  pl.BlockSpec((1, LANES), lambda i: (0, 0))],
        out_specs=[pl.BlockSpec((tm, D), lambda i: (i, 0)),
                   pl.BlockSpec((tm, D), lambda i: (i, 0)),
                   pl.BlockSpec((tm, LANES), lambda i: (i, 0)),
                   pl.BlockSpec((tm, LANES), lambda i: (i, 0))],
        out_shape=[jax.ShapeDtypeStruct((T, D), F32),
                   jax.ShapeDtypeStruct((T, D), BF16),
                   jax.ShapeDtypeStruct((T, LANES), I32),
                   jax.ShapeDtypeStruct((T, LANES), F32)],
        compiler_params=_params("parallel"), name="outproj")(ya, yb, yc, yd, h2d, wo, g, rw, rb)


def _rank_body(ri_ref, rank_ref, cnt_ref, *, tm):
    lane = lax.broadcasted_iota(I32, (tm, LANES), 1)
    e = ri_ref[...]
    oh = [lane == e[:, s:s + 1] for s in range(TOPK_IN)]
    m = (oh[0].astype(F32) + oh[1].astype(F32)).astype(BF16)
    below = (lax.broadcasted_iota(I32, (tm, tm), 0) > lax.broadcasted_iota(I32, (tm, tm), 1)).astype(BF16)
    before = jnp.dot(below, m, preferred_element_type=F32)
    rank = [jnp.sum(jnp.where(oh[s], before, 0.0), axis=-1, keepdims=True) for s in range(TOPK_IN)]
    rank_ref[...] = jnp.where(lane == 0, rank[0], jnp.where(lane == 1, rank[1], 0.0)).astype(I32)
    total = before[tm - 1:tm, :] + m[tm - 1:tm, :].astype(F32)
    cnt_ref[...] = jnp.broadcast_to(total, cnt_ref.shape).astype(I32)


def _rank(ri, tm):
    T = ri.shape[0]
    return pl.pallas_call(
        functools.partial(_rank_body, tm=tm), grid=(T // tm,),
        in_specs=[pl.BlockSpec((tm, LANES), lambda i: (i, 0))],
        out_specs=[pl.BlockSpec((tm, LANES), lambda i: (i, 0)),
                   pl.BlockSpec((SUBLANES, LANES), lambda i: (i, 0))],
        out_shape=[jax.ShapeDtypeStruct((T, LANES), I32),
                   jax.ShapeDtypeStruct((T // tm * SUBLANES, LANES), I32)],
        compiler_params=_params("parallel"), name="rank")(ri)


def _dispatch_plan(expert, lrank, cnt_tile, tm):
    T = expert.shape[0]
    nt = T // tm
    counts = jnp.sum(cnt_tile, axis=0)
    padded = (counts + (RUN_ROWS - 1) + MOE_BLOCK - 1) // MOE_BLOCK * MOE_BLOCK
    pad_end = jnp.cumsum(padded)
    pad_start = pad_end - padded
    gstart = pad_start[None, :] + jnp.cumsum(cnt_tile, axis=0) - cnt_tile
    nchunk = (cnt_tile + RUN_ROWS - 1) // RUN_ROWS
    chunk_end = jnp.cumsum(nchunk, axis=1)
    lstart = (chunk_end - nchunk) * RUN_ROWS
    onehot = expert[:, :, None] == jnp.arange(N_EXPERTS, dtype=I32)[None, None, :]
    pick = lambda tab: jnp.sum(jnp.where(onehot, jnp.repeat(tab, tm, axis=0)[:, None, :], 0), axis=-1)
    lpos =(lrank + pick(lstart)).astype(I32)
    max_chunks = tm * TOPK_IN // RUN_ROWS + N_EXPERTS
    c = jnp.arange(max_chunks, dtype=I32)
    ce = jnp.minimum(jnp.sum(chunk_end[:, None, :] <= c[None, :, None], axis=-1), N_EXPERTS - 1)
    ce_hot = ce[:, :, None] == jnp.arange(N_EXPERTS, dtype=I32)[None, None, :]
    take = lambda tab: jnp.sum(jnp.where(ce_hot, tab[:, None, :], 0), axis=-1)
    chunk_row = take(gstart) + (c[None, :] - take(chunk_end - nchunk)) * RUN_ROWS
    chunk_row = jnp.where(c[None, :] < chunk_end[:, -1:], chunk_row, 0)
    table = jnp.zeros((nt, 1, LANES), I32).at[:, 0, :max_chunks].set(chunk_row.astype(I32))
    table = table.at[:, 0, LANES - 1].set(chunk_end[:, -1].astype(I32))
    n_blk = -(-(T * TOPK_IN + N_EXPERTS * (RUN_ROWS - 1)) // MOE_BLOCK) + N_EXPERTS
    blk_start = jnp.arange(n_blk, dtype=I32) * MOE_BLOCK
    blk_e = jnp.minimum(jnp.sum(pad_end[None, :] <= blk_start[:, None], axis=-1), N_EXPERTS - 1).astype(I32)
    n_used = (pad_end[-1] // MOE_BLOCK).astype(I32).reshape(1)
    return lpos, table, blk_e, n_used, n_blk


def _dispatch_body(tab_ref, lpos_ref, hn_ref, xz_hbm, x_hbm, xs, sem, n_prev, *, tm, nseg, nrow):
    del xz_hbm
    i = pl.program_id(0)
    slot = i % 2
    row = lax.broadcasted_iota(I32, (nrow, tm), 0)
    sel = (row == lpos_ref[0:1, :]) | (row == lpos_ref[1:2, :])
    rows = jnp.dot(sel.astype(BF16), hn_ref[...], preferred_element_type=F32)
    for k in range(nseg):
        xs[slot, pl.ds(k, nrow, stride=nseg), :] = rows[:, k * LANES:(k + 1) * LANES]

    step = RUN_ROWS * nseg

    def chunk_copy(sl, c):
        return pltpu.make_async_copy(
            xs.at[sl, pl.ds(pl.multiple_of(c * step, step), step), :],
            x_hbm.at[pl.ds(pl.multiple_of(tab_ref[0, 0, c] * nseg, nseg), step), :], sem.at[sl])

    def wait_chunks(sl, n):
        def wait(c, carry):
            chunk_copy(sl, 0).wait()
            return carry
        lax.fori_loop(0, n, wait, 0)

    n_chunks = tab_ref[0, 0, LANES - 1]

    @pl.when(i > 0)
    def _():
        wait_chunks(1 - slot, n_prev[0])

    def start(c, carry):
        chunk_copy(slot, c).start()
        return carry

    lax.fori_loop(0, n_chunks, start, 0)
    n_prev[0] = n_chunks

    @pl.when(i == pl.num_programs(0) - 1)
    def _():
        wait_chunks(slot, n_chunks)


def _dispatch(table, lpos_t, hn, n_pad, tm):
    T, D = hn.shape
    nseg = D // LANES
    nrow = tm * TOPK_IN + N_EXPERTS * RUN_ROWS
    x_zero = jnp.zeros((n_pad * nseg, LANES), F32)
    return pl.pallas_call(
        functools.partial(_dispatch_body, tm=tm, nseg=nseg, nrow=nrow), grid=(T // tm,),
        in_specs=[pl.BlockSpec((1, 1, LANES), lambda i: (i, 0, 0), memory_space=pltpu.SMEM),
                  pl.BlockSpec((None, SUBLANES, tm), lambda i: (i, 0, 0)),
                  pl.BlockSpec((tm, D), lambda i: (i, 0)),
                  pl.BlockSpec(memory_space=pl.ANY)],
        out_specs=pl.BlockSpec(memory_space=pl.ANY),
        out_shape=jax.ShapeDtypeStruct((n_pad * nseg, LANES), F32),
        scratch_shapes=[pltpu.VMEM((2, nrow * nseg, LANES), F32), pltpu.SemaphoreType.DMA((2,)),
                        pltpu.SMEM((1,), I32)],
        input_output_aliases={3: 0},
        compiler_params=_params("arbitrary"), name="dispatch")(table, lpos_t, hn, x_zero)


def _moe_body(blk_e_ref, n_used_ref, x_ref, w1_ref, w3_ref, w2_ref, y_ref, w1b, w3b, w2b, *, nseg):
    b = pl.program_id(0)
    R = MOE_BLOCK

    @pl.when(b < n_used_ref[0])
    def _():
        @pl.when((b == 0) | (blk_e_ref[b] != blk_e_ref[jnp.maximum(b - 1, 0)]))
        def _():
            w1b[...] = w1_ref[...].astype(BF16)
            w3b[...] = w3_ref[...].astype(BF16)
            w2b[...] = w2_ref[...].astype(BF16)

        x = jnp.concatenate([x_ref[pl.ds(s, R, stride=nseg), :] for s in range(nseg)], axis=-1).astype(BF16)
        a = jnp.dot(x, w1b[...], preferred_element_type=F32)
        g = jnp.dot(x, w3b[...], preferred_element_type=F32)
        hid = (a * jax.nn.sigmoid(a) * g).astype(BF16)
        y = jnp.dot(hid, w2b[...], preferred_element_type=F32)
        for s in range(nseg):
            y_ref[pl.ds(s, R, stride=nseg), :] = y[:, s * LANES:(s + 1) * LANES]

    @pl.when(b >= n_used_ref[0])
    def _():
        y_ref[...] = jnp.zeros_like(y_ref)


def _moe(blk_e, n_used, x_rows, w1, w3, w2, layer):
    n_blk = blk_e.shape[0]
    _, _, D, DE = w1.shape
    nseg = D // LANES
    R = MOE_BLOCK
    rows = lambda b, be, nu: (jnp.minimum(b, nu[0] - 1), 0)
    wmap = lambda b, be, nu: (layer, be[b], 0, 0)
    grid_spec = pltpu.PrefetchScalarGridSpec(
        num_scalar_prefetch=2, grid=(n_blk,),
        in_specs=[pl.BlockSpec((R * nseg, LANES), rows),
                  pl.BlockSpec((None, None, D, DE), wmap),
                  pl.BlockSpec((None, None, D, DE), wmap),
                  pl.BlockSpec((None, None, DE, D), wmap)],
        out_specs=pl.BlockSpec((R * nseg, LANES), lambda b, be, nu: (b, 0)),
        scratch_shapes=[pltpu.VMEM((D, DE), BF16), pltpu.VMEM((D, DE), BF16), pltpu.VMEM((DE, D), BF16)])
    return pl.pallas_call(
        functools.partial(_moe_body, nseg=nseg), grid_spec=grid_spec,
        out_shape=jax.ShapeDtypeStruct(x_rows.shape, F32),
        compiler_params=_params("arbitrary"), name="moe")(blk_e, n_used, x_rows, w1, w3, w2)


def _combine_body(tab_ref, tabn_ref, h_ref, rf_ref, lpos_ref, y_hbm, o_ref, ybuf, sem, *, tm, nseg, nrow):
    i = pl.program_id(0)
    slot = i % 2
    step = RUN_ROWS * nseg

    def start_chunks(tab, sl):
        def start(c, carry):
            pltpu.make_async_copy(
                y_hbm.at[pl.ds(pl.multiple_of(tab[0, 0, c] * nseg, nseg), step), :],
                ybuf.at[sl, pl.ds(pl.multiple_of(c * step, step), step), :], sem.at[sl]).start()
            return carry
        lax.fori_loop(0, tab[0, 0, LANES - 1], start, 0)

    @pl.when(i == 0)
    def _():
        ybuf[...] = jnp.zeros_like(ybuf)
        start_chunks(tab_ref, 0)

    @pl.when(i + 1 < pl.num_programs(0))
    def _():
        start_chunks(tabn_ref, 1 - slot)

    col = lax.broadcasted_iota(I32, (tm, nrow), 1)
    sel = jnp.zeros((tm, nrow), F32)
    for s in range(TOPK_IN):
        sel = sel + jnp.where(col == lpos_ref[:, s:s + 1], rf_ref[:, s:s + 1], 0.0)

    def wait(c, carry):
        pltpu.make_async_copy(y_hbm.at[pl.ds(0, step), :], ybuf.at[slot, pl.ds(0, step), :],
                              sem.at[slot]).wait()
        return carry

    lax.fori_loop(0, tab_ref[0, 0, LANES - 1], wait, 0)
    y = jnp.concatenate([ybuf[slot, pl.ds(k, nrow, stride=nseg), :] for k in range(nseg)], axis=-1)
    o_ref[...] = h_ref[...] + jnp.dot(sel.astype(BF16), y.astype(BF16), preferred_element_type=F32)


def _combine(table, h2d, rf, lpos, y_rows, tm):
    T, D = h2d.shape
    nseg = D // LANES
    nt = T // tm
    nrow = tm * TOPK_IN + N_EXPERTS * RUN_ROWS
    return pl.pallas_call(
        functools.partial(_combine_body, tm=tm, nseg=nseg, nrow=nrow), grid=(nt,),
        in_specs=[pl.BlockSpec((1, 1, LANES), lambda i: (i, 0, 0), memory_space=pltpu.SMEM),
                  pl.BlockSpec((1, 1, LANES), lambda i: (jnp.minimum(i + 1, nt - 1), 0, 0),
                               memory_space=pltpu.SMEM),
                  pl.BlockSpec((tm, D), lambda i: (i, 0)),
                  pl.BlockSpec((tm, LANES), lambda i: (i, 0)),
                  pl.BlockSpec((tm, LANES), lambda i: (i, 0)),
                  pl.BlockSpec(memory_space=pl.ANY)],
        out_specs=pl.BlockSpec((tm, D), lambda i: (i, 0)),
        out_shape=jax.ShapeDtypeStruct((T, D), F32),
        scratch_shapes=[pltpu.VMEM((2, nrow * nseg, LANES), F32), pltpu.SemaphoreType.DMA((2,))],
        compiler_params=_params("arbitrary"), name="combine")(table, table, h2d, rf, lpos, y_rows)


def _pad_lanes(a, lane0, rows=1):
    out = jnp.zeros((rows, LANES), a.dtype)
    return out.at[:, lane0:lane0 + a.shape[-1]].set(a.reshape(rows, -1))


def _layer(h2d, B, S, p):
    T, D = h2d.shape
    tile = lambda a, n: jnp.tile(a.reshape(1, -1), (1, n))

    offs = np.cumsum([0, GROUP_W, GROUP_W, GROUP_W, N_HEADS, GROUP_W, GROUP_W,
                      GROUP_W, GROUP_W, GROUP_W, N_HEADS, N_HEADS, GROUP_W, GROUP_W])
    seg = lambda k: p['w_in'][:, offs[k]:offs[k + 1]]
    wb = jnp.concatenate([seg(0), seg(1), seg(2), seg(4), seg(5), seg(6), seg(7), seg(8), seg(11), seg(12)],
                         axis=1).astype(BF16)
    ws = jnp.zeros((D, LANES), F32)
    ws = ws.at[:, LANE_FOX:LANE_FOX + N_HEADS].set(seg(3))
    ws = ws.at[:, LANE_DECAY:LANE_DECAY + N_HEADS].set(seg(9))
    ws = ws.at[:, LANE_BETA:LANE_BETA + N_HEADS].set(seg(10)).astype(BF16)
    gate_prm = jnp.concatenate([_pad_lanes(p['fox_f_bias'], LANE_FOX), _pad_lanes(p['gdn_dt_bias'], LANE_DECAY),
                                _pad_lanes(p['gdn_a_log'], LANE_DECAY), jnp.zeros((SUBLANES - 3, LANES), F32)], axis=0)

    big, small = _inproj(h2d, p['attn_norm_g'].reshape(1, D), wb, ws, tm=min(512, T))
    big3 = big.reshape(B, S, N_BIG_COLS * GROUP_W)
    ts = min(512, S)
    gcol, grow = _gates(small.reshape(B, S, LANES), gate_prm, ts)

    tq = min(256, S)
    ya = _fox(big3, gcol, tile(p['fox_qn_g'], N_HEADS), tile(p['fox_kn_g'], N_HEADS),
              p['fox_out_g'].reshape(1, GROUP_W), tq)

    bst = jnp.repeat(p['gmlp_bs'].T, HEAD_DIM, axis=1)
    yb = _gmlp(big, p['gmlp_ln_g'].reshape(1, -1), p['gmlp_ln_b'].reshape(1, -1), p['gmlp_ws'], bst,
               p['gmlp_out_g'].reshape(1, -1), tm=min(512, T))

    conv_w3 = p['gdn_conv_w'].reshape(CONV_K, 3, GROUP_W).transpose(1, 0, 2)
    qn, kn, vv = _gdn_prep(big3, conv_w3, ts)
    grow4 = grow.reshape(B, 2 * SUBLANES, S // CHUNK, CHUNK).transpose(0, 2, 1, 3)
    yc = _gdn(qn, kn, vv, big3, gcol, grow4, p['gdn_norm_g'].reshape(1, HEAD_DIM), ts)

    wbd = jax.scipy.linalg.block_diag(*[p['pool_w'][g] for g in range(len(POOL_WINDOWS))]).astype(BF16)
    yd = _pool(big3, wbd, p['pool_scale'].reshape(1, -1), p['pool_out_g'].reshape(1, -1), ts)

    rw = jnp.zeros((D, LANES), F32).at[:, :N_GROUPS].set(p['router_g_w'])
    rw = rw.at[:, N_GROUPS:N_GROUPS + N_EXPERTS].set(p['router_e_w'])
    rw = jnp.concatenate(_split(rw), axis=1)
    rb = jnp.zeros((1, LANES), F32).at[0, :N_GROUPS].set(p['router_g_b'])
    rb = rb.at[0, N_GROUPS:N_GROUPS + N_EXPERTS].set(p['router_e_b'])
    flat = lambda a: a.reshape(T, GROUP_W)
    h_new, hn_rows, ri, rf = _outproj(flat(ya), yb, flat(yc), flat(yd), h2d, p['w_out'].astype(BF16),
                                      p['ffn_norm_g'].reshape(1, D), rw, rb, tm=min(512, T))

    tmd = min(256, T)
    lrank, cnt = _rank(ri, tmd)
    cnt_tile = cnt.reshape(T // tmd, SUBLANES, LANES)[:, 0, :N_EXPERTS]
    lpos, table, blk_e, n_used, n_blk = _dispatch_plan(ri[:, :TOPK_IN], lrank[:, :TOPK_IN], cnt_tile, tmd)
    lpos_t = jnp.zeros((T // tmd, SUBLANES, tmd), I32).at[:, :TOPK_IN, :].set(
        lpos.reshape(T // tmd, tmd, TOPK_IN).transpose(0, 2, 1))
    x_rows = _dispatch(table, lpos_t, hn_rows, n_blk * MOE_BLOCK, tmd)
    y_rows = _moe(blk_e, n_used, x_rows, p['moe_w1'], p['moe_w3'], p['moe_w2'], p['layer'])
    lpos_pad = jnp.zeros((T, LANES), I32).at[:, :TOPK_IN].set(lpos)
    return _combine(table, h_new, rf, lpos_pad, y_rows, tmd)


def kernel(x, attn_norm_g, w_in, w_out, fox_f_bias, fox_qn_g, fox_kn_g, fox_out_g, gmlp_ln_g, gmlp_ln_b, gmlp_ws, gmlp_bs, gmlp_out_g, gdn_conv_w, gdn_a_log, gdn_dt_bias, gdn_norm_g, pool_w, pool_scale, pool_out_g, ffn_norm_g, router_g_w, router_g_b, router_e_w, router_e_b, moe_w1, moe_w3, moe_w2):
    B, S, D = x.shape
    names = ('attn_norm_g', 'w_in', 'w_out', 'fox_f_bias', 'fox_qn_g', 'fox_kn_g', 'fox_out_g', 'gmlp_ln_g',
             'gmlp_ln_b', 'gmlp_ws', 'gmlp_bs', 'gmlp_out_g', 'gdn_conv_w', 'gdn_a_log', 'gdn_dt_bias',
             'gdn_norm_g', 'pool_w', 'pool_scale', 'pool_out_g', 'ffn_norm_g', 'router_g_w', 'router_g_b',
             'router_e_w', 'router_e_b', 'moe_w1', 'moe_w3', 'moe_w2')
    vals = (attn_norm_g, w_in, w_out, fox_f_bias, fox_qn_g, fox_kn_g, fox_out_g, gmlp_ln_g, gmlp_ln_b, gmlp_ws,
            gmlp_bs, gmlp_out_g, gdn_conv_w, gdn_a_log, gdn_dt_bias, gdn_norm_g, pool_w, pool_scale, pool_out_g,
            ffn_norm_g, router_g_w, router_g_b, router_e_w, router_e_b, moe_w1, moe_w3, moe_w2)
    h = x.reshape(B * S, D)
    stacked = ('moe_w1', 'moe_w3', 'moe_w2')
    for l in range(w_in.shape[0]):
        p = {n: (v if n in stacked else v[l]) for n, v in zip(names, vals)}
        p['layer'] = l
        h = _layer(h, B, S, p)
    return h.reshape(B, S, D)
```

```python
import functools

import jax
import jax.numpy as jnp
import numpy as np
from jax import lax
from jax.experimental import pallas as pl
from jax.experimental.pallas import tpu as pltpu

F32 = jnp.float32
BF16 = jnp.bfloat16
I32 = jnp.int32

EPS = 1e-6
HEAD_DIM = 64
GROUP_W = 256
N_HEADS = GROUP_W // HEAD_DIM
CHUNK = 64
GMLP_LEN = 128
CONV_K = 4
POOL_WINDOWS = (2, 4, 8, 16)
N_GROUPS = 4
EXPERTS_PER_GROUP = 8
N_EXPERTS = N_GROUPS * EXPERTS_PER_GROUP
TOPK_IN = 2
MOE_BLOCK = 256
FOX_VT_ROWS = HEAD_DIM + 16
FOX_BOUND_LOG2 = 40.0
FOX_SKIP_LOG2 = 160.0
RUN_ROWS = 8
LANES = 128
SUBLANES = 8
VMEM_LIMIT = 56 * 1024 * 1024

COL_FQ, COL_FK, COL_FV, COL_GU, COL_GV, COL_DQ, COL_DK, COL_DV, COL_DG, COL_PZ = range(10)
N_BIG_COLS = 10
LANE_FOX, LANE_DECAY, LANE_BETA = 0, 4, 8


def _params(*sem):
    return pltpu.CompilerParams(dimension_semantics=sem, vmem_limit_bytes=VMEM_LIMIT)


def _head_ones():
    r = lax.broadcasted_iota(I32, (GROUP_W, GROUP_W), 0) // HEAD_DIM
    c = lax.broadcasted_iota(I32, (GROUP_W, GROUP_W), 1) // HEAD_DIM
    return (r == c).astype(BF16)


def _head_sums(x, ones_bd):
    hi = x.astype(BF16)
    lo = (x - hi.astype(F32)).astype(BF16)
    return (jnp.dot(hi, ones_bd, preferred_element_type=F32)
            + jnp.dot(lo, ones_bd, preferred_element_type=F32))


def _rms(x, g):
    return x * lax.rsqrt(jnp.mean(x * x, axis=-1, keepdims=True) + EPS) * g


def _mm(a, b):
    return jnp.dot(a.astype(BF16), b.astype(BF16), preferred_element_type=F32)


def _mm_nt(a, b):
    return lax.dot_general(a.astype(BF16), b.astype(BF16), (((1,), (1,)), ((), ())),
                           preferred_element_type=F32)


def _mm_tn(a, b):
    return lax.dot_general(a.astype(BF16), b.astype(BF16), (((0,), (0,)), ((), ())),
                           preferred_element_type=F32)


def _split(a):
    hi = a.astype(BF16)
    return hi, (a - hi.astype(F32)).astype(BF16)


def _mm3(a, b):
    ah, al = _split(a)
    bh, bl = _split(b)
    d = functools.partial(jnp.dot, preferred_element_type=F32)
    return d(ah, bh) + (d(ah, bl) + d(al, bh))


def _inproj_body(x_ref, g_ref, wb_ref, ws_ref, big_ref, small_ref):
    xn = _rms(x_ref[...], g_ref[...]).astype(BF16)
    big_ref[...] = jnp.dot(xn, wb_ref[...], preferred_element_type=F32)
    small_ref[...] = jnp.dot(xn, ws_ref[...], preferred_element_type=F32)


def _inproj(x2d, g, wb, ws, tm):
    T, D = x2d.shape
    nb = wb.shape[1]
    return pl.pallas_call(
        _inproj_body, grid=(T // tm,),
        in_specs=[pl.BlockSpec((tm, D), lambda i: (i, 0)),
                  pl.BlockSpec((1, D), lambda i: (0, 0)),
                  pl.BlockSpec((D, nb), lambda i: (0, 0)),
                  pl.BlockSpec((D, LANES), lambda i: (0, 0))],
        out_specs=[pl.BlockSpec((tm, nb), lambda i: (i, 0)),
                   pl.BlockSpec((tm, LANES), lambda i: (i, 0))],
        out_shape=[jax.ShapeDtypeStruct((T, nb), F32), jax.ShapeDtypeStruct((T, LANES), F32)],
        compiler_params=_params("parallel"), name="inproj")(x2d, g, wb, ws)


def _gates_body(sm_ref, p_ref, col_ref, row_ref, carry_ref, *, ts):
    @pl.when(pl.program_id(1) == 0)
    def _():
        carry_ref[...] = jnp.zeros_like(carry_ref)

    x = sm_ref[...]
    lane = lax.broadcasted_iota(I32, (ts, LANES), 1)
    is_fox = lane < LANE_DECAY
    is_dec = (lane >= LANE_DECAY) & (lane < LANE_BETA)
    is_beta = (lane >= LANE_BETA) & (lane < LANE_BETA + N_HEADS)
    logf = jax.nn.log_sigmoid(x + p_ref[0:1, :])
    g = -jnp.exp(p_ref[2:3, :]) * jax.nn.softplus(x + p_ref[1:2, :])
    beta = jax.nn.sigmoid(x)
    r = lax.broadcasted_iota(I32, (ts, ts), 0)
    c = lax.broadcasted_iota(I32, (ts, ts), 1)
    tri_full = (r >= c).astype(BF16)
    tri_chunk = ((r >= c) & (r // CHUNK == c // CHUNK)).astype(BF16)
    vals = jnp.where(is_fox, logf, jnp.where(is_dec, g, 0.0))
    hi = vals.astype(BF16)
    mid = (vals - hi.astype(F32)).astype(BF16)
    lo = (vals - hi.astype(F32) - mid.astype(F32)).astype(BF16)
    parts = jnp.concatenate([hi, mid, lo], axis=1)

    def tri_sum(tri):
        t = jnp.dot(tri, parts, preferred_element_type=F32)
        return t[:, :LANES] + (t[:, LANES:2 * LANES] + t[:, 2 * LANES:])

    cf = tri_sum(tri_full) + carry_ref[...]
    cg = tri_sum(tri_chunk)
    carry_ref[...] = cf[ts - 1:ts, :]
    out = jnp.where(is_fox, cf, jnp.where(is_dec, cg, jnp.where(is_beta, beta, 0.0)))
    col_ref[...] = out
    row_ref[...] = out.T[:2 * SUBLANES, :]


def _gates(small3, prm, ts):
    B, S, _ = small3.shape
    return pl.pallas_call(
        functools.partial(_gates_body, ts=ts), grid=(B, S // ts),
        in_specs=[pl.BlockSpec((None, ts, LANES), lambda b, j: (b, j, 0)),
                  pl.BlockSpec((SUBLANES, LANES), lambda b, j: (0, 0))],
        out_specs=[pl.BlockSpec((None, ts, LANES), lambda b, j: (b, j, 0)),
                   pl.BlockSpec((None, 2 * SUBLANES, ts), lambda b, j: (b, 0, j))],
        out_shape=[jax.ShapeDtypeStruct((B, S, LANES), F32),
                   jax.ShapeDtypeStruct((B, 2 * SUBLANES, S), F32)],
        scratch_shapes=[pltpu.VMEM((1, LANES), F32)],
        compiler_params=_params("parallel", "arbitrary"), name="gates")(small3, prm)


def _split3_lanes(x, lane, lane0):
    hi = x.astype(BF16).astype(F32)
    mid = (x - hi).astype(BF16).astype(F32)
    lo = (x - hi - mid).astype(BF16).astype(F32)
    return jnp.where(lane == lane0, hi, jnp.where(lane == lane0 + 1, mid, jnp.where(lane == lane0 + 2, lo, 0.0)))


def _fox_body(jstart_ref, q_ref, k_ref, v_ref, ccol_ref, qg_ref, kg_ref, og_ref, o_ref,
              kn_scr, vt_scr, q_scr, m_scr, l_scr, acc_scr, s_scr, kmax_scr, *, tq, nk):
    b = pl.program_id(0)
    i = pl.program_id(1)
    ones_bd = _head_ones()
    lane = lax.broadcasted_iota(I32, (tq, LANES), 1)
    log2e = 1.0 / np.log(2.0)
    c_lane, r_lane = HEAD_DIM, HEAD_DIM + 3

    def head_norm(x, g):
        ss = _head_sums(x * x, ones_bd)
        return x * lax.rsqrt(ss * (1.0 / HEAD_DIM) + EPS) * g

    def head_tile(x, h, extra):
        pair = x[:, (h // 2) * LANES:(h // 2 + 1) * LANES]
        if h % 2:
            pair = pltpu.roll(pair, HEAD_DIM, 1)
        return jnp.where(lane < HEAD_DIM, pair, extra).astype(BF16)

    def sq_norms(x):
        xr = x.astype(BF16).astype(F32)
        return _head_sums(xr * xr, ones_bd)

    @pl.when(i == 0)
    def _():
        k_ones = jnp.where((lane >= r_lane) & (lane < r_lane + 3), 1.0, 0.0)
        vt_tail = (lax.broadcasted_iota(I32, (FOX_VT_ROWS - HEAD_DIM, tq), 0) == 0).astype(BF16)
        kmax = jnp.zeros((1, GROUP_W), F32)
        for c in range(nk):
            rows = slice(c * tq, (c + 1) * tq)
            kc = head_norm(k_ref[rows, :], kg_ref[...])
            kmax = jnp.maximum(kmax, jnp.max(sq_norms(kc), axis=0, keepdims=True))
            vt = v_ref[rows, :].T.astype(BF16)
            cc = ccol_ref[rows, :] * (-log2e)
            for h in range(N_HEADS):
                extra = _split3_lanes(cc[:, LANE_FOX + h:LANE_FOX + h + 1], lane, c_lane) + k_ones
                kn_scr[h, c] = head_tile(kc, h, extra)
                vt_scr[h, c] = jnp.concatenate([vt[h * HEAD_DIM:(h + 1) * HEAD_DIM, :], vt_tail], axis=0)
        kmax_scr[...] = kmax

    qn = head_norm(q_ref[...], qg_ref[...]) * (HEAD_DIM ** -0.5 * log2e)
    bound = jnp.sqrt(sq_norms(qn) * kmax_scr[...]) * 1.001
    bounded = jnp.max(bound) <= FOX_BOUND_LOG2
    c_i = ccol_ref[pl.ds(pl.multiple_of(i * tq, tq), tq), :] * log2e
    q_ones = jnp.where((lane >= c_lane) & (lane < c_lane + 3), 1.0, 0.0)
    for h in range(N_HEADS):
        r_i = bound[:, h * HEAD_DIM:h * HEAD_DIM + 1] - c_i[:, LANE_FOX + h:LANE_FOX + h + 1]
        neg_r = jnp.where(bounded, -r_i, 0.0)
        q_scr[h] = head_tile(qn, h, q_ones + _split3_lanes(neg_r, lane, r_lane))
    acc_scr[...] = jnp.zeros_like(acc_scr)
    causal = (lax.broadcasted_iota(I32, (tq, tq), 0) <= lax.broadcasted_iota(I32, (tq, tq), 1))

    heads = range(N_HEADS)

    def scores(j):
        return [lax.dot_general(kn_scr[h, j], q_scr[h], (((1,), (1,)), ((), ())),
                                preferred_element_type=F32) for h in heads]

    def stash(s):
        for h in heads:
            s_scr[h] = s[h]

    def absorb_general(j, masked):
        p, alpha = [], []
        for h in heads:
            s = s_scr[h]
            if masked:
                s = jnp.where(causal, s, -jnp.inf)
            m_old = m_scr[h]
            m_new = jnp.maximum(m_old, jnp.max(s, axis=0, keepdims=True))
            alpha.append(jnp.exp2(m_old - m_new))
            ph = jnp.exp2(s - m_new)
            l_scr[h] = alpha[h] * l_scr[h] + jnp.sum(ph, axis=0, keepdims=True)
            m_scr[h] = m_new
            p.append(ph.astype(BF16))
        pv = [jnp.dot(vt_scr[h, j], p[h], preferred_element_type=F32) for h in heads]
        for h in heads:
            acc_scr[h] = alpha[h] * acc_scr[h] + pv[h]

    def absorb_bounded(j, masked):
        p = []
        for h in heads:
            s = s_scr[h]
            if masked:
                s = jnp.where(causal, s, -jnp.inf)
            p.append(jnp.exp2(s).astype(BF16))
        pv = [jnp.dot(vt_scr[h, j], p[h], preferred_element_type=F32) for h in heads]
        for h in heads:
            acc_scr[h] += pv[h]

    def run(absorb, j0):
        stash(scores(j0))

        def body(j, c):
            s_next = scores(j + 1)
            absorb(j, False)
            stash(s_next)
            return c

        lax.fori_loop(j0, i, body, 0)
        absorb(i, True)

    @pl.when(bounded)
    def _():
        run(absorb_bounded, jstart_ref[b * nk + i])

    @pl.when(jnp.logical_not(bounded))
    def _():
        m_scr[...] = jnp.full_like(m_scr, -jnp.inf)
        l_scr[...] = jnp.zeros_like(l_scr)
        run(absorb_general, 0)
        for h in heads:
            acc_scr[h, HEAD_DIM:HEAD_DIM + 1, :] = l_scr[h]

    o_t = jnp.concatenate([acc_scr[h, :HEAD_DIM, :] / acc_scr[h, HEAD_DIM:HEAD_DIM + 1, :] for h in heads],
                          axis=0)
    o_ref[...] = _rms(o_t.T, og_ref[...])


def _fox_first_block(gcol, tq):
    B, S, _ = gcol.shape
    nk = S // tq
    c2 = gcol[:, :, LANE_FOX:LANE_FOX + N_HEADS] * (1.0 / np.log(2.0))
    first = c2[:, 0::tq, :]
    last = c2[:, tq - 1::tq, :]
    dead = (first[:, :, None, :] - last[:, None, :, :]) < -FOX_SKIP_LOG2
    dead = dead & (jnp.arange(nk)[None, :, None, None] > jnp.arange(nk)[None, None, :, None])
    return jnp.min(jnp.sum(dead, axis=2), axis=-1).astype(I32).reshape(B * nk)


def _fox(big3, gcol, qg, kg, og, tq):
    B, S, _ = big3.shape
    nk = S // tq
    row = pl.BlockSpec((1, GROUP_W), lambda b, i, js: (0, 0))
    grid_spec = pltpu.PrefetchScalarGridSpec(
        num_scalar_prefetch=1, grid=(B, nk),
        in_specs=[pl.BlockSpec((None, tq, GROUP_W), lambda b, i, js: (b, i, COL_FQ)),
                  pl.BlockSpec((None, S, GROUP_W), lambda b, i, js: (b, 0, COL_FK)),
                  pl.BlockSpec((None, S, GROUP_W), lambda b, i, js: (b, 0, COL_FV)),
                  pl.BlockSpec((None, S, LANES), lambda b, i, js: (b, 0, 0)),
                  row, row, row],
        out_specs=pl.BlockSpec((None, tq, GROUP_W), lambda b, i, js: (b, i, 0)),
        scratch_shapes=[pltpu.VMEM((N_HEADS, nk, tq, LANES), BF16),
                        pltpu.VMEM((N_HEADS, nk, FOX_VT_ROWS, tq), BF16),
                        pltpu.VMEM((N_HEADS, tq, LANES), BF16),
                        pltpu.VMEM((N_HEADS, 1, tq), F32),
                        pltpu.VMEM((N_HEADS, 1, tq), F32),
                        pltpu.VMEM((N_HEADS, FOX_VT_ROWS, tq), F32),
                        pltpu.VMEM((N_HEADS, tq, tq), F32),
                        pltpu.VMEM((1, GROUP_W), F32)])
    return pl.pallas_call(
        functools.partial(_fox_body, tq=tq, nk=nk), grid_spec=grid_spec,
        out_shape=jax.ShapeDtypeStruct((B, S, GROUP_W), F32),
        compiler_params=_params("parallel", "arbitrary"), name="fox")(
            _fox_first_block(gcol, tq), big3, big3, big3, gcol, qg, kg, og)


def _gelu(x):
    return 0.5 * x * (1.0 + lax.erf(x * (2.0 ** -0.5)))


def _gmlp_body(u_ref, v_ref, lg_ref, lb_ref, ws_ref, bst_ref, og_ref, o_ref, *, nwin):
    L = GMLP_LEN
    r = lax.broadcasted_iota(I32, (L, L), 0) // CHUNK
    c = lax.broadcasted_iota(I32, (L, L), 1) // CHUNK
    mask = r >= c
    ws = [jnp.where(mask, ws_ref[h], 0.0).astype(BF16) for h in range(N_HEADS)]
    for n in range(nwin):
        u = _gelu(u_ref[n * L:(n + 1) * L, :])
        v = _gelu(v_ref[n * L:(n + 1) * L, :])
        mu = jnp.mean(v, axis=-1, keepdims=True)
        vc = v - mu
        var = jnp.mean(vc * vc, axis=-1, keepdims=True)
        vn = (vc * lax.rsqrt(var + EPS) * lg_ref[...] + lb_ref[...]).astype(BF16)
        mixed = jnp.concatenate(
            [jnp.dot(ws[h], vn[:, h * HEAD_DIM:(h + 1) * HEAD_DIM], preferred_element_type=F32)
             for h in range(N_HEADS)], axis=-1) + bst_ref[...]
        o_ref[n * L:(n + 1) * L, :] = _rms(u * mixed, og_ref[...])


def _gmlp(big, lg, lb, ws, bst, og, tm):
    T = big.shape[0]
    row = pl.BlockSpec((1, GROUP_W), lambda i: (0, 0))
    return pl.pallas_call(
        functools.partial(_gmlp_body, nwin=tm // GMLP_LEN), grid=(T // tm,),
        in_specs=[pl.BlockSpec((tm, GROUP_W), lambda i: (i, COL_GU)),
                  pl.BlockSpec((tm, GROUP_W), lambda i: (i, COL_GV)),
                  row, row,
                  pl.BlockSpec((N_HEADS, GMLP_LEN, GMLP_LEN), lambda i: (0, 0, 0)),
                  pl.BlockSpec((GMLP_LEN, GROUP_W), lambda i: (0, 0)),
                  row],
        out_specs=pl.BlockSpec((tm, GROUP_W), lambda i: (i, 0)),
        out_shape=jax.ShapeDtypeStruct((T, GROUP_W), F32),
        compiler_params=_params("parallel"), name="gmlp")(big, big, lg, lb, ws, bst, og)


def _gdn_prep_body(q_ref, k_ref, v_ref, hq_ref, hk_ref, hv_ref, w_ref, qo_ref, ko_ref, vo_ref, *, ts):
    first = pl.program_id(1) == 0
    ones_bd = _head_ones()

    def conv(x_ref, halo_ref, w):
        halo = jnp.where(first, 0.0, halo_ref[...])
        xx = jnp.concatenate([halo, x_ref[...]], axis=0)
        y = w[CONV_K - 1:CONV_K, :] * xx[SUBLANES:, :]
        for j in range(CONV_K - 1):
            y = y + w[j:j + 1, :] * pltpu.roll(xx, CONV_K - 1 - j, 0)[SUBLANES:, :]
        return y * jax.nn.sigmoid(y)

    def l2(t):
        return t * lax.rsqrt(_head_sums(t * t, ones_bd) + EPS)

    qo_ref[...] = l2(conv(q_ref, hq_ref, w_ref[0])) * (HEAD_DIM ** -0.5)
    ko_ref[...] = l2(conv(k_ref, hk_ref, w_ref[1]))
    vo_ref[...] = conv(v_ref, hv_ref, w_ref[2])


def _gdn_prep(big3, conv_w3, ts):
    B, S, _ = big3.shape
    hb = ts // SUBLANES
    blk = lambda col: pl.BlockSpec((None, ts, GROUP_W), lambda b, i: (b, i, col))
    halo = lambda col: pl.BlockSpec((None, SUBLANES, GROUP_W),
                                    lambda b, i: (b, jnp.maximum(i * hb - 1, 0), col))
    out = pl.BlockSpec((None, ts, GROUP_W), lambda b, i: (b, i, 0))
    shp = jax.ShapeDtypeStruct((B, S, GROUP_W), F32)
    return pl.pallas_call(
        functools.partial(_gdn_prep_body, ts=ts), grid=(B, S // ts),
        in_specs=[blk(COL_DQ), blk(COL_DK), blk(COL_DV), halo(COL_DQ), halo(COL_DK), halo(COL_DV),
                  pl.BlockSpec((3, CONV_K, GROUP_W), lambda b, i: (0, 0, 0))],
        out_specs=[out, out, out], out_shape=[shp, shp, shp],
        compiler_params=_params("parallel", "parallel"), name="gdn_prep")(
            big3, big3, big3, big3, big3, big3, conv_w3)


def _gdn_body(q_ref, k_ref, v_ref, gate_ref, gcol_ref, grow_ref, ng_ref, o_ref,
              s_scr, u_scr, wq_scr, a_scr, kd_scr, dl_scr, t_scr, p_scr, rhs_scr, *, nchunk):
    C = CHUNK

    @pl.when(pl.program_id(1) == 0)
    def _():
        s_scr[...] = jnp.zeros_like(s_scr)

    r = lax.broadcasted_iota(I32, (C, C), 0)
    c = lax.broadcasted_iota(I32, (C, C), 1)
    tri = r >= c
    strict = r > c
    eye = (r == c).astype(F32)

    items = [(n, h) for n in range(nchunk) for h in range(N_HEADS)]
    mmb = functools.partial(jnp.dot, preferred_element_type=F32)
    for it, (n, h) in enumerate(items):
        rows = slice(n * C, (n + 1) * C)
        sl = slice(h * HEAD_DIM, (h + 1) * HEAD_DIM)
        q, k, v = q_ref[rows, sl], k_ref[rows, sl], v_ref[rows, sl]
        gc = gcol_ref[rows, LANE_DECAY + h:LANE_DECAY + h + 1]
        gr = grow_ref[n, LANE_DECAY + h:LANE_DECAY + h + 1, :]
        beta = gcol_ref[rows, LANE_BETA + h:LANE_BETA + h + 1]
        decay = jnp.exp(jnp.where(tri, gc - gr, -jnp.inf))
        kb = k * beta
        kk = _mm_nt(jnp.concatenate([kb, q], axis=0), k)
        x = jnp.where(strict, -(kk[:C] * decay), 0.0)
        t_scr[it] = eye + x
        p_scr[it] = x.astype(BF16)
        eg = jnp.exp(gc)
        g_last = gc[C - 1:C, :]
        rhs_scr[it] = jnp.concatenate([v * beta, kb * eg], axis=1).astype(BF16)
        wq_scr[n, h, C:, :] = (q * eg).astype(BF16)
        a_scr[n, h] = jnp.where(tri, kk[C:] * decay, 0.0).astype(BF16)
        kd_scr[n, h] = (k * jnp.exp(g_last - gc)).astype(BF16)
        dl_scr[n, h] = jnp.broadcast_to(jnp.exp(g_last), (1, HEAD_DIM))
    for _ in range(5):
        for it in range(len(items)):
            p = p_scr[it]
            p_scr[it] = mmb(p, p).astype(BF16)
        for it in range(len(items)):
            t = t_scr[it]
            t_scr[it] = t + mmb(t.astype(BF16), p_scr[it])
    for it, (n, h) in enumerate(items):
        uw = mmb(t_scr[it].astype(BF16), rhs_scr[it])
        u_scr[n, h] = uw[:, :HEAD_DIM]
        wq_scr[n, h, :C, :] = uw[:, HEAD_DIM:].astype(BF16)

    heads = range(N_HEADS)
    state = [s_scr[h] for h in heads]
    for n in range(nchunk):
        rows = slice(n * C, (n + 1) * C)
        ws = [mmb(wq_scr[n, h], state[h].astype(BF16)) for h in heads]
        vb = [(u_scr[n, h] - ws[h][:C]).astype(BF16) for h in heads]
        o = [ws[h][C:] + mmb(a_scr[n, h], vb[h]) for h in heads]
        state = [state[h] * dl_scr[n, h] + lax.dot_general(
            kd_scr[n, h], vb[h], (((0,), (0,)), ((), ())), preferred_element_type=F32) for h in heads]
        gate = gate_ref[rows, :]
        y = jnp.concatenate([_rms(o[h], ng_ref[...]) for h in heads], axis=-1)
        o_ref[rows, :] = y * (gate * jax.nn.sigmoid(gate))
    for h in heads:
        s_scr[h] = state[h]


def _gdn(qn, kn, vv, big3, gcol, grow4, ng, ts):
    B, S, _ = qn.shape
    nchunk = ts // CHUNK
    blk = pl.BlockSpec((None, ts, GROUP_W), lambda b, i: (b, i, 0))
    return pl.pallas_call(
        functools.partial(_gdn_body, nchunk=nchunk), grid=(B, S // ts),
        in_specs=[blk, blk, blk,
                  pl.BlockSpec((None, ts, GROUP_W), lambda b, i: (b, i, COL_DG)),
                  pl.BlockSpec((None, ts, LANES), lambda b, i: (b, i, 0)),
                  pl.BlockSpec((None, nchunk, 2 * SUBLANES, CHUNK), lambda b, i: (b, i, 0, 0)),
                  pl.BlockSpec((1, HEAD_DIM), lambda b, i: (0, 0))],
        out_specs=blk, out_shape=jax.ShapeDtypeStruct((B, S, GROUP_W), F32),
        scratch_shapes=[pltpu.VMEM((N_HEADS, HEAD_DIM, HEAD_DIM), F32),
                        pltpu.VMEM((nchunk, N_HEADS, CHUNK, HEAD_DIM), F32),
                        pltpu.VMEM((nchunk, N_HEADS, 2 * CHUNK, HEAD_DIM), BF16),
                        pltpu.VMEM((nchunk, N_HEADS, CHUNK, CHUNK), BF16),
                        pltpu.VMEM((nchunk, N_HEADS, CHUNK, HEAD_DIM), BF16),
                        pltpu.VMEM((nchunk, N_HEADS, 1, HEAD_DIM), F32),
                        pltpu.VMEM((nchunk * N_HEADS, CHUNK, CHUNK), F32),
                        pltpu.VMEM((nchunk * N_HEADS, CHUNK, CHUNK), BF16),
                        pltpu.VMEM((nchunk * N_HEADS, CHUNK, 2 * HEAD_DIM), BF16)],
        compiler_params=_params("parallel", "arbitrary"), name="gdn")(
            qn, kn, vv, big3, gcol, grow4, ng)


def _pool_body(z_ref, halo_ref, w_ref, sc_ref, og_ref, o_ref, *, ts):
    i = pl.program_id(1)
    hr = 2 * SUBLANES
    z = z_ref[...]
    halo = jnp.where(i == 0, 0.0, halo_ref[...])
    s1 = jnp.concatenate([halo, z], axis=0)
    s2 = s1 + pltpu.roll(s1, 1, 0)
    s4 = s2 + pltpu.roll(s2, 2, 0)
    s8 = s4 + pltpu.roll(s4, 4, 0)
    s16 = s8 + pltpu.roll(s8, 8, 0)
    grp = lax.broadcasted_iota(I32, (ts, GROUP_W), 1) // (GROUP_W // len(POOL_WINDOWS))
    t = lax.broadcasted_iota(I32, (ts, GROUP_W), 0) + i * ts
    total = jnp.where(grp == 0, s2[hr:], jnp.where(grp == 1, s4[hr:], jnp.where(grp == 2, s8[hr:], s16[hr:])))
    win = jnp.where(grp == 0, POOL_WINDOWS[0], jnp.where(grp == 1, POOL_WINDOWS[1],
                    jnp.where(grp == 2, POOL_WINDOWS[2], POOL_WINDOWS[3])))
    pooled = total / jnp.minimum(t + 1, win).astype(F32)
    y = _mm(pooled - z, w_ref[...]) * sc_ref[...]
    o_ref[...] = _rms(y, og_ref[...])


def _pool(big3, wbd, sc, og, ts):
    B, S, _ = big3.shape
    hr = 2 * SUBLANES
    hb = ts // hr
    row = pl.BlockSpec((1, GROUP_W), lambda b, i: (0, 0))
    return pl.pallas_call(
        functools.partial(_pool_body, ts=ts), grid=(B, S // ts),
        in_specs=[pl.BlockSpec((None, ts, GROUP_W), lambda b, i: (b, i, COL_PZ)),
                  pl.BlockSpec((None, hr, GROUP_W), lambda b, i: (b, jnp.maximum(i * hb - 1, 0), COL_PZ)),
                  pl.BlockSpec((GROUP_W, GROUP_W), lambda b, i: (0, 0)), row, row],
        out_specs=pl.BlockSpec((None, ts, GROUP_W), lambda b, i: (b, i, 0)),
        out_shape=jax.ShapeDtypeStruct((B, S, GROUP_W), F32),
        compiler_params=_params("parallel", "parallel"), name="pool")(big3, big3, wbd, sc, og)


def _outproj_body(ya_ref, yb_ref, yc_ref, yd_ref, h_ref, wo_ref, g_ref, rw_ref, rb_ref,
                  hnew_ref, hn_ref, ri_ref, rf_ref, *, tm, d):
    y = jnp.concatenate([ya_ref[...], yb_ref[...], yc_ref[...], yd_ref[...]], axis=-1).astype(BF16)
    h_new = h_ref[...] + jnp.dot(y, wo_ref[...], preferred_element_type=F32)
    hnew_ref[...] = h_new
    hn = _rms(h_new, g_ref[...])
    hn_hi, hn_lo = _split(hn)
    hn_ref[...] = hn_hi
    t = jnp.dot(hn_hi, rw_ref[...], preferred_element_type=F32)
    logits = (t[:, :LANES] + t[:, LANES:]
              + jnp.dot(hn_lo, rw_ref[:, :LANES], preferred_element_type=F32)) + rb_ref[...]
    lane = lax.broadcasted_iota(I32, (tm, LANES), 1)
    neg = -jnp.inf
    big_lane = LANES

    def masked_top(vals, mask):
        v = jnp.where(mask, vals, neg)
        mx = jnp.max(v, axis=-1, keepdims=True)
        idx = jnp.min(jnp.where(mask & (v == mx), lane, big_lane), axis=-1, keepdims=True)
        return v, mx, idx

    gmask = lane < N_GROUPS
    gv, gmx, gidx = masked_top(logits, gmask)
    g_top = 1.0 / jnp.sum(jnp.where(gmask, jnp.exp(gv - gmx), 0.0), axis=-1, keepdims=True)
    lo = N_GROUPS + gidx * EXPERTS_PER_GROUP
    emask = (lane >= lo) & (lane < lo + EXPERTS_PER_GROUP)
    ev, emx, eidx1 = masked_top(logits, emask)
    esum = jnp.sum(jnp.where(emask, jnp.exp(ev - emx), 0.0), axis=-1, keepdims=True)
    p1 = 1.0 / esum
    _, emx2, eidx2 = masked_top(logits, emask & (lane != eidx1))
    p2 = jnp.exp(emx2 - emx) / esum
    denom = p1 + p2
    ri_ref[...] = jnp.where(lane == 0, eidx1 - N_GROUPS, jnp.where(lane == 1, eidx2 - N_GROUPS, 0))
    rf_ref[...] = jnp.where(lane == 0, g_top * p1 / denom, jnp.where(lane == 1, g_top * p2 / denom, 0.0))


def _outproj(ya, yb, yc, yd, h2d, wo, g, rw, rb, tm):
    T, D = h2d.shape
    nseg = D // LANES
    yblk = pl.BlockSpec((tm, GROUP_W), lambda i: (i, 0))
    return pl.pallas_call(
        functools.partial(_outproj_body, tm=tm, d=D), grid=(T // tm,),
        in_specs=[yblk, yblk, yblk, yblk,
                  pl.BlockSpec((tm, D), lambda i: (i, 0)),
                  pl.BlockSpec((D, D), lambda i: (0, 0)),
                  pl.BlockSpec((1, D), lambda i: (0, 0)),
                  pl.BlockSpec((D, 2 * LANES), lambda i: (0, 0)),
                  pl.BlockSpec((1, LANES), lambda i: (0, 0))],
        out_specs=[pl.BlockSpec((tm, D), lambda i: (i, 0)),
                   pl.BlockSpec((tm, D), lambda i: (i, 0)),
                   pl.BlockSpec((tm, LANES), lambda i: (i, 0)),
                   pl.BlockSpec((tm, LANES), lambda i: (i, 0))],
        out_shape=[jax.ShapeDtypeStruct((T, D), F32),
                   jax.ShapeDtypeStruct((T, D), BF16),
                   jax.ShapeDtypeStruct((T, LANES), I32),
                   jax.ShapeDtypeStruct((T, LANES), F32)],
        compiler_params=_params("parallel"), name="outproj")(ya, yb, yc, yd, h2d, wo, g, rw, rb)


def _rank_body(ri_ref, rank_ref, cnt_ref, *, tm):
    lane = lax.broadcasted_iota(I32, (tm, LANES), 1)
    e = ri_ref[...]
    oh = [lane == e[:, s:s + 1] for s in range(TOPK_IN)]
    m = (oh[0].astype(F32) + oh[1].astype(F32)).astype(BF16)
    below = (lax.broadcasted_iota(I32, (tm, tm), 0) > lax.broadcasted_iota(I32, (tm, tm), 1)).astype(BF16)
    before = jnp.dot(below, m, preferred_element_type=F32)
    rank = [jnp.sum(jnp.where(oh[s], before, 0.0), axis=-1, keepdims=True) for s in range(TOPK_IN)]
    rank_ref[...] = jnp.where(lane == 0, rank[0], jnp.where(lane == 1, rank[1], 0.0)).astype(I32)
    total = before[tm - 1:tm, :] + m[tm - 1:tm, :].astype(F32)
    cnt_ref[...] = jnp.broadcast_to(total, cnt_ref.shape).astype(I32)


def _rank(ri, tm):
    T = ri.shape[0]
    return pl.pallas_call(
        functools.partial(_rank_body, tm=tm), grid=(T // tm,),
        in_specs=[pl.BlockSpec((tm, LANES), lambda i: (i, 0))],
        out_specs=[pl.BlockSpec((tm, LANES), lambda i: (i, 0)),
                   pl.BlockSpec((SUBLANES, LANES), lambda i: (i, 0))],
        out_shape=[jax.ShapeDtypeStruct((T, LANES), I32),
                   jax.ShapeDtypeStruct((T // tm * SUBLANES, LANES), I32)],
        compiler_params=_params("parallel"), name="rank")(ri)


def _dispatch_plan(expert, lrank, cnt_tile, tm):
    T = expert.shape[0]
    nt = T // tm
    counts = jnp.sum(cnt_tile, axis=0)
    padded = (counts + RUN_ROWS + MOE_BLOCK - 1) // MOE_BLOCK * MOE_BLOCK
    pad_end = jnp.cumsum(padded)
    pad_start = pad_end - padded
    gstart = pad_start[None, :] + jnp.cumsum(cnt_tile, axis=0) - cnt_tile
    nchunk = (cnt_tile + RUN_ROWS - 1) // RUN_ROWS
    chunk_end = jnp.cumsum(nchunk, axis=1)
    lstart = (chunk_end - nchunk) * RUN_ROWS
    onehot = expert[:, :, None] == jnp.arange(N_EXPERTS, dtype=I32)[None, None, :]
    pick = lambda tab: jnp.sum(jnp.where(onehot, jnp.repeat(tab, tm, axis=0)[:, None, :], 0), axis=-1)
    lpos =(lrank + pick(lstart)).astype(I32)
    max_chunks = tm * TOPK_IN // RUN_ROWS + N_EXPERTS
    c = jnp.arange(max_chunks, dtype=I32)
    ce = jnp.minimum(jnp.sum(chunk_end[:, None, :] <= c[None, :, None], axis=-1), N_EXPERTS - 1)
    ce_hot = ce[:, :, None] == jnp.arange(N_EXPERTS, dtype=I32)[None, None, :]
    take = lambda tab: jnp.sum(jnp.where(ce_hot, tab[:, None, :], 0), axis=-1)
    chunk_row = take(gstart) + (c[None, :] - take(chunk_end - nchunk)) * RUN_ROWS
    chunk_row = jnp.where(c[None, :] < chunk_end[:, -1:], chunk_row, 0)
    table = jnp.zeros((nt, 1, LANES), I32).at[:, 0, :max_chunks].set(chunk_row.astype(I32))
    table = table.at[:, 0, LANES - 1].set(chunk_end[:, -1].astype(I32))
    n_blk = -(-(T * TOPK_IN + N_EXPERTS * RUN_ROWS) // MOE_BLOCK) + N_EXPERTS + 1
    blk_start = jnp.arange(n_blk, dtype=I32) * MOE_BLOCK
    blk_e = jnp.minimum(jnp.sum(pad_end[None, :] <= blk_start[:, None], axis=-1), N_EXPERTS - 1).astype(I32)
    n_used = (pad_end[-1] // MOE_BLOCK).astype(I32).reshape(1)
    zero_start = jnp.concatenate([pad_start + counts, pad_end[-1:]])
    zero_end = jnp.concatenate([pad_end, jnp.full((1,), n_blk * MOE_BLOCK, pad_end.dtype)])
    nz = N_EXPERTS + 1
    zfill = jnp.zeros((1, LANES), I32).at[0, :nz].set(zero_start.astype(I32))
    zfill = zfill.at[0, nz:2 * nz].set(((zero_end - zero_start) // RUN_ROWS).astype(I32))
    zfill = zfill.at[0, 2 * nz:3 * nz].set((zero_end - RUN_ROWS).astype(I32))
    return lpos, table, zfill, blk_e, n_used, n_blk


def _dispatch_body(zf_ref, tab_ref, lpos_ref, hn_ref, x_hbm, xs, zero, sem, zsem, n_prev, *, tm, nseg, nrow):
    i = pl.program_id(0)
    slot = i % 2
    step = RUN_ROWS * nseg

    @pl.when(i == 0)
    def _():
        zero[...] = jnp.zeros_like(zero)

        def zero_copy(r):
            return pltpu.make_async_copy(zero, x_hbm.at[pl.ds(pl.multiple_of(r * nseg, nseg), step), :], zsem)

        nz = N_EXPERTS + 1

        def for_chunks(fn):
            for e in range(nz):
                def body(k, carry, e=e):
                    fn(zero_copy(zf_ref[0, e] + k * RUN_ROWS))
                    return carry
                lax.fori_loop(0, zf_ref[0, nz + e], body, 0)

        for_chunks(lambda c: c.start())
        for_chunks(lambda c: c.wait())
        for e in range(nz):
            zero_copy(zf_ref[0, 2 * nz + e]).start()
        for e in range(nz):
            zero_copy(zf_ref[0, 2 * nz + e]).wait()

    row = lax.broadcasted_iota(I32, (nrow, tm), 0)
    sel = (row == lpos_ref[0:1, :]) | (row == lpos_ref[1:2, :])
    rows = jnp.dot(sel.astype(BF16), hn_ref[...], preferred_element_type=F32)
    for k in range(nseg):
        xs[slot, pl.ds(k, nrow, stride=nseg), :] = rows[:, k * LANES:(k + 1) * LANES]

    def chunk_copy(sl, c):
        return pltpu.make_async_copy(
            xs.at[sl, pl.ds(pl.multiple_of(c * step, step), step), :],
            x_hbm.at[pl.ds(pl.multiple_of(tab_ref[0, 0, c] * nseg, nseg), step), :], sem.at[sl])

    def wait_chunks(sl, n):
        def wait(c, carry):
            chunk_copy(sl, 0).wait()
            return carry
        lax.fori_loop(0, n, wait, 0)

    n_chunks = tab_ref[0, 0, LANES - 1]

    @pl.when(i > 0)
    def _():
        wait_chunks(1 - slot, n_prev[0])

    def start(c, carry):
        chunk_copy(slot, c).start()
        return carry

    lax.fori_loop(0, n_chunks, start, 0)
    n_prev[0] = n_chunks

    @pl.when(i == pl.num_programs(0) - 1)
    def _():
        wait_chunks(slot, n_chunks)


def _dispatch(zfill, table, lpos_t, hn, n_pad, tm):
    T, D = hn.shape
    nseg = D // LANES
    nrow = tm * TOPK_IN + N_EXPERTS * RUN_ROWS
    return pl.pallas_call(
        functools.partial(_dispatch_body, tm=tm, nseg=nseg, nrow=nrow), grid=(T // tm,),
        in_specs=[pl.BlockSpec(memory_space=pltpu.SMEM),
                  pl.BlockSpec((1, 1, LANES), lambda i: (i, 0, 0), memory_space=pltpu.SMEM),
                  pl.BlockSpec((None, SUBLANES, tm), lambda i: (i, 0, 0)),
                  pl.BlockSpec((tm, D), lambda i: (i, 0))],
        out_specs=pl.BlockSpec(memory_space=pl.ANY),
        out_shape=jax.ShapeDtypeStruct((n_pad * nseg, LANES), F32),
        scratch_shapes=[pltpu.VMEM((2, nrow * nseg, LANES), F32), pltpu.VMEM((RUN_ROWS * nseg, LANES), F32),
                        pltpu.SemaphoreType.DMA((2,)), pltpu.SemaphoreType.DMA, pltpu.SMEM((1,), I32)],
        compiler_params=_params("arbitrary"), name="dispatch")(zfill, table, lpos_t, hn)


def _moe_body(blk_e_ref, n_used_ref, seg_ref, nxt_ref, x_ref, w1_hbm, w3_hbm, w2_hbm, y_ref,
              wf1, wf3, wf2, w1b, w3b, w2b, sem, *, nseg, layer):
    b = pl.program_id(0)
    R = MOE_BLOCK

    def fetch(e, slot):
        return [pltpu.make_async_copy(src.at[layer, e], dst.at[slot], sem.at[slot, k])
                for k, (src, dst) in enumerate(((w1_hbm, wf1), (w3_hbm, wf3), (w2_hbm, wf2)))]

    @pl.when(b < n_used_ref[0])
    def _():
        @pl.when((b == 0) | (blk_e_ref[b] != blk_e_ref[jnp.maximum(b - 1, 0)]))
        def _():
            e = blk_e_ref[b]
            slot = seg_ref[b] % 2

            @pl.when(b == 0)
            def _():
                for c in fetch(e, slot):
                    c.start()

            for c in fetch(e, slot):
                c.wait()
            w1b[...] = wf1[slot].astype(BF16)
            w3b[...] = wf3[slot].astype(BF16)
            w2b[...] = wf2[slot].astype(BF16)

            @pl.when(nxt_ref[b] >= 0)
            def _():
                for c in fetch(nxt_ref[b], 1 - slot):
                    c.start()

        x = jnp.concatenate([x_ref[pl.ds(s, R, stride=nseg), :] for s in range(nseg)], axis=-1).astype(BF16)
        a = jnp.dot(x, w1b[...], preferred_element_type=F32)
        g = jnp.dot(x, w3b[...], preferred_element_type=F32)
        hid = (a * jax.nn.sigmoid(a) * g).astype(BF16)
        y = jnp.dot(hid, w2b[...], preferred_element_type=F32)
        for s in range(nseg):
            y_ref[pl.ds(s, R, stride=nseg), :] = y[:, s * LANES:(s + 1) * LANES]

    @pl.when(b >= n_used_ref[0])
    def _():
        y_ref[...] = jnp.zeros_like(y_ref)


def _moe(blk_e, n_used, x_rows, w1, w3, w2, layer):
    n_blk = blk_e.shape[0]
    _, _, D, DE = w1.shape
    nseg = D // LANES
    R = MOE_BLOCK
    blk = jnp.arange(n_blk, dtype=I32)
    used = blk < n_used[0]
    change = (blk > 0) & (blk_e != jnp.roll(blk_e, 1)) & used
    seg = jnp.cumsum(change.astype(I32)).astype(I32)
    later = (blk[None, :] > blk[:, None]) & (seg[None, :] > seg[:, None]) & used[None, :]
    nxt_blk = jnp.min(jnp.where(later, blk[None, :], n_blk), axis=1)
    nxt = jnp.where(nxt_blk < n_blk, blk_e[jnp.minimum(nxt_blk, n_blk - 1)], -1).astype(I32)
    rows = lambda b, be, nu, sg, nx: (jnp.minimum(b, nu[0] - 1), 0)
    grid_spec = pltpu.PrefetchScalarGridSpec(
        num_scalar_prefetch=4, grid=(n_blk,),
        in_specs=[pl.BlockSpec((R * nseg, LANES), rows),
                  pl.BlockSpec(memory_space=pl.ANY), pl.BlockSpec(memory_space=pl.ANY),
                  pl.BlockSpec(memory_space=pl.ANY)],
        out_specs=pl.BlockSpec((R * nseg, LANES), lambda b, be, nu, sg, nx: (b, 0)),
        scratch_shapes=[pltpu.VMEM((2, D, DE), F32), pltpu.VMEM((2, D, DE), F32), pltpu.VMEM((2, DE, D), F32),
                        pltpu.VMEM((D, DE), BF16), pltpu.VMEM((D, DE), BF16), pltpu.VMEM((DE, D), BF16),
                        pltpu.SemaphoreType.DMA((2, 3))])
    return pl.pallas_call(
        functools.partial(_moe_body, nseg=nseg, layer=layer), grid_spec=grid_spec,
        out_shape=jax.ShapeDtypeStruct(x_rows.shape, F32),
        compiler_params=_params("arbitrary"), name="moe")(blk_e, n_used, seg, nxt, x_rows, w1, w3, w2)


def _combine_body(tab_ref, tabn_ref, h_ref, rf_ref, lpos_ref, y_hbm, o_ref, ybuf, sem, *, tm, nseg, nrow):
    i = pl.program_id(0)
    slot = i % 2
    step = RUN_ROWS * nseg

    def start_chunks(tab, sl):
        def start(c, carry):
            pltpu.make_async_copy(
                y_hbm.at[pl.ds(pl.multiple_of(tab[0, 0, c] * nseg, nseg), step), :],
                ybuf.at[sl, pl.ds(pl.multiple_of(c * step, step), step), :], sem.at[sl]).start()
            return carry
        lax.fori_loop(0, tab[0, 0, LANES - 1], start, 0)

    @pl.when(i == 0)
    def _():
        ybuf[...] = jnp.zeros_like(ybuf)
        start_chunks(tab_ref, 0)

    @pl.when(i + 1 < pl.num_programs(0))
    def _():
        start_chunks(tabn_ref, 1 - slot)

    col = lax.broadcasted_iota(I32, (tm, nrow), 1)
    sel = jnp.zeros((tm, nrow), F32)
    for s in range(TOPK_IN):
        sel = sel + jnp.where(col == lpos_ref[:, s:s + 1], rf_ref[:, s:s + 1], 0.0)

    def wait(c, carry):
        pltpu.make_async_copy(y_hbm.at[pl.ds(0, step), :], ybuf.at[slot, pl.ds(0, step), :],
                              sem.at[slot]).wait()
        return carry

    lax.fori_loop(0, tab_ref[0, 0, LANES - 1], wait, 0)
    y = jnp.concatenate([ybuf[slot, pl.ds(k, nrow, stride=nseg), :] for k in range(nseg)], axis=-1)
    o_ref[...] = h_ref[...] + jnp.dot(sel.astype(BF16), y.astype(BF16), preferred_element_type=F32)


def _combine(table, h2d, rf, lpos, y_rows, tm):
    T, D = h2d.shape
    nseg = D // LANES
    nt = T // tm
    nrow = tm * TOPK_IN + N_EXPERTS * RUN_ROWS
    return pl.pallas_call(
        functools.partial(_combine_body, tm=tm, nseg=nseg, nrow=nrow), grid=(nt,),
        in_specs=[pl.BlockSpec((1, 1, LANES), lambda i: (i, 0, 0), memory_space=pltpu.SMEM),
                  pl.BlockSpec((1, 1, LANES), lambda i: (jnp.minimum(i + 1, nt - 1), 0, 0),
                               memory_space=pltpu.SMEM),
                  pl.BlockSpec((tm, D), lambda i: (i, 0)),
                  pl.BlockSpec((tm, LANES), lambda i: (i, 0)),
                  pl.BlockSpec((tm, LANES), lambda i: (i, 0)),
                  pl.BlockSpec(memory_space=pl.ANY)],
        out_specs=pl.BlockSpec((tm, D), lambda i: (i, 0)),
        out_shape=jax.ShapeDtypeStruct((T, D), F32),
        scratch_shapes=[pltpu.VMEM((2, nrow * nseg, LANES), F32), pltpu.SemaphoreType.DMA((2,))],
        compiler_params=_params("arbitrary"), name="combine")(table, table, h2d, rf, lpos, y_rows)


def _pad_lanes(a, lane0, rows=1):
    out = jnp.zeros((rows, LANES), a.dtype)
    return out.at[:, lane0:lane0 + a.shape[-1]].set(a.reshape(rows, -1))


def _layer(h2d, B, S, p):
    T, D = h2d.shape
    tile = lambda a, n: jnp.tile(a.reshape(1, -1), (1, n))

    offs = np.cumsum([0, GROUP_W, GROUP_W, GROUP_W, N_HEADS, GROUP_W, GROUP_W,
                      GROUP_W, GROUP_W, GROUP_W, N_HEADS, N_HEADS, GROUP_W, GROUP_W])
    seg = lambda k: p['w_in'][:, offs[k]:offs[k + 1]]
    wb = jnp.concatenate([seg(0), seg(1), seg(2), seg(4), seg(5), seg(6), seg(7), seg(8), seg(11), seg(12)],
                         axis=1).astype(BF16)
    ws = jnp.zeros((D, LANES), F32)
    ws = ws.at[:, LANE_FOX:LANE_FOX + N_HEADS].set(seg(3))
    ws = ws.at[:, LANE_DECAY:LANE_DECAY + N_HEADS].set(seg(9))
    ws = ws.at[:, LANE_BETA:LANE_BETA + N_HEADS].set(seg(10)).astype(BF16)
    gate_prm = jnp.concatenate([_pad_lanes(p['fox_f_bias'], LANE_FOX), _pad_lanes(p['gdn_dt_bias'], LANE_DECAY),
                                _pad_lanes(p['gdn_a_log'], LANE_DECAY), jnp.zeros((SUBLANES - 3, LANES), F32)], axis=0)

    big, small = _inproj(h2d, p['attn_norm_g'].reshape(1, D), wb, ws, tm=min(512, T))
    big3 = big.reshape(B, S, N_BIG_COLS * GROUP_W)
    ts = min(512, S)
    gcol, grow = _gates(small.reshape(B, S, LANES), gate_prm, ts)

    tq = min(256, S)
    ya = _fox(big3, gcol, tile(p['fox_qn_g'], N_HEADS), tile(p['fox_kn_g'], N_HEADS),
              p['fox_out_g'].reshape(1, GROUP_W), tq)

    bst = jnp.repeat(p['gmlp_bs'].T, HEAD_DIM, axis=1)
    yb = _gmlp(big, p['gmlp_ln_g'].reshape(1, -1), p['gmlp_ln_b'].reshape(1, -1), p['gmlp_ws'], bst,
               p['gmlp_out_g'].reshape(1, -1), tm=min(512, T))

    conv_w3 = p['gdn_conv_w'].reshape(CONV_K, 3, GROUP_W).transpose(1, 0, 2)
    qn, kn, vv = _gdn_prep(big3, conv_w3, ts)
    grow4 = grow.reshape(B, 2 * SUBLANES, S // CHUNK, CHUNK).transpose(0, 2, 1, 3)
    yc = _gdn(qn, kn, vv, big3, gcol, grow4, p['gdn_norm_g'].reshape(1, HEAD_DIM), ts)

    wbd = jax.scipy.linalg.block_diag(*[p['pool_w'][g] for g in range(len(POOL_WINDOWS))]).astype(BF16)
    yd = _pool(big3, wbd, p['pool_scale'].reshape(1, -1), p['pool_out_g'].reshape(1, -1), ts)

    rw = jnp.zeros((D, LANES), F32).at[:, :N_GROUPS].set(p['router_g_w'])
    rw = rw.at[:, N_GROUPS:N_GROUPS + N_EXPERTS].set(p['router_e_w'])
    rw = jnp.concatenate(_split(rw), axis=1)
    rb = jnp.zeros((1, LANES), F32).at[0, :N_GROUPS].set(p['router_g_b'])
    rb = rb.at[0, N_GROUPS:N_GROUPS + N_EXPERTS].set(p['router_e_b'])
    flat = lambda a: a.reshape(T, GROUP_W)
    h_new, hn_rows, ri, rf = _outproj(flat(ya), yb, flat(yc), flat(yd), h2d, p['w_out'].astype(BF16),
                                      p['ffn_norm_g'].reshape(1, D), rw, rb, tm=min(512, T))

    tmd = min(256, T)
    lrank, cnt = _rank(ri, tmd)
    cnt_tile = cnt.reshape(T // tmd, SUBLANES, LANES)[:, 0, :N_EXPERTS]
    lpos, table, zfill, blk_e, n_used, n_blk = _dispatch_plan(ri[:, :TOPK_IN], lrank[:, :TOPK_IN], cnt_tile, tmd)
    lpos_t = jnp.zeros((T // tmd, SUBLANES, tmd), I32).at[:, :TOPK_IN, :].set(
        lpos.reshape(T // tmd, tmd, TOPK_IN).transpose(0, 2, 1))
    x_rows = _dispatch(zfill, table, lpos_t, hn_rows, n_blk * MOE_BLOCK, tmd)
    y_rows = _moe(blk_e, n_used, x_rows, p['moe_w1'], p['moe_w3'], p['moe_w2'], p['layer'])
    lpos_pad = jnp.zeros((T, LANES), I32).at[:, :TOPK_IN].set(lpos)
    return _combine(table, h_new, rf, lpos_pad, y_rows, tmd)


def kernel(x, attn_norm_g, w_in, w_out, fox_f_bias, fox_qn_g, fox_kn_g, fox_out_g, gmlp_ln_g, gmlp_ln_b, gmlp_ws, gmlp_bs, gmlp_out_g, gdn_conv_w, gdn_a_log, gdn_dt_bias, gdn_norm_g, pool_w, pool_scale, pool_out_g, ffn_norm_g, router_g_w, router_g_b, router_e_w, router_e_b, moe_w1, moe_w3, moe_w2):
    B, S, D = x.shape
    names = ('attn_norm_g', 'w_in', 'w_out', 'fox_f_bias', 'fox_qn_g', 'fox_kn_g', 'fox_out_g', 'gmlp_ln_g',
             'gmlp_ln_b', 'gmlp_ws', 'gmlp_bs', 'gmlp_out_g', 'gdn_conv_w', 'gdn_a_log', 'gdn_dt_bias',
             'gdn_norm_g', 'pool_w', 'pool_scale', 'pool_out_g', 'ffn_norm_g', 'router_g_w', 'router_g_b',
             'router_e_w', 'router_e_b', 'moe_w1', 'moe_w3', 'moe_w2')
    vals = (attn_norm_g, w_in, w_out, fox_f_bias, fox_qn_g, fox_kn_g, fox_out_g, gmlp_ln_g, gmlp_ln_b, gmlp_ws,
            gmlp_bs, gmlp_out_g, gdn_conv_w, gdn_a_log, gdn_dt_bias, gdn_norm_g, pool_w, pool_scale, pool_out_g,
            ffn_norm_g, router_g_w, router_g_b, router_e_w, router_e_b, moe_w1, moe_w3, moe_w2)
    h = x.reshape(B * S, D)
    stacked = ('moe_w1', 'moe_w3', 'moe_w2')
    for l in range(w_in.shape[0]):
        p = {n: (v if n in stacked else v[l]) for n, v in zip(names, vals)}
        p['layer'] = l
        h = _layer(h, B, S, p)
    return h.reshape(B, S, D)
```

```python
import functools

import jax
import jax.numpy as jnp
import numpy as np
from jax import lax
from jax.experimental import pallas as pl
from jax.experimental.pallas import tpu as pltpu

F32 = jnp.float32
BF16 = jnp.bfloat16
I32 = jnp.int32

EPS = 1e-6
HEAD_DIM = 64
GROUP_W = 256
N_HEADS = GROUP_W // HEAD_DIM
CHUNK = 64
GMLP_LEN = 128
CONV_K = 4
POOL_WINDOWS = (2, 4, 8, 16)
N_GROUPS = 4
EXPERTS_PER_GROUP = 8
N_EXPERTS = N_GROUPS * EXPERTS_PER_GROUP
TOPK_IN = 2
MOE_BLOCK = 256
FOX_VT_ROWS = HEAD_DIM + 16
FOX_BOUND_LOG2 = 40.0
FOX_SKIP_LOG2 = 160.0
RUN_ROWS = 8
LANES = 128
SUBLANES = 8
VMEM_LIMIT = 56 * 1024 * 1024

COL_FQ, COL_FK, COL_FV, COL_GU, COL_GV, COL_DQ, COL_DK, COL_DV, COL_DG, COL_PZ = range(10)
N_BIG_COLS = 10
LANE_FOX, LANE_DECAY, LANE_BETA = 0, 4, 8


def _params(*sem):
    return pltpu.CompilerParams(dimension_semantics=sem, vmem_limit_bytes=VMEM_LIMIT)


def _head_ones():
    r = lax.broadcasted_iota(I32, (GROUP_W, GROUP_W), 0) // HEAD_DIM
    c = lax.broadcasted_iota(I32, (GROUP_W, GROUP_W), 1) // HEAD_DIM
    return (r == c).astype(BF16)


def _head_sums(x, ones_bd):
    hi = x.astype(BF16)
    lo = (x - hi.astype(F32)).astype(BF16)
    return (jnp.dot(hi, ones_bd, preferred_element_type=F32)
            + jnp.dot(lo, ones_bd, preferred_element_type=F32))


def _rms(x, g):
    return x * lax.rsqrt(jnp.mean(x * x, axis=-1, keepdims=True) + EPS) * g


def _mm(a, b):
    return jnp.dot(a.astype(BF16), b.astype(BF16), preferred_element_type=F32)


def _mm_nt(a, b):
    return lax.dot_general(a.astype(BF16), b.astype(BF16), (((1,), (1,)), ((), ())),
                           preferred_element_type=F32)


def _mm_tn(a, b):
    return lax.dot_general(a.astype(BF16), b.astype(BF16), (((0,), (0,)), ((), ())),
                           preferred_element_type=F32)


def _split(a):
    hi = a.astype(BF16)
    return hi, (a - hi.astype(F32)).astype(BF16)


def _mm3(a, b):
    ah, al = _split(a)
    bh, bl = _split(b)
    d = functools.partial(jnp.dot, preferred_element_type=F32)
    return d(ah, bh) + (d(ah, bl) + d(al, bh))


def _inproj_body(x_ref, g_ref, wb_ref, ws_ref, big_ref, small_ref):
    xn = _rms(x_ref[...], g_ref[...]).astype(BF16)
    big_ref[...] = jnp.dot(xn, wb_ref[...], preferred_element_type=F32)
    small_ref[...] = jnp.dot(xn, ws_ref[...], preferred_element_type=F32)


def _inproj(x2d, g, wb, ws, tm):
    T, D = x2d.shape
    nb = wb.shape[1]
    return pl.pallas_call(
        _inproj_body, grid=(T // tm,),
        in_specs=[pl.BlockSpec((tm, D), lambda i: (i, 0)),
                  pl.BlockSpec((1, D), lambda i: (0, 0)),
                  pl.BlockSpec((D, nb), lambda i: (0, 0)),
                  pl.BlockSpec((D, LANES), lambda i: (0, 0))],
        out_specs=[pl.BlockSpec((tm, nb), lambda i: (i, 0)),
                   pl.BlockSpec((tm, LANES), lambda i: (i, 0))],
        out_shape=[jax.ShapeDtypeStruct((T, nb), F32), jax.ShapeDtypeStruct((T, LANES), F32)],
        compiler_params=_params("parallel"), name="inproj")(x2d, g, wb, ws)


def _gates_body(sm_ref, p_ref, col_ref, row_ref, carry_ref, *, ts):
    @pl.when(pl.program_id(1) == 0)
    def _():
        carry_ref[...] = jnp.zeros_like(carry_ref)

    x = sm_ref[...]
    lane = lax.broadcasted_iota(I32, (ts, LANES), 1)
    is_fox = lane < LANE_DECAY
    is_dec = (lane >= LANE_DECAY) & (lane < LANE_BETA)
    is_beta = (lane >= LANE_BETA) & (lane < LANE_BETA + N_HEADS)
    logf = jax.nn.log_sigmoid(x + p_ref[0:1, :])
    g = -jnp.exp(p_ref[2:3, :]) * jax.nn.softplus(x + p_ref[1:2, :])
    beta = jax.nn.sigmoid(x)
    r = lax.broadcasted_iota(I32, (ts, ts), 0)
    c = lax.broadcasted_iota(I32, (ts, ts), 1)
    tri_full = (r >= c).astype(BF16)
    tri_chunk = ((r >= c) & (r // CHUNK == c // CHUNK)).astype(BF16)
    vals = jnp.where(is_fox, logf, jnp.where(is_dec, g, 0.0))
    hi = vals.astype(BF16)
    mid = (vals - hi.astype(F32)).astype(BF16)
    lo = (vals - hi.astype(F32) - mid.astype(F32)).astype(BF16)
    parts = jnp.concatenate([hi, mid, lo], axis=1)

    def tri_sum(tri):
        t = jnp.dot(tri, parts, preferred_element_type=F32)
        return t[:, :LANES] + (t[:, LANES:2 * LANES] + t[:, 2 * LANES:])

    cf = tri_sum(tri_full) + carry_ref[...]
    cg = tri_sum(tri_chunk)
    carry_ref[...] = cf[ts - 1:ts, :]
    out = jnp.where(is_fox, cf, jnp.where(is_dec, cg, jnp.where(is_beta, beta, 0.0)))
    col_ref[...] = out
    row_ref[...] = out.T[:2 * SUBLANES, :]


def _gates(small3, prm, ts):
    B, S, _ = small3.shape
    return pl.pallas_call(
        functools.partial(_gates_body, ts=ts), grid=(B, S // ts),
        in_specs=[pl.BlockSpec((None, ts, LANES), lambda b, j: (b, j, 0)),
                  pl.BlockSpec((SUBLANES, LANES), lambda b, j: (0, 0))],
        out_specs=[pl.BlockSpec((None, ts, LANES), lambda b, j: (b, j, 0)),
                   pl.BlockSpec((None, 2 * SUBLANES, ts), lambda b, j: (b, 0, j))],
        out_shape=[jax.ShapeDtypeStruct((B, S, LANES), F32),
                   jax.ShapeDtypeStruct((B, 2 * SUBLANES, S), F32)],
        scratch_shapes=[pltpu.VMEM((1, LANES), F32)],
        compiler_params=_params("parallel", "arbitrary"), name="gates")(small3, prm)


def _split3_lanes(x, lane, lane0):
    hi = x.astype(BF16).astype(F32)
    mid = (x - hi).astype(BF16).astype(F32)
    lo = (x - hi - mid).astype(BF16).astype(F32)
    return jnp.where(lane == lane0, hi, jnp.where(lane == lane0 + 1, mid, jnp.where(lane == lane0 + 2, lo, 0.0)))


def _fox_body(jstart_ref, q_ref, k_ref, v_ref, ccol_ref, qg_ref, kg_ref, og_ref, o_ref,
              kn_scr, vt_scr, q_scr, m_scr, l_scr, acc_scr, s_scr, kmax_scr, *, tq, nk):
    b = pl.program_id(0)
    i = pl.program_id(1)
    ones_bd = _head_ones()
    lane = lax.broadcasted_iota(I32, (tq, LANES), 1)
    log2e = 1.0 / np.log(2.0)
    c_lane, r_lane = HEAD_DIM, HEAD_DIM + 3

    def head_norm(x, g):
        ss = _head_sums(x * x, ones_bd)
        return x * lax.rsqrt(ss * (1.0 / HEAD_DIM) + EPS) * g

    def head_tile(x, h, extra):
        pair = x[:, (h // 2) * LANES:(h // 2 + 1) * LANES]
        if h % 2:
            pair = pltpu.roll(pair, HEAD_DIM, 1)
        return jnp.where(lane < HEAD_DIM, pair, extra).astype(BF16)

    def sq_norms(x):
        xr = x.astype(BF16).astype(F32)
        return _head_sums(xr * xr, ones_bd)

    @pl.when(i == 0)
    def _():
        k_ones = jnp.where((lane >= r_lane) & (lane < r_lane + 3), 1.0, 0.0)
        vt_tail = (lax.broadcasted_iota(I32, (FOX_VT_ROWS - HEAD_DIM, tq), 0) == 0).astype(BF16)
        kmax = jnp.zeros((1, GROUP_W), F32)
        for c in range(nk):
            rows = slice(c * tq, (c + 1) * tq)
            kc = head_norm(k_ref[rows, :], kg_ref[...])
            kmax = jnp.maximum(kmax, jnp.max(sq_norms(kc), axis=0, keepdims=True))
            vt = v_ref[rows, :].T.astype(BF16)
            cc = ccol_ref[rows, :] * (-log2e)
            for h in range(N_HEADS):
                extra = _split3_lanes(cc[:, LANE_FOX + h:LANE_FOX + h + 1], lane, c_lane) + k_ones
                kn_scr[h, c] = head_tile(kc, h, extra)
                vt_scr[h, c] = jnp.concatenate([vt[h * HEAD_DIM:(h + 1) * HEAD_DIM, :], vt_tail], axis=0)
        kmax_scr[...] = kmax

    qn = head_norm(q_ref[...], qg_ref[...]) * (HEAD_DIM ** -0.5 * log2e)
    bound = jnp.sqrt(sq_norms(qn) * kmax_scr[...]) * 1.001
    bounded = jnp.max(bound) <= FOX_BOUND_LOG2
    c_i = ccol_ref[pl.ds(pl.multiple_of(i * tq, tq), tq), :] * log2e
    q_ones = jnp.where((lane >= c_lane) & (lane < c_lane + 3), 1.0, 0.0)
    for h in range(N_HEADS):
        r_i = bound[:, h * HEAD_DIM:h * HEAD_DIM + 1] - c_i[:, LANE_FOX + h:LANE_FOX + h + 1]
        neg_r = jnp.where(bounded, -r_i, 0.0)
        q_scr[h] = head_tile(qn, h, q_ones + _split3_lanes(neg_r, lane, r_lane))
    acc_scr[...] = jnp.zeros_like(acc_scr)
    causal = (lax.broadcasted_iota(I32, (tq, tq), 0) <= lax.broadcasted_iota(I32, (tq, tq), 1))

    heads = range(N_HEADS)

    def scores(j):
        return [lax.dot_general(kn_scr[h, j], q_scr[h], (((1,), (1,)), ((), ())),
                                preferred_element_type=F32) for h in heads]

    def stash(s):
        for h in heads:
            s_scr[h] = s[h]

    def absorb_general(j, masked):
        p, alpha = [], []
        for h in heads:
            s = s_scr[h]
            if masked:
                s = jnp.where(causal, s, -jnp.inf)
            m_old = m_scr[h]
            m_new = jnp.maximum(m_old, jnp.max(s, axis=0, keepdims=True))
            alpha.append(jnp.exp2(m_old - m_new))
            ph = jnp.exp2(s - m_new)
            l_scr[h] = alpha[h] * l_scr[h] + jnp.sum(ph, axis=0, keepdims=True)
            m_scr[h] = m_new
            p.append(ph.astype(BF16))
        pv = [jnp.dot(vt_scr[h, j], p[h], preferred_element_type=F32) for h in heads]
        for h in heads:
            acc_scr[h] = alpha[h] * acc_scr[h] + pv[h]

    def absorb_bounded(j, masked):
        p = []
        for h in heads:
            s = s_scr[h]
            if masked:
                s = jnp.where(causal, s, -jnp.inf)
            p.append(jnp.exp2(s).astype(BF16))
        pv = [jnp.dot(vt_scr[h, j], p[h], preferred_element_type=F32) for h in heads]
        for h in heads:
            acc_scr[h] += pv[h]

    def run(absorb, j0):
        stash(scores(j0))

        def body(j, c):
            s_next = scores(j + 1)
            absorb(j, False)
            stash(s_next)
            return c

        lax.fori_loop(j0, i, body, 0)
        absorb(i, True)

    @pl.when(bounded)
    def _():
        run(absorb_bounded, jstart_ref[b * nk + i])

    @pl.when(jnp.logical_not(bounded))
    def _():
        m_scr[...] = jnp.full_like(m_scr, -jnp.inf)
        l_scr[...] = jnp.zeros_like(l_scr)
        run(absorb_general, 0)
        for h in heads:
            acc_scr[h, HEAD_DIM:HEAD_DIM + 1, :] = l_scr[h]

    o_t = jnp.concatenate([acc_scr[h, :HEAD_DIM, :] / acc_scr[h, HEAD_DIM:HEAD_DIM + 1, :] for h in heads],
                          axis=0)
    o_ref[...] = _rms(o_t.T, og_ref[...])


def _fox_first_block(gcol, tq):
    B, S, _ = gcol.shape
    nk = S // tq
    c2 = gcol[:, :, LANE_FOX:LANE_FOX + N_HEADS] * (1.0 / np.log(2.0))
    first = c2[:, 0::tq, :]
    last = c2[:, tq - 1::tq, :]
    dead = (first[:, :, None, :] - last[:, None, :, :]) < -FOX_SKIP_LOG2
    dead = dead & (jnp.arange(nk)[None, :, None, None] > jnp.arange(nk)[None, None, :, None])
    return jnp.min(jnp.sum(dead, axis=2), axis=-1).astype(I32).reshape(B * nk)


def _fox(big3, gcol, qg, kg, og, tq):
    B, S, _ = big3.shape
    nk = S // tq
    row = pl.BlockSpec((1, GROUP_W), lambda b, i, js: (0, 0))
    grid_spec = pltpu.PrefetchScalarGridSpec(
        num_scalar_prefetch=1, grid=(B, nk),
        in_specs=[pl.BlockSpec((None, tq, GROUP_W), lambda b, i, js: (b, i, COL_FQ)),
                  pl.BlockSpec((None, S, GROUP_W), lambda b, i, js: (b, 0, COL_FK)),
                  pl.BlockSpec((None, S, GROUP_W), lambda b, i, js: (b, 0, COL_FV)),
                  pl.BlockSpec((None, S, LANES), lambda b, i, js: (b, 0, 0)),
                  row, row, row],
        out_specs=pl.BlockSpec((None, tq, GROUP_W), lambda b, i, js: (b, i, 0)),
        scratch_shapes=[pltpu.VMEM((N_HEADS, nk, tq, LANES), BF16),
                        pltpu.VMEM((N_HEADS, nk, FOX_VT_ROWS, tq), BF16),
                        pltpu.VMEM((N_HEADS, tq, LANES), BF16),
                        pltpu.VMEM((N_HEADS, 1, tq), F32),
                        pltpu.VMEM((N_HEADS, 1, tq), F32),
                        pltpu.VMEM((N_HEADS, FOX_VT_ROWS, tq), F32),
                        pltpu.VMEM((N_HEADS, tq, tq), F32),
                        pltpu.VMEM((1, GROUP_W), F32)])
    return pl.pallas_call(
        functools.partial(_fox_body, tq=tq, nk=nk), grid_spec=grid_spec,
        out_shape=jax.ShapeDtypeStruct((B, S, GROUP_W), F32),
        compiler_params=_params("parallel", "arbitrary"), name="fox")(
            _fox_first_block(gcol, tq), big3, big3, big3, gcol, qg, kg, og)


def _gelu(x):
    return 0.5 * x * (1.0 + lax.erf(x * (2.0 ** -0.5)))


def _gmlp_body(u_ref, v_ref, lg_ref, lb_ref, ws_ref, bst_ref, og_ref, o_ref, *, nwin):
    L = GMLP_LEN
    r = lax.broadcasted_iota(I32, (L, L), 0) // CHUNK
    c = lax.broadcasted_iota(I32, (L, L), 1) // CHUNK
    mask = r >= c
    ws = [jnp.where(mask, ws_ref[h], 0.0).astype(BF16) for h in range(N_HEADS)]
    for n in range(nwin):
        u = _gelu(u_ref[n * L:(n + 1) * L, :])
        v = _gelu(v_ref[n * L:(n + 1) * L, :])
        mu = jnp.mean(v, axis=-1, keepdims=True)
        vc = v - mu
        var = jnp.mean(vc * vc, axis=-1, keepdims=True)
        vn = (vc * lax.rsqrt(var + EPS) * lg_ref[...] + lb_ref[...]).astype(BF16)
        mixed = jnp.concatenate(
            [jnp.dot(ws[h], vn[:, h * HEAD_DIM:(h + 1) * HEAD_DIM], preferred_element_type=F32)
             for h in range(N_HEADS)], axis=-1) + bst_ref[...]
        o_ref[n * L:(n + 1) * L, :] = _rms(u * mixed, og_ref[...])


def _gmlp(big, lg, lb, ws, bst, og, tm):
    T = big.shape[0]
    row = pl.BlockSpec((1, GROUP_W), lambda i: (0, 0))
    return pl.pallas_call(
        functools.partial(_gmlp_body, nwin=tm // GMLP_LEN), grid=(T // tm,),
        in_specs=[pl.BlockSpec((tm, GROUP_W), lambda i: (i, COL_GU)),
                  pl.BlockSpec((tm, GROUP_W), lambda i: (i, COL_GV)),
                  row, row,
                  pl.BlockSpec((N_HEADS, GMLP_LEN, GMLP_LEN), lambda i: (0, 0, 0)),
                  pl.BlockSpec((GMLP_LEN, GROUP_W), lambda i: (0, 0)),
                  row],
        out_specs=pl.BlockSpec((tm, GROUP_W), lambda i: (i, 0)),
        out_shape=jax.ShapeDtypeStruct((T, GROUP_W), F32),
        compiler_params=_params("parallel"), name="gmlp")(big, big, lg, lb, ws, bst, og)


def _gdn_body(q_ref, k_ref, v_ref, w_ref, gate_ref, gcol_ref, grow_ref, ng_ref, o_ref,
              s_scr, u_scr, wq_scr, a_scr, kd_scr, dl_scr, t_scr, p_scr, rhs_scr, halo_scr, qkv_scr, *, nchunk):
    C = CHUNK
    ts = nchunk * C
    first = pl.program_id(1) == 0

    @pl.when(first)
    def _():
        s_scr[...] = jnp.zeros_like(s_scr)

    ones_bd = _head_ones()
    for a, x_ref in enumerate((q_ref, k_ref, v_ref)):
        w = w_ref[a]
        x = x_ref[...]
        xx = jnp.concatenate([jnp.where(first, 0.0, halo_scr[a]), x], axis=0)
        halo_scr[a] = x[ts - SUBLANES:, :]
        y = w[CONV_K - 1:CONV_K, :] * x
        for j in range(CONV_K - 1):
            y = y + w[j:j + 1, :] * pltpu.roll(xx, CONV_K - 1 - j, 0)[SUBLANES:, :]
        y = y * jax.nn.sigmoid(y)
        if a < 2:
            y = y * lax.rsqrt(_head_sums(y * y, ones_bd) + EPS)
        qkv_scr[a] = y * (HEAD_DIM ** -0.5) if a == 0 else y

    r = lax.broadcasted_iota(I32, (C, C), 0)
    c = lax.broadcasted_iota(I32, (C, C), 1)
    tri = r >= c
    strict = r > c
    eye = (r == c).astype(F32)

    items = [(n, h) for n in range(nchunk) for h in range(N_HEADS)]
    mmb = functools.partial(jnp.dot, preferred_element_type=F32)
    for it, (n, h) in enumerate(items):
        rows = slice(n * C, (n + 1) * C)
        sl = slice(h * HEAD_DIM, (h + 1) * HEAD_DIM)
        q, k, v = qkv_scr[0, rows, sl], qkv_scr[1, rows, sl], qkv_scr[2, rows, sl]
        gc = gcol_ref[rows, LANE_DECAY + h:LANE_DECAY + h + 1]
        gr = grow_ref[n, LANE_DECAY + h:LANE_DECAY + h + 1, :]
        beta = gcol_ref[rows, LANE_BETA + h:LANE_BETA + h + 1]
        decay = jnp.exp(jnp.where(tri, gc - gr, -jnp.inf))
        kb = k * beta
        kk = _mm_nt(jnp.concatenate([kb, q], axis=0), k)
        x = jnp.where(strict, -(kk[:C] * decay), 0.0)
        t_scr[it] = eye + x
        p_scr[it] = x.astype(BF16)
        eg = jnp.exp(gc)
        g_last = gc[C - 1:C, :]
        rhs_scr[it] = jnp.concatenate([v * beta, kb * eg], axis=1).astype(BF16)
        wq_scr[n, h, C:, :] = (q * eg).astype(BF16)
        a_scr[n, h] = jnp.where(tri, kk[C:] * decay, 0.0).astype(BF16)
        kd_scr[n, h] = (k * jnp.exp(g_last - gc)).astype(BF16)
        dl_scr[n, h] = jnp.broadcast_to(jnp.exp(g_last), (1, HEAD_DIM))
    for _ in range(5):
        for it in range(len(items)):
            p = p_scr[it]
            p_scr[it] = mmb(p, p).astype(BF16)
        for it in range(len(items)):
            t = t_scr[it]
            t_scr[it] = t + mmb(t.astype(BF16), p_scr[it])
    for it, (n, h) in enumerate(items):
        uw = mmb(t_scr[it].astype(BF16), rhs_scr[it])
        u_scr[n, h] = uw[:, :HEAD_DIM]
        wq_scr[n, h, :C, :] = uw[:, HEAD_DIM:].astype(BF16)

    heads = range(N_HEADS)
    state = [s_scr[h] for h in heads]
    for n in range(nchunk):
        rows = slice(n * C, (n + 1) * C)
        ws = [mmb(wq_scr[n, h], state[h].astype(BF16)) for h in heads]
        vb = [(u_scr[n, h] - ws[h][:C]).astype(BF16) for h in heads]
        o = [ws[h][C:] + mmb(a_scr[n, h], vb[h]) for h in heads]
        state = [state[h] * dl_scr[n, h] + lax.dot_general(
            kd_scr[n, h], vb[h], (((0,), (0,)), ((), ())), preferred_element_type=F32) for h in heads]
        gate = gate_ref[rows, :]
        y = jnp.concatenate([_rms(o[h], ng_ref[...]) for h in heads], axis=-1)
        o_ref[rows, :] = y * (gate * jax.nn.sigmoid(gate))
    for h in heads:
        s_scr[h] = state[h]


def _gdn(big3, conv_w3, gcol, grow4, ng, ts):
    B, S, _ = big3.shape
    nchunk = ts // CHUNK
    col = lambda c: pl.BlockSpec((None, ts, GROUP_W), lambda b, i: (b, i, c))
    blk = col(0)
    return pl.pallas_call(
        functools.partial(_gdn_body, nchunk=nchunk), grid=(B, S // ts),
        in_specs=[col(COL_DQ), col(COL_DK), col(COL_DV),
                  pl.BlockSpec((3, CONV_K, GROUP_W), lambda b, i: (0, 0, 0)),
                  col(COL_DG),
                  pl.BlockSpec((None, ts, LANES), lambda b, i: (b, i, 0)),
                  pl.BlockSpec((None, nchunk, 2 * SUBLANES, CHUNK), lambda b, i: (b, i, 0, 0)),
                  pl.BlockSpec((1, HEAD_DIM), lambda b, i: (0, 0))],
        out_specs=blk, out_shape=jax.ShapeDtypeStruct((B, S, GROUP_W), F32),
        scratch_shapes=[pltpu.VMEM((N_HEADS, HEAD_DIM, HEAD_DIM), F32),
                        pltpu.VMEM((nchunk, N_HEADS, CHUNK, HEAD_DIM), F32),
                        pltpu.VMEM((nchunk, N_HEADS, 2 * CHUNK, HEAD_DIM), BF16),
                        pltpu.VMEM((nchunk, N_HEADS, CHUNK, CHUNK), BF16),
                        pltpu.VMEM((nchunk, N_HEADS, CHUNK, HEAD_DIM), BF16),
                        pltpu.VMEM((nchunk, N_HEADS, 1, HEAD_DIM), F32),
                        pltpu.VMEM((nchunk * N_HEADS, CHUNK, CHUNK), F32),
                        pltpu.VMEM((nchunk * N_HEADS, CHUNK, CHUNK), BF16),
                        pltpu.VMEM((nchunk * N_HEADS, CHUNK, 2 * HEAD_DIM), BF16),
                        pltpu.VMEM((3, SUBLANES, GROUP_W), F32),
                        pltpu.VMEM((3, ts, GROUP_W), F32)],
        compiler_params=_params("parallel", "arbitrary"), name="gdn")(
            big3, big3, big3, conv_w3, big3, gcol, grow4, ng)


def _pool_body(z_ref, halo_ref, w_ref, sc_ref, og_ref, o_ref, *, ts):
    i = pl.program_id(1)
    hr = 2 * SUBLANES
    z = z_ref[...]
    halo = jnp.where(i == 0, 0.0, halo_ref[...])
    s1 = jnp.concatenate([halo, z], axis=0)
    s2 = s1 + pltpu.roll(s1, 1, 0)
    s4 = s2 + pltpu.roll(s2, 2, 0)
    s8 = s4 + pltpu.roll(s4, 4, 0)
    s16 = s8 + pltpu.roll(s8, 8, 0)
    grp = lax.broadcasted_iota(I32, (ts, GROUP_W), 1) // (GROUP_W // len(POOL_WINDOWS))
    t = lax.broadcasted_iota(I32, (ts, GROUP_W), 0) + i * ts
    total = jnp.where(grp == 0, s2[hr:], jnp.where(grp == 1, s4[hr:], jnp.where(grp == 2, s8[hr:], s16[hr:])))
    win = jnp.where(grp == 0, POOL_WINDOWS[0], jnp.where(grp == 1, POOL_WINDOWS[1],
                    jnp.where(grp == 2, POOL_WINDOWS[2], POOL_WINDOWS[3])))
    pooled = total / jnp.minimum(t + 1, win).astype(F32)
    y = _mm(pooled - z, w_ref[...]) * sc_ref[...]
    o_ref[...] = _rms(y, og_ref[...])


def _pool(big3, wbd, sc, og, ts):
    B, S, _ = big3.shape
    hr = 2 * SUBLANES
    hb = ts // hr
    row = pl.BlockSpec((1, GROUP_W), lambda b, i: (0, 0))
    return pl.pallas_call(
        functools.partial(_pool_body, ts=ts), grid=(B, S // ts),
        in_specs=[pl.BlockSpec((None, ts, GROUP_W), lambda b, i: (b, i, COL_PZ)),
                  pl.BlockSpec((None, hr, GROUP_W), lambda b, i: (b, jnp.maximum(i * hb - 1, 0), COL_PZ)),
                  pl.BlockSpec((GROUP_W, GROUP_W), lambda b, i: (0, 0)), row, row],
        out_specs=pl.BlockSpec((None, ts, GROUP_W), lambda b, i: (b, i, 0)),
        out_shape=jax.ShapeDtypeStruct((B, S, GROUP_W), F32),
        compiler_params=_params("parallel", "parallel"), name="pool")(big3, big3, wbd, sc, og)


def _outproj_body(ya_ref, yb_ref, yc_ref, yd_ref, h_ref, wo_ref, g_ref, rw_ref, rb_ref,
                  hnew_ref, hn_ref, ri_ref, rf_ref, cnt_ref, *, tm, tr):
    y = jnp.concatenate([ya_ref[...], yb_ref[...], yc_ref[...], yd_ref[...]], axis=-1).astype(BF16)
    h_new = h_ref[...] + jnp.dot(y, wo_ref[...], preferred_element_type=F32)
    hnew_ref[...] = h_new
    hn = _rms(h_new, g_ref[...])
    hn_hi, hn_lo = _split(hn)
    hn_ref[...] = hn_hi
    t = jnp.dot(hn_hi, rw_ref[...], preferred_element_type=F32)
    logits = (t[:, :LANES] + t[:, LANES:]
              + jnp.dot(hn_lo, rw_ref[:, :LANES], preferred_element_type=F32)) + rb_ref[...]
    lane = lax.broadcasted_iota(I32, (tm, LANES), 1)
    neg = -jnp.inf
    big_lane = LANES

    def masked_top(vals, mask):
        v = jnp.where(mask, vals, neg)
        mx = jnp.max(v, axis=-1, keepdims=True)
        idx = jnp.min(jnp.where(mask & (v == mx), lane, big_lane), axis=-1, keepdims=True)
        return v, mx, idx

    gmask = lane < N_GROUPS
    gv, gmx, gidx = masked_top(logits, gmask)
    g_top = 1.0 / jnp.sum(jnp.where(gmask, jnp.exp(gv - gmx), 0.0), axis=-1, keepdims=True)
    lo = N_GROUPS + gidx * EXPERTS_PER_GROUP
    emask = (lane >= lo) & (lane < lo + EXPERTS_PER_GROUP)
    ev, emx, eidx1 = masked_top(logits, emask)
    esum = jnp.sum(jnp.where(emask, jnp.exp(ev - emx), 0.0), axis=-1, keepdims=True)
    p1 = 1.0 / esum
    _, emx2, eidx2 = masked_top(logits, emask & (lane != eidx1))
    p2 = jnp.exp(emx2 - emx) / esum
    denom = p1 + p2
    rf_ref[...] = jnp.where(lane == 0, g_top * p1 / denom, jnp.where(lane == 1, g_top * p2 / denom, 0.0))

    expert = [eidx1 - N_GROUPS, eidx2 - N_GROUPS]
    hot = [lane == e for e in expert]
    m = (hot[0] | hot[1]).astype(BF16)
    below = (lax.broadcasted_iota(I32, (tr, tr), 0) > lax.broadcasted_iota(I32, (tr, tr), 1)).astype(BF16)
    before = jnp.concatenate(
        [jnp.dot(below, m[k * tr:(k + 1) * tr, :], preferred_element_type=F32) for k in range(tm // tr)], axis=0)
    rank = [jnp.sum(jnp.where(hot[s], before, 0.0), axis=-1, keepdims=True).astype(I32) for s in range(TOPK_IN)]
    out = jnp.zeros((tm, LANES), I32)
    for s in range(TOPK_IN):
        out = jnp.where(lane == s, expert[s], jnp.where(lane == TOPK_IN + s, rank[s], out))
    ri_ref[...] = out
    for k in range(tm // tr):
        last = (k + 1) * tr - 1
        total = before[last:last + 1, :] + m[last:last + 1, :].astype(F32)
        cnt_ref[k * SUBLANES:(k + 1) * SUBLANES, :] = jnp.broadcast_to(total, (SUBLANES, LANES)).astype(I32)


def _outproj(ya, yb, yc, yd, h2d, wo, g, rw, rb, tm, tr):
    T, D = h2d.shape
    yblk = pl.BlockSpec((tm, GROUP_W), lambda i: (i, 0))
    cnt_rows = tm // tr * SUBLANES
    return pl.pallas_call(
        functools.partial(_outproj_body, tm=tm, tr=tr), grid=(T // tm,),
        in_specs=[yblk, yblk, yblk, yblk,
                  pl.BlockSpec((tm, D), lambda i: (i, 0)),
                  pl.BlockSpec((D, D), lambda i: (0, 0)),
                  pl.BlockSpec((1, D), lambda i: (0, 0)),
                  pl.BlockSpec((D, 2 * LANES), lambda i: (0, 0)),
                  pl.BlockSpec((1, LANES), lambda i: (0, 0))],
        out_specs=[pl.BlockSpec((tm, D), lambda i: (i, 0)),
                   pl.BlockSpec((tm, D), lambda i: (i, 0)),
                   pl.BlockSpec((tm, LANES), lambda i: (i, 0)),
                   pl.BlockSpec((tm, LANES), lambda i: (i, 0)),
                   pl.BlockSpec((cnt_rows, LANES), lambda i: (i, 0))],
        out_shape=[jax.ShapeDtypeStruct((T, D), F32),
                   jax.ShapeDtypeStruct((T, D), BF16),
                   jax.ShapeDtypeStruct((T, LANES), I32),
                   jax.ShapeDtypeStruct((T, LANES), F32),
                   jax.ShapeDtypeStruct((T // tr * SUBLANES, LANES), I32)],
        compiler_params=_params("parallel"), name="outproj")(ya, yb, yc, yd, h2d, wo, g, rw, rb)


def _dispatch_plan(expert, lrank, cnt_tile, tm):
    T = expert.shape[0]
    nt = T // tm
    counts = jnp.sum(cnt_tile, axis=0)
    padded = (counts + RUN_ROWS + MOE_BLOCK - 1) // MOE_BLOCK * MOE_BLOCK
    pad_end = jnp.cumsum(padded)
    pad_start = pad_end - padded
    gstart = pad_start[None, :] + jnp.cumsum(cnt_tile, axis=0) - cnt_tile
    nchunk = (cnt_tile + RUN_ROWS - 1) // RUN_ROWS
    chunk_end = jnp.cumsum(nchunk, axis=1)
    lstart = (chunk_end - nchunk) * RUN_ROWS
    onehot = expert[:, :, None] == jnp.arange(N_EXPERTS, dtype=I32)[None, None, :]
    pick = lambda tab: jnp.sum(jnp.where(onehot, jnp.repeat(tab, tm, axis=0)[:, None, :], 0), axis=-1)
    lpos =(lrank + pick(lstart)).astype(I32)
    max_chunks = tm * TOPK_IN // RUN_ROWS + N_EXPERTS
    c = jnp.arange(max_chunks, dtype=I32)
    ce = jnp.minimum(jnp.sum(chunk_end[:, None, :] <= c[None, :, None], axis=-1), N_EXPERTS - 1)
    ce_hot = ce[:, :, None] == jnp.arange(N_EXPERTS, dtype=I32)[None, None, :]
    take = lambda tab: jnp.sum(jnp.where(ce_hot, tab[:, None, :], 0), axis=-1)
    chunk_row = take(gstart) + (c[None, :] - take(chunk_end - nchunk)) * RUN_ROWS
    chunk_row = jnp.where(c[None, :] < chunk_end[:, -1:], chunk_row, 0)
    table = jnp.zeros((nt, 1, LANES), I32).at[:, 0, :max_chunks].set(chunk_row.astype(I32))
    table = table.at[:, 0, LANES - 1].set(chunk_end[:, -1].astype(I32))
    n_blk = -(-(T * TOPK_IN + N_EXPERTS * RUN_ROWS) // MOE_BLOCK) + N_EXPERTS + 1
    blk_start = jnp.arange(n_blk, dtype=I32) * MOE_BLOCK
    blk_e = jnp.minimum(jnp.sum(pad_end[None, :] <= blk_start[:, None], axis=-1), N_EXPERTS - 1).astype(I32)
    n_used = (pad_end[-1] // MOE_BLOCK).astype(I32).reshape(1)
    zero_start = jnp.concatenate([pad_start + counts, pad_end[-1:]])
    zero_end = jnp.concatenate([pad_end, jnp.full((1,), n_blk * MOE_BLOCK, pad_end.dtype)])
    nz = N_EXPERTS + 1
    zfill = jnp.zeros((1, LANES), I32).at[0, :nz].set(zero_start.astype(I32))
    zfill = zfill.at[0, nz:2 * nz].set(((zero_end - zero_start) // RUN_ROWS).astype(I32))
    zfill = zfill.at[0, 2 * nz:3 * nz].set((zero_end - RUN_ROWS).astype(I32))
    return lpos, table, zfill, blk_e, n_used, n_blk


def _dispatch_body(zf_ref, tab_ref, lpos_ref, hn_ref, x_hbm, xs, zero, sem, zsem, n_prev, *, tm, nseg, nrow):
    i = pl.program_id(0)
    slot = i % 2
    step = RUN_ROWS * nseg

    @pl.when(i == 0)
    def _():
        zero[...] = jnp.zeros_like(zero)

        def zero_copy(r):
            return pltpu.make_async_copy(zero, x_hbm.at[pl.ds(pl.multiple_of(r * nseg, nseg), step), :], zsem)

        nz = N_EXPERTS + 1

        def for_chunks(fn):
            for e in range(nz):
                def body(k, carry, e=e):
                    fn(zero_copy(zf_ref[0, e] + k * RUN_ROWS))
                    return carry
                lax.fori_loop(0, zf_ref[0, nz + e], body, 0)

        for_chunks(lambda c: c.start())
        for_chunks(lambda c: c.wait())
        for e in range(nz):
            zero_copy(zf_ref[0, 2 * nz + e]).start()
        for e in range(nz):
            zero_copy(zf_ref[0, 2 * nz + e]).wait()

    row = lax.broadcasted_iota(I32, (nrow, tm), 0)
    sel = (row == lpos_ref[0:1, :]) | (row == lpos_ref[1:2, :])
    rows = jnp.dot(sel.astype(BF16), hn_ref[...], preferred_element_type=F32)
    for k in range(nseg):
        xs[slot, pl.ds(k, nrow, stride=nseg), :] = rows[:, k * LANES:(k + 1) * LANES]

    def chunk_copy(sl, c):
        return pltpu.make_async_copy(
            xs.at[sl, pl.ds(pl.multiple_of(c * step, step), step), :],
            x_hbm.at[pl.ds(pl.multiple_of(tab_ref[0, 0, c] * nseg, nseg), step), :], sem.at[sl])

    def wait_chunks(sl, n):
        def wait(c, carry):
            chunk_copy(sl, 0).wait()
            return carry
        lax.fori_loop(0, n, wait, 0)

    n_chunks = tab_ref[0, 0, LANES - 1]

    @pl.when(i > 0)
    def _():
        wait_chunks(1 - slot, n_prev[0])

    def start(c, carry):
        chunk_copy(slot, c).start()
        return carry

    lax.fori_loop(0, n_chunks, start, 0)
    n_prev[0] = n_chunks

    @pl.when(i == pl.num_programs(0) - 1)
    def _():
        wait_chunks(slot, n_chunks)


def _dispatch(zfill, table, lpos_t, hn, n_pad, tm):
    T, D = hn.shape
    nseg = D // LANES
    nrow = tm * TOPK_IN + N_EXPERTS * RUN_ROWS
    return pl.pallas_call(
        functools.partial(_dispatch_body, tm=tm, nseg=nseg, nrow=nrow), grid=(T // tm,),
        in_specs=[pl.BlockSpec(memory_space=pltpu.SMEM),
                  pl.BlockSpec((1, 1, LANES), lambda i: (i, 0, 0), memory_space=pltpu.SMEM),
                  pl.BlockSpec((None, SUBLANES, tm), lambda i: (i, 0, 0)),
                  pl.BlockSpec((tm, D), lambda i: (i, 0))],
        out_specs=pl.BlockSpec(memory_space=pl.ANY),
        out_shape=jax.ShapeDtypeStruct((n_pad * nseg, LANES), F32),
        scratch_shapes=[pltpu.VMEM((2, nrow * nseg, LANES), F32), pltpu.VMEM((RUN_ROWS * nseg, LANES), F32),
                        pltpu.SemaphoreType.DMA((2,)), pltpu.SemaphoreType.DMA, pltpu.SMEM((1,), I32)],
        compiler_params=_params("arbitrary"), name="dispatch")(zfill, table, lpos_t, hn)


def _moe_body(blk_e_ref, n_used_ref, seg_ref, nxt_ref, x_ref, w1_hbm, w3_hbm, w2_hbm, y_ref,
              wf1, wf3, wf2, w1b, w3b, w2b, sem, *, nseg, layer):
    b = pl.program_id(0)
    R = MOE_BLOCK

    def fetch(e, slot):
        return [pltpu.make_async_copy(src.at[layer, e], dst.at[slot], sem.at[slot, k])
                for k, (src, dst) in enumerate(((w1_hbm, wf1), (w3_hbm, wf3), (w2_hbm, wf2)))]

    @pl.when(b < n_used_ref[0])
    def _():
        @pl.when((b == 0) | (blk_e_ref[b] != blk_e_ref[jnp.maximum(b - 1, 0)]))
        def _():
            e = blk_e_ref[b]
            slot = seg_ref[b] % 2

            @pl.when(b == 0)
            def _():
                for c in fetch(e, slot):
                    c.start()

            for c in fetch(e, slot):
                c.wait()
            w1b[...] = wf1[slot].astype(BF16)
            w3b[...] = wf3[slot].astype(BF16)
            w2b[...] = wf2[slot].astype(BF16)

            @pl.when(nxt_ref[b] >= 0)
            def _():
                for c in fetch(nxt_ref[b], 1 - slot):
                    c.start()

        x = jnp.concatenate([x_ref[pl.ds(s, R, stride=nseg), :] for s in range(nseg)], axis=-1).astype(BF16)
        a = jnp.dot(x, w1b[...], preferred_element_type=F32)
        g = jnp.dot(x, w3b[...], preferred_element_type=F32)
        hid = (a * jax.nn.sigmoid(a) * g).astype(BF16)
        y = jnp.dot(hid, w2b[...], preferred_element_type=F32)
        for s in range(nseg):
            y_ref[pl.ds(s, R, stride=nseg), :] = y[:, s * LANES:(s + 1) * LANES]

    @pl.when(b >= n_used_ref[0])
    def _():
        y_ref[...] = jnp.zeros_like(y_ref)


def _moe(blk_e, n_used, x_rows, w1, w3, w2, layer):
    n_blk = blk_e.shape[0]
    _, _, D, DE = w1.shape
    nseg = D // LANES
    R = MOE_BLOCK
    blk = jnp.arange(n_blk, dtype=I32)
    used = blk < n_used[0]
    change = (blk > 0) & (blk_e != jnp.roll(blk_e, 1)) & used
    seg = jnp.cumsum(change.astype(I32)).astype(I32)
    later = (blk[None, :] > blk[:, None]) & (seg[None, :] > seg[:, None]) & used[None, :]
    nxt_blk = jnp.min(jnp.where(later, blk[None, :], n_blk), axis=1)
    nxt = jnp.where(nxt_blk < n_blk, blk_e[jnp.minimum(nxt_blk, n_blk - 1)], -1).astype(I32)
    rows = lambda b, be, nu, sg, nx: (jnp.minimum(b, nu[0] - 1), 0)
    grid_spec = pltpu.PrefetchScalarGridSpec(
        num_scalar_prefetch=4, grid=(n_blk,),
        in_specs=[pl.BlockSpec((R * nseg, LANES), rows),
                  pl.BlockSpec(memory_space=pl.ANY), pl.BlockSpec(memory_space=pl.ANY),
                  pl.BlockSpec(memory_space=pl.ANY)],
        out_specs=pl.BlockSpec((R * nseg, LANES), lambda b, be, nu, sg, nx: (b, 0)),
        scratch_shapes=[pltpu.VMEM((2, D, DE), F32), pltpu.VMEM((2, D, DE), F32), pltpu.VMEM((2, DE, D), F32),
                        pltpu.VMEM((D, DE), BF16), pltpu.VMEM((D, DE), BF16), pltpu.VMEM((DE, D), BF16),
                        pltpu.SemaphoreType.DMA((2, 3))])
    return pl.pallas_call(
        functools.partial(_moe_body, nseg=nseg, layer=layer), grid_spec=grid_spec,
        out_shape=jax.ShapeDtypeStruct(x_rows.shape, F32),
        compiler_params=_params("arbitrary"), name="moe")(blk_e, n_used, seg, nxt, x_rows, w1, w3, w2)


def _combine_body(tab_ref, tabn_ref, h_ref, rf_ref, lpos_ref, y_hbm, o_ref, ybuf, sem, *, tm, nseg, nrow):
    i = pl.program_id(0)
    slot = i % 2
    step = RUN_ROWS * nseg

    def start_chunks(tab, sl):
        def start(c, carry):
            pltpu.make_async_copy(
                y_hbm.at[pl.ds(pl.multiple_of(tab[0, 0, c] * nseg, nseg), step), :],
                ybuf.at[sl, pl.ds(pl.multiple_of(c * step, step), step), :], sem.at[sl]).start()
            return carry
        lax.fori_loop(0, tab[0, 0, LANES - 1], start, 0)

    @pl.when(i == 0)
    def _():
        ybuf[...] = jnp.zeros_like(ybuf)
        start_chunks(tab_ref, 0)

    @pl.when(i + 1 < pl.num_programs(0))
    def _():
        start_chunks(tabn_ref, 1 - slot)

    col = lax.broadcasted_iota(I32, (tm, nrow), 1)
    sel = jnp.zeros((tm, nrow), F32)
    for s in range(TOPK_IN):
        sel = sel + jnp.where(col == lpos_ref[:, s:s + 1], rf_ref[:, s:s + 1], 0.0)

    def wait(c, carry):
        pltpu.make_async_copy(y_hbm.at[pl.ds(0, step), :], ybuf.at[slot, pl.ds(0, step), :],
                              sem.at[slot]).wait()
        return carry

    lax.fori_loop(0, tab_ref[0, 0, LANES - 1], wait, 0)
    y = jnp.concatenate([ybuf[slot, pl.ds(k, nrow, stride=nseg), :] for k in range(nseg)], axis=-1)
    o_ref[...] = h_ref[...] + jnp.dot(sel.astype(BF16), y.astype(BF16), preferred_element_type=F32)


def _combine(table, h2d, rf, lpos, y_rows, tm):
    T, D = h2d.shape
    nseg = D // LANES
    nt = T // tm
    nrow = tm * TOPK_IN + N_EXPERTS * RUN_ROWS
    return pl.pallas_call(
        functools.partial(_combine_body, tm=tm, nseg=nseg, nrow=nrow), grid=(nt,),
        in_specs=[pl.BlockSpec((1, 1, LANES), lambda i: (i, 0, 0), memory_space=pltpu.SMEM),
                  pl.BlockSpec((1, 1, LANES), lambda i: (jnp.minimum(i + 1, nt - 1), 0, 0),
                               memory_space=pltpu.SMEM),
                  pl.BlockSpec((tm, D), lambda i: (i, 0)),
                  pl.BlockSpec((tm, LANES), lambda i: (i, 0)),
                  pl.BlockSpec((tm, LANES), lambda i: (i, 0)),
                  pl.BlockSpec(memory_space=pl.ANY)],
        out_specs=pl.BlockSpec((tm, D), lambda i: (i, 0)),
        out_shape=jax.ShapeDtypeStruct((T, D), F32),
        scratch_shapes=[pltpu.VMEM((2, nrow * nseg, LANES), F32), pltpu.SemaphoreType.DMA((2,))],
        compiler_params=_params("arbitrary"), name="combine")(table, table, h2d, rf, lpos, y_rows)


def _pad_lanes(a, lane0, rows=1):
    out = jnp.zeros((rows, LANES), a.dtype)
    return out.at[:, lane0:lane0 + a.shape[-1]].set(a.reshape(rows, -1))


def _layer(h2d, B, S, p):
    T, D = h2d.shape
    tile = lambda a, n: jnp.tile(a.reshape(1, -1), (1, n))

    offs = np.cumsum([0, GROUP_W, GROUP_W, GROUP_W, N_HEADS, GROUP_W, GROUP_W,
                      GROUP_W, GROUP_W, GROUP_W, N_HEADS, N_HEADS, GROUP_W, GROUP_W])
    seg = lambda k: p['w_in'][:, offs[k]:offs[k + 1]]
    wb = jnp.concatenate([seg(0), seg(1), seg(2), seg(4), seg(5), seg(6), seg(7), seg(8), seg(11), seg(12)],
                         axis=1).astype(BF16)
    ws = jnp.zeros((D, LANES), F32)
    ws = ws.at[:, LANE_FOX:LANE_FOX + N_HEADS].set(seg(3))
    ws = ws.at[:, LANE_DECAY:LANE_DECAY + N_HEADS].set(seg(9))
    ws = ws.at[:, LANE_BETA:LANE_BETA + N_HEADS].set(seg(10)).astype(BF16)
    gate_prm = jnp.concatenate([_pad_lanes(p['fox_f_bias'], LANE_FOX), _pad_lanes(p['gdn_dt_bias'], LANE_DECAY),
                                _pad_lanes(p['gdn_a_log'], LANE_DECAY), jnp.zeros((SUBLANES - 3, LANES), F32)], axis=0)

    big, small = _inproj(h2d, p['attn_norm_g'].reshape(1, D), wb, ws, tm=min(512, T))
    big3 = big.reshape(B, S, N_BIG_COLS * GROUP_W)
    ts = min(512, S)
    gcol, grow = _gates(small.reshape(B, S, LANES), gate_prm, ts)

    tq = min(256, S)
    ya = _fox(big3, gcol, tile(p['fox_qn_g'], N_HEADS), tile(p['fox_kn_g'], N_HEADS),
              p['fox_out_g'].reshape(1, GROUP_W), tq)

    bst = jnp.repeat(p['gmlp_bs'].T, HEAD_DIM, axis=1)
    yb = _gmlp(big, p['gmlp_ln_g'].reshape(1, -1), p['gmlp_ln_b'].reshape(1, -1), p['gmlp_ws'], bst,
               p['gmlp_out_g'].reshape(1, -1), tm=min(512, T))

    conv_w3 = p['gdn_conv_w'].reshape(CONV_K, 3, GROUP_W).transpose(1, 0, 2)
    grow4 = grow.reshape(B, 2 * SUBLANES, S // CHUNK, CHUNK).transpose(0, 2, 1, 3)
    yc = _gdn(big3, conv_w3, gcol, grow4, p['gdn_norm_g'].reshape(1, HEAD_DIM), ts)

    wbd = jax.scipy.linalg.block_diag(*[p['pool_w'][g] for g in range(len(POOL_WINDOWS))]).astype(BF16)
    yd = _pool(big3, wbd, p['pool_scale'].reshape(1, -1), p['pool_out_g'].reshape(1, -1), ts)

    rw = jnp.zeros((D, LANES), F32).at[:, :N_GROUPS].set(p['router_g_w'])
    rw = rw.at[:, N_GROUPS:N_GROUPS + N_EXPERTS].set(p['router_e_w'])
    rw = jnp.concatenate(_split(rw), axis=1)
    rb = jnp.zeros((1, LANES), F32).at[0, :N_GROUPS].set(p['router_g_b'])
    rb = rb.at[0, N_GROUPS:N_GROUPS + N_EXPERTS].set(p['router_e_b'])
    flat = lambda a: a.reshape(T, GROUP_W)
    tmd = min(256, T)
    h_new, hn_rows, ri, rf, cnt = _outproj(flat(ya), yb, flat(yc), flat(yd), h2d, p['w_out'].astype(BF16),
                                           p['ffn_norm_g'].reshape(1, D), rw, rb, tm=min(512, T), tr=tmd)
    cnt_tile = cnt.reshape(T // tmd, SUBLANES, LANES)[:, 0, :N_EXPERTS]
    lpos, table, zfill, blk_e, n_used, n_blk = _dispatch_plan(
        ri[:, :TOPK_IN], ri[:, TOPK_IN:2 * TOPK_IN], cnt_tile, tmd)
    lpos_t = jnp.zeros((T // tmd, SUBLANES, tmd), I32).at[:, :TOPK_IN, :].set(
        lpos.reshape(T // tmd, tmd, TOPK_IN).transpose(0, 2, 1))
    x_rows = _dispatch(zfill, table, lpos_t, hn_rows, n_blk * MOE_BLOCK, tmd)
    y_rows = _moe(blk_e, n_used, x_rows, p['moe_w1'], p['moe_w3'], p['moe_w2'], p['layer'])
    lpos_pad = jnp.zeros((T, LANES), I32).at[:, :TOPK_IN].set(lpos)
    return _combine(table, h_new, rf, lpos_pad, y_rows, tmd)


def kernel(x, attn_norm_g, w_in, w_out, fox_f_bias, fox_qn_g, fox_kn_g, fox_out_g, gmlp_ln_g, gmlp_ln_b, gmlp_ws, gmlp_bs, gmlp_out_g, gdn_conv_w, gdn_a_log, gdn_dt_bias, gdn_norm_g, pool_w, pool_scale, pool_out_g, ffn_norm_g, router_g_w, router_g_b, router_e_w, router_e_b, moe_w1, moe_w3, moe_w2):
    B, S, D = x.shape
    names = ('attn_norm_g', 'w_in', 'w_out', 'fox_f_bias', 'fox_qn_g', 'fox_kn_g', 'fox_out_g', 'gmlp_ln_g',
             'gmlp_ln_b', 'gmlp_ws', 'gmlp_bs', 'gmlp_out_g', 'gdn_conv_w', 'gdn_a_log', 'gdn_dt_bias',
             'gdn_norm_g', 'pool_w', 'pool_scale', 'pool_out_g', 'ffn_norm_g', 'router_g_w', 'router_g_b',
             'router_e_w', 'router_e_b', 'moe_w1', 'moe_w3', 'moe_w2')
    vals = (attn_norm_g, w_in, w_out, fox_f_bias, fox_qn_g, fox_kn_g, fox_out_g, gmlp_ln_g, gmlp_ln_b, gmlp_ws,
            gmlp_bs, gmlp_out_g, gdn_conv_w, gdn_a_log, gdn_dt_bias, gdn_norm_g, pool_w, pool_scale, pool_out_g,
            ffn_norm_g, router_g_w, router_g_b, router_e_w, router_e_b, moe_w1, moe_w3, moe_w2)
    h = x.reshape(B * S, D)
    stacked = ('moe_w1', 'moe_w3', 'moe_w2')
    for l in range(w_in.shape[0]):
        p = {n: (v if n in stacked else v[l]) for n, v in zip(names, vals)}
        p['layer'] = l
        h = _layer(h, B, S, p)
    return h.reshape(B, S, D)
```

```python
import functools

import jax
import jax.numpy as jnp
import numpy as np
from jax import lax
from jax.experimental import pallas as pl
from jax.experimental.pallas import tpu as pltpu

F32 = jnp.float32
BF16 = jnp.bfloat16
I32 = jnp.int32

EPS = 1e-6
HEAD_DIM = 64
GROUP_W = 256
N_HEADS = GROUP_W // HEAD_DIM
CHUNK = 64
GMLP_LEN = 128
CONV_K = 4
POOL_WINDOWS = (2, 4, 8, 16)
N_GROUPS = 4
EXPERTS_PER_GROUP = 8
N_EXPERTS = N_GROUPS * EXPERTS_PER_GROUP
TOPK_IN = 2
MOE_BLOCK = 256
FOX_VT_ROWS = HEAD_DIM + 16
FOX_BOUND_LOG2 = 40.0
FOX_SKIP_LOG2 = 160.0
RUN_ROWS = 8
LANES = 128
SUBLANES = 8
VMEM_LIMIT = 56 * 1024 * 1024

COL_FQ, COL_FK, COL_FV, COL_GU, COL_GV, COL_DQ, COL_DK, COL_DV, COL_DG, COL_PZ = range(10)
N_BIG_COLS = 10
LANE_FOX, LANE_DECAY, LANE_BETA = 0, 4, 8


def _params(*sem):
    return pltpu.CompilerParams(dimension_semantics=sem, vmem_limit_bytes=VMEM_LIMIT)


def _head_ones():
    r = lax.broadcasted_iota(I32, (GROUP_W, GROUP_W), 0) // HEAD_DIM
    c = lax.broadcasted_iota(I32, (GROUP_W, GROUP_W), 1) // HEAD_DIM
    return (r == c).astype(BF16)


def _head_sums(x, ones_bd):
    hi = x.astype(BF16)
    lo = (x - hi.astype(F32)).astype(BF16)
    return (jnp.dot(hi, ones_bd, preferred_element_type=F32)
            + jnp.dot(lo, ones_bd, preferred_element_type=F32))


def _rms(x, g):
    return x * lax.rsqrt(jnp.mean(x * x, axis=-1, keepdims=True) + EPS) * g


def _mm(a, b):
    return jnp.dot(a.astype(BF16), b.astype(BF16), preferred_element_type=F32)


def _mm_nt(a, b):
    return lax.dot_general(a.astype(BF16), b.astype(BF16), (((1,), (1,)), ((), ())),
                           preferred_element_type=F32)


def _mm_tn(a, b):
    return lax.dot_general(a.astype(BF16), b.astype(BF16), (((0,), (0,)), ((), ())),
                           preferred_element_type=F32)


def _split(a):
    hi = a.astype(BF16)
    return hi, (a - hi.astype(F32)).astype(BF16)


def _mm3(a, b):
    ah, al = _split(a)
    bh, bl = _split(b)
    d = functools.partial(jnp.dot, preferred_element_type=F32)
    return d(ah, bh) + (d(ah, bl) + d(al, bh))


def _inproj_body(x_ref, g_ref, w_ref, big_ref, small_ref):
    xn = _rms(x_ref[...], g_ref[...]).astype(BF16)
    nb = big_ref.shape[1]
    big_ref[...] = jnp.dot(xn, w_ref[:, :nb], preferred_element_type=F32)
    small_ref[...] = jnp.dot(xn, w_ref[:, nb:], preferred_element_type=F32)


def _inproj(x2d, g, w, tm):
    T, D = x2d.shape
    nb = w.shape[1] - LANES
    return pl.pallas_call(
        _inproj_body, grid=(T // tm,),
        in_specs=[pl.BlockSpec((tm, D), lambda i: (i, 0)),
                  pl.BlockSpec((1, D), lambda i: (0, 0)),
                  pl.BlockSpec((D, nb + LANES), lambda i: (0, 0))],
        out_specs=[pl.BlockSpec((tm, nb), lambda i: (i, 0)),
                   pl.BlockSpec((tm, LANES), lambda i: (i, 0))],
        out_shape=[jax.ShapeDtypeStruct((T, nb), F32), jax.ShapeDtypeStruct((T, LANES), F32)],
        compiler_params=_params("parallel"), name="inproj")(x2d, g, w)


def _gates_body(sm_ref, p_ref, col_ref, row_ref, carry_ref, *, ts):
    @pl.when(pl.program_id(1) == 0)
    def _():
        carry_ref[...] = jnp.zeros_like(carry_ref)

    x = sm_ref[...]
    lane = lax.broadcasted_iota(I32, (ts, LANES), 1)
    is_fox = lane < LANE_DECAY
    is_dec = (lane >= LANE_DECAY) & (lane < LANE_BETA)
    is_beta = (lane >= LANE_BETA) & (lane < LANE_BETA + N_HEADS)
    logf = jax.nn.log_sigmoid(x + p_ref[0:1, :])
    g = -jnp.exp(p_ref[2:3, :]) * jax.nn.softplus(x + p_ref[1:2, :])
    beta = jax.nn.sigmoid(x)
    r = lax.broadcasted_iota(I32, (ts, ts), 0)
    c = lax.broadcasted_iota(I32, (ts, ts), 1)
    tri_full = (r >= c).astype(BF16)
    tri_chunk = ((r >= c) & (r // CHUNK == c // CHUNK)).astype(BF16)
    vals = jnp.where(is_fox, logf, jnp.where(is_dec, g, 0.0))
    hi = vals.astype(BF16)
    mid = (vals - hi.astype(F32)).astype(BF16)
    lo = (vals - hi.astype(F32) - mid.astype(F32)).astype(BF16)
    parts = jnp.concatenate([hi, mid, lo], axis=1)

    def tri_sum(tri):
        t = jnp.dot(tri, parts, preferred_element_type=F32)
        return t[:, :LANES] + (t[:, LANES:2 * LANES] + t[:, 2 * LANES:])

    cf = tri_sum(tri_full) + carry_ref[...]
    cg = tri_sum(tri_chunk)
    carry_ref[...] = cf[ts - 1:ts, :]
    out = jnp.where(is_fox, cf, jnp.where(is_dec, cg, jnp.where(is_beta, beta, 0.0)))
    col_ref[...] = out
    row_ref[...] = out.T[:2 * SUBLANES, :]


def _gates(small3, prm, ts):
    B, S, _ = small3.shape
    return pl.pallas_call(
        functools.partial(_gates_body, ts=ts), grid=(B, S // ts),
        in_specs=[pl.BlockSpec((None, ts, LANES), lambda b, j: (b, j, 0)),
                  pl.BlockSpec((SUBLANES, LANES), lambda b, j: (0, 0))],
        out_specs=[pl.BlockSpec((None, ts, LANES), lambda b, j: (b, j, 0)),
                   pl.BlockSpec((None, 2 * SUBLANES, ts), lambda b, j: (b, 0, j))],
        out_shape=[jax.ShapeDtypeStruct((B, S, LANES), F32),
                   jax.ShapeDtypeStruct((B, 2 * SUBLANES, S), F32)],
        scratch_shapes=[pltpu.VMEM((1, LANES), F32)],
        compiler_params=_params("parallel", "arbitrary"), name="gates")(small3, prm)


def _split3_lanes(x, lane, lane0):
    hi = x.astype(BF16).astype(F32)
    mid = (x - hi).astype(BF16).astype(F32)
    lo = (x - hi - mid).astype(BF16).astype(F32)
    return jnp.where(lane == lane0, hi, jnp.where(lane == lane0 + 1, mid, jnp.where(lane == lane0 + 2, lo, 0.0)))


def _fox_body(jstart_ref, q_ref, k_ref, v_ref, ccol_ref, qg_ref, kg_ref, og_ref, o_ref,
              kn_scr, vt_scr, q_scr, m_scr, l_scr, acc_scr, s_scr, kmax_scr, *, tq, nk):
    b = pl.program_id(0)
    i = pl.program_id(1)
    ones_bd = _head_ones()
    lane = lax.broadcasted_iota(I32, (tq, LANES), 1)
    log2e = 1.0 / np.log(2.0)
    c_lane, r_lane = HEAD_DIM, HEAD_DIM + 3

    def head_norm(x, g):
        ss = _head_sums(x * x, ones_bd)
        return x * lax.rsqrt(ss * (1.0 / HEAD_DIM) + EPS) * g

    def head_tile(x, h, extra):
        pair = x[:, (h // 2) * LANES:(h // 2 + 1) * LANES]
        if h % 2:
            pair = pltpu.roll(pair, HEAD_DIM, 1)
        return jnp.where(lane < HEAD_DIM, pair, extra).astype(BF16)

    def sq_norms(x):
        xr = x.astype(BF16).astype(F32)
        return _head_sums(xr * xr, ones_bd)

    @pl.when(i == 0)
    def _():
        k_ones = jnp.where((lane >= r_lane) & (lane < r_lane + 3), 1.0, 0.0)
        vt_tail = (lax.broadcasted_iota(I32, (FOX_VT_ROWS - HEAD_DIM, tq), 0) == 0).astype(BF16)
        kmax = jnp.zeros((1, GROUP_W), F32)
        for c in range(nk):
            rows = slice(c * tq, (c + 1) * tq)
            kc = head_norm(k_ref[rows, :], kg_ref[...])
            kmax = jnp.maximum(kmax, jnp.max(sq_norms(kc), axis=0, keepdims=True))
            vt = v_ref[rows, :].T.astype(BF16)
            cc = ccol_ref[rows, :] * (-log2e)
            for h in range(N_HEADS):
                extra = _split3_lanes(cc[:, LANE_FOX + h:LANE_FOX + h + 1], lane, c_lane) + k_ones
                kn_scr[h, c] = head_tile(kc, h, extra)
                vt_scr[h, c] = jnp.concatenate([vt[h * HEAD_DIM:(h + 1) * HEAD_DIM, :], vt_tail], axis=0)
        kmax_scr[...] = kmax

    qn = head_norm(q_ref[...], qg_ref[...]) * (HEAD_DIM ** -0.5 * log2e)
    bound = jnp.sqrt(sq_norms(qn) * kmax_scr[...]) * 1.001
    bounded = jnp.max(bound) <= FOX_BOUND_LOG2
    c_i = ccol_ref[pl.ds(pl.multiple_of(i * tq, tq), tq), :] * log2e
    q_ones = jnp.where((lane >= c_lane) & (lane < c_lane + 3), 1.0, 0.0)
    for h in range(N_HEADS):
        r_i = bound[:, h * HEAD_DIM:h * HEAD_DIM + 1] - c_i[:, LANE_FOX + h:LANE_FOX + h + 1]
        neg_r = jnp.where(bounded, -r_i, 0.0)
        q_scr[h] = head_tile(qn, h, q_ones + _split3_lanes(neg_r, lane, r_lane))
    acc_scr[...] = jnp.zeros_like(acc_scr)
    causal = (lax.broadcasted_iota(I32, (tq, tq), 0) <= lax.broadcasted_iota(I32, (tq, tq), 1))

    heads = range(N_HEADS)

    def scores(j):
        return [lax.dot_general(kn_scr[h, j], q_scr[h], (((1,), (1,)), ((), ())),
                                preferred_element_type=F32) for h in heads]

    def stash(s):
        for h in heads:
            s_scr[h] = s[h]

    def absorb_general(j, masked):
        p, alpha = [], []
        for h in heads:
            s = s_scr[h]
            if masked:
                s = jnp.where(causal, s, -jnp.inf)
            m_old = m_scr[h]
            m_new = jnp.maximum(m_old, jnp.max(s, axis=0, keepdims=True))
            alpha.append(jnp.exp2(m_old - m_new))
            ph = jnp.exp2(s - m_new)
            l_scr[h] = alpha[h] * l_scr[h] + jnp.sum(ph, axis=0, keepdims=True)
            m_scr[h] = m_new
            p.append(ph.astype(BF16))
        pv = [jnp.dot(vt_scr[h, j], p[h], preferred_element_type=F32) for h in heads]
        for h in heads:
            acc_scr[h] = alpha[h] * acc_scr[h] + pv[h]

    def absorb_bounded(j, masked):
        p = []
        for h in heads:
            s = s_scr[h]
            if masked:
                s = jnp.where(causal, s, -jnp.inf)
            p.append(jnp.exp2(s).astype(BF16))
        pv = [jnp.dot(vt_scr[h, j], p[h], preferred_element_type=F32) for h in heads]
        for h in heads:
            acc_scr[h] += pv[h]

    def run(absorb, j0):
        stash(scores(j0))

        def body(j, c):
            s_next = scores(j + 1)
            absorb(j, False)
            stash(s_next)
            return c

        lax.fori_loop(j0, i, body, 0)
        absorb(i, True)

    @pl.when(bounded)
    def _():
        run(absorb_bounded, jstart_ref[b * nk + i])

    @pl.when(jnp.logical_not(bounded))
    def _():
        m_scr[...] = jnp.full_like(m_scr, -jnp.inf)
        l_scr[...] = jnp.zeros_like(l_scr)
        run(absorb_general, 0)
        for h in heads:
            acc_scr[h, HEAD_DIM:HEAD_DIM + 1, :] = l_scr[h]

    o_t = jnp.concatenate([acc_scr[h, :HEAD_DIM, :] / acc_scr[h, HEAD_DIM:HEAD_DIM + 1, :] for h in heads],
                          axis=0)
    o_ref[...] = _rms(o_t.T, og_ref[...])


def _fox_first_block(gcol, tq):
    B, S, _ = gcol.shape
    nk = S // tq
    c2 = gcol[:, :, LANE_FOX:LANE_FOX + N_HEADS] * (1.0 / np.log(2.0))
    first = c2[:, 0::tq, :]
    last = c2[:, tq - 1::tq, :]
    dead = (first[:, :, None, :] - last[:, None, :, :]) < -FOX_SKIP_LOG2
    dead = dead & (jnp.arange(nk)[None, :, None, None] > jnp.arange(nk)[None, None, :, None])
    return jnp.min(jnp.sum(dead, axis=2), axis=-1).astype(I32).reshape(B * nk)


def _fox(big3, gcol, qg, kg, og, tq):
    B, S, _ = big3.shape
    nk = S // tq
    row = pl.BlockSpec((1, GROUP_W), lambda b, i, js: (0, 0))
    grid_spec = pltpu.PrefetchScalarGridSpec(
        num_scalar_prefetch=1, grid=(B, nk),
        in_specs=[pl.BlockSpec((None, tq, GROUP_W), lambda b, i, js: (b, i, COL_FQ)),
                  pl.BlockSpec((None, S, GROUP_W), lambda b, i, js: (b, 0, COL_FK)),
                  pl.BlockSpec((None, S, GROUP_W), lambda b, i, js: (b, 0, COL_FV)),
                  pl.BlockSpec((None, S, LANES), lambda b, i, js: (b, 0, 0)),
                  row, row, row],
        out_specs=pl.BlockSpec((None, tq, GROUP_W), lambda b, i, js: (b, i, 0)),
        scratch_shapes=[pltpu.VMEM((N_HEADS, nk, tq, LANES), BF16),
                        pltpu.VMEM((N_HEADS, nk, FOX_VT_ROWS, tq), BF16),
                        pltpu.VMEM((N_HEADS, tq, LANES), BF16),
                        pltpu.VMEM((N_HEADS, 1, tq), F32),
                        pltpu.VMEM((N_HEADS, 1, tq), F32),
                        pltpu.VMEM((N_HEADS, FOX_VT_ROWS, tq), F32),
                        pltpu.VMEM((N_HEADS, tq, tq), F32),
                        pltpu.VMEM((1, GROUP_W), F32)])
    return pl.pallas_call(
        functools.partial(_fox_body, tq=tq, nk=nk), grid_spec=grid_spec,
        out_shape=jax.ShapeDtypeStruct((B, S, GROUP_W), F32),
        compiler_params=_params("parallel", "arbitrary"), name="fox")(
            _fox_first_block(gcol, tq), big3, big3, big3, gcol, qg, kg, og)


def _gelu(x):
    return 0.5 * x * (1.0 + lax.erf(x * (2.0 ** -0.5)))


def _gmlp_body(u_ref, v_ref, lg_ref, lb_ref, ws_ref, bst_ref, og_ref, o_ref, *, nwin):
    L = GMLP_LEN
    r = lax.broadcasted_iota(I32, (L, L), 0) // CHUNK
    c = lax.broadcasted_iota(I32, (L, L), 1) // CHUNK
    mask = r >= c
    ws = [jnp.where(mask, ws_ref[h], 0.0).astype(BF16) for h in range(N_HEADS)]
    for n in range(nwin):
        u = _gelu(u_ref[n * L:(n + 1) * L, :])
        v = _gelu(v_ref[n * L:(n + 1) * L, :])
        mu = jnp.mean(v, axis=-1, keepdims=True)
        vc = v - mu
        var = jnp.mean(vc * vc, axis=-1, keepdims=True)
        vn = (vc * lax.rsqrt(var + EPS) * lg_ref[...] + lb_ref[...]).astype(BF16)
        mixed = jnp.concatenate(
            [jnp.dot(ws[h], vn[:, h * HEAD_DIM:(h + 1) * HEAD_DIM], preferred_element_type=F32)
             for h in range(N_HEADS)], axis=-1) + bst_ref[...]
        o_ref[n * L:(n + 1) * L, :] = _rms(u * mixed, og_ref[...])


def _gmlp(big, lg, lb, ws, bst, og, tm):
    T = big.shape[0]
    row = pl.BlockSpec((1, GROUP_W), lambda i: (0, 0))
    return pl.pallas_call(
        functools.partial(_gmlp_body, nwin=tm // GMLP_LEN), grid=(T // tm,),
        in_specs=[pl.BlockSpec((tm, GROUP_W), lambda i: (i, COL_GU)),
                  pl.BlockSpec((tm, GROUP_W), lambda i: (i, COL_GV)),
                  row, row,
                  pl.BlockSpec((N_HEADS, GMLP_LEN, GMLP_LEN), lambda i: (0, 0, 0)),
                  pl.BlockSpec((GMLP_LEN, GROUP_W), lambda i: (0, 0)),
                  row],
        out_specs=pl.BlockSpec((tm, GROUP_W), lambda i: (i, 0)),
        out_shape=jax.ShapeDtypeStruct((T, GROUP_W), F32),
        compiler_params=_params("parallel"), name="gmlp")(big, big, lg, lb, ws, bst, og)


def _gdn_body(q_ref, k_ref, v_ref, w_ref, gate_ref, gcol_ref, grow_ref, ng_ref, o_ref,
              s_scr, u_scr, wq_scr, a_scr, kd_scr, dl_scr, t_scr, p_scr, rhs_scr, halo_scr, qkv_scr, *, nchunk):
    C = CHUNK
    ts = nchunk * C
    first = pl.program_id(1) == 0

    @pl.when(first)
    def _():
        s_scr[...] = jnp.zeros_like(s_scr)

    ones_bd = _head_ones()
    for a, x_ref in enumerate((q_ref, k_ref, v_ref)):
        w = w_ref[a]
        x = x_ref[...]
        xx = jnp.concatenate([jnp.where(first, 0.0, halo_scr[a]), x], axis=0)
        halo_scr[a] = x[ts - SUBLANES:, :]
        y = w[CONV_K - 1:CONV_K, :] * x
        for j in range(CONV_K - 1):
            y = y + w[j:j + 1, :] * pltpu.roll(xx, CONV_K - 1 - j, 0)[SUBLANES:, :]
        y = y * jax.nn.sigmoid(y)
        if a < 2:
            y = y * lax.rsqrt(_head_sums(y * y, ones_bd) + EPS)
        qkv_scr[a] = y * (HEAD_DIM ** -0.5) if a == 0 else y

    r = lax.broadcasted_iota(I32, (C, C), 0)
    c = lax.broadcasted_iota(I32, (C, C), 1)
    tri = r >= c
    strict = r > c
    eye = (r == c).astype(F32)

    items = [(n, h) for n in range(nchunk) for h in range(N_HEADS)]
    mmb = functools.partial(jnp.dot, preferred_element_type=F32)
    for it, (n, h) in enumerate(items):
        rows = slice(n * C, (n + 1) * C)
        sl = slice(h * HEAD_DIM, (h + 1) * HEAD_DIM)
        q, k, v = qkv_scr[0, rows, sl], qkv_scr[1, rows, sl], qkv_scr[2, rows, sl]
        gc = gcol_ref[rows, LANE_DECAY + h:LANE_DECAY + h + 1]
        gr = grow_ref[n, LANE_DECAY + h:LANE_DECAY + h + 1, :]
        beta = gcol_ref[rows, LANE_BETA + h:LANE_BETA + h + 1]
        decay = jnp.exp(jnp.where(tri, gc - gr, -jnp.inf))
        kb = k * beta
        kk = _mm_nt(jnp.concatenate([kb, q], axis=0), k)
        x = jnp.where(strict, -(kk[:C] * decay), 0.0)
        t_scr[it] = eye + x
        p_scr[it] = x.astype(BF16)
        eg = jnp.exp(gc)
        g_last = gc[C - 1:C, :]
        rhs_scr[it] = jnp.concatenate([v * beta, kb * eg], axis=1).astype(BF16)
        wq_scr[n, h, C:, :] = (q * eg).astype(BF16)
        a_scr[n, h] = jnp.where(tri, kk[C:] * decay, 0.0).astype(BF16)
        kd_scr[n, h] = (k * jnp.exp(g_last - gc)).astype(BF16)
        dl_scr[n, h] = jnp.broadcast_to(jnp.exp(g_last), (1, HEAD_DIM))
    for _ in range(5):
        for it in range(len(items)):
            p = p_scr[it]
            p_scr[it] = mmb(p, p).astype(BF16)
        for it in range(len(items)):
            t = t_scr[it]
            t_scr[it] = t + mmb(t.astype(BF16), p_scr[it])
    for it, (n, h) in enumerate(items):
        uw = mmb(t_scr[it].astype(BF16), rhs_scr[it])
        u_scr[n, h] = uw[:, :HEAD_DIM]
        wq_scr[n, h, :C, :] = uw[:, HEAD_DIM:].astype(BF16)

    heads = range(N_HEADS)
    state = [s_scr[h] for h in heads]
    for n in range(nchunk):
        rows = slice(n * C, (n + 1) * C)
        ws = [mmb(wq_scr[n, h], state[h].astype(BF16)) for h in heads]
        vb = [(u_scr[n, h] - ws[h][:C]).astype(BF16) for h in heads]
        o = [ws[h][C:] + mmb(a_scr[n, h], vb[h]) for h in heads]
        state = [state[h] * dl_scr[n, h] + lax.dot_general(
            kd_scr[n, h], vb[h], (((0,), (0,)), ((), ())), preferred_element_type=F32) for h in heads]
        gate = gate_ref[rows, :]
        y = jnp.concatenate([_rms(o[h], ng_ref[...]) for h in heads], axis=-1)
        o_ref[rows, :] = y * (gate * jax.nn.sigmoid(gate))
    for h in heads:
        s_scr[h] = state[h]


def _gdn(big3, conv_w3, gcol, grow4, ng, ts):
    B, S, _ = big3.shape
    nchunk = ts // CHUNK
    col = lambda c: pl.BlockSpec((None, ts, GROUP_W), lambda b, i: (b, i, c))
    blk = col(0)
    return pl.pallas_call(
        functools.partial(_gdn_body, nchunk=nchunk), grid=(B, S // ts),
        in_specs=[col(COL_DQ), col(COL_DK), col(COL_DV),
                  pl.BlockSpec((3, CONV_K, GROUP_W), lambda b, i: (0, 0, 0)),
                  col(COL_DG),
                  pl.BlockSpec((None, ts, LANES), lambda b, i: (b, i, 0)),
                  pl.BlockSpec((None, nchunk, 2 * SUBLANES, CHUNK), lambda b, i: (b, i, 0, 0)),
                  pl.BlockSpec((1, HEAD_DIM), lambda b, i: (0, 0))],
        out_specs=blk, out_shape=jax.ShapeDtypeStruct((B, S, GROUP_W), F32),
        scratch_shapes=[pltpu.VMEM((N_HEADS, HEAD_DIM, HEAD_DIM), F32),
                        pltpu.VMEM((nchunk, N_HEADS, CHUNK, HEAD_DIM), F32),
                        pltpu.VMEM((nchunk, N_HEADS, 2 * CHUNK, HEAD_DIM), BF16),
                        pltpu.VMEM((nchunk, N_HEADS, CHUNK, CHUNK), BF16),
                        pltpu.VMEM((nchunk, N_HEADS, CHUNK, HEAD_DIM), BF16),
                        pltpu.VMEM((nchunk, N_HEADS, 1, HEAD_DIM), F32),
                        pltpu.VMEM((nchunk * N_HEADS, CHUNK, CHUNK), F32),
                        pltpu.VMEM((nchunk * N_HEADS, CHUNK, CHUNK), BF16),
                        pltpu.VMEM((nchunk * N_HEADS, CHUNK, 2 * HEAD_DIM), BF16),
                        pltpu.VMEM((3, SUBLANES, GROUP_W), F32),
                        pltpu.VMEM((3, ts, GROUP_W), F32)],
        compiler_params=_params("parallel", "arbitrary"), name="gdn")(
            big3, big3, big3, conv_w3, big3, gcol, grow4, ng)


def _pool_body(z_ref, halo_ref, w_ref, sc_ref, og_ref, o_ref, *, ts):
    i = pl.program_id(1)
    hr = 2 * SUBLANES
    z = z_ref[...]
    halo = jnp.where(i == 0, 0.0, halo_ref[...])
    s1 = jnp.concatenate([halo, z], axis=0)
    s2 = s1 + pltpu.roll(s1, 1, 0)
    s4 = s2 + pltpu.roll(s2, 2, 0)
    s8 = s4 + pltpu.roll(s4, 4, 0)
    s16 = s8 + pltpu.roll(s8, 8, 0)
    grp = lax.broadcasted_iota(I32, (ts, GROUP_W), 1) // (GROUP_W // len(POOL_WINDOWS))
    t = lax.broadcasted_iota(I32, (ts, GROUP_W), 0) + i * ts
    total = jnp.where(grp == 0, s2[hr:], jnp.where(grp == 1, s4[hr:], jnp.where(grp == 2, s8[hr:], s16[hr:])))
    win = jnp.where(grp == 0, POOL_WINDOWS[0], jnp.where(grp == 1, POOL_WINDOWS[1],
                    jnp.where(grp == 2, POOL_WINDOWS[2], POOL_WINDOWS[3])))
    pooled = total / jnp.minimum(t + 1, win).astype(F32)
    y = _mm(pooled - z, w_ref[...]) * sc_ref[...]
    o_ref[...] = _rms(y, og_ref[...])


def _pool(big3, wbd, sc, og, ts):
    B, S, _ = big3.shape
    hr = 2 * SUBLANES
    hb = ts // hr
    row = pl.BlockSpec((1, GROUP_W), lambda b, i: (0, 0))
    return pl.pallas_call(
        functools.partial(_pool_body, ts=ts), grid=(B, S // ts),
        in_specs=[pl.BlockSpec((None, ts, GROUP_W), lambda b, i: (b, i, COL_PZ)),
                  pl.BlockSpec((None, hr, GROUP_W), lambda b, i: (b, jnp.maximum(i * hb - 1, 0), COL_PZ)),
                  pl.BlockSpec((GROUP_W, GROUP_W), lambda b, i: (0, 0)), row, row],
        out_specs=pl.BlockSpec((None, ts, GROUP_W), lambda b, i: (b, i, 0)),
        out_shape=jax.ShapeDtypeStruct((B, S, GROUP_W), F32),
        compiler_params=_params("parallel", "parallel"), name="pool")(big3, big3, wbd, sc, og)


def _outproj_body(ya_ref, yb_ref, yc_ref, yd_ref, h_ref, wo_ref, g_ref, rw_ref, rb_ref,
                  hnew_ref, hn_ref, ri_ref, rf_ref, cnt_ref, *, tm, tr):
    y = jnp.concatenate([ya_ref[...], yb_ref[...], yc_ref[...], yd_ref[...]], axis=-1).astype(BF16)
    h_new = h_ref[...] + jnp.dot(y, wo_ref[...], preferred_element_type=F32)
    hnew_ref[...] = h_new
    hn = _rms(h_new, g_ref[...])
    hn_hi, hn_lo = _split(hn)
    hn_ref[...] = hn_hi
    t = jnp.dot(hn_hi, rw_ref[...], preferred_element_type=F32)
    logits = (t[:, :LANES] + t[:, LANES:]
              + jnp.dot(hn_lo, rw_ref[:, :LANES], preferred_element_type=F32)) + rb_ref[...]
    lane = lax.broadcasted_iota(I32, (tm, LANES), 1)
    neg = -jnp.inf
    big_lane = LANES

    def masked_top(vals, mask):
        v = jnp.where(mask, vals, neg)
        mx = jnp.max(v, axis=-1, keepdims=True)
        idx = jnp.min(jnp.where(mask & (v == mx), lane, big_lane), axis=-1, keepdims=True)
        return v, mx, idx

    gmask = lane < N_GROUPS
    gv, gmx, gidx = masked_top(logits, gmask)
    g_top = 1.0 / jnp.sum(jnp.where(gmask, jnp.exp(gv - gmx), 0.0), axis=-1, keepdims=True)
    lo = N_GROUPS + gidx * EXPERTS_PER_GROUP
    emask = (lane >= lo) & (lane < lo + EXPERTS_PER_GROUP)
    ev, emx, eidx1 = masked_top(logits, emask)
    esum = jnp.sum(jnp.where(emask, jnp.exp(ev - emx), 0.0), axis=-1, keepdims=True)
    p1 = 1.0 / esum
    _, emx2, eidx2 = masked_top(logits, emask & (lane != eidx1))
    p2 = jnp.exp(emx2 - emx) / esum
    denom = p1 + p2
    rf_ref[...] = jnp.where(lane == 0, g_top * p1 / denom, jnp.where(lane == 1, g_top * p2 / denom, 0.0))

    expert = [eidx1 - N_GROUPS, eidx2 - N_GROUPS]
    hot = [lane == e for e in expert]
    m = (hot[0] | hot[1]).astype(BF16)
    below = (lax.broadcasted_iota(I32, (tr, tr), 0) > lax.broadcasted_iota(I32, (tr, tr), 1)).astype(BF16)
    before = jnp.concatenate(
        [jnp.dot(below, m[k * tr:(k + 1) * tr, :], preferred_element_type=F32) for k in range(tm // tr)], axis=0)
    rank = [jnp.sum(jnp.where(hot[s], before, 0.0), axis=-1, keepdims=True).astype(I32) for s in range(TOPK_IN)]
    out = jnp.zeros((tm, LANES), I32)
    for s in range(TOPK_IN):
        out = jnp.where(lane == s, expert[s], jnp.where(lane == TOPK_IN + s, rank[s], out))
    ri_ref[...] = out
    for k in range(tm // tr):
        last = (k + 1) * tr - 1
        total = before[last:last + 1, :] + m[last:last + 1, :].astype(F32)
        cnt_ref[k * SUBLANES:(k + 1) * SUBLANES, :] = jnp.broadcast_to(total, (SUBLANES, LANES)).astype(I32)


def _outproj(ya, yb, yc, yd, h2d, wo, g, rw, rb, tm, tr):
    T, D = h2d.shape
    yblk = pl.BlockSpec((tm, GROUP_W), lambda i: (i, 0))
    cnt_rows = tm // tr * SUBLANES
    return pl.pallas_call(
        functools.partial(_outproj_body, tm=tm, tr=tr), grid=(T // tm,),
        in_specs=[yblk, yblk, yblk, yblk,
                  pl.BlockSpec((tm, D), lambda i: (i, 0)),
                  pl.BlockSpec((D, D), lambda i: (0, 0)),
                  pl.BlockSpec((1, D), lambda i: (0, 0)),
                  pl.BlockSpec((D, 2 * LANES), lambda i: (0, 0)),
                  pl.BlockSpec((1, LANES), lambda i: (0, 0))],
        out_specs=[pl.BlockSpec((tm, D), lambda i: (i, 0)),
                   pl.BlockSpec((tm, D), lambda i: (i, 0)),
                   pl.BlockSpec((tm, LANES), lambda i: (i, 0)),
                   pl.BlockSpec((tm, LANES), lambda i: (i, 0)),
                   pl.BlockSpec((cnt_rows, LANES), lambda i: (i, 0))],
        out_shape=[jax.ShapeDtypeStruct((T, D), F32),
                   jax.ShapeDtypeStruct((T, D), BF16),
                   jax.ShapeDtypeStruct((T, LANES), I32),
                   jax.ShapeDtypeStruct((T, LANES), F32),
                   jax.ShapeDtypeStruct((T // tr * SUBLANES, LANES), I32)],
        compiler_params=_params("parallel"), name="outproj")(ya, yb, yc, yd, h2d, wo, g, rw, rb)


def _dispatch_plan(expert, lrank, cnt_tile, tm):
    T = expert.shape[0]
    nt = T // tm
    counts = jnp.sum(cnt_tile, axis=0)
    padded = (counts + RUN_ROWS + MOE_BLOCK - 1) // MOE_BLOCK * MOE_BLOCK
    pad_end = jnp.cumsum(padded)
    pad_start = pad_end - padded
    gstart = pad_start[None, :] + jnp.cumsum(cnt_tile, axis=0) - cnt_tile
    nchunk = (cnt_tile + RUN_ROWS - 1) // RUN_ROWS
    chunk_end = jnp.cumsum(nchunk, axis=1)
    lstart = (chunk_end - nchunk) * RUN_ROWS
    onehot = expert[:, :, None] == jnp.arange(N_EXPERTS, dtype=I32)[None, None, :]
    pick = lambda tab: jnp.sum(jnp.where(onehot, jnp.repeat(tab, tm, axis=0)[:, None, :], 0), axis=-1)
    lpos =(lrank + pick(lstart)).astype(I32)
    max_chunks = tm * TOPK_IN // RUN_ROWS + N_EXPERTS
    c = jnp.arange(max_chunks, dtype=I32)
    ce = jnp.minimum(jnp.sum(chunk_end[:, None, :] <= c[None, :, None], axis=-1), N_EXPERTS - 1)
    ce_hot = ce[:, :, None] == jnp.arange(N_EXPERTS, dtype=I32)[None, None, :]
    take = lambda tab: jnp.sum(jnp.where(ce_hot, tab[:, None, :], 0), axis=-1)
    chunk_row = take(gstart) + (c[None, :] - take(chunk_end - nchunk)) * RUN_ROWS
    chunk_row = jnp.where(c[None, :] < chunk_end[:, -1:], chunk_row, 0)
    table = jnp.concatenate([chunk_row.astype(I32), jnp.zeros((nt, LANES - 1 - max_chunks), I32),
                             chunk_end[:, -1:].astype(I32)], axis=1).reshape(nt, 1, LANES)
    n_blk = -(-(T * TOPK_IN + N_EXPERTS * RUN_ROWS) // MOE_BLOCK) + N_EXPERTS + 1
    blk_start = jnp.arange(n_blk, dtype=I32) * MOE_BLOCK
    blk_e = jnp.minimum(jnp.sum(pad_end[None, :] <= blk_start[:, None], axis=-1), N_EXPERTS - 1).astype(I32)
    n_used = (pad_end[-1] // MOE_BLOCK).astype(I32).reshape(1)
    zero_start = jnp.concatenate([pad_start + counts, pad_end[-1:]])
    zero_end = jnp.concatenate([pad_end, jnp.full((1,), n_blk * MOE_BLOCK, pad_end.dtype)])
    nz = N_EXPERTS + 1
    zfill = jnp.concatenate([zero_start, (zero_end - zero_start) // RUN_ROWS, zero_end - RUN_ROWS,
                             jnp.zeros((LANES - 3 * nz,), zero_start.dtype)]).astype(I32).reshape(1, LANES)
    return lpos, table, zfill, blk_e, n_used, n_blk


def _dispatch_body(zf_ref, tab_ref, lpos_ref, hn_ref, x_hbm, xs, zero, sem, zsem, n_prev, *, tm, nseg, nrow):
    i = pl.program_id(0)
    slot = i % 2
    step = RUN_ROWS * nseg

    @pl.when(i == 0)
    def _():
        zero[...] = jnp.zeros_like(zero)

        def zero_copy(r):
            return pltpu.make_async_copy(zero, x_hbm.at[pl.ds(pl.multiple_of(r * nseg, nseg), step), :], zsem)

        nz = N_EXPERTS + 1

        def for_chunks(fn):
            for e in range(nz):
                def body(k, carry, e=e):
                    fn(zero_copy(zf_ref[0, e] + k * RUN_ROWS))
                    return carry
                lax.fori_loop(0, zf_ref[0, nz + e], body, 0)

        for_chunks(lambda c: c.start())
        for_chunks(lambda c: c.wait())
        for e in range(nz):
            zero_copy(zf_ref[0, 2 * nz + e]).start()
        for e in range(nz):
            zero_copy(zf_ref[0, 2 * nz + e]).wait()

    row = lax.broadcasted_iota(I32, (nrow, tm), 0)
    sel = (row == lpos_ref[0:1, :]) | (row == lpos_ref[1:2, :])
    rows = jnp.dot(sel.astype(BF16), hn_ref[...], preferred_element_type=F32)
    for k in range(nseg):
        xs[slot, pl.ds(k, nrow, stride=nseg), :] = rows[:, k * LANES:(k + 1) * LANES]

    def chunk_copy(sl, c):
        return pltpu.make_async_copy(
            xs.at[sl, pl.ds(pl.multiple_of(c * step, step), step), :],
            x_hbm.at[pl.ds(pl.multiple_of(tab_ref[0, 0, c] * nseg, nseg), step), :], sem.at[sl])

    def wait_chunks(sl, n):
        def wait(c, carry):
            chunk_copy(sl, 0).wait()
            return carry
        lax.fori_loop(0, n, wait, 0)

    n_chunks = tab_ref[0, 0, LANES - 1]

    @pl.when(i > 0)
    def _():
        wait_chunks(1 - slot, n_prev[0])

    def start(c, carry):
        chunk_copy(slot, c).start()
        return carry

    lax.fori_loop(0, n_chunks, start, 0)
    n_prev[0] = n_chunks

    @pl.when(i == pl.num_programs(0) - 1)
    def _():
        wait_chunks(slot, n_chunks)


def _dispatch(zfill, table, lpos_t, hn, n_pad, tm):
    T, D = hn.shape
    nseg = D // LANES
    nrow = tm * TOPK_IN + N_EXPERTS * RUN_ROWS
    return pl.pallas_call(
        functools.partial(_dispatch_body, tm=tm, nseg=nseg, nrow=nrow), grid=(T // tm,),
        in_specs=[pl.BlockSpec(memory_space=pltpu.SMEM),
                  pl.BlockSpec((1, 1, LANES), lambda i: (i, 0, 0), memory_space=pltpu.SMEM),
                  pl.BlockSpec((None, TOPK_IN, tm), lambda i: (i, 0, 0)),
                  pl.BlockSpec((tm, D), lambda i: (i, 0))],
        out_specs=pl.BlockSpec(memory_space=pl.ANY),
        out_shape=jax.ShapeDtypeStruct((n_pad * nseg, LANES), F32),
        scratch_shapes=[pltpu.VMEM((2, nrow * nseg, LANES), F32), pltpu.VMEM((RUN_ROWS * nseg, LANES), F32),
                        pltpu.SemaphoreType.DMA((2,)), pltpu.SemaphoreType.DMA, pltpu.SMEM((1,), I32)],
        compiler_params=_params("arbitrary"), name="dispatch")(zfill, table, lpos_t, hn)


def _moe_body(blk_e_ref, n_used_ref, seg_ref, nxt_ref, x_ref, w1_hbm, w3_hbm, w2_hbm, y_ref,
              wf1, wf3, wf2, w1b, w3b, w2b, sem, *, nseg, layer):
    b = pl.program_id(0)
    R = MOE_BLOCK

    def fetch(e, slot):
        return [pltpu.make_async_copy(src.at[layer, e], dst.at[slot], sem.at[slot, k])
                for k, (src, dst) in enumerate(((w1_hbm, wf1), (w3_hbm, wf3), (w2_hbm, wf2)))]

    @pl.when(b < n_used_ref[0])
    def _():
        @pl.when((b == 0) | (blk_e_ref[b] != blk_e_ref[jnp.maximum(b - 1, 0)]))
        def _():
            e = blk_e_ref[b]
            slot = seg_ref[b] % 2

            @pl.when(b == 0)
            def _():
                for c in fetch(e, slot):
                    c.start()

            for c in fetch(e, slot):
                c.wait()
            w1b[...] = wf1[slot].astype(BF16)
            w3b[...] = wf3[slot].astype(BF16)
            w2b[...] = wf2[slot].astype(BF16)

            @pl.when(nxt_ref[b] >= 0)
            def _():
                for c in fetch(nxt_ref[b], 1 - slot):
                    c.start()

        x = jnp.concatenate([x_ref[pl.ds(s, R, stride=nseg), :] for s in range(nseg)], axis=-1).astype(BF16)
        a = jnp.dot(x, w1b[...], preferred_element_type=F32)
        g = jnp.dot(x, w3b[...], preferred_element_type=F32)
        hid = (a * jax.nn.sigmoid(a) * g).astype(BF16)
        y = jnp.dot(hid, w2b[...], preferred_element_type=F32)
        for s in range(nseg):
            y_ref[pl.ds(s, R, stride=nseg), :] = y[:, s * LANES:(s + 1) * LANES]

    @pl.when(b >= n_used_ref[0])
    def _():
        y_ref[...] = jnp.zeros_like(y_ref)


def _moe(blk_e, n_used, x_rows, w1, w3, w2, layer):
    n_blk = blk_e.shape[0]
    _, _, D, DE = w1.shape
    nseg = D // LANES
    R = MOE_BLOCK
    blk = jnp.arange(n_blk, dtype=I32)
    used = blk < n_used[0]
    change = (blk > 0) & (blk_e != jnp.roll(blk_e, 1)) & used
    seg = jnp.cumsum(change.astype(I32)).astype(I32)
    later = (blk[None, :] > blk[:, None]) & (seg[None, :] > seg[:, None]) & used[None, :]
    nxt_blk = jnp.min(jnp.where(later, blk[None, :], n_blk), axis=1)
    nxt = jnp.where(nxt_blk < n_blk, blk_e[jnp.minimum(nxt_blk, n_blk - 1)], -1).astype(I32)
    rows = lambda b, be, nu, sg, nx: (jnp.minimum(b, nu[0] - 1), 0)
    grid_spec = pltpu.PrefetchScalarGridSpec(
        num_scalar_prefetch=4, grid=(n_blk,),
        in_specs=[pl.BlockSpec((R * nseg, LANES), rows),
                  pl.BlockSpec(memory_space=pl.ANY), pl.BlockSpec(memory_space=pl.ANY),
                  pl.BlockSpec(memory_space=pl.ANY)],
        out_specs=pl.BlockSpec((R * nseg, LANES), lambda b, be, nu, sg, nx: (b, 0)),
        scratch_shapes=[pltpu.VMEM((2, D, DE), F32), pltpu.VMEM((2, D, DE), F32), pltpu.VMEM((2, DE, D), F32),
                        pltpu.VMEM((D, DE), BF16), pltpu.VMEM((D, DE), BF16), pltpu.VMEM((DE, D), BF16),
                        pltpu.SemaphoreType.DMA((2, 3))])
    return pl.pallas_call(
        functools.partial(_moe_body, nseg=nseg, layer=layer), grid_spec=grid_spec,
        out_shape=jax.ShapeDtypeStruct(x_rows.shape, F32),
        compiler_params=_params("arbitrary"), name="moe")(blk_e, n_used, seg, nxt, x_rows, w1, w3, w2)


def _combine_body(tab_ref, tabn_ref, h_ref, rf_ref, lpos_ref, y_hbm, o_ref, ybuf, sem, *, tm, nseg, nrow):
    i = pl.program_id(0)
    slot = i % 2
    step = RUN_ROWS * nseg

    def start_chunks(tab, sl):
        def start(c, carry):
            pltpu.make_async_copy(
                y_hbm.at[pl.ds(pl.multiple_of(tab[0, 0, c] * nseg, nseg), step), :],
                ybuf.at[sl, pl.ds(pl.multiple_of(c * step, step), step), :], sem.at[sl]).start()
            return carry
        lax.fori_loop(0, tab[0, 0, LANES - 1], start, 0)

    @pl.when(i == 0)
    def _():
        ybuf[...] = jnp.zeros_like(ybuf)
        start_chunks(tab_ref, 0)

    @pl.when(i + 1 < pl.num_programs(0))
    def _():
        start_chunks(tabn_ref, 1 - slot)

    col = lax.broadcasted_iota(I32, (tm, nrow), 1)
    sel = jnp.zeros((tm, nrow), F32)
    for s in range(TOPK_IN):
        sel = sel + jnp.where(col == lpos_ref[:, s:s + 1], rf_ref[:, s:s + 1], 0.0)

    def wait(c, carry):
        pltpu.make_async_copy(y_hbm.at[pl.ds(0, step), :], ybuf.at[slot, pl.ds(0, step), :],
                              sem.at[slot]).wait()
        return carry

    lax.fori_loop(0, tab_ref[0, 0, LANES - 1], wait, 0)
    y = jnp.concatenate([ybuf[slot, pl.ds(k, nrow, stride=nseg), :] for k in range(nseg)], axis=-1)
    o_ref[...] = h_ref[...] + jnp.dot(sel.astype(BF16), y.astype(BF16), preferred_element_type=F32)


def _combine(table, h2d, rf, lpos, y_rows, tm):
    T, D = h2d.shape
    nseg = D // LANES
    nt = T // tm
    nrow = tm * TOPK_IN + N_EXPERTS * RUN_ROWS
    return pl.pallas_call(
        functools.partial(_combine_body, tm=tm, nseg=nseg, nrow=nrow), grid=(nt,),
        in_specs=[pl.BlockSpec((1, 1, LANES), lambda i: (i, 0, 0), memory_space=pltpu.SMEM),
                  pl.BlockSpec((1, 1, LANES), lambda i: (jnp.minimum(i + 1, nt - 1), 0, 0),
                               memory_space=pltpu.SMEM),
                  pl.BlockSpec((tm, D), lambda i: (i, 0)),
                  pl.BlockSpec((tm, LANES), lambda i: (i, 0)),
                  pl.BlockSpec((tm, TOPK_IN), lambda i: (i, 0)),
                  pl.BlockSpec(memory_space=pl.ANY)],
        out_specs=pl.BlockSpec((tm, D), lambda i: (i, 0)),
        out_shape=jax.ShapeDtypeStruct((T, D), F32),
        scratch_shapes=[pltpu.VMEM((2, nrow * nseg, LANES), F32), pltpu.SemaphoreType.DMA((2,))],
        compiler_params=_params("arbitrary"), name="combine")(table, table, h2d, rf, lpos, y_rows)


def _pad_lanes(a, lane0, rows=1):
    a = a.reshape(rows, -1)
    return jnp.pad(a, ((0, 0), (lane0, LANES - lane0 - a.shape[-1])))


def _layer(h2d, B, S, p):
    T, D = h2d.shape
    tile = lambda a, n: jnp.tile(a.reshape(1, -1), (1, n))

    offs = np.cumsum([0, GROUP_W, GROUP_W, GROUP_W, N_HEADS, GROUP_W, GROUP_W,
                      GROUP_W, GROUP_W, GROUP_W, N_HEADS, N_HEADS, GROUP_W, GROUP_W])
    seg = lambda k: p['w_in'][:, offs[k]:offs[k + 1]]
    w_all = jnp.concatenate([seg(0), seg(1), seg(2), seg(4), seg(5), seg(6), seg(7), seg(8), seg(11), seg(12),
                             seg(3), seg(9), seg(10), jnp.zeros((D, LANES - 3 * N_HEADS), F32)],
                            axis=1).astype(BF16)
    gate_prm = jnp.concatenate([_pad_lanes(p['fox_f_bias'], LANE_FOX), _pad_lanes(p['gdn_dt_bias'], LANE_DECAY),
                                _pad_lanes(p['gdn_a_log'], LANE_DECAY), jnp.zeros((SUBLANES - 3, LANES), F32)], axis=0)

    big, small = _inproj(h2d, p['attn_norm_g'].reshape(1, D), w_all, tm=min(512, T))
    big3 = big.reshape(B, S, N_BIG_COLS * GROUP_W)
    ts = min(512, S)
    gcol, grow = _gates(small.reshape(B, S, LANES), gate_prm, ts)

    tq = min(256, S)
    ya = _fox(big3, gcol, tile(p['fox_qn_g'], N_HEADS), tile(p['fox_kn_g'], N_HEADS),
              p['fox_out_g'].reshape(1, GROUP_W), tq)

    bst = jnp.repeat(p['gmlp_bs'].T, HEAD_DIM, axis=1)
    yb = _gmlp(big, p['gmlp_ln_g'].reshape(1, -1), p['gmlp_ln_b'].reshape(1, -1), p['gmlp_ws'], bst,
               p['gmlp_out_g'].reshape(1, -1), tm=min(512, T))

    conv_w3 = p['gdn_conv_w'].reshape(CONV_K, 3, GROUP_W).transpose(1, 0, 2)
    grow4 = grow.reshape(B, 2 * SUBLANES, S // CHUNK, CHUNK).transpose(0, 2, 1, 3)
    yc = _gdn(big3, conv_w3, gcol, grow4, p['gdn_norm_g'].reshape(1, HEAD_DIM), ts)

    wbd = jax.scipy.linalg.block_diag(*[p['pool_w'][g] for g in range(len(POOL_WINDOWS))]).astype(BF16)
    yd = _pool(big3, wbd, p['pool_scale'].reshape(1, -1), p['pool_out_g'].reshape(1, -1), ts)

    rw = _pad_lanes(jnp.concatenate([p['router_g_w'], p['router_e_w']], axis=1), 0, rows=D)
    rw = jnp.concatenate(_split(rw), axis=1)
    rb = _pad_lanes(jnp.concatenate([p['router_g_b'], p['router_e_b']]), 0)
    flat = lambda a: a.reshape(T, GROUP_W)
    tmd = min(256, T)
    h_new, hn_rows, ri, rf, cnt = _outproj(flat(ya), yb, flat(yc), flat(yd), h2d, p['w_out'].astype(BF16),
                                           p['ffn_norm_g'].reshape(1, D), rw, rb, tm=min(512, T), tr=tmd)
    cnt_tile = cnt.reshape(T // tmd, SUBLANES, LANES)[:, 0, :N_EXPERTS]
    lpos, table, zfill, blk_e, n_used, n_blk = _dispatch_plan(
        ri[:, :TOPK_IN], ri[:, TOPK_IN:2 * TOPK_IN], cnt_tile, tmd)
    lpos_t = lpos.reshape(T // tmd, tmd, TOPK_IN).transpose(0, 2, 1)
    x_rows = _dispatch(zfill, table, lpos_t, hn_rows, n_blk * MOE_BLOCK, tmd)
    y_rows = _moe(blk_e, n_used, x_rows, p['moe_w1'], p['moe_w3'], p['moe_w2'], p['layer'])
    return _combine(table, h_new, rf, lpos, y_rows, tmd)


def kernel(x, attn_norm_g, w_in, w_out, fox_f_bias, fox_qn_g, fox_kn_g, fox_out_g, gmlp_ln_g, gmlp_ln_b, gmlp_ws, gmlp_bs, gmlp_out_g, gdn_conv_w, gdn_a_log, gdn_dt_bias, gdn_norm_g, pool_w, pool_scale, pool_out_g, ffn_norm_g, router_g_w, router_g_b, router_e_w, router_e_b, moe_w1, moe_w3, moe_w2):
    B, S, D = x.shape
    names = ('attn_norm_g', 'w_in', 'w_out', 'fox_f_bias', 'fox_qn_g', 'fox_kn_g', 'fox_out_g', 'gmlp_ln_g',
             'gmlp_ln_b', 'gmlp_ws', 'gmlp_bs', 'gmlp_out_g', 'gdn_conv_w', 'gdn_a_log', 'gdn_dt_bias',
             'gdn_norm_g', 'pool_w', 'pool_scale', 'pool_out_g', 'ffn_norm_g', 'router_g_w', 'router_g_b',
             'router_e_w', 'router_e_b', 'moe_w1', 'moe_w3', 'moe_w2')
    vals = (attn_norm_g, w_in, w_out, fox_f_bias, fox_qn_g, fox_kn_g, fox_out_g, gmlp_ln_g, gmlp_ln_b, gmlp_ws,
            gmlp_bs, gmlp_out_g, gdn_conv_w, gdn_a_log, gdn_dt_bias, gdn_norm_g, pool_w, pool_scale, pool_out_g,
            ffn_norm_g, router_g_w, router_g_b, router_e_w, router_e_b, moe_w1, moe_w3, moe_w2)
    h = x.reshape(B * S, D)
    stacked = ('moe_w1', 'moe_w3', 'moe_w2')
    for l in range(w_in.shape[0]):
        p = {n: (v if n in stacked else v[l]) for n, v in zip(names, vals)}
        p['layer'] = l
        h = _layer(h, B, S, p)
    return h.reshape(B, S, D)
```

```python
import functools

import jax
import jax.numpy as jnp
import numpy as np
from jax import lax
from jax.experimental import pallas as pl
from jax.experimental.pallas import tpu as pltpu

F32 = jnp.float32
BF16 = jnp.bfloat16
I32 = jnp.int32

EPS = 1e-6
HEAD_DIM = 64
GROUP_W = 256
N_HEADS = GROUP_W // HEAD_DIM
CHUNK = 64
GMLP_LEN = 128
CONV_K = 4
POOL_WINDOWS = (2, 4, 8, 16)
N_GROUPS = 4
EXPERTS_PER_GROUP = 8
N_EXPERTS = N_GROUPS * EXPERTS_PER_GROUP
TOPK_IN = 2
MOE_BLOCK = 256
FOX_VT_ROWS = HEAD_DIM + 16
FOX_BOUND_LOG2 = 40.0
FOX_SKIP_LOG2 = 160.0
RUN_ROWS = 8
LANES = 128
SUBLANES = 8
VMEM_LIMIT = 56 * 1024 * 1024

COL_FQ, COL_FK, COL_FV, COL_GU, COL_GV, COL_DQ, COL_DK, COL_DV, COL_DG, COL_PZ = range(10)
N_BIG_COLS = 10
LANE_FOX, LANE_DECAY, LANE_BETA = 0, 4, 8


def _params(*sem):
    return pltpu.CompilerParams(dimension_semantics=sem, vmem_limit_bytes=VMEM_LIMIT)


def _head_ones():
    r = lax.broadcasted_iota(I32, (GROUP_W, GROUP_W), 0) // HEAD_DIM
    c = lax.broadcasted_iota(I32, (GROUP_W, GROUP_W), 1) // HEAD_DIM
    return (r == c).astype(BF16)


def _head_sums(x, ones_bd):
    hi = x.astype(BF16)
    lo = (x - hi.astype(F32)).astype(BF16)
    return (jnp.dot(hi, ones_bd, preferred_element_type=F32)
            + jnp.dot(lo, ones_bd, preferred_element_type=F32))


def _rms(x, g):
    return x * lax.rsqrt(jnp.mean(x * x, axis=-1, keepdims=True) + EPS) * g


def _mm(a, b):
    return jnp.dot(a.astype(BF16), b.astype(BF16), preferred_element_type=F32)


def _mm_nt(a, b):
    return lax.dot_general(a.astype(BF16), b.astype(BF16), (((1,), (1,)), ((), ())),
                           preferred_element_type=F32)


def _mm_tn(a, b):
    return lax.dot_general(a.astype(BF16), b.astype(BF16), (((0,), (0,)), ((), ())),
                           preferred_element_type=F32)


def _split(a):
    hi = a.astype(BF16)
    return hi, (a - hi.astype(F32)).astype(BF16)


def _mm3(a, b):
    ah, al = _split(a)
    bh, bl = _split(b)
    d = functools.partial(jnp.dot, preferred_element_type=F32)
    return d(ah, bh) + (d(ah, bl) + d(al, bh))


def _inproj_body(x_ref, g_ref, w_ref, big_ref, small_ref):
    xn = _rms(x_ref[...], g_ref[...]).astype(BF16)
    nb = big_ref.shape[1]
    big_ref[...] = jnp.dot(xn, w_ref[:, :nb], preferred_element_type=F32)
    small_ref[...] = jnp.dot(xn, w_ref[:, nb:], preferred_element_type=F32)


def _inproj(x2d, g, w, tm):
    T, D = x2d.shape
    nb = w.shape[1] - LANES
    return pl.pallas_call(
        _inproj_body, grid=(T // tm,),
        in_specs=[pl.BlockSpec((tm, D), lambda i: (i, 0)),
                  pl.BlockSpec((1, D), lambda i: (0, 0)),
                  pl.BlockSpec((D, nb + LANES), lambda i: (0, 0))],
        out_specs=[pl.BlockSpec((tm, nb), lambda i: (i, 0)),
                   pl.BlockSpec((tm, LANES), lambda i: (i, 0))],
        out_shape=[jax.ShapeDtypeStruct((T, nb), F32), jax.ShapeDtypeStruct((T, LANES), F32)],
        compiler_params=_params("parallel"), name="inproj")(x2d, g, w)


def _gates_body(sm_ref, p_ref, col_ref, exp_ref, carry_ref, *, ts):
    @pl.when(pl.program_id(1) == 0)
    def _():
        carry_ref[...] = jnp.zeros_like(carry_ref)

    x = sm_ref[...]
    lane = lax.broadcasted_iota(I32, (ts, LANES), 1)
    is_fox = lane < LANE_DECAY
    is_dec = (lane >= LANE_DECAY) & (lane < LANE_BETA)
    is_beta = (lane >= LANE_BETA) & (lane < LANE_BETA + N_HEADS)
    logf = jax.nn.log_sigmoid(x + p_ref[0:1, :])
    g = -jnp.exp(p_ref[2:3, :]) * jax.nn.softplus(x + p_ref[1:2, :])
    beta = jax.nn.sigmoid(x)
    r = lax.broadcasted_iota(I32, (ts, ts), 0)
    c = lax.broadcasted_iota(I32, (ts, ts), 1)
    tri_full = (r >= c).astype(BF16)
    tri_chunk = ((r >= c) & (r // CHUNK == c // CHUNK)).astype(BF16)
    vals = jnp.where(is_fox, logf, jnp.where(is_dec, g, 0.0))
    hi = vals.astype(BF16)
    mid = (vals - hi.astype(F32)).astype(BF16)
    lo = (vals - hi.astype(F32) - mid.astype(F32)).astype(BF16)
    parts = jnp.concatenate([hi, mid, lo], axis=1)

    def tri_sum(tri):
        t = jnp.dot(tri, parts, preferred_element_type=F32)
        return t[:, :LANES] + (t[:, LANES:2 * LANES] + t[:, 2 * LANES:])

    cf = tri_sum(tri_full) + carry_ref[...]
    cg = tri_sum(tri_chunk)
    carry_ref[...] = cf[ts - 1:ts, :]
    out = jnp.where(is_fox, cf, jnp.where(is_dec, cg, jnp.where(is_beta, beta, 0.0)))
    col_ref[...] = out
    o_hi = out.astype(BF16)
    o_mid = (out - o_hi.astype(F32)).astype(BF16)
    o_lo = (out - o_hi.astype(F32) - o_mid.astype(F32)).astype(BF16)
    src = lax.broadcasted_iota(I32, (3 * LANES, 2 * GROUP_W), 0) % LANES
    dst = lax.broadcasted_iota(I32, (3 * LANES, 2 * GROUP_W), 1)
    want = jnp.where(dst < GROUP_W, LANE_DECAY, LANE_BETA) + (dst % GROUP_W) // HEAD_DIM
    exp_ref[...] = jnp.dot(jnp.concatenate([o_hi, o_mid, o_lo], axis=1), (src == want).astype(BF16),
                           preferred_element_type=F32)


def _gates(small3, prm, ts):
    B, S, _ = small3.shape
    return pl.pallas_call(
        functools.partial(_gates_body, ts=ts), grid=(B, S // ts),
        in_specs=[pl.BlockSpec((None, ts, LANES), lambda b, j: (b, j, 0)),
                  pl.BlockSpec((SUBLANES, LANES), lambda b, j: (0, 0))],
        out_specs=[pl.BlockSpec((None, ts, LANES), lambda b, j: (b, j, 0)),
                   pl.BlockSpec((None, ts, 2 * GROUP_W), lambda b, j: (b, j, 0))],
        out_shape=[jax.ShapeDtypeStruct((B, S, LANES), F32),
                   jax.ShapeDtypeStruct((B, S, 2 * GROUP_W), F32)],
        scratch_shapes=[pltpu.VMEM((1, LANES), F32)],
        compiler_params=_params("parallel", "arbitrary"), name="gates")(small3, prm)


def _split3_lanes(x, lane, lane0):
    hi = x.astype(BF16).astype(F32)
    mid = (x - hi).astype(BF16).astype(F32)
    lo = (x - hi - mid).astype(BF16).astype(F32)
    return jnp.where(lane == lane0, hi, jnp.where(lane == lane0 + 1, mid, jnp.where(lane == lane0 + 2, lo, 0.0)))


def _fox_body(jstart_ref, q_ref, k_ref, v_ref, ccol_ref, qg_ref, kg_ref, og_ref, o_ref,
              kn_scr, vt_scr, q_scr, m_scr, l_scr, acc_scr, s_scr, kmax_scr, *, tq, nk):
    b = pl.program_id(0)
    i = pl.program_id(1)
    ones_bd = _head_ones()
    lane = lax.broadcasted_iota(I32, (tq, LANES), 1)
    log2e = 1.0 / np.log(2.0)
    c_lane, r_lane = HEAD_DIM, HEAD_DIM + 3

    def head_norm(x, g):
        ss = _head_sums(x * x, ones_bd)
        return x * lax.rsqrt(ss * (1.0 / HEAD_DIM) + EPS) * g

    def head_tile(x, h, extra):
        pair = x[:, (h // 2) * LANES:(h // 2 + 1) * LANES]
        if h % 2:
            pair = pltpu.roll(pair, HEAD_DIM, 1)
        return jnp.where(lane < HEAD_DIM, pair, extra).astype(BF16)

    def sq_norms(x):
        xr = x.astype(BF16).astype(F32)
        return _head_sums(xr * xr, ones_bd)

    @pl.when(i == 0)
    def _():
        k_ones = jnp.where((lane >= r_lane) & (lane < r_lane + 3), 1.0, 0.0)
        vt_tail = (lax.broadcasted_iota(I32, (FOX_VT_ROWS - HEAD_DIM, tq), 0) == 0).astype(BF16)
        kmax = jnp.zeros((1, GROUP_W), F32)
        for c in range(nk):
            rows = slice(c * tq, (c + 1) * tq)
            kc = head_norm(k_ref[rows, :], kg_ref[...])
            kmax = jnp.maximum(kmax, jnp.max(sq_norms(kc), axis=0, keepdims=True))
            vt = v_ref[rows, :].T.astype(BF16)
            cc = ccol_ref[rows, :] * (-log2e)
            for h in range(N_HEADS):
                extra = _split3_lanes(cc[:, LANE_FOX + h:LANE_FOX + h + 1], lane, c_lane) + k_ones
                kn_scr[h, c] = head_tile(kc, h, extra)
                vt_scr[h, c] = jnp.concatenate([vt[h * HEAD_DIM:(h + 1) * HEAD_DIM, :], vt_tail], axis=0)
        kmax_scr[...] = kmax

    qn = head_norm(q_ref[...], qg_ref[...]) * (HEAD_DIM ** -0.5 * log2e)
    bound = jnp.sqrt(sq_norms(qn) * kmax_scr[...]) * 1.001
    bounded = jnp.max(bound) <= FOX_BOUND_LOG2
    c_i = ccol_ref[pl.ds(pl.multiple_of(i * tq, tq), tq), :] * log2e
    q_ones = jnp.where((lane >= c_lane) & (lane < c_lane + 3), 1.0, 0.0)
    for h in range(N_HEADS):
        r_i = bound[:, h * HEAD_DIM:h * HEAD_DIM + 1] - c_i[:, LANE_FOX + h:LANE_FOX + h + 1]
        neg_r = jnp.where(bounded, -r_i, 0.0)
        q_scr[h] = head_tile(qn, h, q_ones + _split3_lanes(neg_r, lane, r_lane))
    acc_scr[...] = jnp.zeros_like(acc_scr)
    causal = (lax.broadcasted_iota(I32, (tq, tq), 0) <= lax.broadcasted_iota(I32, (tq, tq), 1))

    heads = range(N_HEADS)

    def scores(j):
        return [lax.dot_general(kn_scr[h, j], q_scr[h], (((1,), (1,)), ((), ())),
                                preferred_element_type=F32) for h in heads]

    def stash(s):
        for h in heads:
            s_scr[h] = s[h]

    def absorb_general(j, masked):
        p, alpha = [], []
        for h in heads:
            s = s_scr[h]
            if masked:
                s = jnp.where(causal, s, -jnp.inf)
            m_old = m_scr[h]
            m_new = jnp.maximum(m_old, jnp.max(s, axis=0, keepdims=True))
            alpha.append(jnp.exp2(m_old - m_new))
            ph = jnp.exp2(s - m_new)
            l_scr[h] = alpha[h] * l_scr[h] + jnp.sum(ph, axis=0, keepdims=True)
            m_scr[h] = m_new
            p.append(ph.astype(BF16))
        pv = [jnp.dot(vt_scr[h, j], p[h], preferred_element_type=F32) for h in heads]
        for h in heads:
            acc_scr[h] = alpha[h] * acc_scr[h] + pv[h]

    def absorb_bounded(j, masked):
        p = []
        for h in heads:
            s = s_scr[h]
            if masked:
                s = jnp.where(causal, s, -jnp.inf)
            p.append(jnp.exp2(s).astype(BF16))
        pv = [jnp.dot(vt_scr[h, j], p[h], preferred_element_type=F32) for h in heads]
        for h in heads:
            acc_scr[h] += pv[h]

    def run(absorb, j0):
        stash(scores(j0))

        def body(j, c):
            s_next = scores(j + 1)
            absorb(j, False)
            stash(s_next)
            return c

        lax.fori_loop(j0, i, body, 0)
        absorb(i, True)

    @pl.when(bounded)
    def _():
        run(absorb_bounded, jstart_ref[b * nk + i])

    @pl.when(jnp.logical_not(bounded))
    def _():
        m_scr[...] = jnp.full_like(m_scr, -jnp.inf)
        l_scr[...] = jnp.zeros_like(l_scr)
        run(absorb_general, 0)
        for h in heads:
            acc_scr[h, HEAD_DIM:HEAD_DIM + 1, :] = l_scr[h]

    o_t = jnp.concatenate([acc_scr[h, :HEAD_DIM, :] / acc_scr[h, HEAD_DIM:HEAD_DIM + 1, :] for h in heads],
                          axis=0)
    o_ref[...] = _rms(o_t.T, og_ref[...])


def _fox_first_block(gcol, tq):
    B, S, _ = gcol.shape
    nk = S // tq
    c2 = gcol[:, :, LANE_FOX:LANE_FOX + N_HEADS] * (1.0 / np.log(2.0))
    first = c2[:, 0::tq, :]
    last = c2[:, tq - 1::tq, :]
    dead = (first[:, :, None, :] - last[:, None, :, :]) < -FOX_SKIP_LOG2
    dead = dead & (jnp.arange(nk)[None, :, None, None] > jnp.arange(nk)[None, None, :, None])
    return jnp.min(jnp.sum(dead, axis=2), axis=-1).astype(I32).reshape(B * nk)


def _fox(big3, gcol, qg, kg, og, tq):
    B, S, _ = big3.shape
    nk = S // tq
    row = pl.BlockSpec((1, GROUP_W), lambda b, i, js: (0, 0))
    grid_spec = pltpu.PrefetchScalarGridSpec(
        num_scalar_prefetch=1, grid=(B, nk),
        in_specs=[pl.BlockSpec((None, tq, GROUP_W), lambda b, i, js: (b, i, COL_FQ)),
                  pl.BlockSpec((None, S, GROUP_W), lambda b, i, js: (b, 0, COL_FK)),
                  pl.BlockSpec((None, S, GROUP_W), lambda b, i, js: (b, 0, COL_FV)),
                  pl.BlockSpec((None, S, LANES), lambda b, i, js: (b, 0, 0)),
                  row, row, row],
        out_specs=pl.BlockSpec((None, tq, GROUP_W), lambda b, i, js: (b, i, 0)),
        scratch_shapes=[pltpu.VMEM((N_HEADS, nk, tq, LANES), BF16),
                        pltpu.VMEM((N_HEADS, nk, FOX_VT_ROWS, tq), BF16),
                        pltpu.VMEM((N_HEADS, tq, LANES), BF16),
                        pltpu.VMEM((N_HEADS, 1, tq), F32),
                        pltpu.VMEM((N_HEADS, 1, tq), F32),
                        pltpu.VMEM((N_HEADS, FOX_VT_ROWS, tq), F32),
                        pltpu.VMEM((N_HEADS, tq, tq), F32),
                        pltpu.VMEM((1, GROUP_W), F32)])
    return pl.pallas_call(
        functools.partial(_fox_body, tq=tq, nk=nk), grid_spec=grid_spec,
        out_shape=jax.ShapeDtypeStruct((B, S, GROUP_W), F32),
        compiler_params=_params("parallel", "arbitrary"), name="fox")(
            _fox_first_block(gcol, tq), big3, big3, big3, gcol, qg, kg, og)


def _gelu(x):
    return 0.5 * x * (1.0 + lax.erf(x * (2.0 ** -0.5)))


def _gmlp_body(u_ref, v_ref, lg_ref, lb_ref, ws_ref, bst_ref, og_ref, o_ref, *, nwin):
    L = GMLP_LEN
    r = lax.broadcasted_iota(I32, (L, L), 0) // CHUNK
    c = lax.broadcasted_iota(I32, (L, L), 1) // CHUNK
    mask = r >= c
    ws = [jnp.where(mask, ws_ref[h], 0.0).astype(BF16) for h in range(N_HEADS)]
    for n in range(nwin):
        u = _gelu(u_ref[n * L:(n + 1) * L, :])
        v = _gelu(v_ref[n * L:(n + 1) * L, :])
        mu = jnp.mean(v, axis=-1, keepdims=True)
        vc = v - mu
        var = jnp.mean(vc * vc, axis=-1, keepdims=True)
        vn = (vc * lax.rsqrt(var + EPS) * lg_ref[...] + lb_ref[...]).astype(BF16)
        mixed = jnp.concatenate(
            [jnp.dot(ws[h], vn[:, h * HEAD_DIM:(h + 1) * HEAD_DIM], preferred_element_type=F32)
             for h in range(N_HEADS)], axis=-1) + bst_ref[...]
        o_ref[n * L:(n + 1) * L, :] = _rms(u * mixed, og_ref[...])


def _gmlp(big, lg, lb, ws, bst, og, tm):
    T = big.shape[0]
    row = pl.BlockSpec((1, GROUP_W), lambda i: (0, 0))
    return pl.pallas_call(
        functools.partial(_gmlp_body, nwin=tm // GMLP_LEN), grid=(T // tm,),
        in_specs=[pl.BlockSpec((tm, GROUP_W), lambda i: (i, COL_GU)),
                  pl.BlockSpec((tm, GROUP_W), lambda i: (i, COL_GV)),
                  row, row,
                  pl.BlockSpec((N_HEADS, GMLP_LEN, GMLP_LEN), lambda i: (0, 0, 0)),
                  pl.BlockSpec((GMLP_LEN, GROUP_W), lambda i: (0, 0)),
                  row],
        out_specs=pl.BlockSpec((tm, GROUP_W), lambda i: (i, 0)),
        out_shape=jax.ShapeDtypeStruct((T, GROUP_W), F32),
        compiler_params=_params("parallel"), name="gmlp")(big, big, lg, lb, ws, bst, og)


def _gdn_body(q_ref, k_ref, v_ref, w_ref, gate_ref, gx_ref, ng_ref, o_ref,
              s_scr, u_scr, wq_scr, a_scr, kd_scr, dl_scr, t_scr, p_scr, rhs_scr, halo_scr, qkv_scr, *, nchunk, nb):
    C = CHUNK
    ts = nchunk * C
    first = pl.program_id(1) == 0

    @pl.when(first)
    def _():
        s_scr[...] = jnp.zeros_like(s_scr)

    ones_bd = _head_ones()
    for bb, a in [(bb, a) for bb in range(nb) for a in range(3)]:
        w = w_ref[a]
        x = (q_ref, k_ref, v_ref)[a][bb]
        xx = jnp.concatenate([jnp.where(first, 0.0, halo_scr[bb * 3 + a]), x], axis=0)
        halo_scr[bb * 3 + a] = x[ts - SUBLANES:, :]
        y = w[CONV_K - 1:CONV_K, :] * x
        for j in range(CONV_K - 1):
            y = y + w[j:j + 1, :] * pltpu.roll(xx, CONV_K - 1 - j, 0)[SUBLANES:, :]
        y = y * jax.nn.sigmoid(y)
        if a < 2:
            y = y * lax.rsqrt(_head_sums(y * y, ones_bd) + EPS)
        qkv_scr[bb * 3 + a] = y * (HEAD_DIM ** -0.5) if a == 0 else y

    W = GROUP_W
    pos = lax.broadcasted_iota(I32, (C, W), 1) % HEAD_DIM
    r = lax.broadcasted_iota(I32, (C, W), 0)
    tri, strict, eye = r >= pos, r > pos, r == pos
    same_head = (lax.broadcasted_iota(I32, (W, W), 0) // HEAD_DIM
                 == lax.broadcasted_iota(I32, (W, W), 1) // HEAD_DIM)
    mmb = functools.partial(jnp.dot, preferred_element_type=F32)

    def block_diag(x):
        return jnp.where(same_head, jnp.concatenate([x.astype(BF16)] * N_HEADS, axis=0), 0.0)

    items = nb * nchunk
    for n in range(items):
        bb = n // nchunk
        rows = slice((n % nchunk) * C, (n % nchunk + 1) * C)
        q, k, v = (qkv_scr[bb * 3 + a, rows, :] for a in range(3))
        gc = gx_ref[bb, rows, :W]
        beta = gx_ref[bb, rows, W:]
        gr = jnp.sum(jnp.where(eye, gc, 0.0), axis=0, keepdims=True)
        decay = jnp.exp(jnp.where(tri, gc - gr, -jnp.inf))
        kb = k * beta
        kk = lax.dot_general(jnp.concatenate([kb, q], axis=0).astype(BF16), block_diag(k),
                             (((1,), (1,)), ((), ())), preferred_element_type=F32)
        x = jnp.where(strict, -(kk[:C] * decay), 0.0)
        t_scr[n] = jnp.where(eye, 1.0, 0.0) + x
        p_scr[n] = x.astype(BF16)
        eg = jnp.exp(gc)
        g_last = gc[C - 1:C, :]
        rhs_scr[n, 0] = block_diag(v * beta)
        rhs_scr[n, 1] = block_diag(kb * eg)
        wq_scr[n, C:, :] = (q * eg).astype(BF16)
        a_scr[n] = jnp.where(tri, kk[C:] * decay, 0.0).astype(BF16)
        kd_scr[n] = (k * jnp.exp(g_last - gc)).astype(BF16)
        dl_scr[n] = jnp.exp(g_last)
    for level in range(1, 6):
        for n in range(items):
            p = p_scr[n]
            p_scr[n] = mmb(p, block_diag(p)).astype(BF16)
        for n in range(items):
            t = t_scr[n]
            t_scr[n] = t + mmb(t.astype(BF16), block_diag(p_scr[n]))
    for n in range(items):
        t = t_scr[n].astype(BF16)
        u_scr[n] = mmb(t, rhs_scr[n, 0])
        wq_scr[n, :C, :] = mmb(t, rhs_scr[n, 1]).astype(BF16)

    ones_bd = _head_ones()
    batch = range(nb)
    state = [s_scr[bb] for bb in batch]
    for c in range(nchunk):
        rows = slice(c * C, (c + 1) * C)
        it = [bb * nchunk + c for bb in batch]
        ws = [mmb(wq_scr[it[bb]], state[bb].astype(BF16)) for bb in batch]
        vb = [(u_scr[it[bb]] - ws[bb][:C]).astype(BF16) for bb in batch]
        o = [ws[bb][C:] + mmb(a_scr[it[bb]], block_diag(vb[bb])) for bb in batch]
        kv = [lax.dot_general(kd_scr[it[bb]], vb[bb], (((0,), (0,)), ((), ())), preferred_element_type=F32)
              for bb in batch]
        state = [state[bb] * dl_scr[it[bb]] + jnp.where(same_head, kv[bb], 0.0) for bb in batch]
        for bb in batch:
            gate = gate_ref[bb, rows, :]
            y = o[bb] * lax.rsqrt(_head_sums(o[bb] * o[bb], ones_bd) * (1.0 / HEAD_DIM) + EPS) * ng_ref[...]
            o_ref[bb, rows, :] = y * (gate * jax.nn.sigmoid(gate))
    for bb in batch:
        s_scr[bb] = state[bb]


def _gdn(big3, conv_w3, gx, ng, ts, nb):
    B, S, _ = big3.shape
    nchunk = ts // CHUNK
    col = lambda c: pl.BlockSpec((nb, ts, GROUP_W), lambda b, i: (b, i, c))
    W = GROUP_W
    items = nb * nchunk
    return pl.pallas_call(
        functools.partial(_gdn_body, nchunk=nchunk, nb=nb), grid=(B // nb, S // ts),
        in_specs=[col(COL_DQ), col(COL_DK), col(COL_DV),
                  pl.BlockSpec((3, CONV_K, W), lambda b, i: (0, 0, 0)),
                  col(COL_DG),
                  pl.BlockSpec((nb, ts, 2 * W), lambda b, i: (b, i, 0)),
                  pl.BlockSpec((1, W), lambda b, i: (0, 0))],
        out_specs=col(0), out_shape=jax.ShapeDtypeStruct((B, S, W), F32),
        scratch_shapes=[pltpu.VMEM((nb, W, W), F32),
                        pltpu.VMEM((items, CHUNK, W), F32),
                        pltpu.VMEM((items, 2 * CHUNK, W), BF16),
                        pltpu.VMEM((items, CHUNK, W), BF16),
                        pltpu.VMEM((items, CHUNK, W), BF16),
                        pltpu.VMEM((items, 1, W), F32),
                        pltpu.VMEM((items, CHUNK, W), F32),
                        pltpu.VMEM((items, CHUNK, W), BF16),
                        pltpu.VMEM((items, 2, W, W), BF16),
                        pltpu.VMEM((nb * 3, SUBLANES, W), F32),
                        pltpu.VMEM((nb * 3, ts, W), F32)],
        compiler_params=_params("parallel", "arbitrary"), name="gdn")(
            big3, big3, big3, conv_w3, big3, gx, ng)


def _pool_body(z_ref, halo_ref, w_ref, sc_ref, og_ref, o_ref, *, ts):
    i = pl.program_id(1)
    hr = 2 * SUBLANES
    z = z_ref[...]
    halo = jnp.where(i == 0, 0.0, halo_ref[...])
    s1 = jnp.concatenate([halo, z], axis=0)
    s2 = s1 + pltpu.roll(s1, 1, 0)
    s4 = s2 + pltpu.roll(s2, 2, 0)
    s8 = s4 + pltpu.roll(s4, 4, 0)
    s16 = s8 + pltpu.roll(s8, 8, 0)
    grp = lax.broadcasted_iota(I32, (ts, GROUP_W), 1) // (GROUP_W // len(POOL_WINDOWS))
    t = lax.broadcasted_iota(I32, (ts, GROUP_W), 0) + i * ts
    total = jnp.where(grp == 0, s2[hr:], jnp.where(grp == 1, s4[hr:], jnp.where(grp == 2, s8[hr:], s16[hr:])))
    win = jnp.where(grp == 0, POOL_WINDOWS[0], jnp.where(grp == 1, POOL_WINDOWS[1],
                    jnp.where(grp == 2, POOL_WINDOWS[2], POOL_WINDOWS[3])))
    pooled = total / jnp.minimum(t + 1, win).astype(F32)
    y = _mm(pooled - z, w_ref[...]) * sc_ref[...]
    o_ref[...] = _rms(y, og_ref[...])


def _pool(big3, wbd, sc, og, ts):
    B, S, _ = big3.shape
    hr = 2 * SUBLANES
    hb = ts // hr
    row = pl.BlockSpec((1, GROUP_W), lambda b, i: (0, 0))
    return pl.pallas_call(
        functools.partial(_pool_body, ts=ts), grid=(B, S // ts),
        in_specs=[pl.BlockSpec((None, ts, GROUP_W), lambda b, i: (b, i, COL_PZ)),
                  pl.BlockSpec((None, hr, GROUP_W), lambda b, i: (b, jnp.maximum(i * hb - 1, 0), COL_PZ)),
                  pl.BlockSpec((GROUP_W, GROUP_W), lambda b, i: (0, 0)), row, row],
        out_specs=pl.BlockSpec((None, ts, GROUP_W), lambda b, i: (b, i, 0)),
        out_shape=jax.ShapeDtypeStruct((B, S, GROUP_W), F32),
        compiler_params=_params("parallel", "parallel"), name="pool")(big3, big3, wbd, sc, og)


def _outproj_body(ya_ref, yb_ref, yc_ref, yd_ref, h_ref, wo_ref, g_ref, rw_ref, rb_ref,
                  hnew_ref, hn_ref, ri_ref, rf_ref, cnt_ref, *, tm, tr):
    y = jnp.concatenate([ya_ref[...], yb_ref[...], yc_ref[...], yd_ref[...]], axis=-1).astype(BF16)
    h_new = h_ref[...] + jnp.dot(y, wo_ref[...], preferred_element_type=F32)
    hnew_ref[...] = h_new
    hn = _rms(h_new, g_ref[...])
    hn_hi, hn_lo = _split(hn)
    hn_ref[...] = hn_hi
    t = jnp.dot(hn_hi, rw_ref[...], preferred_element_type=F32)
    logits = (t[:, :LANES] + t[:, LANES:]
              + jnp.dot(hn_lo, rw_ref[:, :LANES], preferred_element_type=F32)) + rb_ref[...]
    lane = lax.broadcasted_iota(I32, (tm, LANES), 1)
    neg = -jnp.inf
    big_lane = LANES

    def masked_top(vals, mask):
        v = jnp.where(mask, vals, neg)
        mx = jnp.max(v, axis=-1, keepdims=True)
        idx = jnp.min(jnp.where(mask & (v == mx), lane, big_lane), axis=-1, keepdims=True)
        return v, mx, idx

    gmask = lane < N_GROUPS
    gv, gmx, gidx = masked_top(logits, gmask)
    g_top = 1.0 / jnp.sum(jnp.where(gmask, jnp.exp(gv - gmx), 0.0), axis=-1, keepdims=True)
    lo = N_GROUPS + gidx * EXPERTS_PER_GROUP
    emask = (lane >= lo) & (lane < lo + EXPERTS_PER_GROUP)
    ev, emx, eidx1 = masked_top(logits, emask)
    esum = jnp.sum(jnp.where(emask, jnp.exp(ev - emx), 0.0), axis=-1, keepdims=True)
    p1 = 1.0 / esum
    _, emx2, eidx2 = masked_top(logits, emask & (lane != eidx1))
    p2 = jnp.exp(emx2 - emx) / esum
    denom = p1 + p2
    rf_ref[...] = jnp.where(lane == 0, g_top * p1 / denom, jnp.where(lane == 1, g_top * p2 / denom, 0.0))

    expert = [eidx1 - N_GROUPS, eidx2 - N_GROUPS]
    hot = [lane == e for e in expert]
    m = (hot[0] | hot[1]).astype(BF16)
    below = (lax.broadcasted_iota(I32, (tr, tr), 0) > lax.broadcasted_iota(I32, (tr, tr), 1)).astype(BF16)
    before = jnp.concatenate(
        [jnp.dot(below, m[k * tr:(k + 1) * tr, :], preferred_element_type=F32) for k in range(tm // tr)], axis=0)
    rank = [jnp.sum(jnp.where(hot[s], before, 0.0), axis=-1, keepdims=True).astype(I32) for s in range(TOPK_IN)]
    out = jnp.zeros((tm, LANES), I32)
    for s in range(TOPK_IN):
        out = jnp.where(lane == s, expert[s], jnp.where(lane == TOPK_IN + s, rank[s], out))
    ri_ref[...] = out
    for k in range(tm // tr):
        last = (k + 1) * tr - 1
        total = before[last:last + 1, :] + m[last:last + 1, :].astype(F32)
        cnt_ref[k * SUBLANES:(k + 1) * SUBLANES, :] = jnp.broadcast_to(total, (SUBLANES, LANES)).astype(I32)


def _outproj(ya, yb, yc, yd, h2d, wo, g, rw, rb, tm, tr):
    T, D = h2d.shape
    yblk = pl.BlockSpec((tm, GROUP_W), lambda i: (i, 0))
    cnt_rows = tm // tr * SUBLANES
    return pl.pallas_call(
        functools.partial(_outproj_body, tm=tm, tr=tr), grid=(T // tm,),
        in_specs=[yblk, yblk, yblk, yblk,
                  pl.BlockSpec((tm, D), lambda i: (i, 0)),
                  pl.BlockSpec((D, D), lambda i: (0, 0)),
                  pl.BlockSpec((1, D), lambda i: (0, 0)),
                  pl.BlockSpec((D, 2 * LANES), lambda i: (0, 0)),
                  pl.BlockSpec((1, LANES), lambda i: (0, 0))],
        out_specs=[pl.BlockSpec((tm, D), lambda i: (i, 0)),
                   pl.BlockSpec((tm, D), lambda i: (i, 0)),
                   pl.BlockSpec((tm, LANES), lambda i: (i, 0)),
                   pl.BlockSpec((tm, LANES), lambda i: (i, 0)),
                   pl.BlockSpec((cnt_rows, LANES), lambda i: (i, 0))],
        out_shape=[jax.ShapeDtypeStruct((T, D), F32),
                   jax.ShapeDtypeStruct((T, D), BF16),
                   jax.ShapeDtypeStruct((T, LANES), I32),
                   jax.ShapeDtypeStruct((T, LANES), F32),
                   jax.ShapeDtypeStruct((T // tr * SUBLANES, LANES), I32)],
        compiler_params=_params("parallel"), name="outproj")(ya, yb, yc, yd, h2d, wo, g, rw, rb)


def _dispatch_plan(expert, lrank, cnt_tile, tm):
    T = expert.shape[0]
    nt = T // tm
    counts = jnp.sum(cnt_tile, axis=0)
    padded = (counts + RUN_ROWS + MOE_BLOCK - 1) // MOE_BLOCK * MOE_BLOCK
    pad_end = jnp.cumsum(padded)
    pad_start = pad_end - padded
    gstart = pad_start[None, :] + jnp.cumsum(cnt_tile, axis=0) - cnt_tile
    nchunk = (cnt_tile + RUN_ROWS - 1) // RUN_ROWS
    chunk_end = jnp.cumsum(nchunk, axis=1)
    lstart = (chunk_end - nchunk) * RUN_ROWS
    onehot = expert[:, :, None] == jnp.arange(N_EXPERTS, dtype=I32)[None, None, :]
    pick = lambda tab: jnp.sum(jnp.where(onehot, jnp.repeat(tab, tm, axis=0)[:, None, :], 0), axis=-1)
    lpos =(lrank + pick(lstart)).astype(I32)
    max_chunks = tm * TOPK_IN // RUN_ROWS + N_EXPERTS
    c = jnp.arange(max_chunks, dtype=I32)
    ce = jnp.minimum(jnp.sum(chunk_end[:, None, :] <= c[None, :, None], axis=-1), N_EXPERTS - 1)
    ce_hot = ce[:, :, None] == jnp.arange(N_EXPERTS, dtype=I32)[None, None, :]
    take = lambda tab: jnp.sum(jnp.where(ce_hot, tab[:, None, :], 0), axis=-1)
    chunk_row = take(gstart) + (c[None, :] - take(chunk_end - nchunk)) * RUN_ROWS
    chunk_row = jnp.where(c[None, :] < chunk_end[:, -1:], chunk_row, 0)
    table = jnp.concatenate([chunk_row.astype(I32), jnp.zeros((nt, LANES - 1 - max_chunks), I32),
                             chunk_end[:, -1:].astype(I32)], axis=1).reshape(nt, 1, LANES)
    n_blk = -(-(T * TOPK_IN + N_EXPERTS * RUN_ROWS) // MOE_BLOCK) + N_EXPERTS + 1
    blk_start = jnp.arange(n_blk, dtype=I32) * MOE_BLOCK
    blk_e = jnp.minimum(jnp.sum(pad_end[None, :] <= blk_start[:, None], axis=-1), N_EXPERTS - 1).astype(I32)
    n_used = (pad_end[-1] // MOE_BLOCK).astype(I32).reshape(1)
    zero_start = jnp.concatenate([pad_start + counts, pad_end[-1:]])
    zero_end = jnp.concatenate([pad_end, jnp.full((1,), n_blk * MOE_BLOCK, pad_end.dtype)])
    nz = N_EXPERTS + 1
    zfill = jnp.concatenate([zero_start, (zero_end - zero_start) // RUN_ROWS, zero_end - RUN_ROWS,
                             jnp.zeros((LANES - 3 * nz,), zero_start.dtype)]).astype(I32).reshape(1, LANES)
    return lpos, table, zfill, blk_e, n_used, n_blk


def _dispatch_body(zf_ref, tab_ref, lpos_ref, hn_ref, x_hbm, xs, zero, sem, zsem, n_prev, *, tm, nseg, nrow):
    i = pl.program_id(0)
    slot = i % 2
    step = RUN_ROWS * nseg

    @pl.when(i == 0)
    def _():
        zero[...] = jnp.zeros_like(zero)

        def zero_copy(r):
            return pltpu.make_async_copy(zero, x_hbm.at[pl.ds(pl.multiple_of(r * nseg, nseg), step), :], zsem)

        nz = N_EXPERTS + 1

        def for_chunks(fn):
            for e in range(nz):
                def body(k, carry, e=e):
                    fn(zero_copy(zf_ref[0, e] + k * RUN_ROWS))
                    return carry
                lax.fori_loop(0, zf_ref[0, nz + e], body, 0)

        for_chunks(lambda c: c.start())
        for_chunks(lambda c: c.wait())
        for e in range(nz):
            zero_copy(zf_ref[0, 2 * nz + e]).start()
        for e in range(nz):
            zero_copy(zf_ref[0, 2 * nz + e]).wait()

    row = lax.broadcasted_iota(I32, (nrow, tm), 0)
    sel = (row == lpos_ref[0:1, :]) | (row == lpos_ref[1:2, :])
    rows = jnp.dot(sel.astype(BF16), hn_ref[...], preferred_element_type=F32)
    for k in range(nseg):
        xs[slot, pl.ds(k, nrow, stride=nseg), :] = rows[:, k * LANES:(k + 1) * LANES]

    def chunk_copy(sl, c):
        return pltpu.make_async_copy(
            xs.at[sl, pl.ds(pl.multiple_of(c * step, step), step), :],
            x_hbm.at[pl.ds(pl.multiple_of(tab_ref[0, 0, c] * nseg, nseg), step), :], sem.at[sl])

    def wait_chunks(sl, n):
        def wait(c, carry):
            chunk_copy(sl, 0).wait()
            return carry
        lax.fori_loop(0, n, wait, 0)

    n_chunks = tab_ref[0, 0, LANES - 1]

    @pl.when(i > 0)
    def _():
        wait_chunks(1 - slot, n_prev[0])

    def start(c, carry):
        chunk_copy(slot, c).start()
        return carry

    lax.fori_loop(0, n_chunks, start, 0)
    n_prev[0] = n_chunks

    @pl.when(i == pl.num_programs(0) - 1)
    def _():
        wait_chunks(slot, n_chunks)


def _dispatch(zfill, table, lpos_t, hn, n_pad, tm):
    T, D = hn.shape
    nseg = D // LANES
    nrow = tm * TOPK_IN + N_EXPERTS * RUN_ROWS
    return pl.pallas_call(
        functools.partial(_dispatch_body, tm=tm, nseg=nseg, nrow=nrow), grid=(T // tm,),
        in_specs=[pl.BlockSpec(memory_space=pltpu.SMEM),
                  pl.BlockSpec((1, 1, LANES), lambda i: (i, 0, 0), memory_space=pltpu.SMEM),
                  pl.BlockSpec((None, TOPK_IN, tm), lambda i: (i, 0, 0)),
                  pl.BlockSpec((tm, D), lambda i: (i, 0))],
        out_specs=pl.BlockSpec(memory_space=pl.ANY),
        out_shape=jax.ShapeDtypeStruct((n_pad * nseg, LANES), F32),
        scratch_shapes=[pltpu.VMEM((2, nrow * nseg, LANES), F32), pltpu.VMEM((RUN_ROWS * nseg, LANES), F32),
                        pltpu.SemaphoreType.DMA((2,)), pltpu.SemaphoreType.DMA, pltpu.SMEM((1,), I32)],
        compiler_params=_params("arbitrary"), name="dispatch")(zfill, table, lpos_t, hn)


def _moe_body(blk_e_ref, n_used_ref, seg_ref, nxt_ref, x_ref, w1_hbm, w3_hbm, w2_hbm, y_ref,
              wf1, wf3, wf2, w1b, w3b, w2b, sem, *, nseg, layer):
    b = pl.program_id(0)
    R = MOE_BLOCK

    def fetch(e, slot):
        return [pltpu.make_async_copy(src.at[layer, e], dst.at[slot], sem.at[slot, k])
                for k, (src, dst) in enumerate(((w1_hbm, wf1), (w3_hbm, wf3), (w2_hbm, wf2)))]

    @pl.when(b < n_used_ref[0])
    def _():
        @pl.when((b == 0) | (blk_e_ref[b] != blk_e_ref[jnp.maximum(b - 1, 0)]))
        def _():
            e = blk_e_ref[b]
            slot = seg_ref[b] % 2

            @pl.when(b == 0)
            def _():
                for c in fetch(e, slot):
                    c.start()

            for c in fetch(e, slot):
                c.wait()
            w1b[...] = wf1[slot].astype(BF16)
            w3b[...] = wf3[slot].astype(BF16)
            w2b[...] = wf2[slot].astype(BF16)

            @pl.when(nxt_ref[b] >= 0)
            def _():
                for c in fetch(nxt_ref[b], 1 - slot):
                    c.start()

        x = jnp.concatenate([x_ref[pl.ds(s, R, stride=nseg), :] for s in range(nseg)], axis=-1).astype(BF16)
        a = jnp.dot(x, w1b[...], preferred_element_type=F32)
        g = jnp.dot(x, w3b[...], preferred_element_type=F32)
        hid = (a * jax.nn.sigmoid(a) * g).astype(BF16)
        y = jnp.dot(hid, w2b[...], preferred_element_type=F32)
        for s in range(nseg):
            y_ref[pl.ds(s, R, stride=nseg), :] = y[:, s * LANES:(s + 1) * LANES]

    @pl.when(b >= n_used_ref[0])
    def _():
        y_ref[...] = jnp.zeros_like(y_ref)


def _moe(blk_e, n_used, x_rows, w1, w3, w2, layer):
    n_blk = blk_e.shape[0]
    _, _, D, DE = w1.shape
    nseg = D // LANES
    R = MOE_BLOCK
    blk = jnp.arange(n_blk, dtype=I32)
    used = blk < n_used[0]
    change = (blk > 0) & (blk_e != jnp.roll(blk_e, 1)) & used
    seg = jnp.cumsum(change.astype(I32)).astype(I32)
    later = (blk[None, :] > blk[:, None]) & (seg[None, :] > seg[:, None]) & used[None, :]
    nxt_blk = jnp.min(jnp.where(later, blk[None, :], n_blk), axis=1)
    nxt = jnp.where(nxt_blk < n_blk, blk_e[jnp.minimum(nxt_blk, n_blk - 1)], -1).astype(I32)
    rows = lambda b, be, nu, sg, nx: (jnp.minimum(b, nu[0] - 1), 0)
    grid_spec = pltpu.PrefetchScalarGridSpec(
        num_scalar_prefetch=4, grid=(n_blk,),
        in_specs=[pl.BlockSpec((R * nseg, LANES), rows),
                  pl.BlockSpec(memory_space=pl.ANY), pl.BlockSpec(memory_space=pl.ANY),
                  pl.BlockSpec(memory_space=pl.ANY)],
        out_specs=pl.BlockSpec((R * nseg, LANES), lambda b, be, nu, sg, nx: (b, 0)),
        scratch_shapes=[pltpu.VMEM((2, D, DE), F32), pltpu.VMEM((2, D, DE), F32), pltpu.VMEM((2, DE, D), F32),
                        pltpu.VMEM((D, DE), BF16), pltpu.VMEM((D, DE), BF16), pltpu.VMEM((DE, D), BF16),
                        pltpu.SemaphoreType.DMA((2, 3))])
    return pl.pallas_call(
        functools.partial(_moe_body, nseg=nseg, layer=layer), grid_spec=grid_spec,
        out_shape=jax.ShapeDtypeStruct(x_rows.shape, F32),
        compiler_params=_params("arbitrary"), name="moe")(blk_e, n_used, seg, nxt, x_rows, w1, w3, w2)


def _combine_body(tab_ref, tabn_ref, h_ref, rf_ref, lpos_ref, y_hbm, o_ref, ybuf, sem, *, tm, nseg, nrow):
    i = pl.program_id(0)
    slot = i % 2
    step = RUN_ROWS * nseg

    def start_chunks(tab, sl):
        def start(c, carry):
            pltpu.make_async_copy(
                y_hbm.at[pl.ds(pl.multiple_of(tab[0, 0, c] * nseg, nseg), step), :],
                ybuf.at[sl, pl.ds(pl.multiple_of(c * step, step), step), :], sem.at[sl]).start()
            return carry
        lax.fori_loop(0, tab[0, 0, LANES - 1], start, 0)

    @pl.when(i == 0)
    def _():
        ybuf[...] = jnp.zeros_like(ybuf)
        start_chunks(tab_ref, 0)

    @pl.when(i + 1 < pl.num_programs(0))
    def _():
        start_chunks(tabn_ref, 1 - slot)

    col = lax.broadcasted_iota(I32, (tm, nrow), 1)
    sel = jnp.zeros((tm, nrow), F32)
    for s in range(TOPK_IN):
        sel = sel + jnp.where(col == lpos_ref[:, s:s + 1], rf_ref[:, s:s + 1], 0.0)

    def wait(c, carry):
        pltpu.make_async_copy(y_hbm.at[pl.ds(0, step), :], ybuf.at[slot, pl.ds(0, step), :],
                              sem.at[slot]).wait()
        return carry

    lax.fori_loop(0, tab_ref[0, 0, LANES - 1], wait, 0)
    y = jnp.concatenate([ybuf[slot, pl.ds(k, nrow, stride=nseg), :] for k in range(nseg)], axis=-1)
    o_ref[...] = h_ref[...] + jnp.dot(sel.astype(BF16), y.astype(BF16), preferred_element_type=F32)


def _combine(table, h2d, rf, lpos, y_rows, tm):
    T, D = h2d.shape
    nseg = D // LANES
    nt = T // tm
    nrow = tm * TOPK_IN + N_EXPERTS * RUN_ROWS
    return pl.pallas_call(
        functools.partial(_combine_body, tm=tm, nseg=nseg, nrow=nrow), grid=(nt,),
        in_specs=[pl.BlockSpec((1, 1, LANES), lambda i: (i, 0, 0), memory_space=pltpu.SMEM),
                  pl.BlockSpec((1, 1, LANES), lambda i: (jnp.minimum(i + 1, nt - 1), 0, 0),
                               memory_space=pltpu.SMEM),
                  pl.BlockSpec((tm, D), lambda i: (i, 0)),
                  pl.BlockSpec((tm, LANES), lambda i: (i, 0)),
                  pl.BlockSpec((tm, TOPK_IN), lambda i: (i, 0)),
                  pl.BlockSpec(memory_space=pl.ANY)],
        out_specs=pl.BlockSpec((tm, D), lambda i: (i, 0)),
        out_shape=jax.ShapeDtypeStruct((T, D), F32),
        scratch_shapes=[pltpu.VMEM((2, nrow * nseg, LANES), F32), pltpu.SemaphoreType.DMA((2,))],
        compiler_params=_params("arbitrary"), name="combine")(table, table, h2d, rf, lpos, y_rows)


def _pad_lanes(a, lane0, rows=1):
    a = a.reshape(rows, -1)
    return jnp.pad(a, ((0, 0), (lane0, LANES - lane0 - a.shape[-1])))


def _layer(h2d, B, S, p):
    T, D = h2d.shape
    tile = lambda a, n: jnp.tile(a.reshape(1, -1), (1, n))

    offs = np.cumsum([0, GROUP_W, GROUP_W, GROUP_W, N_HEADS, GROUP_W, GROUP_W,
                      GROUP_W, GROUP_W, GROUP_W, N_HEADS, N_HEADS, GROUP_W, GROUP_W])
    seg = lambda k: p['w_in'][:, offs[k]:offs[k + 1]]
    w_all = jnp.concatenate([seg(0), seg(1), seg(2), seg(4), seg(5), seg(6), seg(7), seg(8), seg(11), seg(12),
                             seg(3), seg(9), seg(10), jnp.zeros((D, LANES - 3 * N_HEADS), F32)],
                            axis=1).astype(BF16)
    gate_prm = jnp.concatenate([_pad_lanes(p['fox_f_bias'], LANE_FOX), _pad_lanes(p['gdn_dt_bias'], LANE_DECAY),
                                _pad_lanes(p['gdn_a_log'], LANE_DECAY), jnp.zeros((SUBLANES - 3, LANES), F32)], axis=0)

    big, small = _inproj(h2d, p['attn_norm_g'].reshape(1, D), w_all, tm=min(512, T))
    big3 = big.reshape(B, S, N_BIG_COLS * GROUP_W)
    ts = min(512, S)
    gcol, gx = _gates(small.reshape(B, S, LANES), gate_prm, ts)

    tq = min(256, S)
    ya = _fox(big3, gcol, tile(p['fox_qn_g'], N_HEADS), tile(p['fox_kn_g'], N_HEADS),
              p['fox_out_g'].reshape(1, GROUP_W), tq)

    bst = jnp.repeat(p['gmlp_bs'].T, HEAD_DIM, axis=1)
    yb = _gmlp(big, p['gmlp_ln_g'].reshape(1, -1), p['gmlp_ln_b'].reshape(1, -1), p['gmlp_ws'], bst,
               p['gmlp_out_g'].reshape(1, -1), tm=min(512, T))

    conv_w3 = p['gdn_conv_w'].reshape(CONV_K, 3, GROUP_W).transpose(1, 0, 2)
    yc = _gdn(big3, conv_w3, gx, tile(p['gdn_norm_g'], N_HEADS), ts, nb=2 if B % 2 == 0 else 1)

    wbd = jax.scipy.linalg.block_diag(*[p['pool_w'][g] for g in range(len(POOL_WINDOWS))]).astype(BF16)
    yd = _pool(big3, wbd, p['pool_scale'].reshape(1, -1), p['pool_out_g'].reshape(1, -1), ts)

    rw = _pad_lanes(jnp.concatenate([p['router_g_w'], p['router_e_w']], axis=1), 0, rows=D)
    rw = jnp.concatenate(_split(rw), axis=1)
    rb = _pad_lanes(jnp.concatenate([p['router_g_b'], p['router_e_b']]), 0)
    flat = lambda a: a.reshape(T, GROUP_W)
    tmd = min(256, T)
    h_new, hn_rows, ri, rf, cnt = _outproj(flat(ya), yb, flat(yc), flat(yd), h2d, p['w_out'].astype(BF16),
                                           p['ffn_norm_g'].reshape(1, D), rw, rb, tm=min(512, T), tr=tmd)
    cnt_tile = cnt.reshape(T // tmd, SUBLANES, LANES)[:, 0, :N_EXPERTS]
    lpos, table, zfill, blk_e, n_used, n_blk = _dispatch_plan(
        ri[:, :TOPK_IN], ri[:, TOPK_IN:2 * TOPK_IN], cnt_tile, tmd)
    lpos_t = lpos.reshape(T // tmd, tmd, TOPK_IN).transpose(0, 2, 1)
    x_rows = _dispatch(zfill, table, lpos_t, hn_rows, n_blk * MOE_BLOCK, tmd)
    y_rows = _moe(blk_e, n_used, x_rows, p['moe_w1'], p['moe_w3'], p['moe_w2'], p['layer'])
    return _combine(table, h_new, rf, lpos, y_rows, tmd)


def kernel(x, attn_norm_g, w_in, w_out, fox_f_bias, fox_qn_g, fox_kn_g, fox_out_g, gmlp_ln_g, gmlp_ln_b, gmlp_ws, gmlp_bs, gmlp_out_g, gdn_conv_w, gdn_a_log, gdn_dt_bias, gdn_norm_g, pool_w, pool_scale, pool_out_g, ffn_norm_g, router_g_w, router_g_b, router_e_w, router_e_b, moe_w1, moe_w3, moe_w2):
    B, S, D = x.shape
    names = ('attn_norm_g', 'w_in', 'w_out', 'fox_f_bias', 'fox_qn_g', 'fox_kn_g', 'fox_out_g', 'gmlp_ln_g',
             'gmlp_ln_b', 'gmlp_ws', 'gmlp_bs', 'gmlp_out_g', 'gdn_conv_w', 'gdn_a_log', 'gdn_dt_bias',
             'gdn_norm_g', 'pool_w', 'pool_scale', 'pool_out_g', 'ffn_norm_g', 'router_g_w', 'router_g_b',
             'router_e_w', 'router_e_b', 'moe_w1', 'moe_w3', 'moe_w2')
    vals = (attn_norm_g, w_in, w_out, fox_f_bias, fox_qn_g, fox_kn_g, fox_out_g, gmlp_ln_g, gmlp_ln_b, gmlp_ws,
            gmlp_bs, gmlp_out_g, gdn_conv_w, gdn_a_log, gdn_dt_bias, gdn_norm_g, pool_w, pool_scale, pool_out_g,
            ffn_norm_g, router_g_w, router_g_b, router_e_w, router_e_b, moe_w1, moe_w3, moe_w2)
    h = x.reshape(B * S, D)
    stacked = ('moe_w1', 'moe_w3', 'moe_w2')
    for l in range(w_in.shape[0]):
        p = {n: (v if n in stacked else v[l]) for n, v in zip(names, vals)}
        p['layer'] = l
        h = _layer(h, B, S, p)
    return h.reshape(B, S, D)
```

```python
import functools

import jax
import jax.numpy as jnp
import numpy as np
from jax import lax
from jax.experimental import pallas as pl
from jax.experimental.pallas import tpu as pltpu

F32 = jnp.float32
BF16 = jnp.bfloat16
I32 = jnp.int32

EPS = 1e-6
HEAD_DIM = 64
GROUP_W = 256
N_HEADS = GROUP_W // HEAD_DIM
CHUNK = 64
GMLP_LEN = 128
CONV_K = 4
POOL_WINDOWS = (2, 4, 8, 16)
N_GROUPS = 4
EXPERTS_PER_GROUP = 8
N_EXPERTS = N_GROUPS * EXPERTS_PER_GROUP
TOPK_IN = 2
MOE_BLOCK = 256
FOX_VT_ROWS = HEAD_DIM + 16
FOX_BOUND_LOG2 = 40.0
FOX_SKIP_LOG2 = 160.0
RUN_ROWS = 16
LANES = 128
SUBLANES = 8
VMEM_LIMIT = 56 * 1024 * 1024

COL_FQ, COL_FK, COL_FV, COL_GU, COL_GV, COL_DQ, COL_DK, COL_DV, COL_DG, COL_PZ = range(10)
N_BIG_COLS = 10
LANE_FOX, LANE_DECAY, LANE_BETA = 0, 4, 8


def _params(*sem):
    return pltpu.CompilerParams(dimension_semantics=sem, vmem_limit_bytes=VMEM_LIMIT)


def _head_ones():
    r = lax.broadcasted_iota(I32, (GROUP_W, GROUP_W), 0) // HEAD_DIM
    c = lax.broadcasted_iota(I32, (GROUP_W, GROUP_W), 1) // HEAD_DIM
    return (r == c).astype(BF16)


def _head_sums(x, ones_bd):
    hi = x.astype(BF16)
    lo = (x - hi.astype(F32)).astype(BF16)
    return (jnp.dot(hi, ones_bd, preferred_element_type=F32)
            + jnp.dot(lo, ones_bd, preferred_element_type=F32))


def _rms(x, g):
    return x * lax.rsqrt(jnp.mean(x * x, axis=-1, keepdims=True) + EPS) * g


def _mm(a, b):
    return jnp.dot(a.astype(BF16), b.astype(BF16), preferred_element_type=F32)


def _mm_nt(a, b):
    return lax.dot_general(a.astype(BF16), b.astype(BF16), (((1,), (1,)), ((), ())),
                           preferred_element_type=F32)


def _mm_tn(a, b):
    return lax.dot_general(a.astype(BF16), b.astype(BF16), (((0,), (0,)), ((), ())),
                           preferred_element_type=F32)


def _split(a):
    hi = a.astype(BF16)
    return hi, (a - hi.astype(F32)).astype(BF16)


def _mm3(a, b):
    ah, al = _split(a)
    bh, bl = _split(b)
    d = functools.partial(jnp.dot, preferred_element_type=F32)
    return d(ah, bh) + (d(ah, bl) + d(al, bh))


def _inproj_body(x_ref, g_ref, w_ref, big_ref, small_ref):
    xn = _rms(x_ref[...], g_ref[...]).astype(BF16)
    nb = big_ref.shape[1]
    big_ref[...] = jnp.dot(xn, w_ref[:, :nb], preferred_element_type=F32)
    small_ref[...] = jnp.dot(xn, w_ref[:, nb:], preferred_element_type=F32)


def _inproj(x2d, g, w, tm):
    T, D = x2d.shape
    nb = w.shape[1] - LANES
    return pl.pallas_call(
        _inproj_body, grid=(T // tm,),
        in_specs=[pl.BlockSpec((tm, D), lambda i: (i, 0)),
                  pl.BlockSpec((1, D), lambda i: (0, 0)),
                  pl.BlockSpec((D, nb + LANES), lambda i: (0, 0))],
        out_specs=[pl.BlockSpec((tm, nb), lambda i: (i, 0)),
                   pl.BlockSpec((tm, LANES), lambda i: (i, 0))],
        out_shape=[jax.ShapeDtypeStruct((T, nb), F32), jax.ShapeDtypeStruct((T, LANES), F32)],
        compiler_params=_params("parallel"), name="inproj")(x2d, g, w)


def _gates_body(sm_ref, p_ref, col_ref, exp_ref, carry_ref, *, ts):
    @pl.when(pl.program_id(1) == 0)
    def _():
        carry_ref[...] = jnp.zeros_like(carry_ref)

    x = sm_ref[...]
    lane = lax.broadcasted_iota(I32, (ts, LANES), 1)
    is_fox = lane < LANE_DECAY
    is_dec = (lane >= LANE_DECAY) & (lane < LANE_BETA)
    is_beta = (lane >= LANE_BETA) & (lane < LANE_BETA + N_HEADS)
    logf = jax.nn.log_sigmoid(x + p_ref[0:1, :])
    g = -jnp.exp(p_ref[2:3, :]) * jax.nn.softplus(x + p_ref[1:2, :])
    beta = jax.nn.sigmoid(x)
    r = lax.broadcasted_iota(I32, (ts, ts), 0)
    c = lax.broadcasted_iota(I32, (ts, ts), 1)
    tri_full = (r >= c).astype(BF16)
    tri_chunk = ((r >= c) & (r // CHUNK == c // CHUNK)).astype(BF16)
    vals = jnp.where(is_fox, logf, jnp.where(is_dec, g, 0.0))
    hi = vals.astype(BF16)
    mid = (vals - hi.astype(F32)).astype(BF16)
    lo = (vals - hi.astype(F32) - mid.astype(F32)).astype(BF16)
    parts = jnp.concatenate([hi, mid, lo], axis=1)

    def tri_sum(tri):
        t = jnp.dot(tri, parts, preferred_element_type=F32)
        return t[:, :LANES] + (t[:, LANES:2 * LANES] + t[:, 2 * LANES:])

    cf = tri_sum(tri_full) + carry_ref[...]
    cg = tri_sum(tri_chunk)
    carry_ref[...] = cf[ts - 1:ts, :]
    out = jnp.where(is_fox, cf, jnp.where(is_dec, cg, jnp.where(is_beta, beta, 0.0)))
    col_ref[...] = out
    o_hi = out.astype(BF16)
    o_mid = (out - o_hi.astype(F32)).astype(BF16)
    o_lo = (out - o_hi.astype(F32) - o_mid.astype(F32)).astype(BF16)
    src = lax.broadcasted_iota(I32, (3 * LANES, 2 * GROUP_W), 0) % LANES
    dst = lax.broadcasted_iota(I32, (3 * LANES, 2 * GROUP_W), 1)
    want = jnp.where(dst < GROUP_W, LANE_DECAY, LANE_BETA) + (dst % GROUP_W) // HEAD_DIM
    exp_ref[...] = jnp.dot(jnp.concatenate([o_hi, o_mid, o_lo], axis=1), (src == want).astype(BF16),
                           preferred_element_type=F32)


def _gates(small3, prm, ts):
    B, S, _ = small3.shape
    return pl.pallas_call(
        functools.partial(_gates_body, ts=ts), grid=(B, S // ts),
        in_specs=[pl.BlockSpec((None, ts, LANES), lambda b, j: (b, j, 0)),
                  pl.BlockSpec((SUBLANES, LANES), lambda b, j: (0, 0))],
        out_specs=[pl.BlockSpec((None, ts, LANES), lambda b, j: (b, j, 0)),
                   pl.BlockSpec((None, ts, 2 * GROUP_W), lambda b, j: (b, j, 0))],
        out_shape=[jax.ShapeDtypeStruct((B, S, LANES), F32),
                   jax.ShapeDtypeStruct((B, S, 2 * GROUP_W), F32)],
        scratch_shapes=[pltpu.VMEM((1, LANES), F32)],
        compiler_params=_params("parallel", "arbitrary"), name="gates")(small3, prm)


def _split3_lanes(x, lane, lane0):
    hi = x.astype(BF16).astype(F32)
    mid = (x - hi).astype(BF16).astype(F32)
    lo = (x - hi - mid).astype(BF16).astype(F32)
    return jnp.where(lane == lane0, hi, jnp.where(lane == lane0 + 1, mid, jnp.where(lane == lane0 + 2, lo, 0.0)))


def _fox_body(jstart_ref, q_ref, k_ref, v_ref, ccol_ref, qg_ref, kg_ref, og_ref, o_ref,
              kn_scr, vt_scr, q_scr, m_scr, l_scr, acc_scr, s_scr, kmax_scr, *, tq, nk):
    b = pl.program_id(0)
    i = pl.program_id(1)
    ones_bd = _head_ones()
    lane = lax.broadcasted_iota(I32, (tq, LANES), 1)
    log2e = 1.0 / np.log(2.0)
    c_lane, r_lane = HEAD_DIM, HEAD_DIM + 3

    def head_norm(x, g):
        ss = _head_sums(x * x, ones_bd)
        return x * lax.rsqrt(ss * (1.0 / HEAD_DIM) + EPS) * g

    def head_tile(x, h, extra):
        pair = x[:, (h // 2) * LANES:(h // 2 + 1) * LANES]
        if h % 2:
            pair = pltpu.roll(pair, HEAD_DIM, 1)
        return jnp.where(lane < HEAD_DIM, pair, extra).astype(BF16)

    def sq_norms(x):
        xr = x.astype(BF16).astype(F32)
        return _head_sums(xr * xr, ones_bd)

    @pl.when(i == 0)
    def _():
        k_ones = jnp.where((lane >= r_lane) & (lane < r_lane + 3), 1.0, 0.0)
        vt_tail = (lax.broadcasted_iota(I32, (FOX_VT_ROWS - HEAD_DIM, tq), 0) == 0).astype(BF16)
        kmax = jnp.zeros((1, GROUP_W), F32)
        for c in range(nk):
            rows = slice(c * tq, (c + 1) * tq)
            kc = head_norm(k_ref[rows, :], kg_ref[...])
            kmax = jnp.maximum(kmax, jnp.max(sq_norms(kc), axis=0, keepdims=True))
            vt = v_ref[rows, :].T.astype(BF16)
            cc = ccol_ref[rows, :] * (-log2e)
            for h in range(N_HEADS):
                extra = _split3_lanes(cc[:, LANE_FOX + h:LANE_FOX + h + 1], lane, c_lane) + k_ones
                kn_scr[h, c] = head_tile(kc, h, extra)
                vt_scr[h, c] = jnp.concatenate([vt[h * HEAD_DIM:(h + 1) * HEAD_DIM, :], vt_tail], axis=0)
        kmax_scr[...] = kmax

    qn = head_norm(q_ref[...], qg_ref[...]) * (HEAD_DIM ** -0.5 * log2e)
    bound = jnp.sqrt(sq_norms(qn) * kmax_scr[...]) * 1.001
    bounded = jnp.max(bound) <= FOX_BOUND_LOG2
    c_i = ccol_ref[pl.ds(pl.multiple_of(i * tq, tq), tq), :] * log2e
    q_ones = jnp.where((lane >= c_lane) & (lane < c_lane + 3), 1.0, 0.0)
    for h in range(N_HEADS):
        r_i = bound[:, h * HEAD_DIM:h * HEAD_DIM + 1] - c_i[:, LANE_FOX + h:LANE_FOX + h + 1]
        neg_r = jnp.where(bounded, -r_i, 0.0)
        q_scr[h] = head_tile(qn, h, q_ones + _split3_lanes(neg_r, lane, r_lane))
    acc_scr[...] = jnp.zeros_like(acc_scr)
    causal = (lax.broadcasted_iota(I32, (tq, tq), 0) <= lax.broadcasted_iota(I32, (tq, tq), 1))

    heads = range(N_HEADS)

    def scores(j):
        return [lax.dot_general(kn_scr[h, j], q_scr[h], (((1,), (1,)), ((), ())),
                                preferred_element_type=F32) for h in heads]

    def stash(s):
        for h in heads:
            s_scr[h] = s[h]

    def absorb_general(j, masked):
        p, alpha = [], []
        for h in heads:
            s = s_scr[h]
            if masked:
                s = jnp.where(causal, s, -jnp.inf)
            m_old = m_scr[h]
            m_new = jnp.maximum(m_old, jnp.max(s, axis=0, keepdims=True))
            alpha.append(jnp.exp2(m_old - m_new))
            ph = jnp.exp2(s - m_new)
            l_scr[h] = alpha[h] * l_scr[h] + jnp.sum(ph, axis=0, keepdims=True)
            m_scr[h] = m_new
            p.append(ph.astype(BF16))
        pv = [jnp.dot(vt_scr[h, j], p[h], preferred_element_type=F32) for h in heads]
        for h in heads:
            acc_scr[h] = alpha[h] * acc_scr[h] + pv[h]

    def absorb_bounded(j, masked):
        p = []
        for h in heads:
            s = s_scr[h]
            if masked:
                s = jnp.where(causal, s, -jnp.inf)
            p.append(jnp.exp2(s).astype(BF16))
        pv = [jnp.dot(vt_scr[h, j], p[h], preferred_element_type=F32) for h in heads]
        for h in heads:
            acc_scr[h] += pv[h]

    def run(absorb, j0):
        stash(scores(j0))

        def body(j, c):
            s_next = scores(j + 1)
            absorb(j, False)
            stash(s_next)
            return c

        lax.fori_loop(j0, i, body, 0)
        absorb(i, True)

    @pl.when(bounded)
    def _():
        run(absorb_bounded, jstart_ref[b * nk + i])

    @pl.when(jnp.logical_not(bounded))
    def _():
        m_scr[...] = jnp.full_like(m_scr, -jnp.inf)
        l_scr[...] = jnp.zeros_like(l_scr)
        run(absorb_general, 0)
        for h in heads:
            acc_scr[h, HEAD_DIM:HEAD_DIM + 1, :] = l_scr[h]

    o_t = jnp.concatenate([acc_scr[h, :HEAD_DIM, :] / acc_scr[h, HEAD_DIM:HEAD_DIM + 1, :] for h in heads],
                          axis=0)
    o_ref[...] = _rms(o_t.T, og_ref[...])


def _fox_first_block(gcol, tq):
    B, S, _ = gcol.shape
    nk = S // tq
    c2 = gcol[:, :, LANE_FOX:LANE_FOX + N_HEADS] * (1.0 / np.log(2.0))
    first = c2[:, 0::tq, :]
    last = c2[:, tq - 1::tq, :]
    dead = (first[:, :, None, :] - last[:, None, :, :]) < -FOX_SKIP_LOG2
    dead = dead & (jnp.arange(nk)[None, :, None, None] > jnp.arange(nk)[None, None, :, None])
    return jnp.min(jnp.sum(dead, axis=2), axis=-1).astype(I32).reshape(B * nk)


def _fox(big3, gcol, qg, kg, og, tq):
    B, S, _ = big3.shape
    nk = S // tq
    row = pl.BlockSpec((1, GROUP_W), lambda b, i, js: (0, 0))
    grid_spec = pltpu.PrefetchScalarGridSpec(
        num_scalar_prefetch=1, grid=(B, nk),
        in_specs=[pl.BlockSpec((None, tq, GROUP_W), lambda b, i, js: (b, i, COL_FQ)),
                  pl.BlockSpec((None, S, GROUP_W), lambda b, i, js: (b, 0, COL_FK)),
                  pl.BlockSpec((None, S, GROUP_W), lambda b, i, js: (b, 0, COL_FV)),
                  pl.BlockSpec((None, S, LANES), lambda b, i, js: (b, 0, 0)),
                  row, row, row],
        out_specs=pl.BlockSpec((None, tq, GROUP_W), lambda b, i, js: (b, i, 0)),
        scratch_shapes=[pltpu.VMEM((N_HEADS, nk, tq, LANES), BF16),
                        pltpu.VMEM((N_HEADS, nk, FOX_VT_ROWS, tq), BF16),
                        pltpu.VMEM((N_HEADS, tq, LANES), BF16),
                        pltpu.VMEM((N_HEADS, 1, tq), F32),
                        pltpu.VMEM((N_HEADS, 1, tq), F32),
                        pltpu.VMEM((N_HEADS, FOX_VT_ROWS, tq), F32),
                        pltpu.VMEM((N_HEADS, tq, tq), F32),
                        pltpu.VMEM((1, GROUP_W), F32)])
    return pl.pallas_call(
        functools.partial(_fox_body, tq=tq, nk=nk), grid_spec=grid_spec,
        out_shape=jax.ShapeDtypeStruct((B, S, GROUP_W), F32),
        compiler_params=_params("parallel", "arbitrary"), name="fox")(
            _fox_first_block(gcol, tq), big3, big3, big3, gcol, qg, kg, og)


def _gelu(x):
    return 0.5 * x * (1.0 + lax.erf(x * (2.0 ** -0.5)))


def _gmlp_body(u_ref, v_ref, lg_ref, lb_ref, ws_ref, bst_ref, og_ref, o_ref, *, nwin):
    L = GMLP_LEN
    r = lax.broadcasted_iota(I32, (L, L), 0) // CHUNK
    c = lax.broadcasted_iota(I32, (L, L), 1) // CHUNK
    mask = r >= c
    ws = [jnp.where(mask, ws_ref[h], 0.0).astype(BF16) for h in range(N_HEADS)]
    for n in range(nwin):
        u = _gelu(u_ref[n * L:(n + 1) * L, :])
        v = _gelu(v_ref[n * L:(n + 1) * L, :])
        mu = jnp.mean(v, axis=-1, keepdims=True)
        vc = v - mu
        var = jnp.mean(vc * vc, axis=-1, keepdims=True)
        vn = (vc * lax.rsqrt(var + EPS) * lg_ref[...] + lb_ref[...]).astype(BF16)
        mixed = jnp.concatenate(
            [jnp.dot(ws[h], vn[:, h * HEAD_DIM:(h + 1) * HEAD_DIM], preferred_element_type=F32)
             for h in range(N_HEADS)], axis=-1) + bst_ref[...]
        o_ref[n * L:(n + 1) * L, :] = _rms(u * mixed, og_ref[...])


def _gmlp(big, lg, lb, ws, bst, og, tm):
    T = big.shape[0]
    row = pl.BlockSpec((1, GROUP_W), lambda i: (0, 0))
    return pl.pallas_call(
        functools.partial(_gmlp_body, nwin=tm // GMLP_LEN), grid=(T // tm,),
        in_specs=[pl.BlockSpec((tm, GROUP_W), lambda i: (i, COL_GU)),
                  pl.BlockSpec((tm, GROUP_W), lambda i: (i, COL_GV)),
                  row, row,
                  pl.BlockSpec((N_HEADS, GMLP_LEN, GMLP_LEN), lambda i: (0, 0, 0)),
                  pl.BlockSpec((GMLP_LEN, GROUP_W), lambda i: (0, 0)),
                  row],
        out_specs=pl.BlockSpec((tm, GROUP_W), lambda i: (i, 0)),
        out_shape=jax.ShapeDtypeStruct((T, GROUP_W), F32),
        compiler_params=_params("parallel"), name="gmlp")(big, big, lg, lb, ws, bst, og)


def _gdn_body(q_ref, k_ref, v_ref, w_ref, gate_ref, gx_ref, ng_ref, o_ref,
              s_scr, u_scr, wq_scr, a_scr, kd_scr, dl_scr, t_scr, p_scr, rhs_scr, halo_scr, qkv_scr, *, nchunk, nb):
    C = CHUNK
    ts = nchunk * C
    first = pl.program_id(1) == 0

    @pl.when(first)
    def _():
        s_scr[...] = jnp.zeros_like(s_scr)

    ones_bd = _head_ones()
    for bb, a in [(bb, a) for bb in range(nb) for a in range(3)]:
        w = w_ref[a]
        x = (q_ref, k_ref, v_ref)[a][bb]
        xx = jnp.concatenate([jnp.where(first, 0.0, halo_scr[bb * 3 + a]), x], axis=0)
        halo_scr[bb * 3 + a] = x[ts - SUBLANES:, :]
        y = w[CONV_K - 1:CONV_K, :] * x
        for j in range(CONV_K - 1):
            y = y + w[j:j + 1, :] * pltpu.roll(xx, CONV_K - 1 - j, 0)[SUBLANES:, :]
        y = y * jax.nn.sigmoid(y)
        if a < 2:
            y = y * lax.rsqrt(_head_sums(y * y, ones_bd) + EPS)
        qkv_scr[bb * 3 + a] = y * (HEAD_DIM ** -0.5) if a == 0 else y

    W = GROUP_W
    pos = lax.broadcasted_iota(I32, (C, W), 1) % HEAD_DIM
    r = lax.broadcasted_iota(I32, (C, W), 0)
    tri, strict, eye = r >= pos, r > pos, r == pos
    same_head = (lax.broadcasted_iota(I32, (W, W), 0) // HEAD_DIM
                 == lax.broadcasted_iota(I32, (W, W), 1) // HEAD_DIM)
    mmb = functools.partial(jnp.dot, preferred_element_type=F32)

    def block_diag(x):
        return jnp.where(same_head, jnp.concatenate([x.astype(BF16)] * N_HEADS, axis=0), 0.0)

    items = nb * nchunk
    for n in range(items):
        bb = n // nchunk
        rows = slice((n % nchunk) * C, (n % nchunk + 1) * C)
        q, k, v = (qkv_scr[bb * 3 + a, rows, :] for a in range(3))
        gc = gx_ref[bb, rows, :W]
        beta = gx_ref[bb, rows, W:]
        gr = jnp.sum(jnp.where(eye, gc, 0.0), axis=0, keepdims=True)
        decay = jnp.exp(jnp.where(tri, gc - gr, -jnp.inf))
        kb = k * beta
        kk = lax.dot_general(jnp.concatenate([kb, q], axis=0).astype(BF16), block_diag(k),
                             (((1,), (1,)), ((), ())), preferred_element_type=F32)
        x = jnp.where(strict, -(kk[:C] * decay), 0.0)
        t_scr[n] = jnp.where(eye, 1.0, 0.0) + x
        p_scr[n] = x.astype(BF16)
        eg = jnp.exp(gc)
        g_last = gc[C - 1:C, :]
        rhs_scr[n, 0] = block_diag(v * beta)
        rhs_scr[n, 1] = block_diag(kb * eg)
        wq_scr[n, C:, :] = (q * eg).astype(BF16)
        a_scr[n] = jnp.where(tri, kk[C:] * decay, 0.0).astype(BF16)
        kd_scr[n] = (k * jnp.exp(g_last - gc)).astype(BF16)
        dl_scr[n] = jnp.exp(g_last)
    for level in range(1, 6):
        for n in range(items):
            p = p_scr[n]
            p_scr[n] = mmb(p, block_diag(p)).astype(BF16)
        for n in range(items):
            t = t_scr[n]
            t_scr[n] = t + mmb(t.astype(BF16), block_diag(p_scr[n]))
    for n in range(items):
        t = t_scr[n].astype(BF16)
        u_scr[n] = mmb(t, rhs_scr[n, 0])
        wq_scr[n, :C, :] = mmb(t, rhs_scr[n, 1]).astype(BF16)

    ones_bd = _head_ones()
    batch = range(nb)
    state = [s_scr[bb] for bb in batch]
    for c in range(nchunk):
        rows = slice(c * C, (c + 1) * C)
        it = [bb * nchunk + c for bb in batch]
        ws = [mmb(wq_scr[it[bb]], state[bb].astype(BF16)) for bb in batch]
        vb = [(u_scr[it[bb]] - ws[bb][:C]).astype(BF16) for bb in batch]
        o = [ws[bb][C:] + mmb(a_scr[it[bb]], block_diag(vb[bb])) for bb in batch]
        kv = [lax.dot_general(kd_scr[it[bb]], vb[bb], (((0,), (0,)), ((), ())), preferred_element_type=F32)
              for bb in batch]
        state = [state[bb] * dl_scr[it[bb]] + jnp.where(same_head, kv[bb], 0.0) for bb in batch]
        for bb in batch:
            gate = gate_ref[bb, rows, :]
            y = o[bb] * lax.rsqrt(_head_sums(o[bb] * o[bb], ones_bd) * (1.0 / HEAD_DIM) + EPS) * ng_ref[...]
            o_ref[bb, rows, :] = y * (gate * jax.nn.sigmoid(gate))
    for bb in batch:
        s_scr[bb] = state[bb]


def _gdn(big3, conv_w3, gx, ng, ts, nb):
    B, S, _ = big3.shape
    nchunk = ts // CHUNK
    col = lambda c: pl.BlockSpec((nb, ts, GROUP_W), lambda b, i: (b, i, c))
    W = GROUP_W
    items = nb * nchunk
    return pl.pallas_call(
        functools.partial(_gdn_body, nchunk=nchunk, nb=nb), grid=(B // nb, S // ts),
        in_specs=[col(COL_DQ), col(COL_DK), col(COL_DV),
                  pl.BlockSpec((3, CONV_K, W), lambda b, i: (0, 0, 0)),
                  col(COL_DG),
                  pl.BlockSpec((nb, ts, 2 * W), lambda b, i: (b, i, 0)),
                  pl.BlockSpec((1, W), lambda b, i: (0, 0))],
        out_specs=col(0), out_shape=jax.ShapeDtypeStruct((B, S, W), F32),
        scratch_shapes=[pltpu.VMEM((nb, W, W), F32),
                        pltpu.VMEM((items, CHUNK, W), F32),
                        pltpu.VMEM((items, 2 * CHUNK, W), BF16),
                        pltpu.VMEM((items, CHUNK, W), BF16),
                        pltpu.VMEM((items, CHUNK, W), BF16),
                        pltpu.VMEM((items, 1, W), F32),
                        pltpu.VMEM((items, CHUNK, W), F32),
                        pltpu.VMEM((items, CHUNK, W), BF16),
                        pltpu.VMEM((items, 2, W, W), BF16),
                        pltpu.VMEM((nb * 3, SUBLANES, W), F32),
                        pltpu.VMEM((nb * 3, ts, W), F32)],
        compiler_params=_params("parallel", "arbitrary"), name="gdn")(
            big3, big3, big3, conv_w3, big3, gx, ng)


def _pool_body(z_ref, halo_ref, w_ref, sc_ref, og_ref, o_ref, *, ts):
    i = pl.program_id(1)
    hr = 2 * SUBLANES
    z = z_ref[...]
    halo = jnp.where(i == 0, 0.0, halo_ref[...])
    s1 = jnp.concatenate([halo, z], axis=0)
    s2 = s1 + pltpu.roll(s1, 1, 0)
    s4 = s2 + pltpu.roll(s2, 2, 0)
    s8 = s4 + pltpu.roll(s4, 4, 0)
    s16 = s8 + pltpu.roll(s8, 8, 0)
    grp = lax.broadcasted_iota(I32, (ts, GROUP_W), 1) // (GROUP_W // len(POOL_WINDOWS))
    t = lax.broadcasted_iota(I32, (ts, GROUP_W), 0) + i * ts
    total = jnp.where(grp == 0, s2[hr:], jnp.where(grp == 1, s4[hr:], jnp.where(grp == 2, s8[hr:], s16[hr:])))
    win = jnp.where(grp == 0, POOL_WINDOWS[0], jnp.where(grp == 1, POOL_WINDOWS[1],
                    jnp.where(grp == 2, POOL_WINDOWS[2], POOL_WINDOWS[3])))
    pooled = total / jnp.minimum(t + 1, win).astype(F32)
    y = _mm(pooled - z, w_ref[...]) * sc_ref[...]
    o_ref[...] = _rms(y, og_ref[...])


def _pool(big3, wbd, sc, og, ts):
    B, S, _ = big3.shape
    hr = 2 * SUBLANES
    hb = ts // hr
    row = pl.BlockSpec((1, GROUP_W), lambda b, i: (0, 0))
    return pl.pallas_call(
        functools.partial(_pool_body, ts=ts), grid=(B, S // ts),
        in_specs=[pl.BlockSpec((None, ts, GROUP_W), lambda b, i: (b, i, COL_PZ)),
                  pl.BlockSpec((None, hr, GROUP_W), lambda b, i: (b, jnp.maximum(i * hb - 1, 0), COL_PZ)),
                  pl.BlockSpec((GROUP_W, GROUP_W), lambda b, i: (0, 0)), row, row],
        out_specs=pl.BlockSpec((None, ts, GROUP_W), lambda b, i: (b, i, 0)),
        out_shape=jax.ShapeDtypeStruct((B, S, GROUP_W), F32),
        compiler_params=_params("parallel", "parallel"), name="pool")(big3, big3, wbd, sc, og)


def _outproj_body(ya_ref, yb_ref, yc_ref, yd_ref, h_ref, wo_ref, g_ref, rw_ref, rb_ref,
                  hnew_ref, hn_ref, ri_ref, rf_ref, cnt_ref, *, tm, tr):
    y = jnp.concatenate([ya_ref[...], yb_ref[...], yc_ref[...], yd_ref[...]], axis=-1).astype(BF16)
    h_new = h_ref[...] + jnp.dot(y, wo_ref[...], preferred_element_type=F32)
    hnew_ref[...] = h_new
    hn = _rms(h_new, g_ref[...])
    hn_hi, hn_lo = _split(hn)
    hn_ref[...] = hn_hi
    t = jnp.dot(hn_hi, rw_ref[...], preferred_element_type=F32)
    logits = (t[:, :LANES] + t[:, LANES:]
              + jnp.dot(hn_lo, rw_ref[:, :LANES], preferred_element_type=F32)) + rb_ref[...]
    lane = lax.broadcasted_iota(I32, (tm, LANES), 1)
    neg = -jnp.inf
    big_lane = LANES

    def masked_top(vals, mask):
        v = jnp.where(mask, vals, neg)
        mx = jnp.max(v, axis=-1, keepdims=True)
        idx = jnp.min(jnp.where(mask & (v == mx), lane, big_lane), axis=-1, keepdims=True)
        return v, mx, idx

    gmask = lane < N_GROUPS
    gv, gmx, gidx = masked_top(logits, gmask)
    g_top = 1.0 / jnp.sum(jnp.where(gmask, jnp.exp(gv - gmx), 0.0), axis=-1, keepdims=True)
    lo = N_GROUPS + gidx * EXPERTS_PER_GROUP
    emask = (lane >= lo) & (lane < lo + EXPERTS_PER_GROUP)
    ev, emx, eidx1 = masked_top(logits, emask)
    esum = jnp.sum(jnp.where(emask, jnp.exp(ev - emx), 0.0), axis=-1, keepdims=True)
    p1 = 1.0 / esum
    _, emx2, eidx2 = masked_top(logits, emask & (lane != eidx1))
    p2 = jnp.exp(emx2 - emx) / esum
    denom = p1 + p2
    rf_ref[...] = jnp.where(lane == 0, g_top * p1 / denom, jnp.where(lane == 1, g_top * p2 / denom, 0.0))

    expert = [eidx1 - N_GROUPS, eidx2 - N_GROUPS]
    hot = [lane == e for e in expert]
    m = (hot[0] | hot[1]).astype(BF16)
    below = (lax.broadcasted_iota(I32, (tr, tr), 0) > lax.broadcasted_iota(I32, (tr, tr), 1)).astype(BF16)
    before = jnp.concatenate(
        [jnp.dot(below, m[k * tr:(k + 1) * tr, :], preferred_element_type=F32) for k in range(tm // tr)], axis=0)
    rank = [jnp.sum(jnp.where(hot[s], before, 0.0), axis=-1, keepdims=True).astype(I32) for s in range(TOPK_IN)]
    out = jnp.zeros((tm, LANES), I32)
    for s in range(TOPK_IN):
        out = jnp.where(lane == s, expert[s], jnp.where(lane == TOPK_IN + s, rank[s], out))
    ri_ref[...] = out
    for k in range(tm // tr):
        last = (k + 1) * tr - 1
        total = before[last:last + 1, :] + m[last:last + 1, :].astype(F32)
        cnt_ref[k * SUBLANES:(k + 1) * SUBLANES, :] = jnp.broadcast_to(total, (SUBLANES, LANES)).astype(I32)


def _outproj(ya, yb, yc, yd, h2d, wo, g, rw, rb, tm, tr):
    T, D = h2d.shape
    yblk = pl.BlockSpec((tm, GROUP_W), lambda i: (i, 0))
    cnt_rows = tm // tr * SUBLANES
    return pl.pallas_call(
        functools.partial(_outproj_body, tm=tm, tr=tr), grid=(T // tm,),
        in_specs=[yblk, yblk, yblk, yblk,
                  pl.BlockSpec((tm, D), lambda i: (i, 0)),
                  pl.BlockSpec((D, D), lambda i: (0, 0)),
                  pl.BlockSpec((1, D), lambda i: (0, 0)),
                  pl.BlockSpec((D, 2 * LANES), lambda i: (0, 0)),
                  pl.BlockSpec((1, LANES), lambda i: (0, 0))],
        out_specs=[pl.BlockSpec((tm, D), lambda i: (i, 0)),
                   pl.BlockSpec((tm, D), lambda i: (i, 0)),
                   pl.BlockSpec((tm, LANES), lambda i: (i, 0)),
                   pl.BlockSpec((tm, LANES), lambda i: (i, 0)),
                   pl.BlockSpec((cnt_rows, LANES), lambda i: (i, 0))],
        out_shape=[jax.ShapeDtypeStruct((T, D), F32),
                   jax.ShapeDtypeStruct((T, D), BF16),
                   jax.ShapeDtypeStruct((T, LANES), I32),
                   jax.ShapeDtypeStruct((T, LANES), F32),
                   jax.ShapeDtypeStruct((T // tr * SUBLANES, LANES), I32)],
        compiler_params=_params("parallel"), name="outproj")(ya, yb, yc, yd, h2d, wo, g, rw, rb)


def _dispatch_plan(expert, lrank, cnt_tile, tm):
    T = expert.shape[0]
    nt = T // tm
    counts = jnp.sum(cnt_tile, axis=0)
    padded = (counts + RUN_ROWS + MOE_BLOCK - 1) // MOE_BLOCK * MOE_BLOCK
    pad_end = jnp.cumsum(padded)
    pad_start = pad_end - padded
    gstart = pad_start[None, :] + jnp.cumsum(cnt_tile, axis=0) - cnt_tile
    nchunk = (cnt_tile + RUN_ROWS - 1) // RUN_ROWS
    chunk_end = jnp.cumsum(nchunk, axis=1)
    lstart = (chunk_end - nchunk) * RUN_ROWS
    onehot = expert[:, :, None] == jnp.arange(N_EXPERTS, dtype=I32)[None, None, :]
    pick = lambda tab: jnp.sum(jnp.where(onehot, jnp.repeat(tab, tm, axis=0)[:, None, :], 0), axis=-1)
    lpos =(lrank + pick(lstart)).astype(I32)
    max_chunks = tm * TOPK_IN // RUN_ROWS + N_EXPERTS
    c = jnp.arange(max_chunks, dtype=I32)
    ce = jnp.minimum(jnp.sum(chunk_end[:, None, :] <= c[None, :, None], axis=-1), N_EXPERTS - 1)
    ce_hot = ce[:, :, None] == jnp.arange(N_EXPERTS, dtype=I32)[None, None, :]
    take = lambda tab: jnp.sum(jnp.where(ce_hot, tab[:, None, :], 0), axis=-1)
    chunk_row = take(gstart) + (c[None, :] - take(chunk_end - nchunk)) * RUN_ROWS
    chunk_row = jnp.where(c[None, :] < chunk_end[:, -1:], chunk_row, 0)
    table = jnp.concatenate([chunk_row.astype(I32), jnp.zeros((nt, LANES - 1 - max_chunks), I32),
                             chunk_end[:, -1:].astype(I32)], axis=1).reshape(nt, 1, LANES)
    n_blk = -(-(T * TOPK_IN + N_EXPERTS * RUN_ROWS) // MOE_BLOCK) + N_EXPERTS + 1
    blk_start = jnp.arange(n_blk, dtype=I32) * MOE_BLOCK
    blk_e = jnp.minimum(jnp.sum(pad_end[None, :] <= blk_start[:, None], axis=-1), N_EXPERTS - 1).astype(I32)
    n_used = (pad_end[-1] // MOE_BLOCK).astype(I32).reshape(1)
    zero_start = jnp.concatenate([pad_start + counts, pad_end[-1:]])
    zero_end = jnp.concatenate([pad_end, jnp.full((1,), n_blk * MOE_BLOCK, pad_end.dtype)])
    nz = N_EXPERTS + 1
    zfill = jnp.concatenate([zero_start, (zero_end - zero_start) // RUN_ROWS, zero_end - RUN_ROWS,
                             jnp.zeros((LANES - 3 * nz,), zero_start.dtype)]).astype(I32).reshape(1, LANES)
    return lpos, table, zfill, blk_e, n_used, n_blk


def _dispatch_body(zf_ref, tab_ref, lpos_ref, hn_ref, x_hbm, xs, zero, sem, zsem, n_prev, *, tm, nseg, nrow):
    i = pl.program_id(0)
    slot = i % 2
    step = RUN_ROWS * nseg

    @pl.when(i == 0)
    def _():
        zero[...] = jnp.zeros_like(zero)

        def zero_copy(r):
            return pltpu.make_async_copy(zero, x_hbm.at[pl.ds(pl.multiple_of(r * nseg, nseg), step), :], zsem)

        nz = N_EXPERTS + 1

        def for_chunks(fn):
            for e in range(nz):
                def body(k, carry, e=e):
                    fn(zero_copy(zf_ref[0, e] + k * RUN_ROWS))
                    return carry
                lax.fori_loop(0, zf_ref[0, nz + e], body, 0)

        for_chunks(lambda c: c.start())
        for_chunks(lambda c: c.wait())
        for e in range(nz):
            zero_copy(zf_ref[0, 2 * nz + e]).start()
        for e in range(nz):
            zero_copy(zf_ref[0, 2 * nz + e]).wait()

    row = lax.broadcasted_iota(I32, (nrow, tm), 0)
    sel = (row == lpos_ref[0:1, :]) | (row == lpos_ref[1:2, :])
    rows = jnp.dot(sel.astype(BF16), hn_ref[...], preferred_element_type=F32)
    for k in range(nseg):
        xs[slot, pl.ds(k, nrow, stride=nseg), :] = rows[:, k * LANES:(k + 1) * LANES]

    def chunk_copy(sl, c):
        return pltpu.make_async_copy(
            xs.at[sl, pl.ds(pl.multiple_of(c * step, step), step), :],
            x_hbm.at[pl.ds(pl.multiple_of(tab_ref[0, 0, c] * nseg, nseg), step), :], sem.at[sl])

    def wait_chunks(sl, n):
        def wait(c, carry):
            chunk_copy(sl, 0).wait()
            return carry
        lax.fori_loop(0, n, wait, 0)

    n_chunks = tab_ref[0, 0, LANES - 1]

    @pl.when(i > 0)
    def _():
        wait_chunks(1 - slot, n_prev[0])

    def start(c, carry):
        chunk_copy(slot, c).start()
        return carry

    lax.fori_loop(0, n_chunks, start, 0)
    n_prev[0] = n_chunks

    @pl.when(i == pl.num_programs(0) - 1)
    def _():
        wait_chunks(slot, n_chunks)


def _dispatch(zfill, table, lpos_t, hn, n_pad, tm):
    T, D = hn.shape
    nseg = D // LANES
    nrow = tm * TOPK_IN + N_EXPERTS * RUN_ROWS
    return pl.pallas_call(
        functools.partial(_dispatch_body, tm=tm, nseg=nseg, nrow=nrow), grid=(T // tm,),
        in_specs=[pl.BlockSpec(memory_space=pltpu.SMEM),
                  pl.BlockSpec((1, 1, LANES), lambda i: (i, 0, 0), memory_space=pltpu.SMEM),
                  pl.BlockSpec((None, TOPK_IN, tm), lambda i: (i, 0, 0)),
                  pl.BlockSpec((tm, D), lambda i: (i, 0))],
        out_specs=pl.BlockSpec(memory_space=pl.ANY),
        out_shape=jax.ShapeDtypeStruct((n_pad * nseg, LANES), F32),
        scratch_shapes=[pltpu.VMEM((2, nrow * nseg, LANES), F32), pltpu.VMEM((RUN_ROWS * nseg, LANES), F32),
                        pltpu.SemaphoreType.DMA((2,)), pltpu.SemaphoreType.DMA, pltpu.SMEM((1,), I32)],
        compiler_params=_params("arbitrary"), name="dispatch")(zfill, table, lpos_t, hn)


def _moe_body(blk_e_ref, n_used_ref, seg_ref, nxt_ref, x_ref, w1_hbm, w3_hbm, w2_hbm, y_ref,
              wf1, wf3, wf2, w1b, w3b, w2b, sem, *, nseg, layer):
    b = pl.program_id(0)
    R = MOE_BLOCK

    def fetch(e, slot):
        return [pltpu.make_async_copy(src.at[layer, e], dst.at[slot], sem.at[slot, k])
                for k, (src, dst) in enumerate(((w1_hbm, wf1), (w3_hbm, wf3), (w2_hbm, wf2)))]

    @pl.when(b < n_used_ref[0])
    def _():
        @pl.when((b == 0) | (blk_e_ref[b] != blk_e_ref[jnp.maximum(b - 1, 0)]))
        def _():
            e = blk_e_ref[b]
            slot = seg_ref[b] % 2

            @pl.when(b == 0)
            def _():
                for c in fetch(e, slot):
                    c.start()

            for c in fetch(e, slot):
                c.wait()
            w1b[...] = wf1[slot].astype(BF16)
            w3b[...] = wf3[slot].astype(BF16)
            w2b[...] = wf2[slot].astype(BF16)

            @pl.when(nxt_ref[b] >= 0)
            def _():
                for c in fetch(nxt_ref[b], 1 - slot):
                    c.start()

        x = jnp.concatenate([x_ref[pl.ds(s, R, stride=nseg), :] for s in range(nseg)], axis=-1).astype(BF16)
        a = jnp.dot(x, w1b[...], preferred_element_type=F32)
        g = jnp.dot(x, w3b[...], preferred_element_type=F32)
        hid = (a * jax.nn.sigmoid(a) * g).astype(BF16)
        y = jnp.dot(hid, w2b[...], preferred_element_type=F32)
        for s in range(nseg):
            y_ref[pl.ds(s, R, stride=nseg), :] = y[:, s * LANES:(s + 1) * LANES]

    @pl.when(b >= n_used_ref[0])
    def _():
        y_ref[...] = jnp.zeros_like(y_ref)


def _moe(blk_e, n_used, x_rows, w1, w3, w2, layer):
    n_blk = blk_e.shape[0]
    _, _, D, DE = w1.shape
    nseg = D // LANES
    R = MOE_BLOCK
    blk = jnp.arange(n_blk, dtype=I32)
    used = blk < n_used[0]
    change = (blk > 0) & (blk_e != jnp.roll(blk_e, 1)) & used
    seg = jnp.cumsum(change.astype(I32)).astype(I32)
    later = (blk[None, :] > blk[:, None]) & (seg[None, :] > seg[:, None]) & used[None, :]
    nxt_blk = jnp.min(jnp.where(later, blk[None, :], n_blk), axis=1)
    nxt = jnp.where(nxt_blk < n_blk, blk_e[jnp.minimum(nxt_blk, n_blk - 1)], -1).astype(I32)
    rows = lambda b, be, nu, sg, nx: (jnp.minimum(b, nu[0] - 1), 0)
    grid_spec = pltpu.PrefetchScalarGridSpec(
        num_scalar_prefetch=4, grid=(n_blk,),
        in_specs=[pl.BlockSpec((R * nseg, LANES), rows),
                  pl.BlockSpec(memory_space=pl.ANY), pl.BlockSpec(memory_space=pl.ANY),
                  pl.BlockSpec(memory_space=pl.ANY)],
        out_specs=pl.BlockSpec((R * nseg, LANES), lambda b, be, nu, sg, nx: (b, 0)),
        scratch_shapes=[pltpu.VMEM((2, D, DE), F32), pltpu.VMEM((2, D, DE), F32), pltpu.VMEM((2, DE, D), F32),
                        pltpu.VMEM((D, DE), BF16), pltpu.VMEM((D, DE), BF16), pltpu.VMEM((DE, D), BF16),
                        pltpu.SemaphoreType.DMA((2, 3))])
    return pl.pallas_call(
        functools.partial(_moe_body, nseg=nseg, layer=layer), grid_spec=grid_spec,
        out_shape=jax.ShapeDtypeStruct(x_rows.shape, F32),
        compiler_params=_params("arbitrary"), name="moe")(blk_e, n_used, seg, nxt, x_rows, w1, w3, w2)


def _combine_body(tab_ref, tabn_ref, h_ref, rf_ref, lpos_ref, y_hbm, o_ref, ybuf, sem, *, tm, nseg, nrow):
    i = pl.program_id(0)
    slot = i % 2
    step = RUN_ROWS * nseg

    def start_chunks(tab, sl):
        def start(c, carry):
            pltpu.make_async_copy(
                y_hbm.at[pl.ds(pl.multiple_of(tab[0, 0, c] * nseg, nseg), step), :],
                ybuf.at[sl, pl.ds(pl.multiple_of(c * step, step), step), :], sem.at[sl]).start()
            return carry
        lax.fori_loop(0, tab[0, 0, LANES - 1], start, 0)

    @pl.when(i == 0)
    def _():
        ybuf[...] = jnp.zeros_like(ybuf)
        start_chunks(tab_ref, 0)

    @pl.when(i + 1 < pl.num_programs(0))
    def _():
        start_chunks(tabn_ref, 1 - slot)

    col = lax.broadcasted_iota(I32, (tm, nrow), 1)
    sel = jnp.zeros((tm, nrow), F32)
    for s in range(TOPK_IN):
        sel = sel + jnp.where(col == lpos_ref[:, s:s + 1], rf_ref[:, s:s + 1], 0.0)

    def wait(c, carry):
        pltpu.make_async_copy(y_hbm.at[pl.ds(0, step), :], ybuf.at[slot, pl.ds(0, step), :],
                              sem.at[slot]).wait()
        return carry

    lax.fori_loop(0, tab_ref[0, 0, LANES - 1], wait, 0)
    y = jnp.concatenate([ybuf[slot, pl.ds(k, nrow, stride=nseg), :] for k in range(nseg)], axis=-1)
    o_ref[...] = h_ref[...] + jnp.dot(sel.astype(BF16), y.astype(BF16), preferred_element_type=F32)


def _combine(table, h2d, rf, lpos, y_rows, tm):
    T, D = h2d.shape
    nseg = D // LANES
    nt = T // tm
    nrow = tm * TOPK_IN + N_EXPERTS * RUN_ROWS
    return pl.pallas_call(
        functools.partial(_combine_body, tm=tm, nseg=nseg, nrow=nrow), grid=(nt,),
        in_specs=[pl.BlockSpec((1, 1, LANES), lambda i: (i, 0, 0), memory_space=pltpu.SMEM),
                  pl.BlockSpec((1, 1, LANES), lambda i: (jnp.minimum(i + 1, nt - 1), 0, 0),
                               memory_space=pltpu.SMEM),
                  pl.BlockSpec((tm, D), lambda i: (i, 0)),
                  pl.BlockSpec((tm, LANES), lambda i: (i, 0)),
                  pl.BlockSpec((tm, TOPK_IN), lambda i: (i, 0)),
                  pl.BlockSpec(memory_space=pl.ANY)],
        out_specs=pl.BlockSpec((tm, D), lambda i: (i, 0)),
        out_shape=jax.ShapeDtypeStruct((T, D), F32),
        scratch_shapes=[pltpu.VMEM((2, nrow * nseg, LANES), F32), pltpu.SemaphoreType.DMA((2,))],
        compiler_params=_params("arbitrary"), name="combine")(table, table, h2d, rf, lpos, y_rows)


def _pad_lanes(a, lane0, rows=1):
    a = a.reshape(rows, -1)
    return jnp.pad(a, ((0, 0), (lane0, LANES - lane0 - a.shape[-1])))


def _layer(h2d, B, S, p):
    T, D = h2d.shape
    tile = lambda a, n: jnp.tile(a.reshape(1, -1), (1, n))

    offs = np.cumsum([0, GROUP_W, GROUP_W, GROUP_W, N_HEADS, GROUP_W, GROUP_W,
                      GROUP_W, GROUP_W, GROUP_W, N_HEADS, N_HEADS, GROUP_W, GROUP_W])
    seg = lambda k: p['w_in'][:, offs[k]:offs[k + 1]]
    w_all = jnp.concatenate([seg(0), seg(1), seg(2), seg(4), seg(5), seg(6), seg(7), seg(8), seg(11), seg(12),
                             seg(3), seg(9), seg(10), jnp.zeros((D, LANES - 3 * N_HEADS), F32)],
                            axis=1).astype(BF16)
    gate_prm = jnp.concatenate([_pad_lanes(p['fox_f_bias'], LANE_FOX), _pad_lanes(p['gdn_dt_bias'], LANE_DECAY),
                                _pad_lanes(p['gdn_a_log'], LANE_DECAY), jnp.zeros((SUBLANES - 3, LANES), F32)], axis=0)

    big, small = _inproj(h2d, p['attn_norm_g'].reshape(1, D), w_all, tm=min(512, T))
    big3 = big.reshape(B, S, N_BIG_COLS * GROUP_W)
    ts = min(512, S)
    gcol, gx = _gates(small.reshape(B, S, LANES), gate_prm, ts)

    tq = min(256, S)
    ya = _fox(big3, gcol, tile(p['fox_qn_g'], N_HEADS), tile(p['fox_kn_g'], N_HEADS),
              p['fox_out_g'].reshape(1, GROUP_W), tq)

    bst = jnp.repeat(p['gmlp_bs'].T, HEAD_DIM, axis=1)
    yb = _gmlp(big, p['gmlp_ln_g'].reshape(1, -1), p['gmlp_ln_b'].reshape(1, -1), p['gmlp_ws'], bst,
               p['gmlp_out_g'].reshape(1, -1), tm=min(512, T))

    conv_w3 = p['gdn_conv_w'].reshape(CONV_K, 3, GROUP_W).transpose(1, 0, 2)
    nb = 4 if B % 4 == 0 else (2 if B % 2 == 0 else 1)
    yc = _gdn(big3, conv_w3, gx, tile(p['gdn_norm_g'], N_HEADS), min(1024 // nb, S), nb)

    wbd = jax.scipy.linalg.block_diag(*[p['pool_w'][g] for g in range(len(POOL_WINDOWS))]).astype(BF16)
    yd = _pool(big3, wbd, p['pool_scale'].reshape(1, -1), p['pool_out_g'].reshape(1, -1), ts)

    rw = _pad_lanes(jnp.concatenate([p['router_g_w'], p['router_e_w']], axis=1), 0, rows=D)
    rw = jnp.concatenate(_split(rw), axis=1)
    rb = _pad_lanes(jnp.concatenate([p['router_g_b'], p['router_e_b']]), 0)
    flat = lambda a: a.reshape(T, GROUP_W)
    tmd = min(256, T)
    h_new, hn_rows, ri, rf, cnt = _outproj(flat(ya), yb, flat(yc), flat(yd), h2d, p['w_out'].astype(BF16),
                                           p['ffn_norm_g'].reshape(1, D), rw, rb, tm=min(512, T), tr=tmd)
    cnt_tile = cnt.reshape(T // tmd, SUBLANES, LANES)[:, 0, :N_EXPERTS]
    lpos, table, zfill, blk_e, n_used, n_blk = _dispatch_plan(
        ri[:, :TOPK_IN], ri[:, TOPK_IN:2 * TOPK_IN], cnt_tile, tmd)
    lpos_t = lpos.reshape(T // tmd, tmd, TOPK_IN).transpose(0, 2, 1)
    x_rows = _dispatch(zfill, table, lpos_t, hn_rows, n_blk * MOE_BLOCK, tmd)
    y_rows = _moe(blk_e, n_used, x_rows, p['moe_w1'], p['moe_w3'], p['moe_w2'], p['layer'])
    return _combine(table, h_new, rf, lpos, y_rows, tmd)


def kernel(x, attn_norm_g, w_in, w_out, fox_f_bias, fox_qn_g, fox_kn_g, fox_out_g, gmlp_ln_g, gmlp_ln_b, gmlp_ws, gmlp_bs, gmlp_out_g, gdn_conv_w, gdn_a_log, gdn_dt_bias, gdn_norm_g, pool_w, pool_scale, pool_out_g, ffn_norm_g, router_g_w, router_g_b, router_e_w, router_e_b, moe_w1, moe_w3, moe_w2):
    B, S, D = x.shape
    names = ('attn_norm_g', 'w_in', 'w_out', 'fox_f_bias', 'fox_qn_g', 'fox_kn_g', 'fox_out_g', 'gmlp_ln_g',
             'gmlp_ln_b', 'gmlp_ws', 'gmlp_bs', 'gmlp_out_g', 'gdn_conv_w', 'gdn_a_log', 'gdn_dt_bias',
             'gdn_norm_g', 'pool_w', 'pool_scale', 'pool_out_g', 'ffn_norm_g', 'router_g_w', 'router_g_b',
             'router_e_w', 'router_e_b', 'moe_w1', 'moe_w3', 'moe_w2')
    vals = (attn_norm_g, w_in, w_out, fox_f_bias, fox_qn_g, fox_kn_g, fox_out_g, gmlp_ln_g, gmlp_ln_b, gmlp_ws,
            gmlp_bs, gmlp_out_g, gdn_conv_w, gdn_a_log, gdn_dt_bias, gdn_norm_g, pool_w, pool_scale, pool_out_g,
            ffn_norm_g, router_g_w, router_g_b, router_e_w, router_e_b, moe_w1, moe_w3, moe_w2)
    h = x.reshape(B * S, D)
    stacked = ('moe_w1', 'moe_w3', 'moe_w2')
    for l in range(w_in.shape[0]):
        p = {n: (v if n in stacked else v[l]) for n, v in zip(names, vals)}
        p['layer'] = l
        h = _layer(h, B, S, p)
    return h.reshape(B, S, D)
```

```python
import functools

import jax
import jax.numpy as jnp
import numpy as np
from jax import lax
from jax.experimental import pallas as pl
from jax.experimental.pallas import tpu as pltpu

F32 = jnp.float32
BF16 = jnp.bfloat16
I32 = jnp.int32

EPS = 1e-6
HEAD_DIM = 64
GROUP_W = 256
N_HEADS = GROUP_W // HEAD_DIM
CHUNK = 64
GMLP_LEN = 128
CONV_K = 4
POOL_WINDOWS = (2, 4, 8, 16)
N_GROUPS = 4
EXPERTS_PER_GROUP = 8
N_EXPERTS = N_GROUPS * EXPERTS_PER_GROUP
TOPK_IN = 2
MOE_BLOCK = 256
FOX_VT_ROWS = HEAD_DIM + 16
FOX_BOUND_LOG2 = 40.0
FOX_SKIP_LOG2 = 160.0
RUN_ROWS = 16
LANES = 128
SUBLANES = 8
VMEM_LIMIT = 56 * 1024 * 1024

COL_FQ, COL_FK, COL_FV, COL_GU, COL_GV, COL_DQ, COL_DK, COL_DV, COL_DG, COL_PZ = range(10)
N_BIG_COLS = 10
LANE_FOX, LANE_DECAY, LANE_BETA = 0, 4, 8


def _params(*sem):
    return pltpu.CompilerParams(dimension_semantics=sem, vmem_limit_bytes=VMEM_LIMIT)


def _head_ones():
    r = lax.broadcasted_iota(I32, (GROUP_W, GROUP_W), 0) // HEAD_DIM
    c = lax.broadcasted_iota(I32, (GROUP_W, GROUP_W), 1) // HEAD_DIM
    return (r == c).astype(BF16)


def _head_sums(x, ones_bd):
    hi = x.astype(BF16)
    lo = (x - hi.astype(F32)).astype(BF16)
    return (jnp.dot(hi, ones_bd, preferred_element_type=F32)
            + jnp.dot(lo, ones_bd, preferred_element_type=F32))


def _rms(x, g):
    return x * lax.rsqrt(jnp.mean(x * x, axis=-1, keepdims=True) + EPS) * g


def _mm(a, b):
    return jnp.dot(a.astype(BF16), b.astype(BF16), preferred_element_type=F32)


def _mm_nt(a, b):
    return lax.dot_general(a.astype(BF16), b.astype(BF16), (((1,), (1,)), ((), ())),
                           preferred_element_type=F32)


def _mm_tn(a, b):
    return lax.dot_general(a.astype(BF16), b.astype(BF16), (((0,), (0,)), ((), ())),
                           preferred_element_type=F32)


def _split(a):
    hi = a.astype(BF16)
    return hi, (a - hi.astype(F32)).astype(BF16)


def _mm3(a, b):
    ah, al = _split(a)
    bh, bl = _split(b)
    d = functools.partial(jnp.dot, preferred_element_type=F32)
    return d(ah, bh) + (d(ah, bl) + d(al, bh))


def _inproj_body(x_ref, g_ref, w_ref, big_ref, small_ref):
    xn = _rms(x_ref[...], g_ref[...]).astype(BF16)
    nb = big_ref.shape[1]
    big_ref[...] = jnp.dot(xn, w_ref[:, :nb], preferred_element_type=F32).astype(big_ref.dtype)
    small_ref[...] = jnp.dot(xn, w_ref[:, nb:], preferred_element_type=F32)


def _inproj(x2d, g, w, tm):
    T, D = x2d.shape
    nb = w.shape[1] - LANES
    return pl.pallas_call(
        _inproj_body, grid=(T // tm,),
        in_specs=[pl.BlockSpec((tm, D), lambda i: (i, 0)),
                  pl.BlockSpec((1, D), lambda i: (0, 0)),
                  pl.BlockSpec((D, nb + LANES), lambda i: (0, 0))],
        out_specs=[pl.BlockSpec((tm, nb), lambda i: (i, 0)),
                   pl.BlockSpec((tm, LANES), lambda i: (i, 0))],
        out_shape=[jax.ShapeDtypeStruct((T, nb), BF16), jax.ShapeDtypeStruct((T, LANES), F32)],
        compiler_params=_params("parallel"), name="inproj")(x2d, g, w)


def _gates_body(sm_ref, p_ref, col_ref, exp_ref, carry_ref, *, ts):
    @pl.when(pl.program_id(1) == 0)
    def _():
        carry_ref[...] = jnp.zeros_like(carry_ref)

    x = sm_ref[...]
    lane = lax.broadcasted_iota(I32, (ts, LANES), 1)
    is_fox = lane < LANE_DECAY
    is_dec = (lane >= LANE_DECAY) & (lane < LANE_BETA)
    is_beta = (lane >= LANE_BETA) & (lane < LANE_BETA + N_HEADS)
    logf = jax.nn.log_sigmoid(x + p_ref[0:1, :])
    g = -jnp.exp(p_ref[2:3, :]) * jax.nn.softplus(x + p_ref[1:2, :])
    beta = jax.nn.sigmoid(x)
    r = lax.broadcasted_iota(I32, (ts, ts), 0)
    c = lax.broadcasted_iota(I32, (ts, ts), 1)
    tri_full = (r >= c).astype(BF16)
    tri_chunk = ((r >= c) & (r // CHUNK == c // CHUNK)).astype(BF16)
    vals = jnp.where(is_fox, logf, jnp.where(is_dec, g, 0.0))
    hi = vals.astype(BF16)
    mid = (vals - hi.astype(F32)).astype(BF16)
    lo = (vals - hi.astype(F32) - mid.astype(F32)).astype(BF16)
    parts = jnp.concatenate([hi, mid, lo], axis=1)

    def tri_sum(tri):
        t = jnp.dot(tri, parts, preferred_element_type=F32)
        return t[:, :LANES] + (t[:, LANES:2 * LANES] + t[:, 2 * LANES:])

    cf = tri_sum(tri_full) + carry_ref[...]
    cg = tri_sum(tri_chunk)
    carry_ref[...] = cf[ts - 1:ts, :]
    out = jnp.where(is_fox, cf, jnp.where(is_dec, cg, jnp.where(is_beta, beta, 0.0)))
    col_ref[...] = out
    o_hi = out.astype(BF16)
    o_mid = (out - o_hi.astype(F32)).astype(BF16)
    o_lo = (out - o_hi.astype(F32) - o_mid.astype(F32)).astype(BF16)
    src = lax.broadcasted_iota(I32, (3 * LANES, 2 * GROUP_W), 0) % LANES
    dst = lax.broadcasted_iota(I32, (3 * LANES, 2 * GROUP_W), 1)
    want = jnp.where(dst < GROUP_W, LANE_DECAY, LANE_BETA) + (dst % GROUP_W) // HEAD_DIM
    exp_ref[...] = jnp.dot(jnp.concatenate([o_hi, o_mid, o_lo], axis=1), (src == want).astype(BF16),
                           preferred_element_type=F32)


def _gates(small3, prm, ts):
    B, S, _ = small3.shape
    return pl.pallas_call(
        functools.partial(_gates_body, ts=ts), grid=(B, S // ts),
        in_specs=[pl.BlockSpec((None, ts, LANES), lambda b, j: (b, j, 0)),
                  pl.BlockSpec((SUBLANES, LANES), lambda b, j: (0, 0))],
        out_specs=[pl.BlockSpec((None, ts, LANES), lambda b, j: (b, j, 0)),
                   pl.BlockSpec((None, ts, 2 * GROUP_W), lambda b, j: (b, j, 0))],
        out_shape=[jax.ShapeDtypeStruct((B, S, LANES), F32),
                   jax.ShapeDtypeStruct((B, S, 2 * GROUP_W), F32)],
        scratch_shapes=[pltpu.VMEM((1, LANES), F32)],
        compiler_params=_params("parallel", "arbitrary"), name="gates")(small3, prm)


def _split3_lanes(x, lane, lane0):
    hi = x.astype(BF16).astype(F32)
    mid = (x - hi).astype(BF16).astype(F32)
    lo = (x - hi - mid).astype(BF16).astype(F32)
    return jnp.where(lane == lane0, hi, jnp.where(lane == lane0 + 1, mid, jnp.where(lane == lane0 + 2, lo, 0.0)))


def _fox_body(jstart_ref, q_ref, k_ref, v_ref, ccol_ref, qg_ref, kg_ref, og_ref, o_ref,
              kn_scr, vt_scr, q_scr, m_scr, l_scr, acc_scr, s_scr, kmax_scr, *, tq, nk):
    b = pl.program_id(0)
    i = pl.program_id(1)
    ones_bd = _head_ones()
    lane = lax.broadcasted_iota(I32, (tq, LANES), 1)
    log2e = 1.0 / np.log(2.0)
    c_lane, r_lane = HEAD_DIM, HEAD_DIM + 3

    def head_norm(x, g):
        ss = _head_sums(x * x, ones_bd)
        return x * lax.rsqrt(ss * (1.0 / HEAD_DIM) + EPS) * g

    def head_tile(x, h, extra):
        pair = x[:, (h // 2) * LANES:(h // 2 + 1) * LANES]
        if h % 2:
            pair = pltpu.roll(pair, HEAD_DIM, 1)
        return jnp.where(lane < HEAD_DIM, pair, extra).astype(BF16)

    def sq_norms(x):
        xr = x.astype(BF16).astype(F32)
        return _head_sums(xr * xr, ones_bd)

    @pl.when(i == 0)
    def _():
        k_ones = jnp.where((lane >= r_lane) & (lane < r_lane + 3), 1.0, 0.0)
        vt_tail = (lax.broadcasted_iota(I32, (FOX_VT_ROWS - HEAD_DIM, tq), 0) == 0).astype(BF16)
        kmax = jnp.zeros((1, GROUP_W), F32)
        for c in range(nk):
            rows = slice(c * tq, (c + 1) * tq)
            kc = head_norm(k_ref[rows, :].astype(F32), kg_ref[...])
            kmax = jnp.maximum(kmax, jnp.max(sq_norms(kc), axis=0, keepdims=True))
            vt = v_ref[rows, :].astype(F32).T.astype(BF16)
            cc = ccol_ref[rows, :] * (-log2e)
            for h in range(N_HEADS):
                extra = _split3_lanes(cc[:, LANE_FOX + h:LANE_FOX + h + 1], lane, c_lane) + k_ones
                kn_scr[h, c] = head_tile(kc, h, extra)
                vt_scr[h, c] = jnp.concatenate([vt[h * HEAD_DIM:(h + 1) * HEAD_DIM, :], vt_tail], axis=0)
        kmax_scr[...] = kmax

    qn = head_norm(q_ref[...].astype(F32), qg_ref[...]) * (HEAD_DIM ** -0.5 * log2e)
    bound = jnp.sqrt(sq_norms(qn) * kmax_scr[...]) * 1.001
    bounded = jnp.max(bound) <= FOX_BOUND_LOG2
    c_i = ccol_ref[pl.ds(pl.multiple_of(i * tq, tq), tq), :] * log2e
    q_ones = jnp.where((lane >= c_lane) & (lane < c_lane + 3), 1.0, 0.0)
    for h in range(N_HEADS):
        r_i = bound[:, h * HEAD_DIM:h * HEAD_DIM + 1] - c_i[:, LANE_FOX + h:LANE_FOX + h + 1]
        neg_r = jnp.where(bounded, -r_i, 0.0)
        q_scr[h] = head_tile(qn, h, q_ones + _split3_lanes(neg_r, lane, r_lane))
    acc_scr[...] = jnp.zeros_like(acc_scr)
    causal = (lax.broadcasted_iota(I32, (tq, tq), 0) <= lax.broadcasted_iota(I32, (tq, tq), 1))

    heads = range(N_HEADS)

    def scores(j):
        return [lax.dot_general(kn_scr[h, j], q_scr[h], (((1,), (1,)), ((), ())),
                                preferred_element_type=F32) for h in heads]

    def stash(s):
        for h in heads:
            s_scr[h] = s[h]

    def absorb_general(j, masked):
        p, alpha = [], []
        for h in heads:
            s = s_scr[h]
            if masked:
                s = jnp.where(causal, s, -jnp.inf)
            m_old = m_scr[h]
            m_new = jnp.maximum(m_old, jnp.max(s, axis=0, keepdims=True))
            alpha.append(jnp.exp2(m_old - m_new))
            ph = jnp.exp2(s - m_new)
            l_scr[h] = alpha[h] * l_scr[h] + jnp.sum(ph, axis=0, keepdims=True)
            m_scr[h] = m_new
            p.append(ph.astype(BF16))
        pv = [jnp.dot(vt_scr[h, j], p[h], preferred_element_type=F32) for h in heads]
        for h in heads:
            acc_scr[h] = alpha[h] * acc_scr[h] + pv[h]

    def absorb_bounded(j, masked):
        p = []
        for h in heads:
            s = s_scr[h]
            if masked:
                s = jnp.where(causal, s, -jnp.inf)
            p.append(jnp.exp2(s).astype(BF16))
        pv = [jnp.dot(vt_scr[h, j], p[h], preferred_element_type=F32) for h in heads]
        for h in heads:
            acc_scr[h] += pv[h]

    def run(absorb, j0):
        stash(scores(j0))

        def body(j, c):
            s_next = scores(j + 1)
            absorb(j, False)
            stash(s_next)
            return c

        lax.fori_loop(j0, i, body, 0)
        absorb(i, True)

    @pl.when(bounded)
    def _():
        run(absorb_bounded, jstart_ref[b * nk + i])

    @pl.when(jnp.logical_not(bounded))
    def _():
        m_scr[...] = jnp.full_like(m_scr, -jnp.inf)
        l_scr[...] = jnp.zeros_like(l_scr)
        run(absorb_general, 0)
        for h in heads:
            acc_scr[h, HEAD_DIM:HEAD_DIM + 1, :] = l_scr[h]

    o_t = jnp.concatenate([acc_scr[h, :HEAD_DIM, :] / acc_scr[h, HEAD_DIM:HEAD_DIM + 1, :] for h in heads],
                          axis=0)
    o_ref[...] = _rms(o_t.T, og_ref[...])


def _fox_first_block(gcol, tq):
    B, S, _ = gcol.shape
    nk = S // tq
    c2 = gcol[:, :, LANE_FOX:LANE_FOX + N_HEADS] * (1.0 / np.log(2.0))
    first = c2[:, 0::tq, :]
    last = c2[:, tq - 1::tq, :]
    dead = (first[:, :, None, :] - last[:, None, :, :]) < -FOX_SKIP_LOG2
    dead = dead & (jnp.arange(nk)[None, :, None, None] > jnp.arange(nk)[None, None, :, None])
    return jnp.min(jnp.sum(dead, axis=2), axis=-1).astype(I32).reshape(B * nk)


def _fox(big3, gcol, qg, kg, og, tq):
    B, S, _ = big3.shape
    nk = S // tq
    row = pl.BlockSpec((1, GROUP_W), lambda b, i, js: (0, 0))
    grid_spec = pltpu.PrefetchScalarGridSpec(
        num_scalar_prefetch=1, grid=(B, nk),
        in_specs=[pl.BlockSpec((None, tq, GROUP_W), lambda b, i, js: (b, i, COL_FQ)),
                  pl.BlockSpec((None, S, GROUP_W), lambda b, i, js: (b, 0, COL_FK)),
                  pl.BlockSpec((None, S, GROUP_W), lambda b, i, js: (b, 0, COL_FV)),
                  pl.BlockSpec((None, S, LANES), lambda b, i, js: (b, 0, 0)),
                  row, row, row],
        out_specs=pl.BlockSpec((None, tq, GROUP_W), lambda b, i, js: (b, i, 0)),
        scratch_shapes=[pltpu.VMEM((N_HEADS, nk, tq, LANES), BF16),
                        pltpu.VMEM((N_HEADS, nk, FOX_VT_ROWS, tq), BF16),
                        pltpu.VMEM((N_HEADS, tq, LANES), BF16),
                        pltpu.VMEM((N_HEADS, 1, tq), F32),
                        pltpu.VMEM((N_HEADS, 1, tq), F32),
                        pltpu.VMEM((N_HEADS, FOX_VT_ROWS, tq), F32),
                        pltpu.VMEM((N_HEADS, tq, tq), F32),
                        pltpu.VMEM((1, GROUP_W), F32)])
    return pl.pallas_call(
        functools.partial(_fox_body, tq=tq, nk=nk), grid_spec=grid_spec,
        out_shape=jax.ShapeDtypeStruct((B, S, GROUP_W), F32),
        compiler_params=_params("parallel", "arbitrary"), name="fox")(
            _fox_first_block(gcol, tq), big3, big3, big3, gcol, qg, kg, og)


def _gelu(x):
    return 0.5 * x * (1.0 + lax.erf(x * (2.0 ** -0.5)))


def _gmlp_body(u_ref, v_ref, lg_ref, lb_ref, ws_ref, bst_ref, og_ref, o_ref, *, nwin):
    L = GMLP_LEN
    r = lax.broadcasted_iota(I32, (L, L), 0) // CHUNK
    c = lax.broadcasted_iota(I32, (L, L), 1) // CHUNK
    mask = r >= c
    ws = [jnp.where(mask, ws_ref[h], 0.0).astype(BF16) for h in range(N_HEADS)]
    for n in range(nwin):
        u = _gelu(u_ref[n * L:(n + 1) * L, :].astype(F32))
        v = _gelu(v_ref[n * L:(n + 1) * L, :].astype(F32))
        mu = jnp.mean(v, axis=-1, keepdims=True)
        vc = v - mu
        var = jnp.mean(vc * vc, axis=-1, keepdims=True)
        vn = (vc * lax.rsqrt(var + EPS) * lg_ref[...] + lb_ref[...]).astype(BF16)
        mixed = jnp.concatenate(
            [jnp.dot(ws[h], vn[:, h * HEAD_DIM:(h + 1) * HEAD_DIM], preferred_element_type=F32)
             for h in range(N_HEADS)], axis=-1) + bst_ref[...]
        o_ref[n * L:(n + 1) * L, :] = _rms(u * mixed, og_ref[...])


def _gmlp(big, lg, lb, ws, bst, og, tm):
    T = big.shape[0]
    row = pl.BlockSpec((1, GROUP_W), lambda i: (0, 0))
    return pl.pallas_call(
        functools.partial(_gmlp_body, nwin=tm // GMLP_LEN), grid=(T // tm,),
        in_specs=[pl.BlockSpec((tm, GROUP_W), lambda i: (i, COL_GU)),
                  pl.BlockSpec((tm, GROUP_W), lambda i: (i, COL_GV)),
                  row, row,
                  pl.BlockSpec((N_HEADS, GMLP_LEN, GMLP_LEN), lambda i: (0, 0, 0)),
                  pl.BlockSpec((GMLP_LEN, GROUP_W), lambda i: (0, 0)),
                  row],
        out_specs=pl.BlockSpec((tm, GROUP_W), lambda i: (i, 0)),
        out_shape=jax.ShapeDtypeStruct((T, GROUP_W), F32),
        compiler_params=_params("parallel"), name="gmlp")(big, big, lg, lb, ws, bst, og)


def _gdn_body(q_ref, k_ref, v_ref, w_ref, gate_ref, gx_ref, ng_ref, o_ref,
              s_scr, u_scr, wq_scr, a_scr, kd_scr, dl_scr, t_scr, p_scr, rhs_scr, halo_scr, qkv_scr, *, nchunk, nb):
    C = CHUNK
    ts = nchunk * C
    first = pl.program_id(1) == 0

    @pl.when(first)
    def _():
        s_scr[...] = jnp.zeros_like(s_scr)

    ones_bd = _head_ones()
    for bb, a in [(bb, a) for bb in range(nb) for a in range(3)]:
        w = w_ref[a]
        x = (q_ref, k_ref, v_ref)[a][bb].astype(F32)
        xx = jnp.concatenate([jnp.where(first, 0.0, halo_scr[bb * 3 + a]), x], axis=0)
        halo_scr[bb * 3 + a] = x[ts - SUBLANES:, :]
        y = w[CONV_K - 1:CONV_K, :] * x
        for j in range(CONV_K - 1):
            y = y + w[j:j + 1, :] * pltpu.roll(xx, CONV_K - 1 - j, 0)[SUBLANES:, :]
        y = y * jax.nn.sigmoid(y)
        if a < 2:
            y = y * lax.rsqrt(_head_sums(y * y, ones_bd) + EPS)
        qkv_scr[bb * 3 + a] = y * (HEAD_DIM ** -0.5) if a == 0 else y

    W = GROUP_W
    pos = lax.broadcasted_iota(I32, (C, W), 1) % HEAD_DIM
    r = lax.broadcasted_iota(I32, (C, W), 0)
    tri, strict, eye = r >= pos, r > pos, r == pos
    same_head = (lax.broadcasted_iota(I32, (W, W), 0) // HEAD_DIM
                 == lax.broadcasted_iota(I32, (W, W), 1) // HEAD_DIM)
    mmb = functools.partial(jnp.dot, preferred_element_type=F32)

    def block_diag(x):
        return jnp.where(same_head, jnp.concatenate([x.astype(BF16)] * N_HEADS, axis=0), 0.0)

    items = nb * nchunk
    for n in range(items):
        bb = n // nchunk
        rows = slice((n % nchunk) * C, (n % nchunk + 1) * C)
        q, k, v = (qkv_scr[bb * 3 + a, rows, :] for a in range(3))
        gc = gx_ref[bb, rows, :W]
        beta = gx_ref[bb, rows, W:]
        gr = jnp.sum(jnp.where(eye, gc, 0.0), axis=0, keepdims=True)
        decay = jnp.exp(jnp.where(tri, gc - gr, -jnp.inf))
        kb = k * beta
        kk = lax.dot_general(jnp.concatenate([kb, q], axis=0).astype(BF16), block_diag(k),
                             (((1,), (1,)), ((), ())), preferred_element_type=F32)
        x = jnp.where(strict, -(kk[:C] * decay), 0.0)
        t_scr[n] = jnp.where(eye, 1.0, 0.0) + x
        p_scr[n] = x.astype(BF16)
        eg = jnp.exp(gc)
        g_last = gc[C - 1:C, :]
        rhs_scr[n, 0] = block_diag(v * beta)
        rhs_scr[n, 1] = block_diag(kb * eg)
        wq_scr[n, C:, :] = (q * eg).astype(BF16)
        a_scr[n] = jnp.where(tri, kk[C:] * decay, 0.0).astype(BF16)
        kd_scr[n] = (k * jnp.exp(g_last - gc)).astype(BF16)
        dl_scr[n] = jnp.exp(g_last)
    for level in range(1, 6):
        for n in range(items):
            p = p_scr[n]
            p_scr[n] = mmb(p, block_diag(p)).astype(BF16)
        for n in range(items):
            t = t_scr[n]
            t_scr[n] = t + mmb(t.astype(BF16), block_diag(p_scr[n]))
    for n in range(items):
        t = t_scr[n].astype(BF16)
        u_scr[n] = mmb(t, rhs_scr[n, 0])
        wq_scr[n, :C, :] = mmb(t, rhs_scr[n, 1]).astype(BF16)

    ones_bd = _head_ones()
    batch = range(nb)
    state = [s_scr[bb] for bb in batch]
    for c in range(nchunk):
        rows = slice(c * C, (c + 1) * C)
        it = [bb * nchunk + c for bb in batch]
        ws = [mmb(wq_scr[it[bb]], state[bb].astype(BF16)) for bb in batch]
        vb = [(u_scr[it[bb]] - ws[bb][:C]).astype(BF16) for bb in batch]
        o = [ws[bb][C:] + mmb(a_scr[it[bb]], block_diag(vb[bb])) for bb in batch]
        kv = [lax.dot_general(kd_scr[it[bb]], vb[bb], (((0,), (0,)), ((), ())), preferred_element_type=F32)
              for bb in batch]
        state = [state[bb] * dl_scr[it[bb]] + jnp.where(same_head, kv[bb], 0.0) for bb in batch]
        for bb in batch:
            gate = gate_ref[bb, rows, :].astype(F32)
            y = o[bb] * lax.rsqrt(_head_sums(o[bb] * o[bb], ones_bd) * (1.0 / HEAD_DIM) + EPS) * ng_ref[...]
            o_ref[bb, rows, :] = y * (gate * jax.nn.sigmoid(gate))
    for bb in batch:
        s_scr[bb] = state[bb]


def _gdn(big3, conv_w3, gx, ng, ts, nb):
    B, S, _ = big3.shape
    nchunk = ts // CHUNK
    col = lambda c: pl.BlockSpec((nb, ts, GROUP_W), lambda b, i: (b, i, c))
    W = GROUP_W
    items = nb * nchunk
    return pl.pallas_call(
        functools.partial(_gdn_body, nchunk=nchunk, nb=nb), grid=(B // nb, S // ts),
        in_specs=[col(COL_DQ), col(COL_DK), col(COL_DV),
                  pl.BlockSpec((3, CONV_K, W), lambda b, i: (0, 0, 0)),
                  col(COL_DG),
                  pl.BlockSpec((nb, ts, 2 * W), lambda b, i: (b, i, 0)),
                  pl.BlockSpec((1, W), lambda b, i: (0, 0))],
        out_specs=col(0), out_shape=jax.ShapeDtypeStruct((B, S, W), F32),
        scratch_shapes=[pltpu.VMEM((nb, W, W), F32),
                        pltpu.VMEM((items, CHUNK, W), F32),
                        pltpu.VMEM((items, 2 * CHUNK, W), BF16),
                        pltpu.VMEM((items, CHUNK, W), BF16),
                        pltpu.VMEM((items, CHUNK, W), BF16),
                        pltpu.VMEM((items, 1, W), F32),
                        pltpu.VMEM((items, CHUNK, W), F32),
                        pltpu.VMEM((items, CHUNK, W), BF16),
                        pltpu.VMEM((items, 2, W, W), BF16),
                        pltpu.VMEM((nb * 3, SUBLANES, W), F32),
                        pltpu.VMEM((nb * 3, ts, W), F32)],
        compiler_params=_params("parallel", "arbitrary"), name="gdn")(
            big3, big3, big3, conv_w3, big3, gx, ng)


def _pool_body(z_ref, halo_ref, w_ref, sc_ref, og_ref, o_ref, *, ts):
    i = pl.program_id(1)
    hr = 2 * SUBLANES
    z = z_ref[...].astype(F32)
    halo = jnp.where(i == 0, 0.0, halo_ref[...].astype(F32))
    s1 = jnp.concatenate([halo, z], axis=0)
    s2 = s1 + pltpu.roll(s1, 1, 0)
    s4 = s2 + pltpu.roll(s2, 2, 0)
    s8 = s4 + pltpu.roll(s4, 4, 0)
    s16 = s8 + pltpu.roll(s8, 8, 0)
    grp = lax.broadcasted_iota(I32, (ts, GROUP_W), 1) // (GROUP_W // len(POOL_WINDOWS))
    t = lax.broadcasted_iota(I32, (ts, GROUP_W), 0) + i * ts
    total = jnp.where(grp == 0, s2[hr:], jnp.where(grp == 1, s4[hr:], jnp.where(grp == 2, s8[hr:], s16[hr:])))
    win = jnp.where(grp == 0, POOL_WINDOWS[0], jnp.where(grp == 1, POOL_WINDOWS[1],
                    jnp.where(grp == 2, POOL_WINDOWS[2], POOL_WINDOWS[3])))
    pooled = total / jnp.minimum(t + 1, win).astype(F32)
    y = _mm(pooled - z, w_ref[...]) * sc_ref[...]
    o_ref[...] = _rms(y, og_ref[...])


def _pool(big3, wbd, sc, og, ts):
    B, S, _ = big3.shape
    hr = 2 * SUBLANES
    hb = ts // hr
    row = pl.BlockSpec((1, GROUP_W), lambda b, i: (0, 0))
    return pl.pallas_call(
        functools.partial(_pool_body, ts=ts), grid=(B, S // ts),
        in_specs=[pl.BlockSpec((None, ts, GROUP_W), lambda b, i: (b, i, COL_PZ)),
                  pl.BlockSpec((None, hr, GROUP_W), lambda b, i: (b, jnp.maximum(i * hb - 1, 0), COL_PZ)),
                  pl.BlockSpec((GROUP_W, GROUP_W), lambda b, i: (0, 0)), row, row],
        out_specs=pl.BlockSpec((None, ts, GROUP_W), lambda b, i: (b, i, 0)),
        out_shape=jax.ShapeDtypeStruct((B, S, GROUP_W), F32),
        compiler_params=_params("parallel", "parallel"), name="pool")(big3, big3, wbd, sc, og)


def _outproj_body(ya_ref, yb_ref, yc_ref, yd_ref, h_ref, wo_ref, g_ref, rw_ref, rb_ref,
                  hnew_ref, hn_ref, ri_ref, rf_ref, cnt_ref, *, tm, tr):
    y = jnp.concatenate([ya_ref[...], yb_ref[...], yc_ref[...], yd_ref[...]], axis=-1).astype(BF16)
    h_new = h_ref[...] + jnp.dot(y, wo_ref[...], preferred_element_type=F32)
    hnew_ref[...] = h_new
    hn = _rms(h_new, g_ref[...])
    hn_hi, hn_lo = _split(hn)
    hn_ref[...] = hn_hi
    t = jnp.dot(hn_hi, rw_ref[...], preferred_element_type=F32)
    logits = (t[:, :LANES] + t[:, LANES:]
              + jnp.dot(hn_lo, rw_ref[:, :LANES], preferred_element_type=F32)) + rb_ref[...]
    lane = lax.broadcasted_iota(I32, (tm, LANES), 1)
    neg = -jnp.inf
    big_lane = LANES

    def masked_top(vals, mask):
        v = jnp.where(mask, vals, neg)
        mx = jnp.max(v, axis=-1, keepdims=True)
        idx = jnp.min(jnp.where(mask & (v == mx), lane, big_lane), axis=-1, keepdims=True)
        return v, mx, idx

    gmask = lane < N_GROUPS
    gv, gmx, gidx = masked_top(logits, gmask)
    g_top = 1.0 / jnp.sum(jnp.where(gmask, jnp.exp(gv - gmx), 0.0), axis=-1, keepdims=True)
    lo = N_GROUPS + gidx * EXPERTS_PER_GROUP
    emask = (lane >= lo) & (lane < lo + EXPERTS_PER_GROUP)
    ev, emx, eidx1 = masked_top(logits, emask)
    esum = jnp.sum(jnp.where(emask, jnp.exp(ev - emx), 0.0), axis=-1, keepdims=True)
    p1 = 1.0 / esum
    _, emx2, eidx2 = masked_top(logits, emask & (lane != eidx1))
    p2 = jnp.exp(emx2 - emx) / esum
    denom = p1 + p2
    rf_ref[...] = jnp.where(lane == 0, g_top * p1 / denom, jnp.where(lane == 1, g_top * p2 / denom, 0.0))

    expert = [eidx1 - N_GROUPS, eidx2 - N_GROUPS]
    hot = [lane == e for e in expert]
    m = (hot[0] | hot[1]).astype(BF16)
    below = (lax.broadcasted_iota(I32, (tr, tr), 0) > lax.broadcasted_iota(I32, (tr, tr), 1)).astype(BF16)
    before = jnp.concatenate(
        [jnp.dot(below, m[k * tr:(k + 1) * tr, :], preferred_element_type=F32) for k in range(tm // tr)], axis=0)
    rank = [jnp.sum(jnp.where(hot[s], before, 0.0), axis=-1, keepdims=True).astype(I32) for s in range(TOPK_IN)]
    out = jnp.zeros((tm, LANES), I32)
    for s in range(TOPK_IN):
        out = jnp.where(lane == s, expert[s], jnp.where(lane == TOPK_IN + s, rank[s], out))
    ri_ref[...] = out
    for k in range(tm // tr):
        last = (k + 1) * tr - 1
        total = before[last:last + 1, :] + m[last:last + 1, :].astype(F32)
        cnt_ref[k * SUBLANES:(k + 1) * SUBLANES, :] = jnp.broadcast_to(total, (SUBLANES, LANES)).astype(I32)


def _outproj(ya, yb, yc, yd, h2d, wo, g, rw, rb, tm, tr):
    T, D = h2d.shape
    yblk = pl.BlockSpec((tm, GROUP_W), lambda i: (i, 0))
    cnt_rows = tm // tr * SUBLANES
    return pl.pallas_call(
        functools.partial(_outproj_body, tm=tm, tr=tr), grid=(T // tm,),
        in_specs=[yblk, yblk, yblk, yblk,
                  pl.BlockSpec((tm, D), lambda i: (i, 0)),
                  pl.BlockSpec((D, D), lambda i: (0, 0)),
                  pl.BlockSpec((1, D), lambda i: (0, 0)),
                  pl.BlockSpec((D, 2 * LANES), lambda i: (0, 0)),
                  pl.BlockSpec((1, LANES), lambda i: (0, 0))],
        out_specs=[pl.BlockSpec((tm, D), lambda i: (i, 0)),
                   pl.BlockSpec((tm, D), lambda i: (i, 0)),
                   pl.BlockSpec((tm, LANES), lambda i: (i, 0)),
                   pl.BlockSpec((tm, LANES), lambda i: (i, 0)),
                   pl.BlockSpec((cnt_rows, LANES), lambda i: (i, 0))],
        out_shape=[jax.ShapeDtypeStruct((T, D), F32),
                   jax.ShapeDtypeStruct((T, D), BF16),
                   jax.ShapeDtypeStruct((T, LANES), I32),
                   jax.ShapeDtypeStruct((T, LANES), F32),
                   jax.ShapeDtypeStruct((T // tr * SUBLANES, LANES), I32)],
        compiler_params=_params("parallel"), name="outproj")(ya, yb, yc, yd, h2d, wo, g, rw, rb)


def _dispatch_plan(expert, lrank, cnt_tile, tm):
    T = expert.shape[0]
    nt = T // tm
    counts = jnp.sum(cnt_tile, axis=0)
    padded = (counts + RUN_ROWS + MOE_BLOCK - 1) // MOE_BLOCK * MOE_BLOCK
    pad_end = jnp.cumsum(padded)
    pad_start = pad_end - padded
    gstart = pad_start[None, :] + jnp.cumsum(cnt_tile, axis=0) - cnt_tile
    nchunk = (cnt_tile + RUN_ROWS - 1) // RUN_ROWS
    chunk_end = jnp.cumsum(nchunk, axis=1)
    lstart = (chunk_end - nchunk) * RUN_ROWS
    onehot = expert[:, :, None] == jnp.arange(N_EXPERTS, dtype=I32)[None, None, :]
    pick = lambda tab: jnp.sum(jnp.where(onehot, jnp.repeat(tab, tm, axis=0)[:, None, :], 0), axis=-1)
    lpos =(lrank + pick(lstart)).astype(I32)
    max_chunks = tm * TOPK_IN // RUN_ROWS + N_EXPERTS
    c = jnp.arange(max_chunks, dtype=I32)
    ce = jnp.minimum(jnp.sum(chunk_end[:, None, :] <= c[None, :, None], axis=-1), N_EXPERTS - 1)
    ce_hot = ce[:, :, None] == jnp.arange(N_EXPERTS, dtype=I32)[None, None, :]
    take = lambda tab: jnp.sum(jnp.where(ce_hot, tab[:, None, :], 0), axis=-1)
    chunk_row = take(gstart) + (c[None, :] - take(chunk_end - nchunk)) * RUN_ROWS
    chunk_row = jnp.where(c[None, :] < chunk_end[:, -1:], chunk_row, 0)
    table = jnp.concatenate([chunk_row.astype(I32), jnp.zeros((nt, LANES - 1 - max_chunks), I32),
                             chunk_end[:, -1:].astype(I32)], axis=1).reshape(nt, 1, LANES)
    n_blk = -(-(T * TOPK_IN + N_EXPERTS * RUN_ROWS) // MOE_BLOCK) + N_EXPERTS + 1
    blk_start = jnp.arange(n_blk, dtype=I32) * MOE_BLOCK
    blk_e = jnp.minimum(jnp.sum(pad_end[None, :] <= blk_start[:, None], axis=-1), N_EXPERTS - 1).astype(I32)
    n_used = (pad_end[-1] // MOE_BLOCK).astype(I32).reshape(1)
    zero_start = jnp.concatenate([pad_start + counts, pad_end[-1:]])
    zero_end = jnp.concatenate([pad_end, jnp.full((1,), n_blk * MOE_BLOCK, pad_end.dtype)])
    nz = N_EXPERTS + 1
    zfill = jnp.concatenate([zero_start, (zero_end - zero_start) // RUN_ROWS, zero_end - RUN_ROWS,
                             jnp.zeros((LANES - 3 * nz,), zero_start.dtype)]).astype(I32).reshape(1, LANES)
    return lpos, table, zfill, blk_e, n_used, n_blk


def _dispatch_body(zf_ref, tab_ref, lpos_ref, hn_ref, x_hbm, xs, zero, sem, zsem, n_prev, *, tm, nseg, nrow):
    i = pl.program_id(0)
    slot = i % 2
    step = RUN_ROWS * nseg

    @pl.when(i == 0)
    def _():
        zero[...] = jnp.zeros_like(zero)

        def zero_copy(r):
            return pltpu.make_async_copy(zero, x_hbm.at[pl.ds(pl.multiple_of(r * nseg, nseg), step), :], zsem)

        nz = N_EXPERTS + 1

        def for_chunks(fn):
            for e in range(nz):
                def body(k, carry, e=e):
                    fn(zero_copy(zf_ref[0, e] + k * RUN_ROWS))
                    return carry
                lax.fori_loop(0, zf_ref[0, nz + e], body, 0)

        for_chunks(lambda c: c.start())
        for_chunks(lambda c: c.wait())
        for e in range(nz):
            zero_copy(zf_ref[0, 2 * nz + e]).start()
        for e in range(nz):
            zero_copy(zf_ref[0, 2 * nz + e]).wait()

    row = lax.broadcasted_iota(I32, (nrow, tm), 0)
    sel = (row == lpos_ref[0:1, :]) | (row == lpos_ref[1:2, :])
    rows = jnp.dot(sel.astype(BF16), hn_ref[...], preferred_element_type=F32)
    for k in range(nseg):
        xs[slot, pl.ds(k, nrow, stride=nseg), :] = rows[:, k * LANES:(k + 1) * LANES]

    def chunk_copy(sl, c):
        return pltpu.make_async_copy(
            xs.at[sl, pl.ds(pl.multiple_of(c * step, step), step), :],
            x_hbm.at[pl.ds(pl.multiple_of(tab_ref[0, 0, c] * nseg, nseg), step), :], sem.at[sl])

    def wait_chunks(sl, n):
        def wait(c, carry):
            chunk_copy(sl, 0).wait()
            return carry
        lax.fori_loop(0, n, wait, 0)

    n_chunks = tab_ref[0, 0, LANES - 1]

    @pl.when(i > 0)
    def _():
        wait_chunks(1 - slot, n_prev[0])

    def start(c, carry):
        chunk_copy(slot, c).start()
        return carry

    lax.fori_loop(0, n_chunks, start, 0)
    n_prev[0] = n_chunks

    @pl.when(i == pl.num_programs(0) - 1)
    def _():
        wait_chunks(slot, n_chunks)


def _dispatch(zfill, table, lpos_t, hn, n_pad, tm):
    T, D = hn.shape
    nseg = D // LANES
    nrow = tm * TOPK_IN + N_EXPERTS * RUN_ROWS
    return pl.pallas_call(
        functools.partial(_dispatch_body, tm=tm, nseg=nseg, nrow=nrow), grid=(T // tm,),
        in_specs=[pl.BlockSpec(memory_space=pltpu.SMEM),
                  pl.BlockSpec((1, 1, LANES), lambda i: (i, 0, 0), memory_space=pltpu.SMEM),
                  pl.BlockSpec((None, TOPK_IN, tm), lambda i: (i, 0, 0)),
                  pl.BlockSpec((tm, D), lambda i: (i, 0))],
        out_specs=pl.BlockSpec(memory_space=pl.ANY),
        out_shape=jax.ShapeDtypeStruct((n_pad * nseg, LANES), F32),
        scratch_shapes=[pltpu.VMEM((2, nrow * nseg, LANES), F32), pltpu.VMEM((RUN_ROWS * nseg, LANES), F32),
                        pltpu.SemaphoreType.DMA((2,)), pltpu.SemaphoreType.DMA, pltpu.SMEM((1,), I32)],
        compiler_params=_params("arbitrary"), name="dispatch")(zfill, table, lpos_t, hn)


def _moe_body(blk_e_ref, n_used_ref, seg_ref, nxt_ref, x_ref, w1_hbm, w3_hbm, w2_hbm, y_ref,
              wf1, wf3, wf2, w1b, w3b, w2b, sem, *, nseg, layer):
    b = pl.program_id(0)
    R = MOE_BLOCK

    def fetch(e, slot):
        return [pltpu.make_async_copy(src.at[layer, e], dst.at[slot], sem.at[slot, k])
                for k, (src, dst) in enumerate(((w1_hbm, wf1), (w3_hbm, wf3), (w2_hbm, wf2)))]

    @pl.when(b < n_used_ref[0])
    def _():
        @pl.when((b == 0) | (blk_e_ref[b] != blk_e_ref[jnp.maximum(b - 1, 0)]))
        def _():
            e = blk_e_ref[b]
            slot = seg_ref[b] % 2

            @pl.when(b == 0)
            def _():
                for c in fetch(e, slot):
                    c.start()

            for c in fetch(e, slot):
                c.wait()
            w1b[...] = wf1[slot].astype(BF16)
            w3b[...] = wf3[slot].astype(BF16)
            w2b[...] = wf2[slot].astype(BF16)

            @pl.when(nxt_ref[b] >= 0)
            def _():
                for c in fetch(nxt_ref[b], 1 - slot):
                    c.start()

        x = jnp.concatenate([x_ref[pl.ds(s, R, stride=nseg), :] for s in range(nseg)], axis=-1).astype(BF16)
        a = jnp.dot(x, w1b[...], preferred_element_type=F32)
        g = jnp.dot(x, w3b[...], preferred_element_type=F32)
        hid = (a * jax.nn.sigmoid(a) * g).astype(BF16)
        y = jnp.dot(hid, w2b[...], preferred_element_type=F32)
        for s in range(nseg):
            y_ref[pl.ds(s, R, stride=nseg), :] = y[:, s * LANES:(s + 1) * LANES]

    @pl.when(b >= n_used_ref[0])
    def _():
        y_ref[...] = jnp.zeros_like(y_ref)


def _moe(blk_e, n_used, x_rows, w1, w3, w2, layer):
    n_blk = blk_e.shape[0]
    _, _, D, DE = w1.shape
    nseg = D // LANES
    R = MOE_BLOCK
    blk = jnp.arange(n_blk, dtype=I32)
    used = blk < n_used[0]
    change = (blk > 0) & (blk_e != jnp.roll(blk_e, 1)) & used
    seg = jnp.cumsum(change.astype(I32)).astype(I32)
    later = (blk[None, :] > blk[:, None]) & (seg[None, :] > seg[:, None]) & used[None, :]
    nxt_blk = jnp.min(jnp.where(later, blk[None, :], n_blk), axis=1)
    nxt = jnp.where(nxt_blk < n_blk, blk_e[jnp.minimum(nxt_blk, n_blk - 1)], -1).astype(I32)
    rows = lambda b, be, nu, sg, nx: (jnp.minimum(b, nu[0] - 1), 0)
    grid_spec = pltpu.PrefetchScalarGridSpec(
        num_scalar_prefetch=4, grid=(n_blk,),
        in_specs=[pl.BlockSpec((R * nseg, LANES), rows),
                  pl.BlockSpec(memory_space=pl.ANY), pl.BlockSpec(memory_space=pl.ANY),
                  pl.BlockSpec(memory_space=pl.ANY)],
        out_specs=pl.BlockSpec((R * nseg, LANES), lambda b, be, nu, sg, nx: (b, 0)),
        scratch_shapes=[pltpu.VMEM((2, D, DE), F32), pltpu.VMEM((2, D, DE), F32), pltpu.VMEM((2, DE, D), F32),
                        pltpu.VMEM((D, DE), BF16), pltpu.VMEM((D, DE), BF16), pltpu.VMEM((DE, D), BF16),
                        pltpu.SemaphoreType.DMA((2, 3))])
    return pl.pallas_call(
        functools.partial(_moe_body, nseg=nseg, layer=layer), grid_spec=grid_spec,
        out_shape=jax.ShapeDtypeStruct(x_rows.shape, F32),
        compiler_params=_params("arbitrary"), name="moe")(blk_e, n_used, seg, nxt, x_rows, w1, w3, w2)


def _combine_body(tab_ref, tabn_ref, h_ref, rf_ref, lpos_ref, y_hbm, o_ref, ybuf, sem, *, tm, nseg, nrow):
    i = pl.program_id(0)
    slot = i % 2
    step = RUN_ROWS * nseg

    def start_chunks(tab, sl):
        def start(c, carry):
            pltpu.make_async_copy(
                y_hbm.at[pl.ds(pl.multiple_of(tab[0, 0, c] * nseg, nseg), step), :],
                ybuf.at[sl, pl.ds(pl.multiple_of(c * step, step), step), :], sem.at[sl]).start()
            return carry
        lax.fori_loop(0, tab[0, 0, LANES - 1], start, 0)

    @pl.when(i == 0)
    def _():
        ybuf[...] = jnp.zeros_like(ybuf)
        start_chunks(tab_ref, 0)

    @pl.when(i + 1 < pl.num_programs(0))
    def _():
        start_chunks(tabn_ref, 1 - slot)

    col = lax.broadcasted_iota(I32, (tm, nrow), 1)
    sel = jnp.zeros((tm, nrow), F32)
    for s in range(TOPK_IN):
        sel = sel + jnp.where(col == lpos_ref[:, s:s + 1], rf_ref[:, s:s + 1], 0.0)

    def wait(c, carry):
        pltpu.make_async_copy(y_hbm.at[pl.ds(0, step), :], ybuf.at[slot, pl.ds(0, step), :],
                              sem.at[slot]).wait()
        return carry

    lax.fori_loop(0, tab_ref[0, 0, LANES - 1], wait, 0)
    y = jnp.concatenate([ybuf[slot, pl.ds(k, nrow, stride=nseg), :] for k in range(nseg)], axis=-1)
    o_ref[...] = h_ref[...] + jnp.dot(sel.astype(BF16), y.astype(BF16), preferred_element_type=F32)


def _combine(table, h2d, rf, lpos, y_rows, tm):
    T, D = h2d.shape
    nseg = D // LANES
    nt = T // tm
    nrow = tm * TOPK_IN + N_EXPERTS * RUN_ROWS
    return pl.pallas_call(
        functools.partial(_combine_body, tm=tm, nseg=nseg, nrow=nrow), grid=(nt,),
        in_specs=[pl.BlockSpec((1, 1, LANES), lambda i: (i, 0, 0), memory_space=pltpu.SMEM),
                  pl.BlockSpec((1, 1, LANES), lambda i: (jnp.minimum(i + 1, nt - 1), 0, 0),
                               memory_space=pltpu.SMEM),
                  pl.BlockSpec((tm, D), lambda i: (i, 0)),
                  pl.BlockSpec((tm, LANES), lambda i: (i, 0)),
                  pl.BlockSpec((tm, TOPK_IN), lambda i: (i, 0)),
                  pl.BlockSpec(memory_space=pl.ANY)],
        out_specs=pl.BlockSpec((tm, D), lambda i: (i, 0)),
        out_shape=jax.ShapeDtypeStruct((T, D), F32),
        scratch_shapes=[pltpu.VMEM((2, nrow * nseg, LANES), F32), pltpu.SemaphoreType.DMA((2,))],
        compiler_params=_params("arbitrary"), name="combine")(table, table, h2d, rf, lpos, y_rows)


def _pad_lanes(a, lane0, rows=1):
    a = a.reshape(rows, -1)
    return jnp.pad(a, ((0, 0), (lane0, LANES - lane0 - a.shape[-1])))


def _layer(h2d, B, S, p):
    T, D = h2d.shape
    tile = lambda a, n: jnp.tile(a.reshape(1, -1), (1, n))

    offs = np.cumsum([0, GROUP_W, GROUP_W, GROUP_W, N_HEADS, GROUP_W, GROUP_W,
                      GROUP_W, GROUP_W, GROUP_W, N_HEADS, N_HEADS, GROUP_W, GROUP_W])
    seg = lambda k: p['w_in'][:, offs[k]:offs[k + 1]]
    w_all = jnp.concatenate([seg(0), seg(1), seg(2), seg(4), seg(5), seg(6), seg(7), seg(8), seg(11), seg(12),
                             seg(3), seg(9), seg(10), jnp.zeros((D, LANES - 3 * N_HEADS), F32)],
                            axis=1).astype(BF16)
    gate_prm = jnp.concatenate([_pad_lanes(p['fox_f_bias'], LANE_FOX), _pad_lanes(p['gdn_dt_bias'], LANE_DECAY),
                                _pad_lanes(p['gdn_a_log'], LANE_DECAY), jnp.zeros((SUBLANES - 3, LANES), F32)], axis=0)

    big, small = _inproj(h2d, p['attn_norm_g'].reshape(1, D), w_all, tm=min(512, T))
    big3 = big.reshape(B, S, N_BIG_COLS * GROUP_W)
    ts = min(512, S)
    gcol, gx = _gates(small.reshape(B, S, LANES), gate_prm, ts)

    tq = min(256, S)
    ya = _fox(big3, gcol, tile(p['fox_qn_g'], N_HEADS), tile(p['fox_kn_g'], N_HEADS),
              p['fox_out_g'].reshape(1, GROUP_W), tq)

    bst = jnp.repeat(p['gmlp_bs'].T, HEAD_DIM, axis=1)
    yb = _gmlp(big, p['gmlp_ln_g'].reshape(1, -1), p['gmlp_ln_b'].reshape(1, -1), p['gmlp_ws'], bst,
               p['gmlp_out_g'].reshape(1, -1), tm=min(512, T))

    conv_w3 = p['gdn_conv_w'].reshape(CONV_K, 3, GROUP_W).transpose(1, 0, 2)
    nb = 4 if B % 4 == 0 else (2 if B % 2 == 0 else 1)
    yc = _gdn(big3, conv_w3, gx, tile(p['gdn_norm_g'], N_HEADS), min(1024 // nb, S), nb)

    wbd = jax.scipy.linalg.block_diag(*[p['pool_w'][g] for g in range(len(POOL_WINDOWS))]).astype(BF16)
    yd = _pool(big3, wbd, p['pool_scale'].reshape(1, -1), p['pool_out_g'].reshape(1, -1), ts)

    rw = _pad_lanes(jnp.concatenate([p['router_g_w'], p['router_e_w']], axis=1), 0, rows=D)
    rw = jnp.concatenate(_split(rw), axis=1)
    rb = _pad_lanes(jnp.concatenate([p['router_g_b'], p['router_e_b']]), 0)
    flat = lambda a: a.reshape(T, GROUP_W)
    tmd = min(256, T)
    h_new, hn_rows, ri, rf, cnt = _outproj(flat(ya), yb, flat(yc), flat(yd), h2d, p['w_out'].astype(BF16),
                                           p['ffn_norm_g'].reshape(1, D), rw, rb, tm=min(512, T), tr=tmd)
    cnt_tile = cnt.reshape(T // tmd, SUBLANES, LANES)[:, 0, :N_EXPERTS]
    lpos, table, zfill, blk_e, n_used, n_blk = _dispatch_plan(
        ri[:, :TOPK_IN], ri[:, TOPK_IN:2 * TOPK_IN], cnt_tile, tmd)
    lpos_t = lpos.reshape(T // tmd, tmd, TOPK_IN).transpose(0, 2, 1)
    x_rows = _dispatch(zfill, table, lpos_t, hn_rows, n_blk * MOE_BLOCK, tmd)
    y_rows = _moe(blk_e, n_used, x_rows, p['moe_w1'], p['moe_w3'], p['moe_w2'], p['layer'])
    return _combine(table, h_new, rf, lpos, y_rows, tmd)


def kernel(x, attn_norm_g, w_in, w_out, fox_f_bias, fox_qn_g, fox_kn_g, fox_out_g, gmlp_ln_g, gmlp_ln_b, gmlp_ws, gmlp_bs, gmlp_out_g, gdn_conv_w, gdn_a_log, gdn_dt_bias, gdn_norm_g, pool_w, pool_scale, pool_out_g, ffn_norm_g, router_g_w, router_g_b, router_e_w, router_e_b, moe_w1, moe_w3, moe_w2):
    B, S, D = x.shape
    names = ('attn_norm_g', 'w_in', 'w_out', 'fox_f_bias', 'fox_qn_g', 'fox_kn_g', 'fox_out_g', 'gmlp_ln_g',
             'gmlp_ln_b', 'gmlp_ws', 'gmlp_bs', 'gmlp_out_g', 'gdn_conv_w', 'gdn_a_log', 'gdn_dt_bias',
             'gdn_norm_g', 'pool_w', 'pool_scale', 'pool_out_g', 'ffn_norm_g', 'router_g_w', 'router_g_b',
             'router_e_w', 'router_e_b', 'moe_w1', 'moe_w3', 'moe_w2')
    vals = (attn_norm_g, w_in, w_out, fox_f_bias, fox_qn_g, fox_kn_g, fox_out_g, gmlp_ln_g, gmlp_ln_b, gmlp_ws,
            gmlp_bs, gmlp_out_g, gdn_conv_w, gdn_a_log, gdn_dt_bias, gdn_norm_g, pool_w, pool_scale, pool_out_g,
            ffn_norm_g, router_g_w, router_g_b, router_e_w, router_e_b, moe_w1, moe_w3, moe_w2)
    h = x.reshape(B * S, D)
    stacked = ('moe_w1', 'moe_w3', 'moe_w2')
    for l in range(w_in.shape[0]):
        p = {n: (v if n in stacked else v[l]) for n, v in zip(names, vals)}
        p['layer'] = l
        h = _layer(h, B, S, p)
    return h.reshape(B, S, D)
```

```python
import functools

import jax
import jax.numpy as jnp
import numpy as np
from jax import lax
from jax.experimental import pallas as pl
from jax.experimental.pallas import tpu as pltpu

F32 = jnp.float32
BF16 = jnp.bfloat16
I32 = jnp.int32

EPS = 1e-6
HEAD_DIM = 64
GROUP_W = 256
N_HEADS = GROUP_W // HEAD_DIM
CHUNK = 64
GMLP_LEN = 128
CONV_K = 4
POOL_WINDOWS = (2, 4, 8, 16)
N_GROUPS = 4
EXPERTS_PER_GROUP = 8
N_EXPERTS = N_GROUPS * EXPERTS_PER_GROUP
TOPK_IN = 2
MOE_BLOCK = 256
FOX_VT_ROWS = HEAD_DIM + 16
FOX_BOUND_LOG2 = 40.0
FOX_SKIP_LOG2 = 160.0
RUN_ROWS = 16
LANES = 128
SUBLANES = 8
VMEM_LIMIT = 56 * 1024 * 1024

COL_FQ, COL_FK, COL_FV, COL_GU, COL_GV, COL_DQ, COL_DK, COL_DV, COL_DG, COL_PZ = range(10)
N_BIG_COLS = 10
LANE_FOX, LANE_DECAY, LANE_BETA = 0, 4, 8


def _params(*sem):
    return pltpu.CompilerParams(dimension_semantics=sem, vmem_limit_bytes=VMEM_LIMIT)


def _head_ones():
    r = lax.broadcasted_iota(I32, (GROUP_W, GROUP_W), 0) // HEAD_DIM
    c = lax.broadcasted_iota(I32, (GROUP_W, GROUP_W), 1) // HEAD_DIM
    return (r == c).astype(BF16)


def _head_sums(x, ones_bd):
    hi = x.astype(BF16)
    lo = (x - hi.astype(F32)).astype(BF16)
    return (jnp.dot(hi, ones_bd, preferred_element_type=F32)
            + jnp.dot(lo, ones_bd, preferred_element_type=F32))


def _rms(x, g):
    return x * lax.rsqrt(jnp.mean(x * x, axis=-1, keepdims=True) + EPS) * g


def _mm(a, b):
    return jnp.dot(a.astype(BF16), b.astype(BF16), preferred_element_type=F32)


def _mm_nt(a, b):
    return lax.dot_general(a.astype(BF16), b.astype(BF16), (((1,), (1,)), ((), ())),
                           preferred_element_type=F32)


def _mm_tn(a, b):
    return lax.dot_general(a.astype(BF16), b.astype(BF16), (((0,), (0,)), ((), ())),
                           preferred_element_type=F32)


def _split(a):
    hi = a.astype(BF16)
    return hi, (a - hi.astype(F32)).astype(BF16)


def _mm3(a, b):
    ah, al = _split(a)
    bh, bl = _split(b)
    d = functools.partial(jnp.dot, preferred_element_type=F32)
    return d(ah, bh) + (d(ah, bl) + d(al, bh))


def _inproj_body(x_ref, g_ref, w_ref, big_ref, small_ref):
    xn = _rms(x_ref[...], g_ref[...]).astype(BF16)
    nb = big_ref.shape[1]
    big_ref[...] = jnp.dot(xn, w_ref[:, :nb], preferred_element_type=F32).astype(big_ref.dtype)
    small_ref[...] = jnp.dot(xn, w_ref[:, nb:], preferred_element_type=F32)


def _inproj(x2d, g, w, tm):
    T, D = x2d.shape
    nb = w.shape[1] - LANES
    return pl.pallas_call(
        _inproj_body, grid=(T // tm,),
        in_specs=[pl.BlockSpec((tm, D), lambda i: (i, 0)),
                  pl.BlockSpec((1, D), lambda i: (0, 0)),
                  pl.BlockSpec((D, nb + LANES), lambda i: (0, 0))],
        out_specs=[pl.BlockSpec((tm, nb), lambda i: (i, 0)),
                   pl.BlockSpec((tm, LANES), lambda i: (i, 0))],
        out_shape=[jax.ShapeDtypeStruct((T, nb), BF16), jax.ShapeDtypeStruct((T, LANES), F32)],
        compiler_params=_params("parallel"), name="inproj")(x2d, g, w)


def _gates_body(sm_ref, p_ref, col_ref, exp_ref, carry_ref, *, ts):
    @pl.when(pl.program_id(1) == 0)
    def _():
        carry_ref[...] = jnp.zeros_like(carry_ref)

    x = sm_ref[...]
    lane = lax.broadcasted_iota(I32, (ts, LANES), 1)
    is_fox = lane < LANE_DECAY
    is_dec = (lane >= LANE_DECAY) & (lane < LANE_BETA)
    is_beta = (lane >= LANE_BETA) & (lane < LANE_BETA + N_HEADS)
    logf = jax.nn.log_sigmoid(x + p_ref[0:1, :])
    g = -jnp.exp(p_ref[2:3, :]) * jax.nn.softplus(x + p_ref[1:2, :])
    beta = jax.nn.sigmoid(x)
    r = lax.broadcasted_iota(I32, (ts, ts), 0)
    c = lax.broadcasted_iota(I32, (ts, ts), 1)
    tri_full = (r >= c).astype(BF16)
    tri_chunk = ((r >= c) & (r // CHUNK == c // CHUNK)).astype(BF16)
    vals = jnp.where(is_fox, logf, jnp.where(is_dec, g, 0.0))
    hi = vals.astype(BF16)
    mid = (vals - hi.astype(F32)).astype(BF16)
    lo = (vals - hi.astype(F32) - mid.astype(F32)).astype(BF16)
    parts = jnp.concatenate([hi, mid, lo], axis=1)

    def tri_sum(tri):
        t = jnp.dot(tri, parts, preferred_element_type=F32)
        return t[:, :LANES] + (t[:, LANES:2 * LANES] + t[:, 2 * LANES:])

    cf = tri_sum(tri_full) + carry_ref[...]
    cg = tri_sum(tri_chunk)
    carry_ref[...] = cf[ts - 1:ts, :]
    out = jnp.where(is_fox, cf, jnp.where(is_dec, cg, jnp.where(is_beta, beta, 0.0)))
    col_ref[...] = out
    o_hi = out.astype(BF16)
    o_mid = (out - o_hi.astype(F32)).astype(BF16)
    o_lo = (out - o_hi.astype(F32) - o_mid.astype(F32)).astype(BF16)
    src = lax.broadcasted_iota(I32, (3 * LANES, 2 * GROUP_W), 0) % LANES
    dst = lax.broadcasted_iota(I32, (3 * LANES, 2 * GROUP_W), 1)
    want = jnp.where(dst < GROUP_W, LANE_DECAY, LANE_BETA) + (dst % GROUP_W) // HEAD_DIM
    exp_ref[...] = jnp.dot(jnp.concatenate([o_hi, o_mid, o_lo], axis=1), (src == want).astype(BF16),
                           preferred_element_type=F32)


def _gates(small3, prm, ts):
    B, S, _ = small3.shape
    return pl.pallas_call(
        functools.partial(_gates_body, ts=ts), grid=(B, S // ts),
        in_specs=[pl.BlockSpec((None, ts, LANES), lambda b, j: (b, j, 0)),
                  pl.BlockSpec((SUBLANES, LANES), lambda b, j: (0, 0))],
        out_specs=[pl.BlockSpec((None, ts, LANES), lambda b, j: (b, j, 0)),
                   pl.BlockSpec((None, ts, 2 * GROUP_W), lambda b, j: (b, j, 0))],
        out_shape=[jax.ShapeDtypeStruct((B, S, LANES), F32),
                   jax.ShapeDtypeStruct((B, S, 2 * GROUP_W), F32)],
        scratch_shapes=[pltpu.VMEM((1, LANES), F32)],
        compiler_params=_params("parallel", "arbitrary"), name="gates")(small3, prm)


def _split3_lanes(x, lane, lane0):
    hi = x.astype(BF16).astype(F32)
    mid = (x - hi).astype(BF16).astype(F32)
    lo = (x - hi - mid).astype(BF16).astype(F32)
    return jnp.where(lane == lane0, hi, jnp.where(lane == lane0 + 1, mid, jnp.where(lane == lane0 + 2, lo, 0.0)))


def _fox_body(jstart_ref, q_ref, k_ref, v_ref, ccol_ref, qg_ref, kg_ref, og_ref, bound_ref, o_ref,
              kn_scr, vt_scr, q_scr, m_scr, l_scr, acc_scr, s_scr, *, tq, nk):
    b = pl.program_id(0)
    i = pl.program_id(1)
    ones_bd = _head_ones()
    lane = lax.broadcasted_iota(I32, (tq, LANES), 1)
    log2e = 1.0 / np.log(2.0)
    c_lane, r_lane = HEAD_DIM, HEAD_DIM + 3

    def head_norm(x, g):
        ss = _head_sums(x * x, ones_bd)
        return x * lax.rsqrt(ss * (1.0 / HEAD_DIM) + EPS) * g

    def head_tile(x, h, extra):
        pair = x[:, (h // 2) * LANES:(h // 2 + 1) * LANES]
        if h % 2:
            pair = pltpu.roll(pair, HEAD_DIM, 1)
        return jnp.where(lane < HEAD_DIM, pair, extra).astype(BF16)

    @pl.when(i == 0)
    def _():
        k_ones = jnp.where((lane >= r_lane) & (lane < r_lane + 3), 1.0, 0.0)
        vt_tail = (lax.broadcasted_iota(I32, (FOX_VT_ROWS - HEAD_DIM, tq), 0) == 0).astype(BF16)
        for c in range(nk):
            rows = slice(c * tq, (c + 1) * tq)
            kc = head_norm(k_ref[rows, :].astype(F32), kg_ref[...])
            vt = v_ref[rows, :].astype(F32).T.astype(BF16)
            cc = ccol_ref[rows, :] * (-log2e)
            for h in range(N_HEADS):
                extra = _split3_lanes(cc[:, LANE_FOX + h:LANE_FOX + h + 1], lane, c_lane) + k_ones
                kn_scr[h, c] = head_tile(kc, h, extra)
                vt_scr[h, c] = jnp.concatenate([vt[h * HEAD_DIM:(h + 1) * HEAD_DIM, :], vt_tail], axis=0)

    qn = head_norm(q_ref[...].astype(F32), qg_ref[...]) * (HEAD_DIM ** -0.5 * log2e)
    bound = bound_ref[...]
    bounded = jnp.max(bound) <= FOX_BOUND_LOG2
    c_i = ccol_ref[pl.ds(pl.multiple_of(i * tq, tq), tq), :] * log2e
    q_ones = jnp.where((lane >= c_lane) & (lane < c_lane + 3), 1.0, 0.0)
    for h in range(N_HEADS):
        r_i = bound[:, h * HEAD_DIM:h * HEAD_DIM + 1] - c_i[:, LANE_FOX + h:LANE_FOX + h + 1]
        neg_r = jnp.where(bounded, -r_i, 0.0)
        q_scr[h] = head_tile(qn, h, q_ones + _split3_lanes(neg_r, lane, r_lane))
    acc_scr[...] = jnp.zeros_like(acc_scr)
    causal = (lax.broadcasted_iota(I32, (tq, tq), 0) <= lax.broadcasted_iota(I32, (tq, tq), 1))

    heads = range(N_HEADS)

    def scores(j):
        return [lax.dot_general(kn_scr[h, j], q_scr[h], (((1,), (1,)), ((), ())),
                                preferred_element_type=F32) for h in heads]

    def stash(s):
        for h in heads:
            s_scr[h] = s[h]

    def absorb_general(j, masked):
        p, alpha = [], []
        for h in heads:
            s = s_scr[h]
            if masked:
                s = jnp.where(causal, s, -jnp.inf)
            m_old = m_scr[h]
            m_new = jnp.maximum(m_old, jnp.max(s, axis=0, keepdims=True))
            alpha.append(jnp.exp2(m_old - m_new))
            ph = jnp.exp2(s - m_new)
            l_scr[h] = alpha[h] * l_scr[h] + jnp.sum(ph, axis=0, keepdims=True)
            m_scr[h] = m_new
            p.append(ph.astype(BF16))
        pv = [jnp.dot(vt_scr[h, j], p[h], preferred_element_type=F32) for h in heads]
        for h in heads:
            acc_scr[h] = alpha[h] * acc_scr[h] + pv[h]

    def absorb_bounded(j, masked):
        p = []
        for h in heads:
            s = s_scr[h]
            if masked:
                s = jnp.where(causal, s, -jnp.inf)
            p.append(jnp.exp2(s).astype(BF16))
        pv = [jnp.dot(vt_scr[h, j], p[h], preferred_element_type=F32) for h in heads]
        for h in heads:
            acc_scr[h] += pv[h]

    def run(absorb, j0):
        stash(scores(j0))

        def body(j, c):
            s_next = scores(j + 1)
            absorb(j, False)
            stash(s_next)
            return c

        lax.fori_loop(j0, i, body, 0)
        absorb(i, True)

    @pl.when(bounded)
    def _():
        run(absorb_bounded, jstart_ref[b * nk + i])

    @pl.when(jnp.logical_not(bounded))
    def _():
        m_scr[...] = jnp.full_like(m_scr, -jnp.inf)
        l_scr[...] = jnp.zeros_like(l_scr)
        run(absorb_general, 0)
        for h in heads:
            acc_scr[h, HEAD_DIM:HEAD_DIM + 1, :] = l_scr[h]

    o_t = jnp.concatenate([acc_scr[h, :HEAD_DIM, :] / acc_scr[h, HEAD_DIM:HEAD_DIM + 1, :] for h in heads],
                          axis=0)
    o_ref[...] = _rms(o_t.T, og_ref[...])


def _fox_first_block(gcol, tq):
    B, S, _ = gcol.shape
    nk = S // tq
    c2 = gcol[:, :, LANE_FOX:LANE_FOX + N_HEADS] * (1.0 / np.log(2.0))
    first = c2[:, 0::tq, :]
    last = c2[:, tq - 1::tq, :]
    dead = (first[:, :, None, :] - last[:, None, :, :]) < -FOX_SKIP_LOG2
    dead = dead & (jnp.arange(nk)[None, :, None, None] > jnp.arange(nk)[None, None, :, None])
    return jnp.min(jnp.sum(dead, axis=2), axis=-1).astype(I32).reshape(B * nk)


def _fox(big3, gcol, qg, kg, og, tq):
    B, S, _ = big3.shape
    nk = S // tq
    row = pl.BlockSpec((1, GROUP_W), lambda b, i, js: (0, 0))
    grid_spec = pltpu.PrefetchScalarGridSpec(
        num_scalar_prefetch=1, grid=(B, nk),
        in_specs=[pl.BlockSpec((None, tq, GROUP_W), lambda b, i, js: (b, i, COL_FQ)),
                  pl.BlockSpec((None, S, GROUP_W), lambda b, i, js: (b, 0, COL_FK)),
                  pl.BlockSpec((None, S, GROUP_W), lambda b, i, js: (b, 0, COL_FV)),
                  pl.BlockSpec((None, S, LANES), lambda b, i, js: (b, 0, 0)),
                  row, row, row, row],
        out_specs=pl.BlockSpec((None, tq, GROUP_W), lambda b, i, js: (b, i, 0)),
        scratch_shapes=[pltpu.VMEM((N_HEADS, nk, tq, LANES), BF16),
                        pltpu.VMEM((N_HEADS, nk, FOX_VT_ROWS, tq), BF16),
                        pltpu.VMEM((N_HEADS, tq, LANES), BF16),
                        pltpu.VMEM((N_HEADS, 1, tq), F32),
                        pltpu.VMEM((N_HEADS, 1, tq), F32),
                        pltpu.VMEM((N_HEADS, FOX_VT_ROWS, tq), F32),
                        pltpu.VMEM((N_HEADS, tq, tq), F32)])
    gmax = lambda g: jnp.max(jnp.abs(g.reshape(N_HEADS, HEAD_DIM)), axis=1)
    bound = jnp.repeat(gmax(qg) * gmax(kg) * (HEAD_DIM * HEAD_DIM ** -0.5 / np.log(2.0) * 1.02), HEAD_DIM)
    return pl.pallas_call(
        functools.partial(_fox_body, tq=tq, nk=nk), grid_spec=grid_spec,
        out_shape=jax.ShapeDtypeStruct((B, S, GROUP_W), F32),
        compiler_params=_params("parallel", "arbitrary"), name="fox")(
            _fox_first_block(gcol, tq), big3, big3, big3, gcol, qg, kg, og, bound.reshape(1, GROUP_W))


def _gelu(x):
    return 0.5 * x * (1.0 + lax.erf(x * (2.0 ** -0.5)))


def _gmlp_body(u_ref, v_ref, lg_ref, lb_ref, ws_ref, bst_ref, og_ref, o_ref, *, nwin):
    L = GMLP_LEN
    r = lax.broadcasted_iota(I32, (L, L), 0) // CHUNK
    c = lax.broadcasted_iota(I32, (L, L), 1) // CHUNK
    mask = r >= c
    ws = [jnp.where(mask, ws_ref[h], 0.0).astype(BF16) for h in range(N_HEADS)]
    for n in range(nwin):
        u = _gelu(u_ref[n * L:(n + 1) * L, :].astype(F32))
        v = _gelu(v_ref[n * L:(n + 1) * L, :].astype(F32))
        mu = jnp.mean(v, axis=-1, keepdims=True)
        vc = v - mu
        var = jnp.mean(vc * vc, axis=-1, keepdims=True)
        vn = (vc * lax.rsqrt(var + EPS) * lg_ref[...] + lb_ref[...]).astype(BF16)
        mixed = jnp.concatenate(
            [jnp.dot(ws[h], vn[:, h * HEAD_DIM:(h + 1) * HEAD_DIM], preferred_element_type=F32)
             for h in range(N_HEADS)], axis=-1) + bst_ref[...]
        o_ref[n * L:(n + 1) * L, :] = _rms(u * mixed, og_ref[...])


def _gmlp(big, lg, lb, ws, bst, og, tm):
    T = big.shape[0]
    row = pl.BlockSpec((1, GROUP_W), lambda i: (0, 0))
    return pl.pallas_call(
        functools.partial(_gmlp_body, nwin=tm // GMLP_LEN), grid=(T // tm,),
        in_specs=[pl.BlockSpec((tm, GROUP_W), lambda i: (i, COL_GU)),
                  pl.BlockSpec((tm, GROUP_W), lambda i: (i, COL_GV)),
                  row, row,
                  pl.BlockSpec((N_HEADS, GMLP_LEN, GMLP_LEN), lambda i: (0, 0, 0)),
                  pl.BlockSpec((GMLP_LEN, GROUP_W), lambda i: (0, 0)),
                  row],
        out_specs=pl.BlockSpec((tm, GROUP_W), lambda i: (i, 0)),
        out_shape=jax.ShapeDtypeStruct((T, GROUP_W), F32),
        compiler_params=_params("parallel"), name="gmlp")(big, big, lg, lb, ws, bst, og)


def _gdn_body(q_ref, k_ref, v_ref, w_ref, gate_ref, gx_ref, ng_ref, o_ref,
              s_scr, u_scr, wq_scr, a_scr, kd_scr, dl_scr, t_scr, p_scr, rhs_scr, halo_scr, qkv_scr, *, nchunk, nb):
    C = CHUNK
    ts = nchunk * C
    first = pl.program_id(1) == 0

    @pl.when(first)
    def _():
        s_scr[...] = jnp.zeros_like(s_scr)

    ones_bd = _head_ones()
    for bb, a in [(bb, a) for bb in range(nb) for a in range(3)]:
        w = w_ref[a]
        x = (q_ref, k_ref, v_ref)[a][bb].astype(F32)
        xx = jnp.concatenate([jnp.where(first, 0.0, halo_scr[bb * 3 + a]), x], axis=0)
        halo_scr[bb * 3 + a] = x[ts - SUBLANES:, :]
        y = w[CONV_K - 1:CONV_K, :] * x
        for j in range(CONV_K - 1):
            y = y + w[j:j + 1, :] * pltpu.roll(xx, CONV_K - 1 - j, 0)[SUBLANES:, :]
        y = y * jax.nn.sigmoid(y)
        if a < 2:
            y = y * lax.rsqrt(_head_sums(y * y, ones_bd) + EPS)
        qkv_scr[bb * 3 + a] = y * (HEAD_DIM ** -0.5) if a == 0 else y

    W = GROUP_W
    pos = lax.broadcasted_iota(I32, (C, W), 1) % HEAD_DIM
    r = lax.broadcasted_iota(I32, (C, W), 0)
    tri, strict, eye = r >= pos, r > pos, r == pos
    same_head = (lax.broadcasted_iota(I32, (W, W), 0) // HEAD_DIM
                 == lax.broadcasted_iota(I32, (W, W), 1) // HEAD_DIM)
    mmb = functools.partial(jnp.dot, preferred_element_type=F32)

    def block_diag(x):
        return jnp.where(same_head, jnp.concatenate([x.astype(BF16)] * N_HEADS, axis=0), 0.0)

    items = nb * nchunk
    for n in range(items):
        bb = n // nchunk
        rows = slice((n % nchunk) * C, (n % nchunk + 1) * C)
        q, k, v = (qkv_scr[bb * 3 + a, rows, :] for a in range(3))
        gc = gx_ref[bb, rows, :W]
        beta = gx_ref[bb, rows, W:]
        gr = jnp.sum(jnp.where(eye, gc, 0.0), axis=0, keepdims=True)
        decay = jnp.exp(jnp.where(tri, gc - gr, -jnp.inf))
        kb = k * beta
        kk = lax.dot_general(jnp.concatenate([kb, q], axis=0).astype(BF16), block_diag(k),
                             (((1,), (1,)), ((), ())), preferred_element_type=F32)
        x = jnp.where(strict, -(kk[:C] * decay), 0.0)
        t_scr[n] = jnp.where(eye, 1.0, 0.0) + x
        p_scr[n] = x.astype(BF16)
        eg = jnp.exp(gc)
        g_last = gc[C - 1:C, :]
        rhs_scr[n, 0] = block_diag(v * beta)
        rhs_scr[n, 1] = block_diag(kb * eg)
        wq_scr[n, C:, :] = (q * eg).astype(BF16)
        a_scr[n] = jnp.where(tri, kk[C:] * decay, 0.0).astype(BF16)
        kd_scr[n] = (k * jnp.exp(g_last - gc)).astype(BF16)
        dl_scr[n] = jnp.exp(g_last)
    for level in range(1, 6):
        for n in range(items):
            p = p_scr[n]
            p_scr[n] = mmb(p, block_diag(p)).astype(BF16)
        for n in range(items):
            t = t_scr[n]
            t_scr[n] = t + mmb(t.astype(BF16), block_diag(p_scr[n]))
    for n in range(items):
        t = t_scr[n].astype(BF16)
        u_scr[n] = mmb(t, rhs_scr[n, 0])
        wq_scr[n, :C, :] = mmb(t, rhs_scr[n, 1]).astype(BF16)

    ones_bd = _head_ones()
    batch = range(nb)
    state = [s_scr[bb] for bb in batch]
    for c in range(nchunk):
        rows = slice(c * C, (c + 1) * C)
        it = [bb * nchunk + c for bb in batch]
        ws = [mmb(wq_scr[it[bb]], state[bb].astype(BF16)) for bb in batch]
        vb = [(u_scr[it[bb]] - ws[bb][:C]).astype(BF16) for bb in batch]
        o = [ws[bb][C:] + mmb(a_scr[it[bb]], block_diag(vb[bb])) for bb in batch]
        kv = [lax.dot_general(kd_scr[it[bb]], vb[bb], (((0,), (0,)), ((), ())), preferred_element_type=F32)
              for bb in batch]
        state = [state[bb] * dl_scr[it[bb]] + jnp.where(same_head, kv[bb], 0.0) for bb in batch]
        for bb in batch:
            gate = gate_ref[bb, rows, :].astype(F32)
            y = o[bb] * lax.rsqrt(_head_sums(o[bb] * o[bb], ones_bd) * (1.0 / HEAD_DIM) + EPS) * ng_ref[...]
            o_ref[bb, rows, :] = y * (gate * jax.nn.sigmoid(gate))
    for bb in batch:
        s_scr[bb] = state[bb]


def _gdn(big3, conv_w3, gx, ng, ts, nb):
    B, S, _ = big3.shape
    nchunk = ts // CHUNK
    col = lambda c: pl.BlockSpec((nb, ts, GROUP_W), lambda b, i: (b, i, c))
    W = GROUP_W
    items = nb * nchunk
    return pl.pallas_call(
        functools.partial(_gdn_body, nchunk=nchunk, nb=nb), grid=(B // nb, S // ts),
        in_specs=[col(COL_DQ), col(COL_DK), col(COL_DV),
                  pl.BlockSpec((3, CONV_K, W), lambda b, i: (0, 0, 0)),
                  col(COL_DG),
                  pl.BlockSpec((nb, ts, 2 * W), lambda b, i: (b, i, 0)),
                  pl.BlockSpec((1, W), lambda b, i: (0, 0))],
        out_specs=col(0), out_shape=jax.ShapeDtypeStruct((B, S, W), F32),
        scratch_shapes=[pltpu.VMEM((nb, W, W), F32),
                        pltpu.VMEM((items, CHUNK, W), F32),
                        pltpu.VMEM((items, 2 * CHUNK, W), BF16),
                        pltpu.VMEM((items, CHUNK, W), BF16),
                        pltpu.VMEM((items, CHUNK, W), BF16),
                        pltpu.VMEM((items, 1, W), F32),
                        pltpu.VMEM((items, CHUNK, W), F32),
                        pltpu.VMEM((items, CHUNK, W), BF16),
                        pltpu.VMEM((items, 2, W, W), BF16),
                        pltpu.VMEM((nb * 3, SUBLANES, W), F32),
                        pltpu.VMEM((nb * 3, ts, W), F32)],
        compiler_params=_params("parallel", "arbitrary"), name="gdn")(
            big3, big3, big3, conv_w3, big3, gx, ng)


def _pool_body(z_ref, halo_ref, w_ref, sc_ref, og_ref, o_ref, *, ts):
    i = pl.program_id(1)
    hr = 2 * SUBLANES
    z = z_ref[...].astype(F32)
    halo = jnp.where(i == 0, 0.0, halo_ref[...].astype(F32))
    s1 = jnp.concatenate([halo, z], axis=0)
    s2 = s1 + pltpu.roll(s1, 1, 0)
    s4 = s2 + pltpu.roll(s2, 2, 0)
    s8 = s4 + pltpu.roll(s4, 4, 0)
    s16 = s8 + pltpu.roll(s8, 8, 0)
    grp = lax.broadcasted_iota(I32, (ts, GROUP_W), 1) // (GROUP_W // len(POOL_WINDOWS))
    t = lax.broadcasted_iota(I32, (ts, GROUP_W), 0) + i * ts
    total = jnp.where(grp == 0, s2[hr:], jnp.where(grp == 1, s4[hr:], jnp.where(grp == 2, s8[hr:], s16[hr:])))
    win = jnp.where(grp == 0, POOL_WINDOWS[0], jnp.where(grp == 1, POOL_WINDOWS[1],
                    jnp.where(grp == 2, POOL_WINDOWS[2], POOL_WINDOWS[3])))
    pooled = total / jnp.minimum(t + 1, win).astype(F32)
    y = _mm(pooled - z, w_ref[...]) * sc_ref[...]
    o_ref[...] = _rms(y, og_ref[...])


def _pool(big3, wbd, sc, og, ts):
    B, S, _ = big3.shape
    hr = 2 * SUBLANES
    hb = ts // hr
    row = pl.BlockSpec((1, GROUP_W), lambda b, i: (0, 0))
    return pl.pallas_call(
        functools.partial(_pool_body, ts=ts), grid=(B, S // ts),
        in_specs=[pl.BlockSpec((None, ts, GROUP_W), lambda b, i: (b, i, COL_PZ)),
                  pl.BlockSpec((None, hr, GROUP_W), lambda b, i: (b, jnp.maximum(i * hb - 1, 0), COL_PZ)),
                  pl.BlockSpec((GROUP_W, GROUP_W), lambda b, i: (0, 0)), row, row],
        out_specs=pl.BlockSpec((None, ts, GROUP_W), lambda b, i: (b, i, 0)),
        out_shape=jax.ShapeDtypeStruct((B, S, GROUP_W), F32),
        compiler_params=_params("parallel", "parallel"), name="pool")(big3, big3, wbd, sc, og)


def _outproj_body(ya_ref, yb_ref, yc_ref, yd_ref, h_ref, wo_ref, g_ref, rw_ref, rb_ref,
                  hnew_ref, hn_ref, ri_ref, rf_ref, cnt_ref, *, tm, tr):
    y = jnp.concatenate([ya_ref[...], yb_ref[...], yc_ref[...], yd_ref[...]], axis=-1).astype(BF16)
    h_new = h_ref[...] + jnp.dot(y, wo_ref[...], preferred_element_type=F32)
    hnew_ref[...] = h_new
    hn = _rms(h_new, g_ref[...])
    hn_hi, hn_lo = _split(hn)
    hn_ref[...] = hn_hi
    t = jnp.dot(hn_hi, rw_ref[...], preferred_element_type=F32)
    logits = (t[:, :LANES] + t[:, LANES:]
              + jnp.dot(hn_lo, rw_ref[:, :LANES], preferred_element_type=F32)) + rb_ref[...]
    lane = lax.broadcasted_iota(I32, (tm, LANES), 1)
    neg = -jnp.inf
    big_lane = LANES

    def masked_top(vals, mask):
        v = jnp.where(mask, vals, neg)
        mx = jnp.max(v, axis=-1, keepdims=True)
        idx = jnp.min(jnp.where(mask & (v == mx), lane, big_lane), axis=-1, keepdims=True)
        return v, mx, idx

    gmask = lane < N_GROUPS
    gv, gmx, gidx = masked_top(logits, gmask)
    g_top = 1.0 / jnp.sum(jnp.where(gmask, jnp.exp(gv - gmx), 0.0), axis=-1, keepdims=True)
    lo = N_GROUPS + gidx * EXPERTS_PER_GROUP
    emask = (lane >= lo) & (lane < lo + EXPERTS_PER_GROUP)
    ev, emx, eidx1 = masked_top(logits, emask)
    esum = jnp.sum(jnp.where(emask, jnp.exp(ev - emx), 0.0), axis=-1, keepdims=True)
    p1 = 1.0 / esum
    _, emx2, eidx2 = masked_top(logits, emask & (lane != eidx1))
    p2 = jnp.exp(emx2 - emx) / esum
    denom = p1 + p2
    rf_ref[...] = jnp.where(lane == 0, g_top * p1 / denom, jnp.where(lane == 1, g_top * p2 / denom, 0.0))

    expert = [eidx1 - N_GROUPS, eidx2 - N_GROUPS]
    hot = [lane == e for e in expert]
    m = (hot[0] | hot[1]).astype(BF16)
    below = (lax.broadcasted_iota(I32, (tr, tr), 0) > lax.broadcasted_iota(I32, (tr, tr), 1)).astype(BF16)
    before = jnp.concatenate(
        [jnp.dot(below, m[k * tr:(k + 1) * tr, :], preferred_element_type=F32) for k in range(tm // tr)], axis=0)
    rank = [jnp.sum(jnp.where(hot[s], before, 0.0), axis=-1, keepdims=True).astype(I32) for s in range(TOPK_IN)]
    out = jnp.zeros((tm, LANES), I32)
    for s in range(TOPK_IN):
        out = jnp.where(lane == s, expert[s], jnp.where(lane == TOPK_IN + s, rank[s], out))
    ri_ref[...] = out
    for k in range(tm // tr):
        last = (k + 1) * tr - 1
        total = before[last:last + 1, :] + m[last:last + 1, :].astype(F32)
        cnt_ref[k * SUBLANES:(k + 1) * SUBLANES, :] = jnp.broadcast_to(total, (SUBLANES, LANES)).astype(I32)


def _outproj(ya, yb, yc, yd, h2d, wo, g, rw, rb, tm, tr):
    T, D = h2d.shape
    yblk = pl.BlockSpec((tm, GROUP_W), lambda i: (i, 0))
    cnt_rows = tm // tr * SUBLANES
    return pl.pallas_call(
        functools.partial(_outproj_body, tm=tm, tr=tr), grid=(T // tm,),
        in_specs=[yblk, yblk, yblk, yblk,
                  pl.BlockSpec((tm, D), lambda i: (i, 0)),
                  pl.BlockSpec((D, D), lambda i: (0, 0)),
                  pl.BlockSpec((1, D), lambda i: (0, 0)),
                  pl.BlockSpec((D, 2 * LANES), lambda i: (0, 0)),
                  pl.BlockSpec((1, LANES), lambda i: (0, 0))],
        out_specs=[pl.BlockSpec((tm, D), lambda i: (i, 0)),
                   pl.BlockSpec((tm, D), lambda i: (i, 0)),
                   pl.BlockSpec((tm, LANES), lambda i: (i, 0)),
                   pl.BlockSpec((tm, LANES), lambda i: (i, 0)),
                   pl.BlockSpec((cnt_rows, LANES), lambda i: (i, 0))],
        out_shape=[jax.ShapeDtypeStruct((T, D), F32),
                   jax.ShapeDtypeStruct((T, D), BF16),
                   jax.ShapeDtypeStruct((T, LANES), I32),
                   jax.ShapeDtypeStruct((T, LANES), F32),
                   jax.ShapeDtypeStruct((T // tr * SUBLANES, LANES), I32)],
        compiler_params=_params("parallel"), name="outproj")(ya, yb, yc, yd, h2d, wo, g, rw, rb)


def _dispatch_plan(expert, lrank, cnt_tile, tm):
    T = expert.shape[0]
    nt = T // tm
    counts = jnp.sum(cnt_tile, axis=0)
    padded = (counts + RUN_ROWS + MOE_BLOCK - 1) // MOE_BLOCK * MOE_BLOCK
    pad_end = jnp.cumsum(padded)
    pad_start = pad_end - padded
    gstart = pad_start[None, :] + jnp.cumsum(cnt_tile, axis=0) - cnt_tile
    nchunk = (cnt_tile + RUN_ROWS - 1) // RUN_ROWS
    chunk_end = jnp.cumsum(nchunk, axis=1)
    lstart = (chunk_end - nchunk) * RUN_ROWS
    onehot = expert[:, :, None] == jnp.arange(N_EXPERTS, dtype=I32)[None, None, :]
    pick = lambda tab: jnp.sum(jnp.where(onehot, jnp.repeat(tab, tm, axis=0)[:, None, :], 0), axis=-1)
    lpos =(lrank + pick(lstart)).astype(I32)
    max_chunks = tm * TOPK_IN // RUN_ROWS + N_EXPERTS
    c = jnp.arange(max_chunks, dtype=I32)
    ce = jnp.minimum(jnp.sum(chunk_end[:, None, :] <= c[None, :, None], axis=-1), N_EXPERTS - 1)
    ce_hot = ce[:, :, None] == jnp.arange(N_EXPERTS, dtype=I32)[None, None, :]
    take = lambda tab: jnp.sum(jnp.where(ce_hot, tab[:, None, :], 0), axis=-1)
    chunk_row = take(gstart) + (c[None, :] - take(chunk_end - nchunk)) * RUN_ROWS
    chunk_row = jnp.where(c[None, :] < chunk_end[:, -1:], chunk_row, 0)
    table = jnp.concatenate([chunk_row.astype(I32), jnp.zeros((nt, LANES - 1 - max_chunks), I32),
                             chunk_end[:, -1:].astype(I32)], axis=1).reshape(nt, 1, LANES)
    n_blk = -(-(T * TOPK_IN + N_EXPERTS * RUN_ROWS) // MOE_BLOCK) + N_EXPERTS + 1
    blk_start = jnp.arange(n_blk, dtype=I32) * MOE_BLOCK
    blk_e = jnp.minimum(jnp.sum(pad_end[None, :] <= blk_start[:, None], axis=-1), N_EXPERTS - 1).astype(I32)
    n_used = (pad_end[-1] // MOE_BLOCK).astype(I32).reshape(1)
    zero_start = jnp.concatenate([pad_start + counts, pad_end[-1:]])
    zero_end = jnp.concatenate([pad_end, jnp.full((1,), n_blk * MOE_BLOCK, pad_end.dtype)])
    nz = N_EXPERTS + 1
    zfill = jnp.concatenate([zero_start, (zero_end - zero_start) // RUN_ROWS, zero_end - RUN_ROWS,
                             jnp.zeros((LANES - 3 * nz,), zero_start.dtype)]).astype(I32).reshape(1, LANES)
    return lpos, table, zfill, blk_e, n_used, n_blk


def _dispatch_body(zf_ref, tab_ref, lpos_ref, hn_ref, x_hbm, xs, zero, sem, zsem, n_prev, *, tm, nseg, nrow):
    i = pl.program_id(0)
    slot = i % 2
    step = RUN_ROWS * nseg

    @pl.when(i == 0)
    def _():
        zero[...] = jnp.zeros_like(zero)

        def zero_copy(r):
            return pltpu.make_async_copy(zero, x_hbm.at[pl.ds(pl.multiple_of(r * nseg, nseg), step), :], zsem)

        nz = N_EXPERTS + 1

        def for_chunks(fn):
            for e in range(nz):
                def body(k, carry, e=e):
                    fn(zero_copy(zf_ref[0, e] + k * RUN_ROWS))
                    return carry
                lax.fori_loop(0, zf_ref[0, nz + e], body, 0)

        for_chunks(lambda c: c.start())
        for_chunks(lambda c: c.wait())
        for e in range(nz):
            zero_copy(zf_ref[0, 2 * nz + e]).start()
        for e in range(nz):
            zero_copy(zf_ref[0, 2 * nz + e]).wait()

    row = lax.broadcasted_iota(I32, (nrow, tm), 0)
    sel = (row == lpos_ref[0:1, :]) | (row == lpos_ref[1:2, :])
    rows = jnp.dot(sel.astype(BF16), hn_ref[...], preferred_element_type=F32)
    for k in range(nseg):
        xs[slot, pl.ds(k, nrow, stride=nseg), :] = rows[:, k * LANES:(k + 1) * LANES]

    def chunk_copy(sl, c):
        return pltpu.make_async_copy(
            xs.at[sl, pl.ds(pl.multiple_of(c * step, step), step), :],
            x_hbm.at[pl.ds(pl.multiple_of(tab_ref[0, 0, c] * nseg, nseg), step), :], sem.at[sl])

    def wait_chunks(sl, n):
        def wait(c, carry):
            chunk_copy(sl, 0).wait()
            return carry
        lax.fori_loop(0, n, wait, 0)

    n_chunks = tab_ref[0, 0, LANES - 1]

    @pl.when(i > 0)
    def _():
        wait_chunks(1 - slot, n_prev[0])

    def start(c, carry):
        chunk_copy(slot, c).start()
        return carry

    lax.fori_loop(0, n_chunks, start, 0)
    n_prev[0] = n_chunks

    @pl.when(i == pl.num_programs(0) - 1)
    def _():
        wait_chunks(slot, n_chunks)


def _dispatch(zfill, table, lpos_t, hn, n_pad, tm):
    T, D = hn.shape
    nseg = D // LANES
    nrow = tm * TOPK_IN + N_EXPERTS * RUN_ROWS
    return pl.pallas_call(
        functools.partial(_dispatch_body, tm=tm, nseg=nseg, nrow=nrow), grid=(T // tm,),
        in_specs=[pl.BlockSpec(memory_space=pltpu.SMEM),
                  pl.BlockSpec((1, 1, LANES), lambda i: (i, 0, 0), memory_space=pltpu.SMEM),
                  pl.BlockSpec((None, TOPK_IN, tm), lambda i: (i, 0, 0)),
                  pl.BlockSpec((tm, D), lambda i: (i, 0))],
        out_specs=pl.BlockSpec(memory_space=pl.ANY),
        out_shape=jax.ShapeDtypeStruct((n_pad * nseg, LANES), F32),
        scratch_shapes=[pltpu.VMEM((2, nrow * nseg, LANES), F32), pltpu.VMEM((RUN_ROWS * nseg, LANES), F32),
                        pltpu.SemaphoreType.DMA((2,)), pltpu.SemaphoreType.DMA, pltpu.SMEM((1,), I32)],
        compiler_params=_params("arbitrary"), name="dispatch")(zfill, table, lpos_t, hn)


def _moe_body(blk_e_ref, n_used_ref, seg_ref, nxt_ref, x_ref, w1_hbm, w3_hbm, w2_hbm, y_ref,
              wf1, wf3, wf2, w1b, w3b, w2b, sem, *, nseg, layer):
    b = pl.program_id(0)
    R = MOE_BLOCK

    def fetch(e, slot):
        return [pltpu.make_async_copy(src.at[layer, e], dst.at[slot], sem.at[slot, k])
                for k, (src, dst) in enumerate(((w1_hbm, wf1), (w3_hbm, wf3), (w2_hbm, wf2)))]

    @pl.when(b < n_used_ref[0])
    def _():
        @pl.when((b == 0) | (blk_e_ref[b] != blk_e_ref[jnp.maximum(b - 1, 0)]))
        def _():
            e = blk_e_ref[b]
            slot = seg_ref[b] % 2

            @pl.when(b == 0)
            def _():
                for c in fetch(e, slot):
                    c.start()

            for c in fetch(e, slot):
                c.wait()
            w1b[...] = wf1[slot].astype(BF16)
            w3b[...] = wf3[slot].astype(BF16)
            w2b[...] = wf2[slot].astype(BF16)

            @pl.when(nxt_ref[b] >= 0)
            def _():
                for c in fetch(nxt_ref[b], 1 - slot):
                    c.start()

        x = jnp.concatenate([x_ref[pl.ds(s, R, stride=nseg), :] for s in range(nseg)], axis=-1).astype(BF16)
        a = jnp.dot(x, w1b[...], preferred_element_type=F32)
        g = jnp.dot(x, w3b[...], preferred_element_type=F32)
        hid = (a * jax.nn.sigmoid(a) * g).astype(BF16)
        y = jnp.dot(hid, w2b[...], preferred_element_type=F32)
        for s in range(nseg):
            y_ref[pl.ds(s, R, stride=nseg), :] = y[:, s * LANES:(s + 1) * LANES]

    @pl.when(b >= n_used_ref[0])
    def _():
        y_ref[...] = jnp.zeros_like(y_ref)


def _moe(blk_e, n_used, x_rows, w1, w3, w2, layer):
    n_blk = blk_e.shape[0]
    _, _, D, DE = w1.shape
    nseg = D // LANES
    R = MOE_BLOCK
    blk = jnp.arange(n_blk, dtype=I32)
    used = blk < n_used[0]
    change = (blk > 0) & (blk_e != jnp.roll(blk_e, 1)) & used
    seg = jnp.cumsum(change.astype(I32)).astype(I32)
    later = (blk[None, :] > blk[:, None]) & (seg[None, :] > seg[:, None]) & used[None, :]
    nxt_blk = jnp.min(jnp.where(later, blk[None, :], n_blk), axis=1)
    nxt = jnp.where(nxt_blk < n_blk, blk_e[jnp.minimum(nxt_blk, n_blk - 1)], -1).astype(I32)
    rows = lambda b, be, nu, sg, nx: (jnp.minimum(b, nu[0] - 1), 0)
    grid_spec = pltpu.PrefetchScalarGridSpec(
        num_scalar_prefetch=4, grid=(n_blk,),
        in_specs=[pl.BlockSpec((R * nseg, LANES), rows),
                  pl.BlockSpec(memory_space=pl.ANY), pl.BlockSpec(memory_space=pl.ANY),
                  pl.BlockSpec(memory_space=pl.ANY)],
        out_specs=pl.BlockSpec((R * nseg, LANES), lambda b, be, nu, sg, nx: (b, 0)),
        scratch_shapes=[pltpu.VMEM((2, D, DE), F32), pltpu.VMEM((2, D, DE), F32), pltpu.VMEM((2, DE, D), F32),
                        pltpu.VMEM((D, DE), BF16), pltpu.VMEM((D, DE), BF16), pltpu.VMEM((DE, D), BF16),
                        pltpu.SemaphoreType.DMA((2, 3))])
    return pl.pallas_call(
        functools.partial(_moe_body, nseg=nseg, layer=layer), grid_spec=grid_spec,
        out_shape=jax.ShapeDtypeStruct(x_rows.shape, F32),
        compiler_params=_params("arbitrary"), name="moe")(blk_e, n_used, seg, nxt, x_rows, w1, w3, w2)


def _combine_body(tab_ref, tabn_ref, h_ref, rf_ref, lpos_ref, y_hbm, o_ref, ybuf, sem, *, tm, nseg, nrow):
    i = pl.program_id(0)
    slot = i % 2
    step = RUN_ROWS * nseg

    def start_chunks(tab, sl):
        def start(c, carry):
            pltpu.make_async_copy(
                y_hbm.at[pl.ds(pl.multiple_of(tab[0, 0, c] * nseg, nseg), step), :],
                ybuf.at[sl, pl.ds(pl.multiple_of(c * step, step), step), :], sem.at[sl]).start()
            return carry
        lax.fori_loop(0, tab[0, 0, LANES - 1], start, 0)

    @pl.when(i == 0)
    def _():
        ybuf[...] = jnp.zeros_like(ybuf)
        start_chunks(tab_ref, 0)

    @pl.when(i + 1 < pl.num_programs(0))
    def _():
        start_chunks(tabn_ref, 1 - slot)

    col = lax.broadcasted_iota(I32, (tm, nrow), 1)
    sel = jnp.zeros((tm, nrow), F32)
    for s in range(TOPK_IN):
        sel = sel + jnp.where(col == lpos_ref[:, s:s + 1], rf_ref[:, s:s + 1], 0.0)

    def wait(c, carry):
        pltpu.make_async_copy(y_hbm.at[pl.ds(0, step), :], ybuf.at[slot, pl.ds(0, step), :],
                              sem.at[slot]).wait()
        return carry

    lax.fori_loop(0, tab_ref[0, 0, LANES - 1], wait, 0)
    y = jnp.concatenate([ybuf[slot, pl.ds(k, nrow, stride=nseg), :] for k in range(nseg)], axis=-1)
    o_ref[...] = h_ref[...] + jnp.dot(sel.astype(BF16), y.astype(BF16), preferred_element_type=F32)


def _combine(table, h2d, rf, lpos, y_rows, tm):
    T, D = h2d.shape
    nseg = D // LANES
    nt = T // tm
    nrow = tm * TOPK_IN + N_EXPERTS * RUN_ROWS
    return pl.pallas_call(
        functools.partial(_combine_body, tm=tm, nseg=nseg, nrow=nrow), grid=(nt,),
        in_specs=[pl.BlockSpec((1, 1, LANES), lambda i: (i, 0, 0), memory_space=pltpu.SMEM),
                  pl.BlockSpec((1, 1, LANES), lambda i: (jnp.minimum(i + 1, nt - 1), 0, 0),
                               memory_space=pltpu.SMEM),
                  pl.BlockSpec((tm, D), lambda i: (i, 0)),
                  pl.BlockSpec((tm, LANES), lambda i: (i, 0)),
                  pl.BlockSpec((tm, TOPK_IN), lambda i: (i, 0)),
                  pl.BlockSpec(memory_space=pl.ANY)],
        out_specs=pl.BlockSpec((tm, D), lambda i: (i, 0)),
        out_shape=jax.ShapeDtypeStruct((T, D), F32),
        scratch_shapes=[pltpu.VMEM((2, nrow * nseg, LANES), F32), pltpu.SemaphoreType.DMA((2,))],
        compiler_params=_params("arbitrary"), name="combine")(table, table, h2d, rf, lpos, y_rows)


def _pad_lanes(a, lane0, rows=1):
    a = a.reshape(rows, -1)
    return jnp.pad(a, ((0, 0), (lane0, LANES - lane0 - a.shape[-1])))


def _layer(h2d, B, S, p):
    T, D = h2d.shape
    tile = lambda a, n: jnp.tile(a.reshape(1, -1), (1, n))

    offs = np.cumsum([0, GROUP_W, GROUP_W, GROUP_W, N_HEADS, GROUP_W, GROUP_W,
                      GROUP_W, GROUP_W, GROUP_W, N_HEADS, N_HEADS, GROUP_W, GROUP_W])
    seg = lambda k: p['w_in'][:, offs[k]:offs[k + 1]]
    w_all = jnp.concatenate([seg(0), seg(1), seg(2), seg(4), seg(5), seg(6), seg(7), seg(8), seg(11), seg(12),
                             seg(3), seg(9), seg(10), jnp.zeros((D, LANES - 3 * N_HEADS), F32)],
                            axis=1).astype(BF16)
    gate_prm = jnp.concatenate([_pad_lanes(p['fox_f_bias'], LANE_FOX), _pad_lanes(p['gdn_dt_bias'], LANE_DECAY),
                                _pad_lanes(p['gdn_a_log'], LANE_DECAY), jnp.zeros((SUBLANES - 3, LANES), F32)], axis=0)

    big, small = _inproj(h2d, p['attn_norm_g'].reshape(1, D), w_all, tm=min(512, T))
    big3 = big.reshape(B, S, N_BIG_COLS * GROUP_W)
    ts = min(512, S)
    gcol, gx = _gates(small.reshape(B, S, LANES), gate_prm, ts)

    tq = min(256, S)
    ya = _fox(big3, gcol, tile(p['fox_qn_g'], N_HEADS), tile(p['fox_kn_g'], N_HEADS),
              p['fox_out_g'].reshape(1, GROUP_W), tq)

    bst = jnp.repeat(p['gmlp_bs'].T, HEAD_DIM, axis=1)
    yb = _gmlp(big, p['gmlp_ln_g'].reshape(1, -1), p['gmlp_ln_b'].reshape(1, -1), p['gmlp_ws'], bst,
               p['gmlp_out_g'].reshape(1, -1), tm=min(512, T))

    conv_w3 = p['gdn_conv_w'].reshape(CONV_K, 3, GROUP_W).transpose(1, 0, 2)
    nb = 4 if B % 4 == 0 else (2 if B % 2 == 0 else 1)
    yc = _gdn(big3, conv_w3, gx, tile(p['gdn_norm_g'], N_HEADS), min(1024 // nb, S), nb)

    wbd = jax.scipy.linalg.block_diag(*[p['pool_w'][g] for g in range(len(POOL_WINDOWS))]).astype(BF16)
    yd = _pool(big3, wbd, p['pool_scale'].reshape(1, -1), p['pool_out_g'].reshape(1, -1), ts)

    rw = _pad_lanes(jnp.concatenate([p['router_g_w'], p['router_e_w']], axis=1), 0, rows=D)
    rw = jnp.concatenate(_split(rw), axis=1)
    rb = _pad_lanes(jnp.concatenate([p['router_g_b'], p['router_e_b']]), 0)
    flat = lambda a: a.reshape(T, GROUP_W)
    tmd = min(256, T)
    h_new, hn_rows, ri, rf, cnt = _outproj(flat(ya), yb, flat(yc), flat(yd), h2d, p['w_out'].astype(BF16),
                                           p['ffn_norm_g'].reshape(1, D), rw, rb, tm=min(512, T), tr=tmd)
    cnt_tile = cnt.reshape(T // tmd, SUBLANES, LANES)[:, 0, :N_EXPERTS]
    lpos, table, zfill, blk_e, n_used, n_blk = _dispatch_plan(
        ri[:, :TOPK_IN], ri[:, TOPK_IN:2 * TOPK_IN], cnt_tile, tmd)
    lpos_t = lpos.reshape(T // tmd, tmd, TOPK_IN).transpose(0, 2, 1)
    x_rows = _dispatch(zfill, table, lpos_t, hn_rows, n_blk * MOE_BLOCK, tmd)
    y_rows = _moe(blk_e, n_used, x_rows, p['moe_w1'], p['moe_w3'], p['moe_w2'], p['layer'])
    return _combine(table, h_new, rf, lpos, y_rows, tmd)


def kernel(x, attn_norm_g, w_in, w_out, fox_f_bias, fox_qn_g, fox_kn_g, fox_out_g, gmlp_ln_g, gmlp_ln_b, gmlp_ws, gmlp_bs, gmlp_out_g, gdn_conv_w, gdn_a_log, gdn_dt_bias, gdn_norm_g, pool_w, pool_scale, pool_out_g, ffn_norm_g, router_g_w, router_g_b, router_e_w, router_e_b, moe_w1, moe_w3, moe_w2):
    B, S, D = x.shape
    names = ('attn_norm_g', 'w_in', 'w_out', 'fox_f_bias', 'fox_qn_g', 'fox_kn_g', 'fox_out_g', 'gmlp_ln_g',
             'gmlp_ln_b', 'gmlp_ws', 'gmlp_bs', 'gmlp_out_g', 'gdn_conv_w', 'gdn_a_log', 'gdn_dt_bias',
             'gdn_norm_g', 'pool_w', 'pool_scale', 'pool_out_g', 'ffn_norm_g', 'router_g_w', 'router_g_b',
             'router_e_w', 'router_e_b', 'moe_w1', 'moe_w3', 'moe_w2')
    vals = (attn_norm_g, w_in, w_out, fox_f_bias, fox_qn_g, fox_kn_g, fox_out_g, gmlp_ln_g, gmlp_ln_b, gmlp_ws,
            gmlp_bs, gmlp_out_g, gdn_conv_w, gdn_a_log, gdn_dt_bias, gdn_norm_g, pool_w, pool_scale, pool_out_g,
            ffn_norm_g, router_g_w, router_g_b, router_e_w, router_e_b, moe_w1, moe_w3, moe_w2)
    h = x.reshape(B * S, D)
    stacked = ('moe_w1', 'moe_w3', 'moe_w2')
    for l in range(w_in.shape[0]):
        p = {n: (v if n in stacked else v[l]) for n, v in zip(names, vals)}
        p['layer'] = l
        h = _layer(h, B, S, p)
    return h.reshape(B, S, D)
```

```python
import functools

import jax
import jax.numpy as jnp
import numpy as np
from jax import lax
from jax.experimental import pallas as pl
from jax.experimental.pallas import tpu as pltpu

F32 = jnp.float32
BF16 = jnp.bfloat16
I32 = jnp.int32

EPS = 1e-6
HEAD_DIM = 64
GROUP_W = 256
N_HEADS = GROUP_W // HEAD_DIM
CHUNK = 64
GMLP_LEN = 128
CONV_K = 4
POOL_WINDOWS = (2, 4, 8, 16)
N_GROUPS = 4
EXPERTS_PER_GROUP = 8
N_EXPERTS = N_GROUPS * EXPERTS_PER_GROUP
TOPK_IN = 2
MOE_BLOCK = 256
FOX_VT_ROWS = HEAD_DIM + 16
FOX_BOUND_LOG2 = 40.0
FOX_SKIP_LOG2 = 160.0
RUN_ROWS = 16
LANES = 128
SUBLANES = 8
VMEM_LIMIT = 56 * 1024 * 1024

COL_FQ, COL_FK, COL_FV, COL_GU, COL_GV, COL_DQ, COL_DK, COL_DV, COL_DG, COL_PZ = range(10)
N_BIG_COLS = 10
LANE_FOX, LANE_DECAY, LANE_BETA = 0, 4, 8


def _params(*sem):
    return pltpu.CompilerParams(dimension_semantics=sem, vmem_limit_bytes=VMEM_LIMIT)


def _head_ones():
    r = lax.broadcasted_iota(I32, (GROUP_W, GROUP_W), 0) // HEAD_DIM
    c = lax.broadcasted_iota(I32, (GROUP_W, GROUP_W), 1) // HEAD_DIM
    return (r == c).astype(BF16)


def _head_sums(x, ones_bd):
    hi = x.astype(BF16)
    lo = (x - hi.astype(F32)).astype(BF16)
    return (jnp.dot(hi, ones_bd, preferred_element_type=F32)
            + jnp.dot(lo, ones_bd, preferred_element_type=F32))


def _rms(x, g):
    return x * lax.rsqrt(jnp.mean(x * x, axis=-1, keepdims=True) + EPS) * g


def _mm(a, b):
    return jnp.dot(a.astype(BF16), b.astype(BF16), preferred_element_type=F32)


def _mm_nt(a, b):
    return lax.dot_general(a.astype(BF16), b.astype(BF16), (((1,), (1,)), ((), ())),
                           preferred_element_type=F32)


def _mm_tn(a, b):
    return lax.dot_general(a.astype(BF16), b.astype(BF16), (((0,), (0,)), ((), ())),
                           preferred_element_type=F32)


def _split(a):
    hi = a.astype(BF16)
    return hi, (a - hi.astype(F32)).astype(BF16)


def _mm3(a, b):
    ah, al = _split(a)
    bh, bl = _split(b)
    d = functools.partial(jnp.dot, preferred_element_type=F32)
    return d(ah, bh) + (d(ah, bl) + d(al, bh))


def _inproj_body(x_ref, g_ref, w_ref, big_ref, small_ref):
    xn = _rms(x_ref[...], g_ref[...]).astype(BF16)
    nb = big_ref.shape[1]
    big_ref[...] = jnp.dot(xn, w_ref[:, :nb], preferred_element_type=F32).astype(big_ref.dtype)
    small_ref[...] = jnp.dot(xn, w_ref[:, nb:], preferred_element_type=F32)


def _inproj(x2d, g, w, tm):
    T, D = x2d.shape
    nb = w.shape[1] - LANES
    return pl.pallas_call(
        _inproj_body, grid=(T // tm,),
        in_specs=[pl.BlockSpec((tm, D), lambda i: (i, 0)),
                  pl.BlockSpec((1, D), lambda i: (0, 0)),
                  pl.BlockSpec((D, nb + LANES), lambda i: (0, 0))],
        out_specs=[pl.BlockSpec((tm, nb), lambda i: (i, 0)),
                   pl.BlockSpec((tm, LANES), lambda i: (i, 0))],
        out_shape=[jax.ShapeDtypeStruct((T, nb), BF16), jax.ShapeDtypeStruct((T, LANES), F32)],
        compiler_params=_params("parallel"), name="inproj")(x2d, g, w)


def _gates_body(sm_ref, p_ref, col_ref, exp_ref, carry_ref, *, ts):
    @pl.when(pl.program_id(1) == 0)
    def _():
        carry_ref[...] = jnp.zeros_like(carry_ref)

    x = sm_ref[...]
    lane = lax.broadcasted_iota(I32, (ts, LANES), 1)
    is_fox = lane < LANE_DECAY
    is_dec = (lane >= LANE_DECAY) & (lane < LANE_BETA)
    is_beta = (lane >= LANE_BETA) & (lane < LANE_BETA + N_HEADS)
    logf = jax.nn.log_sigmoid(x + p_ref[0:1, :])
    g = -jnp.exp(p_ref[2:3, :]) * jax.nn.softplus(x + p_ref[1:2, :])
    beta = jax.nn.sigmoid(x)
    r = lax.broadcasted_iota(I32, (ts, ts), 0)
    c = lax.broadcasted_iota(I32, (ts, ts), 1)
    tri_full = (r >= c).astype(BF16)
    tri_chunk = ((r >= c) & (r // CHUNK == c // CHUNK)).astype(BF16)
    vals = jnp.where(is_fox, logf, jnp.where(is_dec, g, 0.0))
    hi = vals.astype(BF16)
    mid = (vals - hi.astype(F32)).astype(BF16)
    lo = (vals - hi.astype(F32) - mid.astype(F32)).astype(BF16)
    parts = jnp.concatenate([hi, mid, lo], axis=1)

    def tri_sum(tri):
        t = jnp.dot(tri, parts, preferred_element_type=F32)
        return t[:, :LANES] + (t[:, LANES:2 * LANES] + t[:, 2 * LANES:])

    cf = tri_sum(tri_full) + carry_ref[...]
    cg = tri_sum(tri_chunk)
    carry_ref[...] = cf[ts - 1:ts, :]
    out = jnp.where(is_fox, cf, jnp.where(is_dec, cg, jnp.where(is_beta, beta, 0.0)))
    col_ref[...] = out
    o_hi = out.astype(BF16)
    o_mid = (out - o_hi.astype(F32)).astype(BF16)
    o_lo = (out - o_hi.astype(F32) - o_mid.astype(F32)).astype(BF16)
    src = lax.broadcasted_iota(I32, (3 * LANES, 2 * GROUP_W), 0) % LANES
    dst = lax.broadcasted_iota(I32, (3 * LANES, 2 * GROUP_W), 1)
    want = jnp.where(dst < GROUP_W, LANE_DECAY, LANE_BETA) + (dst % GROUP_W) // HEAD_DIM
    exp_ref[...] = jnp.dot(jnp.concatenate([o_hi, o_mid, o_lo], axis=1), (src == want).astype(BF16),
                           preferred_element_type=F32)


def _gates(small3, prm, ts):
    B, S, _ = small3.shape
    return pl.pallas_call(
        functools.partial(_gates_body, ts=ts), grid=(B, S // ts),
        in_specs=[pl.BlockSpec((None, ts, LANES), lambda b, j: (b, j, 0)),
                  pl.BlockSpec((SUBLANES, LANES), lambda b, j: (0, 0))],
        out_specs=[pl.BlockSpec((None, ts, LANES), lambda b, j: (b, j, 0)),
                   pl.BlockSpec((None, ts, 2 * GROUP_W), lambda b, j: (b, j, 0))],
        out_shape=[jax.ShapeDtypeStruct((B, S, LANES), F32),
                   jax.ShapeDtypeStruct((B, S, 2 * GROUP_W), F32)],
        scratch_shapes=[pltpu.VMEM((1, LANES), F32)],
        compiler_params=_params("parallel", "arbitrary"), name="gates")(small3, prm)


def _fox_placer(lane0):
    src = lax.broadcasted_iota(I32, (3 * LANES, N_HEADS * LANES), 0)
    dst = lax.broadcasted_iota(I32, (3 * LANES, N_HEADS * LANES), 1)
    return ((src % LANES == LANE_FOX + dst // LANES) & (dst % LANES == lane0 + src // LANES)).astype(BF16)


def _split3_place(vals, placer):
    hi = vals.astype(BF16)
    mid = (vals - hi.astype(F32)).astype(BF16)
    lo = (vals - hi.astype(F32) - mid.astype(F32)).astype(BF16)
    return jnp.dot(jnp.concatenate([hi, mid, lo], axis=1), placer, preferred_element_type=F32)


def _fox_body(jstart_ref, q_ref, k_ref, v_ref, ccol_ref, qg_ref, kg_ref, og_ref, bound_ref, o_ref,
              kn_scr, vt_scr, q_scr, m_scr, l_scr, acc_scr, s_scr, place_scr, *, tq, nk):
    b = pl.program_id(0)
    i = pl.program_id(1)
    ones_bd = _head_ones()
    lane = lax.broadcasted_iota(I32, (tq, LANES), 1)
    log2e = 1.0 / np.log(2.0)
    c_lane, r_lane = HEAD_DIM, HEAD_DIM + 3

    def head_norm(x, g):
        ss = _head_sums(x * x, ones_bd)
        return x * lax.rsqrt(ss * (1.0 / HEAD_DIM) + EPS) * g

    def head_tile(x, h, extra):
        pair = x[:, (h // 2) * LANES:(h // 2 + 1) * LANES]
        if h % 2:
            pair = pltpu.roll(pair, HEAD_DIM, 1)
        return jnp.where(lane < HEAD_DIM, pair, extra).astype(BF16)

    @pl.when(i == 0)
    def _():
        k_ones = jnp.where((lane >= r_lane) & (lane < r_lane + 3), 1.0, 0.0)
        vt_tail = (lax.broadcasted_iota(I32, (FOX_VT_ROWS - HEAD_DIM, tq), 0) == 0).astype(BF16)
        place_scr[0] = _fox_placer(c_lane)
        place_scr[1] = _fox_placer(r_lane)
        for c in range(nk):
            rows = slice(c * tq, (c + 1) * tq)
            kc = head_norm(k_ref[rows, :].astype(F32), kg_ref[...])
            vt = v_ref[rows, :].astype(F32).T.astype(BF16)
            extras = _split3_place(ccol_ref[rows, :] * (-log2e), place_scr[0])
            for h in range(N_HEADS):
                extra = extras[:, h * LANES:(h + 1) * LANES] + k_ones
                kn_scr[h, c] = head_tile(kc, h, extra)
                vt_scr[h, c] = jnp.concatenate([vt[h * HEAD_DIM:(h + 1) * HEAD_DIM, :], vt_tail], axis=0)

    qn = head_norm(q_ref[...].astype(F32), qg_ref[...]) * (HEAD_DIM ** -0.5 * log2e)
    bound = bound_ref[...]
    bounded = jnp.max(bound) <= FOX_BOUND_LOG2
    c_i = ccol_ref[pl.ds(pl.multiple_of(i * tq, tq), tq), :] * log2e
    q_ones = jnp.where((lane >= c_lane) & (lane < c_lane + 3), 1.0, 0.0)
    neg_r = _split3_place(jnp.where(bounded, c_i - bound, 0.0), place_scr[1])
    for h in range(N_HEADS):
        q_scr[h] = head_tile(qn, h, q_ones + neg_r[:, h * LANES:(h + 1) * LANES])
    acc_scr[...] = jnp.zeros_like(acc_scr)
    causal = (lax.broadcasted_iota(I32, (tq, tq), 0) <= lax.broadcasted_iota(I32, (tq, tq), 1))

    heads = range(N_HEADS)

    def scores(j):
        return [lax.dot_general(kn_scr[h, j], q_scr[h], (((1,), (1,)), ((), ())),
                                preferred_element_type=F32) for h in heads]

    def stash(s):
        for h in heads:
            s_scr[h] = s[h]

    def absorb_general(j, masked):
        p, alpha = [], []
        for h in heads:
            s = s_scr[h]
            if masked:
                s = jnp.where(causal, s, -jnp.inf)
            m_old = m_scr[h]
            m_new = jnp.maximum(m_old, jnp.max(s, axis=0, keepdims=True))
            alpha.append(jnp.exp2(m_old - m_new))
            ph = jnp.exp2(s - m_new)
            l_scr[h] = alpha[h] * l_scr[h] + jnp.sum(ph, axis=0, keepdims=True)
            m_scr[h] = m_new
            p.append(ph.astype(BF16))
        pv = [jnp.dot(vt_scr[h, j], p[h], preferred_element_type=F32) for h in heads]
        for h in heads:
            acc_scr[h] = alpha[h] * acc_scr[h] + pv[h]

    def absorb_bounded(j, masked):
        p = []
        for h in heads:
            s = s_scr[h]
            if masked:
                s = jnp.where(causal, s, -jnp.inf)
            p.append(jnp.exp2(s).astype(BF16))
        pv = [jnp.dot(vt_scr[h, j], p[h], preferred_element_type=F32) for h in heads]
        for h in heads:
            acc_scr[h] += pv[h]

    def run(absorb, j0):
        stash(scores(j0))

        def body(j, c):
            s_next = scores(j + 1)
            absorb(j, False)
            stash(s_next)
            return c

        lax.fori_loop(j0, i, body, 0)
        absorb(i, True)

    @pl.when(bounded)
    def _():
        run(absorb_bounded, jstart_ref[b * nk + i])

    @pl.when(jnp.logical_not(bounded))
    def _():
        m_scr[...] = jnp.full_like(m_scr, -jnp.inf)
        l_scr[...] = jnp.zeros_like(l_scr)
        run(absorb_general, 0)
        for h in heads:
            acc_scr[h, HEAD_DIM:HEAD_DIM + 1, :] = l_scr[h]

    o_t = jnp.concatenate([acc_scr[h, :HEAD_DIM, :] * (1.0 / acc_scr[h, HEAD_DIM:HEAD_DIM + 1, :]) for h in heads],
                          axis=0)
    o_ref[...] = _rms(o_t.T, og_ref[...])


def _fox_first_block(gcol, tq):
    B, S, _ = gcol.shape
    nk = S // tq
    c2 = gcol[:, :, LANE_FOX:LANE_FOX + N_HEADS] * (1.0 / np.log(2.0))
    first = c2[:, 0::tq, :]
    last = c2[:, tq - 1::tq, :]
    dead = (first[:, :, None, :] - last[:, None, :, :]) < -FOX_SKIP_LOG2
    dead = dead & (jnp.arange(nk)[None, :, None, None] > jnp.arange(nk)[None, None, :, None])
    return jnp.min(jnp.sum(dead, axis=2), axis=-1).astype(I32).reshape(B * nk)


def _fox(big3, gcol, qg, kg, og, tq):
    B, S, _ = big3.shape
    nk = S // tq
    row = pl.BlockSpec((1, GROUP_W), lambda b, i, js: (0, 0))
    grid_spec = pltpu.PrefetchScalarGridSpec(
        num_scalar_prefetch=1, grid=(B, nk),
        in_specs=[pl.BlockSpec((None, tq, GROUP_W), lambda b, i, js: (b, i, COL_FQ)),
                  pl.BlockSpec((None, S, GROUP_W), lambda b, i, js: (b, 0, COL_FK)),
                  pl.BlockSpec((None, S, GROUP_W), lambda b, i, js: (b, 0, COL_FV)),
                  pl.BlockSpec((None, S, LANES), lambda b, i, js: (b, 0, 0)),
                  row, row, row, pl.BlockSpec((1, LANES), lambda b, i, js: (0, 0))],
        out_specs=pl.BlockSpec((None, tq, GROUP_W), lambda b, i, js: (b, i, 0)),
        scratch_shapes=[pltpu.VMEM((N_HEADS, nk, tq, LANES), BF16),
                        pltpu.VMEM((N_HEADS, nk, FOX_VT_ROWS, tq), BF16),
                        pltpu.VMEM((N_HEADS, tq, LANES), BF16),
                        pltpu.VMEM((N_HEADS, 1, tq), F32),
                        pltpu.VMEM((N_HEADS, 1, tq), F32),
                        pltpu.VMEM((N_HEADS, FOX_VT_ROWS, tq), F32),
                        pltpu.VMEM((N_HEADS, tq, tq), F32),
                        pltpu.VMEM((2, 3 * LANES, N_HEADS * LANES), BF16)])
    gmax = lambda g: jnp.max(jnp.abs(g.reshape(N_HEADS, HEAD_DIM)), axis=1)
    bound = _pad_lanes(gmax(qg) * gmax(kg) * (HEAD_DIM * HEAD_DIM ** -0.5 / np.log(2.0) * 1.02), LANE_FOX)
    return pl.pallas_call(
        functools.partial(_fox_body, tq=tq, nk=nk), grid_spec=grid_spec,
        out_shape=jax.ShapeDtypeStruct((B, S, GROUP_W), F32),
        compiler_params=_params("parallel", "arbitrary"), name="fox")(
            _fox_first_block(gcol, tq), big3, big3, big3, gcol, qg, kg, og, bound)


def _gelu(x):
    return 0.5 * x * (1.0 + lax.erf(x * (2.0 ** -0.5)))


def _gmlp_body(u_ref, v_ref, lg_ref, lb_ref, ws_ref, bst_ref, og_ref, o_ref, *, nwin):
    L = GMLP_LEN
    r = lax.broadcasted_iota(I32, (L, L), 0) // CHUNK
    c = lax.broadcasted_iota(I32, (L, L), 1) // CHUNK
    mask = r >= c
    ws = [jnp.where(mask, ws_ref[h], 0.0).astype(BF16) for h in range(N_HEADS)]
    for n in range(nwin):
        u = _gelu(u_ref[n * L:(n + 1) * L, :].astype(F32))
        v = _gelu(v_ref[n * L:(n + 1) * L, :].astype(F32))
        mu = jnp.mean(v, axis=-1, keepdims=True)
        vc = v - mu
        var = jnp.mean(vc * vc, axis=-1, keepdims=True)
        vn = (vc * lax.rsqrt(var + EPS) * lg_ref[...] + lb_ref[...]).astype(BF16)
        mixed = jnp.concatenate(
            [jnp.dot(ws[h], vn[:, h * HEAD_DIM:(h + 1) * HEAD_DIM], preferred_element_type=F32)
             for h in range(N_HEADS)], axis=-1) + bst_ref[...]
        o_ref[n * L:(n + 1) * L, :] = _rms(u * mixed, og_ref[...])


def _gmlp(big, lg, lb, ws, bst, og, tm):
    T = big.shape[0]
    row = pl.BlockSpec((1, GROUP_W), lambda i: (0, 0))
    return pl.pallas_call(
        functools.partial(_gmlp_body, nwin=tm // GMLP_LEN), grid=(T // tm,),
        in_specs=[pl.BlockSpec((tm, GROUP_W), lambda i: (i, COL_GU)),
                  pl.BlockSpec((tm, GROUP_W), lambda i: (i, COL_GV)),
                  row, row,
                  pl.BlockSpec((N_HEADS, GMLP_LEN, GMLP_LEN), lambda i: (0, 0, 0)),
                  pl.BlockSpec((GMLP_LEN, GROUP_W), lambda i: (0, 0)),
                  row],
        out_specs=pl.BlockSpec((tm, GROUP_W), lambda i: (i, 0)),
        out_shape=jax.ShapeDtypeStruct((T, GROUP_W), F32),
        compiler_params=_params("parallel"), name="gmlp")(big, big, lg, lb, ws, bst, og)


def _gdn_body(q_ref, k_ref, v_ref, w_ref, gate_ref, gx_ref, ng_ref, o_ref,
              s_scr, u_scr, wq_scr, a_scr, kd_scr, dl_scr, t_scr, p_scr, rhs_scr, halo_scr, qkv_scr, *, nchunk, nb):
    C = CHUNK
    ts = nchunk * C
    first = pl.program_id(1) == 0

    @pl.when(first)
    def _():
        s_scr[...] = jnp.zeros_like(s_scr)

    ones_bd = _head_ones()
    for bb, a in [(bb, a) for bb in range(nb) for a in range(3)]:
        w = w_ref[a]
        x = (q_ref, k_ref, v_ref)[a][bb].astype(F32)
        xx = jnp.concatenate([jnp.where(first, 0.0, halo_scr[bb * 3 + a]), x], axis=0)
        halo_scr[bb * 3 + a] = x[ts - SUBLANES:, :]
        y = w[CONV_K - 1:CONV_K, :] * x
        for j in range(CONV_K - 1):
            y = y + w[j:j + 1, :] * pltpu.roll(xx, CONV_K - 1 - j, 0)[SUBLANES:, :]
        y = y * jax.nn.sigmoid(y)
        if a < 2:
            y = y * lax.rsqrt(_head_sums(y * y, ones_bd) + EPS)
        qkv_scr[bb * 3 + a] = y * (HEAD_DIM ** -0.5) if a == 0 else y

    W = GROUP_W
    pos = lax.broadcasted_iota(I32, (C, W), 1) % HEAD_DIM
    r = lax.broadcasted_iota(I32, (C, W), 0)
    tri, strict, eye = r >= pos, r > pos, r == pos
    same_head = (lax.broadcasted_iota(I32, (W, W), 0) // HEAD_DIM
                 == lax.broadcasted_iota(I32, (W, W), 1) // HEAD_DIM)
    mmb = functools.partial(jnp.dot, preferred_element_type=F32)

    def block_diag(x):
        return jnp.where(same_head, jnp.concatenate([x.astype(BF16)] * N_HEADS, axis=0), 0.0)

    items = nb * nchunk
    for n in range(items):
        bb = n // nchunk
        rows = slice((n % nchunk) * C, (n % nchunk + 1) * C)
        q, k, v = (qkv_scr[bb * 3 + a, rows, :] for a in range(3))
        gc = gx_ref[bb, rows, :W]
        beta = gx_ref[bb, rows, W:]
        gr = jnp.sum(jnp.where(eye, gc, 0.0), axis=0, keepdims=True)
        decay = jnp.exp(jnp.where(tri, gc - gr, -jnp.inf))
        kb = k * beta
        kk = lax.dot_general(jnp.concatenate([kb, q], axis=0).astype(BF16), block_diag(k),
                             (((1,), (1,)), ((), ())), preferred_element_type=F32)
        x = jnp.where(strict, -(kk[:C] * decay), 0.0)
        t_scr[n] = jnp.where(eye, 1.0, 0.0) + x
        p_scr[n] = x.astype(BF16)
        eg = jnp.exp(gc)
        g_last = gc[C - 1:C, :]
        rhs_scr[n, 0] = block_diag(v * beta)
        rhs_scr[n, 1] = block_diag(kb * eg)
        wq_scr[n, C:, :] = (q * eg).astype(BF16)
        a_scr[n] = jnp.where(tri, kk[C:] * decay, 0.0).astype(BF16)
        kd_scr[n] = (k * jnp.exp(g_last - gc)).astype(BF16)
        dl_scr[n] = jnp.exp(g_last)
    for level in range(1, 6):
        for n in range(items):
            p = p_scr[n]
            p_scr[n] = mmb(p, block_diag(p)).astype(BF16)
        for n in range(items):
            t = t_scr[n]
            t_scr[n] = t + mmb(t.astype(BF16), block_diag(p_scr[n]))
    for n in range(items):
        t = t_scr[n].astype(BF16)
        u_scr[n] = mmb(t, rhs_scr[n, 0])
        wq_scr[n, :C, :] = mmb(t, rhs_scr[n, 1]).astype(BF16)

    ones_bd = _head_ones()
    batch = range(nb)
    state = [s_scr[bb] for bb in batch]
    for c in range(nchunk):
        rows = slice(c * C, (c + 1) * C)
        it = [bb * nchunk + c for bb in batch]
        ws = [mmb(wq_scr[it[bb]], state[bb].astype(BF16)) for bb in batch]
        vb = [(u_scr[it[bb]] - ws[bb][:C]).astype(BF16) for bb in batch]
        o = [ws[bb][C:] + mmb(a_scr[it[bb]], block_diag(vb[bb])) for bb in batch]
        kv = [lax.dot_general(kd_scr[it[bb]], vb[bb], (((0,), (0,)), ((), ())), preferred_element_type=F32)
              for bb in batch]
        state = [state[bb] * dl_scr[it[bb]] + jnp.where(same_head, kv[bb], 0.0) for bb in batch]
        for bb in batch:
            gate = gate_ref[bb, rows, :].astype(F32)
            y = o[bb] * lax.rsqrt(_head_sums(o[bb] * o[bb], ones_bd) * (1.0 / HEAD_DIM) + EPS) * ng_ref[...]
            o_ref[bb, rows, :] = y * (gate * jax.nn.sigmoid(gate))
    for bb in batch:
        s_scr[bb] = state[bb]


def _gdn(big3, conv_w3, gx, ng, ts, nb):
    B, S, _ = big3.shape
    nchunk = ts // CHUNK
    col = lambda c: pl.BlockSpec((nb, ts, GROUP_W), lambda b, i: (b, i, c))
    W = GROUP_W
    items = nb * nchunk
    return pl.pallas_call(
        functools.partial(_gdn_body, nchunk=nchunk, nb=nb), grid=(B // nb, S // ts),
        in_specs=[col(COL_DQ), col(COL_DK), col(COL_DV),
                  pl.BlockSpec((3, CONV_K, W), lambda b, i: (0, 0, 0)),
                  col(COL_DG),
                  pl.BlockSpec((nb, ts, 2 * W), lambda b, i: (b, i, 0)),
                  pl.BlockSpec((1, W), lambda b, i: (0, 0))],
        out_specs=col(0), out_shape=jax.ShapeDtypeStruct((B, S, W), F32),
        scratch_shapes=[pltpu.VMEM((nb, W, W), F32),
                        pltpu.VMEM((items, CHUNK, W), F32),
                        pltpu.VMEM((items, 2 * CHUNK, W), BF16),
                        pltpu.VMEM((items, CHUNK, W), BF16),
                        pltpu.VMEM((items, CHUNK, W), BF16),
                        pltpu.VMEM((items, 1, W), F32),
                        pltpu.VMEM((items, CHUNK, W), F32),
                        pltpu.VMEM((items, CHUNK, W), BF16),
                        pltpu.VMEM((items, 2, W, W), BF16),
                        pltpu.VMEM((nb * 3, SUBLANES, W), F32),
                        pltpu.VMEM((nb * 3, ts, W), F32)],
        compiler_params=_params("parallel", "arbitrary"), name="gdn")(
            big3, big3, big3, conv_w3, big3, gx, ng)


def _pool_body(z_ref, halo_ref, w_ref, sc_ref, og_ref, o_ref, *, ts):
    i = pl.program_id(1)
    hr = 2 * SUBLANES
    z = z_ref[...].astype(F32)
    halo = jnp.where(i == 0, 0.0, halo_ref[...].astype(F32))
    s1 = jnp.concatenate([halo, z], axis=0)
    s2 = s1 + pltpu.roll(s1, 1, 0)
    s4 = s2 + pltpu.roll(s2, 2, 0)
    s8 = s4 + pltpu.roll(s4, 4, 0)
    s16 = s8 + pltpu.roll(s8, 8, 0)
    grp = lax.broadcasted_iota(I32, (ts, GROUP_W), 1) // (GROUP_W // len(POOL_WINDOWS))
    t = lax.broadcasted_iota(I32, (ts, GROUP_W), 0) + i * ts
    total = jnp.where(grp == 0, s2[hr:], jnp.where(grp == 1, s4[hr:], jnp.where(grp == 2, s8[hr:], s16[hr:])))
    win = jnp.where(grp == 0, POOL_WINDOWS[0], jnp.where(grp == 1, POOL_WINDOWS[1],
                    jnp.where(grp == 2, POOL_WINDOWS[2], POOL_WINDOWS[3])))
    pooled = total / jnp.minimum(t + 1, win).astype(F32)
    y = _mm(pooled - z, w_ref[...]) * sc_ref[...]
    o_ref[...] = _rms(y, og_ref[...])


def _pool(big3, wbd, sc, og, ts):
    B, S, _ = big3.shape
    hr = 2 * SUBLANES
    hb = ts // hr
    row = pl.BlockSpec((1, GROUP_W), lambda b, i: (0, 0))
    return pl.pallas_call(
        functools.partial(_pool_body, ts=ts), grid=(B, S // ts),
        in_specs=[pl.BlockSpec((None, ts, GROUP_W), lambda b, i: (b, i, COL_PZ)),
                  pl.BlockSpec((None, hr, GROUP_W), lambda b, i: (b, jnp.maximum(i * hb - 1, 0), COL_PZ)),
                  pl.BlockSpec((GROUP_W, GROUP_W), lambda b, i: (0, 0)), row, row],
        out_specs=pl.BlockSpec((None, ts, GROUP_W), lambda b, i: (b, i, 0)),
        out_shape=jax.ShapeDtypeStruct((B, S, GROUP_W), F32),
        compiler_params=_params("parallel", "parallel"), name="pool")(big3, big3, wbd, sc, og)


def _outproj_body(ya_ref, yb_ref, yc_ref, yd_ref, h_ref, wo_ref, g_ref, rw_ref, rb_ref,
                  hnew_ref, hn_ref, ri_ref, rf_ref, cnt_ref, *, tm, tr):
    y = jnp.concatenate([ya_ref[...], yb_ref[...], yc_ref[...], yd_ref[...]], axis=-1).astype(BF16)
    h_new = h_ref[...] + jnp.dot(y, wo_ref[...], preferred_element_type=F32)
    hnew_ref[...] = h_new
    hn = _rms(h_new, g_ref[...])
    hn_hi, hn_lo = _split(hn)
    hn_ref[...] = hn_hi
    t = jnp.dot(hn_hi, rw_ref[...], preferred_element_type=F32)
    logits = (t[:, :LANES] + t[:, LANES:]
              + jnp.dot(hn_lo, rw_ref[:, :LANES], preferred_element_type=F32)) + rb_ref[...]
    lane = lax.broadcasted_iota(I32, (tm, LANES), 1)
    neg = -jnp.inf
    big_lane = LANES

    def masked_top(vals, mask):
        v = jnp.where(mask, vals, neg)
        mx = jnp.max(v, axis=-1, keepdims=True)
        idx = jnp.min(jnp.where(mask & (v == mx), lane, big_lane), axis=-1, keepdims=True)
        return v, mx, idx

    gmask = lane < N_GROUPS
    gv, gmx, gidx = masked_top(logits, gmask)
    g_top = 1.0 / jnp.sum(jnp.where(gmask, jnp.exp(gv - gmx), 0.0), axis=-1, keepdims=True)
    lo = N_GROUPS + gidx * EXPERTS_PER_GROUP
    emask = (lane >= lo) & (lane < lo + EXPERTS_PER_GROUP)
    ev, emx, eidx1 = masked_top(logits, emask)
    esum = jnp.sum(jnp.where(emask, jnp.exp(ev - emx), 0.0), axis=-1, keepdims=True)
    p1 = 1.0 / esum
    _, emx2, eidx2 = masked_top(logits, emask & (lane != eidx1))
    p2 = jnp.exp(emx2 - emx) / esum
    denom = p1 + p2
    rf_ref[...] = jnp.where(lane == 0, g_top * p1 / denom, jnp.where(lane == 1, g_top * p2 / denom, 0.0))

    expert = [eidx1 - N_GROUPS, eidx2 - N_GROUPS]
    hot = [lane == e for e in expert]
    m = (hot[0] | hot[1]).astype(BF16)
    below = (lax.broadcasted_iota(I32, (tr, tr), 0) > lax.broadcasted_iota(I32, (tr, tr), 1)).astype(BF16)
    before = jnp.concatenate(
        [jnp.dot(below, m[k * tr:(k + 1) * tr, :], preferred_element_type=F32) for k in range(tm // tr)], axis=0)
    rank = [jnp.sum(jnp.where(hot[s], before, 0.0), axis=-1, keepdims=True).astype(I32) for s in range(TOPK_IN)]
    out = jnp.zeros((tm, LANES), I32)
    for s in range(TOPK_IN):
        out = jnp.where(lane == s, expert[s], jnp.where(lane == TOPK_IN + s, rank[s], out))
    ri_ref[...] = out
    for k in range(tm // tr):
        last = (k + 1) * tr - 1
        total = before[last:last + 1, :] + m[last:last + 1, :].astype(F32)
        cnt_ref[k * SUBLANES:(k + 1) * SUBLANES, :] = jnp.broadcast_to(total, (SUBLANES, LANES)).astype(I32)


def _outproj(ya, yb, yc, yd, h2d, wo, g, rw, rb, tm, tr):
    T, D = h2d.shape
    yblk = pl.BlockSpec((tm, GROUP_W), lambda i: (i, 0))
    cnt_rows = tm // tr * SUBLANES
    return pl.pallas_call(
        functools.partial(_outproj_body, tm=tm, tr=tr), grid=(T // tm,),
        in_specs=[yblk, yblk, yblk, yblk,
                  pl.BlockSpec((tm, D), lambda i: (i, 0)),
                  pl.BlockSpec((D, D), lambda i: (0, 0)),
                  pl.BlockSpec((1, D), lambda i: (0, 0)),
                  pl.BlockSpec((D, 2 * LANES), lambda i: (0, 0)),
                  pl.BlockSpec((1, LANES), lambda i: (0, 0))],
        out_specs=[pl.BlockSpec((tm, D), lambda i: (i, 0)),
                   pl.BlockSpec((tm, D), lambda i: (i, 0)),
                   pl.BlockSpec((tm, LANES), lambda i: (i, 0)),
                   pl.BlockSpec((tm, LANES), lambda i: (i, 0)),
                   pl.BlockSpec((cnt_rows, LANES), lambda i: (i, 0))],
        out_shape=[jax.ShapeDtypeStruct((T, D), F32),
                   jax.ShapeDtypeStruct((T, D), BF16),
                   jax.ShapeDtypeStruct((T, LANES), I32),
                   jax.ShapeDtypeStruct((T, LANES), F32),
                   jax.ShapeDtypeStruct((T // tr * SUBLANES, LANES), I32)],
        compiler_params=_params("parallel"), name="outproj")(ya, yb, yc, yd, h2d, wo, g, rw, rb)


def _dispatch_plan(expert, lrank, cnt_tile, tm):
    T = expert.shape[0]
    nt = T // tm
    counts = jnp.sum(cnt_tile, axis=0)
    padded = (counts + RUN_ROWS + MOE_BLOCK - 1) // MOE_BLOCK * MOE_BLOCK
    pad_end = jnp.cumsum(padded)
    pad_start = pad_end - padded
    gstart = pad_start[None, :] + jnp.cumsum(cnt_tile, axis=0) - cnt_tile
    nchunk = (cnt_tile + RUN_ROWS - 1) // RUN_ROWS
    chunk_end = jnp.cumsum(nchunk, axis=1)
    lstart = (chunk_end - nchunk) * RUN_ROWS
    onehot = expert[:, :, None] == jnp.arange(N_EXPERTS, dtype=I32)[None, None, :]
    pick = lambda tab: jnp.sum(jnp.where(onehot, jnp.repeat(tab, tm, axis=0)[:, None, :], 0), axis=-1)
    lpos =(lrank + pick(lstart)).astype(I32)
    max_chunks = tm * TOPK_IN // RUN_ROWS + N_EXPERTS
    c = jnp.arange(max_chunks, dtype=I32)
    ce = jnp.minimum(jnp.sum(chunk_end[:, None, :] <= c[None, :, None], axis=-1), N_EXPERTS - 1)
    ce_hot = ce[:, :, None] == jnp.arange(N_EXPERTS, dtype=I32)[None, None, :]
    take = lambda tab: jnp.sum(jnp.where(ce_hot, tab[:, None, :], 0), axis=-1)
    chunk_row = take(gstart) + (c[None, :] - take(chunk_end - nchunk)) * RUN_ROWS
    chunk_row = jnp.where(c[None, :] < chunk_end[:, -1:], chunk_row, 0)
    table = jnp.concatenate([chunk_row.astype(I32), jnp.zeros((nt, LANES - 1 - max_chunks), I32),
                             chunk_end[:, -1:].astype(I32)], axis=1).reshape(nt, 1, LANES)
    n_blk = -(-(T * TOPK_IN + N_EXPERTS * RUN_ROWS) // MOE_BLOCK) + N_EXPERTS + 1
    blk_start = jnp.arange(n_blk, dtype=I32) * MOE_BLOCK
    blk_e = jnp.minimum(jnp.sum(pad_end[None, :] <= blk_start[:, None], axis=-1), N_EXPERTS - 1).astype(I32)
    n_used = (pad_end[-1] // MOE_BLOCK).astype(I32).reshape(1)
    zero_start = jnp.concatenate([pad_start + counts, pad_end[-1:]])
    zero_end = jnp.concatenate([pad_end, jnp.full((1,), n_blk * MOE_BLOCK, pad_end.dtype)])
    nz = N_EXPERTS + 1
    zfill = jnp.concatenate([zero_start, (zero_end - zero_start) // RUN_ROWS, zero_end - RUN_ROWS,
                             jnp.zeros((LANES - 3 * nz,), zero_start.dtype)]).astype(I32).reshape(1, LANES)
    return lpos, table, zfill, blk_e, n_used, n_blk


def _dispatch_body(zf_ref, tab_ref, lpos_ref, hn_ref, x_hbm, xs, zero, sem, zsem, n_prev, *, tm, nseg, nrow):
    i = pl.program_id(0)
    slot = i % 2
    step = RUN_ROWS * nseg

    @pl.when(i == 0)
    def _():
        zero[...] = jnp.zeros_like(zero)

        def zero_copy(r):
            return pltpu.make_async_copy(zero, x_hbm.at[pl.ds(pl.multiple_of(r * nseg, nseg), step), :], zsem)

        nz = N_EXPERTS + 1

        def for_chunks(fn):
            for e in range(nz):
                def body(k, carry, e=e):
                    fn(zero_copy(zf_ref[0, e] + k * RUN_ROWS))
                    return carry
                lax.fori_loop(0, zf_ref[0, nz + e], body, 0)

        for_chunks(lambda c: c.start())
        for_chunks(lambda c: c.wait())
        for e in range(nz):
            zero_copy(zf_ref[0, 2 * nz + e]).start()
        for e in range(nz):
            zero_copy(zf_ref[0, 2 * nz + e]).wait()

    row = lax.broadcasted_iota(I32, (nrow, tm), 0)
    sel = (row == lpos_ref[0:1, :]) | (row == lpos_ref[1:2, :])
    rows = jnp.dot(sel.astype(BF16), hn_ref[...], preferred_element_type=F32)
    for k in range(nseg):
        xs[slot, pl.ds(k, nrow, stride=nseg), :] = rows[:, k * LANES:(k + 1) * LANES]

    def chunk_copy(sl, c):
        return pltpu.make_async_copy(
            xs.at[sl, pl.ds(pl.multiple_of(c * step, step), step), :],
            x_hbm.at[pl.ds(pl.multiple_of(tab_ref[0, 0, c] * nseg, nseg), step), :], sem.at[sl])

    def wait_chunks(sl, n):
        def wait(c, carry):
            chunk_copy(sl, 0).wait()
            return carry
        lax.fori_loop(0, n, wait, 0)

    n_chunks = tab_ref[0, 0, LANES - 1]

    @pl.when(i > 0)
    def _():
        wait_chunks(1 - slot, n_prev[0])

    def start(c, carry):
        chunk_copy(slot, c).start()
        return carry

    lax.fori_loop(0, n_chunks, start, 0)
    n_prev[0] = n_chunks

    @pl.when(i == pl.num_programs(0) - 1)
    def _():
        wait_chunks(slot, n_chunks)


def _dispatch(zfill, table, lpos_t, hn, n_pad, tm):
    T, D = hn.shape
    nseg = D // LANES
    nrow = tm * TOPK_IN + N_EXPERTS * RUN_ROWS
    return pl.pallas_call(
        functools.partial(_dispatch_body, tm=tm, nseg=nseg, nrow=nrow), grid=(T // tm,),
        in_specs=[pl.BlockSpec(memory_space=pltpu.SMEM),
                  pl.BlockSpec((1, 1, LANES), lambda i: (i, 0, 0), memory_space=pltpu.SMEM),
                  pl.BlockSpec((None, TOPK_IN, tm), lambda i: (i, 0, 0)),
                  pl.BlockSpec((tm, D), lambda i: (i, 0))],
        out_specs=pl.BlockSpec(memory_space=pl.ANY),
        out_shape=jax.ShapeDtypeStruct((n_pad * nseg, LANES), F32),
        scratch_shapes=[pltpu.VMEM((2, nrow * nseg, LANES), F32), pltpu.VMEM((RUN_ROWS * nseg, LANES), F32),
                        pltpu.SemaphoreType.DMA((2,)), pltpu.SemaphoreType.DMA, pltpu.SMEM((1,), I32)],
        compiler_params=_params("arbitrary"), name="dispatch")(zfill, table, lpos_t, hn)


def _moe_body(blk_e_ref, n_used_ref, seg_ref, nxt_ref, x_ref, w1_hbm, w3_hbm, w2_hbm, y_ref,
              wf1, wf3, wf2, w1b, w3b, w2b, sem, *, nseg, layer):
    b = pl.program_id(0)
    R = MOE_BLOCK

    def fetch(e, slot):
        return [pltpu.make_async_copy(src.at[layer, e], dst.at[slot], sem.at[slot, k])
                for k, (src, dst) in enumerate(((w1_hbm, wf1), (w3_hbm, wf3), (w2_hbm, wf2)))]

    @pl.when(b < n_used_ref[0])
    def _():
        @pl.when((b == 0) | (blk_e_ref[b] != blk_e_ref[jnp.maximum(b - 1, 0)]))
        def _():
            e = blk_e_ref[b]
            slot = seg_ref[b] % 2

            @pl.when(b == 0)
            def _():
                for c in fetch(e, slot):
                    c.start()

            for c in fetch(e, slot):
                c.wait()
            w1b[...] = wf1[slot].astype(BF16)
            w3b[...] = wf3[slot].astype(BF16)
            w2b[...] = wf2[slot].astype(BF16)

            @pl.when(nxt_ref[b] >= 0)
            def _():
                for c in fetch(nxt_ref[b], 1 - slot):
                    c.start()

        x = jnp.concatenate([x_ref[pl.ds(s, R, stride=nseg), :] for s in range(nseg)], axis=-1).astype(BF16)
        a = jnp.dot(x, w1b[...], preferred_element_type=F32)
        g = jnp.dot(x, w3b[...], preferred_element_type=F32)
        hid = (a * jax.nn.sigmoid(a) * g).astype(BF16)
        y = jnp.dot(hid, w2b[...], preferred_element_type=F32)
        for s in range(nseg):
            y_ref[pl.ds(s, R, stride=nseg), :] = y[:, s * LANES:(s + 1) * LANES]

    @pl.when(b >= n_used_ref[0])
    def _():
        y_ref[...] = jnp.zeros_like(y_ref)


def _moe(blk_e, n_used, x_rows, w1, w3, w2, layer):
    n_blk = blk_e.shape[0]
    _, _, D, DE = w1.shape
    nseg = D // LANES
    R = MOE_BLOCK
    blk = jnp.arange(n_blk, dtype=I32)
    used = blk < n_used[0]
    change = (blk > 0) & (blk_e != jnp.roll(blk_e, 1)) & used
    seg = jnp.cumsum(change.astype(I32)).astype(I32)
    later = (blk[None, :] > blk[:, None]) & (seg[None, :] > seg[:, None]) & used[None, :]
    nxt_blk = jnp.min(jnp.where(later, blk[None, :], n_blk), axis=1)
    nxt = jnp.where(nxt_blk < n_blk, blk_e[jnp.minimum(nxt_blk, n_blk - 1)], -1).astype(I32)
    rows = lambda b, be, nu, sg, nx: (jnp.minimum(b, nu[0] - 1), 0)
    grid_spec = pltpu.PrefetchScalarGridSpec(
        num_scalar_prefetch=4, grid=(n_blk,),
        in_specs=[pl.BlockSpec((R * nseg, LANES), rows),
                  pl.BlockSpec(memory_space=pl.ANY), pl.BlockSpec(memory_space=pl.ANY),
                  pl.BlockSpec(memory_space=pl.ANY)],
        out_specs=pl.BlockSpec((R * nseg, LANES), lambda b, be, nu, sg, nx: (b, 0)),
        scratch_shapes=[pltpu.VMEM((2, D, DE), F32), pltpu.VMEM((2, D, DE), F32), pltpu.VMEM((2, DE, D), F32),
                        pltpu.VMEM((D, DE), BF16), pltpu.VMEM((D, DE), BF16), pltpu.VMEM((DE, D), BF16),
                        pltpu.SemaphoreType.DMA((2, 3))])
    return pl.pallas_call(
        functools.partial(_moe_body, nseg=nseg, layer=layer), grid_spec=grid_spec,
        out_shape=jax.ShapeDtypeStruct(x_rows.shape, F32),
        compiler_params=_params("arbitrary"), name="moe")(blk_e, n_used, seg, nxt, x_rows, w1, w3, w2)


def _combine_body(tab_ref, tabn_ref, h_ref, rf_ref, lpos_ref, y_hbm, o_ref, ybuf, sem, *, tm, nseg, nrow):
    i = pl.program_id(0)
    slot = i % 2
    step = RUN_ROWS * nseg

    def start_chunks(tab, sl):
        def start(c, carry):
            pltpu.make_async_copy(
                y_hbm.at[pl.ds(pl.multiple_of(tab[0, 0, c] * nseg, nseg), step), :],
                ybuf.at[sl, pl.ds(pl.multiple_of(c * step, step), step), :], sem.at[sl]).start()
            return carry
        lax.fori_loop(0, tab[0, 0, LANES - 1], start, 0)

    @pl.when(i == 0)
    def _():
        ybuf[...] = jnp.zeros_like(ybuf)
        start_chunks(tab_ref, 0)

    @pl.when(i + 1 < pl.num_programs(0))
    def _():
        start_chunks(tabn_ref, 1 - slot)

    col = lax.broadcasted_iota(I32, (tm, nrow), 1)
    sel = jnp.zeros((tm, nrow), F32)
    for s in range(TOPK_IN):
        sel = sel + jnp.where(col == lpos_ref[:, s:s + 1], rf_ref[:, s:s + 1], 0.0)

    def wait(c, carry):
        pltpu.make_async_copy(y_hbm.at[pl.ds(0, step), :], ybuf.at[slot, pl.ds(0, step), :],
                              sem.at[slot]).wait()
        return carry

    lax.fori_loop(0, tab_ref[0, 0, LANES - 1], wait, 0)
    y = jnp.concatenate([ybuf[slot, pl.ds(k, nrow, stride=nseg), :] for k in range(nseg)], axis=-1)
    o_ref[...] = h_ref[...] + jnp.dot(sel.astype(BF16), y.astype(BF16), preferred_element_type=F32)


def _combine(table, h2d, rf, lpos, y_rows, tm):
    T, D = h2d.shape
    nseg = D // LANES
    nt = T // tm
    nrow = tm * TOPK_IN + N_EXPERTS * RUN_ROWS
    return pl.pallas_call(
        functools.partial(_combine_body, tm=tm, nseg=nseg, nrow=nrow), grid=(nt,),
        in_specs=[pl.BlockSpec((1, 1, LANES), lambda i: (i, 0, 0), memory_space=pltpu.SMEM),
                  pl.BlockSpec((1, 1, LANES), lambda i: (jnp.minimum(i + 1, nt - 1), 0, 0),
                               memory_space=pltpu.SMEM),
                  pl.BlockSpec((tm, D), lambda i: (i, 0)),
                  pl.BlockSpec((tm, LANES), lambda i: (i, 0)),
                  pl.BlockSpec((tm, TOPK_IN), lambda i: (i, 0)),
                  pl.BlockSpec(memory_space=pl.ANY)],
        out_specs=pl.BlockSpec((tm, D), lambda i: (i, 0)),
        out_shape=jax.ShapeDtypeStruct((T, D), F32),
        scratch_shapes=[pltpu.VMEM((2, nrow * nseg, LANES), F32), pltpu.SemaphoreType.DMA((2,))],
        compiler_params=_params("arbitrary"), name="combine")(table, table, h2d, rf, lpos, y_rows)


def _pad_lanes(a, lane0, rows=1):
    a = a.reshape(rows, -1)
    return jnp.pad(a, ((0, 0), (lane0, LANES - lane0 - a.shape[-1])))


def _layer(h2d, B, S, p):
    T, D = h2d.shape
    tile = lambda a, n: jnp.tile(a.reshape(1, -1), (1, n))

    offs = np.cumsum([0, GROUP_W, GROUP_W, GROUP_W, N_HEADS, GROUP_W, GROUP_W,
                      GROUP_W, GROUP_W, GROUP_W, N_HEADS, N_HEADS, GROUP_W, GROUP_W])
    seg = lambda k: p['w_in'][:, offs[k]:offs[k + 1]]
    w_all = jnp.concatenate([seg(0), seg(1), seg(2), seg(4), seg(5), seg(6), seg(7), seg(8), seg(11), seg(12),
                             seg(3), seg(9), seg(10), jnp.zeros((D, LANES - 3 * N_HEADS), F32)],
                            axis=1).astype(BF16)
    gate_prm = jnp.concatenate([_pad_lanes(p['fox_f_bias'], LANE_FOX), _pad_lanes(p['gdn_dt_bias'], LANE_DECAY),
                                _pad_lanes(p['gdn_a_log'], LANE_DECAY), jnp.zeros((SUBLANES - 3, LANES), F32)], axis=0)

    big, small = _inproj(h2d, p['attn_norm_g'].reshape(1, D), w_all, tm=min(512, T))
    big3 = big.reshape(B, S, N_BIG_COLS * GROUP_W)
    ts = min(512, S)
    gcol, gx = _gates(small.reshape(B, S, LANES), gate_prm, ts)

    tq = min(256, S)
    ya = _fox(big3, gcol, tile(p['fox_qn_g'], N_HEADS), tile(p['fox_kn_g'], N_HEADS),
              p['fox_out_g'].reshape(1, GROUP_W), tq)

    bst = jnp.repeat(p['gmlp_bs'].T, HEAD_DIM, axis=1)
    yb = _gmlp(big, p['gmlp_ln_g'].reshape(1, -1), p['gmlp_ln_b'].reshape(1, -1), p['gmlp_ws'], bst,
               p['gmlp_out_g'].reshape(1, -1), tm=min(512, T))

    conv_w3 = p['gdn_conv_w'].reshape(CONV_K, 3, GROUP_W).transpose(1, 0, 2)
    nb = 4 if B % 4 == 0 else (2 if B % 2 == 0 else 1)
    yc = _gdn(big3, conv_w3, gx, tile(p['gdn_norm_g'], N_HEADS), min(1024 // nb, S), nb)

    wbd = jax.scipy.linalg.block_diag(*[p['pool_w'][g] for g in range(len(POOL_WINDOWS))]).astype(BF16)
    yd = _pool(big3, wbd, p['pool_scale'].reshape(1, -1), p['pool_out_g'].reshape(1, -1), ts)

    rw = _pad_lanes(jnp.concatenate([p['router_g_w'], p['router_e_w']], axis=1), 0, rows=D)
    rw = jnp.concatenate(_split(rw), axis=1)
    rb = _pad_lanes(jnp.concatenate([p['router_g_b'], p['router_e_b']]), 0)
    flat = lambda a: a.reshape(T, GROUP_W)
    tmd = min(256, T)
    h_new, hn_rows, ri, rf, cnt = _outproj(flat(ya), yb, flat(yc), flat(yd), h2d, p['w_out'].astype(BF16),
                                           p['ffn_norm_g'].reshape(1, D), rw, rb, tm=min(512, T), tr=tmd)
    cnt_tile = cnt.reshape(T // tmd, SUBLANES, LANES)[:, 0, :N_EXPERTS]
    lpos, table, zfill, blk_e, n_used, n_blk = _dispatch_plan(
        ri[:, :TOPK_IN], ri[:, TOPK_IN:2 * TOPK_IN], cnt_tile, tmd)
    lpos_t = lpos.reshape(T // tmd, tmd, TOPK_IN).transpose(0, 2, 1)
    x_rows = _dispatch(zfill, table, lpos_t, hn_rows, n_blk * MOE_BLOCK, tmd)
    y_rows = _moe(blk_e, n_used, x_rows, p['moe_w1'], p['moe_w3'], p['moe_w2'], p['layer'])
    return _combine(table, h_new, rf, lpos, y_rows, tmd)


def kernel(x, attn_norm_g, w_in, w_out, fox_f_bias, fox_qn_g, fox_kn_g, fox_out_g, gmlp_ln_g, gmlp_ln_b, gmlp_ws, gmlp_bs, gmlp_out_g, gdn_conv_w, gdn_a_log, gdn_dt_bias, gdn_norm_g, pool_w, pool_scale, pool_out_g, ffn_norm_g, router_g_w, router_g_b, router_e_w, router_e_b, moe_w1, moe_w3, moe_w2):
    B, S, D = x.shape
    names = ('attn_norm_g', 'w_in', 'w_out', 'fox_f_bias', 'fox_qn_g', 'fox_kn_g', 'fox_out_g', 'gmlp_ln_g',
             'gmlp_ln_b', 'gmlp_ws', 'gmlp_bs', 'gmlp_out_g', 'gdn_conv_w', 'gdn_a_log', 'gdn_dt_bias',
             'gdn_norm_g', 'pool_w', 'pool_scale', 'pool_out_g', 'ffn_norm_g', 'router_g_w', 'router_g_b',
             'router_e_w', 'router_e_b', 'moe_w1', 'moe_w3', 'moe_w2')
    vals = (attn_norm_g, w_in, w_out, fox_f_bias, fox_qn_g, fox_kn_g, fox_out_g, gmlp_ln_g, gmlp_ln_b, gmlp_ws,
            gmlp_bs, gmlp_out_g, gdn_conv_w, gdn_a_log, gdn_dt_bias, gdn_norm_g, pool_w, pool_scale, pool_out_g,
            ffn_norm_g, router_g_w, router_g_b, router_e_w, router_e_b, moe_w1, moe_w3, moe_w2)
    h = x.reshape(B * S, D)
    stacked = ('moe_w1', 'moe_w3', 'moe_w2')
    for l in range(w_in.shape[0]):
        p = {n: (v if n in stacked else v[l]) for n, v in zip(names, vals)}
        p['layer'] = l
        h = _layer(h, B, S, p)
    return h.reshape(B, S, D)
```

```python
import functools

import jax
import jax.numpy as jnp
import numpy as np
from jax import lax
from jax.experimental import pallas as pl
from jax.experimental.pallas import tpu as pltpu

F32 = jnp.float32
BF16 = jnp.bfloat16
I32 = jnp.int32

EPS = 1e-6
HEAD_DIM = 64
GROUP_W = 256
N_HEADS = GROUP_W // HEAD_DIM
CHUNK = 64
GMLP_LEN = 128
CONV_K = 4
POOL_WINDOWS = (2, 4, 8, 16)
N_GROUPS = 4
EXPERTS_PER_GROUP = 8
N_EXPERTS = N_GROUPS * EXPERTS_PER_GROUP
TOPK_IN = 2
MOE_BLOCK = 256
FOX_VT_ROWS = HEAD_DIM + 16
FOX_BOUND_LOG2 = 40.0
FOX_SKIP_LOG2 = 160.0
RUN_ROWS = 16
LANES = 128
SUBLANES = 8
VMEM_LIMIT = 56 * 1024 * 1024

COL_FQ, COL_FK, COL_FV, COL_GU, COL_GV, COL_DQ, COL_DK, COL_DV, COL_DG, COL_PZ = range(10)
N_BIG_COLS = 10
LANE_FOX, LANE_DECAY, LANE_BETA = 0, 4, 8


def _params(*sem):
    return pltpu.CompilerParams(dimension_semantics=sem, vmem_limit_bytes=VMEM_LIMIT)


def _head_ones():
    r = lax.broadcasted_iota(I32, (GROUP_W, GROUP_W), 0) // HEAD_DIM
    c = lax.broadcasted_iota(I32, (GROUP_W, GROUP_W), 1) // HEAD_DIM
    return (r == c).astype(BF16)


def _head_sums(x, ones_bd):
    hi = x.astype(BF16)
    lo = (x - hi.astype(F32)).astype(BF16)
    return (jnp.dot(hi, ones_bd, preferred_element_type=F32)
            + jnp.dot(lo, ones_bd, preferred_element_type=F32))


def _rms(x, g):
    return x * lax.rsqrt(jnp.mean(x * x, axis=-1, keepdims=True) + EPS) * g


def _mm(a, b):
    return jnp.dot(a.astype(BF16), b.astype(BF16), preferred_element_type=F32)


def _mm_nt(a, b):
    return lax.dot_general(a.astype(BF16), b.astype(BF16), (((1,), (1,)), ((), ())),
                           preferred_element_type=F32)


def _mm_tn(a, b):
    return lax.dot_general(a.astype(BF16), b.astype(BF16), (((0,), (0,)), ((), ())),
                           preferred_element_type=F32)


def _split(a):
    hi = a.astype(BF16)
    return hi, (a - hi.astype(F32)).astype(BF16)


def _mm3(a, b):
    ah, al = _split(a)
    bh, bl = _split(b)
    d = functools.partial(jnp.dot, preferred_element_type=F32)
    return d(ah, bh) + (d(ah, bl) + d(al, bh))


def _inproj_body(x_ref, g_ref, w_ref, big_ref, small_ref):
    xn = _rms(x_ref[...], g_ref[...]).astype(BF16)
    nb = big_ref.shape[1]
    big_ref[...] = jnp.dot(xn, w_ref[:, :nb], preferred_element_type=F32).astype(big_ref.dtype)
    small_ref[...] = jnp.dot(xn, w_ref[:, nb:], preferred_element_type=F32)


def _inproj(x2d, g, w, tm):
    T, D = x2d.shape
    nb = w.shape[1] - LANES
    return pl.pallas_call(
        _inproj_body, grid=(T // tm,),
        in_specs=[pl.BlockSpec((tm, D), lambda i: (i, 0)),
                  pl.BlockSpec((1, D), lambda i: (0, 0)),
                  pl.BlockSpec((D, nb + LANES), lambda i: (0, 0))],
        out_specs=[pl.BlockSpec((tm, nb), lambda i: (i, 0)),
                   pl.BlockSpec((tm, LANES), lambda i: (i, 0))],
        out_shape=[jax.ShapeDtypeStruct((T, nb), BF16), jax.ShapeDtypeStruct((T, LANES), F32)],
        compiler_params=_params("parallel"), name="inproj")(x2d, g, w)


def _gates_body(sm_ref, p_ref, col_ref, exp_ref, carry_ref, *, ts):
    @pl.when(pl.program_id(1) == 0)
    def _():
        carry_ref[...] = jnp.zeros_like(carry_ref)

    x = sm_ref[...]
    lane = lax.broadcasted_iota(I32, (ts, LANES), 1)
    is_fox = lane < LANE_DECAY
    is_dec = (lane >= LANE_DECAY) & (lane < LANE_BETA)
    is_beta = (lane >= LANE_BETA) & (lane < LANE_BETA + N_HEADS)
    logf = jax.nn.log_sigmoid(x + p_ref[0:1, :])
    g = -jnp.exp(p_ref[2:3, :]) * jax.nn.softplus(x + p_ref[1:2, :])
    beta = jax.nn.sigmoid(x)
    tri = (lax.broadcasted_iota(I32, (CHUNK, CHUNK), 0) >= lax.broadcasted_iota(I32, (CHUNK, CHUNK), 1)).astype(BF16)
    vals = jnp.where(is_fox, logf, jnp.where(is_dec, g, 0.0))
    hi = vals.astype(BF16)
    mid = (vals - hi.astype(F32)).astype(BF16)
    lo = (vals - hi.astype(F32) - mid.astype(F32)).astype(BF16)
    parts = jnp.concatenate([hi, mid, lo], axis=1)
    running = carry_ref[...]
    cf_blocks, cg_blocks = [], []
    for n in range(ts // CHUNK):
        t = jnp.dot(tri, parts[n * CHUNK:(n + 1) * CHUNK, :], preferred_element_type=F32)
        within = t[:, :LANES] + (t[:, LANES:2 * LANES] + t[:, 2 * LANES:])
        cg_blocks.append(within)
        cf_blocks.append(within + running)
        running = cf_blocks[-1][CHUNK - 1:CHUNK, :]
    cf = jnp.concatenate(cf_blocks, axis=0)
    cg = jnp.concatenate(cg_blocks, axis=0)
    carry_ref[...] = running
    out = jnp.where(is_fox, cf, jnp.where(is_dec, cg, jnp.where(is_beta, beta, 0.0)))
    col_ref[...] = out
    o_hi = out.astype(BF16)
    o_mid = (out - o_hi.astype(F32)).astype(BF16)
    o_lo = (out - o_hi.astype(F32) - o_mid.astype(F32)).astype(BF16)
    src = lax.broadcasted_iota(I32, (3 * LANES, 2 * GROUP_W), 0) % LANES
    dst = lax.broadcasted_iota(I32, (3 * LANES, 2 * GROUP_W), 1)
    want = jnp.where(dst < GROUP_W, LANE_DECAY, LANE_BETA) + (dst % GROUP_W) // HEAD_DIM
    exp_ref[...] = jnp.dot(jnp.concatenate([o_hi, o_mid, o_lo], axis=1), (src == want).astype(BF16),
                           preferred_element_type=F32)


def _gates(small3, prm, ts):
    B, S, _ = small3.shape
    return pl.pallas_call(
        functools.partial(_gates_body, ts=ts), grid=(B, S // ts),
        in_specs=[pl.BlockSpec((None, ts, LANES), lambda b, j: (b, j, 0)),
                  pl.BlockSpec((SUBLANES, LANES), lambda b, j: (0, 0))],
        out_specs=[pl.BlockSpec((None, ts, LANES), lambda b, j: (b, j, 0)),
                   pl.BlockSpec((None, ts, 2 * GROUP_W), lambda b, j: (b, j, 0))],
        out_shape=[jax.ShapeDtypeStruct((B, S, LANES), F32),
                   jax.ShapeDtypeStruct((B, S, 2 * GROUP_W), F32)],
        scratch_shapes=[pltpu.VMEM((1, LANES), F32)],
        compiler_params=_params("parallel", "arbitrary"), name="gates")(small3, prm)


def _fox_placer(lane0):
    src = lax.broadcasted_iota(I32, (3 * LANES, N_HEADS * LANES), 0)
    dst = lax.broadcasted_iota(I32, (3 * LANES, N_HEADS * LANES), 1)
    return ((src % LANES == LANE_FOX + dst // LANES) & (dst % LANES == lane0 + src // LANES)).astype(BF16)


def _split3_place(vals, placer):
    hi = vals.astype(BF16)
    mid = (vals - hi.astype(F32)).astype(BF16)
    lo = (vals - hi.astype(F32) - mid.astype(F32)).astype(BF16)
    return jnp.dot(jnp.concatenate([hi, mid, lo], axis=1), placer, preferred_element_type=F32)


def _fox_body(jstart_ref, q_ref, k_ref, v_ref, ccol_ref, qg_ref, kg_ref, og_ref, bound_ref, o_ref,
              kn_scr, vt_scr, q_scr, m_scr, l_scr, acc_scr, s_scr, place_scr, *, tq, nk):
    b = pl.program_id(0)
    i = pl.program_id(1)
    ones_bd = _head_ones()
    lane = lax.broadcasted_iota(I32, (tq, LANES), 1)
    log2e = 1.0 / np.log(2.0)
    c_lane, r_lane = HEAD_DIM, HEAD_DIM + 3

    def head_norm(x, g):
        ss = _head_sums(x * x, ones_bd)
        return x * lax.rsqrt(ss * (1.0 / HEAD_DIM) + EPS) * g

    def head_tile(x, h, extra):
        pair = x[:, (h // 2) * LANES:(h // 2 + 1) * LANES]
        if h % 2:
            pair = pltpu.roll(pair, HEAD_DIM, 1)
        return jnp.where(lane < HEAD_DIM, pair, extra).astype(BF16)

    @pl.when(i == 0)
    def _():
        k_ones = jnp.where((lane >= r_lane) & (lane < r_lane + 3), 1.0, 0.0)
        vt_tail = (lax.broadcasted_iota(I32, (FOX_VT_ROWS - HEAD_DIM, tq), 0) == 0).astype(BF16)
        place_scr[0] = _fox_placer(c_lane)
        place_scr[1] = _fox_placer(r_lane)
        for c in range(nk):
            rows = slice(c * tq, (c + 1) * tq)
            kc = head_norm(k_ref[rows, :].astype(F32), kg_ref[...])
            vt = v_ref[rows, :].astype(F32).T.astype(BF16)
            extras = _split3_place(ccol_ref[rows, :] * (-log2e), place_scr[0])
            for h in range(N_HEADS):
                extra = extras[:, h * LANES:(h + 1) * LANES] + k_ones
                kn_scr[h, c] = head_tile(kc, h, extra)
                vt_scr[h, c] = jnp.concatenate([vt[h * HEAD_DIM:(h + 1) * HEAD_DIM, :], vt_tail], axis=0)

    qn = head_norm(q_ref[...].astype(F32), qg_ref[...]) * (HEAD_DIM ** -0.5 * log2e)
    bound = bound_ref[...]
    bounded = jnp.max(bound) <= FOX_BOUND_LOG2
    c_i = ccol_ref[pl.ds(pl.multiple_of(i * tq, tq), tq), :] * log2e
    q_ones = jnp.where((lane >= c_lane) & (lane < c_lane + 3), 1.0, 0.0)
    neg_r = _split3_place(jnp.where(bounded, c_i - bound, 0.0), place_scr[1])
    for h in range(N_HEADS):
        q_scr[h] = head_tile(qn, h, q_ones + neg_r[:, h * LANES:(h + 1) * LANES])
    acc_scr[...] = jnp.zeros_like(acc_scr)
    causal = (lax.broadcasted_iota(I32, (tq, tq), 0) <= lax.broadcasted_iota(I32, (tq, tq), 1))

    heads = range(N_HEADS)

    def scores(j):
        return [lax.dot_general(kn_scr[h, j], q_scr[h], (((1,), (1,)), ((), ())),
                                preferred_element_type=F32) for h in heads]

    def stash(s):
        for h in heads:
            s_scr[h] = s[h]

    def absorb_general(j, masked):
        p, alpha = [], []
        for h in heads:
            s = s_scr[h]
            if masked:
                s = jnp.where(causal, s, -jnp.inf)
            m_old = m_scr[h]
            m_new = jnp.maximum(m_old, jnp.max(s, axis=0, keepdims=True))
            alpha.append(jnp.exp2(m_old - m_new))
            ph = jnp.exp2(s - m_new)
            l_scr[h] = alpha[h] * l_scr[h] + jnp.sum(ph, axis=0, keepdims=True)
            m_scr[h] = m_new
            p.append(ph.astype(BF16))
        pv = [jnp.dot(vt_scr[h, j], p[h], preferred_element_type=F32) for h in heads]
        for h in heads:
            acc_scr[h] = alpha[h] * acc_scr[h] + pv[h]

    def absorb_bounded(j, masked):
        p = []
        for h in heads:
            s = s_scr[h]
            if masked:
                s = jnp.where(causal, s, -jnp.inf)
            p.append(jnp.exp2(s).astype(BF16))
        pv = [jnp.dot(vt_scr[h, j], p[h], preferred_element_type=F32) for h in heads]
        for h in heads:
            acc_scr[h] += pv[h]

    def run(absorb, j0):
        stash(scores(j0))

        def body(j, c):
            s_next = scores(j + 1)
            absorb(j, False)
            stash(s_next)
            return c

        lax.fori_loop(j0, i, body, 0)
        absorb(i, True)

    @pl.when(bounded)
    def _():
        run(absorb_bounded, jstart_ref[b * nk + i])

    @pl.when(jnp.logical_not(bounded))
    def _():
        m_scr[...] = jnp.full_like(m_scr, -jnp.inf)
        l_scr[...] = jnp.zeros_like(l_scr)
        run(absorb_general, 0)
        for h in heads:
            acc_scr[h, HEAD_DIM:HEAD_DIM + 1, :] = l_scr[h]

    o_t = jnp.concatenate([acc_scr[h, :HEAD_DIM, :] * (1.0 / acc_scr[h, HEAD_DIM:HEAD_DIM + 1, :]) for h in heads],
                          axis=0)
    o_ref[...] = _rms(o_t.T, og_ref[...])


def _fox_first_block(gcol, tq):
    B, S, _ = gcol.shape
    nk = S // tq
    c2 = gcol[:, :, LANE_FOX:LANE_FOX + N_HEADS] * (1.0 / np.log(2.0))
    first = c2[:, 0::tq, :]
    last = c2[:, tq - 1::tq, :]
    dead = (first[:, :, None, :] - last[:, None, :, :]) < -FOX_SKIP_LOG2
    dead = dead & (jnp.arange(nk)[None, :, None, None] > jnp.arange(nk)[None, None, :, None])
    return jnp.min(jnp.sum(dead, axis=2), axis=-1).astype(I32).reshape(B * nk)


def _fox(big3, gcol, qg, kg, og, tq):
    B, S, _ = big3.shape
    nk = S // tq
    row = pl.BlockSpec((1, GROUP_W), lambda b, i, js: (0, 0))
    grid_spec = pltpu.PrefetchScalarGridSpec(
        num_scalar_prefetch=1, grid=(B, nk),
        in_specs=[pl.BlockSpec((None, tq, GROUP_W), lambda b, i, js: (b, i, COL_FQ)),
                  pl.BlockSpec((None, S, GROUP_W), lambda b, i, js: (b, 0, COL_FK)),
                  pl.BlockSpec((None, S, GROUP_W), lambda b, i, js: (b, 0, COL_FV)),
                  pl.BlockSpec((None, S, LANES), lambda b, i, js: (b, 0, 0)),
                  row, row, row, pl.BlockSpec((1, LANES), lambda b, i, js: (0, 0))],
        out_specs=pl.BlockSpec((None, tq, GROUP_W), lambda b, i, js: (b, i, 0)),
        scratch_shapes=[pltpu.VMEM((N_HEADS, nk, tq, LANES), BF16),
                        pltpu.VMEM((N_HEADS, nk, FOX_VT_ROWS, tq), BF16),
                        pltpu.VMEM((N_HEADS, tq, LANES), BF16),
                        pltpu.VMEM((N_HEADS, 1, tq), F32),
                        pltpu.VMEM((N_HEADS, 1, tq), F32),
                        pltpu.VMEM((N_HEADS, FOX_VT_ROWS, tq), F32),
                        pltpu.VMEM((N_HEADS, tq, tq), F32),
                        pltpu.VMEM((2, 3 * LANES, N_HEADS * LANES), BF16)])
    gmax = lambda g: jnp.max(jnp.abs(g.reshape(N_HEADS, HEAD_DIM)), axis=1)
    bound = _pad_lanes(gmax(qg) * gmax(kg) * (HEAD_DIM * HEAD_DIM ** -0.5 / np.log(2.0) * 1.02), LANE_FOX)
    return pl.pallas_call(
        functools.partial(_fox_body, tq=tq, nk=nk), grid_spec=grid_spec,
        out_shape=jax.ShapeDtypeStruct((B, S, GROUP_W), F32),
        compiler_params=_params("parallel", "arbitrary"), name="fox")(
            _fox_first_block(gcol, tq), big3, big3, big3, gcol, qg, kg, og, bound)


def _gelu(x):
    return 0.5 * x * (1.0 + lax.erf(x * (2.0 ** -0.5)))


def _gmlp_body(u_ref, v_ref, lg_ref, lb_ref, ws_ref, bst_ref, og_ref, o_ref, *, nwin):
    L = GMLP_LEN
    r = lax.broadcasted_iota(I32, (L, L), 0) // CHUNK
    c = lax.broadcasted_iota(I32, (L, L), 1) // CHUNK
    mask = r >= c
    ws = [jnp.where(mask, ws_ref[h], 0.0).astype(BF16) for h in range(N_HEADS)]
    for n in range(nwin):
        u = _gelu(u_ref[n * L:(n + 1) * L, :].astype(F32))
        v = _gelu(v_ref[n * L:(n + 1) * L, :].astype(F32))
        mu = jnp.mean(v, axis=-1, keepdims=True)
        vc = v - mu
        var = jnp.mean(vc * vc, axis=-1, keepdims=True)
        vn = (vc * lax.rsqrt(var + EPS) * lg_ref[...] + lb_ref[...]).astype(BF16)
        mixed = jnp.concatenate(
            [jnp.dot(ws[h], vn[:, h * HEAD_DIM:(h + 1) * HEAD_DIM], preferred_element_type=F32)
             for h in range(N_HEADS)], axis=-1) + bst_ref[...]
        o_ref[n * L:(n + 1) * L, :] = _rms(u * mixed, og_ref[...])


def _gmlp(big, lg, lb, ws, bst, og, tm):
    T = big.shape[0]
    row = pl.BlockSpec((1, GROUP_W), lambda i: (0, 0))
    return pl.pallas_call(
        functools.partial(_gmlp_body, nwin=tm // GMLP_LEN), grid=(T // tm,),
        in_specs=[pl.BlockSpec((tm, GROUP_W), lambda i: (i, COL_GU)),
                  pl.BlockSpec((tm, GROUP_W), lambda i: (i, COL_GV)),
                  row, row,
                  pl.BlockSpec((N_HEADS, GMLP_LEN, GMLP_LEN), lambda i: (0, 0, 0)),
                  pl.BlockSpec((GMLP_LEN, GROUP_W), lambda i: (0, 0)),
                  row],
        out_specs=pl.BlockSpec((tm, GROUP_W), lambda i: (i, 0)),
        out_shape=jax.ShapeDtypeStruct((T, GROUP_W), F32),
        compiler_params=_params("parallel"), name="gmlp")(big, big, lg, lb, ws, bst, og)


def _gdn_body(q_ref, k_ref, v_ref, w_ref, gate_ref, gx_ref, ng_ref, o_ref,
              s_scr, u_scr, wq_scr, a_scr, kd_scr, dl_scr, t_scr, p_scr, rhs_scr, halo_scr, qkv_scr, *, nchunk, nb):
    C = CHUNK
    ts = nchunk * C
    first = pl.program_id(1) == 0

    @pl.when(first)
    def _():
        s_scr[...] = jnp.zeros_like(s_scr)

    ones_bd = _head_ones()
    for bb, a in [(bb, a) for bb in range(nb) for a in range(3)]:
        w = w_ref[a]
        x = (q_ref, k_ref, v_ref)[a][bb].astype(F32)
        xx = jnp.concatenate([jnp.where(first, 0.0, halo_scr[bb * 3 + a]), x], axis=0)
        halo_scr[bb * 3 + a] = x[ts - SUBLANES:, :]
        y = w[CONV_K - 1:CONV_K, :] * x
        for j in range(CONV_K - 1):
            y = y + w[j:j + 1, :] * pltpu.roll(xx, CONV_K - 1 - j, 0)[SUBLANES:, :]
        y = y * jax.nn.sigmoid(y)
        if a < 2:
            y = y * lax.rsqrt(_head_sums(y * y, ones_bd) + EPS)
        qkv_scr[bb * 3 + a] = y * (HEAD_DIM ** -0.5) if a == 0 else y

    W = GROUP_W
    pos = lax.broadcasted_iota(I32, (C, W), 1) % HEAD_DIM
    r = lax.broadcasted_iota(I32, (C, W), 0)
    tri, strict, eye = r >= pos, r > pos, r == pos
    same_head = (lax.broadcasted_iota(I32, (W, W), 0) // HEAD_DIM
                 == lax.broadcasted_iota(I32, (W, W), 1) // HEAD_DIM)
    mmb = functools.partial(jnp.dot, preferred_element_type=F32)

    def block_diag(x):
        return jnp.where(same_head, jnp.concatenate([x.astype(BF16)] * N_HEADS, axis=0), 0.0)

    items = nb * nchunk
    for n in range(items):
        bb = n // nchunk
        rows = slice((n % nchunk) * C, (n % nchunk + 1) * C)
        q, k, v = (qkv_scr[bb * 3 + a, rows, :] for a in range(3))
        gc = gx_ref[bb, rows, :W]
        beta = gx_ref[bb, rows, W:]
        gr = jnp.sum(jnp.where(eye, gc, 0.0), axis=0, keepdims=True)
        decay = jnp.exp(jnp.where(tri, gc - gr, -jnp.inf))
        kb = k * beta
        kk = lax.dot_general(jnp.concatenate([kb, q], axis=0).astype(BF16), block_diag(k),
                             (((1,), (1,)), ((), ())), preferred_element_type=F32)
        x = jnp.where(strict, -(kk[:C] * decay), 0.0)
        t_scr[n] = jnp.where(eye, 1.0, 0.0) + x
        p_scr[n] = x.astype(BF16)
        eg = jnp.exp(gc)
        g_last = gc[C - 1:C, :]
        rhs_scr[n, 0] = block_diag(v * beta)
        rhs_scr[n, 1] = block_diag(kb * eg)
        wq_scr[n, C:, :] = (q * eg).astype(BF16)
        a_scr[n] = jnp.where(tri, kk[C:] * decay, 0.0).astype(BF16)
        kd_scr[n] = (k * jnp.exp(g_last - gc)).astype(BF16)
        dl_scr[n] = jnp.exp(g_last)
    for level in range(1, 6):
        for n in range(items):
            p = p_scr[n]
            p_scr[n] = mmb(p, block_diag(p)).astype(BF16)
        for n in range(items):
            t = t_scr[n]
            t_scr[n] = t + mmb(t.astype(BF16), block_diag(p_scr[n]))
    for n in range(items):
        t = t_scr[n].astype(BF16)
        u_scr[n] = mmb(t, rhs_scr[n, 0])
        wq_scr[n, :C, :] = mmb(t, rhs_scr[n, 1]).astype(BF16)

    ones_bd = _head_ones()
    batch = range(nb)
    state = [s_scr[bb] for bb in batch]
    for c in range(nchunk):
        rows = slice(c * C, (c + 1) * C)
        it = [bb * nchunk + c for bb in batch]
        ws = [mmb(wq_scr[it[bb]], state[bb].astype(BF16)) for bb in batch]
        vb = [(u_scr[it[bb]] - ws[bb][:C]).astype(BF16) for bb in batch]
        o = [ws[bb][C:] + mmb(a_scr[it[bb]], block_diag(vb[bb])) for bb in batch]
        kv = [lax.dot_general(kd_scr[it[bb]], vb[bb], (((0,), (0,)), ((), ())), preferred_element_type=F32)
              for bb in batch]
        state = [state[bb] * dl_scr[it[bb]] + jnp.where(same_head, kv[bb], 0.0) for bb in batch]
        for bb in batch:
            gate = gate_ref[bb, rows, :].astype(F32)
            y = o[bb] * lax.rsqrt(_head_sums(o[bb] * o[bb], ones_bd) * (1.0 / HEAD_DIM) + EPS) * ng_ref[...]
            o_ref[bb, rows, :] = y * (gate * jax.nn.sigmoid(gate))
    for bb in batch:
        s_scr[bb] = state[bb]


def _gdn(big3, conv_w3, gx, ng, ts, nb):
    B, S, _ = big3.shape
    nchunk = ts // CHUNK
    col = lambda c: pl.BlockSpec((nb, ts, GROUP_W), lambda b, i: (b, i, c))
    W = GROUP_W
    items = nb * nchunk
    return pl.pallas_call(
        functools.partial(_gdn_body, nchunk=nchunk, nb=nb), grid=(B // nb, S // ts),
        in_specs=[col(COL_DQ), col(COL_DK), col(COL_DV),
                  pl.BlockSpec((3, CONV_K, W), lambda b, i: (0, 0, 0)),
                  col(COL_DG),
                  pl.BlockSpec((nb, ts, 2 * W), lambda b, i: (b, i, 0)),
                  pl.BlockSpec((1, W), lambda b, i: (0, 0))],
        out_specs=col(0), out_shape=jax.ShapeDtypeStruct((B, S, W), F32),
        scratch_shapes=[pltpu.VMEM((nb, W, W), F32),
                        pltpu.VMEM((items, CHUNK, W), F32),
                        pltpu.VMEM((items, 2 * CHUNK, W), BF16),
                        pltpu.VMEM((items, CHUNK, W), BF16),
                        pltpu.VMEM((items, CHUNK, W), BF16),
                        pltpu.VMEM((items, 1, W), F32),
                        pltpu.VMEM((items, CHUNK, W), F32),
                        pltpu.VMEM((items, CHUNK, W), BF16),
                        pltpu.VMEM((items, 2, W, W), BF16),
                        pltpu.VMEM((nb * 3, SUBLANES, W), F32),
                        pltpu.VMEM((nb * 3, ts, W), F32)],
        compiler_params=_params("parallel", "arbitrary"), name="gdn")(
            big3, big3, big3, conv_w3, big3, gx, ng)


def _pool_body(z_ref, halo_ref, w_ref, sc_ref, og_ref, o_ref, *, ts):
    i = pl.program_id(1)
    hr = 2 * SUBLANES
    z = z_ref[...].astype(F32)
    halo = jnp.where(i == 0, 0.0, halo_ref[...].astype(F32))
    s1 = jnp.concatenate([halo, z], axis=0)
    s2 = s1 + pltpu.roll(s1, 1, 0)
    s4 = s2 + pltpu.roll(s2, 2, 0)
    s8 = s4 + pltpu.roll(s4, 4, 0)
    s16 = s8 + pltpu.roll(s8, 8, 0)
    grp = lax.broadcasted_iota(I32, (ts, GROUP_W), 1) // (GROUP_W // len(POOL_WINDOWS))
    t = lax.broadcasted_iota(I32, (ts, GROUP_W), 0) + i * ts
    total = jnp.where(grp == 0, s2[hr:], jnp.where(grp == 1, s4[hr:], jnp.where(grp == 2, s8[hr:], s16[hr:])))
    win = jnp.where(grp == 0, POOL_WINDOWS[0], jnp.where(grp == 1, POOL_WINDOWS[1],
                    jnp.where(grp == 2, POOL_WINDOWS[2], POOL_WINDOWS[3])))
    pooled = total / jnp.minimum(t + 1, win).astype(F32)
    y = _mm(pooled - z, w_ref[...]) * sc_ref[...]
    o_ref[...] = _rms(y, og_ref[...])


def _pool(big3, wbd, sc, og, ts):
    B, S, _ = big3.shape
    hr = 2 * SUBLANES
    hb = ts // hr
    row = pl.BlockSpec((1, GROUP_W), lambda b, i: (0, 0))
    return pl.pallas_call(
        functools.partial(_pool_body, ts=ts), grid=(B, S // ts),
        in_specs=[pl.BlockSpec((None, ts, GROUP_W), lambda b, i: (b, i, COL_PZ)),
                  pl.BlockSpec((None, hr, GROUP_W), lambda b, i: (b, jnp.maximum(i * hb - 1, 0), COL_PZ)),
                  pl.BlockSpec((GROUP_W, GROUP_W), lambda b, i: (0, 0)), row, row],
        out_specs=pl.BlockSpec((None, ts, GROUP_W), lambda b, i: (b, i, 0)),
        out_shape=jax.ShapeDtypeStruct((B, S, GROUP_W), F32),
        compiler_params=_params("parallel", "parallel"), name="pool")(big3, big3, wbd, sc, og)


def _outproj_body(ya_ref, yb_ref, yc_ref, yd_ref, h_ref, wo_ref, g_ref, rw_ref, rb_ref,
                  hnew_ref, hn_ref, ri_ref, rf_ref, cnt_ref, *, tm, tr):
    y = jnp.concatenate([ya_ref[...], yb_ref[...], yc_ref[...], yd_ref[...]], axis=-1).astype(BF16)
    h_new = h_ref[...] + jnp.dot(y, wo_ref[...], preferred_element_type=F32)
    hnew_ref[...] = h_new
    hn = _rms(h_new, g_ref[...])
    hn_hi, hn_lo = _split(hn)
    hn_ref[...] = hn_hi
    t = jnp.dot(hn_hi, rw_ref[...], preferred_element_type=F32)
    logits = (t[:, :LANES] + t[:, LANES:]
              + jnp.dot(hn_lo, rw_ref[:, :LANES], preferred_element_type=F32)) + rb_ref[...]
    lane = lax.broadcasted_iota(I32, (tm, LANES), 1)
    neg = -jnp.inf
    big_lane = LANES

    def masked_top(vals, mask):
        v = jnp.where(mask, vals, neg)
        mx = jnp.max(v, axis=-1, keepdims=True)
        idx = jnp.min(jnp.where(mask & (v == mx), lane, big_lane), axis=-1, keepdims=True)
        return v, mx, idx

    gmask = lane < N_GROUPS
    gv, gmx, gidx = masked_top(logits, gmask)
    g_top = 1.0 / jnp.sum(jnp.where(gmask, jnp.exp(gv - gmx), 0.0), axis=-1, keepdims=True)
    lo = N_GROUPS + gidx * EXPERTS_PER_GROUP
    emask = (lane >= lo) & (lane < lo + EXPERTS_PER_GROUP)
    ev, emx, eidx1 = masked_top(logits, emask)
    esum = jnp.sum(jnp.where(emask, jnp.exp(ev - emx), 0.0), axis=-1, keepdims=True)
    p1 = 1.0 / esum
    _, emx2, eidx2 = masked_top(logits, emask & (lane != eidx1))
    p2 = jnp.exp(emx2 - emx) / esum
    denom = p1 + p2
    rf_ref[...] = jnp.where(lane == 0, g_top * p1 / denom, jnp.where(lane == 1, g_top * p2 / denom, 0.0))

    expert = [eidx1 - N_GROUPS, eidx2 - N_GROUPS]
    hot = [lane == e for e in expert]
    m = (hot[0] | hot[1]).astype(BF16)
    below = (lax.broadcasted_iota(I32, (tr, tr), 0) > lax.broadcasted_iota(I32, (tr, tr), 1)).astype(BF16)
    before = jnp.concatenate(
        [jnp.dot(below, m[k * tr:(k + 1) * tr, :], preferred_element_type=F32) for k in range(tm // tr)], axis=0)
    rank = [jnp.sum(jnp.where(hot[s], before, 0.0), axis=-1, keepdims=True).astype(I32) for s in range(TOPK_IN)]
    out = jnp.zeros((tm, LANES), I32)
    for s in range(TOPK_IN):
        out = jnp.where(lane == s, expert[s], jnp.where(lane == TOPK_IN + s, rank[s], out))
    ri_ref[...] = out
    for k in range(tm // tr):
        last = (k + 1) * tr - 1
        total = before[last:last + 1, :] + m[last:last + 1, :].astype(F32)
        cnt_ref[k * SUBLANES:(k + 1) * SUBLANES, :] = jnp.broadcast_to(total, (SUBLANES, LANES)).astype(I32)


def _outproj(ya, yb, yc, yd, h2d, wo, g, rw, rb, tm, tr):
    T, D = h2d.shape
    yblk = pl.BlockSpec((tm, GROUP_W), lambda i: (i, 0))
    cnt_rows = tm // tr * SUBLANES
    return pl.pallas_call(
        functools.partial(_outproj_body, tm=tm, tr=tr), grid=(T // tm,),
        in_specs=[yblk, yblk, yblk, yblk,
                  pl.BlockSpec((tm, D), lambda i: (i, 0)),
                  pl.BlockSpec((D, D), lambda i: (0, 0)),
                  pl.BlockSpec((1, D), lambda i: (0, 0)),
                  pl.BlockSpec((D, 2 * LANES), lambda i: (0, 0)),
                  pl.BlockSpec((1, LANES), lambda i: (0, 0))],
        out_specs=[pl.BlockSpec((tm, D), lambda i: (i, 0)),
                   pl.BlockSpec((tm, D), lambda i: (i, 0)),
                   pl.BlockSpec((tm, LANES), lambda i: (i, 0)),
                   pl.BlockSpec((tm, LANES), lambda i: (i, 0)),
                   pl.BlockSpec((cnt_rows, LANES), lambda i: (i, 0))],
        out_shape=[jax.ShapeDtypeStruct((T, D), F32),
                   jax.ShapeDtypeStruct((T, D), BF16),
                   jax.ShapeDtypeStruct((T, LANES), I32),
                   jax.ShapeDtypeStruct((T, LANES), F32),
                   jax.ShapeDtypeStruct((T // tr * SUBLANES, LANES), I32)],
        compiler_params=_params("parallel"), name="outproj")(ya, yb, yc, yd, h2d, wo, g, rw, rb)


def _dispatch_plan(expert, lrank, cnt_tile, tm):
    T = expert.shape[0]
    nt = T // tm
    counts = jnp.sum(cnt_tile, axis=0)
    padded = (counts + RUN_ROWS + MOE_BLOCK - 1) // MOE_BLOCK * MOE_BLOCK
    pad_end = jnp.cumsum(padded)
    pad_start = pad_end - padded
    gstart = pad_start[None, :] + jnp.cumsum(cnt_tile, axis=0) - cnt_tile
    nchunk = (cnt_tile + RUN_ROWS - 1) // RUN_ROWS
    chunk_end = jnp.cumsum(nchunk, axis=1)
    lstart = (chunk_end - nchunk) * RUN_ROWS
    onehot = expert[:, :, None] == jnp.arange(N_EXPERTS, dtype=I32)[None, None, :]
    pick = lambda tab: jnp.sum(jnp.where(onehot, jnp.repeat(tab, tm, axis=0)[:, None, :], 0), axis=-1)
    lpos =(lrank + pick(lstart)).astype(I32)
    max_chunks = tm * TOPK_IN // RUN_ROWS + N_EXPERTS
    c = jnp.arange(max_chunks, dtype=I32)
    ce = jnp.minimum(jnp.sum(chunk_end[:, None, :] <= c[None, :, None], axis=-1), N_EXPERTS - 1)
    ce_hot = ce[:, :, None] == jnp.arange(N_EXPERTS, dtype=I32)[None, None, :]
    take = lambda tab: jnp.sum(jnp.where(ce_hot, tab[:, None, :], 0), axis=-1)
    chunk_row = take(gstart) + (c[None, :] - take(chunk_end - nchunk)) * RUN_ROWS
    chunk_row = jnp.where(c[None, :] < chunk_end[:, -1:], chunk_row, 0)
    table = jnp.concatenate([chunk_row.astype(I32), jnp.zeros((nt, LANES - 1 - max_chunks), I32),
                             chunk_end[:, -1:].astype(I32)], axis=1).reshape(nt, 1, LANES)
    n_blk = -(-(T * TOPK_IN + N_EXPERTS * RUN_ROWS) // MOE_BLOCK) + N_EXPERTS + 1
    blk_start = jnp.arange(n_blk, dtype=I32) * MOE_BLOCK
    blk_e = jnp.minimum(jnp.sum(pad_end[None, :] <= blk_start[:, None], axis=-1), N_EXPERTS - 1).astype(I32)
    n_used = (pad_end[-1] // MOE_BLOCK).astype(I32).reshape(1)
    zero_start = jnp.concatenate([pad_start + counts, pad_end[-1:]])
    zero_end = jnp.concatenate([pad_end, jnp.full((1,), n_blk * MOE_BLOCK, pad_end.dtype)])
    nz = N_EXPERTS + 1
    zfill = jnp.concatenate([zero_start, (zero_end - zero_start) // RUN_ROWS, zero_end - RUN_ROWS,
                             jnp.zeros((LANES - 3 * nz,), zero_start.dtype)]).astype(I32).reshape(1, LANES)
    return lpos, table, zfill, blk_e, n_used, n_blk


def _dispatch_body(zf_ref, tab_ref, lpos_ref, hn_ref, x_hbm, xs, zero, sem, zsem, n_prev, *, tm, nseg, nrow):
    i = pl.program_id(0)
    slot = i % 2
    step = RUN_ROWS * nseg

    @pl.when(i == 0)
    def _():
        zero[...] = jnp.zeros_like(zero)

        def zero_copy(r):
            return pltpu.make_async_copy(zero, x_hbm.at[pl.ds(pl.multiple_of(r * nseg, nseg), step), :], zsem)

        nz = N_EXPERTS + 1

        def for_chunks(fn):
            for e in range(nz):
                def body(k, carry, e=e):
                    fn(zero_copy(zf_ref[0, e] + k * RUN_ROWS))
                    return carry
                lax.fori_loop(0, zf_ref[0, nz + e], body, 0)

        for_chunks(lambda c: c.start())
        for_chunks(lambda c: c.wait())
        for e in range(nz):
            zero_copy(zf_ref[0, 2 * nz + e]).start()
        for e in range(nz):
            zero_copy(zf_ref[0, 2 * nz + e]).wait()

    row = lax.broadcasted_iota(I32, (nrow, tm), 0)
    sel = (row == lpos_ref[0:1, :]) | (row == lpos_ref[1:2, :])
    rows = jnp.dot(sel.astype(BF16), hn_ref[...], preferred_element_type=F32)
    for k in range(nseg):
        xs[slot, pl.ds(k, nrow, stride=nseg), :] = rows[:, k * LANES:(k + 1) * LANES]

    def chunk_copy(sl, c):
        return pltpu.make_async_copy(
            xs.at[sl, pl.ds(pl.multiple_of(c * step, step), step), :],
            x_hbm.at[pl.ds(pl.multiple_of(tab_ref[0, 0, c] * nseg, nseg), step), :], sem.at[sl])

    def wait_chunks(sl, n):
        def wait(c, carry):
            chunk_copy(sl, 0).wait()
            return carry
        lax.fori_loop(0, n, wait, 0)

    n_chunks = tab_ref[0, 0, LANES - 1]

    @pl.when(i > 0)
    def _():
        wait_chunks(1 - slot, n_prev[0])

    def start(c, carry):
        chunk_copy(slot, c).start()
        return carry

    lax.fori_loop(0, n_chunks, start, 0)
    n_prev[0] = n_chunks

    @pl.when(i == pl.num_programs(0) - 1)
    def _():
        wait_chunks(slot, n_chunks)


def _dispatch(zfill, table, lpos_t, hn, n_pad, tm):
    T, D = hn.shape
    nseg = D // LANES
    nrow = tm * TOPK_IN + N_EXPERTS * RUN_ROWS
    return pl.pallas_call(
        functools.partial(_dispatch_body, tm=tm, nseg=nseg, nrow=nrow), grid=(T // tm,),
        in_specs=[pl.BlockSpec(memory_space=pltpu.SMEM),
                  pl.BlockSpec((1, 1, LANES), lambda i: (i, 0, 0), memory_space=pltpu.SMEM),
                  pl.BlockSpec((None, TOPK_IN, tm), lambda i: (i, 0, 0)),
                  pl.BlockSpec((tm, D), lambda i: (i, 0))],
        out_specs=pl.BlockSpec(memory_space=pl.ANY),
        out_shape=jax.ShapeDtypeStruct((n_pad * nseg, LANES), F32),
        scratch_shapes=[pltpu.VMEM((2, nrow * nseg, LANES), F32), pltpu.VMEM((RUN_ROWS * nseg, LANES), F32),
                        pltpu.SemaphoreType.DMA((2,)), pltpu.SemaphoreType.DMA, pltpu.SMEM((1,), I32)],
        compiler_params=_params("arbitrary"), name="dispatch")(zfill, table, lpos_t, hn)


def _moe_body(blk_e_ref, n_used_ref, seg_ref, nxt_ref, x_ref, w1_hbm, w3_hbm, w2_hbm, y_ref,
              wf1, wf3, wf2, w1b, w3b, w2b, sem, *, nseg, layer):
    b = pl.program_id(0)
    R = MOE_BLOCK

    def fetch(e, slot):
        return [pltpu.make_async_copy(src.at[layer, e], dst.at[slot], sem.at[slot, k])
                for k, (src, dst) in enumerate(((w1_hbm, wf1), (w3_hbm, wf3), (w2_hbm, wf2)))]

    @pl.when(b < n_used_ref[0])
    def _():
        @pl.when((b == 0) | (blk_e_ref[b] != blk_e_ref[jnp.maximum(b - 1, 0)]))
        def _():
            e = blk_e_ref[b]
            slot = seg_ref[b] % 2

            @pl.when(b == 0)
            def _():
                for c in fetch(e, slot):
                    c.start()

            for c in fetch(e, slot):
                c.wait()
            w1b[...] = wf1[slot].astype(BF16)
            w3b[...] = wf3[slot].astype(BF16)
            w2b[...] = wf2[slot].astype(BF16)

            @pl.when(nxt_ref[b] >= 0)
            def _():
                for c in fetch(nxt_ref[b], 1 - slot):
                    c.start()

        x = jnp.concatenate([x_ref[pl.ds(s, R, stride=nseg), :] for s in range(nseg)], axis=-1).astype(BF16)
        a = jnp.dot(x, w1b[...], preferred_element_type=F32)
        g = jnp.dot(x, w3b[...], preferred_element_type=F32)
        hid = (a * jax.nn.sigmoid(a) * g).astype(BF16)
        y = jnp.dot(hid, w2b[...], preferred_element_type=F32)
        for s in range(nseg):
            y_ref[pl.ds(s, R, stride=nseg), :] = y[:, s * LANES:(s + 1) * LANES]

    @pl.when(b >= n_used_ref[0])
    def _():
        y_ref[...] = jnp.zeros_like(y_ref)


def _moe(blk_e, n_used, x_rows, w1, w3, w2, layer):
    n_blk = blk_e.shape[0]
    _, _, D, DE = w1.shape
    nseg = D // LANES
    R = MOE_BLOCK
    blk = jnp.arange(n_blk, dtype=I32)
    used = blk < n_used[0]
    change = (blk > 0) & (blk_e != jnp.roll(blk_e, 1)) & used
    seg = jnp.cumsum(change.astype(I32)).astype(I32)
    later = (blk[None, :] > blk[:, None]) & (seg[None, :] > seg[:, None]) & used[None, :]
    nxt_blk = jnp.min(jnp.where(later, blk[None, :], n_blk), axis=1)
    nxt = jnp.where(nxt_blk < n_blk, blk_e[jnp.minimum(nxt_blk, n_blk - 1)], -1).astype(I32)
    rows = lambda b, be, nu, sg, nx: (jnp.minimum(b, nu[0] - 1), 0)
    grid_spec = pltpu.PrefetchScalarGridSpec(
        num_scalar_prefetch=4, grid=(n_blk,),
        in_specs=[pl.BlockSpec((R * nseg, LANES), rows),
                  pl.BlockSpec(memory_space=pl.ANY), pl.BlockSpec(memory_space=pl.ANY),
                  pl.BlockSpec(memory_space=pl.ANY)],
        out_specs=pl.BlockSpec((R * nseg, LANES), lambda b, be, nu, sg, nx: (b, 0)),
        scratch_shapes=[pltpu.VMEM((2, D, DE), F32), pltpu.VMEM((2, D, DE), F32), pltpu.VMEM((2, DE, D), F32),
                        pltpu.VMEM((D, DE), BF16), pltpu.VMEM((D, DE), BF16), pltpu.VMEM((DE, D), BF16),
                        pltpu.SemaphoreType.DMA((2, 3))])
    return pl.pallas_call(
        functools.partial(_moe_body, nseg=nseg, layer=layer), grid_spec=grid_spec,
        out_shape=jax.ShapeDtypeStruct(x_rows.shape, F32),
        compiler_params=_params("arbitrary"), name="moe")(blk_e, n_used, seg, nxt, x_rows, w1, w3, w2)


def _combine_body(tab_ref, tabn_ref, h_ref, rf_ref, lpos_ref, y_hbm, o_ref, ybuf, sem, *, tm, nseg, nrow):
    i = pl.program_id(0)
    slot = i % 2
    step = RUN_ROWS * nseg

    def start_chunks(tab, sl):
        def start(c, carry):
            pltpu.make_async_copy(
                y_hbm.at[pl.ds(pl.multiple_of(tab[0, 0, c] * nseg, nseg), step), :],
                ybuf.at[sl, pl.ds(pl.multiple_of(c * step, step), step), :], sem.at[sl]).start()
            return carry
        lax.fori_loop(0, tab[0, 0, LANES - 1], start, 0)

    @pl.when(i == 0)
    def _():
        ybuf[...] = jnp.zeros_like(ybuf)
        start_chunks(tab_ref, 0)

    @pl.when(i + 1 < pl.num_programs(0))
    def _():
        start_chunks(tabn_ref, 1 - slot)

    col = lax.broadcasted_iota(I32, (tm, nrow), 1)
    sel = jnp.zeros((tm, nrow), F32)
    for s in range(TOPK_IN):
        sel = sel + jnp.where(col == lpos_ref[:, s:s + 1], rf_ref[:, s:s + 1], 0.0)

    def wait(c, carry):
        pltpu.make_async_copy(y_hbm.at[pl.ds(0, step), :], ybuf.at[slot, pl.ds(0, step), :],
                              sem.at[slot]).wait()
        return carry

    lax.fori_loop(0, tab_ref[0, 0, LANES - 1], wait, 0)
    y = jnp.concatenate([ybuf[slot, pl.ds(k, nrow, stride=nseg), :] for k in range(nseg)], axis=-1)
    o_ref[...] = h_ref[...] + jnp.dot(sel.astype(BF16), y.astype(BF16), preferred_element_type=F32)


def _combine(table, h2d, rf, lpos, y_rows, tm):
    T, D = h2d.shape
    nseg = D // LANES
    nt = T // tm
    nrow = tm * TOPK_IN + N_EXPERTS * RUN_ROWS
    return pl.pallas_call(
        functools.partial(_combine_body, tm=tm, nseg=nseg, nrow=nrow), grid=(nt,),
        in_specs=[pl.BlockSpec((1, 1, LANES), lambda i: (i, 0, 0), memory_space=pltpu.SMEM),
                  pl.BlockSpec((1, 1, LANES), lambda i: (jnp.minimum(i + 1, nt - 1), 0, 0),
                               memory_space=pltpu.SMEM),
                  pl.BlockSpec((tm, D), lambda i: (i, 0)),
                  pl.BlockSpec((tm, LANES), lambda i: (i, 0)),
                  pl.BlockSpec((tm, TOPK_IN), lambda i: (i, 0)),
                  pl.BlockSpec(memory_space=pl.ANY)],
        out_specs=pl.BlockSpec((tm, D), lambda i: (i, 0)),
        out_shape=jax.ShapeDtypeStruct((T, D), F32),
        scratch_shapes=[pltpu.VMEM((2, nrow * nseg, LANES), F32), pltpu.SemaphoreType.DMA((2,))],
        compiler_params=_params("arbitrary"), name="combine")(table, table, h2d, rf, lpos, y_rows)


def _pad_lanes(a, lane0, rows=1):
    a = a.reshape(rows, -1)
    return jnp.pad(a, ((0, 0), (lane0, LANES - lane0 - a.shape[-1])))


def _layer(h2d, B, S, p):
    T, D = h2d.shape
    tile = lambda a, n: jnp.tile(a.reshape(1, -1), (1, n))

    offs = np.cumsum([0, GROUP_W, GROUP_W, GROUP_W, N_HEADS, GROUP_W, GROUP_W,
                      GROUP_W, GROUP_W, GROUP_W, N_HEADS, N_HEADS, GROUP_W, GROUP_W])
    seg = lambda k: p['w_in'][:, offs[k]:offs[k + 1]]
    w_all = jnp.concatenate([seg(0), seg(1), seg(2), seg(4), seg(5), seg(6), seg(7), seg(8), seg(11), seg(12),
                             seg(3), seg(9), seg(10), jnp.zeros((D, LANES - 3 * N_HEADS), F32)],
                            axis=1).astype(BF16)
    gate_prm = jnp.concatenate([_pad_lanes(p['fox_f_bias'], LANE_FOX), _pad_lanes(p['gdn_dt_bias'], LANE_DECAY),
                                _pad_lanes(p['gdn_a_log'], LANE_DECAY), jnp.zeros((SUBLANES - 3, LANES), F32)], axis=0)

    big, small = _inproj(h2d, p['attn_norm_g'].reshape(1, D), w_all, tm=min(512, T))
    big3 = big.reshape(B, S, N_BIG_COLS * GROUP_W)
    ts = min(512, S)
    gcol, gx = _gates(small.reshape(B, S, LANES), gate_prm, ts)

    tq = min(256, S)
    ya = _fox(big3, gcol, tile(p['fox_qn_g'], N_HEADS), tile(p['fox_kn_g'], N_HEADS),
              p['fox_out_g'].reshape(1, GROUP_W), tq)

    bst = jnp.repeat(p['gmlp_bs'].T, HEAD_DIM, axis=1)
    yb = _gmlp(big, p['gmlp_ln_g'].reshape(1, -1), p['gmlp_ln_b'].reshape(1, -1), p['gmlp_ws'], bst,
               p['gmlp_out_g'].reshape(1, -1), tm=min(512, T))

    conv_w3 = p['gdn_conv_w'].reshape(CONV_K, 3, GROUP_W).transpose(1, 0, 2)
    nb = 4 if B % 4 == 0 else (2 if B % 2 == 0 else 1)
    yc = _gdn(big3, conv_w3, gx, tile(p['gdn_norm_g'], N_HEADS), min(1024 // nb, S), nb)

    wbd = jax.scipy.linalg.block_diag(*[p['pool_w'][g] for g in range(len(POOL_WINDOWS))]).astype(BF16)
    yd = _pool(big3, wbd, p['pool_scale'].reshape(1, -1), p['pool_out_g'].reshape(1, -1), ts)

    rw = _pad_lanes(jnp.concatenate([p['router_g_w'], p['router_e_w']], axis=1), 0, rows=D)
    rw = jnp.concatenate(_split(rw), axis=1)
    rb = _pad_lanes(jnp.concatenate([p['router_g_b'], p['router_e_b']]), 0)
    flat = lambda a: a.reshape(T, GROUP_W)
    tmd = min(256, T)
    h_new, hn_rows, ri, rf, cnt = _outproj(flat(ya), yb, flat(yc), flat(yd), h2d, p['w_out'].astype(BF16),
                                           p['ffn_norm_g'].reshape(1, D), rw, rb, tm=min(512, T), tr=tmd)
    cnt_tile = cnt.reshape(T // tmd, SUBLANES, LANES)[:, 0, :N_EXPERTS]
    lpos, table, zfill, blk_e, n_used, n_blk = _dispatch_plan(
        ri[:, :TOPK_IN], ri[:, TOPK_IN:2 * TOPK_IN], cnt_tile, tmd)
    lpos_t = lpos.reshape(T // tmd, tmd, TOPK_IN).transpose(0, 2, 1)
    x_rows = _dispatch(zfill, table, lpos_t, hn_rows, n_blk * MOE_BLOCK, tmd)
    y_rows = _moe(blk_e, n_used, x_rows, p['moe_w1'], p['moe_w3'], p['moe_w2'], p['layer'])
    return _combine(table, h_new, rf, lpos, y_rows, tmd)


def kernel(x, attn_norm_g, w_in, w_out, fox_f_bias, fox_qn_g, fox_kn_g, fox_out_g, gmlp_ln_g, gmlp_ln_b, gmlp_ws, gmlp_bs, gmlp_out_g, gdn_conv_w, gdn_a_log, gdn_dt_bias, gdn_norm_g, pool_w, pool_scale, pool_out_g, ffn_norm_g, router_g_w, router_g_b, router_e_w, router_e_b, moe_w1, moe_w3, moe_w2):
    B, S, D = x.shape
    names = ('attn_norm_g', 'w_in', 'w_out', 'fox_f_bias', 'fox_qn_g', 'fox_kn_g', 'fox_out_g', 'gmlp_ln_g',
             'gmlp_ln_b', 'gmlp_ws', 'gmlp_bs', 'gmlp_out_g', 'gdn_conv_w', 'gdn_a_log', 'gdn_dt_bias',
             'gdn_norm_g', 'pool_w', 'pool_scale', 'pool_out_g', 'ffn_norm_g', 'router_g_w', 'router_g_b',
             'router_e_w', 'router_e_b', 'moe_w1', 'moe_w3', 'moe_w2')
    vals = (attn_norm_g, w_in, w_out, fox_f_bias, fox_qn_g, fox_kn_g, fox_out_g, gmlp_ln_g, gmlp_ln_b, gmlp_ws,
            gmlp_bs, gmlp_out_g, gdn_conv_w, gdn_a_log, gdn_dt_bias, gdn_norm_g, pool_w, pool_scale, pool_out_g,
            ffn_norm_g, router_g_w, router_g_b, router_e_w, router_e_b, moe_w1, moe_w3, moe_w2)
    h = x.reshape(B * S, D)
    stacked = ('moe_w1', 'moe_w3', 'moe_w2')
    for l in range(w_in.shape[0]):
        p = {n: (v if n in stacked else v[l]) for n, v in zip(names, vals)}
        p['layer'] = l
        h = _layer(h, B, S, p)
    return h.reshape(B, S, D)
```

```python
import functools

import jax
import jax.numpy as jnp
import numpy as np
from jax import lax
from jax.experimental import pallas as pl
from jax.experimental.pallas import tpu as pltpu

F32 = jnp.float32
BF16 = jnp.bfloat16
I32 = jnp.int32

EPS = 1e-6
HEAD_DIM = 64
GROUP_W = 256
N_HEADS = GROUP_W // HEAD_DIM
CHUNK = 64
GMLP_LEN = 128
CONV_K = 4
POOL_WINDOWS = (2, 4, 8, 16)
N_GROUPS = 4
EXPERTS_PER_GROUP = 8
N_EXPERTS = N_GROUPS * EXPERTS_PER_GROUP
TOPK_IN = 2
MOE_BLOCK = 256
FOX_VT_ROWS = HEAD_DIM + 16
FOX_BOUND_LOG2 = 40.0
FOX_SKIP_LOG2 = 160.0
RUN_ROWS = 16
LANES = 128
SUBLANES = 8
VMEM_LIMIT = 56 * 1024 * 1024

COL_FQ, COL_FK, COL_FV, COL_GU, COL_GV, COL_DQ, COL_DK, COL_DV, COL_DG, COL_PZ = range(10)
N_BIG_COLS = 10
LANE_FOX, LANE_DECAY, LANE_BETA = 0, 4, 8


def _params(*sem):
    return pltpu.CompilerParams(dimension_semantics=sem, vmem_limit_bytes=VMEM_LIMIT)


def _head_ones():
    r = lax.broadcasted_iota(I32, (GROUP_W, GROUP_W), 0) // HEAD_DIM
    c = lax.broadcasted_iota(I32, (GROUP_W, GROUP_W), 1) // HEAD_DIM
    return (r == c).astype(BF16)


def _head_sums(x, ones_bd):
    hi = x.astype(BF16)
    lo = (x - hi.astype(F32)).astype(BF16)
    return (jnp.dot(hi, ones_bd, preferred_element_type=F32)
            + jnp.dot(lo, ones_bd, preferred_element_type=F32))


def _rms(x, g):
    return x * lax.rsqrt(jnp.mean(x * x, axis=-1, keepdims=True) + EPS) * g


def _mm(a, b):
    return jnp.dot(a.astype(BF16), b.astype(BF16), preferred_element_type=F32)


def _mm_nt(a, b):
    return lax.dot_general(a.astype(BF16), b.astype(BF16), (((1,), (1,)), ((), ())),
                           preferred_element_type=F32)


def _mm_tn(a, b):
    return lax.dot_general(a.astype(BF16), b.astype(BF16), (((0,), (0,)), ((), ())),
                           preferred_element_type=F32)


def _split(a):
    hi = a.astype(BF16)
    return hi, (a - hi.astype(F32)).astype(BF16)


def _mm3(a, b):
    ah, al = _split(a)
    bh, bl = _split(b)
    d = functools.partial(jnp.dot, preferred_element_type=F32)
    return d(ah, bh) + (d(ah, bl) + d(al, bh))


def _inproj_body(x_ref, g_ref, w_ref, big_ref, small_ref):
    xn = _rms(x_ref[...], g_ref[...]).astype(BF16)
    nb = big_ref.shape[1]
    big_ref[...] = jnp.dot(xn, w_ref[:, :nb], preferred_element_type=F32).astype(big_ref.dtype)
    small_ref[...] = jnp.dot(xn, w_ref[:, nb:], preferred_element_type=F32)


def _inproj(x2d, g, w, tm):
    T, D = x2d.shape
    nb = w.shape[1] - LANES
    return pl.pallas_call(
        _inproj_body, grid=(T // tm,),
        in_specs=[pl.BlockSpec((tm, D), lambda i: (i, 0)),
                  pl.BlockSpec((1, D), lambda i: (0, 0)),
                  pl.BlockSpec((D, nb + LANES), lambda i: (0, 0))],
        out_specs=[pl.BlockSpec((tm, nb), lambda i: (i, 0)),
                   pl.BlockSpec((tm, LANES), lambda i: (i, 0))],
        out_shape=[jax.ShapeDtypeStruct((T, nb), BF16), jax.ShapeDtypeStruct((T, LANES), F32)],
        compiler_params=_params("parallel"), name="inproj")(x2d, g, w)


def _gates_body(sm_ref, p_ref, col_ref, exp_ref, carry_ref, *, ts):
    @pl.when(pl.program_id(1) == 0)
    def _():
        carry_ref[...] = jnp.zeros_like(carry_ref)

    x = sm_ref[...]
    lane = lax.broadcasted_iota(I32, (ts, LANES), 1)
    is_fox = lane < LANE_DECAY
    is_dec = (lane >= LANE_DECAY) & (lane < LANE_BETA)
    is_beta = (lane >= LANE_BETA) & (lane < LANE_BETA + N_HEADS)
    logf = jax.nn.log_sigmoid(x + p_ref[0:1, :])
    g = -jnp.exp(p_ref[2:3, :]) * jax.nn.softplus(x + p_ref[1:2, :])
    beta = jax.nn.sigmoid(x)
    tri = (lax.broadcasted_iota(I32, (CHUNK, CHUNK), 0) >= lax.broadcasted_iota(I32, (CHUNK, CHUNK), 1)).astype(BF16)
    vals = jnp.where(is_fox, logf, jnp.where(is_dec, g, 0.0))
    hi = vals.astype(BF16)
    mid = (vals - hi.astype(F32)).astype(BF16)
    lo = (vals - hi.astype(F32) - mid.astype(F32)).astype(BF16)
    parts = jnp.concatenate([hi, mid, lo], axis=1)
    running = carry_ref[...]
    cf_blocks, cg_blocks = [], []
    for n in range(ts // CHUNK):
        t = jnp.dot(tri, parts[n * CHUNK:(n + 1) * CHUNK, :], preferred_element_type=F32)
        within = t[:, :LANES] + (t[:, LANES:2 * LANES] + t[:, 2 * LANES:])
        cg_blocks.append(within)
        cf_blocks.append(within + running)
        running = cf_blocks[-1][CHUNK - 1:CHUNK, :]
    cf = jnp.concatenate(cf_blocks, axis=0)
    cg = jnp.concatenate(cg_blocks, axis=0)
    carry_ref[...] = running
    out = jnp.where(is_fox, cf, jnp.where(is_dec, cg, jnp.where(is_beta, beta, 0.0)))
    col_ref[...] = out
    o_hi = out.astype(BF16)
    o_mid = (out - o_hi.astype(F32)).astype(BF16)
    o_lo = (out - o_hi.astype(F32) - o_mid.astype(F32)).astype(BF16)
    src = lax.broadcasted_iota(I32, (3 * LANES, 2 * GROUP_W), 0) % LANES
    dst = lax.broadcasted_iota(I32, (3 * LANES, 2 * GROUP_W), 1)
    want = jnp.where(dst < GROUP_W, LANE_DECAY, LANE_BETA) + (dst % GROUP_W) // HEAD_DIM
    exp_ref[...] = jnp.dot(jnp.concatenate([o_hi, o_mid, o_lo], axis=1), (src == want).astype(BF16),
                           preferred_element_type=F32)


def _gates(small3, prm, ts):
    B, S, _ = small3.shape
    return pl.pallas_call(
        functools.partial(_gates_body, ts=ts), grid=(B, S // ts),
        in_specs=[pl.BlockSpec((None, ts, LANES), lambda b, j: (b, j, 0)),
                  pl.BlockSpec((SUBLANES, LANES), lambda b, j: (0, 0))],
        out_specs=[pl.BlockSpec((None, ts, LANES), lambda b, j: (b, j, 0)),
                   pl.BlockSpec((None, ts, 2 * GROUP_W), lambda b, j: (b, j, 0))],
        out_shape=[jax.ShapeDtypeStruct((B, S, LANES), F32),
                   jax.ShapeDtypeStruct((B, S, 2 * GROUP_W), F32)],
        scratch_shapes=[pltpu.VMEM((1, LANES), F32)],
        compiler_params=_params("parallel", "arbitrary"), name="gates")(small3, prm)


def _fox_placer(lane0):
    src = lax.broadcasted_iota(I32, (3 * LANES, N_HEADS * LANES), 0)
    dst = lax.broadcasted_iota(I32, (3 * LANES, N_HEADS * LANES), 1)
    return ((src % LANES == LANE_FOX + dst // LANES) & (dst % LANES == lane0 + src // LANES)).astype(BF16)


def _split3_place(vals, placer):
    hi = vals.astype(BF16)
    mid = (vals - hi.astype(F32)).astype(BF16)
    lo = (vals - hi.astype(F32) - mid.astype(F32)).astype(BF16)
    return jnp.dot(jnp.concatenate([hi, mid, lo], axis=1), placer, preferred_element_type=F32)


def _fox_body(jstart_ref, q_ref, k_ref, v_ref, ccol_ref, qg_ref, kg_ref, og_ref, bound_ref, o_ref,
              kn_scr, vt_scr, q_scr, m_scr, l_scr, acc_scr, s_scr, place_scr, *, tq, nk):
    b = pl.program_id(0)
    i = pl.program_id(1)
    ones_bd = _head_ones()
    lane = lax.broadcasted_iota(I32, (tq, LANES), 1)
    log2e = 1.0 / np.log(2.0)
    c_lane, r_lane = HEAD_DIM, HEAD_DIM + 3

    def head_norm(x, g):
        ss = _head_sums(x * x, ones_bd)
        return x * lax.rsqrt(ss * (1.0 / HEAD_DIM) + EPS) * g

    def head_tile(x, h, extra):
        pair = x[:, (h // 2) * LANES:(h // 2 + 1) * LANES]
        if h % 2:
            pair = pltpu.roll(pair, HEAD_DIM, 1)
        return jnp.where(lane < HEAD_DIM, pair, extra).astype(BF16)

    @pl.when(i == 0)
    def _():
        k_ones = jnp.where((lane >= r_lane) & (lane < r_lane + 3), 1.0, 0.0)
        vt_tail = (lax.broadcasted_iota(I32, (FOX_VT_ROWS - HEAD_DIM, tq), 0) == 0).astype(BF16)
        place_scr[0] = _fox_placer(c_lane)
        place_scr[1] = _fox_placer(r_lane)
        for c in range(nk):
            rows = slice(c * tq, (c + 1) * tq)
            kc = head_norm(k_ref[rows, :].astype(F32), kg_ref[...])
            vt = v_ref[rows, :].astype(F32).T.astype(BF16)
            extras = _split3_place(ccol_ref[rows, :] * (-log2e), place_scr[0])
            for h in range(N_HEADS):
                extra = extras[:, h * LANES:(h + 1) * LANES] + k_ones
                kn_scr[h, c] = head_tile(kc, h, extra)
                vt_scr[h, c] = jnp.concatenate([vt[h * HEAD_DIM:(h + 1) * HEAD_DIM, :], vt_tail], axis=0)

    qn = head_norm(q_ref[...].astype(F32), qg_ref[...]) * (HEAD_DIM ** -0.5 * log2e)
    bound = bound_ref[...]
    bounded = jnp.max(bound) <= FOX_BOUND_LOG2
    c_i = ccol_ref[pl.ds(pl.multiple_of(i * tq, tq), tq), :] * log2e
    q_ones = jnp.where((lane >= c_lane) & (lane < c_lane + 3), 1.0, 0.0)
    neg_r = _split3_place(jnp.where(bounded, c_i - bound, 0.0), place_scr[1])
    for h in range(N_HEADS):
        q_scr[h] = head_tile(qn, h, q_ones + neg_r[:, h * LANES:(h + 1) * LANES])
    acc_scr[...] = jnp.zeros_like(acc_scr)
    causal = (lax.broadcasted_iota(I32, (tq, tq), 0) <= lax.broadcasted_iota(I32, (tq, tq), 1))

    heads = range(N_HEADS)

    def scores(j):
        return [lax.dot_general(kn_scr[h, j], q_scr[h], (((1,), (1,)), ((), ())),
                                preferred_element_type=F32) for h in heads]

    def stash(s):
        for h in heads:
            s_scr[h] = s[h]

    def absorb_general(j, masked):
        p, alpha = [], []
        for h in heads:
            s = s_scr[h]
            if masked:
                s = jnp.where(causal, s, -jnp.inf)
            m_old = m_scr[h]
            m_new = jnp.maximum(m_old, jnp.max(s, axis=0, keepdims=True))
            alpha.append(jnp.exp2(m_old - m_new))
            ph = jnp.exp2(s - m_new)
            l_scr[h] = alpha[h] * l_scr[h] + jnp.sum(ph, axis=0, keepdims=True)
            m_scr[h] = m_new
            p.append(ph.astype(BF16))
        pv = [jnp.dot(vt_scr[h, j], p[h], preferred_element_type=F32) for h in heads]
        for h in heads:
            acc_scr[h] = alpha[h] * acc_scr[h] + pv[h]

    def absorb_bounded(j, masked):
        p = []
        for h in heads:
            s = s_scr[h]
            if masked:
                s = jnp.where(causal, s, -jnp.inf)
            p.append(jnp.exp2(s).astype(BF16))
        pv = [jnp.dot(vt_scr[h, j], p[h], preferred_element_type=F32) for h in heads]
        for h in heads:
            acc_scr[h] += pv[h]

    def run(absorb, j0):
        stash(scores(j0))

        def body(j, c):
            s_next = scores(j + 1)
            absorb(j, False)
            stash(s_next)
            return c

        lax.fori_loop(j0, i, body, 0)
        absorb(i, True)

    @pl.when(bounded)
    def _():
        run(absorb_bounded, jstart_ref[b * nk + i])

    @pl.when(jnp.logical_not(bounded))
    def _():
        m_scr[...] = jnp.full_like(m_scr, -jnp.inf)
        l_scr[...] = jnp.zeros_like(l_scr)
        run(absorb_general, 0)
        for h in heads:
            acc_scr[h, HEAD_DIM:HEAD_DIM + 1, :] = l_scr[h]

    o_t = jnp.concatenate([acc_scr[h, :HEAD_DIM, :] * (1.0 / acc_scr[h, HEAD_DIM:HEAD_DIM + 1, :]) for h in heads],
                          axis=0)
    o_ref[...] = _rms(o_t.T, og_ref[...])


def _fox_first_block(gcol, tq):
    B, S, _ = gcol.shape
    nk = S // tq
    c2 = gcol[:, :, LANE_FOX:LANE_FOX + N_HEADS] * (1.0 / np.log(2.0))
    first = c2[:, 0::tq, :]
    last = c2[:, tq - 1::tq, :]
    dead = (first[:, :, None, :] - last[:, None, :, :]) < -FOX_SKIP_LOG2
    dead = dead & (jnp.arange(nk)[None, :, None, None] > jnp.arange(nk)[None, None, :, None])
    return jnp.min(jnp.sum(dead, axis=2), axis=-1).astype(I32).reshape(B * nk)


def _fox(big3, gcol, qg, kg, og, tq):
    B, S, _ = big3.shape
    nk = S // tq
    row = pl.BlockSpec((1, GROUP_W), lambda b, i, js: (0, 0))
    grid_spec = pltpu.PrefetchScalarGridSpec(
        num_scalar_prefetch=1, grid=(B, nk),
        in_specs=[pl.BlockSpec((None, tq, GROUP_W), lambda b, i, js: (b, i, COL_FQ)),
                  pl.BlockSpec((None, S, GROUP_W), lambda b, i, js: (b, 0, COL_FK)),
                  pl.BlockSpec((None, S, GROUP_W), lambda b, i, js: (b, 0, COL_FV)),
                  pl.BlockSpec((None, S, LANES), lambda b, i, js: (b, 0, 0)),
                  row, row, row, pl.BlockSpec((1, LANES), lambda b, i, js: (0, 0))],
        out_specs=pl.BlockSpec((None, tq, GROUP_W), lambda b, i, js: (b, i, 0)),
        scratch_shapes=[pltpu.VMEM((N_HEADS, nk, tq, LANES), BF16),
                        pltpu.VMEM((N_HEADS, nk, FOX_VT_ROWS, tq), BF16),
                        pltpu.VMEM((N_HEADS, tq, LANES), BF16),
                        pltpu.VMEM((N_HEADS, 1, tq), F32),
                        pltpu.VMEM((N_HEADS, 1, tq), F32),
                        pltpu.VMEM((N_HEADS, FOX_VT_ROWS, tq), F32),
                        pltpu.VMEM((N_HEADS, tq, tq), F32),
                        pltpu.VMEM((2, 3 * LANES, N_HEADS * LANES), BF16)])
    gmax = lambda g: jnp.max(jnp.abs(g.reshape(N_HEADS, HEAD_DIM)), axis=1)
    bound = _pad_lanes(gmax(qg) * gmax(kg) * (HEAD_DIM * HEAD_DIM ** -0.5 / np.log(2.0) * 1.02), LANE_FOX)
    return pl.pallas_call(
        functools.partial(_fox_body, tq=tq, nk=nk), grid_spec=grid_spec,
        out_shape=jax.ShapeDtypeStruct((B, S, GROUP_W), F32),
        compiler_params=_params("parallel", "arbitrary"), name="fox")(
            _fox_first_block(gcol, tq), big3, big3, big3, gcol, qg, kg, og, bound)


def _gelu(x):
    return 0.5 * x * (1.0 + lax.erf(x * (2.0 ** -0.5)))


def _gmlp_body(u_ref, v_ref, lg_ref, lb_ref, ws_ref, bst_ref, og_ref, o_ref, *, nwin):
    L = GMLP_LEN
    r = lax.broadcasted_iota(I32, (L, L), 0) // CHUNK
    c = lax.broadcasted_iota(I32, (L, L), 1) // CHUNK
    mask = r >= c
    ws = [jnp.where(mask, ws_ref[h], 0.0).astype(BF16) for h in range(N_HEADS)]
    for n in range(nwin):
        u = _gelu(u_ref[n * L:(n + 1) * L, :].astype(F32))
        v = _gelu(v_ref[n * L:(n + 1) * L, :].astype(F32))
        mu = jnp.mean(v, axis=-1, keepdims=True)
        vc = v - mu
        var = jnp.mean(vc * vc, axis=-1, keepdims=True)
        vn = (vc * lax.rsqrt(var + EPS) * lg_ref[...] + lb_ref[...]).astype(BF16)
        mixed = jnp.concatenate(
            [jnp.dot(ws[h], vn[:, h * HEAD_DIM:(h + 1) * HEAD_DIM], preferred_element_type=F32)
             for h in range(N_HEADS)], axis=-1) + bst_ref[...]
        o_ref[n * L:(n + 1) * L, :] = _rms(u * mixed, og_ref[...])


def _gmlp(big, lg, lb, ws, bst, og, tm):
    T = big.shape[0]
    row = pl.BlockSpec((1, GROUP_W), lambda i: (0, 0))
    return pl.pallas_call(
        functools.partial(_gmlp_body, nwin=tm // GMLP_LEN), grid=(T // tm,),
        in_specs=[pl.BlockSpec((tm, GROUP_W), lambda i: (i, COL_GU)),
                  pl.BlockSpec((tm, GROUP_W), lambda i: (i, COL_GV)),
                  row, row,
                  pl.BlockSpec((N_HEADS, GMLP_LEN, GMLP_LEN), lambda i: (0, 0, 0)),
                  pl.BlockSpec((GMLP_LEN, GROUP_W), lambda i: (0, 0)),
                  row],
        out_specs=pl.BlockSpec((tm, GROUP_W), lambda i: (i, 0)),
        out_shape=jax.ShapeDtypeStruct((T, GROUP_W), F32),
        compiler_params=_params("parallel"), name="gmlp")(big, big, lg, lb, ws, bst, og)


def _gdn_body(q_ref, k_ref, v_ref, w_ref, gate_ref, gx_ref, ng_ref, o_ref,
              s_scr, u_scr, wq_scr, a_scr, kd_scr, dl_scr, t_scr, p_scr, rhs_scr, halo_scr, qkv_scr, *, nchunk, nb):
    C = CHUNK
    ts = nchunk * C
    first = pl.program_id(1) == 0

    @pl.when(first)
    def _():
        s_scr[...] = jnp.zeros_like(s_scr)

    ones_bd = _head_ones()
    for bb, a in [(bb, a) for bb in range(nb) for a in range(3)]:
        w = w_ref[a]
        x = (q_ref, k_ref, v_ref)[a][bb].astype(F32)
        xx = jnp.concatenate([jnp.where(first, 0.0, halo_scr[bb * 3 + a]), x], axis=0)
        halo_scr[bb * 3 + a] = x[ts - SUBLANES:, :]
        y = w[CONV_K - 1:CONV_K, :] * x
        for j in range(CONV_K - 1):
            y = y + w[j:j + 1, :] * pltpu.roll(xx, CONV_K - 1 - j, 0)[SUBLANES:, :]
        y = y * jax.nn.sigmoid(y)
        if a < 2:
            y = y * lax.rsqrt(_head_sums(y * y, ones_bd) + EPS)
        qkv_scr[bb * 3 + a] = y * (HEAD_DIM ** -0.5) if a == 0 else y

    W = GROUP_W
    pos = lax.broadcasted_iota(I32, (C, W), 1) % HEAD_DIM
    r = lax.broadcasted_iota(I32, (C, W), 0)
    tri, strict, eye = r >= pos, r > pos, r == pos
    same_head = (lax.broadcasted_iota(I32, (W, W), 0) // HEAD_DIM
                 == lax.broadcasted_iota(I32, (W, W), 1) // HEAD_DIM)
    mmb = functools.partial(jnp.dot, preferred_element_type=F32)

    def block_diag(x):
        return jnp.where(same_head, jnp.concatenate([x.astype(BF16)] * N_HEADS, axis=0), 0.0)

    items = nb * nchunk
    for n in range(items):
        bb = n // nchunk
        rows = slice((n % nchunk) * C, (n % nchunk + 1) * C)
        q, k, v = (qkv_scr[bb * 3 + a, rows, :] for a in range(3))
        gc = gx_ref[bb, rows, :W]
        beta = gx_ref[bb, rows, W:]
        gr = jnp.sum(jnp.where(eye, gc, 0.0), axis=0, keepdims=True)
        decay = jnp.exp(jnp.where(tri, gc - gr, -jnp.inf))
        kb = k * beta
        kk = lax.dot_general(jnp.concatenate([kb, q], axis=0).astype(BF16), block_diag(k),
                             (((1,), (1,)), ((), ())), preferred_element_type=F32)
        x = jnp.where(strict, -(kk[:C] * decay), 0.0)
        t_scr[n] = jnp.where(eye, 1.0, 0.0) + x
        p_scr[n] = x.astype(BF16)
        eg = jnp.exp(gc)
        g_last = gc[C - 1:C, :]
        rhs_scr[n, 0] = block_diag(v * beta)
        rhs_scr[n, 1] = block_diag(kb * eg)
        wq_scr[n, C:, :] = (q * eg).astype(BF16)
        a_scr[n] = jnp.where(tri, kk[C:] * decay, 0.0).astype(BF16)
        kd_scr[n] = (k * jnp.exp(g_last - gc)).astype(BF16)
        dl_scr[n] = jnp.exp(g_last)
    for level in range(1, 6):
        for n in range(items):
            p = p_scr[n]
            p_scr[n] = mmb(p, block_diag(p)).astype(BF16)
        for n in range(items):
            t = t_scr[n]
            t_scr[n] = t + mmb(t.astype(BF16), block_diag(p_scr[n]))
    for n in range(items):
        t = t_scr[n].astype(BF16)
        u_scr[n] = mmb(t, rhs_scr[n, 0])
        wq_scr[n, :C, :] = mmb(t, rhs_scr[n, 1]).astype(BF16)

    ones_bd = _head_ones()
    batch = range(nb)
    state = [s_scr[bb] for bb in batch]
    for c in range(nchunk):
        rows = slice(c * C, (c + 1) * C)
        it = [bb * nchunk + c for bb in batch]
        ws = [mmb(wq_scr[it[bb]], state[bb].astype(BF16)) for bb in batch]
        vb = [(u_scr[it[bb]] - ws[bb][:C]).astype(BF16) for bb in batch]
        o = [ws[bb][C:] + mmb(a_scr[it[bb]], block_diag(vb[bb])) for bb in batch]
        kv = [lax.dot_general(kd_scr[it[bb]], vb[bb], (((0,), (0,)), ((), ())), preferred_element_type=F32)
              for bb in batch]
        state = [state[bb] * dl_scr[it[bb]] + jnp.where(same_head, kv[bb], 0.0) for bb in batch]
        for bb in batch:
            gate = gate_ref[bb, rows, :].astype(F32)
            y = o[bb] * lax.rsqrt(_head_sums(o[bb] * o[bb], ones_bd) * (1.0 / HEAD_DIM) + EPS) * ng_ref[...]
            o_ref[bb, rows, :] = y * (gate * jax.nn.sigmoid(gate))
    for bb in batch:
        s_scr[bb] = state[bb]


def _gdn(big3, conv_w3, gx, ng, ts, nb):
    B, S, _ = big3.shape
    nchunk = ts // CHUNK
    col = lambda c: pl.BlockSpec((nb, ts, GROUP_W), lambda b, i: (b, i, c))
    W = GROUP_W
    items = nb * nchunk
    return pl.pallas_call(
        functools.partial(_gdn_body, nchunk=nchunk, nb=nb), grid=(B // nb, S // ts),
        in_specs=[col(COL_DQ), col(COL_DK), col(COL_DV),
                  pl.BlockSpec((3, CONV_K, W), lambda b, i: (0, 0, 0)),
                  col(COL_DG),
                  pl.BlockSpec((nb, ts, 2 * W), lambda b, i: (b, i, 0)),
                  pl.BlockSpec((1, W), lambda b, i: (0, 0))],
        out_specs=col(0), out_shape=jax.ShapeDtypeStruct((B, S, W), F32),
        scratch_shapes=[pltpu.VMEM((nb, W, W), F32),
                        pltpu.VMEM((items, CHUNK, W), F32),
                        pltpu.VMEM((items, 2 * CHUNK, W), BF16),
                        pltpu.VMEM((items, CHUNK, W), BF16),
                        pltpu.VMEM((items, CHUNK, W), BF16),
                        pltpu.VMEM((items, 1, W), F32),
                        pltpu.VMEM((items, CHUNK, W), F32),
                        pltpu.VMEM((items, CHUNK, W), BF16),
                        pltpu.VMEM((items, 2, W, W), BF16),
                        pltpu.VMEM((nb * 3, SUBLANES, W), F32),
                        pltpu.VMEM((nb * 3, ts, W), F32)],
        compiler_params=_params("parallel", "arbitrary"), name="gdn")(
            big3, big3, big3, conv_w3, big3, gx, ng)


def _pool_body(z_ref, halo_ref, w_ref, sc_ref, og_ref, o_ref, *, ts):
    i = pl.program_id(1)
    hr = 2 * SUBLANES
    z = z_ref[...].astype(F32)
    halo = jnp.where(i == 0, 0.0, halo_ref[...].astype(F32))
    s1 = jnp.concatenate([halo, z], axis=0)
    s2 = s1 + pltpu.roll(s1, 1, 0)
    s4 = s2 + pltpu.roll(s2, 2, 0)
    s8 = s4 + pltpu.roll(s4, 4, 0)
    s16 = s8 + pltpu.roll(s8, 8, 0)
    grp = lax.broadcasted_iota(I32, (ts, GROUP_W), 1) // (GROUP_W // len(POOL_WINDOWS))
    t = lax.broadcasted_iota(I32, (ts, GROUP_W), 0) + i * ts
    total = jnp.where(grp == 0, s2[hr:], jnp.where(grp == 1, s4[hr:], jnp.where(grp == 2, s8[hr:], s16[hr:])))
    win = jnp.where(grp == 0, POOL_WINDOWS[0], jnp.where(grp == 1, POOL_WINDOWS[1],
                    jnp.where(grp == 2, POOL_WINDOWS[2], POOL_WINDOWS[3])))
    pooled = total / jnp.minimum(t + 1, win).astype(F32)
    y = _mm(pooled - z, w_ref[...]) * sc_ref[...]
    o_ref[...] = _rms(y, og_ref[...])


def _pool(big3, wbd, sc, og, ts):
    B, S, _ = big3.shape
    hr = 2 * SUBLANES
    hb = ts // hr
    row = pl.BlockSpec((1, GROUP_W), lambda b, i: (0, 0))
    return pl.pallas_call(
        functools.partial(_pool_body, ts=ts), grid=(B, S // ts),
        in_specs=[pl.BlockSpec((None, ts, GROUP_W), lambda b, i: (b, i, COL_PZ)),
                  pl.BlockSpec((None, hr, GROUP_W), lambda b, i: (b, jnp.maximum(i * hb - 1, 0), COL_PZ)),
                  pl.BlockSpec((GROUP_W, GROUP_W), lambda b, i: (0, 0)), row, row],
        out_specs=pl.BlockSpec((None, ts, GROUP_W), lambda b, i: (b, i, 0)),
        out_shape=jax.ShapeDtypeStruct((B, S, GROUP_W), F32),
        compiler_params=_params("parallel", "parallel"), name="pool")(big3, big3, wbd, sc, og)


def _outproj_body(ya_ref, yb_ref, yc_ref, yd_ref, h_ref, wo_ref, g_ref, rw_ref, rb_ref,
                  hnew_ref, hn_ref, ri_ref, rf_ref, cnt_ref, *, tm, tr):
    y = jnp.concatenate([ya_ref[...], yb_ref[...], yc_ref[...], yd_ref[...]], axis=-1).astype(BF16)
    h_new = h_ref[...] + jnp.dot(y, wo_ref[...], preferred_element_type=F32)
    hnew_ref[...] = h_new
    hn = _rms(h_new, g_ref[...])
    hn_hi, hn_lo = _split(hn)
    hn_ref[...] = hn_hi
    t = jnp.dot(hn_hi, rw_ref[...], preferred_element_type=F32)
    logits = (t[:, :LANES] + t[:, LANES:]
              + jnp.dot(hn_lo, rw_ref[:, :LANES], preferred_element_type=F32)) + rb_ref[...]
    lane = lax.broadcasted_iota(I32, (tm, LANES), 1)
    neg = -jnp.inf
    big_lane = LANES

    def masked_top(vals, mask):
        v = jnp.where(mask, vals, neg)
        mx = jnp.max(v, axis=-1, keepdims=True)
        idx = jnp.min(jnp.where(mask & (v == mx), lane, big_lane), axis=-1, keepdims=True)
        return v, mx, idx

    gmask = lane < N_GROUPS
    gv, gmx, gidx = masked_top(logits, gmask)
    g_top = 1.0 / jnp.sum(jnp.where(gmask, jnp.exp(gv - gmx), 0.0), axis=-1, keepdims=True)
    lo = N_GROUPS + gidx * EXPERTS_PER_GROUP
    emask = (lane >= lo) & (lane < lo + EXPERTS_PER_GROUP)
    ev, emx, eidx1 = masked_top(logits, emask)
    esum = jnp.sum(jnp.where(emask, jnp.exp(ev - emx), 0.0), axis=-1, keepdims=True)
    p1 = 1.0 / esum
    _, emx2, eidx2 = masked_top(logits, emask & (lane != eidx1))
    p2 = jnp.exp(emx2 - emx) / esum
    denom = p1 + p2
    rf_ref[...] = jnp.where(lane == 0, g_top * p1 / denom, jnp.where(lane == 1, g_top * p2 / denom, 0.0))

    expert = [eidx1 - N_GROUPS, eidx2 - N_GROUPS]
    hot = [lane == e for e in expert]
    m = (hot[0] | hot[1]).astype(BF16)
    below = (lax.broadcasted_iota(I32, (tr, tr), 0) > lax.broadcasted_iota(I32, (tr, tr), 1)).astype(BF16)
    before = jnp.concatenate(
        [jnp.dot(below, m[k * tr:(k + 1) * tr, :], preferred_element_type=F32) for k in range(tm // tr)], axis=0)
    rank = [jnp.sum(jnp.where(hot[s], before, 0.0), axis=-1, keepdims=True).astype(I32) for s in range(TOPK_IN)]
    out = jnp.zeros((tm, LANES), I32)
    for s in range(TOPK_IN):
        out = jnp.where(lane == s, expert[s], jnp.where(lane == TOPK_IN + s, rank[s], out))
    ri_ref[...] = out
    for k in range(tm // tr):
        last = (k + 1) * tr - 1
        total = before[last:last + 1, :] + m[last:last + 1, :].astype(F32)
        cnt_ref[k * SUBLANES:(k + 1) * SUBLANES, :] = jnp.broadcast_to(total, (SUBLANES, LANES)).astype(I32)


def _outproj(ya, yb, yc, yd, h2d, wo, g, rw, rb, tm, tr):
    T, D = h2d.shape
    yblk = pl.BlockSpec((tm, GROUP_W), lambda i: (i, 0))
    cnt_rows = tm // tr * SUBLANES
    return pl.pallas_call(
        functools.partial(_outproj_body, tm=tm, tr=tr), grid=(T // tm,),
        in_specs=[yblk, yblk, yblk, yblk,
                  pl.BlockSpec((tm, D), lambda i: (i, 0)),
                  pl.BlockSpec((D, D), lambda i: (0, 0)),
                  pl.BlockSpec((1, D), lambda i: (0, 0)),
                  pl.BlockSpec((D, 2 * LANES), lambda i: (0, 0)),
                  pl.BlockSpec((1, LANES), lambda i: (0, 0))],
        out_specs=[pl.BlockSpec((tm, D), lambda i: (i, 0)),
                   pl.BlockSpec((tm, D), lambda i: (i, 0)),
                   pl.BlockSpec((tm, LANES), lambda i: (i, 0)),
                   pl.BlockSpec((tm, LANES), lambda i: (i, 0)),
                   pl.BlockSpec((cnt_rows, LANES), lambda i: (i, 0))],
        out_shape=[jax.ShapeDtypeStruct((T, D), F32),
                   jax.ShapeDtypeStruct((T, D), BF16),
                   jax.ShapeDtypeStruct((T, LANES), I32),
                   jax.ShapeDtypeStruct((T, LANES), F32),
                   jax.ShapeDtypeStruct((T // tr * SUBLANES, LANES), I32)],
        compiler_params=_params("parallel"), name="outproj")(ya, yb, yc, yd, h2d, wo, g, rw, rb)


def _dispatch_plan(expert, lrank, cnt_tile, tm):
    T = expert.shape[0]
    nt = T // tm
    counts = jnp.sum(cnt_tile, axis=0)
    padded = (counts + RUN_ROWS + MOE_BLOCK - 1) // MOE_BLOCK * MOE_BLOCK
    pad_end = jnp.cumsum(padded)
    pad_start = pad_end - padded
    gstart = pad_start[None, :] + jnp.cumsum(cnt_tile, axis=0) - cnt_tile
    nchunk = (cnt_tile + RUN_ROWS - 1) // RUN_ROWS
    chunk_end = jnp.cumsum(nchunk, axis=1)
    lstart = (chunk_end - nchunk) * RUN_ROWS
    onehot = expert[:, :, None] == jnp.arange(N_EXPERTS, dtype=I32)[None, None, :]
    pick = lambda tab: jnp.sum(jnp.where(onehot, jnp.repeat(tab, tm, axis=0)[:, None, :], 0), axis=-1)
    lpos =(lrank + pick(lstart)).astype(I32)
    max_chunks = tm * TOPK_IN // RUN_ROWS + N_EXPERTS
    c = jnp.arange(max_chunks, dtype=I32)
    ce = jnp.minimum(jnp.sum(chunk_end[:, None, :] <= c[None, :, None], axis=-1), N_EXPERTS - 1)
    ce_hot = ce[:, :, None] == jnp.arange(N_EXPERTS, dtype=I32)[None, None, :]
    take = lambda tab: jnp.sum(jnp.where(ce_hot, tab[:, None, :], 0), axis=-1)
    chunk_row = take(gstart) + (c[None, :] - take(chunk_end - nchunk)) * RUN_ROWS
    chunk_row = jnp.where(c[None, :] < chunk_end[:, -1:], chunk_row, 0)
    table = jnp.concatenate([chunk_row.astype(I32), jnp.zeros((nt, LANES - 1 - max_chunks), I32),
                             chunk_end[:, -1:].astype(I32)], axis=1).reshape(nt, 1, LANES)
    n_blk = -(-(T * TOPK_IN + N_EXPERTS * RUN_ROWS) // MOE_BLOCK) + N_EXPERTS + 1
    blk_start = jnp.arange(n_blk, dtype=I32) * MOE_BLOCK
    blk_e = jnp.minimum(jnp.sum(pad_end[None, :] <= blk_start[:, None], axis=-1), N_EXPERTS - 1).astype(I32)
    n_used = (pad_end[-1] // MOE_BLOCK).astype(I32).reshape(1)
    zero_start = jnp.concatenate([pad_start + counts, pad_end[-1:]])
    zero_end = jnp.concatenate([pad_end, jnp.full((1,), n_blk * MOE_BLOCK, pad_end.dtype)])
    nz = N_EXPERTS + 1
    zfill = jnp.concatenate([zero_start, (zero_end - zero_start) // RUN_ROWS, zero_end - RUN_ROWS,
                             jnp.zeros((LANES - 3 * nz,), zero_start.dtype)]).astype(I32).reshape(1, LANES)
    return lpos, table, zfill, blk_e, n_used, n_blk


def _dispatch_body(zf_ref, tab_ref, lpos_ref, hn_ref, x_hbm, xs, zero, sem, zsem, n_prev, *, tm, nseg, nrow):
    i = pl.program_id(0)
    slot = i % 2
    step = RUN_ROWS * nseg

    @pl.when(i == 0)
    def _():
        zero[...] = jnp.zeros_like(zero)

        def zero_copy(r):
            return pltpu.make_async_copy(zero, x_hbm.at[pl.ds(pl.multiple_of(r * nseg, nseg), step), :], zsem)

        nz = N_EXPERTS + 1

        def for_chunks(fn):
            for e in range(nz):
                def body(k, carry, e=e):
                    fn(zero_copy(zf_ref[0, e] + k * RUN_ROWS))
                    return carry
                lax.fori_loop(0, zf_ref[0, nz + e], body, 0)

        for_chunks(lambda c: c.start())
        for_chunks(lambda c: c.wait())
        for e in range(nz):
            zero_copy(zf_ref[0, 2 * nz + e]).start()
        for e in range(nz):
            zero_copy(zf_ref[0, 2 * nz + e]).wait()

    row = lax.broadcasted_iota(I32, (nrow, tm), 0)
    sel = (row == lpos_ref[0:1, :]) | (row == lpos_ref[1:2, :])
    rows = jnp.dot(sel.astype(BF16), hn_ref[...], preferred_element_type=F32)
    for k in range(nseg):
        xs[slot, pl.ds(k, nrow, stride=nseg), :] = rows[:, k * LANES:(k + 1) * LANES]

    def chunk_copy(sl, c):
        return pltpu.make_async_copy(
            xs.at[sl, pl.ds(pl.multiple_of(c * step, step), step), :],
            x_hbm.at[pl.ds(pl.multiple_of(tab_ref[0, 0, c] * nseg, nseg), step), :], sem.at[sl])

    def wait_chunks(sl, n):
        def wait(c, carry):
            chunk_copy(sl, 0).wait()
            return carry
        lax.fori_loop(0, n, wait, 0)

    n_chunks = tab_ref[0, 0, LANES - 1]

    @pl.when(i > 0)
    def _():
        wait_chunks(1 - slot, n_prev[0])

    def start(c, carry):
        chunk_copy(slot, c).start()
        return carry

    lax.fori_loop(0, n_chunks, start, 0)
    n_prev[0] = n_chunks

    @pl.when(i == pl.num_programs(0) - 1)
    def _():
        wait_chunks(slot, n_chunks)


def _dispatch(zfill, table, lpos_t, hn, n_pad, tm):
    T, D = hn.shape
    nseg = D // LANES
    nrow = tm * TOPK_IN + N_EXPERTS * RUN_ROWS
    return pl.pallas_call(
        functools.partial(_dispatch_body, tm=tm, nseg=nseg, nrow=nrow), grid=(T // tm,),
        in_specs=[pl.BlockSpec(memory_space=pltpu.SMEM),
                  pl.BlockSpec((1, 1, LANES), lambda i: (i, 0, 0), memory_space=pltpu.SMEM),
                  pl.BlockSpec((None, TOPK_IN, tm), lambda i: (i, 0, 0)),
                  pl.BlockSpec((tm, D), lambda i: (i, 0))],
        out_specs=pl.BlockSpec(memory_space=pl.ANY),
        out_shape=jax.ShapeDtypeStruct((n_pad * nseg, LANES), F32),
        scratch_shapes=[pltpu.VMEM((2, nrow * nseg, LANES), F32), pltpu.VMEM((RUN_ROWS * nseg, LANES), F32),
                        pltpu.SemaphoreType.DMA((2,)), pltpu.SemaphoreType.DMA, pltpu.SMEM((1,), I32)],
        compiler_params=_params("arbitrary"), name="dispatch")(zfill, table, lpos_t, hn)


def _moe_body(blk_e_ref, n_used_ref, seg_ref, nxt_ref, x_ref, w1_hbm, w3_hbm, w2_hbm, y_ref,
              wf1, wf3, wf2, w1b, w3b, w2b, sem, *, nseg, layer):
    b = pl.program_id(0)
    R = MOE_BLOCK

    def fetch(e, slot):
        return [pltpu.make_async_copy(src.at[layer, e], dst.at[slot], sem.at[slot, k])
                for k, (src, dst) in enumerate(((w1_hbm, wf1), (w3_hbm, wf3), (w2_hbm, wf2)))]

    @pl.when(b < n_used_ref[0])
    def _():
        @pl.when((b == 0) | (blk_e_ref[b] != blk_e_ref[jnp.maximum(b - 1, 0)]))
        def _():
            e = blk_e_ref[b]
            slot = seg_ref[b] % 2

            @pl.when(b == 0)
            def _():
                for c in fetch(e, slot):
                    c.start()

            for c in fetch(e, slot):
                c.wait()
            w1b[...] = wf1[slot].astype(BF16)
            w3b[...] = wf3[slot].astype(BF16)
            w2b[...] = wf2[slot].astype(BF16)

            @pl.when(nxt_ref[b] >= 0)
            def _():
                for c in fetch(nxt_ref[b], 1 - slot):
                    c.start()

        x = jnp.concatenate([x_ref[pl.ds(s, R, stride=nseg), :] for s in range(nseg)], axis=-1).astype(BF16)
        a = jnp.dot(x, w1b[...], preferred_element_type=F32)
        g = jnp.dot(x, w3b[...], preferred_element_type=F32)
        hid = (a * jax.nn.sigmoid(a) * g).astype(BF16)
        y = jnp.dot(hid, w2b[...], preferred_element_type=F32)
        for s in range(nseg):
            y_ref[pl.ds(s, R, stride=nseg), :] = y[:, s * LANES:(s + 1) * LANES]

    @pl.when(b >= n_used_ref[0])
    def _():
        y_ref[...] = jnp.zeros_like(y_ref)


def _moe(blk_e, n_used, x_rows, w1, w3, w2, layer):
    n_blk = blk_e.shape[0]
    _, _, D, DE = w1.shape
    nseg = D // LANES
    R = MOE_BLOCK
    blk = jnp.arange(n_blk, dtype=I32)
    used = blk < n_used[0]
    change = (blk > 0) & (blk_e != jnp.roll(blk_e, 1)) & used
    seg = jnp.cumsum(change.astype(I32)).astype(I32)
    later = (blk[None, :] > blk[:, None]) & (seg[None, :] > seg[:, None]) & used[None, :]
    nxt_blk = jnp.min(jnp.where(later, blk[None, :], n_blk), axis=1)
    nxt = jnp.where(nxt_blk < n_blk, blk_e[jnp.minimum(nxt_blk, n_blk - 1)], -1).astype(I32)
    rows = lambda b, be, nu, sg, nx: (jnp.minimum(b, nu[0] - 1), 0)
    grid_spec = pltpu.PrefetchScalarGridSpec(
        num_scalar_prefetch=4, grid=(n_blk,),
        in_specs=[pl.BlockSpec((R * nseg, LANES), rows),
                  pl.BlockSpec(memory_space=pl.ANY), pl.BlockSpec(memory_space=pl.ANY),
                  pl.BlockSpec(memory_space=pl.ANY)],
        out_specs=pl.BlockSpec((R * nseg, LANES), lambda b, be, nu, sg, nx: (b, 0)),
        scratch_shapes=[pltpu.VMEM((2, D, DE), F32), pltpu.VMEM((2, D, DE), F32), pltpu.VMEM((2, DE, D), F32),
                        pltpu.VMEM((D, DE), BF16), pltpu.VMEM((D, DE), BF16), pltpu.VMEM((DE, D), BF16),
                        pltpu.SemaphoreType.DMA((2, 3))])
    return pl.pallas_call(
        functools.partial(_moe_body, nseg=nseg, layer=layer), grid_spec=grid_spec,
        out_shape=jax.ShapeDtypeStruct(x_rows.shape, F32),
        compiler_params=_params("arbitrary"), name="moe")(blk_e, n_used, seg, nxt, x_rows, w1, w3, w2)


def _combine_body(tab_ref, tabn_ref, h_ref, rf_ref, lpos_ref, y_hbm, o_ref, ybuf, sem, *, tm, nseg, nrow):
    i = pl.program_id(0)
    slot = i % 2
    step = RUN_ROWS * nseg

    def start_chunks(tab, sl):
        def start(c, carry):
            pltpu.make_async_copy(
                y_hbm.at[pl.ds(pl.multiple_of(tab[0, 0, c] * nseg, nseg), step), :],
                ybuf.at[sl, pl.ds(pl.multiple_of(c * step, step), step), :], sem.at[sl]).start()
            return carry
        lax.fori_loop(0, tab[0, 0, LANES - 1], start, 0)

    @pl.when(i == 0)
    def _():
        ybuf[...] = jnp.zeros_like(ybuf)
        start_chunks(tab_ref, 0)

    @pl.when(i + 1 < pl.num_programs(0))
    def _():
        start_chunks(tabn_ref, 1 - slot)

    col = lax.broadcasted_iota(I32, (tm, nrow), 1)
    sel = jnp.zeros((tm, nrow), F32)
    for s in range(TOPK_IN):
        sel = sel + jnp.where(col == lpos_ref[:, s:s + 1], rf_ref[:, s:s + 1], 0.0)

    def wait(c, carry):
        pltpu.make_async_copy(y_hbm.at[pl.ds(0, step), :], ybuf.at[slot, pl.ds(0, step), :],
                              sem.at[slot]).wait()
        return carry

    lax.fori_loop(0, tab_ref[0, 0, LANES - 1], wait, 0)
    y = jnp.concatenate([ybuf[slot, pl.ds(k, nrow, stride=nseg), :] for k in range(nseg)], axis=-1)
    o_ref[...] = h_ref[...] + jnp.dot(sel.astype(BF16), y.astype(BF16), preferred_element_type=F32)


def _combine(table, h2d, rf, lpos, y_rows, tm):
    T, D = h2d.shape
    nseg = D // LANES
    nt = T // tm
    nrow = tm * TOPK_IN + N_EXPERTS * RUN_ROWS
    return pl.pallas_call(
        functools.partial(_combine_body, tm=tm, nseg=nseg, nrow=nrow), grid=(nt,),
        in_specs=[pl.BlockSpec((1, 1, LANES), lambda i: (i, 0, 0), memory_space=pltpu.SMEM),
                  pl.BlockSpec((1, 1, LANES), lambda i: (jnp.minimum(i + 1, nt - 1), 0, 0),
                               memory_space=pltpu.SMEM),
                  pl.BlockSpec((tm, D), lambda i: (i, 0)),
                  pl.BlockSpec((tm, LANES), lambda i: (i, 0)),
                  pl.BlockSpec((tm, TOPK_IN), lambda i: (i, 0)),
                  pl.BlockSpec(memory_space=pl.ANY)],
        out_specs=pl.BlockSpec((tm, D), lambda i: (i, 0)),
        out_shape=jax.ShapeDtypeStruct((T, D), F32),
        scratch_shapes=[pltpu.VMEM((2, nrow * nseg, LANES), F32), pltpu.SemaphoreType.DMA((2,))],
        compiler_params=_params("arbitrary"), name="combine")(table, table, h2d, rf, lpos, y_rows)


def _pad_lanes(a, lane0, rows=1):
    a = a.reshape(rows, -1)
    return jnp.pad(a, ((0, 0), (lane0, LANES - lane0 - a.shape[-1])))


def _layer(h2d, B, S, p):
    T, D = h2d.shape
    tile = lambda a, n: jnp.tile(a.reshape(1, -1), (1, n))

    offs = np.cumsum([0, GROUP_W, GROUP_W, GROUP_W, N_HEADS, GROUP_W, GROUP_W,
                      GROUP_W, GROUP_W, GROUP_W, N_HEADS, N_HEADS, GROUP_W, GROUP_W])
    seg = lambda k: p['w_in'][:, offs[k]:offs[k + 1]]
    w_all = jnp.concatenate([seg(0), seg(1), seg(2), seg(4), seg(5), seg(6), seg(7), seg(8), seg(11), seg(12),
                             seg(3), seg(9), seg(10), jnp.zeros((D, LANES - 3 * N_HEADS), F32)],
                            axis=1).astype(BF16)
    gate_prm = jnp.concatenate([_pad_lanes(p['fox_f_bias'], LANE_FOX), _pad_lanes(p['gdn_dt_bias'], LANE_DECAY),
                                _pad_lanes(p['gdn_a_log'], LANE_DECAY), jnp.zeros((SUBLANES - 3, LANES), F32)], axis=0)

    big, small = _inproj(h2d, p['attn_norm_g'].reshape(1, D), w_all, tm=min(1024, T))
    big3 = big.reshape(B, S, N_BIG_COLS * GROUP_W)
    ts = min(1024, S)
    gcol, gx = _gates(small.reshape(B, S, LANES), gate_prm, ts)

    tq = min(256, S)
    ya = _fox(big3, gcol, tile(p['fox_qn_g'], N_HEADS), tile(p['fox_kn_g'], N_HEADS),
              p['fox_out_g'].reshape(1, GROUP_W), tq)

    bst = jnp.repeat(p['gmlp_bs'].T, HEAD_DIM, axis=1)
    yb = _gmlp(big, p['gmlp_ln_g'].reshape(1, -1), p['gmlp_ln_b'].reshape(1, -1), p['gmlp_ws'], bst,
               p['gmlp_out_g'].reshape(1, -1), tm=min(1024, T))

    conv_w3 = p['gdn_conv_w'].reshape(CONV_K, 3, GROUP_W).transpose(1, 0, 2)
    nb = 4 if B % 4 == 0 else (2 if B % 2 == 0 else 1)
    yc = _gdn(big3, conv_w3, gx, tile(p['gdn_norm_g'], N_HEADS), min(1024 // nb, S), nb)

    wbd = jax.scipy.linalg.block_diag(*[p['pool_w'][g] for g in range(len(POOL_WINDOWS))]).astype(BF16)
    yd = _pool(big3, wbd, p['pool_scale'].reshape(1, -1), p['pool_out_g'].reshape(1, -1), ts)

    rw = _pad_lanes(jnp.concatenate([p['router_g_w'], p['router_e_w']], axis=1), 0, rows=D)
    rw = jnp.concatenate(_split(rw), axis=1)
    rb = _pad_lanes(jnp.concatenate([p['router_g_b'], p['router_e_b']]), 0)
    flat = lambda a: a.reshape(T, GROUP_W)
    tmd = min(256, T)
    h_new, hn_rows, ri, rf, cnt = _outproj(flat(ya), yb, flat(yc), flat(yd), h2d, p['w_out'].astype(BF16),
                                           p['ffn_norm_g'].reshape(1, D), rw, rb, tm=min(512, T), tr=tmd)
    cnt_tile = cnt.reshape(T // tmd, SUBLANES, LANES)[:, 0, :N_EXPERTS]
    lpos, table, zfill, blk_e, n_used, n_blk = _dispatch_plan(
        ri[:, :TOPK_IN], ri[:, TOPK_IN:2 * TOPK_IN], cnt_tile, tmd)
    lpos_t = lpos.reshape(T // tmd, tmd, TOPK_IN).transpose(0, 2, 1)
    x_rows = _dispatch(zfill, table, lpos_t, hn_rows, n_blk * MOE_BLOCK, tmd)
    y_rows = _moe(blk_e, n_used, x_rows, p['moe_w1'], p['moe_w3'], p['moe_w2'], p['layer'])
    return _combine(table, h_new, rf, lpos, y_rows, tmd)


def kernel(x, attn_norm_g, w_in, w_out, fox_f_bias, fox_qn_g, fox_kn_g, fox_out_g, gmlp_ln_g, gmlp_ln_b, gmlp_ws, gmlp_bs, gmlp_out_g, gdn_conv_w, gdn_a_log, gdn_dt_bias, gdn_norm_g, pool_w, pool_scale, pool_out_g, ffn_norm_g, router_g_w, router_g_b, router_e_w, router_e_b, moe_w1, moe_w3, moe_w2):
    B, S, D = x.shape
    names = ('attn_norm_g', 'w_in', 'w_out', 'fox_f_bias', 'fox_qn_g', 'fox_kn_g', 'fox_out_g', 'gmlp_ln_g',
             'gmlp_ln_b', 'gmlp_ws', 'gmlp_bs', 'gmlp_out_g', 'gdn_conv_w', 'gdn_a_log', 'gdn_dt_bias',
             'gdn_norm_g', 'pool_w', 'pool_scale', 'pool_out_g', 'ffn_norm_g', 'router_g_w', 'router_g_b',
             'router_e_w', 'router_e_b', 'moe_w1', 'moe_w3', 'moe_w2')
    vals = (attn_norm_g, w_in, w_out, fox_f_bias, fox_qn_g, fox_kn_g, fox_out_g, gmlp_ln_g, gmlp_ln_b, gmlp_ws,
            gmlp_bs, gmlp_out_g, gdn_conv_w, gdn_a_log, gdn_dt_bias, gdn_norm_g, pool_w, pool_scale, pool_out_g,
            ffn_norm_g, router_g_w, router_g_b, router_e_w, router_e_b, moe_w1, moe_w3, moe_w2)
    h = x.reshape(B * S, D)
    stacked = ('moe_w1', 'moe_w3', 'moe_w2')
    for l in range(w_in.shape[0]):
        p = {n: (v if n in stacked else v[l]) for n, v in zip(names, vals)}
        p['layer'] = l
        h = _layer(h, B, S, p)
    return h.reshape(B, S, D)
```

```python
import functools

import jax
import jax.numpy as jnp
import numpy as np
from jax import lax
from jax.experimental import pallas as pl
from jax.experimental.pallas import tpu as pltpu

F32 = jnp.float32
BF16 = jnp.bfloat16
I32 = jnp.int32

EPS = 1e-6
HEAD_DIM = 64
GROUP_W = 256
N_HEADS = GROUP_W // HEAD_DIM
CHUNK = 64
GMLP_LEN = 128
CONV_K = 4
POOL_WINDOWS = (2, 4, 8, 16)
N_GROUPS = 4
EXPERTS_PER_GROUP = 8
N_EXPERTS = N_GROUPS * EXPERTS_PER_GROUP
TOPK_IN = 2
MOE_BLOCK = 256
FOX_VT_ROWS = HEAD_DIM + 16
FOX_BOUND_LOG2 = 40.0
FOX_SKIP_LOG2 = 160.0
RUN_ROWS = 16
LANES = 128
SUBLANES = 8
VMEM_LIMIT = 56 * 1024 * 1024

COL_FQ, COL_FK, COL_FV, COL_GU, COL_GV, COL_DQ, COL_DK, COL_DV, COL_DG, COL_PZ = range(10)
N_BIG_COLS = 10
LANE_FOX, LANE_DECAY, LANE_BETA = 0, 4, 8


def _params(*sem):
    return pltpu.CompilerParams(dimension_semantics=sem, vmem_limit_bytes=VMEM_LIMIT)


def _head_ones():
    r = lax.broadcasted_iota(I32, (GROUP_W, GROUP_W), 0) // HEAD_DIM
    c = lax.broadcasted_iota(I32, (GROUP_W, GROUP_W), 1) // HEAD_DIM
    return (r == c).astype(BF16)


def _head_sums(x, ones_bd):
    hi = x.astype(BF16)
    lo = (x - hi.astype(F32)).astype(BF16)
    return (jnp.dot(hi, ones_bd, preferred_element_type=F32)
            + jnp.dot(lo, ones_bd, preferred_element_type=F32))


def _rms(x, g):
    return x * lax.rsqrt(jnp.mean(x * x, axis=-1, keepdims=True) + EPS) * g


def _mm(a, b):
    return jnp.dot(a.astype(BF16), b.astype(BF16), preferred_element_type=F32)


def _mm_nt(a, b):
    return lax.dot_general(a.astype(BF16), b.astype(BF16), (((1,), (1,)), ((), ())),
                           preferred_element_type=F32)


def _mm_tn(a, b):
    return lax.dot_general(a.astype(BF16), b.astype(BF16), (((0,), (0,)), ((), ())),
                           preferred_element_type=F32)


def _split(a):
    hi = a.astype(BF16)
    return hi, (a - hi.astype(F32)).astype(BF16)


def _mm3(a, b):
    ah, al = _split(a)
    bh, bl = _split(b)
    d = functools.partial(jnp.dot, preferred_element_type=F32)
    return d(ah, bh) + (d(ah, bl) + d(al, bh))


def _inproj_body(x_ref, g_ref, w_ref, big_ref, small_ref):
    xn = _rms(x_ref[...], g_ref[...]).astype(BF16)
    nb = big_ref.shape[1]
    big_ref[...] = jnp.dot(xn, w_ref[:, :nb], preferred_element_type=F32).astype(big_ref.dtype)
    small_ref[...] = jnp.dot(xn, w_ref[:, nb:], preferred_element_type=F32)


def _inproj(x2d, g, w, tm):
    T, D = x2d.shape
    nb = w.shape[1] - LANES
    return pl.pallas_call(
        _inproj_body, grid=(T // tm,),
        in_specs=[pl.BlockSpec((tm, D), lambda i: (i, 0)),
                  pl.BlockSpec((1, D), lambda i: (0, 0)),
                  pl.BlockSpec((D, nb + LANES), lambda i: (0, 0))],
        out_specs=[pl.BlockSpec((tm, nb), lambda i: (i, 0)),
                   pl.BlockSpec((tm, LANES), lambda i: (i, 0))],
        out_shape=[jax.ShapeDtypeStruct((T, nb), BF16), jax.ShapeDtypeStruct((T, LANES), F32)],
        compiler_params=_params("parallel"), name="inproj")(x2d, g, w)


def _gates_body(sm_ref, p_ref, col_ref, exp_ref, carry_ref, *, ts):
    @pl.when(pl.program_id(1) == 0)
    def _():
        carry_ref[...] = jnp.zeros_like(carry_ref)

    x = sm_ref[...]
    lane = lax.broadcasted_iota(I32, (ts, LANES), 1)
    is_fox = lane < LANE_DECAY
    is_dec = (lane >= LANE_DECAY) & (lane < LANE_BETA)
    is_beta = (lane >= LANE_BETA) & (lane < LANE_BETA + N_HEADS)
    logf = jax.nn.log_sigmoid(x + p_ref[0:1, :])
    g = -jnp.exp(p_ref[2:3, :]) * jax.nn.softplus(x + p_ref[1:2, :])
    beta = jax.nn.sigmoid(x)
    tri = (lax.broadcasted_iota(I32, (CHUNK, CHUNK), 0) >= lax.broadcasted_iota(I32, (CHUNK, CHUNK), 1)).astype(BF16)
    vals = jnp.where(is_fox, logf, jnp.where(is_dec, g, 0.0))
    hi = vals.astype(BF16)
    mid = (vals - hi.astype(F32)).astype(BF16)
    lo = (vals - hi.astype(F32) - mid.astype(F32)).astype(BF16)
    parts = jnp.concatenate([hi, mid, lo], axis=1)
    running = carry_ref[...]
    cf_blocks, cg_blocks = [], []
    for n in range(ts // CHUNK):
        t = jnp.dot(tri, parts[n * CHUNK:(n + 1) * CHUNK, :], preferred_element_type=F32)
        within = t[:, :LANES] + (t[:, LANES:2 * LANES] + t[:, 2 * LANES:])
        cg_blocks.append(within)
        cf_blocks.append(within + running)
        running = cf_blocks[-1][CHUNK - 1:CHUNK, :]
    cf = jnp.concatenate(cf_blocks, axis=0)
    cg = jnp.concatenate(cg_blocks, axis=0)
    carry_ref[...] = running
    out = jnp.where(is_fox, cf, jnp.where(is_dec, cg, jnp.where(is_beta, beta, 0.0)))
    col_ref[...] = out
    o_hi = out.astype(BF16)
    o_mid = (out - o_hi.astype(F32)).astype(BF16)
    o_lo = (out - o_hi.astype(F32) - o_mid.astype(F32)).astype(BF16)
    src = lax.broadcasted_iota(I32, (3 * LANES, 2 * GROUP_W), 0) % LANES
    dst = lax.broadcasted_iota(I32, (3 * LANES, 2 * GROUP_W), 1)
    want = jnp.where(dst < GROUP_W, LANE_DECAY, LANE_BETA) + (dst % GROUP_W) // HEAD_DIM
    exp_ref[...] = jnp.dot(jnp.concatenate([o_hi, o_mid, o_lo], axis=1), (src == want).astype(BF16),
                           preferred_element_type=F32)


def _gates(small3, prm, ts):
    B, S, _ = small3.shape
    return pl.pallas_call(
        functools.partial(_gates_body, ts=ts), grid=(B, S // ts),
        in_specs=[pl.BlockSpec((None, ts, LANES), lambda b, j: (b, j, 0)),
                  pl.BlockSpec((SUBLANES, LANES), lambda b, j: (0, 0))],
        out_specs=[pl.BlockSpec((None, ts, LANES), lambda b, j: (b, j, 0)),
                   pl.BlockSpec((None, ts, 2 * GROUP_W), lambda b, j: (b, j, 0))],
        out_shape=[jax.ShapeDtypeStruct((B, S, LANES), F32),
                   jax.ShapeDtypeStruct((B, S, 2 * GROUP_W), F32)],
        scratch_shapes=[pltpu.VMEM((1, LANES), F32)],
        compiler_params=_params("parallel", "arbitrary"), name="gates")(small3, prm)


def _fox_placer(lane0):
    src = lax.broadcasted_iota(I32, (3 * LANES, N_HEADS * LANES), 0)
    dst = lax.broadcasted_iota(I32, (3 * LANES, N_HEADS * LANES), 1)
    return ((src % LANES == LANE_FOX + dst // LANES) & (dst % LANES == lane0 + src // LANES)).astype(BF16)


def _split3_place(vals, placer):
    hi = vals.astype(BF16)
    mid = (vals - hi.astype(F32)).astype(BF16)
    lo = (vals - hi.astype(F32) - mid.astype(F32)).astype(BF16)
    return jnp.dot(jnp.concatenate([hi, mid, lo], axis=1), placer, preferred_element_type=F32)


def _fox_body(jstart_ref, q_ref, k_ref, v_ref, ccol_ref, qg_ref, kg_ref, og_ref, bound_ref, o_ref,
              kn_scr, vt_scr, q_scr, m_scr, l_scr, acc_scr, s_scr, place_scr, *, tq, nk):
    b = pl.program_id(0)
    i = pl.program_id(1)
    ones_bd = _head_ones()
    lane = lax.broadcasted_iota(I32, (tq, LANES), 1)
    log2e = 1.0 / np.log(2.0)
    c_lane, r_lane = HEAD_DIM, HEAD_DIM + 3

    def head_norm(x, g):
        ss = _head_sums(x * x, ones_bd)
        return x * lax.rsqrt(ss * (1.0 / HEAD_DIM) + EPS) * g

    def head_tile(x, h, extra):
        pair = x[:, (h // 2) * LANES:(h // 2 + 1) * LANES]
        if h % 2:
            pair = pltpu.roll(pair, HEAD_DIM, 1)
        return jnp.where(lane < HEAD_DIM, pair, extra).astype(BF16)

    @pl.when(i == 0)
    def _():
        k_ones = jnp.where((lane >= r_lane) & (lane < r_lane + 3), 1.0, 0.0)
        vt_tail = (lax.broadcasted_iota(I32, (FOX_VT_ROWS - HEAD_DIM, tq), 0) == 0).astype(BF16)
        place_scr[0] = _fox_placer(c_lane)
        place_scr[1] = _fox_placer(r_lane)
        for c in range(nk):
            rows = slice(c * tq, (c + 1) * tq)
            kc = head_norm(k_ref[rows, :].astype(F32), kg_ref[...])
            vt = v_ref[rows, :].astype(F32).T.astype(BF16)
            extras = _split3_place(ccol_ref[rows, :] * (-log2e), place_scr[0])
            for h in range(N_HEADS):
                extra = extras[:, h * LANES:(h + 1) * LANES] + k_ones
                kn_scr[h, c] = head_tile(kc, h, extra)
                vt_scr[h, c] = jnp.concatenate([vt[h * HEAD_DIM:(h + 1) * HEAD_DIM, :], vt_tail], axis=0)

    qn = head_norm(q_ref[...].astype(F32), qg_ref[...]) * (HEAD_DIM ** -0.5 * log2e)
    bound = bound_ref[...]
    bounded = jnp.max(bound) <= FOX_BOUND_LOG2
    c_i = ccol_ref[pl.ds(pl.multiple_of(i * tq, tq), tq), :] * log2e
    q_ones = jnp.where((lane >= c_lane) & (lane < c_lane + 3), 1.0, 0.0)
    neg_r = _split3_place(jnp.where(bounded, c_i - bound, 0.0), place_scr[1])
    for h in range(N_HEADS):
        q_scr[h] = head_tile(qn, h, q_ones + neg_r[:, h * LANES:(h + 1) * LANES])
    acc_scr[...] = jnp.zeros_like(acc_scr)
    causal = (lax.broadcasted_iota(I32, (tq, tq), 0) <= lax.broadcasted_iota(I32, (tq, tq), 1))

    heads = range(N_HEADS)

    def scores(j):
        return [lax.dot_general(kn_scr[h, j], q_scr[h], (((1,), (1,)), ((), ())),
                                preferred_element_type=F32) for h in heads]

    def stash(s):
        for h in heads:
            s_scr[h] = s[h]

    def absorb_general(j, masked):
        p, alpha = [], []
        for h in heads:
            s = s_scr[h]
            if masked:
                s = jnp.where(causal, s, -jnp.inf)
            m_old = m_scr[h]
            m_new = jnp.maximum(m_old, jnp.max(s, axis=0, keepdims=True))
            alpha.append(jnp.exp2(m_old - m_new))
            ph = jnp.exp2(s - m_new)
            l_scr[h] = alpha[h] * l_scr[h] + jnp.sum(ph, axis=0, keepdims=True)
            m_scr[h] = m_new
            p.append(ph.astype(BF16))
        pv = [jnp.dot(vt_scr[h, j], p[h], preferred_element_type=F32) for h in heads]
        for h in heads:
            acc_scr[h] = alpha[h] * acc_scr[h] + pv[h]

    def absorb_bounded(j, masked):
        p = []
        for h in heads:
            s = s_scr[h]
            if masked:
                s = jnp.where(causal, s, -jnp.inf)
            p.append(jnp.exp2(s).astype(BF16))
        pv = [jnp.dot(vt_scr[h, j], p[h], preferred_element_type=F32) for h in heads]
        for h in heads:
            acc_scr[h] += pv[h]

    def run(absorb, j0):
        stash(scores(j0))

        def body(j, c):
            s_next = scores(j + 1)
            absorb(j, False)
            stash(s_next)
            return c

        lax.fori_loop(j0, i, body, 0)
        absorb(i, True)

    @pl.when(bounded)
    def _():
        run(absorb_bounded, jstart_ref[b * nk + i])

    @pl.when(jnp.logical_not(bounded))
    def _():
        m_scr[...] = jnp.full_like(m_scr, -jnp.inf)
        l_scr[...] = jnp.zeros_like(l_scr)
        run(absorb_general, 0)
        for h in heads:
            acc_scr[h, HEAD_DIM:HEAD_DIM + 1, :] = l_scr[h]

    o_t = jnp.concatenate([acc_scr[h, :HEAD_DIM, :] * (1.0 / acc_scr[h, HEAD_DIM:HEAD_DIM + 1, :]) for h in heads],
                          axis=0)
    o_ref[...] = _rms(o_t.T, og_ref[...])


def _fox_first_block(gcol, tq):
    B, S, _ = gcol.shape
    nk = S // tq
    c2 = gcol[:, :, LANE_FOX:LANE_FOX + N_HEADS] * (1.0 / np.log(2.0))
    first = c2[:, 0::tq, :]
    last = c2[:, tq - 1::tq, :]
    dead = (first[:, :, None, :] - last[:, None, :, :]) < -FOX_SKIP_LOG2
    dead = dead & (jnp.arange(nk)[None, :, None, None] > jnp.arange(nk)[None, None, :, None])
    return jnp.min(jnp.sum(dead, axis=2), axis=-1).astype(I32).reshape(B * nk)


def _fox(big3, gcol, qg, kg, og, tq):
    B, S, _ = big3.shape
    nk = S // tq
    row = pl.BlockSpec((1, GROUP_W), lambda b, i, js: (0, 0))
    grid_spec = pltpu.PrefetchScalarGridSpec(
        num_scalar_prefetch=1, grid=(B, nk),
        in_specs=[pl.BlockSpec((None, tq, GROUP_W), lambda b, i, js: (b, i, COL_FQ)),
                  pl.BlockSpec((None, S, GROUP_W), lambda b, i, js: (b, 0, COL_FK)),
                  pl.BlockSpec((None, S, GROUP_W), lambda b, i, js: (b, 0, COL_FV)),
                  pl.BlockSpec((None, S, LANES), lambda b, i, js: (b, 0, 0)),
                  row, row, row, pl.BlockSpec((1, LANES), lambda b, i, js: (0, 0))],
        out_specs=pl.BlockSpec((None, tq, GROUP_W), lambda b, i, js: (b, i, 0)),
        scratch_shapes=[pltpu.VMEM((N_HEADS, nk, tq, LANES), BF16),
                        pltpu.VMEM((N_HEADS, nk, FOX_VT_ROWS, tq), BF16),
                        pltpu.VMEM((N_HEADS, tq, LANES), BF16),
                        pltpu.VMEM((N_HEADS, 1, tq), F32),
                        pltpu.VMEM((N_HEADS, 1, tq), F32),
                        pltpu.VMEM((N_HEADS, FOX_VT_ROWS, tq), F32),
                        pltpu.VMEM((N_HEADS, tq, tq), F32),
                        pltpu.VMEM((2, 3 * LANES, N_HEADS * LANES), BF16)])
    gmax = lambda g: jnp.max(jnp.abs(g.reshape(N_HEADS, HEAD_DIM)), axis=1)
    bound = _pad_lanes(gmax(qg) * gmax(kg) * (HEAD_DIM * HEAD_DIM ** -0.5 / np.log(2.0) * 1.02), LANE_FOX)
    return pl.pallas_call(
        functools.partial(_fox_body, tq=tq, nk=nk), grid_spec=grid_spec,
        out_shape=jax.ShapeDtypeStruct((B, S, GROUP_W), F32),
        compiler_params=_params("parallel", "arbitrary"), name="fox")(
            _fox_first_block(gcol, tq), big3, big3, big3, gcol, qg, kg, og, bound)


def _gelu(x):
    return 0.5 * x * (1.0 + lax.erf(x * (2.0 ** -0.5)))


def _gmlp_body(u_ref, v_ref, lg_ref, lb_ref, ws_ref, bst_ref, og_ref, o_ref, *, nwin):
    L = GMLP_LEN
    r = lax.broadcasted_iota(I32, (L, L), 0) // CHUNK
    c = lax.broadcasted_iota(I32, (L, L), 1) // CHUNK
    mask = r >= c
    ws = [jnp.where(mask, ws_ref[h], 0.0).astype(BF16) for h in range(N_HEADS)]
    for n in range(nwin):
        u = _gelu(u_ref[n * L:(n + 1) * L, :].astype(F32))
        v = _gelu(v_ref[n * L:(n + 1) * L, :].astype(F32))
        mu = jnp.mean(v, axis=-1, keepdims=True)
        vc = v - mu
        var = jnp.mean(vc * vc, axis=-1, keepdims=True)
        vn = (vc * lax.rsqrt(var + EPS) * lg_ref[...] + lb_ref[...]).astype(BF16)
        mixed = jnp.concatenate(
            [jnp.dot(ws[h], vn[:, h * HEAD_DIM:(h + 1) * HEAD_DIM], preferred_element_type=F32)
             for h in range(N_HEADS)], axis=-1) + bst_ref[...]
        o_ref[n * L:(n + 1) * L, :] = _rms(u * mixed, og_ref[...])


def _gmlp(big, lg, lb, ws, bst, og, tm):
    T = big.shape[0]
    row = pl.BlockSpec((1, GROUP_W), lambda i: (0, 0))
    return pl.pallas_call(
        functools.partial(_gmlp_body, nwin=tm // GMLP_LEN), grid=(T // tm,),
        in_specs=[pl.BlockSpec((tm, GROUP_W), lambda i: (i, COL_GU)),
                  pl.BlockSpec((tm, GROUP_W), lambda i: (i, COL_GV)),
                  row, row,
                  pl.BlockSpec((N_HEADS, GMLP_LEN, GMLP_LEN), lambda i: (0, 0, 0)),
                  pl.BlockSpec((GMLP_LEN, GROUP_W), lambda i: (0, 0)),
                  row],
        out_specs=pl.BlockSpec((tm, GROUP_W), lambda i: (i, 0)),
        out_shape=jax.ShapeDtypeStruct((T, GROUP_W), F32),
        compiler_params=_params("parallel"), name="gmlp")(big, big, lg, lb, ws, bst, og)


def _gdn_body(q_ref, k_ref, v_ref, w_ref, gate_ref, gx_ref, ng_ref, o_ref,
              s_scr, u_scr, wq_scr, a_scr, kd_scr, dl_scr, t_scr, p_scr, rhs_scr, halo_scr, qkv_scr, *, nchunk, nb):
    C = CHUNK
    ts = nchunk * C
    first = pl.program_id(1) == 0

    @pl.when(first)
    def _():
        s_scr[...] = jnp.zeros_like(s_scr)

    ones_bd = _head_ones()
    for bb, a in [(bb, a) for bb in range(nb) for a in range(3)]:
        w = w_ref[a]
        x = (q_ref, k_ref, v_ref)[a][bb].astype(F32)
        xx = jnp.concatenate([jnp.where(first, 0.0, halo_scr[bb * 3 + a]), x], axis=0)
        halo_scr[bb * 3 + a] = x[ts - SUBLANES:, :]
        y = w[CONV_K - 1:CONV_K, :] * x
        for j in range(CONV_K - 1):
            y = y + w[j:j + 1, :] * pltpu.roll(xx, CONV_K - 1 - j, 0)[SUBLANES:, :]
        y = y * jax.nn.sigmoid(y)
        if a < 2:
            y = y * lax.rsqrt(_head_sums(y * y, ones_bd) + EPS)
        qkv_scr[bb * 3 + a] = y * (HEAD_DIM ** -0.5) if a == 0 else y

    W = GROUP_W
    pos = lax.broadcasted_iota(I32, (C, W), 1) % HEAD_DIM
    r = lax.broadcasted_iota(I32, (C, W), 0)
    tri, strict, eye = r >= pos, r > pos, r == pos
    same_head = (lax.broadcasted_iota(I32, (W, W), 0) // HEAD_DIM
                 == lax.broadcasted_iota(I32, (W, W), 1) // HEAD_DIM)
    mmb = functools.partial(jnp.dot, preferred_element_type=F32)

    def block_diag(x):
        return jnp.where(same_head, jnp.concatenate([x.astype(BF16)] * N_HEADS, axis=0), 0.0)

    items = nb * nchunk
    for n in range(items):
        bb = n // nchunk
        rows = slice((n % nchunk) * C, (n % nchunk + 1) * C)
        q, k, v = (qkv_scr[bb * 3 + a, rows, :] for a in range(3))
        gc = gx_ref[bb, rows, :W]
        beta = gx_ref[bb, rows, W:]
        gr = jnp.sum(jnp.where(eye, gc, 0.0), axis=0, keepdims=True)
        decay = jnp.exp(jnp.where(tri, gc - gr, -jnp.inf))
        kb = k * beta
        kk = lax.dot_general(jnp.concatenate([kb, q], axis=0).astype(BF16), block_diag(k),
                             (((1,), (1,)), ((), ())), preferred_element_type=F32)
        x = jnp.where(strict, -(kk[:C] * decay), 0.0)
        t_scr[n] = jnp.where(eye, 1.0, 0.0) + x
        p_scr[n] = x.astype(BF16)
        eg = jnp.exp(gc)
        g_last = gc[C - 1:C, :]
        rhs_scr[n, 0] = block_diag(v * beta)
        rhs_scr[n, 1] = block_diag(kb * eg)
        wq_scr[n, C:, :] = (q * eg).astype(BF16)
        a_scr[n] = jnp.where(tri, kk[C:] * decay, 0.0).astype(BF16)
        kd_scr[n] = (k * jnp.exp(g_last - gc)).astype(BF16)
        dl_scr[n] = jnp.exp(g_last)
    for level in range(1, 6):
        for n in range(items):
            p = p_scr[n]
            p_scr[n] = mmb(p, block_diag(p)).astype(BF16)
        for n in range(items):
            t = t_scr[n]
            t_scr[n] = t + mmb(t.astype(BF16), block_diag(p_scr[n]))
    for n in range(items):
        t = t_scr[n].astype(BF16)
        u_scr[n] = mmb(t, rhs_scr[n, 0])
        wq_scr[n, :C, :] = mmb(t, rhs_scr[n, 1]).astype(BF16)

    ones_bd = _head_ones()
    batch = range(nb)
    state = [s_scr[bb] for bb in batch]
    for c in range(nchunk):
        rows = slice(c * C, (c + 1) * C)
        it = [bb * nchunk + c for bb in batch]
        ws = [mmb(wq_scr[it[bb]], state[bb].astype(BF16)) for bb in batch]
        vb = [(u_scr[it[bb]] - ws[bb][:C]).astype(BF16) for bb in batch]
        o = [ws[bb][C:] + mmb(a_scr[it[bb]], block_diag(vb[bb])) for bb in batch]
        kv = [lax.dot_general(kd_scr[it[bb]], vb[bb], (((0,), (0,)), ((), ())), preferred_element_type=F32)
              for bb in batch]
        state = [state[bb] * dl_scr[it[bb]] + jnp.where(same_head, kv[bb], 0.0) for bb in batch]
        for bb in batch:
            gate = gate_ref[bb, rows, :].astype(F32)
            y = o[bb] * lax.rsqrt(_head_sums(o[bb] * o[bb], ones_bd) * (1.0 / HEAD_DIM) + EPS) * ng_ref[...]
            o_ref[bb, rows, :] = y * (gate * jax.nn.sigmoid(gate))
    for bb in batch:
        s_scr[bb] = state[bb]


def _gdn(big3, conv_w3, gx, ng, ts, nb):
    B, S, _ = big3.shape
    nchunk = ts // CHUNK
    col = lambda c: pl.BlockSpec((nb, ts, GROUP_W), lambda b, i: (b, i, c))
    W = GROUP_W
    items = nb * nchunk
    return pl.pallas_call(
        functools.partial(_gdn_body, nchunk=nchunk, nb=nb), grid=(B // nb, S // ts),
        in_specs=[col(COL_DQ), col(COL_DK), col(COL_DV),
                  pl.BlockSpec((3, CONV_K, W), lambda b, i: (0, 0, 0)),
                  col(COL_DG),
                  pl.BlockSpec((nb, ts, 2 * W), lambda b, i: (b, i, 0)),
                  pl.BlockSpec((1, W), lambda b, i: (0, 0))],
        out_specs=col(0), out_shape=jax.ShapeDtypeStruct((B, S, W), F32),
        scratch_shapes=[pltpu.VMEM((nb, W, W), F32),
                        pltpu.VMEM((items, CHUNK, W), F32),
                        pltpu.VMEM((items, 2 * CHUNK, W), BF16),
                        pltpu.VMEM((items, CHUNK, W), BF16),
                        pltpu.VMEM((items, CHUNK, W), BF16),
                        pltpu.VMEM((items, 1, W), F32),
                        pltpu.VMEM((items, CHUNK, W), F32),
                        pltpu.VMEM((items, CHUNK, W), BF16),
                        pltpu.VMEM((items, 2, W, W), BF16),
                        pltpu.VMEM((nb * 3, SUBLANES, W), F32),
                        pltpu.VMEM((nb * 3, ts, W), F32)],
        compiler_params=_params("parallel", "arbitrary"), name="gdn")(
            big3, big3, big3, conv_w3, big3, gx, ng)


def _pool_body(z_ref, halo_ref, w_ref, sc_ref, og_ref, o_ref, *, ts):
    i = pl.program_id(1)
    hr = 2 * SUBLANES
    z = z_ref[...].astype(F32)
    halo = jnp.where(i == 0, 0.0, halo_ref[...].astype(F32))
    s1 = jnp.concatenate([halo, z], axis=0)
    s2 = s1 + pltpu.roll(s1, 1, 0)
    s4 = s2 + pltpu.roll(s2, 2, 0)
    s8 = s4 + pltpu.roll(s4, 4, 0)
    s16 = s8 + pltpu.roll(s8, 8, 0)
    grp = lax.broadcasted_iota(I32, (ts, GROUP_W), 1) // (GROUP_W // len(POOL_WINDOWS))
    t = lax.broadcasted_iota(I32, (ts, GROUP_W), 0) + i * ts
    total = jnp.where(grp == 0, s2[hr:], jnp.where(grp == 1, s4[hr:], jnp.where(grp == 2, s8[hr:], s16[hr:])))
    win = jnp.where(grp == 0, POOL_WINDOWS[0], jnp.where(grp == 1, POOL_WINDOWS[1],
                    jnp.where(grp == 2, POOL_WINDOWS[2], POOL_WINDOWS[3])))
    pooled = total / jnp.minimum(t + 1, win).astype(F32)
    y = _mm(pooled - z, w_ref[...]) * sc_ref[...]
    o_ref[...] = _rms(y, og_ref[...])


def _pool(big3, wbd, sc, og, ts):
    B, S, _ = big3.shape
    hr = 2 * SUBLANES
    hb = ts // hr
    row = pl.BlockSpec((1, GROUP_W), lambda b, i: (0, 0))
    return pl.pallas_call(
        functools.partial(_pool_body, ts=ts), grid=(B, S // ts),
        in_specs=[pl.BlockSpec((None, ts, GROUP_W), lambda b, i: (b, i, COL_PZ)),
                  pl.BlockSpec((None, hr, GROUP_W), lambda b, i: (b, jnp.maximum(i * hb - 1, 0), COL_PZ)),
                  pl.BlockSpec((GROUP_W, GROUP_W), lambda b, i: (0, 0)), row, row],
        out_specs=pl.BlockSpec((None, ts, GROUP_W), lambda b, i: (b, i, 0)),
        out_shape=jax.ShapeDtypeStruct((B, S, GROUP_W), F32),
        compiler_params=_params("parallel", "parallel"), name="pool")(big3, big3, wbd, sc, og)


def _outproj_body(ya_ref, yb_ref, yc_ref, yd_ref, h_ref, wo_ref, g_ref, rw_ref, rb_ref,
                  hnew_ref, hn_ref, ri_ref, rf_ref, cnt_ref, *, tm, tr):
    y = jnp.concatenate([ya_ref[...], yb_ref[...], yc_ref[...], yd_ref[...]], axis=-1).astype(BF16)
    h_new = h_ref[...] + jnp.dot(y, wo_ref[...], preferred_element_type=F32)
    hnew_ref[...] = h_new
    hn = _rms(h_new, g_ref[...])
    hn_hi, hn_lo = _split(hn)
    hn_ref[...] = hn_hi
    t = jnp.dot(hn_hi, rw_ref[...], preferred_element_type=F32)
    logits = (t[:, :LANES] + t[:, LANES:]
              + jnp.dot(hn_lo, rw_ref[:, :LANES], preferred_element_type=F32)) + rb_ref[...]
    lane = lax.broadcasted_iota(I32, (tm, LANES), 1)
    neg = -jnp.inf
    big_lane = LANES

    def masked_top(vals, mask):
        v = jnp.where(mask, vals, neg)
        mx = jnp.max(v, axis=-1, keepdims=True)
        idx = jnp.min(jnp.where(mask & (v == mx), lane, big_lane), axis=-1, keepdims=True)
        return v, mx, idx

    gmask = lane < N_GROUPS
    gv, gmx, gidx = masked_top(logits, gmask)
    g_top = 1.0 / jnp.sum(jnp.where(gmask, jnp.exp(gv - gmx), 0.0), axis=-1, keepdims=True)
    lo = N_GROUPS + gidx * EXPERTS_PER_GROUP
    emask = (lane >= lo) & (lane < lo + EXPERTS_PER_GROUP)
    ev, emx, eidx1 = masked_top(logits, emask)
    esum = jnp.sum(jnp.where(emask, jnp.exp(ev - emx), 0.0), axis=-1, keepdims=True)
    p1 = 1.0 / esum
    _, emx2, eidx2 = masked_top(logits, emask & (lane != eidx1))
    p2 = jnp.exp(emx2 - emx) / esum
    denom = p1 + p2
    rf_ref[...] = jnp.where(lane == 0, g_top * p1 / denom, jnp.where(lane == 1, g_top * p2 / denom, 0.0))

    expert = [eidx1 - N_GROUPS, eidx2 - N_GROUPS]
    hot = [lane == e for e in expert]
    m = (hot[0] | hot[1]).astype(BF16)
    below = (lax.broadcasted_iota(I32, (tr, tr), 0) > lax.broadcasted_iota(I32, (tr, tr), 1)).astype(BF16)
    before = jnp.concatenate(
        [jnp.dot(below, m[k * tr:(k + 1) * tr, :], preferred_element_type=F32) for k in range(tm // tr)], axis=0)
    rank = [jnp.sum(jnp.where(hot[s], before, 0.0), axis=-1, keepdims=True).astype(I32) for s in range(TOPK_IN)]
    out = jnp.zeros((tm, LANES), I32)
    for s in range(TOPK_IN):
        out = jnp.where(lane == s, expert[s], jnp.where(lane == TOPK_IN + s, rank[s], out))
    ri_ref[...] = out
    for k in range(tm // tr):
        last = (k + 1) * tr - 1
        total = before[last:last + 1, :] + m[last:last + 1, :].astype(F32)
        cnt_ref[k * SUBLANES:(k + 1) * SUBLANES, :] = jnp.broadcast_to(total, (SUBLANES, LANES)).astype(I32)


def _outproj(ya, yb, yc, yd, h2d, wo, g, rw, rb, tm, tr):
    T, D = h2d.shape
    yblk = pl.BlockSpec((tm, GROUP_W), lambda i: (i, 0))
    cnt_rows = tm // tr * SUBLANES
    return pl.pallas_call(
        functools.partial(_outproj_body, tm=tm, tr=tr), grid=(T // tm,),
        in_specs=[yblk, yblk, yblk, yblk,
                  pl.BlockSpec((tm, D), lambda i: (i, 0)),
                  pl.BlockSpec((D, D), lambda i: (0, 0)),
                  pl.BlockSpec((1, D), lambda i: (0, 0)),
                  pl.BlockSpec((D, 2 * LANES), lambda i: (0, 0)),
                  pl.BlockSpec((1, LANES), lambda i: (0, 0))],
        out_specs=[pl.BlockSpec((tm, D), lambda i: (i, 0)),
                   pl.BlockSpec((tm, D), lambda i: (i, 0)),
                   pl.BlockSpec((tm, LANES), lambda i: (i, 0)),
                   pl.BlockSpec((tm, LANES), lambda i: (i, 0)),
                   pl.BlockSpec((cnt_rows, LANES), lambda i: (i, 0))],
        out_shape=[jax.ShapeDtypeStruct((T, D), F32),
                   jax.ShapeDtypeStruct((T, D), BF16),
                   jax.ShapeDtypeStruct((T, LANES), I32),
                   jax.ShapeDtypeStruct((T, LANES), F32),
                   jax.ShapeDtypeStruct((T // tr * SUBLANES, LANES), I32)],
        compiler_params=_params("parallel"), name="outproj")(ya, yb, yc, yd, h2d, wo, g, rw, rb)


def _dispatch_plan(expert, lrank, cnt_tile, tm):
    T = expert.shape[0]
    nt = T // tm
    counts = jnp.sum(cnt_tile, axis=0)
    padded = (counts + RUN_ROWS + MOE_BLOCK - 1) // MOE_BLOCK * MOE_BLOCK
    pad_end = jnp.cumsum(padded)
    pad_start = pad_end - padded
    gstart = pad_start[None, :] + jnp.cumsum(cnt_tile, axis=0) - cnt_tile
    nchunk = (cnt_tile + RUN_ROWS - 1) // RUN_ROWS
    chunk_end = jnp.cumsum(nchunk, axis=1)
    lstart = (chunk_end - nchunk) * RUN_ROWS
    onehot = expert[:, :, None] == jnp.arange(N_EXPERTS, dtype=I32)[None, None, :]
    pick = lambda tab: jnp.sum(jnp.where(onehot, jnp.repeat(tab, tm, axis=0)[:, None, :], 0), axis=-1)
    lpos =(lrank + pick(lstart)).astype(I32)
    max_chunks = tm * TOPK_IN // RUN_ROWS + N_EXPERTS
    c = jnp.arange(max_chunks, dtype=I32)
    ce = jnp.minimum(jnp.sum(chunk_end[:, None, :] <= c[None, :, None], axis=-1), N_EXPERTS - 1)
    ce_hot = ce[:, :, None] == jnp.arange(N_EXPERTS, dtype=I32)[None, None, :]
    take = lambda tab: jnp.sum(jnp.where(ce_hot, tab[:, None, :], 0), axis=-1)
    chunk_row = take(gstart) + (c[None, :] - take(chunk_end - nchunk)) * RUN_ROWS
    chunk_row = jnp.where(c[None, :] < chunk_end[:, -1:], chunk_row, 0)
    table = jnp.concatenate([chunk_row.astype(I32), jnp.zeros((nt, LANES - 1 - max_chunks), I32),
                             chunk_end[:, -1:].astype(I32)], axis=1).reshape(nt, 1, LANES)
    n_blk = -(-(T * TOPK_IN + N_EXPERTS * RUN_ROWS) // MOE_BLOCK) + N_EXPERTS + 1
    blk_start = jnp.arange(n_blk, dtype=I32) * MOE_BLOCK
    blk_e = jnp.minimum(jnp.sum(pad_end[None, :] <= blk_start[:, None], axis=-1), N_EXPERTS - 1).astype(I32)
    n_used = (pad_end[-1] // MOE_BLOCK).astype(I32).reshape(1)
    zero_start = jnp.concatenate([pad_start + counts, pad_end[-1:]])
    zero_end = jnp.concatenate([pad_end, jnp.full((1,), n_blk * MOE_BLOCK, pad_end.dtype)])
    nz = N_EXPERTS + 1
    zfill = jnp.concatenate([zero_start, (zero_end - zero_start) // RUN_ROWS, zero_end - RUN_ROWS,
                             jnp.zeros((LANES - 3 * nz,), zero_start.dtype)]).astype(I32).reshape(1, LANES)
    return lpos, table, zfill, blk_e, n_used, n_blk


def _dispatch_body(zf_ref, tab_ref, lpos_ref, hn_ref, x_hbm, xs, zero, sem, zsem, n_prev, *, tm, nseg, nrow):
    i = pl.program_id(0)
    slot = i % 2
    step = RUN_ROWS * nseg

    @pl.when(i == 0)
    def _():
        zero[...] = jnp.zeros_like(zero)

        def zero_copy(r):
            return pltpu.make_async_copy(zero, x_hbm.at[pl.ds(pl.multiple_of(r * nseg, nseg), step), :], zsem)

        nz = N_EXPERTS + 1

        def for_chunks(fn):
            for e in range(nz):
                def body(k, carry, e=e):
                    fn(zero_copy(zf_ref[0, e] + k * RUN_ROWS))
                    return carry
                lax.fori_loop(0, zf_ref[0, nz + e], body, 0)

        for_chunks(lambda c: c.start())
        for_chunks(lambda c: c.wait())
        for e in range(nz):
            zero_copy(zf_ref[0, 2 * nz + e]).start()
        for e in range(nz):
            zero_copy(zf_ref[0, 2 * nz + e]).wait()

    row = lax.broadcasted_iota(I32, (nrow, tm), 0)
    sel = (row == lpos_ref[0:1, :]) | (row == lpos_ref[1:2, :])
    rows = jnp.dot(sel.astype(BF16), hn_ref[...], preferred_element_type=F32)
    for k in range(nseg):
        xs[slot, pl.ds(k, nrow, stride=nseg), :] = rows[:, k * LANES:(k + 1) * LANES]

    def chunk_copy(sl, c):
        return pltpu.make_async_copy(
            xs.at[sl, pl.ds(pl.multiple_of(c * step, step), step), :],
            x_hbm.at[pl.ds(pl.multiple_of(tab_ref[0, 0, c] * nseg, nseg), step), :], sem.at[sl])

    def wait_chunks(sl, n):
        def wait(c, carry):
            chunk_copy(sl, 0).wait()
            return carry
        lax.fori_loop(0, n, wait, 0)

    n_chunks = tab_ref[0, 0, LANES - 1]

    @pl.when(i > 0)
    def _():
        wait_chunks(1 - slot, n_prev[0])

    def start(c, carry):
        chunk_copy(slot, c).start()
        return carry

    lax.fori_loop(0, n_chunks, start, 0)
    n_prev[0] = n_chunks

    @pl.when(i == pl.num_programs(0) - 1)
    def _():
        wait_chunks(slot, n_chunks)


def _dispatch(zfill, table, lpos_t, hn, n_pad, tm):
    T, D = hn.shape
    nseg = D // LANES
    nrow = tm * TOPK_IN + N_EXPERTS * RUN_ROWS
    return pl.pallas_call(
        functools.partial(_dispatch_body, tm=tm, nseg=nseg, nrow=nrow), grid=(T // tm,),
        in_specs=[pl.BlockSpec(memory_space=pltpu.SMEM),
                  pl.BlockSpec((1, 1, LANES), lambda i: (i, 0, 0), memory_space=pltpu.SMEM),
                  pl.BlockSpec((None, TOPK_IN, tm), lambda i: (i, 0, 0)),
                  pl.BlockSpec((tm, D), lambda i: (i, 0))],
        out_specs=pl.BlockSpec(memory_space=pl.ANY),
        out_shape=jax.ShapeDtypeStruct((n_pad * nseg, LANES), F32),
        scratch_shapes=[pltpu.VMEM((2, nrow * nseg, LANES), F32), pltpu.VMEM((RUN_ROWS * nseg, LANES), F32),
                        pltpu.SemaphoreType.DMA((2,)), pltpu.SemaphoreType.DMA, pltpu.SMEM((1,), I32)],
        compiler_params=_params("arbitrary"), name="dispatch")(zfill, table, lpos_t, hn)


def _moe_body(blk_e_ref, n_used_ref, seg_ref, nxt_ref, x_ref, w1_hbm, w3_hbm, w2_hbm, y_ref,
              wf1, wf3, wf2, w1b, w3b, w2b, sem, *, nseg, layer):
    b = pl.program_id(0)
    R = MOE_BLOCK

    def fetch(e, slot):
        return [pltpu.make_async_copy(src.at[layer, e], dst.at[slot], sem.at[slot, k])
                for k, (src, dst) in enumerate(((w1_hbm, wf1), (w3_hbm, wf3), (w2_hbm, wf2)))]

    @pl.when(b < n_used_ref[0])
    def _():
        @pl.when((b == 0) | (blk_e_ref[b] != blk_e_ref[jnp.maximum(b - 1, 0)]))
        def _():
            e = blk_e_ref[b]
            slot = seg_ref[b] % 2

            @pl.when(b == 0)
            def _():
                for c in fetch(e, slot):
                    c.start()

            for c in fetch(e, slot):
                c.wait()
            w1b[...] = wf1[slot].astype(BF16)
            w3b[...] = wf3[slot].astype(BF16)
            w2b[...] = wf2[slot].astype(BF16)

            @pl.when(nxt_ref[b] >= 0)
            def _():
                for c in fetch(nxt_ref[b], 1 - slot):
                    c.start()

        x = jnp.concatenate([x_ref[pl.ds(s, R, stride=nseg), :] for s in range(nseg)], axis=-1).astype(BF16)
        a = jnp.dot(x, w1b[...], preferred_element_type=F32)
        g = jnp.dot(x, w3b[...], preferred_element_type=F32)
        hid = (a * jax.nn.sigmoid(a) * g).astype(BF16)
        y = jnp.dot(hid, w2b[...], preferred_element_type=F32)
        for s in range(nseg):
            y_ref[pl.ds(s, R, stride=nseg), :] = y[:, s * LANES:(s + 1) * LANES]

    @pl.when(b >= n_used_ref[0])
    def _():
        y_ref[...] = jnp.zeros_like(y_ref)


def _moe(blk_e, n_used, x_rows, w1, w3, w2, layer):
    n_blk = blk_e.shape[0]
    _, _, D, DE = w1.shape
    nseg = D // LANES
    R = MOE_BLOCK
    blk = jnp.arange(n_blk, dtype=I32)
    used = blk < n_used[0]
    change = (blk > 0) & (blk_e != jnp.roll(blk_e, 1)) & used
    seg = jnp.cumsum(change.astype(I32)).astype(I32)
    later = (blk[None, :] > blk[:, None]) & (seg[None, :] > seg[:, None]) & used[None, :]
    nxt_blk = jnp.min(jnp.where(later, blk[None, :], n_blk), axis=1)
    nxt = jnp.where(nxt_blk < n_blk, blk_e[jnp.minimum(nxt_blk, n_blk - 1)], -1).astype(I32)
    rows = lambda b, be, nu, sg, nx: (jnp.minimum(b, nu[0] - 1), 0)
    grid_spec = pltpu.PrefetchScalarGridSpec(
        num_scalar_prefetch=4, grid=(n_blk,),
        in_specs=[pl.BlockSpec((R * nseg, LANES), rows),
                  pl.BlockSpec(memory_space=pl.ANY), pl.BlockSpec(memory_space=pl.ANY),
                  pl.BlockSpec(memory_space=pl.ANY)],
        out_specs=pl.BlockSpec((R * nseg, LANES), lambda b, be, nu, sg, nx: (b, 0)),
        scratch_shapes=[pltpu.VMEM((2, D, DE), F32), pltpu.VMEM((2, D, DE), F32), pltpu.VMEM((2, DE, D), F32),
                        pltpu.VMEM((D, DE), BF16), pltpu.VMEM((D, DE), BF16), pltpu.VMEM((DE, D), BF16),
                        pltpu.SemaphoreType.DMA((2, 3))])
    return pl.pallas_call(
        functools.partial(_moe_body, nseg=nseg, layer=layer), grid_spec=grid_spec,
        out_shape=jax.ShapeDtypeStruct(x_rows.shape, F32),
        compiler_params=_params("arbitrary"), name="moe")(blk_e, n_used, seg, nxt, x_rows, w1, w3, w2)


def _combine_body(tab_ref, tabn_ref, h_ref, rf_ref, lpos_ref, y_hbm, o_ref, ybuf, sem, *, tm, nseg, nrow):
    i = pl.program_id(0)
    slot = i % 2
    step = RUN_ROWS * nseg

    def start_chunks(tab, sl):
        def start(c, carry):
            pltpu.make_async_copy(
                y_hbm.at[pl.ds(pl.multiple_of(tab[0, 0, c] * nseg, nseg), step), :],
                ybuf.at[sl, pl.ds(pl.multiple_of(c * step, step), step), :], sem.at[sl]).start()
            return carry
        lax.fori_loop(0, tab[0, 0, LANES - 1], start, 0)

    @pl.when(i == 0)
    def _():
        ybuf[...] = jnp.zeros_like(ybuf)
        start_chunks(tab_ref, 0)

    @pl.when(i + 1 < pl.num_programs(0))
    def _():
        start_chunks(tabn_ref, 1 - slot)

    col = lax.broadcasted_iota(I32, (tm, nrow), 1)
    sel = jnp.zeros((tm, nrow), F32)
    for s in range(TOPK_IN):
        sel = sel + jnp.where(col == lpos_ref[:, s:s + 1], rf_ref[:, s:s + 1], 0.0)

    def wait(c, carry):
        pltpu.make_async_copy(y_hbm.at[pl.ds(0, step), :], ybuf.at[slot, pl.ds(0, step), :],
                              sem.at[slot]).wait()
        return carry

    lax.fori_loop(0, tab_ref[0, 0, LANES - 1], wait, 0)
    y = jnp.concatenate([ybuf[slot, pl.ds(k, nrow, stride=nseg), :] for k in range(nseg)], axis=-1)
    o_ref[...] = h_ref[...] + jnp.dot(sel.astype(BF16), y.astype(BF16), preferred_element_type=F32)


def _combine(table, h2d, rf, lpos, y_rows, tm):
    T, D = h2d.shape
    nseg = D // LANES
    nt = T // tm
    nrow = tm * TOPK_IN + N_EXPERTS * RUN_ROWS
    return pl.pallas_call(
        functools.partial(_combine_body, tm=tm, nseg=nseg, nrow=nrow), grid=(nt,),
        in_specs=[pl.BlockSpec((1, 1, LANES), lambda i: (i, 0, 0), memory_space=pltpu.SMEM),
                  pl.BlockSpec((1, 1, LANES), lambda i: (jnp.minimum(i + 1, nt - 1), 0, 0),
                               memory_space=pltpu.SMEM),
                  pl.BlockSpec((tm, D), lambda i: (i, 0)),
                  pl.BlockSpec((tm, LANES), lambda i: (i, 0)),
                  pl.BlockSpec((tm, TOPK_IN), lambda i: (i, 0)),
                  pl.BlockSpec(memory_space=pl.ANY)],
        out_specs=pl.BlockSpec((tm, D), lambda i: (i, 0)),
        out_shape=jax.ShapeDtypeStruct((T, D), F32),
        scratch_shapes=[pltpu.VMEM((2, nrow * nseg, LANES), F32), pltpu.SemaphoreType.DMA((2,))],
        compiler_params=_params("arbitrary"), name="combine")(table, table, h2d, rf, lpos, y_rows)


def _pad_lanes(a, lane0, rows=1):
    a = a.reshape(rows, -1)
    return jnp.pad(a, ((0, 0), (lane0, LANES - lane0 - a.shape[-1])))


def _layer(h2d, B, S, p):
    T, D = h2d.shape
    tile = lambda a, n: jnp.tile(a.reshape(1, -1), (1, n))

    offs = np.cumsum([0, GROUP_W, GROUP_W, GROUP_W, N_HEADS, GROUP_W, GROUP_W,
                      GROUP_W, GROUP_W, GROUP_W, N_HEADS, N_HEADS, GROUP_W, GROUP_W])
    seg = lambda k: p['w_in'][:, offs[k]:offs[k + 1]]
    w_all = jnp.concatenate([seg(0), seg(1), seg(2), seg(4), seg(5), seg(6), seg(7), seg(8), seg(11), seg(12),
                             seg(3), seg(9), seg(10), jnp.zeros((D, LANES - 3 * N_HEADS), F32)],
                            axis=1).astype(BF16)
    gate_prm = jnp.concatenate([_pad_lanes(p['fox_f_bias'], LANE_FOX), _pad_lanes(p['gdn_dt_bias'], LANE_DECAY),
                                _pad_lanes(p['gdn_a_log'], LANE_DECAY), jnp.zeros((SUBLANES - 3, LANES), F32)], axis=0)

    big, small = _inproj(h2d, p['attn_norm_g'].reshape(1, D), w_all, tm=min(1024, T))
    big3 = big.reshape(B, S, N_BIG_COLS * GROUP_W)
    ts = min(1024, S)
    gcol, gx = _gates(small.reshape(B, S, LANES), gate_prm, ts)

    tq = min(256, S)
    ya = _fox(big3, gcol, tile(p['fox_qn_g'], N_HEADS), tile(p['fox_kn_g'], N_HEADS),
              p['fox_out_g'].reshape(1, GROUP_W), tq)

    bst = jnp.repeat(p['gmlp_bs'].T, HEAD_DIM, axis=1)
    yb = _gmlp(big, p['gmlp_ln_g'].reshape(1, -1), p['gmlp_ln_b'].reshape(1, -1), p['gmlp_ws'], bst,
               p['gmlp_out_g'].reshape(1, -1), tm=min(1024, T))

    conv_w3 = p['gdn_conv_w'].reshape(CONV_K, 3, GROUP_W).transpose(1, 0, 2)
    nb = 4 if B % 4 == 0 else (2 if B % 2 == 0 else 1)
    yc = _gdn(big3, conv_w3, gx, tile(p['gdn_norm_g'], N_HEADS), min(1024 // nb, S), nb)

    wbd = jax.scipy.linalg.block_diag(*[p['pool_w'][g] for g in range(len(POOL_WINDOWS))]).astype(BF16)
    yd = _pool(big3, wbd, p['pool_scale'].reshape(1, -1), p['pool_out_g'].reshape(1, -1), ts)

    rw = _pad_lanes(jnp.concatenate([p['router_g_w'], p['router_e_w']], axis=1), 0, rows=D)
    rw = jnp.concatenate(_split(rw), axis=1)
    rb = _pad_lanes(jnp.concatenate([p['router_g_b'], p['router_e_b']]), 0)
    flat = lambda a: a.reshape(T, GROUP_W)
    tmd = min(256, T)
    h_new, hn_rows, ri, rf, cnt = _outproj(flat(ya), yb, flat(yc), flat(yd), h2d, p['w_out'].astype(BF16),
                                           p['ffn_norm_g'].reshape(1, D), rw, rb, tm=min(1024, T), tr=tmd)
    cnt_tile = cnt.reshape(T // tmd, SUBLANES, LANES)[:, 0, :N_EXPERTS]
    lpos, table, zfill, blk_e, n_used, n_blk = _dispatch_plan(
        ri[:, :TOPK_IN], ri[:, TOPK_IN:2 * TOPK_IN], cnt_tile, tmd)
    lpos_t = lpos.reshape(T // tmd, tmd, TOPK_IN).transpose(0, 2, 1)
    x_rows = _dispatch(zfill, table, lpos_t, hn_rows, n_blk * MOE_BLOCK, tmd)
    y_rows = _moe(blk_e, n_used, x_rows, p['moe_w1'], p['moe_w3'], p['moe_w2'], p['layer'])
    return _combine(table, h_new, rf, lpos, y_rows, tmd)


def kernel(x, attn_norm_g, w_in, w_out, fox_f_bias, fox_qn_g, fox_kn_g, fox_out_g, gmlp_ln_g, gmlp_ln_b, gmlp_ws, gmlp_bs, gmlp_out_g, gdn_conv_w, gdn_a_log, gdn_dt_bias, gdn_norm_g, pool_w, pool_scale, pool_out_g, ffn_norm_g, router_g_w, router_g_b, router_e_w, router_e_b, moe_w1, moe_w3, moe_w2):
    B, S, D = x.shape
    names = ('attn_norm_g', 'w_in', 'w_out', 'fox_f_bias', 'fox_qn_g', 'fox_kn_g', 'fox_out_g', 'gmlp_ln_g',
             'gmlp_ln_b', 'gmlp_ws', 'gmlp_bs', 'gmlp_out_g', 'gdn_conv_w', 'gdn_a_log', 'gdn_dt_bias',
             'gdn_norm_g', 'pool_w', 'pool_scale', 'pool_out_g', 'ffn_norm_g', 'router_g_w', 'router_g_b',
             'router_e_w', 'router_e_b', 'moe_w1', 'moe_w3', 'moe_w2')
    vals = (attn_norm_g, w_in, w_out, fox_f_bias, fox_qn_g, fox_kn_g, fox_out_g, gmlp_ln_g, gmlp_ln_b, gmlp_ws,
            gmlp_bs, gmlp_out_g, gdn_conv_w, gdn_a_log, gdn_dt_bias, gdn_norm_g, pool_w, pool_scale, pool_out_g,
            ffn_norm_g, router_g_w, router_g_b, router_e_w, router_e_b, moe_w1, moe_w3, moe_w2)
    h = x.reshape(B * S, D)
    stacked = ('moe_w1', 'moe_w3', 'moe_w2')
    for l in range(w_in.shape[0]):
        p = {n: (v if n in stacked else v[l]) for n, v in zip(names, vals)}
        p['layer'] = l
        h = _layer(h, B, S, p)
    return h.reshape(B, S, D)
```

```python
import functools

import jax
import jax.numpy as jnp
import numpy as np
from jax import lax
from jax.experimental import pallas as pl
from jax.experimental.pallas import tpu as pltpu

F32 = jnp.float32
BF16 = jnp.bfloat16
I32 = jnp.int32

EPS = 1e-6
HEAD_DIM = 64
GROUP_W = 256
N_HEADS = GROUP_W // HEAD_DIM
CHUNK = 64
GMLP_LEN = 128
CONV_K = 4
POOL_WINDOWS = (2, 4, 8, 16)
N_GROUPS = 4
EXPERTS_PER_GROUP = 8
N_EXPERTS = N_GROUPS * EXPERTS_PER_GROUP
TOPK_IN = 2
MOE_BLOCK = 256
FOX_VT_ROWS = HEAD_DIM + 16
FOX_BOUND_LOG2 = 40.0
FOX_SKIP_LOG2 = 160.0
RUN_ROWS = 16
LANES = 128
SUBLANES = 8
VMEM_LIMIT = 56 * 1024 * 1024

COL_FQ, COL_FK, COL_FV, COL_GU, COL_GV, COL_DQ, COL_DK, COL_DV, COL_DG, COL_PZ = range(10)
N_BIG_COLS = 10
LANE_FOX, LANE_DECAY, LANE_BETA = 0, 4, 8


def _params(*sem):
    return pltpu.CompilerParams(dimension_semantics=sem, vmem_limit_bytes=VMEM_LIMIT)


def _head_ones():
    r = lax.broadcasted_iota(I32, (GROUP_W, GROUP_W), 0) // HEAD_DIM
    c = lax.broadcasted_iota(I32, (GROUP_W, GROUP_W), 1) // HEAD_DIM
    return (r == c).astype(BF16)


def _head_sums(x, ones_bd):
    hi = x.astype(BF16)
    lo = (x - hi.astype(F32)).astype(BF16)
    return (jnp.dot(hi, ones_bd, preferred_element_type=F32)
            + jnp.dot(lo, ones_bd, preferred_element_type=F32))


def _rms(x, g):
    return x * lax.rsqrt(jnp.mean(x * x, axis=-1, keepdims=True) + EPS) * g


def _mm(a, b):
    return jnp.dot(a.astype(BF16), b.astype(BF16), preferred_element_type=F32)


def _mm_nt(a, b):
    return lax.dot_general(a.astype(BF16), b.astype(BF16), (((1,), (1,)), ((), ())),
                           preferred_element_type=F32)


def _mm_tn(a, b):
    return lax.dot_general(a.astype(BF16), b.astype(BF16), (((0,), (0,)), ((), ())),
                           preferred_element_type=F32)


def _split(a):
    hi = a.astype(BF16)
    return hi, (a - hi.astype(F32)).astype(BF16)


def _mm3(a, b):
    ah, al = _split(a)
    bh, bl = _split(b)
    d = functools.partial(jnp.dot, preferred_element_type=F32)
    return d(ah, bh) + (d(ah, bl) + d(al, bh))


def _inproj_body(x_ref, g_ref, w_ref, big_ref, small_ref):
    xn = _rms(x_ref[...], g_ref[...]).astype(BF16)
    nb = big_ref.shape[1]
    big_ref[...] = jnp.dot(xn, w_ref[:, :nb], preferred_element_type=F32).astype(big_ref.dtype)
    small_ref[...] = jnp.dot(xn, w_ref[:, nb:], preferred_element_type=F32)


def _inproj(x2d, g, w, tm):
    T, D = x2d.shape
    nb = w.shape[1] - LANES
    return pl.pallas_call(
        _inproj_body, grid=(T // tm,),
        in_specs=[pl.BlockSpec((tm, D), lambda i: (i, 0)),
                  pl.BlockSpec((1, D), lambda i: (0, 0)),
                  pl.BlockSpec((D, nb + LANES), lambda i: (0, 0))],
        out_specs=[pl.BlockSpec((tm, nb), lambda i: (i, 0)),
                   pl.BlockSpec((tm, LANES), lambda i: (i, 0))],
        out_shape=[jax.ShapeDtypeStruct((T, nb), BF16), jax.ShapeDtypeStruct((T, LANES), F32)],
        compiler_params=_params("parallel"), name="inproj")(x2d, g, w)


def _gates_body(sm_ref, p_ref, col_ref, exp_ref, carry_ref, *, ts):
    @pl.when(pl.program_id(1) == 0)
    def _():
        carry_ref[...] = jnp.zeros_like(carry_ref)

    x = sm_ref[...]
    lane = lax.broadcasted_iota(I32, (ts, LANES), 1)
    is_fox = lane < LANE_DECAY
    is_dec = (lane >= LANE_DECAY) & (lane < LANE_BETA)
    is_beta = (lane >= LANE_BETA) & (lane < LANE_BETA + N_HEADS)
    logf = jax.nn.log_sigmoid(x + p_ref[0:1, :])
    g = -jnp.exp(p_ref[2:3, :]) * jax.nn.softplus(x + p_ref[1:2, :])
    beta = jax.nn.sigmoid(x)
    tri = (lax.broadcasted_iota(I32, (CHUNK, CHUNK), 0) >= lax.broadcasted_iota(I32, (CHUNK, CHUNK), 1)).astype(BF16)
    vals = jnp.where(is_fox, logf, jnp.where(is_dec, g, 0.0))
    hi = vals.astype(BF16)
    mid = (vals - hi.astype(F32)).astype(BF16)
    lo = (vals - hi.astype(F32) - mid.astype(F32)).astype(BF16)
    parts = jnp.concatenate([hi, mid, lo], axis=1)
    running = carry_ref[...]
    cf_blocks, cg_blocks = [], []
    for n in range(ts // CHUNK):
        t = jnp.dot(tri, parts[n * CHUNK:(n + 1) * CHUNK, :], preferred_element_type=F32)
        within = t[:, :LANES] + (t[:, LANES:2 * LANES] + t[:, 2 * LANES:])
        cg_blocks.append(within)
        cf_blocks.append(within + running)
        running = cf_blocks[-1][CHUNK - 1:CHUNK, :]
    cf = jnp.concatenate(cf_blocks, axis=0)
    cg = jnp.concatenate(cg_blocks, axis=0)
    carry_ref[...] = running
    out = jnp.where(is_fox, cf, jnp.where(is_dec, cg, jnp.where(is_beta, beta, 0.0)))
    col_ref[...] = out
    o_hi = out.astype(BF16)
    o_mid = (out - o_hi.astype(F32)).astype(BF16)
    o_lo = (out - o_hi.astype(F32) - o_mid.astype(F32)).astype(BF16)
    src = lax.broadcasted_iota(I32, (3 * LANES, 2 * GROUP_W), 0) % LANES
    dst = lax.broadcasted_iota(I32, (3 * LANES, 2 * GROUP_W), 1)
    want = jnp.where(dst < GROUP_W, LANE_DECAY, LANE_BETA) + (dst % GROUP_W) // HEAD_DIM
    exp_ref[...] = jnp.dot(jnp.concatenate([o_hi, o_mid, o_lo], axis=1), (src == want).astype(BF16),
                           preferred_element_type=F32)


def _gates(small3, prm, ts):
    B, S, _ = small3.shape
    return pl.pallas_call(
        functools.partial(_gates_body, ts=ts), grid=(B, S // ts),
        in_specs=[pl.BlockSpec((None, ts, LANES), lambda b, j: (b, j, 0)),
                  pl.BlockSpec((SUBLANES, LANES), lambda b, j: (0, 0))],
        out_specs=[pl.BlockSpec((None, ts, LANES), lambda b, j: (b, j, 0)),
                   pl.BlockSpec((None, ts, 2 * GROUP_W), lambda b, j: (b, j, 0))],
        out_shape=[jax.ShapeDtypeStruct((B, S, LANES), F32),
                   jax.ShapeDtypeStruct((B, S, 2 * GROUP_W), F32)],
        scratch_shapes=[pltpu.VMEM((1, LANES), F32)],
        compiler_params=_params("parallel", "arbitrary"), name="gates")(small3, prm)


def _fox_placer(lane0):
    src = lax.broadcasted_iota(I32, (3 * LANES, N_HEADS * LANES), 0)
    dst = lax.broadcasted_iota(I32, (3 * LANES, N_HEADS * LANES), 1)
    return ((src % LANES == LANE_FOX + dst // LANES) & (dst % LANES == lane0 + src // LANES)).astype(BF16)


def _split3_place(vals, placer):
    hi = vals.astype(BF16)
    mid = (vals - hi.astype(F32)).astype(BF16)
    lo = (vals - hi.astype(F32) - mid.astype(F32)).astype(BF16)
    return jnp.dot(jnp.concatenate([hi, mid, lo], axis=1), placer, preferred_element_type=F32)


def _fox_body(jstart_ref, q_ref, k_ref, v_ref, ccol_ref, qg_ref, kg_ref, og_ref, bound_ref, o_ref,
              kn_scr, vt_scr, q_scr, m_scr, l_scr, acc_scr, s_scr, place_scr, *, tq, nk):
    b = pl.program_id(0)
    i = pl.program_id(1)
    ones_bd = _head_ones()
    lane = lax.broadcasted_iota(I32, (tq, LANES), 1)
    log2e = 1.0 / np.log(2.0)
    c_lane, r_lane = HEAD_DIM, HEAD_DIM + 3

    def head_norm(x, g):
        ss = _head_sums(x * x, ones_bd)
        return x * lax.rsqrt(ss * (1.0 / HEAD_DIM) + EPS) * g

    def head_tile(x, h, extra):
        pair = x[:, (h // 2) * LANES:(h // 2 + 1) * LANES]
        if h % 2:
            pair = pltpu.roll(pair, HEAD_DIM, 1)
        return jnp.where(lane < HEAD_DIM, pair, extra).astype(BF16)

    @pl.when(i == 0)
    def _():
        k_ones = jnp.where((lane >= r_lane) & (lane < r_lane + 3), 1.0, 0.0)
        vt_tail = (lax.broadcasted_iota(I32, (FOX_VT_ROWS - HEAD_DIM, tq), 0) == 0).astype(BF16)
        place_scr[0] = _fox_placer(c_lane)
        place_scr[1] = _fox_placer(r_lane)
        for c in range(nk):
            rows = slice(c * tq, (c + 1) * tq)
            kc = head_norm(k_ref[rows, :].astype(F32), kg_ref[...])
            vt = v_ref[rows, :].astype(F32).T.astype(BF16)
            extras = _split3_place(ccol_ref[rows, :] * (-log2e), place_scr[0])
            for h in range(N_HEADS):
                extra = extras[:, h * LANES:(h + 1) * LANES] + k_ones
                kn_scr[h, c] = head_tile(kc, h, extra)
                vt_scr[h, c] = jnp.concatenate([vt[h * HEAD_DIM:(h + 1) * HEAD_DIM, :], vt_tail], axis=0)

    qn = head_norm(q_ref[...].astype(F32), qg_ref[...]) * (HEAD_DIM ** -0.5 * log2e)
    bound = bound_ref[...]
    bounded = jnp.max(bound) <= FOX_BOUND_LOG2
    c_i = ccol_ref[pl.ds(pl.multiple_of(i * tq, tq), tq), :] * log2e
    q_ones = jnp.where((lane >= c_lane) & (lane < c_lane + 3), 1.0, 0.0)
    neg_r = _split3_place(jnp.where(bounded, c_i - bound, 0.0), place_scr[1])
    for h in range(N_HEADS):
        q_scr[h] = head_tile(qn, h, q_ones + neg_r[:, h * LANES:(h + 1) * LANES])
    acc_scr[...] = jnp.zeros_like(acc_scr)
    causal = (lax.broadcasted_iota(I32, (tq, tq), 0) <= lax.broadcasted_iota(I32, (tq, tq), 1))

    heads = range(N_HEADS)

    def scores(j):
        return [lax.dot_general(kn_scr[h, j], q_scr[h], (((1,), (1,)), ((), ())),
                                preferred_element_type=F32) for h in heads]

    def stash(s):
        for h in heads:
            s_scr[h] = s[h]

    def absorb_general(j, masked):
        p, alpha = [], []
        for h in heads:
            s = s_scr[h]
            if masked:
                s = jnp.where(causal, s, -jnp.inf)
            m_old = m_scr[h]
            m_new = jnp.maximum(m_old, jnp.max(s, axis=0, keepdims=True))
            alpha.append(jnp.exp2(m_old - m_new))
            ph = jnp.exp2(s - m_new)
            l_scr[h] = alpha[h] * l_scr[h] + jnp.sum(ph, axis=0, keepdims=True)
            m_scr[h] = m_new
            p.append(ph.astype(BF16))
        pv = [jnp.dot(vt_scr[h, j], p[h], preferred_element_type=F32) for h in heads]
        for h in heads:
            acc_scr[h] = alpha[h] * acc_scr[h] + pv[h]

    def absorb_bounded(j, masked):
        p = []
        for h in heads:
            s = s_scr[h]
            if masked:
                s = jnp.where(causal, s, -jnp.inf)
            p.append(jnp.exp2(s).astype(BF16))
        pv = [jnp.dot(vt_scr[h, j], p[h], preferred_element_type=F32) for h in heads]
        for h in heads:
            acc_scr[h] += pv[h]

    def run(absorb, j0):
        stash(scores(j0))

        def body(j, c):
            s_next = scores(j + 1)
            absorb(j, False)
            stash(s_next)
            return c

        lax.fori_loop(j0, i, body, 0)
        absorb(i, True)

    @pl.when(bounded)
    def _():
        run(absorb_bounded, jstart_ref[b * nk + i])

    @pl.when(jnp.logical_not(bounded))
    def _():
        m_scr[...] = jnp.full_like(m_scr, -jnp.inf)
        l_scr[...] = jnp.zeros_like(l_scr)
        run(absorb_general, 0)
        for h in heads:
            acc_scr[h, HEAD_DIM:HEAD_DIM + 1, :] = l_scr[h]

    o_t = jnp.concatenate([acc_scr[h, :HEAD_DIM, :] * (1.0 / acc_scr[h, HEAD_DIM:HEAD_DIM + 1, :]) for h in heads],
                          axis=0)
    o_ref[...] = _rms(o_t.T, og_ref[...])


def _fox_first_block(gcol, tq):
    B, S, _ = gcol.shape
    nk = S // tq
    c2 = gcol[:, :, LANE_FOX:LANE_FOX + N_HEADS] * (1.0 / np.log(2.0))
    first = c2[:, 0::tq, :]
    last = c2[:, tq - 1::tq, :]
    dead = (first[:, :, None, :] - last[:, None, :, :]) < -FOX_SKIP_LOG2
    dead = dead & (jnp.arange(nk)[None, :, None, None] > jnp.arange(nk)[None, None, :, None])
    return jnp.min(jnp.sum(dead, axis=2), axis=-1).astype(I32).reshape(B * nk)


def _fox(big3, gcol, qg, kg, og, tq):
    B, S, _ = big3.shape
    nk = S // tq
    row = pl.BlockSpec((1, GROUP_W), lambda b, i, js: (0, 0))
    grid_spec = pltpu.PrefetchScalarGridSpec(
        num_scalar_prefetch=1, grid=(B, nk),
        in_specs=[pl.BlockSpec((None, tq, GROUP_W), lambda b, i, js: (b, i, COL_FQ)),
                  pl.BlockSpec((None, S, GROUP_W), lambda b, i, js: (b, 0, COL_FK)),
                  pl.BlockSpec((None, S, GROUP_W), lambda b, i, js: (b, 0, COL_FV)),
                  pl.BlockSpec((None, S, LANES), lambda b, i, js: (b, 0, 0)),
                  row, row, row, pl.BlockSpec((1, LANES), lambda b, i, js: (0, 0))],
        out_specs=pl.BlockSpec((None, tq, GROUP_W), lambda b, i, js: (b, i, 0)),
        scratch_shapes=[pltpu.VMEM((N_HEADS, nk, tq, LANES), BF16),
                        pltpu.VMEM((N_HEADS, nk, FOX_VT_ROWS, tq), BF16),
                        pltpu.VMEM((N_HEADS, tq, LANES), BF16),
                        pltpu.VMEM((N_HEADS, 1, tq), F32),
                        pltpu.VMEM((N_HEADS, 1, tq), F32),
                        pltpu.VMEM((N_HEADS, FOX_VT_ROWS, tq), F32),
                        pltpu.VMEM((N_HEADS, tq, tq), F32),
                        pltpu.VMEM((2, 3 * LANES, N_HEADS * LANES), BF16)])
    gmax = lambda g: jnp.max(jnp.abs(g.reshape(N_HEADS, HEAD_DIM)), axis=1)
    bound = _pad_lanes(gmax(qg) * gmax(kg) * (HEAD_DIM * HEAD_DIM ** -0.5 / np.log(2.0) * 1.02), LANE_FOX)
    return pl.pallas_call(
        functools.partial(_fox_body, tq=tq, nk=nk), grid_spec=grid_spec,
        out_shape=jax.ShapeDtypeStruct((B, S, GROUP_W), F32),
        compiler_params=_params("parallel", "arbitrary"), name="fox")(
            _fox_first_block(gcol, tq), big3, big3, big3, gcol, qg, kg, og, bound)


def _gelu(x):
    return 0.5 * x * (1.0 + lax.erf(x * (2.0 ** -0.5)))


def _gmlp_body(u_ref, v_ref, lg_ref, lb_ref, ws_ref, bst_ref, og_ref, o_ref, *, nwin):
    L = GMLP_LEN
    r = lax.broadcasted_iota(I32, (L, L), 0) // CHUNK
    c = lax.broadcasted_iota(I32, (L, L), 1) // CHUNK
    mask = r >= c
    ws = [jnp.where(mask, ws_ref[h], 0.0).astype(BF16) for h in range(N_HEADS)]
    for n in range(nwin):
        u = _gelu(u_ref[n * L:(n + 1) * L, :].astype(F32))
        v = _gelu(v_ref[n * L:(n + 1) * L, :].astype(F32))
        mu = jnp.mean(v, axis=-1, keepdims=True)
        vc = v - mu
        var = jnp.mean(vc * vc, axis=-1, keepdims=True)
        vn = (vc * lax.rsqrt(var + EPS) * lg_ref[...] + lb_ref[...]).astype(BF16)
        mixed = jnp.concatenate(
            [jnp.dot(ws[h], vn[:, h * HEAD_DIM:(h + 1) * HEAD_DIM], preferred_element_type=F32)
             for h in range(N_HEADS)], axis=-1) + bst_ref[...]
        o_ref[n * L:(n + 1) * L, :] = _rms(u * mixed, og_ref[...])


def _gmlp(big, lg, lb, ws, bst, og, tm):
    T = big.shape[0]
    row = pl.BlockSpec((1, GROUP_W), lambda i: (0, 0))
    return pl.pallas_call(
        functools.partial(_gmlp_body, nwin=tm // GMLP_LEN), grid=(T // tm,),
        in_specs=[pl.BlockSpec((tm, GROUP_W), lambda i: (i, COL_GU)),
                  pl.BlockSpec((tm, GROUP_W), lambda i: (i, COL_GV)),
                  row, row,
                  pl.BlockSpec((N_HEADS, GMLP_LEN, GMLP_LEN), lambda i: (0, 0, 0)),
                  pl.BlockSpec((GMLP_LEN, GROUP_W), lambda i: (0, 0)),
                  row],
        out_specs=pl.BlockSpec((tm, GROUP_W), lambda i: (i, 0)),
        out_shape=jax.ShapeDtypeStruct((T, GROUP_W), F32),
        compiler_params=_params("parallel"), name="gmlp")(big, big, lg, lb, ws, bst, og)


def _gdn_body(q_ref, k_ref, v_ref, w_ref, gate_ref, gx_ref, ng_ref, o_ref,
              s_scr, u_scr, wq_scr, a_scr, kd_scr, dl_scr, t_scr, p_scr, rhs_scr, halo_scr, qkv_scr, *, nchunk, nb):
    C = CHUNK
    ts = nchunk * C
    first = pl.program_id(1) == 0

    @pl.when(first)
    def _():
        s_scr[...] = jnp.zeros_like(s_scr)

    ones_bd = _head_ones()
    for bb, a in [(bb, a) for bb in range(nb) for a in range(3)]:
        w = w_ref[a]
        x = (q_ref, k_ref, v_ref)[a][bb].astype(F32)
        xx = jnp.concatenate([jnp.where(first, 0.0, halo_scr[bb * 3 + a]), x], axis=0)
        halo_scr[bb * 3 + a] = x[ts - SUBLANES:, :]
        y = w[CONV_K - 1:CONV_K, :] * x
        for j in range(CONV_K - 1):
            y = y + w[j:j + 1, :] * pltpu.roll(xx, CONV_K - 1 - j, 0)[SUBLANES:, :]
        y = y * jax.nn.sigmoid(y)
        if a < 2:
            y = y * lax.rsqrt(_head_sums(y * y, ones_bd) + EPS)
        qkv_scr[bb * 3 + a] = y * (HEAD_DIM ** -0.5) if a == 0 else y

    W = GROUP_W
    pos = lax.broadcasted_iota(I32, (C, W), 1) % HEAD_DIM
    r = lax.broadcasted_iota(I32, (C, W), 0)
    tri, strict, eye = r >= pos, r > pos, r == pos
    same_head = (lax.broadcasted_iota(I32, (W, W), 0) // HEAD_DIM
                 == lax.broadcasted_iota(I32, (W, W), 1) // HEAD_DIM)
    mmb = functools.partial(jnp.dot, preferred_element_type=F32)

    def block_diag(x):
        return jnp.where(same_head, jnp.concatenate([x.astype(BF16)] * N_HEADS, axis=0), 0.0)

    items = nb * nchunk
    for n in range(items):
        bb = n // nchunk
        rows = slice((n % nchunk) * C, (n % nchunk + 1) * C)
        q, k, v = (qkv_scr[bb * 3 + a, rows, :] for a in range(3))
        gc = gx_ref[bb, rows, :W]
        beta = gx_ref[bb, rows, W:]
        gr = jnp.sum(jnp.where(eye, gc, 0.0), axis=0, keepdims=True)
        decay = jnp.exp(jnp.where(tri, gc - gr, -jnp.inf))
        kb = k * beta
        kk = lax.dot_general(jnp.concatenate([kb, q], axis=0).astype(BF16), block_diag(k),
                             (((1,), (1,)), ((), ())), preferred_element_type=F32)
        x = jnp.where(strict, -(kk[:C] * decay), 0.0)
        t_scr[n] = jnp.where(eye, 1.0, 0.0) + x
        p_scr[n] = x.astype(BF16)
        eg = jnp.exp(gc)
        g_last = gc[C - 1:C, :]
        rhs_scr[n, 0] = block_diag(v * beta)
        rhs_scr[n, 1] = block_diag(kb * eg)
        wq_scr[n, C:, :] = (q * eg).astype(BF16)
        a_scr[n] = jnp.where(tri, kk[C:] * decay, 0.0).astype(BF16)
        kd_scr[n] = (k * jnp.exp(g_last - gc)).astype(BF16)
        dl_scr[n] = jnp.exp(g_last)
    for level in range(1, 6):
        for n in range(items):
            p = p_scr[n]
            p_scr[n] = mmb(p, block_diag(p)).astype(BF16)
        for n in range(items):
            t = t_scr[n]
            t_scr[n] = t + mmb(t.astype(BF16), block_diag(p_scr[n]))
    for n in range(items):
        t = t_scr[n].astype(BF16)
        u_scr[n] = mmb(t, rhs_scr[n, 0])
        wq_scr[n, :C, :] = mmb(t, rhs_scr[n, 1]).astype(BF16)

    ones_bd = _head_ones()
    batch = range(nb)
    state = [s_scr[bb] for bb in batch]
    for c in range(nchunk):
        rows = slice(c * C, (c + 1) * C)
        it = [bb * nchunk + c for bb in batch]
        ws = [mmb(wq_scr[it[bb]], state[bb].astype(BF16)) for bb in batch]
        vb = [(u_scr[it[bb]] - ws[bb][:C]).astype(BF16) for bb in batch]
        o = [ws[bb][C:] + mmb(a_scr[it[bb]], block_diag(vb[bb])) for bb in batch]
        kv = [lax.dot_general(kd_scr[it[bb]], vb[bb], (((0,), (0,)), ((), ())), preferred_element_type=F32)
              for bb in batch]
        state = [state[bb] * dl_scr[it[bb]] + jnp.where(same_head, kv[bb], 0.0) for bb in batch]
        for bb in batch:
            gate = gate_ref[bb, rows, :].astype(F32)
            y = o[bb] * lax.rsqrt(_head_sums(o[bb] * o[bb], ones_bd) * (1.0 / HEAD_DIM) + EPS) * ng_ref[...]
            o_ref[bb, rows, :] = y * (gate * jax.nn.sigmoid(gate))
    for bb in batch:
        s_scr[bb] = state[bb]


def _gdn(big3, conv_w3, gx, ng, ts, nb):
    B, S, _ = big3.shape
    nchunk = ts // CHUNK
    col = lambda c: pl.BlockSpec((nb, ts, GROUP_W), lambda b, i: (b, i, c))
    W = GROUP_W
    items = nb * nchunk
    return pl.pallas_call(
        functools.partial(_gdn_body, nchunk=nchunk, nb=nb), grid=(B // nb, S // ts),
        in_specs=[col(COL_DQ), col(COL_DK), col(COL_DV),
                  pl.BlockSpec((3, CONV_K, W), lambda b, i: (0, 0, 0)),
                  col(COL_DG),
                  pl.BlockSpec((nb, ts, 2 * W), lambda b, i: (b, i, 0)),
                  pl.BlockSpec((1, W), lambda b, i: (0, 0))],
        out_specs=col(0), out_shape=jax.ShapeDtypeStruct((B, S, W), F32),
        scratch_shapes=[pltpu.VMEM((nb, W, W), F32),
                        pltpu.VMEM((items, CHUNK, W), F32),
                        pltpu.VMEM((items, 2 * CHUNK, W), BF16),
                        pltpu.VMEM((items, CHUNK, W), BF16),
                        pltpu.VMEM((items, CHUNK, W), BF16),
                        pltpu.VMEM((items, 1, W), F32),
                        pltpu.VMEM((items, CHUNK, W), F32),
                        pltpu.VMEM((items, CHUNK, W), BF16),
                        pltpu.VMEM((items, 2, W, W), BF16),
                        pltpu.VMEM((nb * 3, SUBLANES, W), F32),
                        pltpu.VMEM((nb * 3, ts, W), F32)],
        compiler_params=_params("parallel", "arbitrary"), name="gdn")(
            big3, big3, big3, conv_w3, big3, gx, ng)


def _pool_body(z_ref, halo_ref, w_ref, sc_ref, og_ref, o_ref, *, ts):
    i = pl.program_id(1)
    hr = 2 * SUBLANES
    z = z_ref[...].astype(F32)
    halo = jnp.where(i == 0, 0.0, halo_ref[...].astype(F32))
    s1 = jnp.concatenate([halo, z], axis=0)
    s2 = s1 + pltpu.roll(s1, 1, 0)
    s4 = s2 + pltpu.roll(s2, 2, 0)
    s8 = s4 + pltpu.roll(s4, 4, 0)
    s16 = s8 + pltpu.roll(s8, 8, 0)
    grp = lax.broadcasted_iota(I32, (ts, GROUP_W), 1) // (GROUP_W // len(POOL_WINDOWS))
    t = lax.broadcasted_iota(I32, (ts, GROUP_W), 0) + i * ts
    total = jnp.where(grp == 0, s2[hr:], jnp.where(grp == 1, s4[hr:], jnp.where(grp == 2, s8[hr:], s16[hr:])))
    win = jnp.where(grp == 0, POOL_WINDOWS[0], jnp.where(grp == 1, POOL_WINDOWS[1],
                    jnp.where(grp == 2, POOL_WINDOWS[2], POOL_WINDOWS[3])))
    pooled = total / jnp.minimum(t + 1, win).astype(F32)
    y = _mm(pooled - z, w_ref[...]) * sc_ref[...]
    o_ref[...] = _rms(y, og_ref[...])


def _pool(big3, wbd, sc, og, ts):
    B, S, _ = big3.shape
    hr = 2 * SUBLANES
    hb = ts // hr
    row = pl.BlockSpec((1, GROUP_W), lambda b, i: (0, 0))
    return pl.pallas_call(
        functools.partial(_pool_body, ts=ts), grid=(B, S // ts),
        in_specs=[pl.BlockSpec((None, ts, GROUP_W), lambda b, i: (b, i, COL_PZ)),
                  pl.BlockSpec((None, hr, GROUP_W), lambda b, i: (b, jnp.maximum(i * hb - 1, 0), COL_PZ)),
                  pl.BlockSpec((GROUP_W, GROUP_W), lambda b, i: (0, 0)), row, row],
        out_specs=pl.BlockSpec((None, ts, GROUP_W), lambda b, i: (b, i, 0)),
        out_shape=jax.ShapeDtypeStruct((B, S, GROUP_W), F32),
        compiler_params=_params("parallel", "parallel"), name="pool")(big3, big3, wbd, sc, og)


def _outproj_body(ya_ref, yb_ref, yc_ref, yd_ref, h_ref, wo_ref, g_ref, rw_ref, rb_ref,
                  hnew_ref, hn_ref, ri_ref, rf_ref, cnt_ref, *, tm, tr):
    y = jnp.concatenate([ya_ref[...], yb_ref[...], yc_ref[...], yd_ref[...]], axis=-1).astype(BF16)
    h_new = h_ref[...] + jnp.dot(y, wo_ref[...], preferred_element_type=F32)
    hnew_ref[...] = h_new
    hn = _rms(h_new, g_ref[...])
    hn_hi, hn_lo = _split(hn)
    hn_ref[...] = hn_hi
    t = jnp.dot(hn_hi, rw_ref[...], preferred_element_type=F32)
    logits = (t[:, :LANES] + t[:, LANES:]
              + jnp.dot(hn_lo, rw_ref[:, :LANES], preferred_element_type=F32)) + rb_ref[...]
    lane = lax.broadcasted_iota(I32, (tm, LANES), 1)
    neg = -jnp.inf
    big_lane = LANES

    def masked_top(vals, mask):
        v = jnp.where(mask, vals, neg)
        mx = jnp.max(v, axis=-1, keepdims=True)
        idx = jnp.min(jnp.where(mask & (v == mx), lane, big_lane), axis=-1, keepdims=True)
        return v, mx, idx

    gmask = lane < N_GROUPS
    gv, gmx, gidx = masked_top(logits, gmask)
    g_top = 1.0 / jnp.sum(jnp.where(gmask, jnp.exp(gv - gmx), 0.0), axis=-1, keepdims=True)
    lo = N_GROUPS + gidx * EXPERTS_PER_GROUP
    emask = (lane >= lo) & (lane < lo + EXPERTS_PER_GROUP)
    ev, emx, eidx1 = masked_top(logits, emask)
    esum = jnp.sum(jnp.where(emask, jnp.exp(ev - emx), 0.0), axis=-1, keepdims=True)
    p1 = 1.0 / esum
    _, emx2, eidx2 = masked_top(logits, emask & (lane != eidx1))
    p2 = jnp.exp(emx2 - emx) / esum
    denom = p1 + p2
    rf_ref[...] = jnp.where(lane == 0, g_top * p1 / denom, jnp.where(lane == 1, g_top * p2 / denom, 0.0))

    expert = [eidx1 - N_GROUPS, eidx2 - N_GROUPS]
    hot = [lane == e for e in expert]
    m = (hot[0] | hot[1]).astype(BF16)
    below = (lax.broadcasted_iota(I32, (tr, tr), 0) > lax.broadcasted_iota(I32, (tr, tr), 1)).astype(BF16)
    before = jnp.concatenate(
        [jnp.dot(below, m[k * tr:(k + 1) * tr, :], preferred_element_type=F32) for k in range(tm // tr)], axis=0)
    rank = [jnp.sum(jnp.where(hot[s], before, 0.0), axis=-1, keepdims=True).astype(I32) for s in range(TOPK_IN)]
    out = jnp.zeros((tm, LANES), I32)
    for s in range(TOPK_IN):
        out = jnp.where(lane == s, expert[s], jnp.where(lane == TOPK_IN + s, rank[s], out))
    ri_ref[...] = out
    for k in range(tm // tr):
        last = (k + 1) * tr - 1
        total = before[last:last + 1, :] + m[last:last + 1, :].astype(F32)
        cnt_ref[k * SUBLANES:(k + 1) * SUBLANES, :] = jnp.broadcast_to(total, (SUBLANES, LANES)).astype(I32)


def _outproj(ya, yb, yc, yd, h2d, wo, g, rw, rb, tm, tr):
    T, D = h2d.shape
    yblk = pl.BlockSpec((tm, GROUP_W), lambda i: (i, 0))
    cnt_rows = tm // tr * SUBLANES
    return pl.pallas_call(
        functools.partial(_outproj_body, tm=tm, tr=tr), grid=(T // tm,),
        in_specs=[yblk, yblk, yblk, yblk,
                  pl.BlockSpec((tm, D), lambda i: (i, 0)),
                  pl.BlockSpec((D, D), lambda i: (0, 0)),
                  pl.BlockSpec((1, D), lambda i: (0, 0)),
                  pl.BlockSpec((D, 2 * LANES), lambda i: (0, 0)),
                  pl.BlockSpec((1, LANES), lambda i: (0, 0))],
        out_specs=[pl.BlockSpec((tm, D), lambda i: (i, 0)),
                   pl.BlockSpec((tm, D), lambda i: (i, 0)),
                   pl.BlockSpec((tm, LANES), lambda i: (i, 0)),
                   pl.BlockSpec((tm, LANES), lambda i: (i, 0)),
                   pl.BlockSpec((cnt_rows, LANES), lambda i: (i, 0))],
        out_shape=[jax.ShapeDtypeStruct((T, D), F32),
                   jax.ShapeDtypeStruct((T, D), BF16),
                   jax.ShapeDtypeStruct((T, LANES), I32),
                   jax.ShapeDtypeStruct((T, LANES), F32),
                   jax.ShapeDtypeStruct((T // tr * SUBLANES, LANES), I32)],
        compiler_params=_params("parallel"), name="outproj")(ya, yb, yc, yd, h2d, wo, g, rw, rb)


def _dispatch_plan(expert, lrank, cnt_tile, tm):
    T = expert.shape[0]
    nt = T // tm
    counts = jnp.sum(cnt_tile, axis=0)
    padded = (counts + RUN_ROWS + MOE_BLOCK - 1) // MOE_BLOCK * MOE_BLOCK
    pad_end = jnp.cumsum(padded)
    pad_start = pad_end - padded
    gstart = pad_start[None, :] + jnp.cumsum(cnt_tile, axis=0) - cnt_tile
    nchunk = (cnt_tile + RUN_ROWS - 1) // RUN_ROWS
    chunk_end = jnp.cumsum(nchunk, axis=1)
    lstart = (chunk_end - nchunk) * RUN_ROWS
    onehot = expert[:, :, None] == jnp.arange(N_EXPERTS, dtype=I32)[None, None, :]
    pick = lambda tab: jnp.sum(jnp.where(onehot, jnp.repeat(tab, tm, axis=0)[:, None, :], 0), axis=-1)
    lpos =(lrank + pick(lstart)).astype(I32)
    max_chunks = tm * TOPK_IN // RUN_ROWS + N_EXPERTS
    c = jnp.arange(max_chunks, dtype=I32)
    ce = jnp.minimum(jnp.sum(chunk_end[:, None, :] <= c[None, :, None], axis=-1), N_EXPERTS - 1)
    ce_hot = ce[:, :, None] == jnp.arange(N_EXPERTS, dtype=I32)[None, None, :]
    take = lambda tab: jnp.sum(jnp.where(ce_hot, tab[:, None, :], 0), axis=-1)
    chunk_row = take(gstart) + (c[None, :] - take(chunk_end - nchunk)) * RUN_ROWS
    chunk_row = jnp.where(c[None, :] < chunk_end[:, -1:], chunk_row, 0)
    table = jnp.concatenate([chunk_row.astype(I32), jnp.zeros((nt, LANES - 1 - max_chunks), I32),
                             chunk_end[:, -1:].astype(I32)], axis=1).reshape(nt, 1, LANES)
    n_blk = -(-(T * TOPK_IN + N_EXPERTS * RUN_ROWS) // MOE_BLOCK) + N_EXPERTS + 1
    blk_start = jnp.arange(n_blk, dtype=I32) * MOE_BLOCK
    blk_e = jnp.minimum(jnp.sum(pad_end[None, :] <= blk_start[:, None], axis=-1), N_EXPERTS - 1).astype(I32)
    n_used = (pad_end[-1] // MOE_BLOCK).astype(I32).reshape(1)
    zero_start = jnp.concatenate([pad_start + counts, pad_end[-1:]])
    zero_end = jnp.concatenate([pad_end, jnp.full((1,), n_blk * MOE_BLOCK, pad_end.dtype)])
    nz = N_EXPERTS + 1
    zfill = jnp.concatenate([zero_start, (zero_end - zero_start) // RUN_ROWS, zero_end - RUN_ROWS,
                             jnp.zeros((LANES - 3 * nz,), zero_start.dtype)]).astype(I32).reshape(1, LANES)
    return lpos, table, zfill, blk_e, n_used, n_blk


def _dispatch_body(zf_ref, tab_ref, lpos_ref, hn_ref, x_hbm, xs, zero, sem, zsem, n_prev, *, tm, nseg, nrow):
    i = pl.program_id(0)
    slot = i % 2
    step = RUN_ROWS * nseg

    @pl.when(i == 0)
    def _():
        zero[...] = jnp.zeros_like(zero)

        def zero_copy(r):
            return pltpu.make_async_copy(zero, x_hbm.at[pl.ds(pl.multiple_of(r * nseg, nseg), step), :], zsem)

        nz = N_EXPERTS + 1

        def for_chunks(fn):
            for e in range(nz):
                def body(k, carry, e=e):
                    fn(zero_copy(zf_ref[0, e] + k * RUN_ROWS))
                    return carry
                lax.fori_loop(0, zf_ref[0, nz + e], body, 0)

        for_chunks(lambda c: c.start())
        for_chunks(lambda c: c.wait())
        for e in range(nz):
            zero_copy(zf_ref[0, 2 * nz + e]).start()
        for e in range(nz):
            zero_copy(zf_ref[0, 2 * nz + e]).wait()

    row = lax.broadcasted_iota(I32, (nrow, tm), 0)
    sel = (row == lpos_ref[0:1, :]) | (row == lpos_ref[1:2, :])
    rows = jnp.dot(sel.astype(BF16), hn_ref[...], preferred_element_type=F32)
    for k in range(nseg):
        xs[slot, pl.ds(k, nrow, stride=nseg), :] = rows[:, k * LANES:(k + 1) * LANES]

    def chunk_copy(sl, c):
        return pltpu.make_async_copy(
            xs.at[sl, pl.ds(pl.multiple_of(c * step, step), step), :],
            x_hbm.at[pl.ds(pl.multiple_of(tab_ref[0, 0, c] * nseg, nseg), step), :], sem.at[sl])

    def wait_chunks(sl, n):
        def wait(c, carry):
            chunk_copy(sl, 0).wait()
            return carry
        lax.fori_loop(0, n, wait, 0)

    n_chunks = tab_ref[0, 0, LANES - 1]

    @pl.when(i > 0)
    def _():
        wait_chunks(1 - slot, n_prev[0])

    def start(c, carry):
        chunk_copy(slot, c).start()
        return carry

    lax.fori_loop(0, n_chunks, start, 0)
    n_prev[0] = n_chunks

    @pl.when(i == pl.num_programs(0) - 1)
    def _():
        wait_chunks(slot, n_chunks)


def _dispatch(zfill, table, lpos_t, hn, n_pad, tm):
    T, D = hn.shape
    nseg = D // LANES
    nrow = tm * TOPK_IN + N_EXPERTS * RUN_ROWS
    return pl.pallas_call(
        functools.partial(_dispatch_body, tm=tm, nseg=nseg, nrow=nrow), grid=(T // tm,),
        in_specs=[pl.BlockSpec(memory_space=pltpu.SMEM),
                  pl.BlockSpec((1, 1, LANES), lambda i: (i, 0, 0), memory_space=pltpu.SMEM),
                  pl.BlockSpec((None, TOPK_IN, tm), lambda i: (i, 0, 0)),
                  pl.BlockSpec((tm, D), lambda i: (i, 0))],
        out_specs=pl.BlockSpec(memory_space=pl.ANY),
        out_shape=jax.ShapeDtypeStruct((n_pad * nseg, LANES), F32),
        scratch_shapes=[pltpu.VMEM((2, nrow * nseg, LANES), F32), pltpu.VMEM((RUN_ROWS * nseg, LANES), F32),
                        pltpu.SemaphoreType.DMA((2,)), pltpu.SemaphoreType.DMA, pltpu.SMEM((1,), I32)],
        compiler_params=_params("arbitrary"), name="dispatch")(zfill, table, lpos_t, hn)


def _moe_body(blk_e_ref, n_used_ref, seg_ref, nxt_ref, x_ref, w1_hbm, w3_hbm, w2_hbm, y_ref,
              wf1, wf3, wf2, w1b, w3b, w2b, sem, *, nseg, layer):
    b = pl.program_id(0)
    R = MOE_BLOCK

    def fetch(e, slot):
        return [pltpu.make_async_copy(src.at[layer, e], dst.at[slot], sem.at[slot, k])
                for k, (src, dst) in enumerate(((w1_hbm, wf1), (w3_hbm, wf3), (w2_hbm, wf2)))]

    @pl.when(b < n_used_ref[0])
    def _():
        @pl.when((b == 0) | (blk_e_ref[b] != blk_e_ref[jnp.maximum(b - 1, 0)]))
        def _():
            e = blk_e_ref[b]
            slot = seg_ref[b] % 2

            @pl.when(b == 0)
            def _():
                for c in fetch(e, slot):
                    c.start()

            for c in fetch(e, slot):
                c.wait()
            w1b[...] = wf1[slot].astype(BF16)
            w3b[...] = wf3[slot].astype(BF16)
            w2b[...] = wf2[slot].astype(BF16)

            @pl.when(nxt_ref[b] >= 0)
            def _():
                for c in fetch(nxt_ref[b], 1 - slot):
                    c.start()

        x = jnp.concatenate([x_ref[pl.ds(s, R, stride=nseg), :] for s in range(nseg)], axis=-1).astype(BF16)
        a = jnp.dot(x, w1b[...], preferred_element_type=F32)
        g = jnp.dot(x, w3b[...], preferred_element_type=F32)
        hid = (a * jax.nn.sigmoid(a) * g).astype(BF16)
        y = jnp.dot(hid, w2b[...], preferred_element_type=F32)
        for s in range(nseg):
            y_ref[pl.ds(s, R, stride=nseg), :] = y[:, s * LANES:(s + 1) * LANES]

    @pl.when(b >= n_used_ref[0])
    def _():
        y_ref[...] = jnp.zeros_like(y_ref)


def _moe(blk_e, n_used, x_rows, w1, w3, w2, layer):
    n_blk = blk_e.shape[0]
    _, _, D, DE = w1.shape
    nseg = D // LANES
    R = MOE_BLOCK
    blk = jnp.arange(n_blk, dtype=I32)
    used = blk < n_used[0]
    change = (blk > 0) & (blk_e != jnp.roll(blk_e, 1)) & used
    seg = jnp.cumsum(change.astype(I32)).astype(I32)
    later = (blk[None, :] > blk[:, None]) & (seg[None, :] > seg[:, None]) & used[None, :]
    nxt_blk = jnp.min(jnp.where(later, blk[None, :], n_blk), axis=1)
    nxt = jnp.where(nxt_blk < n_blk, blk_e[jnp.minimum(nxt_blk, n_blk - 1)], -1).astype(I32)
    rows = lambda b, be, nu, sg, nx: (jnp.minimum(b, nu[0] - 1), 0)
    grid_spec = pltpu.PrefetchScalarGridSpec(
        num_scalar_prefetch=4, grid=(n_blk,),
        in_specs=[pl.BlockSpec((R * nseg, LANES), rows),
                  pl.BlockSpec(memory_space=pl.ANY), pl.BlockSpec(memory_space=pl.ANY),
                  pl.BlockSpec(memory_space=pl.ANY)],
        out_specs=pl.BlockSpec((R * nseg, LANES), lambda b, be, nu, sg, nx: (b, 0)),
        scratch_shapes=[pltpu.VMEM((2, D, DE), F32), pltpu.VMEM((2, D, DE), F32), pltpu.VMEM((2, DE, D), F32),
                        pltpu.VMEM((D, DE), BF16), pltpu.VMEM((D, DE), BF16), pltpu.VMEM((DE, D), BF16),
                        pltpu.SemaphoreType.DMA((2, 3))])
    return pl.pallas_call(
        functools.partial(_moe_body, nseg=nseg, layer=layer), grid_spec=grid_spec,
        out_shape=jax.ShapeDtypeStruct(x_rows.shape, F32),
        compiler_params=_params("arbitrary"), name="moe")(blk_e, n_used, seg, nxt, x_rows, w1, w3, w2)


def _combine_body(tab_ref, tabn_ref, h_ref, rf_ref, lpos_ref, y_hbm, o_ref, ybuf, sem, *, tm, nseg, nrow):
    i = pl.program_id(0)
    slot = i % 2
    step = RUN_ROWS * nseg

    def start_chunks(tab, sl):
        def start(c, carry):
            pltpu.make_async_copy(
                y_hbm.at[pl.ds(pl.multiple_of(tab[0, 0, c] * nseg, nseg), step), :],
                ybuf.at[sl, pl.ds(pl.multiple_of(c * step, step), step), :], sem.at[sl]).start()
            return carry
        lax.fori_loop(0, tab[0, 0, LANES - 1], start, 0)

    @pl.when(i == 0)
    def _():
        ybuf[...] = jnp.zeros_like(ybuf)
        start_chunks(tab_ref, 0)

    @pl.when(i + 1 < pl.num_programs(0))
    def _():
        start_chunks(tabn_ref, 1 - slot)

    col = lax.broadcasted_iota(I32, (tm, nrow), 1)
    sel = jnp.zeros((tm, nrow), F32)
    for s in range(TOPK_IN):
        sel = sel + jnp.where(col == lpos_ref[:, s:s + 1], rf_ref[:, s:s + 1], 0.0)

    def wait(c, carry):
        pltpu.make_async_copy(y_hbm.at[pl.ds(0, step), :], ybuf.at[slot, pl.ds(0, step), :],
                              sem.at[slot]).wait()
        return carry

    lax.fori_loop(0, tab_ref[0, 0, LANES - 1], wait, 0)
    y = jnp.concatenate([ybuf[slot, pl.ds(k, nrow, stride=nseg), :] for k in range(nseg)], axis=-1)
    o_ref[...] = h_ref[...] + jnp.dot(sel.astype(BF16), y.astype(BF16), preferred_element_type=F32)


def _combine(table, h2d, rf, lpos, y_rows, tm):
    T, D = h2d.shape
    nseg = D // LANES
    nt = T // tm
    nrow = tm * TOPK_IN + N_EXPERTS * RUN_ROWS
    return pl.pallas_call(
        functools.partial(_combine_body, tm=tm, nseg=nseg, nrow=nrow), grid=(nt,),
        in_specs=[pl.BlockSpec((1, 1, LANES), lambda i: (i, 0, 0), memory_space=pltpu.SMEM),
                  pl.BlockSpec((1, 1, LANES), lambda i: (jnp.minimum(i + 1, nt - 1), 0, 0),
                               memory_space=pltpu.SMEM),
                  pl.BlockSpec((tm, D), lambda i: (i, 0)),
                  pl.BlockSpec((tm, LANES), lambda i: (i, 0)),
                  pl.BlockSpec((tm, TOPK_IN), lambda i: (i, 0)),
                  pl.BlockSpec(memory_space=pl.ANY)],
        out_specs=pl.BlockSpec((tm, D), lambda i: (i, 0)),
        out_shape=jax.ShapeDtypeStruct((T, D), F32),
        scratch_shapes=[pltpu.VMEM((2, nrow * nseg, LANES), F32), pltpu.SemaphoreType.DMA((2,))],
        compiler_params=_params("arbitrary"), name="combine")(table, table, h2d, rf, lpos, y_rows)


def _pad_lanes(a, lane0, rows=1):
    a = a.reshape(rows, -1)
    return jnp.pad(a, ((0, 0), (lane0, LANES - lane0 - a.shape[-1])))


def _layer(h2d, B, S, p):
    T, D = h2d.shape
    tile = lambda a, n: jnp.tile(a.reshape(1, -1), (1, n))

    offs = np.cumsum([0, GROUP_W, GROUP_W, GROUP_W, N_HEADS, GROUP_W, GROUP_W,
                      GROUP_W, GROUP_W, GROUP_W, N_HEADS, N_HEADS, GROUP_W, GROUP_W])
    seg = lambda k: p['w_in'][:, offs[k]:offs[k + 1]]
    w_all = jnp.concatenate([seg(0), seg(1), seg(2), seg(4), seg(5), seg(6), seg(7), seg(8), seg(11), seg(12),
                             seg(3), seg(9), seg(10), jnp.zeros((D, LANES - 3 * N_HEADS), F32)],
                            axis=1).astype(BF16)
    gate_prm = jnp.concatenate([_pad_lanes(p['fox_f_bias'], LANE_FOX), _pad_lanes(p['gdn_dt_bias'], LANE_DECAY),
                                _pad_lanes(p['gdn_a_log'], LANE_DECAY), jnp.zeros((SUBLANES - 3, LANES), F32)], axis=0)

    big, small = _inproj(h2d, p['attn_norm_g'].reshape(1, D), w_all, tm=min(1024, T))
    big3 = big.reshape(B, S, N_BIG_COLS * GROUP_W)
    ts = min(1024, S)
    gcol, gx = _gates(small.reshape(B, S, LANES), gate_prm, ts)

    tq = min(256, S)
    ya = _fox(big3, gcol, tile(p['fox_qn_g'], N_HEADS), tile(p['fox_kn_g'], N_HEADS),
              p['fox_out_g'].reshape(1, GROUP_W), tq)

    bst = jnp.repeat(p['gmlp_bs'].T, HEAD_DIM, axis=1)
    yb = _gmlp(big, p['gmlp_ln_g'].reshape(1, -1), p['gmlp_ln_b'].reshape(1, -1), p['gmlp_ws'], bst,
               p['gmlp_out_g'].reshape(1, -1), tm=min(1024, T))

    conv_w3 = p['gdn_conv_w'].reshape(CONV_K, 3, GROUP_W).transpose(1, 0, 2)
    nb = 4 if B % 4 == 0 else (2 if B % 2 == 0 else 1)
    yc = _gdn(big3, conv_w3, gx, tile(p['gdn_norm_g'], N_HEADS), min(1024 // nb, S), nb)

    wbd = jax.scipy.linalg.block_diag(*[p['pool_w'][g] for g in range(len(POOL_WINDOWS))]).astype(BF16)
    yd = _pool(big3, wbd, p['pool_scale'].reshape(1, -1), p['pool_out_g'].reshape(1, -1), ts)

    rw = _pad_lanes(jnp.concatenate([p['router_g_w'], p['router_e_w']], axis=1), 0, rows=D)
    rw = jnp.concatenate(_split(rw), axis=1)
    rb = _pad_lanes(jnp.concatenate([p['router_g_b'], p['router_e_b']]), 0)
    flat = lambda a: a.reshape(T, GROUP_W)
    tmd = min(512, T)
    h_new, hn_rows, ri, rf, cnt = _outproj(flat(ya), yb, flat(yc), flat(yd), h2d, p['w_out'].astype(BF16),
                                           p['ffn_norm_g'].reshape(1, D), rw, rb, tm=min(1024, T), tr=tmd)
    cnt_tile = cnt.reshape(T // tmd, SUBLANES, LANES)[:, 0, :N_EXPERTS]
    lpos, table, zfill, blk_e, n_used, n_blk = _dispatch_plan(
        ri[:, :TOPK_IN], ri[:, TOPK_IN:2 * TOPK_IN], cnt_tile, tmd)
    lpos_t = lpos.reshape(T // tmd, tmd, TOPK_IN).transpose(0, 2, 1)
    x_rows = _dispatch(zfill, table, lpos_t, hn_rows, n_blk * MOE_BLOCK, tmd)
    y_rows = _moe(blk_e, n_used, x_rows, p['moe_w1'], p['moe_w3'], p['moe_w2'], p['layer'])
    return _combine(table, h_new, rf, lpos, y_rows, tmd)


def kernel(x, attn_norm_g, w_in, w_out, fox_f_bias, fox_qn_g, fox_kn_g, fox_out_g, gmlp_ln_g, gmlp_ln_b, gmlp_ws, gmlp_bs, gmlp_out_g, gdn_conv_w, gdn_a_log, gdn_dt_bias, gdn_norm_g, pool_w, pool_scale, pool_out_g, ffn_norm_g, router_g_w, router_g_b, router_e_w, router_e_b, moe_w1, moe_w3, moe_w2):
    B, S, D = x.shape
    names = ('attn_norm_g', 'w_in', 'w_out', 'fox_f_bias', 'fox_qn_g', 'fox_kn_g', 'fox_out_g', 'gmlp_ln_g',
             'gmlp_ln_b', 'gmlp_ws', 'gmlp_bs', 'gmlp_out_g', 'gdn_conv_w', 'gdn_a_log', 'gdn_dt_bias',
             'gdn_norm_g', 'pool_w', 'pool_scale', 'pool_out_g', 'ffn_norm_g', 'router_g_w', 'router_g_b',
             'router_e_w', 'router_e_b', 'moe_w1', 'moe_w3', 'moe_w2')
    vals = (attn_norm_g, w_in, w_out, fox_f_bias, fox_qn_g, fox_kn_g, fox_out_g, gmlp_ln_g, gmlp_ln_b, gmlp_ws,
            gmlp_bs, gmlp_out_g, gdn_conv_w, gdn_a_log, gdn_dt_bias, gdn_norm_g, pool_w, pool_scale, pool_out_g,
            ffn_norm_g, router_g_w, router_g_b, router_e_w, router_e_b, moe_w1, moe_w3, moe_w2)
    h = x.reshape(B * S, D)
    stacked = ('moe_w1', 'moe_w3', 'moe_w2')
    for l in range(w_in.shape[0]):
        p = {n: (v if n in stacked else v[l]) for n, v in zip(names, vals)}
        p['layer'] = l
        h = _layer(h, B, S, p)
    return h.reshape(B, S, D)
```

```python
import functools

import jax
import jax.numpy as jnp
import numpy as np
from jax import lax
from jax.experimental import pallas as pl
from jax.experimental.pallas import tpu as pltpu

F32 = jnp.float32
BF16 = jnp.bfloat16
I32 = jnp.int32

EPS = 1e-6
HEAD_DIM = 64
GROUP_W = 256
N_HEADS = GROUP_W // HEAD_DIM
CHUNK = 64
GMLP_LEN = 128
CONV_K = 4
POOL_WINDOWS = (2, 4, 8, 16)
N_GROUPS = 4
EXPERTS_PER_GROUP = 8
N_EXPERTS = N_GROUPS * EXPERTS_PER_GROUP
TOPK_IN = 2
MOE_BLOCK = 256
FOX_VT_ROWS = HEAD_DIM + 16
FOX_BOUND_LOG2 = 40.0
FOX_SKIP_LOG2 = 160.0
RUN_ROWS = 16
LANES = 128
SUBLANES = 8
VMEM_LIMIT = 56 * 1024 * 1024

COL_FQ, COL_FK, COL_FV, COL_GU, COL_GV, COL_DQ, COL_DK, COL_DV, COL_DG, COL_PZ = range(10)
N_BIG_COLS = 10
LANE_FOX, LANE_DECAY, LANE_BETA = 0, 4, 8


def _params(*sem):
    return pltpu.CompilerParams(dimension_semantics=sem, vmem_limit_bytes=VMEM_LIMIT)


def _head_ones():
    r = lax.broadcasted_iota(I32, (GROUP_W, GROUP_W), 0) // HEAD_DIM
    c = lax.broadcasted_iota(I32, (GROUP_W, GROUP_W), 1) // HEAD_DIM
    return (r == c).astype(BF16)


def _head_sums(x, ones_bd):
    hi = x.astype(BF16)
    lo = (x - hi.astype(F32)).astype(BF16)
    return (jnp.dot(hi, ones_bd, preferred_element_type=F32)
            + jnp.dot(lo, ones_bd, preferred_element_type=F32))


def _rms(x, g):
    return x * lax.rsqrt(jnp.mean(x * x, axis=-1, keepdims=True) + EPS) * g


def _mm(a, b):
    return jnp.dot(a.astype(BF16), b.astype(BF16), preferred_element_type=F32)


def _mm_nt(a, b):
    return lax.dot_general(a.astype(BF16), b.astype(BF16), (((1,), (1,)), ((), ())),
                           preferred_element_type=F32)


def _mm_tn(a, b):
    return lax.dot_general(a.astype(BF16), b.astype(BF16), (((0,), (0,)), ((), ())),
                           preferred_element_type=F32)


def _split(a):
    hi = a.astype(BF16)
    return hi, (a - hi.astype(F32)).astype(BF16)


def _mm3(a, b):
    ah, al = _split(a)
    bh, bl = _split(b)
    d = functools.partial(jnp.dot, preferred_element_type=F32)
    return d(ah, bh) + (d(ah, bl) + d(al, bh))


def _inproj_body(x_ref, g_ref, w_ref, big_ref, small_ref):
    xn = _rms(x_ref[...], g_ref[...]).astype(BF16)
    nb = big_ref.shape[1]
    big_ref[...] = jnp.dot(xn, w_ref[:, :nb], preferred_element_type=F32).astype(big_ref.dtype)
    small_ref[...] = jnp.dot(xn, w_ref[:, nb:], preferred_element_type=F32)


def _inproj(x2d, g, w, tm):
    T, D = x2d.shape
    nb = w.shape[1] - LANES
    return pl.pallas_call(
        _inproj_body, grid=(T // tm,),
        in_specs=[pl.BlockSpec((tm, D), lambda i: (i, 0)),
                  pl.BlockSpec((1, D), lambda i: (0, 0)),
                  pl.BlockSpec((D, nb + LANES), lambda i: (0, 0))],
        out_specs=[pl.BlockSpec((tm, nb), lambda i: (i, 0)),
                   pl.BlockSpec((tm, LANES), lambda i: (i, 0))],
        out_shape=[jax.ShapeDtypeStruct((T, nb), BF16), jax.ShapeDtypeStruct((T, LANES), F32)],
        compiler_params=_params("parallel"), name="inproj")(x2d, g, w)


def _gates_body(sm_ref, p_ref, col_ref, exp_ref, carry_ref, *, ts):
    @pl.when(pl.program_id(1) == 0)
    def _():
        carry_ref[...] = jnp.zeros_like(carry_ref)

    x = sm_ref[...]
    lane = lax.broadcasted_iota(I32, (ts, LANES), 1)
    is_fox = lane < LANE_DECAY
    is_dec = (lane >= LANE_DECAY) & (lane < LANE_BETA)
    is_beta = (lane >= LANE_BETA) & (lane < LANE_BETA + N_HEADS)
    logf = jax.nn.log_sigmoid(x + p_ref[0:1, :])
    g = -jnp.exp(p_ref[2:3, :]) * jax.nn.softplus(x + p_ref[1:2, :])
    beta = jax.nn.sigmoid(x)
    tri = (lax.broadcasted_iota(I32, (CHUNK, CHUNK), 0) >= lax.broadcasted_iota(I32, (CHUNK, CHUNK), 1)).astype(BF16)
    vals = jnp.where(is_fox, logf, jnp.where(is_dec, g, 0.0))
    hi = vals.astype(BF16)
    mid = (vals - hi.astype(F32)).astype(BF16)
    lo = (vals - hi.astype(F32) - mid.astype(F32)).astype(BF16)
    parts = jnp.concatenate([hi, mid, lo], axis=1)
    running = carry_ref[...]
    cf_blocks, cg_blocks = [], []
    for n in range(ts // CHUNK):
        t = jnp.dot(tri, parts[n * CHUNK:(n + 1) * CHUNK, :], preferred_element_type=F32)
        within = t[:, :LANES] + (t[:, LANES:2 * LANES] + t[:, 2 * LANES:])
        cg_blocks.append(within)
        cf_blocks.append(within + running)
        running = cf_blocks[-1][CHUNK - 1:CHUNK, :]
    cf = jnp.concatenate(cf_blocks, axis=0)
    cg = jnp.concatenate(cg_blocks, axis=0)
    carry_ref[...] = running
    out = jnp.where(is_fox, cf, jnp.where(is_dec, cg, jnp.where(is_beta, beta, 0.0)))
    col_ref[...] = out
    o_hi = out.astype(BF16)
    o_mid = (out - o_hi.astype(F32)).astype(BF16)
    o_lo = (out - o_hi.astype(F32) - o_mid.astype(F32)).astype(BF16)
    src = lax.broadcasted_iota(I32, (3 * LANES, 2 * GROUP_W), 0) % LANES
    dst = lax.broadcasted_iota(I32, (3 * LANES, 2 * GROUP_W), 1)
    want = jnp.where(dst < GROUP_W, LANE_DECAY, LANE_BETA) + (dst % GROUP_W) // HEAD_DIM
    exp_ref[...] = jnp.dot(jnp.concatenate([o_hi, o_mid, o_lo], axis=1), (src == want).astype(BF16),
                           preferred_element_type=F32)


def _gates(small3, prm, ts):
    B, S, _ = small3.shape
    return pl.pallas_call(
        functools.partial(_gates_body, ts=ts), grid=(B, S // ts),
        in_specs=[pl.BlockSpec((None, ts, LANES), lambda b, j: (b, j, 0)),
                  pl.BlockSpec((SUBLANES, LANES), lambda b, j: (0, 0))],
        out_specs=[pl.BlockSpec((None, ts, LANES), lambda b, j: (b, j, 0)),
                   pl.BlockSpec((None, ts, 2 * GROUP_W), lambda b, j: (b, j, 0))],
        out_shape=[jax.ShapeDtypeStruct((B, S, LANES), F32),
                   jax.ShapeDtypeStruct((B, S, 2 * GROUP_W), F32)],
        scratch_shapes=[pltpu.VMEM((1, LANES), F32)],
        compiler_params=_params("parallel", "arbitrary"), name="gates")(small3, prm)


def _fox_placer(lane0):
    src = lax.broadcasted_iota(I32, (3 * LANES, N_HEADS * LANES), 0)
    dst = lax.broadcasted_iota(I32, (3 * LANES, N_HEADS * LANES), 1)
    return ((src % LANES == LANE_FOX + dst // LANES) & (dst % LANES == lane0 + src // LANES)).astype(BF16)


def _split3_place(vals, placer):
    hi = vals.astype(BF16)
    mid = (vals - hi.astype(F32)).astype(BF16)
    lo = (vals - hi.astype(F32) - mid.astype(F32)).astype(BF16)
    return jnp.dot(jnp.concatenate([hi, mid, lo], axis=1), placer, preferred_element_type=F32)


def _fox_body(jstart_ref, q_ref, k_ref, v_ref, ccol_ref, qg_ref, kg_ref, og_ref, bound_ref, o_ref,
              kn_scr, vt_scr, q_scr, m_scr, l_scr, acc_scr, s_scr, place_scr, *, tq, nk):
    b = pl.program_id(0)
    i = pl.program_id(1)
    ones_bd = _head_ones()
    lane = lax.broadcasted_iota(I32, (tq, LANES), 1)
    log2e = 1.0 / np.log(2.0)
    c_lane, r_lane = HEAD_DIM, HEAD_DIM + 3

    def head_norm(x, g):
        ss = _head_sums(x * x, ones_bd)
        return x * lax.rsqrt(ss * (1.0 / HEAD_DIM) + EPS) * g

    def head_tile(x, h, extra):
        pair = x[:, (h // 2) * LANES:(h // 2 + 1) * LANES]
        if h % 2:
            pair = pltpu.roll(pair, HEAD_DIM, 1)
        return jnp.where(lane < HEAD_DIM, pair, extra).astype(BF16)

    @pl.when(i == 0)
    def _():
        k_ones = jnp.where((lane >= r_lane) & (lane < r_lane + 3), 1.0, 0.0)
        vt_tail = (lax.broadcasted_iota(I32, (FOX_VT_ROWS - HEAD_DIM, tq), 0) == 0).astype(BF16)
        place_scr[0] = _fox_placer(c_lane)
        place_scr[1] = _fox_placer(r_lane)
        for c in range(nk):
            rows = slice(c * tq, (c + 1) * tq)
            kc = head_norm(k_ref[rows, :].astype(F32), kg_ref[...])
            vt = v_ref[rows, :].astype(F32).T.astype(BF16)
            extras = _split3_place(ccol_ref[rows, :] * (-log2e), place_scr[0])
            for h in range(N_HEADS):
                extra = extras[:, h * LANES:(h + 1) * LANES] + k_ones
                kn_scr[h, c] = head_tile(kc, h, extra)
                vt_scr[h, c] = jnp.concatenate([vt[h * HEAD_DIM:(h + 1) * HEAD_DIM, :], vt_tail], axis=0)

    qn = head_norm(q_ref[...].astype(F32), qg_ref[...]) * (HEAD_DIM ** -0.5 * log2e)
    bound = bound_ref[...]
    bounded = jnp.max(bound) <= FOX_BOUND_LOG2
    c_i = ccol_ref[pl.ds(pl.multiple_of(i * tq, tq), tq), :] * log2e
    q_ones = jnp.where((lane >= c_lane) & (lane < c_lane + 3), 1.0, 0.0)
    neg_r = _split3_place(jnp.where(bounded, c_i - bound, 0.0), place_scr[1])
    for h in range(N_HEADS):
        q_scr[h] = head_tile(qn, h, q_ones + neg_r[:, h * LANES:(h + 1) * LANES])
    acc_scr[...] = jnp.zeros_like(acc_scr)
    causal = (lax.broadcasted_iota(I32, (tq, tq), 0) <= lax.broadcasted_iota(I32, (tq, tq), 1))

    heads = range(N_HEADS)

    def scores(j):
        return [lax.dot_general(kn_scr[h, j], q_scr[h], (((1,), (1,)), ((), ())),
                                preferred_element_type=F32) for h in heads]

    def stash(s):
        for h in heads:
            s_scr[h] = s[h]

    def absorb_general(j, masked):
        p, alpha = [], []
        for h in heads:
            s = s_scr[h]
            if masked:
                s = jnp.where(causal, s, -jnp.inf)
            m_old = m_scr[h]
            m_new = jnp.maximum(m_old, jnp.max(s, axis=0, keepdims=True))
            alpha.append(jnp.exp2(m_old - m_new))
            ph = jnp.exp2(s - m_new)
            l_scr[h] = alpha[h] * l_scr[h] + jnp.sum(ph, axis=0, keepdims=True)
            m_scr[h] = m_new
            p.append(ph.astype(BF16))
        pv = [jnp.dot(vt_scr[h, j], p[h], preferred_element_type=F32) for h in heads]
        for h in heads:
            acc_scr[h] = alpha[h] * acc_scr[h] + pv[h]

    def absorb_bounded(j, masked):
        p = []
        for h in heads:
            s = s_scr[h]
            if masked:
                s = jnp.where(causal, s, -jnp.inf)
            p.append(jnp.exp2(s).astype(BF16))
        pv = [jnp.dot(vt_scr[h, j], p[h], preferred_element_type=F32) for h in heads]
        for h in heads:
            acc_scr[h] += pv[h]

    def run(absorb, j0):
        stash(scores(j0))

        def body(j, c):
            s_next = scores(j + 1)
            absorb(j, False)
            stash(s_next)
            return c

        lax.fori_loop(j0, i, body, 0)
        absorb(i, True)

    @pl.when(bounded)
    def _():
        run(absorb_bounded, jstart_ref[b * nk + i])

    @pl.when(jnp.logical_not(bounded))
    def _():
        m_scr[...] = jnp.full_like(m_scr, -jnp.inf)
        l_scr[...] = jnp.zeros_like(l_scr)
        run(absorb_general, 0)
        for h in heads:
            acc_scr[h, HEAD_DIM:HEAD_DIM + 1, :] = l_scr[h]

    o_t = jnp.concatenate([acc_scr[h, :HEAD_DIM, :] * (1.0 / acc_scr[h, HEAD_DIM:HEAD_DIM + 1, :]) for h in heads],
                          axis=0)
    o_ref[...] = _rms(o_t.T, og_ref[...])


def _fox_first_block(gcol, tq):
    B, S, _ = gcol.shape
    nk = S // tq
    c2 = gcol[:, :, LANE_FOX:LANE_FOX + N_HEADS] * (1.0 / np.log(2.0))
    first = c2[:, 0::tq, :]
    last = c2[:, tq - 1::tq, :]
    dead = (first[:, :, None, :] - last[:, None, :, :]) < -FOX_SKIP_LOG2
    dead = dead & (jnp.arange(nk)[None, :, None, None] > jnp.arange(nk)[None, None, :, None])
    return jnp.min(jnp.sum(dead, axis=2), axis=-1).astype(I32).reshape(B * nk)


def _fox(big3, gcol, qg, kg, og, tq):
    B, S, _ = big3.shape
    nk = S // tq
    row = pl.BlockSpec((1, GROUP_W), lambda b, i, js: (0, 0))
    grid_spec = pltpu.PrefetchScalarGridSpec(
        num_scalar_prefetch=1, grid=(B, nk),
        in_specs=[pl.BlockSpec((None, tq, GROUP_W), lambda b, i, js: (b, i, COL_FQ)),
                  pl.BlockSpec((None, S, GROUP_W), lambda b, i, js: (b, 0, COL_FK)),
                  pl.BlockSpec((None, S, GROUP_W), lambda b, i, js: (b, 0, COL_FV)),
                  pl.BlockSpec((None, S, LANES), lambda b, i, js: (b, 0, 0)),
                  row, row, row, pl.BlockSpec((1, LANES), lambda b, i, js: (0, 0))],
        out_specs=pl.BlockSpec((None, tq, GROUP_W), lambda b, i, js: (b, i, 0)),
        scratch_shapes=[pltpu.VMEM((N_HEADS, nk, tq, LANES), BF16),
                        pltpu.VMEM((N_HEADS, nk, FOX_VT_ROWS, tq), BF16),
                        pltpu.VMEM((N_HEADS, tq, LANES), BF16),
                        pltpu.VMEM((N_HEADS, 1, tq), F32),
                        pltpu.VMEM((N_HEADS, 1, tq), F32),
                        pltpu.VMEM((N_HEADS, FOX_VT_ROWS, tq), F32),
                        pltpu.VMEM((N_HEADS, tq, tq), F32),
                        pltpu.VMEM((2, 3 * LANES, N_HEADS * LANES), BF16)])
    gmax = lambda g: jnp.max(jnp.abs(g.reshape(N_HEADS, HEAD_DIM)), axis=1)
    bound = _pad_lanes(gmax(qg) * gmax(kg) * (HEAD_DIM * HEAD_DIM ** -0.5 / np.log(2.0) * 1.02), LANE_FOX)
    return pl.pallas_call(
        functools.partial(_fox_body, tq=tq, nk=nk), grid_spec=grid_spec,
        out_shape=jax.ShapeDtypeStruct((B, S, GROUP_W), F32),
        compiler_params=_params("parallel", "arbitrary"), name="fox")(
            _fox_first_block(gcol, tq), big3, big3, big3, gcol, qg, kg, og, bound)


def _gelu(x):
    return 0.5 * x * (1.0 + lax.erf(x * (2.0 ** -0.5)))


def _gmlp_body(u_ref, v_ref, lg_ref, lb_ref, ws_ref, bst_ref, og_ref, o_ref, *, nwin):
    L = GMLP_LEN
    r = lax.broadcasted_iota(I32, (L, L), 0) // CHUNK
    c = lax.broadcasted_iota(I32, (L, L), 1) // CHUNK
    mask = r >= c
    ws = [jnp.where(mask, ws_ref[h], 0.0).astype(BF16) for h in range(N_HEADS)]
    for n in range(nwin):
        u = _gelu(u_ref[n * L:(n + 1) * L, :].astype(F32))
        v = _gelu(v_ref[n * L:(n + 1) * L, :].astype(F32))
        mu = jnp.mean(v, axis=-1, keepdims=True)
        vc = v - mu
        var = jnp.mean(vc * vc, axis=-1, keepdims=True)
        vn = (vc * lax.rsqrt(var + EPS) * lg_ref[...] + lb_ref[...]).astype(BF16)
        mixed = jnp.concatenate(
            [jnp.dot(ws[h], vn[:, h * HEAD_DIM:(h + 1) * HEAD_DIM], preferred_element_type=F32)
             for h in range(N_HEADS)], axis=-1) + bst_ref[...]
        o_ref[n * L:(n + 1) * L, :] = _rms(u * mixed, og_ref[...])


def _gmlp(big, lg, lb, ws, bst, og, tm):
    T = big.shape[0]
    row = pl.BlockSpec((1, GROUP_W), lambda i: (0, 0))
    return pl.pallas_call(
        functools.partial(_gmlp_body, nwin=tm // GMLP_LEN), grid=(T // tm,),
        in_specs=[pl.BlockSpec((tm, GROUP_W), lambda i: (i, COL_GU)),
                  pl.BlockSpec((tm, GROUP_W), lambda i: (i, COL_GV)),
                  row, row,
                  pl.BlockSpec((N_HEADS, GMLP_LEN, GMLP_LEN), lambda i: (0, 0, 0)),
                  pl.BlockSpec((GMLP_LEN, GROUP_W), lambda i: (0, 0)),
                  row],
        out_specs=pl.BlockSpec((tm, GROUP_W), lambda i: (i, 0)),
        out_shape=jax.ShapeDtypeStruct((T, GROUP_W), F32),
        compiler_params=_params("parallel"), name="gmlp")(big, big, lg, lb, ws, bst, og)


def _gdn_body(q_ref, k_ref, v_ref, w_ref, gate_ref, gx_ref, ng_ref, o_ref,
              s_scr, u_scr, wq_scr, a_scr, kd_scr, dl_scr, t_scr, p_scr, rhs_scr, halo_scr, qkv_scr, *, nchunk, nb):
    C = CHUNK
    ts = nchunk * C
    first = pl.program_id(1) == 0

    @pl.when(first)
    def _():
        s_scr[...] = jnp.zeros_like(s_scr)

    ones_bd = _head_ones()
    for bb, a in [(bb, a) for bb in range(nb) for a in range(3)]:
        w = w_ref[a]
        x = (q_ref, k_ref, v_ref)[a][bb].astype(F32)
        xx = jnp.concatenate([jnp.where(first, 0.0, halo_scr[bb * 3 + a]), x], axis=0)
        halo_scr[bb * 3 + a] = x[ts - SUBLANES:, :]
        y = w[CONV_K - 1:CONV_K, :] * x
        for j in range(CONV_K - 1):
            y = y + w[j:j + 1, :] * pltpu.roll(xx, CONV_K - 1 - j, 0)[SUBLANES:, :]
        y = y * jax.nn.sigmoid(y)
        if a < 2:
            y = y * lax.rsqrt(_head_sums(y * y, ones_bd) + EPS)
        qkv_scr[bb * 3 + a] = y * (HEAD_DIM ** -0.5) if a == 0 else y

    W = GROUP_W
    pos = lax.broadcasted_iota(I32, (C, W), 1) % HEAD_DIM
    r = lax.broadcasted_iota(I32, (C, W), 0)
    tri, strict, eye = r >= pos, r > pos, r == pos
    same_head = (lax.broadcasted_iota(I32, (W, W), 0) // HEAD_DIM
                 == lax.broadcasted_iota(I32, (W, W), 1) // HEAD_DIM)
    mmb = functools.partial(jnp.dot, preferred_element_type=F32)

    def block_diag(x):
        return jnp.where(same_head, jnp.concatenate([x.astype(BF16)] * N_HEADS, axis=0), 0.0)

    items = nb * nchunk
    for n in range(items):
        bb = n // nchunk
        rows = slice((n % nchunk) * C, (n % nchunk + 1) * C)
        q, k, v = (qkv_scr[bb * 3 + a, rows, :] for a in range(3))
        gc = gx_ref[bb, rows, :W]
        beta = gx_ref[bb, rows, W:]
        gr = jnp.sum(jnp.where(eye, gc, 0.0), axis=0, keepdims=True)
        decay = jnp.exp(jnp.where(tri, gc - gr, -jnp.inf))
        kb = k * beta
        kk = lax.dot_general(jnp.concatenate([kb, q], axis=0).astype(BF16), block_diag(k),
                             (((1,), (1,)), ((), ())), preferred_element_type=F32)
        x = jnp.where(strict, -(kk[:C] * decay), 0.0)
        t_scr[n] = jnp.where(eye, 1.0, 0.0) + x
        p_scr[n] = x.astype(BF16)
        eg = jnp.exp(gc)
        g_last = gc[C - 1:C, :]
        rhs_scr[n, 0] = block_diag(v * beta)
        rhs_scr[n, 1] = block_diag(kb * eg)
        wq_scr[n, C:, :] = (q * eg).astype(BF16)
        a_scr[n] = jnp.where(tri, kk[C:] * decay, 0.0).astype(BF16)
        kd_scr[n] = (k * jnp.exp(g_last - gc)).astype(BF16)
        dl_scr[n] = jnp.exp(g_last)
    for level in range(1, 6):
        for n in range(items):
            p = p_scr[n]
            p_scr[n] = mmb(p, block_diag(p)).astype(BF16)
        for n in range(items):
            t = t_scr[n]
            t_scr[n] = t + mmb(t.astype(BF16), block_diag(p_scr[n]))
    for n in range(items):
        t = t_scr[n].astype(BF16)
        u_scr[n] = mmb(t, rhs_scr[n, 0])
        wq_scr[n, :C, :] = mmb(t, rhs_scr[n, 1]).astype(BF16)

    ones_bd = _head_ones()
    batch = range(nb)
    state = [s_scr[bb] for bb in batch]
    for c in range(nchunk):
        rows = slice(c * C, (c + 1) * C)
        it = [bb * nchunk + c for bb in batch]
        ws = [mmb(wq_scr[it[bb]], state[bb].astype(BF16)) for bb in batch]
        vb = [(u_scr[it[bb]] - ws[bb][:C]).astype(BF16) for bb in batch]
        o = [ws[bb][C:] + mmb(a_scr[it[bb]], block_diag(vb[bb])) for bb in batch]
        kv = [lax.dot_general(kd_scr[it[bb]], vb[bb], (((0,), (0,)), ((), ())), preferred_element_type=F32)
              for bb in batch]
        state = [state[bb] * dl_scr[it[bb]] + jnp.where(same_head, kv[bb], 0.0) for bb in batch]
        for bb in batch:
            gate = gate_ref[bb, rows, :].astype(F32)
            y = o[bb] * lax.rsqrt(_head_sums(o[bb] * o[bb], ones_bd) * (1.0 / HEAD_DIM) + EPS) * ng_ref[...]
            o_ref[bb, rows, :] = y * (gate * jax.nn.sigmoid(gate))
    for bb in batch:
        s_scr[bb] = state[bb]


def _gdn(big3, conv_w3, gx, ng, ts, nb):
    B, S, _ = big3.shape
    nchunk = ts // CHUNK
    col = lambda c: pl.BlockSpec((nb, ts, GROUP_W), lambda b, i: (b, i, c))
    W = GROUP_W
    items = nb * nchunk
    return pl.pallas_call(
        functools.partial(_gdn_body, nchunk=nchunk, nb=nb), grid=(B // nb, S // ts),
        in_specs=[col(COL_DQ), col(COL_DK), col(COL_DV),
                  pl.BlockSpec((3, CONV_K, W), lambda b, i: (0, 0, 0)),
                  col(COL_DG),
                  pl.BlockSpec((nb, ts, 2 * W), lambda b, i: (b, i, 0)),
                  pl.BlockSpec((1, W), lambda b, i: (0, 0))],
        out_specs=col(0), out_shape=jax.ShapeDtypeStruct((B, S, W), F32),
        scratch_shapes=[pltpu.VMEM((nb, W, W), F32),
                        pltpu.VMEM((items, CHUNK, W), F32),
                        pltpu.VMEM((items, 2 * CHUNK, W), BF16),
                        pltpu.VMEM((items, CHUNK, W), BF16),
                        pltpu.VMEM((items, CHUNK, W), BF16),
                        pltpu.VMEM((items, 1, W), F32),
                        pltpu.VMEM((items, CHUNK, W), F32),
                        pltpu.VMEM((items, CHUNK, W), BF16),
                        pltpu.VMEM((items, 2, W, W), BF16),
                        pltpu.VMEM((nb * 3, SUBLANES, W), F32),
                        pltpu.VMEM((nb * 3, ts, W), F32)],
        compiler_params=_params("parallel", "arbitrary"), name="gdn")(
            big3, big3, big3, conv_w3, big3, gx, ng)


def _pool_body(z_ref, halo_ref, w_ref, sc_ref, og_ref, o_ref, *, ts):
    i = pl.program_id(1)
    hr = 2 * SUBLANES
    z = z_ref[...].astype(F32)
    halo = jnp.where(i == 0, 0.0, halo_ref[...].astype(F32))
    s1 = jnp.concatenate([halo, z], axis=0)
    s2 = s1 + pltpu.roll(s1, 1, 0)
    s4 = s2 + pltpu.roll(s2, 2, 0)
    s8 = s4 + pltpu.roll(s4, 4, 0)
    s16 = s8 + pltpu.roll(s8, 8, 0)
    grp = lax.broadcasted_iota(I32, (ts, GROUP_W), 1) // (GROUP_W // len(POOL_WINDOWS))
    t = lax.broadcasted_iota(I32, (ts, GROUP_W), 0) + i * ts
    total = jnp.where(grp == 0, s2[hr:], jnp.where(grp == 1, s4[hr:], jnp.where(grp == 2, s8[hr:], s16[hr:])))
    win = jnp.where(grp == 0, POOL_WINDOWS[0], jnp.where(grp == 1, POOL_WINDOWS[1],
                    jnp.where(grp == 2, POOL_WINDOWS[2], POOL_WINDOWS[3])))
    pooled = total / jnp.minimum(t + 1, win).astype(F32)
    y = _mm(pooled - z, w_ref[...]) * sc_ref[...]
    o_ref[...] = _rms(y, og_ref[...])


def _pool(big3, wbd, sc, og, ts):
    B, S, _ = big3.shape
    hr = 2 * SUBLANES
    hb = ts // hr
    row = pl.BlockSpec((1, GROUP_W), lambda b, i: (0, 0))
    return pl.pallas_call(
        functools.partial(_pool_body, ts=ts), grid=(B, S // ts),
        in_specs=[pl.BlockSpec((None, ts, GROUP_W), lambda b, i: (b, i, COL_PZ)),
                  pl.BlockSpec((None, hr, GROUP_W), lambda b, i: (b, jnp.maximum(i * hb - 1, 0), COL_PZ)),
                  pl.BlockSpec((GROUP_W, GROUP_W), lambda b, i: (0, 0)), row, row],
        out_specs=pl.BlockSpec((None, ts, GROUP_W), lambda b, i: (b, i, 0)),
        out_shape=jax.ShapeDtypeStruct((B, S, GROUP_W), F32),
        compiler_params=_params("parallel", "parallel"), name="pool")(big3, big3, wbd, sc, og)


def _outproj_body(ya_ref, yb_ref, yc_ref, yd_ref, h_ref, wo_ref, g_ref, rw_ref, rb_ref,
                  hnew_ref, hn_ref, ri_ref, rf_ref, cnt_ref, *, tm, tr):
    y = jnp.concatenate([ya_ref[...], yb_ref[...], yc_ref[...], yd_ref[...]], axis=-1).astype(BF16)
    h_new = h_ref[...] + jnp.dot(y, wo_ref[...], preferred_element_type=F32)
    hnew_ref[...] = h_new
    hn = _rms(h_new, g_ref[...])
    hn_hi, hn_lo = _split(hn)
    hn_ref[...] = hn_hi
    t = jnp.dot(hn_hi, rw_ref[...], preferred_element_type=F32)
    logits = (t[:, :LANES] + t[:, LANES:]
              + jnp.dot(hn_lo, rw_ref[:, :LANES], preferred_element_type=F32)) + rb_ref[...]
    lane = lax.broadcasted_iota(I32, (tm, LANES), 1)
    neg = -jnp.inf
    big_lane = LANES

    def masked_top(vals, mask):
        v = jnp.where(mask, vals, neg)
        mx = jnp.max(v, axis=-1, keepdims=True)
        idx = jnp.min(jnp.where(mask & (v == mx), lane, big_lane), axis=-1, keepdims=True)
        return v, mx, idx

    gmask = lane < N_GROUPS
    gv, gmx, gidx = masked_top(logits, gmask)
    g_top = 1.0 / jnp.sum(jnp.where(gmask, jnp.exp(gv - gmx), 0.0), axis=-1, keepdims=True)
    lo = N_GROUPS + gidx * EXPERTS_PER_GROUP
    emask = (lane >= lo) & (lane < lo + EXPERTS_PER_GROUP)
    ev, emx, eidx1 = masked_top(logits, emask)
    esum = jnp.sum(jnp.where(emask, jnp.exp(ev - emx), 0.0), axis=-1, keepdims=True)
    p1 = 1.0 / esum
    _, emx2, eidx2 = masked_top(logits, emask & (lane != eidx1))
    p2 = jnp.exp(emx2 - emx) / esum
    denom = p1 + p2
    rf_ref[...] = jnp.where(lane == 0, g_top * p1 / denom, jnp.where(lane == 1, g_top * p2 / denom, 0.0))

    expert = [eidx1 - N_GROUPS, eidx2 - N_GROUPS]
    hot = [lane == e for e in expert]
    m = (hot[0] | hot[1]).astype(BF16)
    below = (lax.broadcasted_iota(I32, (tr, tr), 0) > lax.broadcasted_iota(I32, (tr, tr), 1)).astype(BF16)
    before = jnp.concatenate(
        [jnp.dot(below, m[k * tr:(k + 1) * tr, :], preferred_element_type=F32) for k in range(tm // tr)], axis=0)
    rank = [jnp.sum(jnp.where(hot[s], before, 0.0), axis=-1, keepdims=True).astype(I32) for s in range(TOPK_IN)]
    out = jnp.zeros((tm, LANES), I32)
    for s in range(TOPK_IN):
        out = jnp.where(lane == s, expert[s], jnp.where(lane == TOPK_IN + s, rank[s], out))
    ri_ref[...] = out
    for k in range(tm // tr):
        last = (k + 1) * tr - 1
        total = before[last:last + 1, :] + m[last:last + 1, :].astype(F32)
        cnt_ref[k * SUBLANES:(k + 1) * SUBLANES, :] = jnp.broadcast_to(total, (SUBLANES, LANES)).astype(I32)


def _outproj(ya, yb, yc, yd, h2d, wo, g, rw, rb, tm, tr):
    T, D = h2d.shape
    yblk = pl.BlockSpec((tm, GROUP_W), lambda i: (i, 0))
    cnt_rows = tm // tr * SUBLANES
    return pl.pallas_call(
        functools.partial(_outproj_body, tm=tm, tr=tr), grid=(T // tm,),
        in_specs=[yblk, yblk, yblk, yblk,
                  pl.BlockSpec((tm, D), lambda i: (i, 0)),
                  pl.BlockSpec((D, D), lambda i: (0, 0)),
                  pl.BlockSpec((1, D), lambda i: (0, 0)),
                  pl.BlockSpec((D, 2 * LANES), lambda i: (0, 0)),
                  pl.BlockSpec((1, LANES), lambda i: (0, 0))],
        out_specs=[pl.BlockSpec((tm, D), lambda i: (i, 0)),
                   pl.BlockSpec((tm, D), lambda i: (i, 0)),
                   pl.BlockSpec((tm, LANES), lambda i: (i, 0)),
                   pl.BlockSpec((tm, LANES), lambda i: (i, 0)),
                   pl.BlockSpec((cnt_rows, LANES), lambda i: (i, 0))],
        out_shape=[jax.ShapeDtypeStruct((T, D), F32),
                   jax.ShapeDtypeStruct((T, D), BF16),
                   jax.ShapeDtypeStruct((T, LANES), I32),
                   jax.ShapeDtypeStruct((T, LANES), F32),
                   jax.ShapeDtypeStruct((T // tr * SUBLANES, LANES), I32)],
        compiler_params=_params("parallel"), name="outproj")(ya, yb, yc, yd, h2d, wo, g, rw, rb)


def _dispatch_plan(expert, lrank, cnt_tile, tm):
    T = expert.shape[0]
    nt = T // tm
    counts = jnp.sum(cnt_tile, axis=0)
    padded = (counts + RUN_ROWS + MOE_BLOCK - 1) // MOE_BLOCK * MOE_BLOCK
    pad_end = jnp.cumsum(padded)
    pad_start = pad_end - padded
    gstart = pad_start[None, :] + jnp.cumsum(cnt_tile, axis=0) - cnt_tile
    nchunk = (cnt_tile + RUN_ROWS - 1) // RUN_ROWS
    chunk_end = jnp.cumsum(nchunk, axis=1)
    lstart = (chunk_end - nchunk) * RUN_ROWS
    onehot = expert[:, :, None] == jnp.arange(N_EXPERTS, dtype=I32)[None, None, :]
    pick = lambda tab: jnp.sum(jnp.where(onehot, jnp.repeat(tab, tm, axis=0)[:, None, :], 0), axis=-1)
    lpos =(lrank + pick(lstart)).astype(I32)
    max_chunks = tm * TOPK_IN // RUN_ROWS + N_EXPERTS
    c = jnp.arange(max_chunks, dtype=I32)
    ce = jnp.minimum(jnp.sum(chunk_end[:, None, :] <= c[None, :, None], axis=-1), N_EXPERTS - 1)
    ce_hot = ce[:, :, None] == jnp.arange(N_EXPERTS, dtype=I32)[None, None, :]
    take = lambda tab: jnp.sum(jnp.where(ce_hot, tab[:, None, :], 0), axis=-1)
    chunk_row = take(gstart) + (c[None, :] - take(chunk_end - nchunk)) * RUN_ROWS
    chunk_row = jnp.where(c[None, :] < chunk_end[:, -1:], chunk_row, 0)
    table = jnp.concatenate([chunk_row.astype(I32), jnp.zeros((nt, LANES - 1 - max_chunks), I32),
                             chunk_end[:, -1:].astype(I32)], axis=1).reshape(nt, 1, LANES)
    n_blk = -(-(T * TOPK_IN + N_EXPERTS * RUN_ROWS) // MOE_BLOCK) + N_EXPERTS + 1
    blk_start = jnp.arange(n_blk, dtype=I32) * MOE_BLOCK
    blk_e = jnp.minimum(jnp.sum(pad_end[None, :] <= blk_start[:, None], axis=-1), N_EXPERTS - 1).astype(I32)
    n_used = (pad_end[-1] // MOE_BLOCK).astype(I32).reshape(1)
    zero_start = jnp.concatenate([pad_start + counts, pad_end[-1:]])
    zero_end = jnp.concatenate([pad_end, jnp.full((1,), n_blk * MOE_BLOCK, pad_end.dtype)])
    nz = N_EXPERTS + 1
    zfill = jnp.concatenate([zero_start, (zero_end - zero_start) // RUN_ROWS, zero_end - RUN_ROWS,
                             jnp.zeros((LANES - 3 * nz,), zero_start.dtype)]).astype(I32).reshape(1, LANES)
    return lpos, table, zfill, blk_e, n_used, n_blk


def _dispatch_body(zf_ref, tab_ref, lpos_ref, hn_ref, x_hbm, xs, zero, sem, zsem, n_prev, *, tm, nseg, nrow):
    i = pl.program_id(0)
    slot = i % 2
    step = RUN_ROWS * nseg

    @pl.when(i == 0)
    def _():
        zero[...] = jnp.zeros_like(zero)

        def zero_copy(r):
            return pltpu.make_async_copy(zero, x_hbm.at[pl.ds(pl.multiple_of(r * nseg, nseg), step), :], zsem)

        nz = N_EXPERTS + 1

        def for_chunks(fn):
            for e in range(nz):
                def body(k, carry, e=e):
                    fn(zero_copy(zf_ref[0, e] + k * RUN_ROWS))
                    return carry
                lax.fori_loop(0, zf_ref[0, nz + e], body, 0)

        for_chunks(lambda c: c.start())
        for_chunks(lambda c: c.wait())
        for e in range(nz):
            zero_copy(zf_ref[0, 2 * nz + e]).start()
        for e in range(nz):
            zero_copy(zf_ref[0, 2 * nz + e]).wait()

    row = lax.broadcasted_iota(I32, (nrow, tm), 0)
    sel = (row == lpos_ref[0:1, :]) | (row == lpos_ref[1:2, :])
    rows = jnp.dot(sel.astype(BF16), hn_ref[...], preferred_element_type=F32)
    for k in range(nseg):
        xs[slot, pl.ds(k, nrow, stride=nseg), :] = rows[:, k * LANES:(k + 1) * LANES]

    def chunk_copy(sl, c):
        return pltpu.make_async_copy(
            xs.at[sl, pl.ds(pl.multiple_of(c * step, step), step), :],
            x_hbm.at[pl.ds(pl.multiple_of(tab_ref[0, 0, c] * nseg, nseg), step), :], sem.at[sl])

    def wait_chunks(sl, n):
        def wait(c, carry):
            chunk_copy(sl, 0).wait()
            return carry
        lax.fori_loop(0, n, wait, 0)

    n_chunks = tab_ref[0, 0, LANES - 1]

    @pl.when(i > 0)
    def _():
        wait_chunks(1 - slot, n_prev[0])

    def start_pair(c2, carry):
        chunk_copy(slot, 2 * c2).start(priority=0)

        @pl.when(2 * c2 + 1 < n_chunks)
        def _():
            chunk_copy(slot, 2 * c2 + 1).start(priority=1)
        return carry

    lax.fori_loop(0, (n_chunks + 1) // 2, start_pair, 0)
    n_prev[0] = n_chunks

    @pl.when(i == pl.num_programs(0) - 1)
    def _():
        wait_chunks(slot, n_chunks)


def _dispatch(zfill, table, lpos_t, hn, n_pad, tm):
    T, D = hn.shape
    nseg = D // LANES
    nrow = tm * TOPK_IN + N_EXPERTS * RUN_ROWS
    return pl.pallas_call(
        functools.partial(_dispatch_body, tm=tm, nseg=nseg, nrow=nrow), grid=(T // tm,),
        in_specs=[pl.BlockSpec(memory_space=pltpu.SMEM),
                  pl.BlockSpec((1, 1, LANES), lambda i: (i, 0, 0), memory_space=pltpu.SMEM),
                  pl.BlockSpec((None, TOPK_IN, tm), lambda i: (i, 0, 0)),
                  pl.BlockSpec((tm, D), lambda i: (i, 0))],
        out_specs=pl.BlockSpec(memory_space=pl.ANY),
        out_shape=jax.ShapeDtypeStruct((n_pad * nseg, LANES), F32),
        scratch_shapes=[pltpu.VMEM((2, nrow * nseg, LANES), F32), pltpu.VMEM((RUN_ROWS * nseg, LANES), F32),
                        pltpu.SemaphoreType.DMA((2,)), pltpu.SemaphoreType.DMA, pltpu.SMEM((1,), I32)],
        compiler_params=_params("arbitrary"), name="dispatch")(zfill, table, lpos_t, hn)


def _moe_body(blk_e_ref, n_used_ref, seg_ref, nxt_ref, x_ref, w1_hbm, w3_hbm, w2_hbm, y_ref,
              wf1, wf3, wf2, w1b, w3b, w2b, sem, *, nseg, layer):
    b = pl.program_id(0)
    R = MOE_BLOCK

    def fetch(e, slot):
        return [pltpu.make_async_copy(src.at[layer, e], dst.at[slot], sem.at[slot, k])
                for k, (src, dst) in enumerate(((w1_hbm, wf1), (w3_hbm, wf3), (w2_hbm, wf2)))]

    @pl.when(b < n_used_ref[0])
    def _():
        @pl.when((b == 0) | (blk_e_ref[b] != blk_e_ref[jnp.maximum(b - 1, 0)]))
        def _():
            e = blk_e_ref[b]
            slot = seg_ref[b] % 2

            @pl.when(b == 0)
            def _():
                for c in fetch(e, slot):
                    c.start()

            for c in fetch(e, slot):
                c.wait()
            w1b[...] = wf1[slot].astype(BF16)
            w3b[...] = wf3[slot].astype(BF16)
            w2b[...] = wf2[slot].astype(BF16)

            @pl.when(nxt_ref[b] >= 0)
            def _():
                for c in fetch(nxt_ref[b], 1 - slot):
                    c.start()

        x = jnp.concatenate([x_ref[pl.ds(s, R, stride=nseg), :] for s in range(nseg)], axis=-1).astype(BF16)
        a = jnp.dot(x, w1b[...], preferred_element_type=F32)
        g = jnp.dot(x, w3b[...], preferred_element_type=F32)
        hid = (a * jax.nn.sigmoid(a) * g).astype(BF16)
        y = jnp.dot(hid, w2b[...], preferred_element_type=F32)
        for s in range(nseg):
            y_ref[pl.ds(s, R, stride=nseg), :] = y[:, s * LANES:(s + 1) * LANES]

    @pl.when(b >= n_used_ref[0])
    def _():
        y_ref[...] = jnp.zeros_like(y_ref)


def _moe(blk_e, n_used, x_rows, w1, w3, w2, layer):
    n_blk = blk_e.shape[0]
    _, _, D, DE = w1.shape
    nseg = D // LANES
    R = MOE_BLOCK
    blk = jnp.arange(n_blk, dtype=I32)
    used = blk < n_used[0]
    change = (blk > 0) & (blk_e != jnp.roll(blk_e, 1)) & used
    seg = jnp.cumsum(change.astype(I32)).astype(I32)
    later = (blk[None, :] > blk[:, None]) & (seg[None, :] > seg[:, None]) & used[None, :]
    nxt_blk = jnp.min(jnp.where(later, blk[None, :], n_blk), axis=1)
    nxt = jnp.where(nxt_blk < n_blk, blk_e[jnp.minimum(nxt_blk, n_blk - 1)], -1).astype(I32)
    rows = lambda b, be, nu, sg, nx: (jnp.minimum(b, nu[0] - 1), 0)
    grid_spec = pltpu.PrefetchScalarGridSpec(
        num_scalar_prefetch=4, grid=(n_blk,),
        in_specs=[pl.BlockSpec((R * nseg, LANES), rows),
                  pl.BlockSpec(memory_space=pl.ANY), pl.BlockSpec(memory_space=pl.ANY),
                  pl.BlockSpec(memory_space=pl.ANY)],
        out_specs=pl.BlockSpec((R * nseg, LANES), lambda b, be, nu, sg, nx: (b, 0)),
        scratch_shapes=[pltpu.VMEM((2, D, DE), F32), pltpu.VMEM((2, D, DE), F32), pltpu.VMEM((2, DE, D), F32),
                        pltpu.VMEM((D, DE), BF16), pltpu.VMEM((D, DE), BF16), pltpu.VMEM((DE, D), BF16),
                        pltpu.SemaphoreType.DMA((2, 3))])
    return pl.pallas_call(
        functools.partial(_moe_body, nseg=nseg, layer=layer), grid_spec=grid_spec,
        out_shape=jax.ShapeDtypeStruct(x_rows.shape, F32),
        compiler_params=_params("arbitrary"), name="moe")(blk_e, n_used, seg, nxt, x_rows, w1, w3, w2)


def _combine_body(tab_ref, tabn_ref, h_ref, rf_ref, lpos_ref, y_hbm, o_ref, ybuf, sem, *, tm, nseg, nrow):
    i = pl.program_id(0)
    slot = i % 2
    step = RUN_ROWS * nseg

    def start_chunks(tab, sl):
        n = tab[0, 0, LANES - 1]

        def copy(c):
            return pltpu.make_async_copy(
                y_hbm.at[pl.ds(pl.multiple_of(tab[0, 0, c] * nseg, nseg), step), :],
                ybuf.at[sl, pl.ds(pl.multiple_of(c * step, step), step), :], sem.at[sl])

        def start_pair(c2, carry):
            copy(2 * c2).start(priority=0)

            @pl.when(2 * c2 + 1 < n)
            def _():
                copy(2 * c2 + 1).start(priority=1)
            return carry
        lax.fori_loop(0, (n + 1) // 2, start_pair, 0)

    @pl.when(i == 0)
    def _():
        ybuf[...] = jnp.zeros_like(ybuf)
        start_chunks(tab_ref, 0)

    @pl.when(i + 1 < pl.num_programs(0))
    def _():
        start_chunks(tabn_ref, 1 - slot)

    col = lax.broadcasted_iota(I32, (tm, nrow), 1)
    sel = jnp.zeros((tm, nrow), F32)
    for s in range(TOPK_IN):
        sel = sel + jnp.where(col == lpos_ref[:, s:s + 1], rf_ref[:, s:s + 1], 0.0)

    def wait(c, carry):
        pltpu.make_async_copy(y_hbm.at[pl.ds(0, step), :], ybuf.at[slot, pl.ds(0, step), :],
                              sem.at[slot]).wait()
        return carry

    lax.fori_loop(0, tab_ref[0, 0, LANES - 1], wait, 0)
    y = jnp.concatenate([ybuf[slot, pl.ds(k, nrow, stride=nseg), :] for k in range(nseg)], axis=-1)
    o_ref[...] = h_ref[...] + jnp.dot(sel.astype(BF16), y.astype(BF16), preferred_element_type=F32)


def _combine(table, h2d, rf, lpos, y_rows, tm):
    T, D = h2d.shape
    nseg = D // LANES
    nt = T // tm
    nrow = tm * TOPK_IN + N_EXPERTS * RUN_ROWS
    return pl.pallas_call(
        functools.partial(_combine_body, tm=tm, nseg=nseg, nrow=nrow), grid=(nt,),
        in_specs=[pl.BlockSpec((1, 1, LANES), lambda i: (i, 0, 0), memory_space=pltpu.SMEM),
                  pl.BlockSpec((1, 1, LANES), lambda i: (jnp.minimum(i + 1, nt - 1), 0, 0),
                               memory_space=pltpu.SMEM),
                  pl.BlockSpec((tm, D), lambda i: (i, 0)),
                  pl.BlockSpec((tm, LANES), lambda i: (i, 0)),
                  pl.BlockSpec((tm, TOPK_IN), lambda i: (i, 0)),
                  pl.BlockSpec(memory_space=pl.ANY)],
        out_specs=pl.BlockSpec((tm, D), lambda i: (i, 0)),
        out_shape=jax.ShapeDtypeStruct((T, D), F32),
        scratch_shapes=[pltpu.VMEM((2, nrow * nseg, LANES), F32), pltpu.SemaphoreType.DMA((2,))],
        compiler_params=_params("arbitrary"), name="combine")(table, table, h2d, rf, lpos, y_rows)


def _pad_lanes(a, lane0, rows=1):
    a = a.reshape(rows, -1)
    return jnp.pad(a, ((0, 0), (lane0, LANES - lane0 - a.shape[-1])))


def _layer(h2d, B, S, p):
    T, D = h2d.shape
    tile = lambda a, n: jnp.tile(a.reshape(1, -1), (1, n))

    offs = np.cumsum([0, GROUP_W, GROUP_W, GROUP_W, N_HEADS, GROUP_W, GROUP_W,
                      GROUP_W, GROUP_W, GROUP_W, N_HEADS, N_HEADS, GROUP_W, GROUP_W])
    seg = lambda k: p['w_in'][:, offs[k]:offs[k + 1]]
    w_all = jnp.concatenate([seg(0), seg(1), seg(2), seg(4), seg(5), seg(6), seg(7), seg(8), seg(11), seg(12),
                             seg(3), seg(9), seg(10), jnp.zeros((D, LANES - 3 * N_HEADS), F32)],
                            axis=1).astype(BF16)
    gate_prm = jnp.concatenate([_pad_lanes(p['fox_f_bias'], LANE_FOX), _pad_lanes(p['gdn_dt_bias'], LANE_DECAY),
                                _pad_lanes(p['gdn_a_log'], LANE_DECAY), jnp.zeros((SUBLANES - 3, LANES), F32)], axis=0)

    big, small = _inproj(h2d, p['attn_norm_g'].reshape(1, D), w_all, tm=min(1024, T))
    big3 = big.reshape(B, S, N_BIG_COLS * GROUP_W)
    ts = min(1024, S)
    gcol, gx = _gates(small.reshape(B, S, LANES), gate_prm, ts)

    tq = min(256, S)
    ya = _fox(big3, gcol, tile(p['fox_qn_g'], N_HEADS), tile(p['fox_kn_g'], N_HEADS),
              p['fox_out_g'].reshape(1, GROUP_W), tq)

    bst = jnp.repeat(p['gmlp_bs'].T, HEAD_DIM, axis=1)
    yb = _gmlp(big, p['gmlp_ln_g'].reshape(1, -1), p['gmlp_ln_b'].reshape(1, -1), p['gmlp_ws'], bst,
               p['gmlp_out_g'].reshape(1, -1), tm=min(1024, T))

    conv_w3 = p['gdn_conv_w'].reshape(CONV_K, 3, GROUP_W).transpose(1, 0, 2)
    nb = 4 if B % 4 == 0 else (2 if B % 2 == 0 else 1)
    yc = _gdn(big3, conv_w3, gx, tile(p['gdn_norm_g'], N_HEADS), min(1024 // nb, S), nb)

    wbd = jax.scipy.linalg.block_diag(*[p['pool_w'][g] for g in range(len(POOL_WINDOWS))]).astype(BF16)
    yd = _pool(big3, wbd, p['pool_scale'].reshape(1, -1), p['pool_out_g'].reshape(1, -1), ts)

    rw = _pad_lanes(jnp.concatenate([p['router_g_w'], p['router_e_w']], axis=1), 0, rows=D)
    rw = jnp.concatenate(_split(rw), axis=1)
    rb = _pad_lanes(jnp.concatenate([p['router_g_b'], p['router_e_b']]), 0)
    flat = lambda a: a.reshape(T, GROUP_W)
    tmd = min(512, T)
    h_new, hn_rows, ri, rf, cnt = _outproj(flat(ya), yb, flat(yc), flat(yd), h2d, p['w_out'].astype(BF16),
                                           p['ffn_norm_g'].reshape(1, D), rw, rb, tm=min(1024, T), tr=tmd)
    cnt_tile = cnt.reshape(T // tmd, SUBLANES, LANES)[:, 0, :N_EXPERTS]
    lpos, table, zfill, blk_e, n_used, n_blk = _dispatch_plan(
        ri[:, :TOPK_IN], ri[:, TOPK_IN:2 * TOPK_IN], cnt_tile, tmd)
    lpos_t = lpos.reshape(T // tmd, tmd, TOPK_IN).transpose(0, 2, 1)
    x_rows = _dispatch(zfill, table, lpos_t, hn_rows, n_blk * MOE_BLOCK, tmd)
    y_rows = _moe(blk_e, n_used, x_rows, p['moe_w1'], p['moe_w3'], p['moe_w2'], p['layer'])
    return _combine(table, h_new, rf, lpos, y_rows, tmd)


def kernel(x, attn_norm_g, w_in, w_out, fox_f_bias, fox_qn_g, fox_kn_g, fox_out_g, gmlp_ln_g, gmlp_ln_b, gmlp_ws, gmlp_bs, gmlp_out_g, gdn_conv_w, gdn_a_log, gdn_dt_bias, gdn_norm_g, pool_w, pool_scale, pool_out_g, ffn_norm_g, router_g_w, router_g_b, router_e_w, router_e_b, moe_w1, moe_w3, moe_w2):
    B, S, D = x.shape
    names = ('attn_norm_g', 'w_in', 'w_out', 'fox_f_bias', 'fox_qn_g', 'fox_kn_g', 'fox_out_g', 'gmlp_ln_g',
             'gmlp_ln_b', 'gmlp_ws', 'gmlp_bs', 'gmlp_out_g', 'gdn_conv_w', 'gdn_a_log', 'gdn_dt_bias',
             'gdn_norm_g', 'pool_w', 'pool_scale', 'pool_out_g', 'ffn_norm_g', 'router_g_w', 'router_g_b',
             'router_e_w', 'router_e_b', 'moe_w1', 'moe_w3', 'moe_w2')
    vals = (attn_norm_g, w_in, w_out, fox_f_bias, fox_qn_g, fox_kn_g, fox_out_g, gmlp_ln_g, gmlp_ln_b, gmlp_ws,
            gmlp_bs, gmlp_out_g, gdn_conv_w, gdn_a_log, gdn_dt_bias, gdn_norm_g, pool_w, pool_scale, pool_out_g,
            ffn_norm_g, router_g_w, router_g_b, router_e_w, router_e_b, moe_w1, moe_w3, moe_w2)
    h = x.reshape(B * S, D)
    stacked = ('moe_w1', 'moe_w3', 'moe_w2')
    for l in range(w_in.shape[0]):
        p = {n: (v if n in stacked else v[l]) for n, v in zip(names, vals)}
        p['layer'] = l
        h = _layer(h, B, S, p)
    return h.reshape(B, S, D)
```
